```python
import math
import jax
import jax.numpy as jnp
from jax import lax
import numpy as np

D_MODEL = 1024
BATCH = 8
SEQ = 4096
DEPTH = 2

HEAD_DIM = 64
GROUP_HEADS = 4
GROUP_WIDTH = GROUP_HEADS * HEAD_DIM
N_GROUPS = 4
D_MIX = N_GROUPS * GROUP_WIDTH
NSA_HEADS = GROUP_HEADS
GDN_HEADS = GROUP_HEADS
MLSTM_HEADS = GROUP_HEADS
CMP_STRIDE = 16
CMP_BLOCK = 2 * CMP_STRIDE
CMP_HIDDEN = 256
SLC_BLOCK = 64
N_SLC = 16
N_LOCAL_SLC = 2
WINDOW = 512
Q_BLOCK = 128
FORCE = 1e6
N_BUCKETS = 32
MAX_DISTANCE = 128
SC_WIDTH = 3
GDN_CONV = 4
GDN_CHUNK = 64
MLSTM_CHUNK = 64
D_FF = 2816
N_SUBLAYERS = 3
EPS = 1e-6

IN_LAYOUT = (
    ("a_q", GROUP_WIDTH), ("a_k_cmp", HEAD_DIM), ("a_v_cmp", HEAD_DIM),
    ("a_k_slc", HEAD_DIM), ("a_v_slc", HEAD_DIM), ("a_k_win", HEAD_DIM), ("a_v_win", HEAD_DIM),
    ("a_gate", 3 * NSA_HEADS),
    ("b_b", GROUP_WIDTH), ("b_c", GROUP_WIDTH), ("b_x", GROUP_WIDTH),
    ("c_q", GROUP_WIDTH), ("c_k", GROUP_WIDTH), ("c_v", GROUP_WIDTH),
    ("c_beta", GDN_HEADS), ("c_alpha", GDN_HEADS), ("c_z", GROUP_WIDTH),
    ("d_q", GROUP_WIDTH), ("d_k", GROUP_WIDTH), ("d_v", GROUP_WIDTH),
    ("d_i", MLSTM_HEADS), ("d_f", MLSTM_HEADS), ("d_o", GROUP_WIDTH),
)
D_IN = sum(s for _, s in IN_LAYOUT)

kernel_name = "hymba_style_nsa_conv_gdn_mlstm_macaron"


def rms_norm(x, g):
    xf = x.astype(jnp.float32)
    y = xf * lax.rsqrt(jnp.mean(xf * xf, axis=-1, keepdims=True) + EPS)
    return (y * g).astype(x.dtype)


def l2_norm(x):
    xf = x.astype(jnp.float32)
    return (xf * lax.rsqrt(jnp.sum(xf * xf, axis=-1, keepdims=True) + EPS)).astype(x.dtype)


def adaln_pre(h, g, shift, scale):
    return rms_norm(h, g) * (1 + scale) + shift


def swiglu(h, w13, w2):
    a, b = jnp.split(h @ w13, 2, axis=-1)
    return (jax.nn.silu(a) * b) @ w2


def causal_dwconv(x, w):
    K, C = w.shape
    return lax.conv_general_dilated(
        x, w[:, None, :], window_strides=(1,), padding=[(K - 1, 0)],
        dimension_numbers=("NWC", "WIO", "NWC"), feature_group_count=C)


def masked_softmax(logits, mask):
    l = jnp.where(mask, logits.astype(jnp.float32), -1e30)
    m = jnp.max(l, axis=-1, keepdims=True)
    e = jnp.where(mask, jnp.exp(l - m), 0.0)
    return e / jnp.maximum(jnp.sum(e, axis=-1, keepdims=True), 1e-30)


def t5_bucket(dist):
    n = jnp.maximum(dist, 0)
    max_exact = N_BUCKETS // 2
    nf = jnp.maximum(n, 1).astype(jnp.float32)
    large = max_exact + (jnp.log(nf / max_exact) / math.log(MAX_DISTANCE / max_exact)
                         * (N_BUCKETS - max_exact)).astype(jnp.int32)
    large = jnp.minimum(large, N_BUCKETS - 1)
    return jnp.where(n < max_exact, n, large)


def nsa_mixer(q, k_cmp, v_cmp, k_slc, v_slc, k_win, v_win, gate_raw, q_norm_g, k_norm_g,
              cmp_pos, cmp_k_w1, cmp_k_w2, cmp_v_w1, cmp_v_w2, t5_table):
    B, T = q.shape[0], q.shape[1]
    dt = q.dtype
    H, dh = NSA_HEADS, HEAD_DIM
    scale = dh ** -0.5
    q = rms_norm(q.reshape(B, T, H, dh), q_norm_g) * scale
    k_slc = rms_norm(k_slc, k_norm_g)
    k_win = rms_norm(k_win, k_norm_g)
    t = jnp.arange(T)

    nc = T // CMP_STRIDE - 1

    def compress(kv, w1, w2):
        ch = kv.reshape(B, T // CMP_STRIDE, CMP_STRIDE, dh)
        blocks = jnp.concatenate([ch[:, :-1], ch[:, 1:]], axis=2) + cmp_pos
        hid = jax.nn.silu(blocks.reshape(B, nc, CMP_BLOCK * dh) @ w1)
        return hid @ w2

    kc = rms_norm(compress(k_cmp, cmp_k_w1, cmp_k_w2), k_norm_g)
    vc = compress(v_cmp, cmp_v_w1, cmp_v_w2)
    cmp_end = jnp.arange(nc) * CMP_STRIDE + CMP_BLOCK - 1
    cmp_dist = t[:, None] - cmp_end[None, :]
    cmp_bias = t5_table[t5_bucket(cmp_dist)].transpose(2, 0, 1)
    cmp_logits = jnp.einsum("bthd,bnd->bhtn", q, kc) + cmp_bias
    p_cmp = masked_softmax(cmp_logits, cmp_dist >= 0)
    o_cmp = jnp.einsum("bhtn,bnd->bthd", p_cmp.astype(dt), vc)

    nblk = T // SLC_BLOCK
    n_sel = min(N_SLC, nblk)
    ci = np.arange(nc)[:, None]
    bj = np.arange(nblk)[None, :]
    overlap = ((ci * CMP_STRIDE < (bj + 1) * SLC_BLOCK)
               & (ci * CMP_STRIDE + CMP_BLOCK > bj * SLC_BLOCK)).astype(np.float32)
    score = jnp.sum(p_cmp, axis=1) @ jnp.asarray(overlap)
    cur = (t // SLC_BLOCK)[:, None]
    blk = jnp.arange(nblk)[None, :]
    forced = (blk == 0) | ((cur - blk >= 0) & (cur - blk < N_LOCAL_SLC))
    score = jnp.where(forced, FORCE, score)
    score = jnp.where(blk <= cur, score, -FORCE)
    _, sel_idx = lax.top_k(score, n_sel)

    ks_blocks = k_slc.reshape(B, nblk, SLC_BLOCK, dh)
    vs_blocks = v_slc.reshape(B, nblk, SLC_BLOCK, dh)
    kw_p = jnp.pad(k_win, ((0, 0), (WINDOW, 0), (0, 0)))
    vw_p = jnp.pad(v_win, ((0, 0), (WINDOW, 0), (0, 0)))
    nq = T // Q_BLOCK
    J = n_sel * SLC_BLOCK

    def qblock(i):
        s = i * Q_BLOCK
        qb = lax.dynamic_slice_in_dim(q, s, Q_BLOCK, axis=1)
        tq = s + jnp.arange(Q_BLOCK)
        kwb = lax.dynamic_slice_in_dim(kw_p, s, WINDOW + Q_BLOCK, axis=1)
        vwb = lax.dynamic_slice_in_dim(vw_p, s, WINDOW + Q_BLOCK, axis=1)
        kpos = s - WINDOW + jnp.arange(WINDOW + Q_BLOCK)
        wd = tq[:, None] - kpos[None, :]
        wmask = (wd >= 0) & (wd < WINDOW) & (kpos[None, :] >= 0)
        wl = jnp.einsum("bqhd,bkd->bhqk", qb, kwb) + t5_table[t5_bucket(wd)].transpose(2, 0, 1)
        o_w = jnp.einsum("bhqk,bkd->bqhd", masked_softmax(wl, wmask).astype(dt), vwb)
        idx = lax.dynamic_slice_in_dim(sel_idx, s, Q_BLOCK, axis=1)
        ksel = jax.vmap(lambda kb, ib: kb[ib])(ks_blocks, idx)
        vsel = jax.vmap(lambda vb, ib: vb[ib])(vs_blocks, idx)
        spos = idx[..., None] * SLC_BLOCK + jnp.arange(SLC_BLOCK)
        sd = (tq[None, :, None, None] - spos).reshape(B, Q_BLOCK, J)
        sl = jnp.einsum("bqhd,bqnsd->bhqns", qb, ksel).reshape(B, H, Q_BLOCK, J)
        sl = sl + t5_table[t5_bucket(sd)].transpose(0, 3, 1, 2)
        ps = masked_softmax(sl, (sd >= 0)[:, None]).astype(dt)
        o_s = jnp.einsum("bhqj,bqjd->bqhd", ps, vsel.reshape(B, Q_BLOCK, J, dh))
        return o_w, o_s

    o_win, o_slc = lax.map(qblock, jnp.arange(nq))
    o_win = o_win.transpose(1, 0, 2, 3, 4).reshape(B, T, H, dh)
    o_slc = o_slc.transpose(1, 0, 2, 3, 4).reshape(B, T, H, dh)

    g = jax.nn.sigmoid(gate_raw.reshape(B, T, H, 3))
    o = g[..., 0:1] * o_cmp + g[..., 1:2] * o_slc + g[..., 2:3] * o_win
    return o.reshape(B, T, GROUP_WIDTH)


def short_conv_mixer(b_gate, c_gate, xs, conv_w):
    return b_gate * causal_dwconv(c_gate * xs, conv_w)


def to_chunks(a, L):
    B, T, H = a.shape[:3]
    a = a.reshape((B, T // L, L, H) + a.shape[3:])
    return a.transpose((1, 0, 3, 2) + tuple(range(4, a.ndim)))


def from_chunks(a):
    N, B, H, L, d = a.shape
    return a.transpose(1, 0, 3, 2, 4).reshape(B, N * L, H, d)


def chunk_gated_delta_rule(q, k, v, g, beta):
    f32 = jnp.float32
    L = GDN_CHUNK
    q, k, v = (to_chunks(a.astype(f32), L) for a in (q, k, v))
    g = to_chunks(g.astype(f32), L)
    beta = to_chunks(beta.astype(f32), L)
    gc = jnp.cumsum(g, axis=-1)
    incl = jnp.tril(jnp.ones((L, L), bool))
    strict = jnp.tril(jnp.ones((L, L), bool), -1)
    seg = jnp.exp(jnp.where(incl, gc[..., :, None] - gc[..., None, :], -jnp.inf))
    kb = k * beta[..., None]
    a_mat = jnp.where(strict, jnp.einsum("nbhid,nbhjd->nbhij", kb, k) * seg, 0.0) + jnp.eye(L, dtype=f32)
    rhs = jnp.concatenate([v * beta[..., None], kb * jnp.exp(gc)[..., None]], axis=-1)
    sol = lax.linalg.triangular_solve(a_mat, rhs, left_side=True, lower=True, unit_diagonal=True)
    u, w = sol[..., :HEAD_DIM], sol[..., HEAD_DIM:]
    qk = jnp.einsum("nbhid,nbhjd->nbhij", q, k) * seg
    q_dec = q * jnp.exp(gc)[..., None]
    k_dec = k * jnp.exp(gc[..., -1:] - gc)[..., None]
    g_tot = jnp.exp(gc[..., -1])

    def step(S, inp):
        u_c, w_c, qd_c, kd_c, qk_c, gt_c = inp
        v_new = u_c - jnp.einsum("bhlk,bhkv->bhlv", w_c, S)
        o = jnp.einsum("bhlk,bhkv->bhlv", qd_c, S) + jnp.einsum("bhij,bhjv->bhiv", qk_c, v_new)
        S = S * gt_c[..., None, None] + jnp.einsum("bhlk,bhlv->bhkv", kd_c, v_new)
        return S, o

    S0 = jnp.zeros(u.shape[1:3] + (HEAD_DIM, HEAD_DIM), f32)
    _, o = lax.scan(step, S0, (u, w, q_dec, k_dec, qk, g_tot))
    return from_chunks(o)


def gated_deltanet(q, k, v, beta_raw, alpha_raw, z, conv_w, A_log, dt_bias, norm_g):
    B, T = q.shape[0], q.shape[1]
    qkv = jax.nn.silu(causal_dwconv(jnp.concatenate([q, k, v], axis=-1), conv_w))
    q, k, v = (a.reshape(B, T, GDN_HEADS, HEAD_DIM) for a in jnp.split(qkv, 3, axis=-1))
    q = l2_norm(q) * HEAD_DIM ** -0.5
    k = l2_norm(k)
    beta = jax.nn.sigmoid(beta_raw)
    g = -jnp.exp(A_log) * jax.nn.softplus(alpha_raw + dt_bias)
    o = chunk_gated_delta_rule(q, k, v, g, beta).astype(z.dtype)
    o = rms_norm(o, norm_g) * jax.nn.silu(z.reshape(B, T, GDN_HEADS, HEAD_DIM))
    return o.reshape(B, T, GROUP_WIDTH)


def mlstm_chunkwise(q, k, v, i_pre, f_pre):
    f32 = jnp.float32
    L = MLSTM_CHUNK
    q, k, v = (to_chunks(a.astype(f32), L) for a in (q, k, v))
    log_i = to_chunks(i_pre.astype(f32), L)
    log_f = to_chunks(jax.nn.log_sigmoid(f_pre.astype(f32)), L)
    b = jnp.cumsum(log_f, axis=-1)
    incl = jnp.tril(jnp.ones((L, L), bool))
    log_w = jnp.where(incl, b[..., :, None] - b[..., None, :] + log_i[..., None, :], -jnp.inf)
    m_intra = jnp.max(log_w, axis=-1)
    log_w_end = b[..., -1:] - b + log_i
    m_end = jnp.max(log_w_end, axis=-1)
    qk = jnp.einsum("nbhid,nbhjd->nbhij", q, k)

    def step(carry, inp):
        C, n, m = carry
        q_c, k_c, v_c, b_c, lw_c, mi_c, lwe_c, me_c, qk_c = inp
        log_inter = b_c + m[..., None]
        m_t = jnp.maximum(log_inter, mi_c)
        w_inter = jnp.exp(log_inter - m_t)
        s = qk_c * jnp.exp(lw_c - m_t[..., None])
        num = (w_inter[..., None] * jnp.einsum("bhlk,bhkv->bhlv", q_c, C)
               + jnp.einsum("bhij,bhjv->bhiv", s, v_c))
        den = w_inter * jnp.einsum("bhlk,bhk->bhl", q_c, n) + jnp.sum(s, axis=-1)
        h = num / jnp.maximum(jnp.abs(den), jnp.exp(-m_t))[..., None]
        m_new = jnp.maximum(b_c[..., -1] + m, me_c)
        w_old = jnp.exp(b_c[..., -1] + m - m_new)
        w_new = jnp.exp(lwe_c - m_new[..., None])
        C = w_old[..., None, None] * C + jnp.einsum("bhl,bhlk,bhlv->bhkv", w_new, k_c, v_c)
        n = w_old[..., None] * n + jnp.einsum("bhl,bhlk->bhk", w_new, k_c)
        return (C, n, m_new), h

    Bsz, H = q.shape[1], q.shape[2]
    carry0 = (jnp.zeros((Bsz, H, HEAD_DIM, HEAD_DIM), f32), jnp.zeros((Bsz, H, HEAD_DIM), f32),
              jnp.zeros((Bsz, H), f32))
    _, h = lax.scan(step, carry0, (q, k, v, b, log_w, m_intra, log_w_end, m_end, qk))
    return from_chunks(h)


def mlstm_mixer(q, k, v, i_raw, f_raw, o_raw, f_bias, norm_g):
    B, T = q.shape[0], q.shape[1]
    heads = lambda a: a.reshape(B, T, MLSTM_HEADS, HEAD_DIM)
    h = mlstm_chunkwise(heads(q), heads(k) * HEAD_DIM ** -0.5, heads(v), i_raw, f_raw + f_bias)
    h = rms_norm(h.astype(q.dtype), norm_g) * jax.nn.sigmoid(heads(o_raw))
    return h.reshape(B, T, GROUP_WIDTH)


def setup_inputs(seed: int = 0) -> dict:
    key = jax.random.key(seed)
    ks = jax.random.split(key, 32)
    L, D, F = DEPTH, D_MODEL, D_FF
    nrm = lambda k, shape, s: jax.random.normal(k, shape, jnp.float32) * s
    dt = jnp.exp(jax.random.uniform(ks[22], (L, GDN_HEADS), jnp.float32, math.log(1e-3), math.log(1e-1)))
    return {
        "x": nrm(ks[0], (BATCH, SEQ, D), 1.0),
        "c": nrm(ks[1], (BATCH, D), 1.0),
        "ada_w": nrm(ks[2], (L, D, 3 * N_SUBLAYERS * D), 0.5 * D ** -0.5),
        "ada_b": nrm(ks[3], (L, 3 * N_SUBLAYERS * D), 0.02),
        "norm_g": 1.0 + nrm(ks[4], (L, N_SUBLAYERS, D), 0.02),
        "ffn1_w13": nrm(ks[5], (L, D, 2 * F), D ** -0.5),
        "ffn1_w2": nrm(ks[6], (L, F, D), F ** -0.5),
        "ffn2_w13": nrm(ks[7], (L, D, 2 * F), D ** -0.5),
        "ffn2_w2": nrm(ks[8], (L, F, D), F ** -0.5),
        "w_in": nrm(ks[9], (L, D, D_IN), D ** -0.5),
        "b_in": nrm(ks[10], (L, D_IN), 0.02),
        "q_norm_g": 1.0 + nrm(ks[11], (L, HEAD_DIM), 0.02),
        "k_norm_g": 1.0 + nrm(ks[12], (L, HEAD_DIM), 0.02),
        "cmp_pos": nrm(ks[13], (L, CMP_BLOCK, HEAD_DIM), 0.1),
        "cmp_k_w1": nrm(ks[14], (L, CMP_BLOCK * HEAD_DIM, CMP_HIDDEN), (CMP_BLOCK * HEAD_DIM) ** -0.5),
        "cmp_k_w2": nrm(ks[15], (L, CMP_HIDDEN, HEAD_DIM), CMP_HIDDEN ** -0.5),
        "cmp_v_w1": nrm(ks[16], (L, CMP_BLOCK * HEAD_DIM, CMP_HIDDEN), (CMP_BLOCK * HEAD_DIM) ** -0.5),
        "cmp_v_w2": nrm(ks[17], (L, CMP_HIDDEN, HEAD_DIM), CMP_HIDDEN ** -0.5),
        "t5_table": nrm(ks[18], (N_BUCKETS, NSA_HEADS), 0.5),
        "sc_conv_w": nrm(ks[19], (L, SC_WIDTH, GROUP_WIDTH), SC_WIDTH ** -0.5),
        "gdn_conv_w": nrm(ks[20], (L, GDN_CONV, 3 * GROUP_WIDTH), GDN_CONV ** -0.5),
        "gdn_A_log": jnp.log(jax.random.uniform(ks[21], (L, GDN_HEADS), jnp.float32, 1.0, 16.0)),
        "gdn_dt_bias": dt + jnp.log(-jnp.expm1(-dt)),
        "gdn_norm_g": 1.0 + nrm(ks[23], (L, HEAD_DIM), 0.02),
        "mlstm_f_bias": 3.0 + jax.random.uniform(ks[24], (L, MLSTM_HEADS), jnp.float32, 0.0, 3.0),
        "mlstm_norm_g": 1.0 + nrm(ks[25], (L, HEAD_DIM), 0.02),
        "mix_norm_g": 1.0 + nrm(ks[26], (L, 2, GROUP_WIDTH), 0.02),
        "w_out": nrm(ks[27], (L, D_MIX, D), D_MIX ** -0.5),
    }


def reference(x, c, ada_w, ada_b, norm_g, ffn1_w13, ffn1_w2, ffn2_w13, ffn2_w2, w_in, b_in,
              q_norm_g, k_norm_g, cmp_pos, cmp_k_w1, cmp_k_w2, cmp_v_w1, cmp_v_w2, t5_table,
              sc_conv_w, gdn_conv_w, gdn_A_log, gdn_dt_bias, gdn_norm_g, mlstm_f_bias,
              mlstm_norm_g, mix_norm_g, w_out):
    B, T, D = x.shape
    names = [n for n, _ in IN_LAYOUT]
    splits = np.cumsum([s for _, s in IN_LAYOUT])[:-1].tolist()
    cond = jax.nn.silu(c)
    for l in range(DEPTH):
        mod = (cond @ ada_w[l] + ada_b[l]).reshape(B, N_SUBLAYERS, 3, 1, D)
        h = adaln_pre(x, norm_g[l, 0], mod[:, 0, 0], mod[:, 0, 1])
        x = x + 0.5 * mod[:, 0, 2] * swiglu(h, ffn1_w13[l], ffn1_w2[l])
        h = adaln_pre(x, norm_g[l, 1], mod[:, 1, 0], mod[:, 1, 1])
        p = dict(zip(names, jnp.split(h @ w_in[l] + b_in[l], splits, axis=-1)))
        y_a = nsa_mixer(p["a_q"], p["a_k_cmp"], p["a_v_cmp"], p["a_k_slc"], p["a_v_slc"],
                        p["a_k_win"], p["a_v_win"], p["a_gate"], q_norm_g[l], k_norm_g[l],
                        cmp_pos[l], cmp_k_w1[l], cmp_k_w2[l], cmp_v_w1[l], cmp_v_w2[l], t5_table)
        y_b = short_conv_mixer(p["b_b"], p["b_c"], p["b_x"], sc_conv_w[l])
        y_c = gated_deltanet(p["c_q"], p["c_k"], p["c_v"], p["c_beta"], p["c_alpha"], p["c_z"],
                             gdn_conv_w[l], gdn_A_log[l], gdn_dt_bias[l], gdn_norm_g[l])
        y_d = mlstm_mixer(p["d_q"], p["d_k"], p["d_v"], p["d_i"], p["d_f"], p["d_o"],
                          mlstm_f_bias[l], mlstm_norm_g[l])
        y = jnp.concatenate([rms_norm(y_a, mix_norm_g[l, 0]), rms_norm(y_b, mix_norm_g[l, 1]), y_c, y_d],
                            axis=-1)
        x = x + mod[:, 1, 2] * (y @ w_out[l])
        h = adaln_pre(x, norm_g[l, 2], mod[:, 2, 0], mod[:, 2, 1])
        x = x + 0.5 * mod[:, 2, 2] * swiglu(h, ffn2_w13[l], ffn2_w2[l])
    return x
```

```python
import functools
import math

import jax
import jax.numpy as jnp
import numpy as np
from jax import lax
from jax.experimental import pallas as pl
from jax.experimental.pallas import tpu as pltpu

D_MODEL = 1024
HEAD_DIM = 64
GROUP_HEADS = 4
GROUP_WIDTH = GROUP_HEADS * HEAD_DIM
CMP_STRIDE = 16
CMP_BLOCK = 32
SLC_BLOCK = 64
N_SLC = 16
N_LOCAL_SLC = 2
WINDOW = 512
Q_BLOCK = 128
FORCE = 1e6
N_BUCKETS = 32
MAX_DISTANCE = 128
GDN_CHUNK = 64
MLSTM_CHUNK = 64
D_FF = 2816
N_SUBLAYERS = 3
EPS = 1e-6

IN_LAYOUT = (
    ("a_q", 256), ("a_k_cmp", 64), ("a_v_cmp", 64),
    ("a_k_slc", 64), ("a_v_slc", 64), ("a_k_win", 64), ("a_v_win", 64),
    ("a_gate", 12),
    ("b_b", 256), ("b_c", 256), ("b_x", 256),
    ("c_q", 256), ("c_k", 256), ("c_v", 256),
    ("c_beta", 4), ("c_alpha", 4), ("c_z", 256),
    ("d_q", 256), ("d_k", 256), ("d_v", 256),
    ("d_i", 4), ("d_f", 4), ("d_o", 256),
)

VMEM_LIMIT_BYTES = 56 * 1024 * 1024
FFN_TOKEN_TILE = 512
FFN_CHUNK = 256
PROJ_TOKEN_TILE = 512


def _modulated_norm(x, g, scale, shift):
    y = x * lax.rsqrt(jnp.mean(x * x, axis=-1, keepdims=True) + EPS)
    return (y * g) * (1.0 + scale) + shift


def _ada_kernel(c_ref, w_ref, b_ref, o_ref):
    c = c_ref[...]
    cond = c * jax.nn.sigmoid(c)
    o_ref[...] = jnp.dot(cond.astype(jnp.bfloat16), w_ref[...],
                         preferred_element_type=jnp.float32) + b_ref[...]


def ada_modulation(c, w, b):
    B, D = c.shape
    N = w.shape[1]
    tn = 1152
    return pl.pallas_call(
        _ada_kernel,
        grid=(N // tn,),
        in_specs=[pl.BlockSpec((B, D), lambda j: (0, 0)),
                  pl.BlockSpec((D, tn), lambda j: (0, j)),
                  pl.BlockSpec((1, tn), lambda j: (0, j))],
        out_specs=pl.BlockSpec((B, tn), lambda j: (0, j)),
        out_shape=jax.ShapeDtypeStruct((B, N), jnp.float32),
        name="ada_modulation",
    )(c, w, b)


def _ffn_kernel(x_ref, g_ref, shift_ref, scale_ref, gate_ref, w1_ref, w3_ref, w2_ref, o_ref, act_ref):
    x = x_ref[...]
    h = _modulated_norm(x, g_ref[...], scale_ref[0], shift_ref[0]).astype(jnp.bfloat16)
    n_chunks = act_ref.shape[1] // FFN_CHUNK
    for ci in range(n_chunks):
        cs = slice(ci * FFN_CHUNK, (ci + 1) * FFN_CHUNK)
        a = jnp.dot(h, w1_ref[:, cs], preferred_element_type=jnp.float32)
        b = jnp.dot(h, w3_ref[:, cs], preferred_element_type=jnp.float32)
        act_ref[:, cs] = (a * jax.nn.sigmoid(a) * b).astype(jnp.bfloat16)
    y = jnp.dot(act_ref[...], w2_ref[...], preferred_element_type=jnp.float32)
    o_ref[...] = x + (0.5 * gate_ref[0]) * y


def ffn_half_step(x2, g, shift, scale, gate, w1, w3, w2, tokens_per_batch):
    M, D = x2.shape
    F = w2.shape[0]
    tm = FFN_TOKEN_TILE
    tiles_per_batch = tokens_per_batch // tm
    resident = dict(pipeline_mode=pl.Buffered(1))
    mod_spec = pl.BlockSpec((1, 1, D), lambda i: (i // tiles_per_batch, 0, 0))
    return pl.pallas_call(
        _ffn_kernel,
        grid=(M // tm,),
        in_specs=[pl.BlockSpec((tm, D), lambda i: (i, 0)),
                  pl.BlockSpec((1, D), lambda i: (0, 0)),
                  mod_spec, mod_spec, mod_spec,
                  pl.BlockSpec((D, F), lambda i: (0, 0), **resident),
                  pl.BlockSpec((D, F), lambda i: (0, 0), **resident),
                  pl.BlockSpec((F, D), lambda i: (0, 0), **resident)],
        out_specs=pl.BlockSpec((tm, D), lambda i: (i, 0)),
        out_shape=jax.ShapeDtypeStruct((M, D), jnp.float32),
        scratch_shapes=[pltpu.VMEM((tm, F), jnp.bfloat16)],
        compiler_params=pltpu.CompilerParams(dimension_semantics=("arbitrary",),
                                             vmem_limit_bytes=VMEM_LIMIT_BYTES),
        name="ffn_half_step",
    )(x2, g, shift, scale, gate, w1, w3, w2)


def _in_proj_kernel(x_ref, g_ref, shift_ref, scale_ref, w_ref, b_ref, o_ref):
    h = _modulated_norm(x_ref[...], g_ref[...], scale_ref[0], shift_ref[0]).astype(jnp.bfloat16)
    o_ref[...] = jnp.dot(h, w_ref[...], preferred_element_type=jnp.float32) + b_ref[...]


def in_projection(x2, g, shift, scale, w, b, tokens_per_batch):
    M, D = x2.shape
    N = w.shape[1]
    tm = PROJ_TOKEN_TILE
    tiles_per_batch = tokens_per_batch // tm
    mod_spec = pl.BlockSpec((1, 1, D), lambda i: (i // tiles_per_batch, 0, 0))
    return pl.pallas_call(
        _in_proj_kernel,
        grid=(M // tm,),
        in_specs=[pl.BlockSpec((tm, D), lambda i: (i, 0)),
                  pl.BlockSpec((1, D), lambda i: (0, 0)),
                  mod_spec, mod_spec,
                  pl.BlockSpec((D, N), lambda i: (0, 0), pipeline_mode=pl.Buffered(1)),
                  pl.BlockSpec((1, N), lambda i: (0, 0))],
        out_specs=pl.BlockSpec((tm, N), lambda i: (i, 0)),
        out_shape=jax.ShapeDtypeStruct((M, N), jnp.float32),
        compiler_params=pltpu.CompilerParams(dimension_semantics=("arbitrary",),
                                             vmem_limit_bytes=VMEM_LIMIT_BYTES),
        name="in_projection",
    )(x2, g, shift, scale, w, b)


def _out_proj_kernel(x_ref, y_ref, gate_ref, w_ref, o_ref):
    z = jnp.dot(y_ref[...].astype(jnp.bfloat16), w_ref[...], preferred_element_type=jnp.float32)
    o_ref[...] = x_ref[...] + gate_ref[0] * z


def out_projection(x2, y2, gate, w, tokens_per_batch):
    M, D = x2.shape
    K = y2.shape[1]
    tm = PROJ_TOKEN_TILE
    tiles_per_batch = tokens_per_batch // tm
    return pl.pallas_call(
        _out_proj_kernel,
        grid=(M // tm,),
        in_specs=[pl.BlockSpec((tm, D), lambda i: (i, 0)),
                  pl.BlockSpec((tm, K), lambda i: (i, 0)),
                  pl.BlockSpec((1, 1, D), lambda i: (i // tiles_per_batch, 0, 0)),
                  pl.BlockSpec((K, D), lambda i: (0, 0), pipeline_mode=pl.Buffered(1))],
        out_specs=pl.BlockSpec((tm, D), lambda i: (i, 0)),
        out_shape=jax.ShapeDtypeStruct((M, D), jnp.float32),
        compiler_params=pltpu.CompilerParams(dimension_semantics=("arbitrary",),
                                             vmem_limit_bytes=VMEM_LIMIT_BYTES),
        name="out_projection",
    )(x2, y2, gate, w)


def rms_norm(x, g):
    xf = x.astype(jnp.float32)
    y = xf * lax.rsqrt(jnp.mean(xf * xf, axis=-1, keepdims=True) + EPS)
    return (y * g).astype(x.dtype)


def l2_norm(x):
    xf = x.astype(jnp.float32)
    return (xf * lax.rsqrt(jnp.sum(xf * xf, axis=-1, keepdims=True) + EPS)).astype(x.dtype)


def causal_dwconv(x, w):
    K, C = w.shape
    return lax.conv_general_dilated(
        x, w[:, None, :], window_strides=(1,), padding=[(K - 1, 0)],
        dimension_numbers=("NWC", "WIO", "NWC"), feature_group_count=C)


def masked_softmax(logits, mask):
    l = jnp.where(mask, logits.astype(jnp.float32), -1e30)
    m = jnp.max(l, axis=-1, keepdims=True)
    e = jnp.where(mask, jnp.exp(l - m), 0.0)
    return e / jnp.maximum(jnp.sum(e, axis=-1, keepdims=True), 1e-30)


def t5_bucket(dist):
    n = jnp.maximum(dist, 0)
    max_exact = N_BUCKETS // 2
    nf = jnp.maximum(n, 1).astype(jnp.float32)
    large = max_exact + (jnp.log(nf / max_exact) / math.log(MAX_DISTANCE / max_exact)
                         * (N_BUCKETS - max_exact)).astype(jnp.int32)
    large = jnp.minimum(large, N_BUCKETS - 1)
    return jnp.where(n < max_exact, n, large)


def nsa_mixer(q, k_cmp, v_cmp, k_slc, v_slc, k_win, v_win, gate_raw, q_norm_g, k_norm_g,
              cmp_pos, cmp_k_w1, cmp_k_w2, cmp_v_w1, cmp_v_w2, t5_table):
    B, T = q.shape[0], q.shape[1]
    dt = q.dtype
    H, dh = GROUP_HEADS, HEAD_DIM
    scale = dh ** -0.5
    q = rms_norm(q.reshape(B, T, H, dh), q_norm_g) * scale
    k_slc = rms_norm(k_slc, k_norm_g)
    k_win = rms_norm(k_win, k_norm_g)
    t = jnp.arange(T)
    nc = T // CMP_STRIDE - 1

    def compress(kv, w1, w2):
        ch = kv.reshape(B, T // CMP_STRIDE, CMP_STRIDE, dh)
        blocks = jnp.concatenate([ch[:, :-1], ch[:, 1:]], axis=2) + cmp_pos
        hid = jax.nn.silu(blocks.reshape(B, nc, CMP_BLOCK * dh) @ w1)
        return hid @ w2

    kc = rms_norm(compress(k_cmp, cmp_k_w1, cmp_k_w2), k_norm_g)
    vc = compress(v_cmp, cmp_v_w1, cmp_v_w2)
    cmp_end = jnp.arange(nc) * CMP_STRIDE + CMP_BLOCK - 1
    cmp_dist = t[:, None] - cmp_end[None, :]
    cmp_bias = t5_table[t5_bucket(cmp_dist)].transpose(2, 0, 1)
    cmp_logits = jnp.einsum("bthd,bnd->bhtn", q, kc) + cmp_bias
    p_cmp = masked_softmax(cmp_logits, cmp_dist >= 0)
    o_cmp = jnp.einsum("bhtn,bnd->bthd", p_cmp.astype(dt), vc)

    nblk = T // SLC_BLOCK
    n_sel = min(N_SLC, nblk)
    ci = np.arange(nc)[:, None]
    bj = np.arange(nblk)[None, :]
    overlap = ((ci * CMP_STRIDE < (bj + 1) * SLC_BLOCK)
               & (ci * CMP_STRIDE + CMP_BLOCK > bj * SLC_BLOCK)).astype(np.float32)
    score = jnp.sum(p_cmp, axis=1) @ jnp.asarray(overlap)
    cur = (t // SLC_BLOCK)[:, None]
    blk = jnp.arange(nblk)[None, :]
    forced = (blk == 0) | ((cur - blk >= 0) & (cur - blk < N_LOCAL_SLC))
    score = jnp.where(forced, FORCE, score)
    score = jnp.where(blk <= cur, score, -FORCE)
    _, sel_idx = lax.top_k(score, n_sel)

    ks_blocks = k_slc.reshape(B, nblk, SLC_BLOCK, dh)
    vs_blocks = v_slc.reshape(B, nblk, SLC_BLOCK, dh)
    kw_p = jnp.pad(k_win, ((0, 0), (WINDOW, 0), (0, 0)))
    vw_p = jnp.pad(v_win, ((0, 0), (WINDOW, 0), (0, 0)))
    nq = T // Q_BLOCK
    J = n_sel * SLC_BLOCK

    def qblock(i):
        s = i * Q_BLOCK
        qb = lax.dynamic_slice_in_dim(q, s, Q_BLOCK, axis=1)
        tq = s + jnp.arange(Q_BLOCK)
        kwb = lax.dynamic_slice_in_dim(kw_p, s, WINDOW + Q_BLOCK, axis=1)
        vwb = lax.dynamic_slice_in_dim(vw_p, s, WINDOW + Q_BLOCK, axis=1)
        kpos = s - WINDOW + jnp.arange(WINDOW + Q_BLOCK)
        wd = tq[:, None] - kpos[None, :]
        wmask = (wd >= 0) & (wd < WINDOW) & (kpos[None, :] >= 0)
        wl = jnp.einsum("bqhd,bkd->bhqk", qb, kwb) + t5_table[t5_bucket(wd)].transpose(2, 0, 1)
        o_w = jnp.einsum("bhqk,bkd->bqhd", masked_softmax(wl, wmask).astype(dt), vwb)
        idx = lax.dynamic_slice_in_dim(sel_idx, s, Q_BLOCK, axis=1)
        ksel = jax.vmap(lambda kb, ib: kb[ib])(ks_blocks, idx)
        vsel = jax.vmap(lambda vb, ib: vb[ib])(vs_blocks, idx)
        spos = idx[..., None] * SLC_BLOCK + jnp.arange(SLC_BLOCK)
        sd = (tq[None, :, None, None] - spos).reshape(B, Q_BLOCK, J)
        sl = jnp.einsum("bqhd,bqnsd->bhqns", qb, ksel).reshape(B, H, Q_BLOCK, J)
        sl = sl + t5_table[t5_bucket(sd)].transpose(0, 3, 1, 2)
        ps = masked_softmax(sl, (sd >= 0)[:, None]).astype(dt)
        o_s = jnp.einsum("bhqj,bqjd->bqhd", ps, vsel.reshape(B, Q_BLOCK, J, dh))
        return o_w, o_s

    o_win, o_slc = lax.map(qblock, jnp.arange(nq))
    o_win = o_win.transpose(1, 0, 2, 3, 4).reshape(B, T, H, dh)
    o_slc = o_slc.transpose(1, 0, 2, 3, 4).reshape(B, T, H, dh)
    g = jax.nn.sigmoid(gate_raw.reshape(B, T, H, 3))
    o = g[..., 0:1] * o_cmp + g[..., 1:2] * o_slc + g[..., 2:3] * o_win
    return o.reshape(B, T, GROUP_WIDTH)


def short_conv_mixer(b_gate, c_gate, xs, conv_w):
    return b_gate * causal_dwconv(c_gate * xs, conv_w)


def to_chunks(a, L):
    B, T, H = a.shape[:3]
    a = a.reshape((B, T // L, L, H) + a.shape[3:])
    return a.transpose((1, 0, 3, 2) + tuple(range(4, a.ndim)))


def from_chunks(a):
    N, B, H, L, d = a.shape
    return a.transpose(1, 0, 3, 2, 4).reshape(B, N * L, H, d)


def chunk_gated_delta_rule(q, k, v, g, beta):
    f32 = jnp.float32
    L = GDN_CHUNK
    q, k, v = (to_chunks(a.astype(f32), L) for a in (q, k, v))
    g = to_chunks(g.astype(f32), L)
    beta = to_chunks(beta.astype(f32), L)
    gc = jnp.cumsum(g, axis=-1)
    incl = jnp.tril(jnp.ones((L, L), bool))
    strict = jnp.tril(jnp.ones((L, L), bool), -1)
    seg = jnp.exp(jnp.where(incl, gc[..., :, None] - gc[..., None, :], -jnp.inf))
    kb = k * beta[..., None]
    a_mat = jnp.where(strict, jnp.einsum("nbhid,nbhjd->nbhij", kb, k) * seg, 0.0) + jnp.eye(L, dtype=f32)
    rhs = jnp.concatenate([v * beta[..., None], kb * jnp.exp(gc)[..., None]], axis=-1)
    sol = lax.linalg.triangular_solve(a_mat, rhs, left_side=True, lower=True, unit_diagonal=True)
    u, w = sol[..., :HEAD_DIM], sol[..., HEAD_DIM:]
    qk = jnp.einsum("nbhid,nbhjd->nbhij", q, k) * seg
    q_dec = q * jnp.exp(gc)[..., None]
    k_dec = k * jnp.exp(gc[..., -1:] - gc)[..., None]
    g_tot = jnp.exp(gc[..., -1])

    def step(S, inp):
        u_c, w_c, qd_c, kd_c, qk_c, gt_c = inp
        v_new = u_c - jnp.einsum("bhlk,bhkv->bhlv", w_c, S)
        o = jnp.einsum("bhlk,bhkv->bhlv", qd_c, S) + jnp.einsum("bhij,bhjv->bhiv", qk_c, v_new)
        S = S * gt_c[..., None, None] + jnp.einsum("bhlk,bhlv->bhkv", kd_c, v_new)
        return S, o

    S0 = jnp.zeros(u.shape[1:3] + (HEAD_DIM, HEAD_DIM), f32)
    _, o = lax.scan(step, S0, (u, w, q_dec, k_dec, qk, g_tot))
    return from_chunks(o)


def gated_deltanet(q, k, v, beta_raw, alpha_raw, z, conv_w, A_log, dt_bias, norm_g):
    B, T = q.shape[0], q.shape[1]
    qkv = jax.nn.silu(causal_dwconv(jnp.concatenate([q, k, v], axis=-1), conv_w))
    q, k, v = (a.reshape(B, T, GROUP_HEADS, HEAD_DIM) for a in jnp.split(qkv, 3, axis=-1))
    q = l2_norm(q) * HEAD_DIM ** -0.5
    k = l2_norm(k)
    beta = jax.nn.sigmoid(beta_raw)
    g = -jnp.exp(A_log) * jax.nn.softplus(alpha_raw + dt_bias)
    o = chunk_gated_delta_rule(q, k, v, g, beta).astype(z.dtype)
    o = rms_norm(o, norm_g) * jax.nn.silu(z.reshape(B, T, GROUP_HEADS, HEAD_DIM))
    return o.reshape(B, T, GROUP_WIDTH)


def mlstm_chunkwise(q, k, v, i_pre, f_pre):
    f32 = jnp.float32
    L = MLSTM_CHUNK
    q, k, v = (to_chunks(a.astype(f32), L) for a in (q, k, v))
    log_i = to_chunks(i_pre.astype(f32), L)
    log_f = to_chunks(jax.nn.log_sigmoid(f_pre.astype(f32)), L)
    b = jnp.cumsum(log_f, axis=-1)
    incl = jnp.tril(jnp.ones((L, L), bool))
    log_w = jnp.where(incl, b[..., :, None] - b[..., None, :] + log_i[..., None, :], -jnp.inf)
    m_intra = jnp.max(log_w, axis=-1)
    log_w_end = b[..., -1:] - b + log_i
    m_end = jnp.max(log_w_end, axis=-1)
    qk = jnp.einsum("nbhid,nbhjd->nbhij", q, k)

    def step(carry, inp):
        C, n, m = carry
        q_c, k_c, v_c, b_c, lw_c, mi_c, lwe_c, me_c, qk_c = inp
        log_inter = b_c + m[..., None]
        m_t = jnp.maximum(log_inter, mi_c)
        w_inter = jnp.exp(log_inter - m_t)
        s = qk_c * jnp.exp(lw_c - m_t[..., None])
        num = (w_inter[..., None] * jnp.einsum("bhlk,bhkv->bhlv", q_c, C)
               + jnp.einsum("bhij,bhjv->bhiv", s, v_c))
        den = w_inter * jnp.einsum("bhlk,bhk->bhl", q_c, n) + jnp.sum(s, axis=-1)
        h = num / jnp.maximum(jnp.abs(den), jnp.exp(-m_t))[..., None]
        m_new = jnp.maximum(b_c[..., -1] + m, me_c)
        w_old = jnp.exp(b_c[..., -1] + m - m_new)
        w_new = jnp.exp(lwe_c - m_new[..., None])
        C = w_old[..., None, None] * C + jnp.einsum("bhl,bhlk,bhlv->bhkv", w_new, k_c, v_c)
        n = w_old[..., None] * n + jnp.einsum("bhl,bhlk->bhk", w_new, k_c)
        return (C, n, m_new), h

    Bsz, H = q.shape[1], q.shape[2]
    carry0 = (jnp.zeros((Bsz, H, HEAD_DIM, HEAD_DIM), f32), jnp.zeros((Bsz, H, HEAD_DIM), f32),
              jnp.zeros((Bsz, H), f32))
    _, h = lax.scan(step, carry0, (q, k, v, b, log_w, m_intra, log_w_end, m_end, qk))
    return from_chunks(h)


def mlstm_mixer(q, k, v, i_raw, f_raw, o_raw, f_bias, norm_g):
    B, T = q.shape[0], q.shape[1]
    heads = lambda a: a.reshape(B, T, GROUP_HEADS, HEAD_DIM)
    h = mlstm_chunkwise(heads(q), heads(k) * HEAD_DIM ** -0.5, heads(v), i_raw, f_raw + f_bias)
    h = rms_norm(h.astype(q.dtype), norm_g) * jax.nn.sigmoid(heads(o_raw))
    return h.reshape(B, T, GROUP_WIDTH)


def kernel(x, c, ada_w, ada_b, norm_g, ffn1_w13, ffn1_w2, ffn2_w13, ffn2_w2, w_in, b_in, q_norm_g, k_norm_g, cmp_pos, cmp_k_w1, cmp_k_w2, cmp_v_w1, cmp_v_w2, t5_table, sc_conv_w, gdn_conv_w, gdn_A_log, gdn_dt_bias, gdn_norm_g, mlstm_f_bias, mlstm_norm_g, mix_norm_g, w_out):
    B, T, D = x.shape
    depth = ada_w.shape[0]
    bf16 = jnp.bfloat16
    names = [n for n, _ in IN_LAYOUT]
    splits = np.cumsum([s for _, s in IN_LAYOUT])[:-1].tolist()
    x2 = x.reshape(B * T, D)
    for l in range(depth):
        mod = ada_modulation(c, ada_w[l].astype(bf16), ada_b[l][None, :])
        mod = mod.reshape(B, N_SUBLAYERS, 3, 1, D)
        w13 = ffn1_w13[l].astype(bf16)
        x2 = ffn_half_step(x2, norm_g[l, 0][None, :], mod[:, 0, 0], mod[:, 0, 1], mod[:, 0, 2],
                           w13[:, :D_FF], w13[:, D_FF:], ffn1_w2[l].astype(bf16), T)
        pr = in_projection(x2, norm_g[l, 1][None, :], mod[:, 1, 0], mod[:, 1, 1],
                           w_in[l].astype(bf16), b_in[l][None, :], T).reshape(B, T, -1)
        p = dict(zip(names, jnp.split(pr, splits, axis=-1)))
        y_a = nsa_mixer(p["a_q"], p["a_k_cmp"], p["a_v_cmp"], p["a_k_slc"], p["a_v_slc"],
                        p["a_k_win"], p["a_v_win"], p["a_gate"], q_norm_g[l], k_norm_g[l],
                        cmp_pos[l], cmp_k_w1[l], cmp_k_w2[l], cmp_v_w1[l], cmp_v_w2[l], t5_table)
        y_b = short_conv_mixer(p["b_b"], p["b_c"], p["b_x"], sc_conv_w[l])
        y_c = gated_deltanet(p["c_q"], p["c_k"], p["c_v"], p["c_beta"], p["c_alpha"], p["c_z"],
                             gdn_conv_w[l], gdn_A_log[l], gdn_dt_bias[l], gdn_norm_g[l])
        y_d = mlstm_mixer(p["d_q"], p["d_k"], p["d_v"], p["d_i"], p["d_f"], p["d_o"],
                          mlstm_f_bias[l], mlstm_norm_g[l])
        y = jnp.concatenate([rms_norm(y_a, mix_norm_g[l, 0]), rms_norm(y_b, mix_norm_g[l, 1]), y_c, y_d],
                            axis=-1)
        x2 = out_projection(x2, y.reshape(B * T, -1), mod[:, 1, 2], w_out[l].astype(bf16), T)
        w13 = ffn2_w13[l].astype(bf16)
        x2 = ffn_half_step(x2, norm_g[l, 2][None, :], mod[:, 2, 0], mod[:, 2, 1], mod[:, 2, 2],
                           w13[:, :D_FF], w13[:, D_FF:], ffn2_w2[l].astype(bf16), T)
    return x2.reshape(B, T, D)
```

```python
import functools
import math

import jax
import jax.numpy as jnp
import numpy as np
from jax import lax
from jax.experimental import pallas as pl
from jax.experimental.pallas import tpu as pltpu

D_MODEL = 1024
HEAD_DIM = 64
GROUP_HEADS = 4
GROUP_WIDTH = GROUP_HEADS * HEAD_DIM
CMP_STRIDE = 16
CMP_BLOCK = 32
SLC_BLOCK = 64
N_SLC = 16
N_LOCAL_SLC = 2
WINDOW = 512
Q_BLOCK = 128
FORCE = 1e6
N_BUCKETS = 32
MAX_DISTANCE = 128
GDN_CHUNK = 64
MLSTM_CHUNK = 64
D_FF = 2816
N_SUBLAYERS = 3
EPS = 1e-6

IN_LAYOUT = (
    ("a_q", 256), ("a_k_cmp", 64), ("a_v_cmp", 64),
    ("a_k_slc", 64), ("a_v_slc", 64), ("a_k_win", 64), ("a_v_win", 64),
    ("a_gate", 12),
    ("b_b", 256), ("b_c", 256), ("b_x", 256),
    ("c_q", 256), ("c_k", 256), ("c_v", 256),
    ("c_beta", 4), ("c_alpha", 4), ("c_z", 256),
    ("d_q", 256), ("d_k", 256), ("d_v", 256),
    ("d_i", 4), ("d_f", 4), ("d_o", 256),
)

VMEM_LIMIT_BYTES = 56 * 1024 * 1024
FFN_TOKEN_TILE = 512
FFN_CHUNK = 256
PROJ_TOKEN_TILE = 512


def _modulated_norm(x, g, scale, shift):
    y = x * lax.rsqrt(jnp.mean(x * x, axis=-1, keepdims=True) + EPS)
    return (y * g) * (1.0 + scale) + shift


def _ada_kernel(c_ref, w_ref, b_ref, o_ref):
    c = c_ref[...]
    cond = c * jax.nn.sigmoid(c)
    o_ref[...] = jnp.dot(cond.astype(jnp.bfloat16), w_ref[...],
                         preferred_element_type=jnp.float32) + b_ref[...]


def ada_modulation(c, w, b):
    B, D = c.shape
    N = w.shape[1]
    tn = 1152
    return pl.pallas_call(
        _ada_kernel,
        grid=(N // tn,),
        in_specs=[pl.BlockSpec((B, D), lambda j: (0, 0)),
                  pl.BlockSpec((D, tn), lambda j: (0, j)),
                  pl.BlockSpec((1, tn), lambda j: (0, j))],
        out_specs=pl.BlockSpec((B, tn), lambda j: (0, j)),
        out_shape=jax.ShapeDtypeStruct((B, N), jnp.float32),
        name="ada_modulation",
    )(c, w, b)


def _ffn_kernel(x_ref, g_ref, shift_ref, scale_ref, gate_ref, w1_ref, w3_ref, w2_ref, o_ref, act_ref):
    x = x_ref[...]
    h = _modulated_norm(x, g_ref[...], scale_ref[0], shift_ref[0]).astype(jnp.bfloat16)
    n_chunks = act_ref.shape[1] // FFN_CHUNK
    for ci in range(n_chunks):
        cs = slice(ci * FFN_CHUNK, (ci + 1) * FFN_CHUNK)
        a = jnp.dot(h, w1_ref[:, cs], preferred_element_type=jnp.float32)
        b = jnp.dot(h, w3_ref[:, cs], preferred_element_type=jnp.float32)
        act_ref[:, cs] = (a * jax.nn.sigmoid(a) * b).astype(jnp.bfloat16)
    y = jnp.dot(act_ref[...], w2_ref[...], preferred_element_type=jnp.float32)
    o_ref[...] = x + (0.5 * gate_ref[0]) * y


def ffn_half_step(x2, g, shift, scale, gate, w1, w3, w2, tokens_per_batch):
    M, D = x2.shape
    F = w2.shape[0]
    tm = FFN_TOKEN_TILE
    tiles_per_batch = tokens_per_batch // tm
    resident = dict(pipeline_mode=pl.Buffered(1))
    mod_spec = pl.BlockSpec((1, 1, D), lambda i: (i // tiles_per_batch, 0, 0))
    return pl.pallas_call(
        _ffn_kernel,
        grid=(M // tm,),
        in_specs=[pl.BlockSpec((tm, D), lambda i: (i, 0)),
                  pl.BlockSpec((1, D), lambda i: (0, 0)),
                  mod_spec, mod_spec, mod_spec,
                  pl.BlockSpec((D, F), lambda i: (0, 0), **resident),
                  pl.BlockSpec((D, F), lambda i: (0, 0), **resident),
                  pl.BlockSpec((F, D), lambda i: (0, 0), **resident)],
        out_specs=pl.BlockSpec((tm, D), lambda i: (i, 0)),
        out_shape=jax.ShapeDtypeStruct((M, D), jnp.float32),
        scratch_shapes=[pltpu.VMEM((tm, F), jnp.bfloat16)],
        compiler_params=pltpu.CompilerParams(dimension_semantics=("arbitrary",),
                                             vmem_limit_bytes=VMEM_LIMIT_BYTES),
        name="ffn_half_step",
    )(x2, g, shift, scale, gate, w1, w3, w2)


def _in_proj_kernel(x_ref, g_ref, shift_ref, scale_ref, wa_ref, wb_ref, wc_ref, wd_ref,
                    ba_ref, bb_ref, bc_ref, bd_ref, oa_ref, ob_ref, oc_ref, od_ref):
    h = _modulated_norm(x_ref[...], g_ref[...], scale_ref[0], shift_ref[0]).astype(jnp.bfloat16)
    for w_ref, b_ref, o_ref in ((wa_ref, ba_ref, oa_ref), (wb_ref, bb_ref, ob_ref),
                                (wc_ref, bc_ref, oc_ref), (wd_ref, bd_ref, od_ref)):
        o_ref[...] = jnp.dot(h, w_ref[...], preferred_element_type=jnp.float32) + b_ref[...]


def in_projection(x2, g, shift, scale, ws, bs, tokens_per_batch):
    M, D = x2.shape
    tm = PROJ_TOKEN_TILE
    tiles_per_batch = tokens_per_batch // tm
    mod_spec = pl.BlockSpec((1, 1, D), lambda i: (i // tiles_per_batch, 0, 0))
    const = lambda a, **kw: pl.BlockSpec(a.shape, lambda i: (0, 0), **kw)
    return pl.pallas_call(
        _in_proj_kernel,
        grid=(M // tm,),
        in_specs=[pl.BlockSpec((tm, D), lambda i: (i, 0)), const(g), mod_spec, mod_spec]
                 + [const(w, pipeline_mode=pl.Buffered(1)) for w in ws] + [const(b) for b in bs],
        out_specs=[pl.BlockSpec((tm, w.shape[1]), lambda i: (i, 0)) for w in ws],
        out_shape=[jax.ShapeDtypeStruct((M, w.shape[1]), jnp.float32) for w in ws],
        compiler_params=pltpu.CompilerParams(dimension_semantics=("arbitrary",),
                                             vmem_limit_bytes=VMEM_LIMIT_BYTES),
        name="in_projection",
    )(x2, g, shift, scale, *ws, *bs)


def _group_columns():
    offs, o = {}, 0
    for n, s in IN_LAYOUT:
        offs[n] = np.arange(o, o + s)
        o += s
    cat = lambda names: np.concatenate([offs[n] for n in names])
    return (cat(["a_q", "a_k_cmp", "a_v_cmp", "a_k_slc", "a_v_slc", "a_k_win", "a_v_win", "a_gate"]),
            cat(["b_b", "b_c", "b_x"]),
            cat(["c_q", "c_k", "c_v", "c_z", "c_beta", "c_alpha"]),
            cat(["d_q", "d_k", "d_v", "d_o", "d_i", "d_f"]))


GROUP_SLAB_WIDTH = (768, 768, 1152, 1152)


def split_in_weights(w, b):
    ws, bs = [], []
    for idx, width in zip(_group_columns(), GROUP_SLAB_WIDTH):
        pad = width - idx.size
        ws.append(jnp.pad(w[:, idx], ((0, 0), (0, pad))).astype(jnp.bfloat16))
        bs.append(jnp.pad(b[idx], (0, pad))[None, :])
    return ws, bs


def _out_proj_kernel(x_ref, y_ref, gate_ref, w_ref, o_ref):
    z = jnp.dot(y_ref[...].astype(jnp.bfloat16), w_ref[...], preferred_element_type=jnp.float32)
    o_ref[...] = x_ref[...] + gate_ref[0] * z


def out_projection(x2, y2, gate, w, tokens_per_batch):
    M, D = x2.shape
    K = y2.shape[1]
    tm = PROJ_TOKEN_TILE
    tiles_per_batch = tokens_per_batch // tm
    return pl.pallas_call(
        _out_proj_kernel,
        grid=(M // tm,),
        in_specs=[pl.BlockSpec((tm, D), lambda i: (i, 0)),
                  pl.BlockSpec((tm, K), lambda i: (i, 0)),
                  pl.BlockSpec((1, 1, D), lambda i: (i // tiles_per_batch, 0, 0)),
                  pl.BlockSpec((K, D), lambda i: (0, 0), pipeline_mode=pl.Buffered(1))],
        out_specs=pl.BlockSpec((tm, D), lambda i: (i, 0)),
        out_shape=jax.ShapeDtypeStruct((M, D), jnp.float32),
        compiler_params=pltpu.CompilerParams(dimension_semantics=("arbitrary",),
                                             vmem_limit_bytes=VMEM_LIMIT_BYTES),
        name="out_projection",
    )(x2, y2, gate, w)


def rms_norm(x, g):
    xf = x.astype(jnp.float32)
    y = xf * lax.rsqrt(jnp.mean(xf * xf, axis=-1, keepdims=True) + EPS)
    return (y * g).astype(x.dtype)


def l2_norm(x):
    xf = x.astype(jnp.float32)
    return (xf * lax.rsqrt(jnp.sum(xf * xf, axis=-1, keepdims=True) + EPS)).astype(x.dtype)


def causal_dwconv(x, w):
    K, C = w.shape
    return lax.conv_general_dilated(
        x, w[:, None, :], window_strides=(1,), padding=[(K - 1, 0)],
        dimension_numbers=("NWC", "WIO", "NWC"), feature_group_count=C)


def masked_softmax(logits, mask):
    l = jnp.where(mask, logits.astype(jnp.float32), -1e30)
    m = jnp.max(l, axis=-1, keepdims=True)
    e = jnp.where(mask, jnp.exp(l - m), 0.0)
    return e / jnp.maximum(jnp.sum(e, axis=-1, keepdims=True), 1e-30)


def t5_bucket(dist):
    n = jnp.maximum(dist, 0)
    max_exact = N_BUCKETS // 2
    nf = jnp.maximum(n, 1).astype(jnp.float32)
    large = max_exact + (jnp.log(nf / max_exact) / math.log(MAX_DISTANCE / max_exact)
                         * (N_BUCKETS - max_exact)).astype(jnp.int32)
    large = jnp.minimum(large, N_BUCKETS - 1)
    return jnp.where(n < max_exact, n, large)


NEG = -1e30
HIGHEST = lax.Precision.HIGHEST


def _nt_dot(a, b):
    return lax.dot_general(a, b, (((1,), (1,)), ((), ())), preferred_element_type=jnp.float32)


def _stack_heads(qn):
    lane = lax.broadcasted_iota(jnp.int32, (qn.shape[0], 128), 1)
    lo = lane < HEAD_DIM
    zero = jnp.zeros((), qn.dtype)
    halves = (qn[:, 0:128], qn[:, 128:256])
    return jnp.concatenate([jnp.where(lo if h % 2 == 0 else ~lo, halves[h // 2], zero)
                            for h in range(GROUP_HEADS)], axis=0)


def _nsa_prep_kernel(q_ref, kvs_ref, kvw_ref, qg_ref, kg_ref, seg_ref,
                     qn_ref, kxs_ref, v1s_ref, kkw_ref, v1w_ref):
    tm = q_ref.shape[0]
    q = q_ref[...]
    ss = jnp.dot(q * q, seg_ref[...], precision=HIGHEST, preferred_element_type=jnp.float32)
    qn_ref[...] = (q * lax.rsqrt(ss * (1.0 / HEAD_DIM) + EPS) * qg_ref[...]).astype(qn_ref.dtype)
    lane = lax.broadcasted_iota(jnp.int32, (tm, 128), 1)
    lo = lane < HEAD_DIM

    def split(x):
        xr = pltpu.roll(x, HEAD_DIM, 1)
        ss = jnp.sum(jnp.where(lo, x * x, 0.0), axis=-1, keepdims=True)
        kk = jnp.where(lo, x, xr) * lax.rsqrt(ss * (1.0 / HEAD_DIM) + EPS) * kg_ref[...]
        return kk, jnp.where(lo, xr, 1.0)

    kk, v1 = split(kvs_ref[...])
    tok = pl.program_id(1) * tm + lax.broadcasted_iota(jnp.int32, (tm, 128), 0)
    onehot = jnp.where(lane == tok // SLC_BLOCK, 1.0, 0.0)
    kxs_ref[...] = jnp.concatenate([kk, onehot], axis=1).astype(kxs_ref.dtype)
    v1s_ref[...] = v1.astype(v1s_ref.dtype)
    kk, v1 = split(kvw_ref[...])
    kkw_ref[...] = kk.astype(kkw_ref.dtype)
    v1w_ref[...] = v1.astype(v1w_ref.dtype)


def nsa_prep(pa, qg, kg2, seg):
    B, T, _ = pa.shape
    tm = 512
    bf16 = jnp.bfloat16
    col = lambda w, j: pl.BlockSpec((None, tm, w), lambda b, i: (b, i, j))
    const = lambda shape: pl.BlockSpec(shape, lambda b, i: (0, 0))
    return pl.pallas_call(
        _nsa_prep_kernel,
        grid=(B, T // tm),
        in_specs=[col(256, 0), col(128, 3), col(128, 4), const((1, 256)), const((1, 128)), const((256, 256))],
        out_specs=[col(256, 0), col(256, 0), col(128, 0), col(128, 0), col(128, 0)],
        out_shape=[jax.ShapeDtypeStruct((B, T, 256), bf16), jax.ShapeDtypeStruct((B, T, 256), bf16),
                   jax.ShapeDtypeStruct((B, T, 128), bf16), jax.ShapeDtypeStruct((B, T, 128), bf16),
                   jax.ShapeDtypeStruct((B, T, 128), bf16)],
        name="nsa_prep",
    )(pa, pa, pa, qg, kg2, seg)


def _compress_kernel(kch_ref, vch_ref, plo_ref, phi_ref, w1k_ref, w2k_ref, w1v_ref, w2v_ref, kg_ref,
                     kkc_ref, vvc_ref):
    nrow = kch_ref.shape[0]
    half = CMP_STRIDE * HEAD_DIM
    bf16 = jnp.bfloat16
    valid = lax.broadcasted_iota(jnp.int32, (nrow, 128), 0) < nrow - 1

    def comp(ch, w1_ref, w2_ref):
        a = jnp.dot((ch + plo_ref[...]).astype(bf16), w1_ref[0:half, :], preferred_element_type=jnp.float32)
        b = jnp.dot((ch + phi_ref[...]).astype(bf16), w1_ref[half:2 * half, :],
                    preferred_element_type=jnp.float32)
        hid = a + pltpu.roll(b, nrow - 1, 0)
        hid = hid * jax.nn.sigmoid(hid)
        return jnp.dot(hid.astype(bf16), w2_ref[...], preferred_element_type=jnp.float32)

    kc = comp(kch_ref[...], w1k_ref, w2k_ref)
    kc = kc * lax.rsqrt(jnp.mean(kc * kc, axis=-1, keepdims=True) + EPS) * kg_ref[...]
    kkc_ref[...] = jnp.where(valid, kc, 0.0).astype(kkc_ref.dtype)
    vc = comp(vch_ref[...], w1v_ref, w2v_ref)
    vvc_ref[...] = jnp.where(valid, vc, 0.0).astype(vvc_ref.dtype)


def nsa_compress(kch, vch, plo, phi, w1k, w2k, w1v, w2v, kg2):
    B, nrow, width = kch.shape
    bf16 = jnp.bfloat16
    full = lambda a: pl.BlockSpec(a.shape, lambda b: (0,) * a.ndim)
    bspec = pl.BlockSpec((None, nrow, width), lambda b: (b, 0, 0))
    ospec = pl.BlockSpec((None, nrow, 128), lambda b: (b, 0, 0))
    return pl.pallas_call(
        _compress_kernel,
        grid=(B,),
        in_specs=[bspec, bspec, full(plo), full(phi), full(w1k), full(w2k), full(w1v), full(w2v), full(kg2)],
        out_specs=[ospec, ospec],
        out_shape=[jax.ShapeDtypeStruct((B, nrow, 128), bf16), jax.ShapeDtypeStruct((B, nrow, 128), bf16)],
        name="nsa_compress",
    )(kch, vch, plo, phi, w1k, w2k, w1v, w2v, kg2)


def _cmp_select_kernel(qn_ref, kkc_ref, vvc_ref, cb_ref, ovt_ref, ocmp_ref, pen_ref, sc_ref, *, n_sel):
    i = pl.program_id(1)
    tq = qn_ref.shape[0]
    nblk = ovt_ref.shape[0]
    bf16 = jnp.bfloat16
    q4 = _stack_heads(qn_ref[...])
    s = _nt_dot(q4, kkc_ref[...]) + cb_ref[...]
    m = jnp.max(s, axis=-1, keepdims=True)
    e = jnp.where(s > 0.5 * NEG, jnp.exp(s - m), 0.0)
    p = e * (1.0 / jnp.maximum(jnp.sum(e, axis=-1, keepdims=True), 1e-30))
    o = jnp.dot(p.astype(bf16), vvc_ref[...], preferred_element_type=jnp.float32)
    lo = lax.broadcasted_iota(jnp.int32, (tq, 128), 1) < HEAD_DIM
    ocmp_ref[:, 0:128] = jnp.where(lo, o[0:tq], o[tq:2 * tq])
    ocmp_ref[:, 128:256] = jnp.where(lo, o[2 * tq:3 * tq], o[3 * tq:4 * tq])
    psum = p[0:tq] + p[tq:2 * tq] + p[2 * tq:3 * tq] + p[3 * tq:4 * tq]
    hi = psum.astype(bf16)
    lo_part = (psum - hi.astype(jnp.float32)).astype(bf16)
    score = _nt_dot(ovt_ref[...], hi) + _nt_dot(ovt_ref[...], lo_part)
    blk = lax.broadcasted_iota(jnp.int32, (nblk, tq), 0)
    qpos = i * tq + lax.broadcasted_iota(jnp.int32, (nblk, tq), 1)
    cur = qpos // SLC_BLOCK
    forced = (blk == 0) | ((cur - blk >= 0) & (cur - blk < N_LOCAL_SLC))
    score = jnp.where(forced, FORCE, score)
    score = jnp.where(blk <= cur, score, -FORCE)
    sc_ref[...] = score
    cnt = jnp.zeros((nblk, tq), jnp.float32)
    for jp in range(nblk):
        row = sc_ref[jp:jp + 1, :]
        tie = jnp.where(blk > jp, 1.0, 0.0)
        cnt = cnt + jnp.where(row > score, 1.0, jnp.where(row == score, tie, 0.0))
    pen = jnp.where(cnt < n_sel, 0.0, NEG)
    if nblk < 128:
        pen = jnp.concatenate([pen, jnp.zeros((128 - nblk, tq), jnp.float32)], axis=0)
    pen_ref[...] = pen.T.astype(pen_ref.dtype)


def nsa_cmp_select(qn, kkc, vvc, cbias, ovt, n_sel):
    B, T, _ = qn.shape
    nrow = kkc.shape[1]
    nblk = ovt.shape[0]
    tq = Q_BLOCK
    return pl.pallas_call(
        functools.partial(_cmp_select_kernel, n_sel=n_sel),
        grid=(B, T // tq),
        in_specs=[pl.BlockSpec((None, tq, 256), lambda b, i: (b, i, 0)),
                  pl.BlockSpec((None, nrow, 128), lambda b, i: (b, 0, 0)),
                  pl.BlockSpec((None, nrow, 128), lambda b, i: (b, 0, 0)),
                  pl.BlockSpec((None, 4 * tq, nrow), lambda b, i: (i, 0, 0)),
                  pl.BlockSpec((nblk, nrow), lambda b, i: (0, 0))],
        out_specs=[pl.BlockSpec((None, tq, 256), lambda b, i: (b, i, 0)),
                   pl.BlockSpec((None, tq, 128), lambda b, i: (b, i, 0))],
        out_shape=[jax.ShapeDtypeStruct((B, T, 256), jnp.float32),
                   jax.ShapeDtypeStruct((B, T, 128), jnp.bfloat16)],
        scratch_shapes=[pltpu.VMEM((nblk, tq), jnp.float32)],
        name="nsa_cmp_select",
    )(qn, kkc, vvc, cbias, ovt)


def _nsa_attn_kernel(qn_ref, pen_ref, ocmp_ref, gate_ref, kxs_ref, v1s_ref, kkw_ref, v1w_ref,
                     tabs_ref, tabw_ref, mixg_ref, o_ref, acc_ref, m_ref):
    i = pl.program_id(1)
    tq = qn_ref.shape[0]
    bf16 = jnp.bfloat16
    q4 = _stack_heads(qn_ref[...])
    pen = pen_ref[...]
    lhs_sel = jnp.concatenate([q4, jnp.concatenate([pen] * GROUP_HEADS, axis=0)], axis=1)

    def branch(lhs, k_ref, v_ref, tab_ref, first_tile):
        n_tab = tab_ref.shape[0]
        m_ref[...] = jnp.full(m_ref.shape, NEG, jnp.float32)
        acc_ref[...] = jnp.zeros(acc_ref.shape, jnp.float32)

        def body(j, carry):
            off = pl.multiple_of(j * tq, tq)
            s = _nt_dot(lhs, k_ref[pl.ds(off, tq), :]) + tab_ref[jnp.minimum(i - j, n_tab - 1)]
            m_old = m_ref[...]
            m_new = jnp.maximum(m_old, jnp.max(s, axis=-1, keepdims=True))
            p = jnp.exp(s - m_new)
            pv = jnp.dot(p.astype(bf16), v_ref[pl.ds(off, tq), :], preferred_element_type=jnp.float32)
            acc_ref[...] = jnp.exp(m_old - m_new) * acc_ref[...] + pv
            m_ref[...] = m_new
            return carry

        lax.fori_loop(first_tile, i + 1, body, 0)
        acc = acc_ref[...]
        return acc / jnp.maximum(pltpu.roll(acc, HEAD_DIM, 1), 1e-30)

    o_s = branch(lhs_sel, kxs_ref, v1s_ref, tabs_ref, 0)
    o_w = branch(q4, kkw_ref, v1w_ref, tabw_ref, jnp.maximum(i - WINDOW // tq, 0))
    g = jax.nn.sigmoid(gate_ref[...])
    lo = lax.broadcasted_iota(jnp.int32, (tq, 128), 1) < HEAD_DIM
    ocmp = ocmp_ref[...]
    halves = []
    for pair in range(2):
        comb, gc = [], []
        for h in (2 * pair, 2 * pair + 1):
            rows = slice(h * tq, (h + 1) * tq)
            comb.append(g[:, 3 * h + 1:3 * h + 2] * o_s[rows] + g[:, 3 * h + 2:3 * h + 3] * o_w[rows])
            gc.append(g[:, 3 * h:3 * h + 1])
        both = jnp.where(lo, comb[0], pltpu.roll(comb[1], HEAD_DIM, 1))
        halves.append(both + jnp.where(lo, gc[0], gc[1]) * ocmp[:, pair * 128:(pair + 1) * 128])
    y = jnp.concatenate(halves, axis=1)
    o_ref[...] = y * lax.rsqrt(jnp.mean(y * y, axis=-1, keepdims=True) + EPS) * mixg_ref[...]


def nsa_attention(qn, pen, ocmp, pa, kxs, v1s, kkw, v1w, tab_s, tab_w, mixg):
    B, T, _ = qn.shape
    tq = Q_BLOCK
    tile = lambda w, j: pl.BlockSpec((None, tq, w), lambda b, i: (b, i, j))
    whole = lambda w: pl.BlockSpec((None, T, w), lambda b, i: (b, 0, 0))
    full = lambda a: pl.BlockSpec(a.shape, lambda b, i: (0,) * a.ndim)
    return pl.pallas_call(
        _nsa_attn_kernel,
        grid=(B, T // tq),
        in_specs=[tile(256, 0), tile(128, 0), tile(256, 0), tile(128, 5),
                  whole(256), whole(128), whole(128), whole(128), full(tab_s), full(tab_w), full(mixg)],
        out_specs=tile(256, 0),
        out_shape=jax.ShapeDtypeStruct((B, T, 256), jnp.float32),
        scratch_shapes=[pltpu.VMEM((GROUP_HEADS * tq, 128), jnp.float32),
                        pltpu.VMEM((GROUP_HEADS * tq, 1), jnp.float32)],
        compiler_params=pltpu.CompilerParams(dimension_semantics=("arbitrary", "arbitrary"),
                                             vmem_limit_bytes=VMEM_LIMIT_BYTES),
        name="nsa_attention",
    )(qn, pen, ocmp, pa, kxs, v1s, kkw, v1w, tab_s, tab_w, mixg)


def nsa_bias_tables(t5_table, T):
    H = GROUP_HEADS
    tq = Q_BLOCK
    dmax = WINDOW + 2 * tq
    by_dist = t5_table[t5_bucket(jnp.arange(dmax))].T

    def tile(dist, keep):
        b = by_dist[:, jnp.clip(dist, 0, dmax - 1)]
        return jnp.where(keep, b, NEG).reshape(H * dist.shape[0], dist.shape[1])

    r = jnp.arange(tq)[:, None]
    c = jnp.arange(tq)[None, :]
    d = lambda delta: delta * tq + r - c
    tab_s = jnp.stack([tile(d(k), d(k) >= 0) for k in range(3)])
    tab_w = jnp.stack([tile(d(k), (d(k) >= 0) & (d(k) < WINDOW)) for k in range(WINDOW // tq + 1)])
    nrow = T // CMP_STRIDE
    per_tile = tq // CMP_STRIDE
    m = jnp.arange(2 * nrow)[None, :] - nrow
    dist = r - CMP_STRIDE * m - (CMP_BLOCK - 1)
    base = tile(dist, dist >= 0)
    cbias = jnp.stack([base[:, nrow - per_tile * i: 2 * nrow - per_tile * i] for i in range(T // tq)])
    return tab_s, tab_w, cbias


def nsa_overlap_t(T):
    nrow = T // CMP_STRIDE
    nblk = T // SLC_BLOCK
    ci = np.arange(nrow)[None, :]
    bj = np.arange(nblk)[:, None]
    ov = (ci * CMP_STRIDE < (bj + 1) * SLC_BLOCK) & (ci * CMP_STRIDE + CMP_BLOCK > bj * SLC_BLOCK)
    ov = ov & (ci < nrow - 1)
    return jnp.asarray(ov.astype(np.float32), dtype=jnp.bfloat16)


def nsa_mixer_pallas(pa, q_norm_g, k_norm_g, cmp_pos, cmp_k_w1, cmp_k_w2, cmp_v_w1, cmp_v_w2,
                     tables, ovt, mixg):
    B, T, _ = pa.shape
    bf16 = jnp.bfloat16
    tab_s, tab_w, cbias = tables
    qg = (jnp.tile(q_norm_g, GROUP_HEADS) * HEAD_DIM ** -0.5)[None, :]
    kg2 = jnp.tile(k_norm_g, 2)[None, :]
    seg = jnp.asarray(np.kron(np.eye(GROUP_HEADS), np.ones((HEAD_DIM, HEAD_DIM))), jnp.float32)
    qn, kxs, v1s, kkw, v1w = nsa_prep(pa, qg, kg2, seg)
    nrow = T // CMP_STRIDE
    kch = pa[:, :, 256:320].reshape(B, nrow, CMP_STRIDE * HEAD_DIM)
    vch = pa[:, :, 320:384].reshape(B, nrow, CMP_STRIDE * HEAD_DIM)
    plo = cmp_pos[:CMP_STRIDE].reshape(1, -1)
    phi = cmp_pos[CMP_STRIDE:].reshape(1, -1)
    dup = lambda w: jnp.concatenate([w, w], axis=1).astype(bf16)
    kkc, vvc = nsa_compress(kch, vch, plo, phi, cmp_k_w1.astype(bf16), dup(cmp_k_w2),
                            cmp_v_w1.astype(bf16), dup(cmp_v_w2), kg2)
    n_sel = min(N_SLC, T // SLC_BLOCK)
    ocmp, pen = nsa_cmp_select(qn, kkc, vvc, cbias, ovt, n_sel)
    return nsa_attention(qn, pen, ocmp, pa, kxs, v1s, kkw, v1w, tab_s, tab_w, mixg)


def nsa_mixer_jnp_unused(q, k_cmp, v_cmp, k_slc, v_slc, k_win, v_win, gate_raw, q_norm_g, k_norm_g,
              cmp_pos, cmp_k_w1, cmp_k_w2, cmp_v_w1, cmp_v_w2, t5_table):
    B, T = q.shape[0], q.shape[1]
    dt = q.dtype
    H, dh = GROUP_HEADS, HEAD_DIM
    scale = dh ** -0.5
    q = rms_norm(q.reshape(B, T, H, dh), q_norm_g) * scale
    k_slc = rms_norm(k_slc, k_norm_g)
    k_win = rms_norm(k_win, k_norm_g)
    t = jnp.arange(T)
    nc = T // CMP_STRIDE - 1

    def compress(kv, w1, w2):
        ch = kv.reshape(B, T // CMP_STRIDE, CMP_STRIDE, dh)
        blocks = jnp.concatenate([ch[:, :-1], ch[:, 1:]], axis=2) + cmp_pos
        hid = jax.nn.silu(blocks.reshape(B, nc, CMP_BLOCK * dh) @ w1)
        return hid @ w2

    kc = rms_norm(compress(k_cmp, cmp_k_w1, cmp_k_w2), k_norm_g)
    vc = compress(v_cmp, cmp_v_w1, cmp_v_w2)
    cmp_end = jnp.arange(nc) * CMP_STRIDE + CMP_BLOCK - 1
    cmp_dist = t[:, None] - cmp_end[None, :]
    cmp_bias = t5_table[t5_bucket(cmp_dist)].transpose(2, 0, 1)
    cmp_logits = jnp.einsum("bthd,bnd->bhtn", q, kc) + cmp_bias
    p_cmp = masked_softmax(cmp_logits, cmp_dist >= 0)
    o_cmp = jnp.einsum("bhtn,bnd->bthd", p_cmp.astype(dt), vc)

    nblk = T // SLC_BLOCK
    n_sel = min(N_SLC, nblk)
    ci = np.arange(nc)[:, None]
    bj = np.arange(nblk)[None, :]
    overlap = ((ci * CMP_STRIDE < (bj + 1) * SLC_BLOCK)
               & (ci * CMP_STRIDE + CMP_BLOCK > bj * SLC_BLOCK)).astype(np.float32)
    score = jnp.sum(p_cmp, axis=1) @ jnp.asarray(overlap)
    cur = (t // SLC_BLOCK)[:, None]
    blk = jnp.arange(nblk)[None, :]
    forced = (blk == 0) | ((cur - blk >= 0) & (cur - blk < N_LOCAL_SLC))
    score = jnp.where(forced, FORCE, score)
    score = jnp.where(blk <= cur, score, -FORCE)
    _, sel_idx = lax.top_k(score, n_sel)

    ks_blocks = k_slc.reshape(B, nblk, SLC_BLOCK, dh)
    vs_blocks = v_slc.reshape(B, nblk, SLC_BLOCK, dh)
    kw_p = jnp.pad(k_win, ((0, 0), (WINDOW, 0), (0, 0)))
    vw_p = jnp.pad(v_win, ((0, 0), (WINDOW, 0), (0, 0)))
    nq = T // Q_BLOCK
    J = n_sel * SLC_BLOCK

    def qblock(i):
        s = i * Q_BLOCK
        qb = lax.dynamic_slice_in_dim(q, s, Q_BLOCK, axis=1)
        tq = s + jnp.arange(Q_BLOCK)
        kwb = lax.dynamic_slice_in_dim(kw_p, s, WINDOW + Q_BLOCK, axis=1)
        vwb = lax.dynamic_slice_in_dim(vw_p, s, WINDOW + Q_BLOCK, axis=1)
        kpos = s - WINDOW + jnp.arange(WINDOW + Q_BLOCK)
        wd = tq[:, None] - kpos[None, :]
        wmask = (wd >= 0) & (wd < WINDOW) & (kpos[None, :] >= 0)
        wl = jnp.einsum("bqhd,bkd->bhqk", qb, kwb) + t5_table[t5_bucket(wd)].transpose(2, 0, 1)
        o_w = jnp.einsum("bhqk,bkd->bqhd", masked_softmax(wl, wmask).astype(dt), vwb)
        idx = lax.dynamic_slice_in_dim(sel_idx, s, Q_BLOCK, axis=1)
        ksel = jax.vmap(lambda kb, ib: kb[ib])(ks_blocks, idx)
        vsel = jax.vmap(lambda vb, ib: vb[ib])(vs_blocks, idx)
        spos = idx[..., None] * SLC_BLOCK + jnp.arange(SLC_BLOCK)
        sd = (tq[None, :, None, None] - spos).reshape(B, Q_BLOCK, J)
        sl = jnp.einsum("bqhd,bqnsd->bhqns", qb, ksel).reshape(B, H, Q_BLOCK, J)
        sl = sl + t5_table[t5_bucket(sd)].transpose(0, 3, 1, 2)
        ps = masked_softmax(sl, (sd >= 0)[:, None]).astype(dt)
        o_s = jnp.einsum("bhqj,bqjd->bqhd", ps, vsel.reshape(B, Q_BLOCK, J, dh))
        return o_w, o_s

    o_win, o_slc = lax.map(qblock, jnp.arange(nq))
    o_win = o_win.transpose(1, 0, 2, 3, 4).reshape(B, T, H, dh)
    o_slc = o_slc.transpose(1, 0, 2, 3, 4).reshape(B, T, H, dh)
    g = jax.nn.sigmoid(gate_raw.reshape(B, T, H, 3))
    o = g[..., 0:1] * o_cmp + g[..., 1:2] * o_slc + g[..., 2:3] * o_win
    return o.reshape(B, T, GROUP_WIDTH)


def short_conv_mixer(b_gate, c_gate, xs, conv_w):
    return b_gate * causal_dwconv(c_gate * xs, conv_w)


def to_chunks(a, L):
    B, T, H = a.shape[:3]
    a = a.reshape((B, T // L, L, H) + a.shape[3:])
    return a.transpose((1, 0, 3, 2) + tuple(range(4, a.ndim)))


def from_chunks(a):
    N, B, H, L, d = a.shape
    return a.transpose(1, 0, 3, 2, 4).reshape(B, N * L, H, d)


def chunk_gated_delta_rule(q, k, v, g, beta):
    f32 = jnp.float32
    L = GDN_CHUNK
    q, k, v = (to_chunks(a.astype(f32), L) for a in (q, k, v))
    g = to_chunks(g.astype(f32), L)
    beta = to_chunks(beta.astype(f32), L)
    gc = jnp.cumsum(g, axis=-1)
    incl = jnp.tril(jnp.ones((L, L), bool))
    strict = jnp.tril(jnp.ones((L, L), bool), -1)
    seg = jnp.exp(jnp.where(incl, gc[..., :, None] - gc[..., None, :], -jnp.inf))
    kb = k * beta[..., None]
    a_mat = jnp.where(strict, jnp.einsum("nbhid,nbhjd->nbhij", kb, k) * seg, 0.0) + jnp.eye(L, dtype=f32)
    rhs = jnp.concatenate([v * beta[..., None], kb * jnp.exp(gc)[..., None]], axis=-1)
    sol = lax.linalg.triangular_solve(a_mat, rhs, left_side=True, lower=True, unit_diagonal=True)
    u, w = sol[..., :HEAD_DIM], sol[..., HEAD_DIM:]
    qk = jnp.einsum("nbhid,nbhjd->nbhij", q, k) * seg
    q_dec = q * jnp.exp(gc)[..., None]
    k_dec = k * jnp.exp(gc[..., -1:] - gc)[..., None]
    g_tot = jnp.exp(gc[..., -1])

    def step(S, inp):
        u_c, w_c, qd_c, kd_c, qk_c, gt_c = inp
        v_new = u_c - jnp.einsum("bhlk,bhkv->bhlv", w_c, S)
        o = jnp.einsum("bhlk,bhkv->bhlv", qd_c, S) + jnp.einsum("bhij,bhjv->bhiv", qk_c, v_new)
        S = S * gt_c[..., None, None] + jnp.einsum("bhlk,bhlv->bhkv", kd_c, v_new)
        return S, o

    S0 = jnp.zeros(u.shape[1:3] + (HEAD_DIM, HEAD_DIM), f32)
    _, o = lax.scan(step, S0, (u, w, q_dec, k_dec, qk, g_tot))
    return from_chunks(o)


def gated_deltanet(q, k, v, beta_raw, alpha_raw, z, conv_w, A_log, dt_bias, norm_g):
    B, T = q.shape[0], q.shape[1]
    qkv = jax.nn.silu(causal_dwconv(jnp.concatenate([q, k, v], axis=-1), conv_w))
    q, k, v = (a.reshape(B, T, GROUP_HEADS, HEAD_DIM) for a in jnp.split(qkv, 3, axis=-1))
    q = l2_norm(q) * HEAD_DIM ** -0.5
    k = l2_norm(k)
    beta = jax.nn.sigmoid(beta_raw)
    g = -jnp.exp(A_log) * jax.nn.softplus(alpha_raw + dt_bias)
    o = chunk_gated_delta_rule(q, k, v, g, beta).astype(z.dtype)
    o = rms_norm(o, norm_g) * jax.nn.silu(z.reshape(B, T, GROUP_HEADS, HEAD_DIM))
    return o.reshape(B, T, GROUP_WIDTH)


def mlstm_chunkwise(q, k, v, i_pre, f_pre):
    f32 = jnp.float32
    L = MLSTM_CHUNK
    q, k, v = (to_chunks(a.astype(f32), L) for a in (q, k, v))
    log_i = to_chunks(i_pre.astype(f32), L)
    log_f = to_chunks(jax.nn.log_sigmoid(f_pre.astype(f32)), L)
    b = jnp.cumsum(log_f, axis=-1)
    incl = jnp.tril(jnp.ones((L, L), bool))
    log_w = jnp.where(incl, b[..., :, None] - b[..., None, :] + log_i[..., None, :], -jnp.inf)
    m_intra = jnp.max(log_w, axis=-1)
    log_w_end = b[..., -1:] - b + log_i
    m_end = jnp.max(log_w_end, axis=-1)
    qk = jnp.einsum("nbhid,nbhjd->nbhij", q, k)

    def step(carry, inp):
        C, n, m = carry
        q_c, k_c, v_c, b_c, lw_c, mi_c, lwe_c, me_c, qk_c = inp
        log_inter = b_c + m[..., None]
        m_t = jnp.maximum(log_inter, mi_c)
        w_inter = jnp.exp(log_inter - m_t)
        s = qk_c * jnp.exp(lw_c - m_t[..., None])
        num = (w_inter[..., None] * jnp.einsum("bhlk,bhkv->bhlv", q_c, C)
               + jnp.einsum("bhij,bhjv->bhiv", s, v_c))
        den = w_inter * jnp.einsum("bhlk,bhk->bhl", q_c, n) + jnp.sum(s, axis=-1)
        h = num / jnp.maximum(jnp.abs(den), jnp.exp(-m_t))[..., None]
        m_new = jnp.maximum(b_c[..., -1] + m, me_c)
        w_old = jnp.exp(b_c[..., -1] + m - m_new)
        w_new = jnp.exp(lwe_c - m_new[..., None])
        C = w_old[..., None, None] * C + jnp.einsum("bhl,bhlk,bhlv->bhkv", w_new, k_c, v_c)
        n = w_old[..., None] * n + jnp.einsum("bhl,bhlk->bhk", w_new, k_c)
        return (C, n, m_new), h

    Bsz, H = q.shape[1], q.shape[2]
    carry0 = (jnp.zeros((Bsz, H, HEAD_DIM, HEAD_DIM), f32), jnp.zeros((Bsz, H, HEAD_DIM), f32),
              jnp.zeros((Bsz, H), f32))
    _, h = lax.scan(step, carry0, (q, k, v, b, log_w, m_intra, log_w_end, m_end, qk))
    return from_chunks(h)


def mlstm_mixer(q, k, v, i_raw, f_raw, o_raw, f_bias, norm_g):
    B, T = q.shape[0], q.shape[1]
    heads = lambda a: a.reshape(B, T, GROUP_HEADS, HEAD_DIM)
    h = mlstm_chunkwise(heads(q), heads(k) * HEAD_DIM ** -0.5, heads(v), i_raw, f_raw + f_bias)
    h = rms_norm(h.astype(q.dtype), norm_g) * jax.nn.sigmoid(heads(o_raw))
    return h.reshape(B, T, GROUP_WIDTH)


def kernel(x, c, ada_w, ada_b, norm_g, ffn1_w13, ffn1_w2, ffn2_w13, ffn2_w2, w_in, b_in, q_norm_g, k_norm_g, cmp_pos, cmp_k_w1, cmp_k_w2, cmp_v_w1, cmp_v_w2, t5_table, sc_conv_w, gdn_conv_w, gdn_A_log, gdn_dt_bias, gdn_norm_g, mlstm_f_bias, mlstm_norm_g, mix_norm_g, w_out):
    B, T, D = x.shape
    depth = ada_w.shape[0]
    bf16 = jnp.bfloat16
    x2 = x.reshape(B * T, D)
    tables = nsa_bias_tables(t5_table, T)
    ovt = nsa_overlap_t(T)
    for l in range(depth):
        mod = ada_modulation(c, ada_w[l].astype(bf16), ada_b[l][None, :])
        mod = mod.reshape(B, N_SUBLAYERS, 3, 1, D)
        w13 = ffn1_w13[l].astype(bf16)
        x2 = ffn_half_step(x2, norm_g[l, 0][None, :], mod[:, 0, 0], mod[:, 0, 1], mod[:, 0, 2],
                           w13[:, :D_FF], w13[:, D_FF:], ffn1_w2[l].astype(bf16), T)
        ws, bs = split_in_weights(w_in[l], b_in[l])
        pa, pb, pc, pd = in_projection(x2, norm_g[l, 1][None, :], mod[:, 1, 0], mod[:, 1, 1], ws, bs, T)
        pa, pb, pc, pd = (a.reshape(B, T, -1) for a in (pa, pb, pc, pd))
        y_a = nsa_mixer_pallas(pa, q_norm_g[l], k_norm_g[l], cmp_pos[l], cmp_k_w1[l], cmp_k_w2[l],
                               cmp_v_w1[l], cmp_v_w2[l], tables, ovt, mix_norm_g[l, 0][None, :])
        y_b = short_conv_mixer(pb[..., 0:256], pb[..., 256:512], pb[..., 512:768], sc_conv_w[l])
        y_c = gated_deltanet(pc[..., 0:256], pc[..., 256:512], pc[..., 512:768], pc[..., 1024:1028],
                             pc[..., 1028:1032], pc[..., 768:1024],
                             gdn_conv_w[l], gdn_A_log[l], gdn_dt_bias[l], gdn_norm_g[l])
        y_d = mlstm_mixer(pd[..., 0:256], pd[..., 256:512], pd[..., 512:768], pd[..., 1024:1028],
                          pd[..., 1028:1032], pd[..., 768:1024], mlstm_f_bias[l], mlstm_norm_g[l])
        y = jnp.concatenate([y_a, rms_norm(y_b, mix_norm_g[l, 1]), y_c, y_d], axis=-1)
        x2 = out_projection(x2, y.reshape(B * T, -1), mod[:, 1, 2], w_out[l].astype(bf16), T)
        w13 = ffn2_w13[l].astype(bf16)
        x2 = ffn_half_step(x2, norm_g[l, 2][None, :], mod[:, 2, 0], mod[:, 2, 1], mod[:, 2, 2],
                           w13[:, :D_FF], w13[:, D_FF:], ffn2_w2[l].astype(bf16), T)
    return x2.reshape(B, T, D)
```

```python
import functools
import math

import jax
import jax.numpy as jnp
import numpy as np
from jax import lax
from jax.experimental import pallas as pl
from jax.experimental.pallas import tpu as pltpu

D_MODEL = 1024
HEAD_DIM = 64
GROUP_HEADS = 4
GROUP_WIDTH = GROUP_HEADS * HEAD_DIM
CMP_STRIDE = 16
CMP_BLOCK = 32
SLC_BLOCK = 64
N_SLC = 16
N_LOCAL_SLC = 2
WINDOW = 512
Q_BLOCK = 128
FORCE = 1e6
N_BUCKETS = 32
MAX_DISTANCE = 128
GDN_CHUNK = 64
MLSTM_CHUNK = 64
D_FF = 2816
N_SUBLAYERS = 3
EPS = 1e-6

IN_LAYOUT = (
    ("a_q", 256), ("a_k_cmp", 64), ("a_v_cmp", 64),
    ("a_k_slc", 64), ("a_v_slc", 64), ("a_k_win", 64), ("a_v_win", 64),
    ("a_gate", 12),
    ("b_b", 256), ("b_c", 256), ("b_x", 256),
    ("c_q", 256), ("c_k", 256), ("c_v", 256),
    ("c_beta", 4), ("c_alpha", 4), ("c_z", 256),
    ("d_q", 256), ("d_k", 256), ("d_v", 256),
    ("d_i", 4), ("d_f", 4), ("d_o", 256),
)

VMEM_LIMIT_BYTES = 56 * 1024 * 1024
FFN_TOKEN_TILE = 512
FFN_CHUNK = 256
PROJ_TOKEN_TILE = 512


def _modulated_norm(x, g, scale, shift):
    y = x * lax.rsqrt(jnp.mean(x * x, axis=-1, keepdims=True) + EPS)
    return (y * g) * (1.0 + scale) + shift


def _ada_kernel(c_ref, w_ref, b_ref, o_ref):
    c = c_ref[...]
    cond = c * jax.nn.sigmoid(c)
    o_ref[...] = jnp.dot(cond.astype(jnp.bfloat16), w_ref[...],
                         preferred_element_type=jnp.float32) + b_ref[...]


def ada_modulation(c, w, b):
    B, D = c.shape
    N = w.shape[1]
    tn = 1152
    return pl.pallas_call(
        _ada_kernel,
        grid=(N // tn,),
        in_specs=[pl.BlockSpec((B, D), lambda j: (0, 0)),
                  pl.BlockSpec((D, tn), lambda j: (0, j)),
                  pl.BlockSpec((1, tn), lambda j: (0, j))],
        out_specs=pl.BlockSpec((B, tn), lambda j: (0, j)),
        out_shape=jax.ShapeDtypeStruct((B, N), jnp.float32),
        name="ada_modulation",
    )(c, w, b)


def _ffn_kernel(x_ref, g_ref, shift_ref, scale_ref, gate_ref, w1_ref, w3_ref, w2_ref, o_ref, act_ref):
    x = x_ref[...]
    h = _modulated_norm(x, g_ref[...], scale_ref[0], shift_ref[0]).astype(jnp.bfloat16)
    n_chunks = act_ref.shape[1] // FFN_CHUNK
    for ci in range(n_chunks):
        cs = slice(ci * FFN_CHUNK, (ci + 1) * FFN_CHUNK)
        a = jnp.dot(h, w1_ref[:, cs], preferred_element_type=jnp.float32)
        b = jnp.dot(h, w3_ref[:, cs], preferred_element_type=jnp.float32)
        act_ref[:, cs] = (a * jax.nn.sigmoid(a) * b).astype(jnp.bfloat16)
    y = jnp.dot(act_ref[...], w2_ref[...], preferred_element_type=jnp.float32)
    o_ref[...] = x + (0.5 * gate_ref[0]) * y


def ffn_half_step(x2, g, shift, scale, gate, w1, w3, w2, tokens_per_batch):
    M, D = x2.shape
    F = w2.shape[0]
    tm = FFN_TOKEN_TILE
    tiles_per_batch = tokens_per_batch // tm
    resident = dict(pipeline_mode=pl.Buffered(1))
    mod_spec = pl.BlockSpec((1, 1, D), lambda i: (i // tiles_per_batch, 0, 0))
    return pl.pallas_call(
        _ffn_kernel,
        grid=(M // tm,),
        in_specs=[pl.BlockSpec((tm, D), lambda i: (i, 0)),
                  pl.BlockSpec((1, D), lambda i: (0, 0)),
                  mod_spec, mod_spec, mod_spec,
                  pl.BlockSpec((D, F), lambda i: (0, 0), **resident),
                  pl.BlockSpec((D, F), lambda i: (0, 0), **resident),
                  pl.BlockSpec((F, D), lambda i: (0, 0), **resident)],
        out_specs=pl.BlockSpec((tm, D), lambda i: (i, 0)),
        out_shape=jax.ShapeDtypeStruct((M, D), jnp.float32),
        scratch_shapes=[pltpu.VMEM((tm, F), jnp.bfloat16)],
        compiler_params=pltpu.CompilerParams(dimension_semantics=("arbitrary",),
                                             vmem_limit_bytes=VMEM_LIMIT_BYTES),
        name="ffn_half_step",
    )(x2, g, shift, scale, gate, w1, w3, w2)


def _in_proj_kernel(x_ref, g_ref, shift_ref, scale_ref, wa_ref, wb_ref, wc_ref, wd_ref,
                    ba_ref, bb_ref, bc_ref, bd_ref, oa_ref, ob_ref, oc_ref, od_ref):
    h = _modulated_norm(x_ref[...], g_ref[...], scale_ref[0], shift_ref[0]).astype(jnp.bfloat16)
    for w_ref, b_ref, o_ref in ((wa_ref, ba_ref, oa_ref), (wb_ref, bb_ref, ob_ref),
                                (wc_ref, bc_ref, oc_ref), (wd_ref, bd_ref, od_ref)):
        o_ref[...] = jnp.dot(h, w_ref[...], preferred_element_type=jnp.float32) + b_ref[...]


def in_projection(x2, g, shift, scale, ws, bs, tokens_per_batch):
    M, D = x2.shape
    tm = PROJ_TOKEN_TILE
    tiles_per_batch = tokens_per_batch // tm
    mod_spec = pl.BlockSpec((1, 1, D), lambda i: (i // tiles_per_batch, 0, 0))
    const = lambda a, **kw: pl.BlockSpec(a.shape, lambda i: (0, 0), **kw)
    return pl.pallas_call(
        _in_proj_kernel,
        grid=(M // tm,),
        in_specs=[pl.BlockSpec((tm, D), lambda i: (i, 0)), const(g), mod_spec, mod_spec]
                 + [const(w, pipeline_mode=pl.Buffered(1)) for w in ws] + [const(b) for b in bs],
        out_specs=[pl.BlockSpec((tm, w.shape[1]), lambda i: (i, 0)) for w in ws],
        out_shape=[jax.ShapeDtypeStruct((M, w.shape[1]), jnp.float32) for w in ws],
        compiler_params=pltpu.CompilerParams(dimension_semantics=("arbitrary",),
                                             vmem_limit_bytes=VMEM_LIMIT_BYTES),
        name="in_projection",
    )(x2, g, shift, scale, *ws, *bs)


def _group_columns():
    offs, o = {}, 0
    for n, s in IN_LAYOUT:
        offs[n] = np.arange(o, o + s)
        o += s
    cat = lambda names: np.concatenate([offs[n] for n in names])
    return (cat(["a_q", "a_k_cmp", "a_v_cmp", "a_k_slc", "a_v_slc", "a_k_win", "a_v_win", "a_gate"]),
            cat(["b_b", "b_c", "b_x"]),
            cat(["c_q", "c_k", "c_v", "c_z", "c_beta", "c_alpha"]),
            cat(["d_q", "d_k", "d_v", "d_o", "d_i", "d_f"]))


GROUP_SLAB_WIDTH = (768, 768, 1152, 1152)


def split_in_weights(w, b):
    ws, bs = [], []
    for idx, width in zip(_group_columns(), GROUP_SLAB_WIDTH):
        pad = width - idx.size
        ws.append(jnp.pad(w[:, idx], ((0, 0), (0, pad))).astype(jnp.bfloat16))
        bs.append(jnp.pad(b[idx], (0, pad))[None, :])
    return ws, bs


def _out_proj_kernel(x_ref, y_ref, gate_ref, w_ref, o_ref):
    z = jnp.dot(y_ref[...].astype(jnp.bfloat16), w_ref[...], preferred_element_type=jnp.float32)
    o_ref[...] = x_ref[...] + gate_ref[0] * z


def out_projection(x2, y2, gate, w, tokens_per_batch):
    M, D = x2.shape
    K = y2.shape[1]
    tm = PROJ_TOKEN_TILE
    tiles_per_batch = tokens_per_batch // tm
    return pl.pallas_call(
        _out_proj_kernel,
        grid=(M // tm,),
        in_specs=[pl.BlockSpec((tm, D), lambda i: (i, 0)),
                  pl.BlockSpec((tm, K), lambda i: (i, 0)),
                  pl.BlockSpec((1, 1, D), lambda i: (i // tiles_per_batch, 0, 0)),
                  pl.BlockSpec((K, D), lambda i: (0, 0), pipeline_mode=pl.Buffered(1))],
        out_specs=pl.BlockSpec((tm, D), lambda i: (i, 0)),
        out_shape=jax.ShapeDtypeStruct((M, D), jnp.float32),
        compiler_params=pltpu.CompilerParams(dimension_semantics=("arbitrary",),
                                             vmem_limit_bytes=VMEM_LIMIT_BYTES),
        name="out_projection",
    )(x2, y2, gate, w)


def rms_norm(x, g):
    xf = x.astype(jnp.float32)
    y = xf * lax.rsqrt(jnp.mean(xf * xf, axis=-1, keepdims=True) + EPS)
    return (y * g).astype(x.dtype)


def l2_norm(x):
    xf = x.astype(jnp.float32)
    return (xf * lax.rsqrt(jnp.sum(xf * xf, axis=-1, keepdims=True) + EPS)).astype(x.dtype)


def causal_dwconv(x, w):
    K, C = w.shape
    return lax.conv_general_dilated(
        x, w[:, None, :], window_strides=(1,), padding=[(K - 1, 0)],
        dimension_numbers=("NWC", "WIO", "NWC"), feature_group_count=C)


def masked_softmax(logits, mask):
    l = jnp.where(mask, logits.astype(jnp.float32), -1e30)
    m = jnp.max(l, axis=-1, keepdims=True)
    e = jnp.where(mask, jnp.exp(l - m), 0.0)
    return e / jnp.maximum(jnp.sum(e, axis=-1, keepdims=True), 1e-30)


def t5_bucket(dist):
    n = jnp.maximum(dist, 0)
    max_exact = N_BUCKETS // 2
    nf = jnp.maximum(n, 1).astype(jnp.float32)
    large = max_exact + (jnp.log(nf / max_exact) / math.log(MAX_DISTANCE / max_exact)
                         * (N_BUCKETS - max_exact)).astype(jnp.int32)
    large = jnp.minimum(large, N_BUCKETS - 1)
    return jnp.where(n < max_exact, n, large)


NEG = -1e30
HIGHEST = lax.Precision.HIGHEST


def _nt_dot(a, b):
    return lax.dot_general(a, b, (((1,), (1,)), ((), ())), preferred_element_type=jnp.float32)


def _stack_heads(qn):
    lane = lax.broadcasted_iota(jnp.int32, (qn.shape[0], 128), 1)
    lo = lane < HEAD_DIM
    zero = jnp.zeros((), qn.dtype)
    halves = (qn[:, 0:128], qn[:, 128:256])
    return jnp.concatenate([jnp.where(lo if h % 2 == 0 else ~lo, halves[h // 2], zero)
                            for h in range(GROUP_HEADS)], axis=0)


def _nsa_prep_kernel(q_ref, kvs_ref, kvw_ref, qg_ref, kg_ref, seg_ref,
                     qn_ref, kxs_ref, v1s_ref, kkw_ref, v1w_ref):
    tm = q_ref.shape[0]
    q = q_ref[...]
    ss = jnp.dot(q * q, seg_ref[...], precision=HIGHEST, preferred_element_type=jnp.float32)
    qn_ref[...] = (q * lax.rsqrt(ss * (1.0 / HEAD_DIM) + EPS) * qg_ref[...]).astype(qn_ref.dtype)
    lane = lax.broadcasted_iota(jnp.int32, (tm, 128), 1)
    lo = lane < HEAD_DIM

    def split(x):
        xr = pltpu.roll(x, HEAD_DIM, 1)
        ss = jnp.sum(jnp.where(lo, x * x, 0.0), axis=-1, keepdims=True)
        kk = jnp.where(lo, x, xr) * lax.rsqrt(ss * (1.0 / HEAD_DIM) + EPS) * kg_ref[...]
        return kk, jnp.where(lo, xr, 1.0)

    kk, v1 = split(kvs_ref[...])
    tok = pl.program_id(1) * tm + lax.broadcasted_iota(jnp.int32, (tm, 128), 0)
    onehot = jnp.where(lane == tok // SLC_BLOCK, 1.0, 0.0)
    kxs_ref[...] = jnp.concatenate([kk, onehot], axis=1).astype(kxs_ref.dtype)
    v1s_ref[...] = v1.astype(v1s_ref.dtype)
    kk, v1 = split(kvw_ref[...])
    kkw_ref[...] = kk.astype(kkw_ref.dtype)
    v1w_ref[...] = v1.astype(v1w_ref.dtype)


def nsa_prep(pa, qg, kg2, seg):
    B, T, _ = pa.shape
    tm = 512
    bf16 = jnp.bfloat16
    col = lambda w, j: pl.BlockSpec((None, tm, w), lambda b, i: (b, i, j))
    const = lambda shape: pl.BlockSpec(shape, lambda b, i: (0, 0))
    return pl.pallas_call(
        _nsa_prep_kernel,
        grid=(B, T // tm),
        in_specs=[col(256, 0), col(128, 3), col(128, 4), const((1, 256)), const((1, 128)), const((256, 256))],
        out_specs=[col(256, 0), col(256, 0), col(128, 0), col(128, 0), col(128, 0)],
        out_shape=[jax.ShapeDtypeStruct((B, T, 256), bf16), jax.ShapeDtypeStruct((B, T, 256), bf16),
                   jax.ShapeDtypeStruct((B, T, 128), bf16), jax.ShapeDtypeStruct((B, T, 128), bf16),
                   jax.ShapeDtypeStruct((B, T, 128), bf16)],
        name="nsa_prep",
    )(pa, pa, pa, qg, kg2, seg)


def _compress_kernel(kch_ref, vch_ref, plo_ref, phi_ref, w1k_ref, w2k_ref, w1v_ref, w2v_ref, kg_ref,
                     kkc_ref, vvc_ref):
    nrow = kch_ref.shape[0]
    half = CMP_STRIDE * HEAD_DIM
    bf16 = jnp.bfloat16
    valid = lax.broadcasted_iota(jnp.int32, (nrow, 128), 0) < nrow - 1

    def comp(ch, w1_ref, w2_ref):
        a = jnp.dot((ch + plo_ref[...]).astype(bf16), w1_ref[0:half, :], preferred_element_type=jnp.float32)
        b = jnp.dot((ch + phi_ref[...]).astype(bf16), w1_ref[half:2 * half, :],
                    preferred_element_type=jnp.float32)
        hid = a + pltpu.roll(b, nrow - 1, 0)
        hid = hid * jax.nn.sigmoid(hid)
        return jnp.dot(hid.astype(bf16), w2_ref[...], preferred_element_type=jnp.float32)

    kc = comp(kch_ref[...], w1k_ref, w2k_ref)
    kc = kc * lax.rsqrt(jnp.mean(kc * kc, axis=-1, keepdims=True) + EPS) * kg_ref[...]
    kkc_ref[...] = jnp.where(valid, kc, 0.0).astype(kkc_ref.dtype)
    vc = comp(vch_ref[...], w1v_ref, w2v_ref)
    vvc_ref[...] = jnp.where(valid, vc, 0.0).astype(vvc_ref.dtype)


def nsa_compress(kch, vch, plo, phi, w1k, w2k, w1v, w2v, kg2):
    B, nrow, width = kch.shape
    bf16 = jnp.bfloat16
    full = lambda a: pl.BlockSpec(a.shape, lambda b: (0,) * a.ndim)
    bspec = pl.BlockSpec((None, nrow, width), lambda b: (b, 0, 0))
    ospec = pl.BlockSpec((None, nrow, 128), lambda b: (b, 0, 0))
    return pl.pallas_call(
        _compress_kernel,
        grid=(B,),
        in_specs=[bspec, bspec, full(plo), full(phi), full(w1k), full(w2k), full(w1v), full(w2v), full(kg2)],
        out_specs=[ospec, ospec],
        out_shape=[jax.ShapeDtypeStruct((B, nrow, 128), bf16), jax.ShapeDtypeStruct((B, nrow, 128), bf16)],
        name="nsa_compress",
    )(kch, vch, plo, phi, w1k, w2k, w1v, w2v, kg2)


def _cmp_select_kernel(qn_ref, kkc_ref, vvc_ref, cb_ref, ovt_ref, ocmp_ref, pen_ref, sc_ref, *, n_sel):
    i = pl.program_id(1)
    tq = qn_ref.shape[0]
    nblk = ovt_ref.shape[0]
    bf16 = jnp.bfloat16
    q4 = _stack_heads(qn_ref[...])
    s = _nt_dot(q4, kkc_ref[...]) + cb_ref[...]
    m = jnp.max(s, axis=-1, keepdims=True)
    e = jnp.where(s > 0.5 * NEG, jnp.exp(s - m), 0.0)
    p = e * (1.0 / jnp.maximum(jnp.sum(e, axis=-1, keepdims=True), 1e-30))
    o = jnp.dot(p.astype(bf16), vvc_ref[...], preferred_element_type=jnp.float32)
    lo = lax.broadcasted_iota(jnp.int32, (tq, 128), 1) < HEAD_DIM
    ocmp_ref[:, 0:128] = jnp.where(lo, o[0:tq], o[tq:2 * tq])
    ocmp_ref[:, 128:256] = jnp.where(lo, o[2 * tq:3 * tq], o[3 * tq:4 * tq])
    psum = p[0:tq] + p[tq:2 * tq] + p[2 * tq:3 * tq] + p[3 * tq:4 * tq]
    hi = psum.astype(bf16)
    lo_part = (psum - hi.astype(jnp.float32)).astype(bf16)
    score = _nt_dot(ovt_ref[...], hi) + _nt_dot(ovt_ref[...], lo_part)
    blk = lax.broadcasted_iota(jnp.int32, (nblk, tq), 0)
    qpos = i * tq + lax.broadcasted_iota(jnp.int32, (nblk, tq), 1)
    cur = qpos // SLC_BLOCK
    forced = (blk == 0) | ((cur - blk >= 0) & (cur - blk < N_LOCAL_SLC))
    score = jnp.where(forced, FORCE, score)
    score = jnp.where(blk <= cur, score, -FORCE)
    sc_ref[...] = score
    cnt = jnp.zeros((nblk, tq), jnp.float32)
    for jp in range(nblk):
        row = sc_ref[jp:jp + 1, :]
        tie = jnp.where(blk > jp, 1.0, 0.0)
        cnt = cnt + jnp.where(row > score, 1.0, jnp.where(row == score, tie, 0.0))
    pen = jnp.where(cnt < n_sel, 0.0, NEG)
    if nblk < 128:
        pen = jnp.concatenate([pen, jnp.zeros((128 - nblk, tq), jnp.float32)], axis=0)
    pen_ref[...] = pen.T.astype(pen_ref.dtype)


def nsa_cmp_select(qn, kkc, vvc, cbias, ovt, n_sel):
    B, T, _ = qn.shape
    nrow = kkc.shape[1]
    nblk = ovt.shape[0]
    tq = Q_BLOCK
    return pl.pallas_call(
        functools.partial(_cmp_select_kernel, n_sel=n_sel),
        grid=(B, T // tq),
        in_specs=[pl.BlockSpec((None, tq, 256), lambda b, i: (b, i, 0)),
                  pl.BlockSpec((None, nrow, 128), lambda b, i: (b, 0, 0)),
                  pl.BlockSpec((None, nrow, 128), lambda b, i: (b, 0, 0)),
                  pl.BlockSpec((None, 4 * tq, nrow), lambda b, i: (i, 0, 0)),
                  pl.BlockSpec((nblk, nrow), lambda b, i: (0, 0))],
        out_specs=[pl.BlockSpec((None, tq, 256), lambda b, i: (b, i, 0)),
                   pl.BlockSpec((None, tq, 128), lambda b, i: (b, i, 0))],
        out_shape=[jax.ShapeDtypeStruct((B, T, 256), jnp.float32),
                   jax.ShapeDtypeStruct((B, T, 128), jnp.bfloat16)],
        scratch_shapes=[pltpu.VMEM((nblk, tq), jnp.float32)],
        name="nsa_cmp_select",
    )(qn, kkc, vvc, cbias, ovt)


def _nsa_attn_kernel(qn_ref, pen_ref, ocmp_ref, gate_ref, kxs_ref, v1s_ref, kkw_ref, v1w_ref,
                     tabs_ref, tabw_ref, mixg_ref, o_ref, acc_ref, m_ref):
    i = pl.program_id(1)
    tq = qn_ref.shape[0]
    bf16 = jnp.bfloat16
    q4 = _stack_heads(qn_ref[...])
    pen = pen_ref[...]
    lhs_sel = jnp.concatenate([q4, jnp.concatenate([pen] * GROUP_HEADS, axis=0)], axis=1)

    def branch(lhs, k_ref, v_ref, tab_ref, first_tile):
        n_tab = tab_ref.shape[0]
        m_ref[...] = jnp.full(m_ref.shape, NEG, jnp.float32)
        acc_ref[...] = jnp.zeros(acc_ref.shape, jnp.float32)

        def body(j, carry):
            off = pl.multiple_of(j * tq, tq)
            s = _nt_dot(lhs, k_ref[pl.ds(off, tq), :]) + tab_ref[jnp.minimum(i - j, n_tab - 1)]
            m_old = m_ref[...]
            m_new = jnp.maximum(m_old, jnp.max(s, axis=-1, keepdims=True))
            p = jnp.exp(s - m_new)
            pv = jnp.dot(p.astype(bf16), v_ref[pl.ds(off, tq), :], preferred_element_type=jnp.float32)
            acc_ref[...] = jnp.exp(m_old - m_new) * acc_ref[...] + pv
            m_ref[...] = m_new
            return carry

        lax.fori_loop(first_tile, i + 1, body, 0)
        acc = acc_ref[...]
        return acc / jnp.maximum(pltpu.roll(acc, HEAD_DIM, 1), 1e-30)

    o_s = branch(lhs_sel, kxs_ref, v1s_ref, tabs_ref, 0)
    o_w = branch(q4, kkw_ref, v1w_ref, tabw_ref, jnp.maximum(i - WINDOW // tq, 0))
    g = jax.nn.sigmoid(gate_ref[...])
    lo = lax.broadcasted_iota(jnp.int32, (tq, 128), 1) < HEAD_DIM
    ocmp = ocmp_ref[...]
    halves = []
    for pair in range(2):
        comb, gc = [], []
        for h in (2 * pair, 2 * pair + 1):
            rows = slice(h * tq, (h + 1) * tq)
            comb.append(g[:, 3 * h + 1:3 * h + 2] * o_s[rows] + g[:, 3 * h + 2:3 * h + 3] * o_w[rows])
            gc.append(g[:, 3 * h:3 * h + 1])
        both = jnp.where(lo, comb[0], pltpu.roll(comb[1], HEAD_DIM, 1))
        halves.append(both + jnp.where(lo, gc[0], gc[1]) * ocmp[:, pair * 128:(pair + 1) * 128])
    y = jnp.concatenate(halves, axis=1)
    o_ref[...] = y * lax.rsqrt(jnp.mean(y * y, axis=-1, keepdims=True) + EPS) * mixg_ref[...]


def nsa_attention(qn, pen, ocmp, pa, kxs, v1s, kkw, v1w, tab_s, tab_w, mixg):
    B, T, _ = qn.shape
    tq = Q_BLOCK
    tile = lambda w, j: pl.BlockSpec((None, tq, w), lambda b, i: (b, i, j))
    whole = lambda w: pl.BlockSpec((None, T, w), lambda b, i: (b, 0, 0))
    full = lambda a: pl.BlockSpec(a.shape, lambda b, i: (0,) * a.ndim)
    return pl.pallas_call(
        _nsa_attn_kernel,
        grid=(B, T // tq),
        in_specs=[tile(256, 0), tile(128, 0), tile(256, 0), tile(128, 5),
                  whole(256), whole(128), whole(128), whole(128), full(tab_s), full(tab_w), full(mixg)],
        out_specs=tile(256, 0),
        out_shape=jax.ShapeDtypeStruct((B, T, 256), jnp.float32),
        scratch_shapes=[pltpu.VMEM((GROUP_HEADS * tq, 128), jnp.float32),
                        pltpu.VMEM((GROUP_HEADS * tq, 1), jnp.float32)],
        compiler_params=pltpu.CompilerParams(dimension_semantics=("arbitrary", "arbitrary"),
                                             vmem_limit_bytes=VMEM_LIMIT_BYTES),
        name="nsa_attention",
    )(qn, pen, ocmp, pa, kxs, v1s, kkw, v1w, tab_s, tab_w, mixg)


def nsa_bias_tables(t5_table, T):
    H = GROUP_HEADS
    tq = Q_BLOCK
    dmax = WINDOW + 2 * tq
    by_dist = t5_table[t5_bucket(jnp.arange(dmax))].T

    def tile(dist, keep):
        b = by_dist[:, jnp.clip(dist, 0, dmax - 1)]
        return jnp.where(keep, b, NEG).reshape(H * dist.shape[0], dist.shape[1])

    r = jnp.arange(tq)[:, None]
    c = jnp.arange(tq)[None, :]
    d = lambda delta: delta * tq + r - c
    tab_s = jnp.stack([tile(d(k), d(k) >= 0) for k in range(3)])
    tab_w = jnp.stack([tile(d(k), (d(k) >= 0) & (d(k) < WINDOW)) for k in range(WINDOW // tq + 1)])
    nrow = T // CMP_STRIDE
    per_tile = tq // CMP_STRIDE
    m = jnp.arange(2 * nrow)[None, :] - nrow
    dist = r - CMP_STRIDE * m - (CMP_BLOCK - 1)
    base = tile(dist, dist >= 0)
    cbias = jnp.stack([base[:, nrow - per_tile * i: 2 * nrow - per_tile * i] for i in range(T // tq)])
    return tab_s, tab_w, cbias


def nsa_overlap_t(T):
    nrow = T // CMP_STRIDE
    nblk = T // SLC_BLOCK
    ci = np.arange(nrow)[None, :]
    bj = np.arange(nblk)[:, None]
    ov = (ci * CMP_STRIDE < (bj + 1) * SLC_BLOCK) & (ci * CMP_STRIDE + CMP_BLOCK > bj * SLC_BLOCK)
    ov = ov & (ci < nrow - 1)
    return jnp.asarray(ov.astype(np.float32), dtype=jnp.bfloat16)


def nsa_mixer_pallas(pa, q_norm_g, k_norm_g, cmp_pos, cmp_k_w1, cmp_k_w2, cmp_v_w1, cmp_v_w2,
                     tables, ovt, mixg):
    B, T, _ = pa.shape
    bf16 = jnp.bfloat16
    tab_s, tab_w, cbias = tables
    qg = (jnp.tile(q_norm_g, GROUP_HEADS) * HEAD_DIM ** -0.5)[None, :]
    kg2 = jnp.tile(k_norm_g, 2)[None, :]
    seg = jnp.asarray(np.kron(np.eye(GROUP_HEADS), np.ones((HEAD_DIM, HEAD_DIM))), jnp.float32)
    qn, kxs, v1s, kkw, v1w = nsa_prep(pa, qg, kg2, seg)
    nrow = T // CMP_STRIDE
    kch = pa[:, :, 256:320].reshape(B, nrow, CMP_STRIDE * HEAD_DIM)
    vch = pa[:, :, 320:384].reshape(B, nrow, CMP_STRIDE * HEAD_DIM)
    plo = cmp_pos[:CMP_STRIDE].reshape(1, -1)
    phi = cmp_pos[CMP_STRIDE:].reshape(1, -1)
    dup = lambda w: jnp.concatenate([w, w], axis=1).astype(bf16)
    kkc, vvc = nsa_compress(kch, vch, plo, phi, cmp_k_w1.astype(bf16), dup(cmp_k_w2),
                            cmp_v_w1.astype(bf16), dup(cmp_v_w2), kg2)
    n_sel = min(N_SLC, T // SLC_BLOCK)
    ocmp, pen = nsa_cmp_select(qn, kkc, vvc, cbias, ovt, n_sel)
    return nsa_attention(qn, pen, ocmp, pa, kxs, v1s, kkw, v1w, tab_s, tab_w, mixg)


def nsa_mixer_jnp_unused(q, k_cmp, v_cmp, k_slc, v_slc, k_win, v_win, gate_raw, q_norm_g, k_norm_g,
              cmp_pos, cmp_k_w1, cmp_k_w2, cmp_v_w1, cmp_v_w2, t5_table):
    B, T = q.shape[0], q.shape[1]
    dt = q.dtype
    H, dh = GROUP_HEADS, HEAD_DIM
    scale = dh ** -0.5
    q = rms_norm(q.reshape(B, T, H, dh), q_norm_g) * scale
    k_slc = rms_norm(k_slc, k_norm_g)
    k_win = rms_norm(k_win, k_norm_g)
    t = jnp.arange(T)
    nc = T // CMP_STRIDE - 1

    def compress(kv, w1, w2):
        ch = kv.reshape(B, T // CMP_STRIDE, CMP_STRIDE, dh)
        blocks = jnp.concatenate([ch[:, :-1], ch[:, 1:]], axis=2) + cmp_pos
        hid = jax.nn.silu(blocks.reshape(B, nc, CMP_BLOCK * dh) @ w1)
        return hid @ w2

    kc = rms_norm(compress(k_cmp, cmp_k_w1, cmp_k_w2), k_norm_g)
    vc = compress(v_cmp, cmp_v_w1, cmp_v_w2)
    cmp_end = jnp.arange(nc) * CMP_STRIDE + CMP_BLOCK - 1
    cmp_dist = t[:, None] - cmp_end[None, :]
    cmp_bias = t5_table[t5_bucket(cmp_dist)].transpose(2, 0, 1)
    cmp_logits = jnp.einsum("bthd,bnd->bhtn", q, kc) + cmp_bias
    p_cmp = masked_softmax(cmp_logits, cmp_dist >= 0)
    o_cmp = jnp.einsum("bhtn,bnd->bthd", p_cmp.astype(dt), vc)

    nblk = T // SLC_BLOCK
    n_sel = min(N_SLC, nblk)
    ci = np.arange(nc)[:, None]
    bj = np.arange(nblk)[None, :]
    overlap = ((ci * CMP_STRIDE < (bj + 1) * SLC_BLOCK)
               & (ci * CMP_STRIDE + CMP_BLOCK > bj * SLC_BLOCK)).astype(np.float32)
    score = jnp.sum(p_cmp, axis=1) @ jnp.asarray(overlap)
    cur = (t // SLC_BLOCK)[:, None]
    blk = jnp.arange(nblk)[None, :]
    forced = (blk == 0) | ((cur - blk >= 0) & (cur - blk < N_LOCAL_SLC))
    score = jnp.where(forced, FORCE, score)
    score = jnp.where(blk <= cur, score, -FORCE)
    _, sel_idx = lax.top_k(score, n_sel)

    ks_blocks = k_slc.reshape(B, nblk, SLC_BLOCK, dh)
    vs_blocks = v_slc.reshape(B, nblk, SLC_BLOCK, dh)
    kw_p = jnp.pad(k_win, ((0, 0), (WINDOW, 0), (0, 0)))
    vw_p = jnp.pad(v_win, ((0, 0), (WINDOW, 0), (0, 0)))
    nq = T // Q_BLOCK
    J = n_sel * SLC_BLOCK

    def qblock(i):
        s = i * Q_BLOCK
        qb = lax.dynamic_slice_in_dim(q, s, Q_BLOCK, axis=1)
        tq = s + jnp.arange(Q_BLOCK)
        kwb = lax.dynamic_slice_in_dim(kw_p, s, WINDOW + Q_BLOCK, axis=1)
        vwb = lax.dynamic_slice_in_dim(vw_p, s, WINDOW + Q_BLOCK, axis=1)
        kpos = s - WINDOW + jnp.arange(WINDOW + Q_BLOCK)
        wd = tq[:, None] - kpos[None, :]
        wmask = (wd >= 0) & (wd < WINDOW) & (kpos[None, :] >= 0)
        wl = jnp.einsum("bqhd,bkd->bhqk", qb, kwb) + t5_table[t5_bucket(wd)].transpose(2, 0, 1)
        o_w = jnp.einsum("bhqk,bkd->bqhd", masked_softmax(wl, wmask).astype(dt), vwb)
        idx = lax.dynamic_slice_in_dim(sel_idx, s, Q_BLOCK, axis=1)
        ksel = jax.vmap(lambda kb, ib: kb[ib])(ks_blocks, idx)
        vsel = jax.vmap(lambda vb, ib: vb[ib])(vs_blocks, idx)
        spos = idx[..., None] * SLC_BLOCK + jnp.arange(SLC_BLOCK)
        sd = (tq[None, :, None, None] - spos).reshape(B, Q_BLOCK, J)
        sl = jnp.einsum("bqhd,bqnsd->bhqns", qb, ksel).reshape(B, H, Q_BLOCK, J)
        sl = sl + t5_table[t5_bucket(sd)].transpose(0, 3, 1, 2)
        ps = masked_softmax(sl, (sd >= 0)[:, None]).astype(dt)
        o_s = jnp.einsum("bhqj,bqjd->bqhd", ps, vsel.reshape(B, Q_BLOCK, J, dh))
        return o_w, o_s

    o_win, o_slc = lax.map(qblock, jnp.arange(nq))
    o_win = o_win.transpose(1, 0, 2, 3, 4).reshape(B, T, H, dh)
    o_slc = o_slc.transpose(1, 0, 2, 3, 4).reshape(B, T, H, dh)
    g = jax.nn.sigmoid(gate_raw.reshape(B, T, H, 3))
    o = g[..., 0:1] * o_cmp + g[..., 1:2] * o_slc + g[..., 2:3] * o_win
    return o.reshape(B, T, GROUP_WIDTH)


def short_conv_mixer(b_gate, c_gate, xs, conv_w):
    return b_gate * causal_dwconv(c_gate * xs, conv_w)


def to_chunks(a, L):
    B, T, H = a.shape[:3]
    a = a.reshape((B, T // L, L, H) + a.shape[3:])
    return a.transpose((1, 0, 3, 2) + tuple(range(4, a.ndim)))


def from_chunks(a):
    N, B, H, L, d = a.shape
    return a.transpose(1, 0, 3, 2, 4).reshape(B, N * L, H, d)


def chunk_gated_delta_rule(q, k, v, g, beta):
    f32 = jnp.float32
    L = GDN_CHUNK
    q, k, v = (to_chunks(a.astype(f32), L) for a in (q, k, v))
    g = to_chunks(g.astype(f32), L)
    beta = to_chunks(beta.astype(f32), L)
    gc = jnp.cumsum(g, axis=-1)
    incl = jnp.tril(jnp.ones((L, L), bool))
    strict = jnp.tril(jnp.ones((L, L), bool), -1)
    seg = jnp.exp(jnp.where(incl, gc[..., :, None] - gc[..., None, :], -jnp.inf))
    kb = k * beta[..., None]
    a_mat = jnp.where(strict, jnp.einsum("nbhid,nbhjd->nbhij", kb, k) * seg, 0.0) + jnp.eye(L, dtype=f32)
    rhs = jnp.concatenate([v * beta[..., None], kb * jnp.exp(gc)[..., None]], axis=-1)
    sol = lax.linalg.triangular_solve(a_mat, rhs, left_side=True, lower=True, unit_diagonal=True)
    u, w = sol[..., :HEAD_DIM], sol[..., HEAD_DIM:]
    qk = jnp.einsum("nbhid,nbhjd->nbhij", q, k) * seg
    q_dec = q * jnp.exp(gc)[..., None]
    k_dec = k * jnp.exp(gc[..., -1:] - gc)[..., None]
    g_tot = jnp.exp(gc[..., -1])

    def step(S, inp):
        u_c, w_c, qd_c, kd_c, qk_c, gt_c = inp
        v_new = u_c - jnp.einsum("bhlk,bhkv->bhlv", w_c, S)
        o = jnp.einsum("bhlk,bhkv->bhlv", qd_c, S) + jnp.einsum("bhij,bhjv->bhiv", qk_c, v_new)
        S = S * gt_c[..., None, None] + jnp.einsum("bhlk,bhlv->bhkv", kd_c, v_new)
        return S, o

    S0 = jnp.zeros(u.shape[1:3] + (HEAD_DIM, HEAD_DIM), f32)
    _, o = lax.scan(step, S0, (u, w, q_dec, k_dec, qk, g_tot))
    return from_chunks(o)


def gated_deltanet(q, k, v, beta_raw, alpha_raw, z, conv_w, A_log, dt_bias, norm_g):
    B, T = q.shape[0], q.shape[1]
    qkv = jax.nn.silu(causal_dwconv(jnp.concatenate([q, k, v], axis=-1), conv_w))
    q, k, v = (a.reshape(B, T, GROUP_HEADS, HEAD_DIM) for a in jnp.split(qkv, 3, axis=-1))
    q = l2_norm(q) * HEAD_DIM ** -0.5
    k = l2_norm(k)
    beta = jax.nn.sigmoid(beta_raw)
    g = -jnp.exp(A_log) * jax.nn.softplus(alpha_raw + dt_bias)
    o = chunk_gated_delta_rule(q, k, v, g, beta).astype(z.dtype)
    o = rms_norm(o, norm_g) * jax.nn.silu(z.reshape(B, T, GROUP_HEADS, HEAD_DIM))
    return o.reshape(B, T, GROUP_WIDTH)


def mlstm_chunkwise(q, k, v, i_pre, f_pre):
    f32 = jnp.float32
    L = MLSTM_CHUNK
    q, k, v = (to_chunks(a.astype(f32), L) for a in (q, k, v))
    log_i = to_chunks(i_pre.astype(f32), L)
    log_f = to_chunks(jax.nn.log_sigmoid(f_pre.astype(f32)), L)
    b = jnp.cumsum(log_f, axis=-1)
    incl = jnp.tril(jnp.ones((L, L), bool))
    log_w = jnp.where(incl, b[..., :, None] - b[..., None, :] + log_i[..., None, :], -jnp.inf)
    m_intra = jnp.max(log_w, axis=-1)
    log_w_end = b[..., -1:] - b + log_i
    m_end = jnp.max(log_w_end, axis=-1)
    qk = jnp.einsum("nbhid,nbhjd->nbhij", q, k)

    def step(carry, inp):
        C, n, m = carry
        q_c, k_c, v_c, b_c, lw_c, mi_c, lwe_c, me_c, qk_c = inp
        log_inter = b_c + m[..., None]
        m_t = jnp.maximum(log_inter, mi_c)
        w_inter = jnp.exp(log_inter - m_t)
        s = qk_c * jnp.exp(lw_c - m_t[..., None])
        num = (w_inter[..., None] * jnp.einsum("bhlk,bhkv->bhlv", q_c, C)
               + jnp.einsum("bhij,bhjv->bhiv", s, v_c))
        den = w_inter * jnp.einsum("bhlk,bhk->bhl", q_c, n) + jnp.sum(s, axis=-1)
        h = num / jnp.maximum(jnp.abs(den), jnp.exp(-m_t))[..., None]
        m_new = jnp.maximum(b_c[..., -1] + m, me_c)
        w_old = jnp.exp(b_c[..., -1] + m - m_new)
        w_new = jnp.exp(lwe_c - m_new[..., None])
        C = w_old[..., None, None] * C + jnp.einsum("bhl,bhlk,bhlv->bhkv", w_new, k_c, v_c)
        n = w_old[..., None] * n + jnp.einsum("bhl,bhlk->bhk", w_new, k_c)
        return (C, n, m_new), h

    Bsz, H = q.shape[1], q.shape[2]
    carry0 = (jnp.zeros((Bsz, H, HEAD_DIM, HEAD_DIM), f32), jnp.zeros((Bsz, H, HEAD_DIM), f32),
              jnp.zeros((Bsz, H), f32))
    _, h = lax.scan(step, carry0, (q, k, v, b, log_w, m_intra, log_w_end, m_end, qk))
    return from_chunks(h)


def mlstm_mixer(q, k, v, i_raw, f_raw, o_raw, f_bias, norm_g):
    B, T = q.shape[0], q.shape[1]
    heads = lambda a: a.reshape(B, T, GROUP_HEADS, HEAD_DIM)
    h = mlstm_chunkwise(heads(q), heads(k) * HEAD_DIM ** -0.5, heads(v), i_raw, f_raw + f_bias)
    h = rms_norm(h.astype(q.dtype), norm_g) * jax.nn.sigmoid(heads(o_raw))
    return h.reshape(B, T, GROUP_WIDTH)


CONV_CARRY_ROWS = 8


def _shifted(cat, shift, rows):
    return pltpu.roll(cat, shift, 0)[CONV_CARRY_ROWS:CONV_CARRY_ROWS + rows]


def _sconv_kernel(pb_ref, w_ref, g_ref, o_ref, carry_ref):
    rows = pb_ref.shape[0]

    @pl.when(pl.program_id(1) == 0)
    def _():
        carry_ref[...] = jnp.zeros(carry_ref.shape, jnp.float32)

    u = pb_ref[:, 256:512] * pb_ref[:, 512:768]
    cat = jnp.concatenate([carry_ref[...], u], axis=0)
    conv = u * w_ref[2:3, :] + _shifted(cat, 1, rows) * w_ref[1:2, :] + _shifted(cat, 2, rows) * w_ref[0:1, :]
    carry_ref[...] = u[rows - CONV_CARRY_ROWS:rows]
    y = pb_ref[:, 0:256] * conv
    o_ref[...] = y * lax.rsqrt(jnp.mean(y * y, axis=-1, keepdims=True) + EPS) * g_ref[...]


def short_conv_pallas(pb, conv_w, mixg):
    B, T, _ = pb.shape
    rows = 512
    w = jnp.pad(conv_w, ((0, 8 - conv_w.shape[0]), (0, 0)))
    return pl.pallas_call(
        _sconv_kernel,
        grid=(B, T // rows),
        in_specs=[pl.BlockSpec((None, rows, 768), lambda b, t: (b, t, 0)),
                  pl.BlockSpec((8, 256), lambda b, t: (0, 0)),
                  pl.BlockSpec((1, 256), lambda b, t: (0, 0))],
        out_specs=pl.BlockSpec((None, rows, 256), lambda b, t: (b, t, 0)),
        out_shape=jax.ShapeDtypeStruct((B, T, 256), jnp.float32),
        scratch_shapes=[pltpu.VMEM((CONV_CARRY_ROWS, 256), jnp.float32)],
        compiler_params=pltpu.CompilerParams(dimension_semantics=("arbitrary", "arbitrary")),
        name="short_conv",
    )(pb, w, mixg)


def _stack(a, hm):
    return jnp.concatenate([a] * GROUP_HEADS, axis=0) * hm


def _unstack(y, L):
    return y[0:L] + y[L:2 * L] + y[2 * L:3 * L] + y[3 * L:4 * L]


def _tn_dot(a, b):
    return lax.dot_general(a, b, (((0,), (0,)), ((), ())), preferred_element_type=jnp.float32)


def _dot_hi(a, b):
    return jnp.dot(a, b, precision=HIGHEST, preferred_element_type=jnp.float32)


def _softplus(x):
    return jnp.maximum(x, 0.0) + jnp.log(1.0 + jnp.exp(-jnp.abs(x)))


def _block_masks(L):
    n = GROUP_HEADS * L
    r = lax.broadcasted_iota(jnp.int32, (n, n), 0)
    c = lax.broadcasted_iota(jnp.int32, (n, n), 1)
    same = (r // L) == (c // L)
    return same & (c <= r), same & (c < r)


def _head_norm(o, hm):
    return o * lax.rsqrt(_dot_hi(o * o, hm) * (1.0 / HEAD_DIM) + EPS)


def _head_l2(o, hm):
    return o * lax.rsqrt(_dot_hi(o * o, hm) + EPS)


def _gdn_kernel(pc_ref, cw_ref, hp_ref, hm_ref, tri_ref, eb_ref, ea_ref, o_ref, s_ref, carry_ref):
    L = pc_ref.shape[0]
    f32, bf16 = jnp.float32, jnp.bfloat16

    @pl.when(pl.program_id(1) == 0)
    def _():
        s_ref[...] = jnp.zeros(s_ref.shape, f32)
        carry_ref[...] = jnp.zeros(carry_ref.shape, f32)

    hm = hm_ref[...]
    x = pc_ref[:, 0:768]
    cat = jnp.concatenate([carry_ref[...], x], axis=0)
    conv = x * cw_ref[3:4, :]
    for s in (1, 2, 3):
        conv = conv + _shifted(cat, s, L) * cw_ref[3 - s:4 - s, :]
    carry_ref[...] = x[L - CONV_CARRY_ROWS:L]
    qkv = conv * jax.nn.sigmoid(conv)
    q = _head_l2(qkv[:, 0:256], hm) * HEAD_DIM ** -0.5
    k = _head_l2(qkv[:, 256:512], hm)
    v = qkv[:, 512:768]
    tail = pc_ref[:, 1024:1152]
    beta = jax.nn.sigmoid(_dot_hi(tail, eb_ref[...]))
    g = hp_ref[0:1, :] * _softplus(_dot_hi(tail, ea_ref[...]) + hp_ref[1:2, :])
    gc = _dot_hi(tri_ref[...], g)
    egc = jnp.exp(gc)
    kb = k * beta
    ks, kbs, qs, gs = (_stack(a, hm) for a in (k, kb, q, gc))
    avg = jnp.full((GROUP_HEADS * L, GROUP_HEADS * L), 1.0 / HEAD_DIM, f32)
    gcol = _dot_hi(gs, avg)
    grow = lax.dot_general(avg, gs, (((1,), (1,)), ((), ())), precision=HIGHEST,
                           preferred_element_type=f32)
    incl, strict = _block_masks(L)
    seg = jnp.exp(jnp.where(incl, gcol - grow, NEG))
    ksb = ks.astype(bf16)
    a_mat = jnp.where(strict, _nt_dot(kbs.astype(bf16), ksb) * seg, 0.0)
    qk = _nt_dot(qs.astype(bf16), ksb) * seg
    xs = jnp.concatenate([_stack(v * beta, hm), _stack(kb * egc, hm)], axis=1)
    xs = xs - _dot_hi(a_mat, xs)
    pw = a_mat
    for _ in range(int(math.log2(L)) - 1):
        pw = _dot_hi(pw, pw)
        xs = xs + _dot_hi(pw, xs)
    u = _unstack(xs[:, 0:256], L)
    w = _unstack(xs[:, 256:512], L)
    state = s_ref[...]
    sb = state.astype(bf16)
    v_new = u - jnp.dot(w.astype(bf16), sb, preferred_element_type=f32)
    o = jnp.dot((q * egc).astype(bf16), sb, preferred_element_type=f32)
    o = o + _unstack(jnp.dot(qk.astype(bf16), _stack(v_new, hm).astype(bf16), preferred_element_type=f32), L)
    k_dec = k * jnp.exp(gc[L - 1:L, :] - gc)
    s_ref[...] = state * egc[L - 1:L, :] + _tn_dot(k_dec.astype(bf16), v_new.astype(bf16)) * hm
    z = pc_ref[:, 768:1024]
    o_ref[...] = _head_norm(o, hm) * hp_ref[2:3, :] * (z * jax.nn.sigmoid(z))


def _expand_matrix(first_lane):
    e = np.zeros((128, GROUP_WIDTH), np.float32)
    for h in range(GROUP_HEADS):
        e[first_lane + h, h * HEAD_DIM:(h + 1) * HEAD_DIM] = 1.0
    return jnp.asarray(e)


def _head_mask():
    return jnp.asarray(np.kron(np.eye(GROUP_HEADS), np.ones((HEAD_DIM, HEAD_DIM))), jnp.float32)


def _per_head_rows(*vecs):
    rows = [jnp.repeat(v, HEAD_DIM) if v.shape[0] == GROUP_HEADS else jnp.tile(v, GROUP_HEADS) for v in vecs]
    return jnp.pad(jnp.stack(rows), ((0, 8 - len(rows)), (0, 0)))


def gated_deltanet_pallas(pc, conv_w, A_log, dt_bias, norm_g):
    B, T, width = pc.shape
    L = GDN_CHUNK
    n = GROUP_HEADS * L
    hp = _per_head_rows(-jnp.exp(A_log), dt_bias, norm_g)
    cw = jnp.pad(conv_w, ((0, 8 - conv_w.shape[0]), (0, 0)))
    tri = jnp.asarray(np.tril(np.ones((L, L), np.float32)))
    const = lambda a: pl.BlockSpec(a.shape, lambda b, t: (0, 0))
    consts = (cw, hp, _head_mask(), tri, _expand_matrix(0), _expand_matrix(GROUP_HEADS))
    return pl.pallas_call(
        _gdn_kernel,
        grid=(B, T // L),
        in_specs=[pl.BlockSpec((None, L, width), lambda b, t: (b, t, 0))] + [const(a) for a in consts],
        out_specs=pl.BlockSpec((None, L, 256), lambda b, t: (b, t, 0)),
        out_shape=jax.ShapeDtypeStruct((B, T, 256), jnp.float32),
        scratch_shapes=[pltpu.VMEM((n, n), jnp.float32), pltpu.VMEM((CONV_CARRY_ROWS, 768), jnp.float32)],
        compiler_params=pltpu.CompilerParams(dimension_semantics=("arbitrary", "arbitrary")),
        name="gated_deltanet",
    )(pc, *consts)


def _mlstm_kernel(pd_ref, hp_ref, hm_ref, tri_ref, ei_ref, ef_ref, o_ref, c_ref, n_ref, m_ref):
    L = pd_ref.shape[0]
    f32, bf16 = jnp.float32, jnp.bfloat16

    @pl.when(pl.program_id(1) == 0)
    def _():
        c_ref[...] = jnp.zeros(c_ref.shape, f32)
        n_ref[...] = jnp.zeros(n_ref.shape, f32)
        m_ref[...] = jnp.zeros(m_ref.shape, f32)

    hm = hm_ref[...]
    q = pd_ref[:, 0:256]
    k = pd_ref[:, 256:512] * HEAD_DIM ** -0.5
    v = pd_ref[:, 512:768]
    tail = pd_ref[:, 1024:1152]
    log_i = _dot_hi(tail, ei_ref[...])
    log_f = -_softplus(-(_dot_hi(tail, ef_ref[...]) + hp_ref[0:1, :]))
    b = _dot_hi(tri_ref[...], log_f)
    avg = jnp.full((GROUP_HEADS * L, GROUP_HEADS * L), 1.0 / HEAD_DIM, f32)
    bcol = _dot_hi(_stack(b, hm), avg)
    arow = lax.dot_general(avg, _stack(b - log_i, hm), (((1,), (1,)), ((), ())), precision=HIGHEST,
                           preferred_element_type=f32)
    incl, _ = _block_masks(L)
    log_w = jnp.where(incl, bcol - arow, NEG)

    def to_x(col):
        return _unstack(col * hm, L)

    m_intra = to_x(jnp.max(log_w, axis=-1, keepdims=True))
    b_last = b[L - 1:L, :]
    log_w_end = b_last - b + log_i
    m_end = jnp.max(log_w_end, axis=0, keepdims=True)
    m_prev = m_ref[...]
    log_inter = b + m_prev
    m_t = jnp.maximum(log_inter, m_intra)
    w_inter = jnp.exp(log_inter - m_t)
    qk = _nt_dot(_stack(q, hm).astype(bf16), _stack(k, hm).astype(bf16))
    s = qk * jnp.exp(log_w - _dot_hi(_stack(m_t, hm), avg))
    cb = c_ref[...].astype(bf16)
    num = w_inter * jnp.dot(q.astype(bf16), cb, preferred_element_type=f32)
    num = num + _unstack(jnp.dot(s.astype(bf16), _stack(v, hm).astype(bf16), preferred_element_type=f32), L)
    den = w_inter * _dot_hi(q * n_ref[...], hm) + to_x(jnp.sum(s, axis=-1, keepdims=True))
    h = num / jnp.maximum(jnp.abs(den), jnp.exp(-m_t))
    m_new = jnp.maximum(b_last + m_prev, m_end)
    w_old = jnp.exp(b_last + m_prev - m_new)
    kw = k * jnp.exp(log_w_end - m_new)
    c_ref[...] = w_old * c_ref[...] + _tn_dot(kw.astype(bf16), v.astype(bf16)) * hm
    n_ref[...] = w_old * n_ref[...] + jnp.sum(kw, axis=0, keepdims=True)
    m_ref[...] = m_new
    o_ref[...] = _head_norm(h, hm) * hp_ref[1:2, :] * jax.nn.sigmoid(pd_ref[:, 768:1024])


def mlstm_pallas(pd, f_bias, norm_g):
    B, T, width = pd.shape
    L = MLSTM_CHUNK
    n = GROUP_HEADS * L
    hp = _per_head_rows(f_bias, norm_g)
    tri = jnp.asarray(np.tril(np.ones((L, L), np.float32)))
    const = lambda a: pl.BlockSpec(a.shape, lambda b, t: (0, 0))
    consts = (hp, _head_mask(), tri, _expand_matrix(0), _expand_matrix(GROUP_HEADS))
    return pl.pallas_call(
        _mlstm_kernel,
        grid=(B, T // L),
        in_specs=[pl.BlockSpec((None, L, width), lambda b, t: (b, t, 0))] + [const(a) for a in consts],
        out_specs=pl.BlockSpec((None, L, 256), lambda b, t: (b, t, 0)),
        out_shape=jax.ShapeDtypeStruct((B, T, 256), jnp.float32),
        scratch_shapes=[pltpu.VMEM((n, n), jnp.float32), pltpu.VMEM((1, 256), jnp.float32),
                        pltpu.VMEM((1, 256), jnp.float32)],
        compiler_params=pltpu.CompilerParams(dimension_semantics=("arbitrary", "arbitrary")),
        name="mlstm",
    )(pd, *consts)


def kernel(x, c, ada_w, ada_b, norm_g, ffn1_w13, ffn1_w2, ffn2_w13, ffn2_w2, w_in, b_in, q_norm_g, k_norm_g, cmp_pos, cmp_k_w1, cmp_k_w2, cmp_v_w1, cmp_v_w2, t5_table, sc_conv_w, gdn_conv_w, gdn_A_log, gdn_dt_bias, gdn_norm_g, mlstm_f_bias, mlstm_norm_g, mix_norm_g, w_out):
    B, T, D = x.shape
    depth = ada_w.shape[0]
    bf16 = jnp.bfloat16
    x2 = x.reshape(B * T, D)
    tables = nsa_bias_tables(t5_table, T)
    ovt = nsa_overlap_t(T)
    for l in range(depth):
        mod = ada_modulation(c, ada_w[l].astype(bf16), ada_b[l][None, :])
        mod = mod.reshape(B, N_SUBLAYERS, 3, 1, D)
        w13 = ffn1_w13[l].astype(bf16)
        x2 = ffn_half_step(x2, norm_g[l, 0][None, :], mod[:, 0, 0], mod[:, 0, 1], mod[:, 0, 2],
                           w13[:, :D_FF], w13[:, D_FF:], ffn1_w2[l].astype(bf16), T)
        ws, bs = split_in_weights(w_in[l], b_in[l])
        pa, pb, pc, pd = in_projection(x2, norm_g[l, 1][None, :], mod[:, 1, 0], mod[:, 1, 1], ws, bs, T)
        pa, pb, pc, pd = (a.reshape(B, T, -1) for a in (pa, pb, pc, pd))
        y_a = nsa_mixer_pallas(pa, q_norm_g[l], k_norm_g[l], cmp_pos[l], cmp_k_w1[l], cmp_k_w2[l],
                               cmp_v_w1[l], cmp_v_w2[l], tables, ovt, mix_norm_g[l, 0][None, :])
        y_b = short_conv_pallas(pb, sc_conv_w[l], mix_norm_g[l, 1][None, :])
        y_c = gated_deltanet_pallas(pc, gdn_conv_w[l], gdn_A_log[l], gdn_dt_bias[l], gdn_norm_g[l])
        y_d = mlstm_pallas(pd, mlstm_f_bias[l], mlstm_norm_g[l])
        y = jnp.concatenate([y_a, y_b, y_c, y_d], axis=-1)
        x2 = out_projection(x2, y.reshape(B * T, -1), mod[:, 1, 2], w_out[l].astype(bf16), T)
        w13 = ffn2_w13[l].astype(bf16)
        x2 = ffn_half_step(x2, norm_g[l, 2][None, :], mod[:, 2, 0], mod[:, 2, 1], mod[:, 2, 2],
                           w13[:, :D_FF], w13[:, D_FF:], ffn2_w2[l].astype(bf16), T)
    return x2.reshape(B, T, D)
```

```python
import functools
import math

import jax
import jax.numpy as jnp
import numpy as np
from jax import lax
from jax.experimental import pallas as pl
from jax.experimental.pallas import tpu as pltpu

D_MODEL = 1024
HEAD_DIM = 64
GROUP_HEADS = 4
GROUP_WIDTH = GROUP_HEADS * HEAD_DIM
CMP_STRIDE = 16
CMP_BLOCK = 32
SLC_BLOCK = 64
N_SLC = 16
N_LOCAL_SLC = 2
WINDOW = 512
Q_BLOCK = 128
FORCE = 1e6
N_BUCKETS = 32
MAX_DISTANCE = 128
GDN_CHUNK = 64
MLSTM_CHUNK = 64
D_FF = 2816
N_SUBLAYERS = 3
EPS = 1e-6

IN_LAYOUT = (
    ("a_q", 256), ("a_k_cmp", 64), ("a_v_cmp", 64),
    ("a_k_slc", 64), ("a_v_slc", 64), ("a_k_win", 64), ("a_v_win", 64),
    ("a_gate", 12),
    ("b_b", 256), ("b_c", 256), ("b_x", 256),
    ("c_q", 256), ("c_k", 256), ("c_v", 256),
    ("c_beta", 4), ("c_alpha", 4), ("c_z", 256),
    ("d_q", 256), ("d_k", 256), ("d_v", 256),
    ("d_i", 4), ("d_f", 4), ("d_o", 256),
)

VMEM_LIMIT_BYTES = 56 * 1024 * 1024
FFN_TOKEN_TILE = 512
FFN_CHUNK = 256
PROJ_TOKEN_TILE = 512


def _modulated_norm(x, g, scale, shift):
    y = x * lax.rsqrt(jnp.mean(x * x, axis=-1, keepdims=True) + EPS)
    return (y * g) * (1.0 + scale) + shift


def _ada_kernel(c_ref, w_ref, b_ref, o_ref):
    c = c_ref[...]
    cond = c * jax.nn.sigmoid(c)
    o_ref[...] = jnp.dot(cond.astype(jnp.bfloat16), w_ref[...],
                         preferred_element_type=jnp.float32) + b_ref[...]


def ada_modulation(c, w, b):
    B, D = c.shape
    N = w.shape[1]
    tn = 1152
    return pl.pallas_call(
        _ada_kernel,
        grid=(N // tn,),
        in_specs=[pl.BlockSpec((B, D), lambda j: (0, 0)),
                  pl.BlockSpec((D, tn), lambda j: (0, j)),
                  pl.BlockSpec((1, tn), lambda j: (0, j))],
        out_specs=pl.BlockSpec((B, tn), lambda j: (0, j)),
        out_shape=jax.ShapeDtypeStruct((B, N), jnp.float32),
        name="ada_modulation",
    )(c, w, b)


def _ffn_kernel(x_ref, g_ref, shift_ref, scale_ref, gate_ref, w1_ref, w3_ref, w2_ref, o_ref, act_ref):
    x = x_ref[...]
    h = _modulated_norm(x, g_ref[...], scale_ref[0], shift_ref[0]).astype(jnp.bfloat16)
    n_chunks = act_ref.shape[1] // FFN_CHUNK
    for ci in range(n_chunks):
        cs = slice(ci * FFN_CHUNK, (ci + 1) * FFN_CHUNK)
        a = jnp.dot(h, w1_ref[:, cs], preferred_element_type=jnp.float32)
        b = jnp.dot(h, w3_ref[:, cs], preferred_element_type=jnp.float32)
        act_ref[:, cs] = (a * jax.nn.sigmoid(a) * b).astype(jnp.bfloat16)
    y = jnp.dot(act_ref[...], w2_ref[...], preferred_element_type=jnp.float32)
    o_ref[...] = x + (0.5 * gate_ref[0]) * y


def ffn_half_step(x2, g, shift, scale, gate, w1, w3, w2, tokens_per_batch):
    M, D = x2.shape
    F = w2.shape[0]
    tm = FFN_TOKEN_TILE
    tiles_per_batch = tokens_per_batch // tm
    resident = dict(pipeline_mode=pl.Buffered(1))
    mod_spec = pl.BlockSpec((1, 1, D), lambda i: (i // tiles_per_batch, 0, 0))
    return pl.pallas_call(
        _ffn_kernel,
        grid=(M // tm,),
        in_specs=[pl.BlockSpec((tm, D), lambda i: (i, 0)),
                  pl.BlockSpec((1, D), lambda i: (0, 0)),
                  mod_spec, mod_spec, mod_spec,
                  pl.BlockSpec((D, F), lambda i: (0, 0), **resident),
                  pl.BlockSpec((D, F), lambda i: (0, 0), **resident),
                  pl.BlockSpec((F, D), lambda i: (0, 0), **resident)],
        out_specs=pl.BlockSpec((tm, D), lambda i: (i, 0)),
        out_shape=jax.ShapeDtypeStruct((M, D), jnp.float32),
        scratch_shapes=[pltpu.VMEM((tm, F), jnp.bfloat16)],
        compiler_params=pltpu.CompilerParams(dimension_semantics=("arbitrary",),
                                             vmem_limit_bytes=VMEM_LIMIT_BYTES),
        name="ffn_half_step",
    )(x2, g, shift, scale, gate, w1, w3, w2)


def _in_proj_kernel(x_ref, g_ref, shift_ref, scale_ref, wa_ref, wb_ref, wc_ref, wd_ref,
                    ba_ref, bb_ref, bc_ref, bd_ref, oa_ref, ob_ref, oc_ref, od_ref):
    h = _modulated_norm(x_ref[...], g_ref[...], scale_ref[0], shift_ref[0]).astype(jnp.bfloat16)
    for w_ref, b_ref, o_ref in ((wa_ref, ba_ref, oa_ref), (wb_ref, bb_ref, ob_ref),
                                (wc_ref, bc_ref, oc_ref), (wd_ref, bd_ref, od_ref)):
        o_ref[...] = jnp.dot(h, w_ref[...], preferred_element_type=jnp.float32) + b_ref[...]


def in_projection(x2, g, shift, scale, ws, bs, tokens_per_batch):
    M, D = x2.shape
    tm = PROJ_TOKEN_TILE
    tiles_per_batch = tokens_per_batch // tm
    mod_spec = pl.BlockSpec((1, 1, D), lambda i: (i // tiles_per_batch, 0, 0))
    const = lambda a, **kw: pl.BlockSpec(a.shape, lambda i: (0, 0), **kw)
    return pl.pallas_call(
        _in_proj_kernel,
        grid=(M // tm,),
        in_specs=[pl.BlockSpec((tm, D), lambda i: (i, 0)), const(g), mod_spec, mod_spec]
                 + [const(w, pipeline_mode=pl.Buffered(1)) for w in ws] + [const(b) for b in bs],
        out_specs=[pl.BlockSpec((tm, w.shape[1]), lambda i: (i, 0)) for w in ws],
        out_shape=[jax.ShapeDtypeStruct((M, w.shape[1]), jnp.float32) for w in ws],
        compiler_params=pltpu.CompilerParams(dimension_semantics=("arbitrary",),
                                             vmem_limit_bytes=VMEM_LIMIT_BYTES),
        name="in_projection",
    )(x2, g, shift, scale, *ws, *bs)


def _group_columns():
    offs, o = {}, 0
    for n, s in IN_LAYOUT:
        offs[n] = np.arange(o, o + s)
        o += s
    cat = lambda names: np.concatenate([offs[n] for n in names])
    return (cat(["a_q", "a_k_cmp", "a_v_cmp", "a_k_slc", "a_v_slc", "a_k_win", "a_v_win", "a_gate"]),
            cat(["b_b", "b_c", "b_x"]),
            cat(["c_q", "c_k", "c_v", "c_z", "c_beta", "c_alpha"]),
            cat(["d_q", "d_k", "d_v", "d_o", "d_i", "d_f"]))


GROUP_SLAB_WIDTH = (768, 768, 1152, 1152)


def split_in_weights(w, b):
    ws, bs = [], []
    for idx, width in zip(_group_columns(), GROUP_SLAB_WIDTH):
        pad = width - idx.size
        ws.append(jnp.pad(w[:, idx], ((0, 0), (0, pad))).astype(jnp.bfloat16))
        bs.append(jnp.pad(b[idx], (0, pad))[None, :])
    return ws, bs


def _out_proj_kernel(x_ref, y_ref, gate_ref, w_ref, o_ref):
    z = jnp.dot(y_ref[...].astype(jnp.bfloat16), w_ref[...], preferred_element_type=jnp.float32)
    o_ref[...] = x_ref[...] + gate_ref[0] * z


def out_projection(x2, y2, gate, w, tokens_per_batch):
    M, D = x2.shape
    K = y2.shape[1]
    tm = PROJ_TOKEN_TILE
    tiles_per_batch = tokens_per_batch // tm
    return pl.pallas_call(
        _out_proj_kernel,
        grid=(M // tm,),
        in_specs=[pl.BlockSpec((tm, D), lambda i: (i, 0)),
                  pl.BlockSpec((tm, K), lambda i: (i, 0)),
                  pl.BlockSpec((1, 1, D), lambda i: (i // tiles_per_batch, 0, 0)),
                  pl.BlockSpec((K, D), lambda i: (0, 0), pipeline_mode=pl.Buffered(1))],
        out_specs=pl.BlockSpec((tm, D), lambda i: (i, 0)),
        out_shape=jax.ShapeDtypeStruct((M, D), jnp.float32),
        compiler_params=pltpu.CompilerParams(dimension_semantics=("arbitrary",),
                                             vmem_limit_bytes=VMEM_LIMIT_BYTES),
        name="out_projection",
    )(x2, y2, gate, w)


def rms_norm(x, g):
    xf = x.astype(jnp.float32)
    y = xf * lax.rsqrt(jnp.mean(xf * xf, axis=-1, keepdims=True) + EPS)
    return (y * g).astype(x.dtype)


def l2_norm(x):
    xf = x.astype(jnp.float32)
    return (xf * lax.rsqrt(jnp.sum(xf * xf, axis=-1, keepdims=True) + EPS)).astype(x.dtype)


def causal_dwconv(x, w):
    K, C = w.shape
    return lax.conv_general_dilated(
        x, w[:, None, :], window_strides=(1,), padding=[(K - 1, 0)],
        dimension_numbers=("NWC", "WIO", "NWC"), feature_group_count=C)


def masked_softmax(logits, mask):
    l = jnp.where(mask, logits.astype(jnp.float32), -1e30)
    m = jnp.max(l, axis=-1, keepdims=True)
    e = jnp.where(mask, jnp.exp(l - m), 0.0)
    return e / jnp.maximum(jnp.sum(e, axis=-1, keepdims=True), 1e-30)


def t5_bucket(dist):
    n = jnp.maximum(dist, 0)
    max_exact = N_BUCKETS // 2
    nf = jnp.maximum(n, 1).astype(jnp.float32)
    large = max_exact + (jnp.log(nf / max_exact) / math.log(MAX_DISTANCE / max_exact)
                         * (N_BUCKETS - max_exact)).astype(jnp.int32)
    large = jnp.minimum(large, N_BUCKETS - 1)
    return jnp.where(n < max_exact, n, large)


NEG = -1e30
HIGHEST = lax.Precision.HIGHEST


def _nt_dot(a, b):
    return lax.dot_general(a, b, (((1,), (1,)), ((), ())), preferred_element_type=jnp.float32)


def _stack_heads(qn):
    lane = lax.broadcasted_iota(jnp.int32, (qn.shape[0], 128), 1)
    lo = lane < HEAD_DIM
    zero = jnp.zeros((), qn.dtype)
    halves = (qn[:, 0:128], qn[:, 128:256])
    return jnp.concatenate([jnp.where(lo if h % 2 == 0 else ~lo, halves[h // 2], zero)
                            for h in range(GROUP_HEADS)], axis=0)


def _nsa_prep_kernel(q_ref, kvs_ref, kvw_ref, qg_ref, kg_ref, seg_ref,
                     qn_ref, kxs_ref, v1s_ref, kkw_ref, v1w_ref):
    tm = q_ref.shape[0]
    q = q_ref[...]
    ss = jnp.dot(q * q, seg_ref[...], precision=HIGHEST, preferred_element_type=jnp.float32)
    qn_ref[...] = (q * lax.rsqrt(ss * (1.0 / HEAD_DIM) + EPS) * qg_ref[...]).astype(qn_ref.dtype)
    lane = lax.broadcasted_iota(jnp.int32, (tm, 128), 1)
    lo = lane < HEAD_DIM

    def split(x):
        xr = pltpu.roll(x, HEAD_DIM, 1)
        ss = jnp.sum(jnp.where(lo, x * x, 0.0), axis=-1, keepdims=True)
        kk = jnp.where(lo, x, xr) * lax.rsqrt(ss * (1.0 / HEAD_DIM) + EPS) * kg_ref[...]
        return kk, jnp.where(lo, xr, 1.0)

    kk, v1 = split(kvs_ref[...])
    tok = pl.program_id(1) * tm + lax.broadcasted_iota(jnp.int32, (tm, 128), 0)
    onehot = jnp.where(lane == tok // SLC_BLOCK, 1.0, 0.0)
    kxs_ref[...] = jnp.concatenate([kk, onehot], axis=1).astype(kxs_ref.dtype)
    v1s_ref[...] = v1.astype(v1s_ref.dtype)
    kk, v1 = split(kvw_ref[...])
    kkw_ref[...] = kk.astype(kkw_ref.dtype)
    v1w_ref[...] = v1.astype(v1w_ref.dtype)


def nsa_prep(pa, qg, kg2, seg):
    B, T, _ = pa.shape
    tm = 512
    bf16 = jnp.bfloat16
    col = lambda w, j: pl.BlockSpec((None, tm, w), lambda b, i: (b, i, j))
    const = lambda shape: pl.BlockSpec(shape, lambda b, i: (0, 0))
    return pl.pallas_call(
        _nsa_prep_kernel,
        grid=(B, T // tm),
        in_specs=[col(256, 0), col(128, 3), col(128, 4), const((1, 256)), const((1, 128)), const((256, 256))],
        out_specs=[col(256, 0), col(256, 0), col(128, 0), col(128, 0), col(128, 0)],
        out_shape=[jax.ShapeDtypeStruct((B, T, 256), bf16), jax.ShapeDtypeStruct((B, T, 256), bf16),
                   jax.ShapeDtypeStruct((B, T, 128), bf16), jax.ShapeDtypeStruct((B, T, 128), bf16),
                   jax.ShapeDtypeStruct((B, T, 128), bf16)],
        name="nsa_prep",
    )(pa, pa, pa, qg, kg2, seg)


def _compress_kernel(kch_ref, vch_ref, plo_ref, phi_ref, w1k_ref, w2k_ref, w1v_ref, w2v_ref, kg_ref,
                     kkc_ref, vvc_ref):
    nrow = kch_ref.shape[0]
    half = CMP_STRIDE * HEAD_DIM
    bf16 = jnp.bfloat16
    valid = lax.broadcasted_iota(jnp.int32, (nrow, 128), 0) < nrow - 1

    def comp(ch, w1_ref, w2_ref):
        a = jnp.dot((ch + plo_ref[...]).astype(bf16), w1_ref[0:half, :], preferred_element_type=jnp.float32)
        b = jnp.dot((ch + phi_ref[...]).astype(bf16), w1_ref[half:2 * half, :],
                    preferred_element_type=jnp.float32)
        hid = a + pltpu.roll(b, nrow - 1, 0)
        hid = hid * jax.nn.sigmoid(hid)
        return jnp.dot(hid.astype(bf16), w2_ref[...], preferred_element_type=jnp.float32)

    kc = comp(kch_ref[...], w1k_ref, w2k_ref)
    kc = kc * lax.rsqrt(jnp.mean(kc * kc, axis=-1, keepdims=True) + EPS) * kg_ref[...]
    kkc_ref[...] = jnp.where(valid, kc, 0.0).astype(kkc_ref.dtype)
    vc = comp(vch_ref[...], w1v_ref, w2v_ref)
    vvc_ref[...] = jnp.where(valid, vc, 0.0).astype(vvc_ref.dtype)


def nsa_compress(kch, vch, plo, phi, w1k, w2k, w1v, w2v, kg2):
    B, nrow, width = kch.shape
    bf16 = jnp.bfloat16
    full = lambda a: pl.BlockSpec(a.shape, lambda b: (0,) * a.ndim)
    bspec = pl.BlockSpec((None, nrow, width), lambda b: (b, 0, 0))
    ospec = pl.BlockSpec((None, nrow, 128), lambda b: (b, 0, 0))
    return pl.pallas_call(
        _compress_kernel,
        grid=(B,),
        in_specs=[bspec, bspec, full(plo), full(phi), full(w1k), full(w2k), full(w1v), full(w2v), full(kg2)],
        out_specs=[ospec, ospec],
        out_shape=[jax.ShapeDtypeStruct((B, nrow, 128), bf16), jax.ShapeDtypeStruct((B, nrow, 128), bf16)],
        name="nsa_compress",
    )(kch, vch, plo, phi, w1k, w2k, w1v, w2v, kg2)


def _cmp_select_kernel(qn_ref, kkc_ref, vvc_ref, cb_ref, ovt_ref, ocmp_ref, pen_ref, sc_ref, *, n_sel):
    i = pl.program_id(1)
    tq = qn_ref.shape[0]
    nblk = ovt_ref.shape[0]
    bf16 = jnp.bfloat16
    q4 = _stack_heads(qn_ref[...])
    s = _nt_dot(q4, kkc_ref[...]) + cb_ref[...]
    m = jnp.max(s, axis=-1, keepdims=True)
    e = jnp.where(s > 0.5 * NEG, jnp.exp(s - m), 0.0)
    p = e * (1.0 / jnp.maximum(jnp.sum(e, axis=-1, keepdims=True), 1e-30))
    o = jnp.dot(p.astype(bf16), vvc_ref[...], preferred_element_type=jnp.float32)
    lo = lax.broadcasted_iota(jnp.int32, (tq, 128), 1) < HEAD_DIM
    ocmp_ref[:, 0:128] = jnp.where(lo, o[0:tq], o[tq:2 * tq])
    ocmp_ref[:, 128:256] = jnp.where(lo, o[2 * tq:3 * tq], o[3 * tq:4 * tq])
    psum = p[0:tq] + p[tq:2 * tq] + p[2 * tq:3 * tq] + p[3 * tq:4 * tq]
    hi = psum.astype(bf16)
    lo_part = (psum - hi.astype(jnp.float32)).astype(bf16)
    score = _nt_dot(ovt_ref[...], hi) + _nt_dot(ovt_ref[...], lo_part)
    blk = lax.broadcasted_iota(jnp.int32, (nblk, tq), 0)
    qpos = i * tq + lax.broadcasted_iota(jnp.int32, (nblk, tq), 1)
    cur = qpos // SLC_BLOCK
    forced = (blk == 0) | ((cur - blk >= 0) & (cur - blk < N_LOCAL_SLC))
    score = jnp.where(forced, FORCE, score)
    score = jnp.where(blk <= cur, score, -FORCE)
    sc_ref[...] = score
    cnt = jnp.zeros((nblk, tq), jnp.float32)
    for jp in range(nblk):
        row = sc_ref[jp:jp + 1, :]
        tie = jnp.where(blk > jp, 1.0, 0.0)
        cnt = cnt + jnp.where(row > score, 1.0, jnp.where(row == score, tie, 0.0))
    pen = jnp.where(cnt < n_sel, 0.0, NEG)
    if nblk < 128:
        pen = jnp.concatenate([pen, jnp.zeros((128 - nblk, tq), jnp.float32)], axis=0)
    pen_ref[...] = pen.T.astype(pen_ref.dtype)


def nsa_cmp_select(qn, kkc, vvc, cbias, ovt, n_sel):
    B, T, _ = qn.shape
    nrow = kkc.shape[1]
    nblk = ovt.shape[0]
    tq = Q_BLOCK
    return pl.pallas_call(
        functools.partial(_cmp_select_kernel, n_sel=n_sel),
        grid=(B, T // tq),
        in_specs=[pl.BlockSpec((None, tq, 256), lambda b, i: (b, i, 0)),
                  pl.BlockSpec((None, nrow, 128), lambda b, i: (b, 0, 0)),
                  pl.BlockSpec((None, nrow, 128), lambda b, i: (b, 0, 0)),
                  pl.BlockSpec((None, 4 * tq, nrow), lambda b, i: (i, 0, 0)),
                  pl.BlockSpec((nblk, nrow), lambda b, i: (0, 0))],
        out_specs=[pl.BlockSpec((None, tq, 256), lambda b, i: (b, i, 0)),
                   pl.BlockSpec((None, tq, 128), lambda b, i: (b, i, 0))],
        out_shape=[jax.ShapeDtypeStruct((B, T, 256), jnp.float32),
                   jax.ShapeDtypeStruct((B, T, 128), jnp.bfloat16)],
        scratch_shapes=[pltpu.VMEM((nblk, tq), jnp.float32)],
        name="nsa_cmp_select",
    )(qn, kkc, vvc, cbias, ovt)


def _nsa_attn_kernel(qn_ref, pen_ref, ocmp_ref, gate_ref, kxs_ref, v1s_ref, kkw_ref, v1w_ref,
                     tabs_ref, tabw_ref, mixg_ref, o_ref, acc_ref, m_ref):
    i = pl.program_id(1)
    tq = qn_ref.shape[0]
    bf16 = jnp.bfloat16
    q4 = _stack_heads(qn_ref[...])
    pen = pen_ref[...]
    lhs_sel = jnp.concatenate([q4, jnp.concatenate([pen] * GROUP_HEADS, axis=0)], axis=1)

    def branch(lhs, k_ref, v_ref, tab_ref, first_tile):
        n_tab = tab_ref.shape[0]
        m_ref[...] = jnp.full(m_ref.shape, NEG, jnp.float32)
        acc_ref[...] = jnp.zeros(acc_ref.shape, jnp.float32)

        def body(j, carry):
            off = pl.multiple_of(j * tq, tq)
            s = _nt_dot(lhs, k_ref[pl.ds(off, tq), :]) + tab_ref[jnp.minimum(i - j, n_tab - 1)]
            m_old = m_ref[...]
            m_new = jnp.maximum(m_old, jnp.max(s, axis=-1, keepdims=True))
            p = jnp.exp(s - m_new)
            pv = jnp.dot(p.astype(bf16), v_ref[pl.ds(off, tq), :], preferred_element_type=jnp.float32)
            acc_ref[...] = jnp.exp(m_old - m_new) * acc_ref[...] + pv
            m_ref[...] = m_new
            return carry

        lax.fori_loop(first_tile, i + 1, body, 0)
        acc = acc_ref[...]
        return acc / jnp.maximum(pltpu.roll(acc, HEAD_DIM, 1), 1e-30)

    o_s = branch(lhs_sel, kxs_ref, v1s_ref, tabs_ref, 0)
    o_w = branch(q4, kkw_ref, v1w_ref, tabw_ref, jnp.maximum(i - WINDOW // tq, 0))
    g = jax.nn.sigmoid(gate_ref[...])
    lo = lax.broadcasted_iota(jnp.int32, (tq, 128), 1) < HEAD_DIM
    ocmp = ocmp_ref[...]
    halves = []
    for pair in range(2):
        comb, gc = [], []
        for h in (2 * pair, 2 * pair + 1):
            rows = slice(h * tq, (h + 1) * tq)
            comb.append(g[:, 3 * h + 1:3 * h + 2] * o_s[rows] + g[:, 3 * h + 2:3 * h + 3] * o_w[rows])
            gc.append(g[:, 3 * h:3 * h + 1])
        both = jnp.where(lo, comb[0], pltpu.roll(comb[1], HEAD_DIM, 1))
        halves.append(both + jnp.where(lo, gc[0], gc[1]) * ocmp[:, pair * 128:(pair + 1) * 128])
    y = jnp.concatenate(halves, axis=1)
    o_ref[...] = y * lax.rsqrt(jnp.mean(y * y, axis=-1, keepdims=True) + EPS) * mixg_ref[...]


def nsa_attention(qn, pen, ocmp, pa, kxs, v1s, kkw, v1w, tab_s, tab_w, mixg):
    B, T, _ = qn.shape
    tq = Q_BLOCK
    tile = lambda w, j: pl.BlockSpec((None, tq, w), lambda b, i: (b, i, j))
    whole = lambda w: pl.BlockSpec((None, T, w), lambda b, i: (b, 0, 0))
    full = lambda a: pl.BlockSpec(a.shape, lambda b, i: (0,) * a.ndim)
    return pl.pallas_call(
        _nsa_attn_kernel,
        grid=(B, T // tq),
        in_specs=[tile(256, 0), tile(128, 0), tile(256, 0), tile(128, 5),
                  whole(256), whole(128), whole(128), whole(128), full(tab_s), full(tab_w), full(mixg)],
        out_specs=tile(256, 0),
        out_shape=jax.ShapeDtypeStruct((B, T, 256), jnp.float32),
        scratch_shapes=[pltpu.VMEM((GROUP_HEADS * tq, 128), jnp.float32),
                        pltpu.VMEM((GROUP_HEADS * tq, 1), jnp.float32)],
        compiler_params=pltpu.CompilerParams(dimension_semantics=("arbitrary", "arbitrary"),
                                             vmem_limit_bytes=VMEM_LIMIT_BYTES),
        name="nsa_attention",
    )(qn, pen, ocmp, pa, kxs, v1s, kkw, v1w, tab_s, tab_w, mixg)


def nsa_bias_tables(t5_table, T):
    H = GROUP_HEADS
    tq = Q_BLOCK
    dmax = WINDOW + 2 * tq
    by_dist = t5_table[t5_bucket(jnp.arange(dmax))].T

    def tile(dist, keep):
        b = by_dist[:, jnp.clip(dist, 0, dmax - 1)]
        return jnp.where(keep, b, NEG).reshape(H * dist.shape[0], dist.shape[1])

    r = jnp.arange(tq)[:, None]
    c = jnp.arange(tq)[None, :]
    d = lambda delta: delta * tq + r - c
    tab_s = jnp.stack([tile(d(k), d(k) >= 0) for k in range(3)])
    tab_w = jnp.stack([tile(d(k), (d(k) >= 0) & (d(k) < WINDOW)) for k in range(WINDOW // tq + 1)])
    nrow = T // CMP_STRIDE
    per_tile = tq // CMP_STRIDE
    m = jnp.arange(2 * nrow)[None, :] - nrow
    dist = r - CMP_STRIDE * m - (CMP_BLOCK - 1)
    base = tile(dist, dist >= 0)
    cbias = jnp.stack([base[:, nrow - per_tile * i: 2 * nrow - per_tile * i] for i in range(T // tq)])
    return tab_s, tab_w, cbias


def nsa_overlap_t(T):
    nrow = T // CMP_STRIDE
    nblk = T // SLC_BLOCK
    ci = np.arange(nrow)[None, :]
    bj = np.arange(nblk)[:, None]
    ov = (ci * CMP_STRIDE < (bj + 1) * SLC_BLOCK) & (ci * CMP_STRIDE + CMP_BLOCK > bj * SLC_BLOCK)
    ov = ov & (ci < nrow - 1)
    return jnp.asarray(ov.astype(np.float32), dtype=jnp.bfloat16)


def nsa_mixer_pallas(pa, q_norm_g, k_norm_g, cmp_pos, cmp_k_w1, cmp_k_w2, cmp_v_w1, cmp_v_w2,
                     tables, ovt, mixg):
    B, T, _ = pa.shape
    bf16 = jnp.bfloat16
    tab_s, tab_w, cbias = tables
    qg = (jnp.tile(q_norm_g, GROUP_HEADS) * HEAD_DIM ** -0.5)[None, :]
    kg2 = jnp.tile(k_norm_g, 2)[None, :]
    seg = jnp.asarray(np.kron(np.eye(GROUP_HEADS), np.ones((HEAD_DIM, HEAD_DIM))), jnp.float32)
    qn, kxs, v1s, kkw, v1w = nsa_prep(pa, qg, kg2, seg)
    nrow = T // CMP_STRIDE
    kch = pa[:, :, 256:320].reshape(B, nrow, CMP_STRIDE * HEAD_DIM)
    vch = pa[:, :, 320:384].reshape(B, nrow, CMP_STRIDE * HEAD_DIM)
    plo = cmp_pos[:CMP_STRIDE].reshape(1, -1)
    phi = cmp_pos[CMP_STRIDE:].reshape(1, -1)
    dup = lambda w: jnp.concatenate([w, w], axis=1).astype(bf16)
    kkc, vvc = nsa_compress(kch, vch, plo, phi, cmp_k_w1.astype(bf16), dup(cmp_k_w2),
                            cmp_v_w1.astype(bf16), dup(cmp_v_w2), kg2)
    n_sel = min(N_SLC, T // SLC_BLOCK)
    ocmp, pen = nsa_cmp_select(qn, kkc, vvc, cbias, ovt, n_sel)
    return nsa_attention(qn, pen, ocmp, pa, kxs, v1s, kkw, v1w, tab_s, tab_w, mixg)


def nsa_mixer_jnp_unused(q, k_cmp, v_cmp, k_slc, v_slc, k_win, v_win, gate_raw, q_norm_g, k_norm_g,
              cmp_pos, cmp_k_w1, cmp_k_w2, cmp_v_w1, cmp_v_w2, t5_table):
    B, T = q.shape[0], q.shape[1]
    dt = q.dtype
    H, dh = GROUP_HEADS, HEAD_DIM
    scale = dh ** -0.5
    q = rms_norm(q.reshape(B, T, H, dh), q_norm_g) * scale
    k_slc = rms_norm(k_slc, k_norm_g)
    k_win = rms_norm(k_win, k_norm_g)
    t = jnp.arange(T)
    nc = T // CMP_STRIDE - 1

    def compress(kv, w1, w2):
        ch = kv.reshape(B, T // CMP_STRIDE, CMP_STRIDE, dh)
        blocks = jnp.concatenate([ch[:, :-1], ch[:, 1:]], axis=2) + cmp_pos
        hid = jax.nn.silu(blocks.reshape(B, nc, CMP_BLOCK * dh) @ w1)
        return hid @ w2

    kc = rms_norm(compress(k_cmp, cmp_k_w1, cmp_k_w2), k_norm_g)
    vc = compress(v_cmp, cmp_v_w1, cmp_v_w2)
    cmp_end = jnp.arange(nc) * CMP_STRIDE + CMP_BLOCK - 1
    cmp_dist = t[:, None] - cmp_end[None, :]
    cmp_bias = t5_table[t5_bucket(cmp_dist)].transpose(2, 0, 1)
    cmp_logits = jnp.einsum("bthd,bnd->bhtn", q, kc) + cmp_bias
    p_cmp = masked_softmax(cmp_logits, cmp_dist >= 0)
    o_cmp = jnp.einsum("bhtn,bnd->bthd", p_cmp.astype(dt), vc)

    nblk = T // SLC_BLOCK
    n_sel = min(N_SLC, nblk)
    ci = np.arange(nc)[:, None]
    bj = np.arange(nblk)[None, :]
    overlap = ((ci * CMP_STRIDE < (bj + 1) * SLC_BLOCK)
               & (ci * CMP_STRIDE + CMP_BLOCK > bj * SLC_BLOCK)).astype(np.float32)
    score = jnp.sum(p_cmp, axis=1) @ jnp.asarray(overlap)
    cur = (t // SLC_BLOCK)[:, None]
    blk = jnp.arange(nblk)[None, :]
    forced = (blk == 0) | ((cur - blk >= 0) & (cur - blk < N_LOCAL_SLC))
    score = jnp.where(forced, FORCE, score)
    score = jnp.where(blk <= cur, score, -FORCE)
    _, sel_idx = lax.top_k(score, n_sel)

    ks_blocks = k_slc.reshape(B, nblk, SLC_BLOCK, dh)
    vs_blocks = v_slc.reshape(B, nblk, SLC_BLOCK, dh)
    kw_p = jnp.pad(k_win, ((0, 0), (WINDOW, 0), (0, 0)))
    vw_p = jnp.pad(v_win, ((0, 0), (WINDOW, 0), (0, 0)))
    nq = T // Q_BLOCK
    J = n_sel * SLC_BLOCK

    def qblock(i):
        s = i * Q_BLOCK
        qb = lax.dynamic_slice_in_dim(q, s, Q_BLOCK, axis=1)
        tq = s + jnp.arange(Q_BLOCK)
        kwb = lax.dynamic_slice_in_dim(kw_p, s, WINDOW + Q_BLOCK, axis=1)
        vwb = lax.dynamic_slice_in_dim(vw_p, s, WINDOW + Q_BLOCK, axis=1)
        kpos = s - WINDOW + jnp.arange(WINDOW + Q_BLOCK)
        wd = tq[:, None] - kpos[None, :]
        wmask = (wd >= 0) & (wd < WINDOW) & (kpos[None, :] >= 0)
        wl = jnp.einsum("bqhd,bkd->bhqk", qb, kwb) + t5_table[t5_bucket(wd)].transpose(2, 0, 1)
        o_w = jnp.einsum("bhqk,bkd->bqhd", masked_softmax(wl, wmask).astype(dt), vwb)
        idx = lax.dynamic_slice_in_dim(sel_idx, s, Q_BLOCK, axis=1)
        ksel = jax.vmap(lambda kb, ib: kb[ib])(ks_blocks, idx)
        vsel = jax.vmap(lambda vb, ib: vb[ib])(vs_blocks, idx)
        spos = idx[..., None] * SLC_BLOCK + jnp.arange(SLC_BLOCK)
        sd = (tq[None, :, None, None] - spos).reshape(B, Q_BLOCK, J)
        sl = jnp.einsum("bqhd,bqnsd->bhqns", qb, ksel).reshape(B, H, Q_BLOCK, J)
        sl = sl + t5_table[t5_bucket(sd)].transpose(0, 3, 1, 2)
        ps = masked_softmax(sl, (sd >= 0)[:, None]).astype(dt)
        o_s = jnp.einsum("bhqj,bqjd->bqhd", ps, vsel.reshape(B, Q_BLOCK, J, dh))
        return o_w, o_s

    o_win, o_slc = lax.map(qblock, jnp.arange(nq))
    o_win = o_win.transpose(1, 0, 2, 3, 4).reshape(B, T, H, dh)
    o_slc = o_slc.transpose(1, 0, 2, 3, 4).reshape(B, T, H, dh)
    g = jax.nn.sigmoid(gate_raw.reshape(B, T, H, 3))
    o = g[..., 0:1] * o_cmp + g[..., 1:2] * o_slc + g[..., 2:3] * o_win
    return o.reshape(B, T, GROUP_WIDTH)


def short_conv_mixer(b_gate, c_gate, xs, conv_w):
    return b_gate * causal_dwconv(c_gate * xs, conv_w)


def to_chunks(a, L):
    B, T, H = a.shape[:3]
    a = a.reshape((B, T // L, L, H) + a.shape[3:])
    return a.transpose((1, 0, 3, 2) + tuple(range(4, a.ndim)))


def from_chunks(a):
    N, B, H, L, d = a.shape
    return a.transpose(1, 0, 3, 2, 4).reshape(B, N * L, H, d)


def chunk_gated_delta_rule(q, k, v, g, beta):
    f32 = jnp.float32
    L = GDN_CHUNK
    q, k, v = (to_chunks(a.astype(f32), L) for a in (q, k, v))
    g = to_chunks(g.astype(f32), L)
    beta = to_chunks(beta.astype(f32), L)
    gc = jnp.cumsum(g, axis=-1)
    incl = jnp.tril(jnp.ones((L, L), bool))
    strict = jnp.tril(jnp.ones((L, L), bool), -1)
    seg = jnp.exp(jnp.where(incl, gc[..., :, None] - gc[..., None, :], -jnp.inf))
    kb = k * beta[..., None]
    a_mat = jnp.where(strict, jnp.einsum("nbhid,nbhjd->nbhij", kb, k) * seg, 0.0) + jnp.eye(L, dtype=f32)
    rhs = jnp.concatenate([v * beta[..., None], kb * jnp.exp(gc)[..., None]], axis=-1)
    sol = lax.linalg.triangular_solve(a_mat, rhs, left_side=True, lower=True, unit_diagonal=True)
    u, w = sol[..., :HEAD_DIM], sol[..., HEAD_DIM:]
    qk = jnp.einsum("nbhid,nbhjd->nbhij", q, k) * seg
    q_dec = q * jnp.exp(gc)[..., None]
    k_dec = k * jnp.exp(gc[..., -1:] - gc)[..., None]
    g_tot = jnp.exp(gc[..., -1])

    def step(S, inp):
        u_c, w_c, qd_c, kd_c, qk_c, gt_c = inp
        v_new = u_c - jnp.einsum("bhlk,bhkv->bhlv", w_c, S)
        o = jnp.einsum("bhlk,bhkv->bhlv", qd_c, S) + jnp.einsum("bhij,bhjv->bhiv", qk_c, v_new)
        S = S * gt_c[..., None, None] + jnp.einsum("bhlk,bhlv->bhkv", kd_c, v_new)
        return S, o

    S0 = jnp.zeros(u.shape[1:3] + (HEAD_DIM, HEAD_DIM), f32)
    _, o = lax.scan(step, S0, (u, w, q_dec, k_dec, qk, g_tot))
    return from_chunks(o)


def gated_deltanet(q, k, v, beta_raw, alpha_raw, z, conv_w, A_log, dt_bias, norm_g):
    B, T = q.shape[0], q.shape[1]
    qkv = jax.nn.silu(causal_dwconv(jnp.concatenate([q, k, v], axis=-1), conv_w))
    q, k, v = (a.reshape(B, T, GROUP_HEADS, HEAD_DIM) for a in jnp.split(qkv, 3, axis=-1))
    q = l2_norm(q) * HEAD_DIM ** -0.5
    k = l2_norm(k)
    beta = jax.nn.sigmoid(beta_raw)
    g = -jnp.exp(A_log) * jax.nn.softplus(alpha_raw + dt_bias)
    o = chunk_gated_delta_rule(q, k, v, g, beta).astype(z.dtype)
    o = rms_norm(o, norm_g) * jax.nn.silu(z.reshape(B, T, GROUP_HEADS, HEAD_DIM))
    return o.reshape(B, T, GROUP_WIDTH)


def mlstm_chunkwise(q, k, v, i_pre, f_pre):
    f32 = jnp.float32
    L = MLSTM_CHUNK
    q, k, v = (to_chunks(a.astype(f32), L) for a in (q, k, v))
    log_i = to_chunks(i_pre.astype(f32), L)
    log_f = to_chunks(jax.nn.log_sigmoid(f_pre.astype(f32)), L)
    b = jnp.cumsum(log_f, axis=-1)
    incl = jnp.tril(jnp.ones((L, L), bool))
    log_w = jnp.where(incl, b[..., :, None] - b[..., None, :] + log_i[..., None, :], -jnp.inf)
    m_intra = jnp.max(log_w, axis=-1)
    log_w_end = b[..., -1:] - b + log_i
    m_end = jnp.max(log_w_end, axis=-1)
    qk = jnp.einsum("nbhid,nbhjd->nbhij", q, k)

    def step(carry, inp):
        C, n, m = carry
        q_c, k_c, v_c, b_c, lw_c, mi_c, lwe_c, me_c, qk_c = inp
        log_inter = b_c + m[..., None]
        m_t = jnp.maximum(log_inter, mi_c)
        w_inter = jnp.exp(log_inter - m_t)
        s = qk_c * jnp.exp(lw_c - m_t[..., None])
        num = (w_inter[..., None] * jnp.einsum("bhlk,bhkv->bhlv", q_c, C)
               + jnp.einsum("bhij,bhjv->bhiv", s, v_c))
        den = w_inter * jnp.einsum("bhlk,bhk->bhl", q_c, n) + jnp.sum(s, axis=-1)
        h = num / jnp.maximum(jnp.abs(den), jnp.exp(-m_t))[..., None]
        m_new = jnp.maximum(b_c[..., -1] + m, me_c)
        w_old = jnp.exp(b_c[..., -1] + m - m_new)
        w_new = jnp.exp(lwe_c - m_new[..., None])
        C = w_old[..., None, None] * C + jnp.einsum("bhl,bhlk,bhlv->bhkv", w_new, k_c, v_c)
        n = w_old[..., None] * n + jnp.einsum("bhl,bhlk->bhk", w_new, k_c)
        return (C, n, m_new), h

    Bsz, H = q.shape[1], q.shape[2]
    carry0 = (jnp.zeros((Bsz, H, HEAD_DIM, HEAD_DIM), f32), jnp.zeros((Bsz, H, HEAD_DIM), f32),
              jnp.zeros((Bsz, H), f32))
    _, h = lax.scan(step, carry0, (q, k, v, b, log_w, m_intra, log_w_end, m_end, qk))
    return from_chunks(h)


def mlstm_mixer(q, k, v, i_raw, f_raw, o_raw, f_bias, norm_g):
    B, T = q.shape[0], q.shape[1]
    heads = lambda a: a.reshape(B, T, GROUP_HEADS, HEAD_DIM)
    h = mlstm_chunkwise(heads(q), heads(k) * HEAD_DIM ** -0.5, heads(v), i_raw, f_raw + f_bias)
    h = rms_norm(h.astype(q.dtype), norm_g) * jax.nn.sigmoid(heads(o_raw))
    return h.reshape(B, T, GROUP_WIDTH)


CONV_CARRY_ROWS = 8


def _shifted(cat, shift, rows):
    return pltpu.roll(cat, shift, 0)[CONV_CARRY_ROWS:CONV_CARRY_ROWS + rows]


def _sconv_kernel(pb_ref, w_ref, g_ref, o_ref, carry_ref):
    rows = pb_ref.shape[0]

    @pl.when(pl.program_id(1) == 0)
    def _():
        carry_ref[...] = jnp.zeros(carry_ref.shape, jnp.float32)

    u = pb_ref[:, 256:512] * pb_ref[:, 512:768]
    cat = jnp.concatenate([carry_ref[...], u], axis=0)
    conv = u * w_ref[2:3, :] + _shifted(cat, 1, rows) * w_ref[1:2, :] + _shifted(cat, 2, rows) * w_ref[0:1, :]
    carry_ref[...] = u[rows - CONV_CARRY_ROWS:rows]
    y = pb_ref[:, 0:256] * conv
    o_ref[...] = y * lax.rsqrt(jnp.mean(y * y, axis=-1, keepdims=True) + EPS) * g_ref[...]


def short_conv_pallas(pb, conv_w, mixg):
    B, T, _ = pb.shape
    rows = 512
    w = jnp.pad(conv_w, ((0, 8 - conv_w.shape[0]), (0, 0)))
    return pl.pallas_call(
        _sconv_kernel,
        grid=(B, T // rows),
        in_specs=[pl.BlockSpec((None, rows, 768), lambda b, t: (b, t, 0)),
                  pl.BlockSpec((8, 256), lambda b, t: (0, 0)),
                  pl.BlockSpec((1, 256), lambda b, t: (0, 0))],
        out_specs=pl.BlockSpec((None, rows, 256), lambda b, t: (b, t, 0)),
        out_shape=jax.ShapeDtypeStruct((B, T, 256), jnp.float32),
        scratch_shapes=[pltpu.VMEM((CONV_CARRY_ROWS, 256), jnp.float32)],
        compiler_params=pltpu.CompilerParams(dimension_semantics=("arbitrary", "arbitrary")),
        name="short_conv",
    )(pb, w, mixg)


def _stack(a, hm):
    return jnp.concatenate([a] * GROUP_HEADS, axis=0) * hm


def _unstack(y, L):
    return y[0:L] + y[L:2 * L] + y[2 * L:3 * L] + y[3 * L:4 * L]


def _tn_dot(a, b):
    return lax.dot_general(a, b, (((0,), (0,)), ((), ())), preferred_element_type=jnp.float32)


def _dot_hi(a, b):
    return jnp.dot(a, b, precision=HIGHEST, preferred_element_type=jnp.float32)


def _softplus(x):
    return jnp.maximum(x, 0.0) + jnp.log(1.0 + jnp.exp(-jnp.abs(x)))


def _block_masks(L):
    n = GROUP_HEADS * L
    r = lax.broadcasted_iota(jnp.int32, (n, n), 0)
    c = lax.broadcasted_iota(jnp.int32, (n, n), 1)
    same = (r // L) == (c // L)
    return same & (c <= r), same & (c < r)


def _head_norm(o, hm):
    return o * lax.rsqrt(_dot_hi(o * o, hm) * (1.0 / HEAD_DIM) + EPS)


def _head_l2(o, hm):
    return o * lax.rsqrt(_dot_hi(o * o, hm) + EPS)


def _gdn_kernel(pc_ref, cw_ref, hp_ref, hm_ref, tri_ref, eb_ref, ea_ref, o_ref, s_ref, carry_ref):
    L = pc_ref.shape[0]
    f32, bf16 = jnp.float32, jnp.bfloat16

    @pl.when(pl.program_id(1) == 0)
    def _():
        s_ref[...] = jnp.zeros(s_ref.shape, f32)
        carry_ref[...] = jnp.zeros(carry_ref.shape, f32)

    hm = hm_ref[...]
    x = pc_ref[:, 0:768]
    cat = jnp.concatenate([carry_ref[...], x], axis=0)
    conv = x * cw_ref[3:4, :]
    for s in (1, 2, 3):
        conv = conv + _shifted(cat, s, L) * cw_ref[3 - s:4 - s, :]
    carry_ref[...] = x[L - CONV_CARRY_ROWS:L]
    qkv = conv * jax.nn.sigmoid(conv)
    q = _head_l2(qkv[:, 0:256], hm) * HEAD_DIM ** -0.5
    k = _head_l2(qkv[:, 256:512], hm)
    v = qkv[:, 512:768]
    tail = pc_ref[:, 1024:1152]
    beta = jax.nn.sigmoid(_dot_hi(tail, eb_ref[...]))
    g = hp_ref[0:1, :] * _softplus(_dot_hi(tail, ea_ref[...]) + hp_ref[1:2, :])
    gc = _dot_hi(tri_ref[...], g)
    egc = jnp.exp(gc)
    kb = k * beta
    ks, kbs, qs, gs = (_stack(a, hm) for a in (k, kb, q, gc))
    avg = jnp.full((GROUP_HEADS * L, GROUP_HEADS * L), 1.0 / HEAD_DIM, f32)
    gcol = _dot_hi(gs, avg)
    grow = lax.dot_general(avg, gs, (((1,), (1,)), ((), ())), precision=HIGHEST,
                           preferred_element_type=f32)
    incl, strict = _block_masks(L)
    seg = jnp.exp(jnp.where(incl, gcol - grow, NEG))
    ksb = ks.astype(bf16)
    a_mat = jnp.where(strict, _nt_dot(kbs.astype(bf16), ksb) * seg, 0.0)
    qk = _nt_dot(qs.astype(bf16), ksb) * seg
    xs = jnp.concatenate([_stack(v * beta, hm), _stack(kb * egc, hm)], axis=1)
    xs = xs - _dot_hi(a_mat, xs)
    pw = a_mat
    for _ in range(int(math.log2(L)) - 1):
        pw = _dot_hi(pw, pw)
        xs = xs + _dot_hi(pw, xs)
    u = _unstack(xs[:, 0:256], L)
    w = _unstack(xs[:, 256:512], L)
    state = s_ref[...]
    sb = state.astype(bf16)
    v_new = u - jnp.dot(w.astype(bf16), sb, preferred_element_type=f32)
    o = jnp.dot((q * egc).astype(bf16), sb, preferred_element_type=f32)
    o = o + _unstack(jnp.dot(qk.astype(bf16), _stack(v_new, hm).astype(bf16), preferred_element_type=f32), L)
    k_dec = k * jnp.exp(gc[L - 1:L, :] - gc)
    s_ref[...] = state * egc[L - 1:L, :] + _tn_dot(k_dec.astype(bf16), v_new.astype(bf16)) * hm
    z = pc_ref[:, 768:1024]
    o_ref[...] = _head_norm(o, hm) * hp_ref[2:3, :] * (z * jax.nn.sigmoid(z))


def _expand_matrix(first_lane):
    e = np.zeros((128, GROUP_WIDTH), np.float32)
    for h in range(GROUP_HEADS):
        e[first_lane + h, h * HEAD_DIM:(h + 1) * HEAD_DIM] = 1.0
    return jnp.asarray(e)


def _head_mask():
    return jnp.asarray(np.kron(np.eye(GROUP_HEADS), np.ones((HEAD_DIM, HEAD_DIM))), jnp.float32)


def _per_head_rows(*vecs):
    rows = [jnp.repeat(v, HEAD_DIM) if v.shape[0] == GROUP_HEADS else jnp.tile(v, GROUP_HEADS) for v in vecs]
    return jnp.pad(jnp.stack(rows), ((0, 8 - len(rows)), (0, 0)))


def gated_deltanet_pallas(pc, conv_w, A_log, dt_bias, norm_g):
    B, T, width = pc.shape
    L = GDN_CHUNK
    n = GROUP_HEADS * L
    hp = _per_head_rows(-jnp.exp(A_log), dt_bias, norm_g)
    cw = jnp.pad(conv_w, ((0, 8 - conv_w.shape[0]), (0, 0)))
    tri = jnp.asarray(np.tril(np.ones((L, L), np.float32)))
    const = lambda a: pl.BlockSpec(a.shape, lambda b, t: (0, 0))
    consts = (cw, hp, _head_mask(), tri, _expand_matrix(0), _expand_matrix(GROUP_HEADS))
    return pl.pallas_call(
        _gdn_kernel,
        grid=(B, T // L),
        in_specs=[pl.BlockSpec((None, L, width), lambda b, t: (b, t, 0))] + [const(a) for a in consts],
        out_specs=pl.BlockSpec((None, L, 256), lambda b, t: (b, t, 0)),
        out_shape=jax.ShapeDtypeStruct((B, T, 256), jnp.float32),
        scratch_shapes=[pltpu.VMEM((n, n), jnp.float32), pltpu.VMEM((CONV_CARRY_ROWS, 768), jnp.float32)],
        compiler_params=pltpu.CompilerParams(dimension_semantics=("arbitrary", "arbitrary")),
        name="gated_deltanet",
    )(pc, *consts)


def _mlstm_kernel(pd_ref, hp_ref, hm_ref, tri_ref, ei_ref, ef_ref, o_ref, c_ref, n_ref, m_ref):
    L = pd_ref.shape[0]
    f32, bf16 = jnp.float32, jnp.bfloat16

    @pl.when(pl.program_id(1) == 0)
    def _():
        c_ref[...] = jnp.zeros(c_ref.shape, f32)
        n_ref[...] = jnp.zeros(n_ref.shape, f32)
        m_ref[...] = jnp.zeros(m_ref.shape, f32)

    hm = hm_ref[...]
    q = pd_ref[:, 0:256]
    k = pd_ref[:, 256:512] * HEAD_DIM ** -0.5
    v = pd_ref[:, 512:768]
    tail = pd_ref[:, 1024:1152]
    log_i = _dot_hi(tail, ei_ref[...])
    log_f = -_softplus(-(_dot_hi(tail, ef_ref[...]) + hp_ref[0:1, :]))
    b = _dot_hi(tri_ref[...], log_f)
    avg = jnp.full((GROUP_HEADS * L, GROUP_HEADS * L), 1.0 / HEAD_DIM, f32)
    bcol = _dot_hi(_stack(b, hm), avg)
    arow = lax.dot_general(avg, _stack(b - log_i, hm), (((1,), (1,)), ((), ())), precision=HIGHEST,
                           preferred_element_type=f32)
    incl, _ = _block_masks(L)
    log_w = jnp.where(incl, bcol - arow, NEG)

    def to_x(col):
        return _unstack(col * hm, L)

    m_intra = to_x(jnp.max(log_w, axis=-1, keepdims=True))
    b_last = b[L - 1:L, :]
    log_w_end = b_last - b + log_i
    m_end = jnp.max(log_w_end, axis=0, keepdims=True)
    m_prev = m_ref[...]
    log_inter = b + m_prev
    m_t = jnp.maximum(log_inter, m_intra)
    w_inter = jnp.exp(log_inter - m_t)
    qk = _nt_dot(_stack(q, hm).astype(bf16), _stack(k, hm).astype(bf16))
    s = qk * jnp.exp(log_w - _dot_hi(_stack(m_t, hm), avg))
    cb = c_ref[...].astype(bf16)
    num = w_inter * jnp.dot(q.astype(bf16), cb, preferred_element_type=f32)
    num = num + _unstack(jnp.dot(s.astype(bf16), _stack(v, hm).astype(bf16), preferred_element_type=f32), L)
    den = w_inter * _dot_hi(q * n_ref[...], hm) + to_x(jnp.sum(s, axis=-1, keepdims=True))
    h = num / jnp.maximum(jnp.abs(den), jnp.exp(-m_t))
    m_new = jnp.maximum(b_last + m_prev, m_end)
    w_old = jnp.exp(b_last + m_prev - m_new)
    kw = k * jnp.exp(log_w_end - m_new)
    c_ref[...] = w_old * c_ref[...] + _tn_dot(kw.astype(bf16), v.astype(bf16)) * hm
    n_ref[...] = w_old * n_ref[...] + jnp.sum(kw, axis=0, keepdims=True)
    m_ref[...] = m_new
    o_ref[...] = _head_norm(h, hm) * hp_ref[1:2, :] * jax.nn.sigmoid(pd_ref[:, 768:1024])


def mlstm_pallas(pd, f_bias, norm_g):
    B, T, width = pd.shape
    L = MLSTM_CHUNK
    n = GROUP_HEADS * L
    hp = _per_head_rows(f_bias, norm_g)
    tri = jnp.asarray(np.tril(np.ones((L, L), np.float32)))
    const = lambda a: pl.BlockSpec(a.shape, lambda b, t: (0, 0))
    consts = (hp, _head_mask(), tri, _expand_matrix(0), _expand_matrix(GROUP_HEADS))
    return pl.pallas_call(
        _mlstm_kernel,
        grid=(B, T // L),
        in_specs=[pl.BlockSpec((None, L, width), lambda b, t: (b, t, 0))] + [const(a) for a in consts],
        out_specs=pl.BlockSpec((None, L, 256), lambda b, t: (b, t, 0)),
        out_shape=jax.ShapeDtypeStruct((B, T, 256), jnp.float32),
        scratch_shapes=[pltpu.VMEM((n, n), jnp.float32), pltpu.VMEM((1, 256), jnp.float32),
                        pltpu.VMEM((1, 256), jnp.float32)],
        compiler_params=pltpu.CompilerParams(dimension_semantics=("arbitrary", "arbitrary")),
        name="mlstm",
    )(pd, *consts)


def _split_bf16(a, parts):
    out, rest = [], a
    for _ in range(parts):
        piece = rest.astype(jnp.bfloat16)
        out.append(piece)
        rest = rest - piece.astype(jnp.float32)
    return out


def _dot_sel(a, sel, parts=3):
    return sum(jnp.dot(p, sel, preferred_element_type=jnp.float32) for p in _split_bf16(a, parts))


def _sel_dot(sel, a, parts=3):
    return sum(jnp.dot(sel, p, preferred_element_type=jnp.float32) for p in _split_bf16(a, parts))


def _tile_rows(a):
    return jnp.concatenate([a] * GROUP_HEADS, axis=0)


def _head_norm(o, hmb):
    return o * lax.rsqrt(_dot_sel(o * o, hmb, parts=2) * (1.0 / HEAD_DIM) + EPS)


def _head_l2(o, hmb):
    return o * lax.rsqrt(_dot_sel(o * o, hmb, parts=2) + EPS)


def _chunk_masks(L):
    n = GROUP_HEADS * L
    r = np.arange(n)[:, None]
    c = np.arange(n)[None, :]
    same = (r // L) == (c // L)
    i, j = r % L, c % L
    masks = []
    s = 1
    while s < L:
        masks.append(same & (i // (2 * s) == j // (2 * s)) & (i % (2 * s) >= s) & (j % (2 * s) < s))
        s *= 2
    masks += [r == c, same & (j <= i), same & (j < i)]
    return jnp.asarray(np.stack(masks).astype(np.float32))


def _gdn_kernel(pc_ref, cw_ref, hp_ref, hm_ref, tri_ref, eb_ref, ea_ref, lv_ref, o_ref, s_ref, carry_ref):
    L = pc_ref.shape[0]
    f32, bf16 = jnp.float32, jnp.bfloat16
    n_lev = lv_ref.shape[0] - 3

    @pl.when(pl.program_id(1) == 0)
    def _():
        s_ref[...] = jnp.zeros(s_ref.shape, f32)
        carry_ref[...] = jnp.zeros(carry_ref.shape, f32)

    hm = hm_ref[...]
    hmb = hm.astype(bf16)
    x = pc_ref[:, 0:768]
    cat = jnp.concatenate([carry_ref[...], x], axis=0)
    conv = x * cw_ref[3:4, :]
    for s in (1, 2, 3):
        conv = conv + _shifted(cat, s, L) * cw_ref[3 - s:4 - s, :]
    carry_ref[...] = x[L - CONV_CARRY_ROWS:L]
    qkv = conv * jax.nn.sigmoid(conv)
    q = _head_l2(qkv[:, 0:256], hmb) * HEAD_DIM ** -0.5
    k = _head_l2(qkv[:, 256:512], hmb)
    v = qkv[:, 512:768]
    tail = pc_ref[:, 1024:1152]
    beta = jax.nn.sigmoid(_dot_sel(tail, eb_ref[...]))
    g = hp_ref[0:1, :] * _softplus(_dot_sel(tail, ea_ref[...]) + hp_ref[1:2, :])
    gc = _sel_dot(tri_ref[...], g)
    egc = jnp.exp(gc)
    kb = k * beta
    gct = _tile_rows(gc)
    diff = gct - (gct * hm).T
    seg = jnp.exp(jnp.where(lv_ref[n_lev + 1] > 0.5, diff, NEG))
    ksb = (_tile_rows(k) * hm).astype(bf16)
    a_mat = _nt_dot((_tile_rows(kb) * hm).astype(bf16), ksb) * seg * lv_ref[n_lev + 2]
    qk = _nt_dot((_tile_rows(q) * hm).astype(bf16), ksb) * seg
    t_inv = lv_ref[n_lev] - a_mat * lv_ref[0]
    for lev in range(1, n_lev):
        tb = t_inv.astype(bf16)
        te = jnp.dot(tb, (a_mat * lv_ref[lev]).astype(bf16), preferred_element_type=f32)
        t_inv = t_inv - jnp.dot(te.astype(bf16), tb, preferred_element_type=f32)
    rhs = jnp.concatenate([_tile_rows(v * beta) * hm, _tile_rows(kb * egc) * hm], axis=1)
    sol = jnp.dot(t_inv.astype(bf16), rhs.astype(bf16), preferred_element_type=f32)
    u = _unstack(sol[:, 0:256], L)
    w = _unstack(sol[:, 256:512], L)
    state = s_ref[...]
    sb = state.astype(bf16)
    v_new = u - jnp.dot(w.astype(bf16), sb, preferred_element_type=f32)
    o = jnp.dot((q * egc).astype(bf16), sb, preferred_element_type=f32)
    o = o + _unstack(jnp.dot(qk.astype(bf16), (_tile_rows(v_new) * hm).astype(bf16),
                             preferred_element_type=f32), L)
    k_dec = k * jnp.exp(gc[L - 1:L, :] - gc)
    s_ref[...] = state * egc[L - 1:L, :] + _tn_dot(k_dec.astype(bf16), v_new.astype(bf16)) * hm
    z = pc_ref[:, 768:1024]
    o_ref[...] = _head_norm(o, hmb) * hp_ref[2:3, :] * (z * jax.nn.sigmoid(z))


def _expand_matrix(first_lane):
    e = np.zeros((128, GROUP_WIDTH), np.float32)
    for h in range(GROUP_HEADS):
        e[first_lane + h, h * HEAD_DIM:(h + 1) * HEAD_DIM] = 1.0
    return jnp.asarray(e, jnp.bfloat16)


def _chunk_consts(L):
    tri = jnp.asarray(np.tril(np.ones((L, L), np.float32)), jnp.bfloat16)
    return _head_mask(), tri, _expand_matrix(0), _expand_matrix(GROUP_HEADS), _chunk_masks(L)


def _const_spec(a):
    return pl.BlockSpec(a.shape, lambda b, t: (0,) * a.ndim)


def gated_deltanet_pallas(pc, conv_w, A_log, dt_bias, norm_g):
    B, T, width = pc.shape
    L = GDN_CHUNK
    n = GROUP_HEADS * L
    hp = _per_head_rows(-jnp.exp(A_log), dt_bias, norm_g)
    cw = jnp.pad(conv_w, ((0, 8 - conv_w.shape[0]), (0, 0)))
    consts = (cw, hp) + _chunk_consts(L)
    return pl.pallas_call(
        _gdn_kernel,
        grid=(B, T // L),
        in_specs=[pl.BlockSpec((None, L, width), lambda b, t: (b, t, 0))] + [_const_spec(a) for a in consts],
        out_specs=pl.BlockSpec((None, L, 256), lambda b, t: (b, t, 0)),
        out_shape=jax.ShapeDtypeStruct((B, T, 256), jnp.float32),
        scratch_shapes=[pltpu.VMEM((n, n), jnp.float32), pltpu.VMEM((CONV_CARRY_ROWS, 768), jnp.float32)],
        compiler_params=pltpu.CompilerParams(dimension_semantics=("arbitrary", "arbitrary")),
        name="gated_deltanet",
    )(pc, *consts)


def _mlstm_kernel(pd_ref, hp_ref, hm_ref, tri_ref, ei_ref, ef_ref, lv_ref, o_ref, c_ref, n_ref, m_ref):
    L = pd_ref.shape[0]
    f32, bf16 = jnp.float32, jnp.bfloat16
    n_lev = lv_ref.shape[0] - 3

    @pl.when(pl.program_id(1) == 0)
    def _():
        c_ref[...] = jnp.zeros(c_ref.shape, f32)
        n_ref[...] = jnp.zeros(n_ref.shape, f32)
        m_ref[...] = jnp.zeros(m_ref.shape, f32)

    hm = hm_ref[...]
    hmb = hm.astype(bf16)
    q = pd_ref[:, 0:256]
    k = pd_ref[:, 256:512] * HEAD_DIM ** -0.5
    v = pd_ref[:, 512:768]
    tail = pd_ref[:, 1024:1152]
    log_i = _dot_sel(tail, ei_ref[...])
    log_f = -_softplus(-(_dot_sel(tail, ef_ref[...]) + hp_ref[0:1, :]))
    b = _sel_dot(tri_ref[...], log_f)
    log_w = jnp.where(lv_ref[n_lev + 1] > 0.5, _tile_rows(b) - (_tile_rows(b - log_i) * hm).T, NEG)

    def to_x(col):
        return _unstack(col * hm, L)

    m_intra = to_x(jnp.max(log_w, axis=-1, keepdims=True))
    b_last = b[L - 1:L, :]
    log_w_end = b_last - b + log_i
    m_end = jnp.max(log_w_end, axis=0, keepdims=True)
    m_prev = m_ref[...]
    log_inter = b + m_prev
    m_t = jnp.maximum(log_inter, m_intra)
    w_inter = jnp.exp(log_inter - m_t)
    qk = _nt_dot((_tile_rows(q) * hm).astype(bf16), (_tile_rows(k) * hm).astype(bf16))
    s = qk * jnp.exp(log_w - _tile_rows(m_t))
    cb = c_ref[...].astype(bf16)
    num = w_inter * jnp.dot(q.astype(bf16), cb, preferred_element_type=f32)
    num = num + _unstack(jnp.dot(s.astype(bf16), (_tile_rows(v) * hm).astype(bf16),
                                 preferred_element_type=f32), L)
    den = w_inter * _dot_sel(q * n_ref[...], hmb, parts=2) + to_x(jnp.sum(s, axis=-1, keepdims=True))
    h = num / jnp.maximum(jnp.abs(den), jnp.exp(-m_t))
    m_new = jnp.maximum(b_last + m_prev, m_end)
    w_old = jnp.exp(b_last + m_prev - m_new)
    kw = k * jnp.exp(log_w_end - m_new)
    c_ref[...] = w_old * c_ref[...] + _tn_dot(kw.astype(bf16), v.astype(bf16)) * hm
    n_ref[...] = w_old * n_ref[...] + jnp.sum(kw, axis=0, keepdims=True)
    m_ref[...] = m_new
    o_ref[...] = _head_norm(h, hmb) * hp_ref[1:2, :] * jax.nn.sigmoid(pd_ref[:, 768:1024])


def mlstm_pallas(pd, f_bias, norm_g):
    B, T, width = pd.shape
    L = MLSTM_CHUNK
    n = GROUP_HEADS * L
    consts = (_per_head_rows(f_bias, norm_g),) + _chunk_consts(L)
    return pl.pallas_call(
        _mlstm_kernel,
        grid=(B, T // L),
        in_specs=[pl.BlockSpec((None, L, width), lambda b, t: (b, t, 0))] + [_const_spec(a) for a in consts],
        out_specs=pl.BlockSpec((None, L, 256), lambda b, t: (b, t, 0)),
        out_shape=jax.ShapeDtypeStruct((B, T, 256), jnp.float32),
        scratch_shapes=[pltpu.VMEM((n, n), jnp.float32), pltpu.VMEM((1, 256), jnp.float32),
                        pltpu.VMEM((1, 256), jnp.float32)],
        compiler_params=pltpu.CompilerParams(dimension_semantics=("arbitrary", "arbitrary")),
        name="mlstm",
    )(pd, *consts)


def kernel(x, c, ada_w, ada_b, norm_g, ffn1_w13, ffn1_w2, ffn2_w13, ffn2_w2, w_in, b_in, q_norm_g, k_norm_g, cmp_pos, cmp_k_w1, cmp_k_w2, cmp_v_w1, cmp_v_w2, t5_table, sc_conv_w, gdn_conv_w, gdn_A_log, gdn_dt_bias, gdn_norm_g, mlstm_f_bias, mlstm_norm_g, mix_norm_g, w_out):
    B, T, D = x.shape
    depth = ada_w.shape[0]
    bf16 = jnp.bfloat16
    x2 = x.reshape(B * T, D)
    tables = nsa_bias_tables(t5_table, T)
    ovt = nsa_overlap_t(T)
    for l in range(depth):
        mod = ada_modulation(c, ada_w[l].astype(bf16), ada_b[l][None, :])
        mod = mod.reshape(B, N_SUBLAYERS, 3, 1, D)
        w13 = ffn1_w13[l].astype(bf16)
        x2 = ffn_half_step(x2, norm_g[l, 0][None, :], mod[:, 0, 0], mod[:, 0, 1], mod[:, 0, 2],
                           w13[:, :D_FF], w13[:, D_FF:], ffn1_w2[l].astype(bf16), T)
        ws, bs = split_in_weights(w_in[l], b_in[l])
        pa, pb, pc, pd = in_projection(x2, norm_g[l, 1][None, :], mod[:, 1, 0], mod[:, 1, 1], ws, bs, T)
        pa, pb, pc, pd = (a.reshape(B, T, -1) for a in (pa, pb, pc, pd))
        y_a = nsa_mixer_pallas(pa, q_norm_g[l], k_norm_g[l], cmp_pos[l], cmp_k_w1[l], cmp_k_w2[l],
                               cmp_v_w1[l], cmp_v_w2[l], tables, ovt, mix_norm_g[l, 0][None, :])
        y_b = short_conv_pallas(pb, sc_conv_w[l], mix_norm_g[l, 1][None, :])
        y_c = gated_deltanet_pallas(pc, gdn_conv_w[l], gdn_A_log[l], gdn_dt_bias[l], gdn_norm_g[l])
        y_d = mlstm_pallas(pd, mlstm_f_bias[l], mlstm_norm_g[l])
        y = jnp.concatenate([y_a, y_b, y_c, y_d], axis=-1)
        x2 = out_projection(x2, y.reshape(B * T, -1), mod[:, 1, 2], w_out[l].astype(bf16), T)
        w13 = ffn2_w13[l].astype(bf16)
        x2 = ffn_half_step(x2, norm_g[l, 2][None, :], mod[:, 2, 0], mod[:, 2, 1], mod[:, 2, 2],
                           w13[:, :D_FF], w13[:, D_FF:], ffn2_w2[l].astype(bf16), T)
    return x2.reshape(B, T, D)
```

```python
import functools
import math

import jax
import jax.numpy as jnp
import numpy as np
from jax import lax
from jax.experimental import pallas as pl
from jax.experimental.pallas import tpu as pltpu

D_MODEL = 1024
HEAD_DIM = 64
GROUP_HEADS = 4
GROUP_WIDTH = GROUP_HEADS * HEAD_DIM
CMP_STRIDE = 16
CMP_BLOCK = 32
SLC_BLOCK = 64
N_SLC = 16
N_LOCAL_SLC = 2
WINDOW = 512
Q_BLOCK = 128
FORCE = 1e6
N_BUCKETS = 32
MAX_DISTANCE = 128
GDN_CHUNK = 64
MLSTM_CHUNK = 64
D_FF = 2816
N_SUBLAYERS = 3
EPS = 1e-6

IN_LAYOUT = (
    ("a_q", 256), ("a_k_cmp", 64), ("a_v_cmp", 64),
    ("a_k_slc", 64), ("a_v_slc", 64), ("a_k_win", 64), ("a_v_win", 64),
    ("a_gate", 12),
    ("b_b", 256), ("b_c", 256), ("b_x", 256),
    ("c_q", 256), ("c_k", 256), ("c_v", 256),
    ("c_beta", 4), ("c_alpha", 4), ("c_z", 256),
    ("d_q", 256), ("d_k", 256), ("d_v", 256),
    ("d_i", 4), ("d_f", 4), ("d_o", 256),
)

VMEM_LIMIT_BYTES = 56 * 1024 * 1024
FFN_TOKEN_TILE = 512
FFN_CHUNK = 256
PROJ_TOKEN_TILE = 512


def _modulated_norm(x, g, scale, shift):
    y = x * lax.rsqrt(jnp.mean(x * x, axis=-1, keepdims=True) + EPS)
    return (y * g) * (1.0 + scale) + shift


def _ada_kernel(c_ref, w_ref, b_ref, o_ref):
    c = c_ref[...]
    cond = c * jax.nn.sigmoid(c)
    o_ref[...] = jnp.dot(cond.astype(jnp.bfloat16), w_ref[...],
                         preferred_element_type=jnp.float32) + b_ref[...]


def ada_modulation(c, w, b):
    B, D = c.shape
    N = w.shape[1]
    tn = 1152
    return pl.pallas_call(
        _ada_kernel,
        grid=(N // tn,),
        in_specs=[pl.BlockSpec((B, D), lambda j: (0, 0)),
                  pl.BlockSpec((D, tn), lambda j: (0, j)),
                  pl.BlockSpec((1, tn), lambda j: (0, j))],
        out_specs=pl.BlockSpec((B, tn), lambda j: (0, j)),
        out_shape=jax.ShapeDtypeStruct((B, N), jnp.float32),
        name="ada_modulation",
    )(c, w, b)


def _ffn_kernel(x_ref, g_ref, shift_ref, scale_ref, gate_ref, w1_ref, w3_ref, w2_ref, o_ref, act_ref):
    x = x_ref[...]
    h = _modulated_norm(x, g_ref[...], scale_ref[0], shift_ref[0]).astype(jnp.bfloat16)
    n_chunks = act_ref.shape[1] // FFN_CHUNK
    for ci in range(n_chunks):
        cs = slice(ci * FFN_CHUNK, (ci + 1) * FFN_CHUNK)
        a = jnp.dot(h, w1_ref[:, cs], preferred_element_type=jnp.float32)
        b = jnp.dot(h, w3_ref[:, cs], preferred_element_type=jnp.float32)
        act_ref[:, cs] = (a * jax.nn.sigmoid(a) * b).astype(jnp.bfloat16)
    y = jnp.dot(act_ref[...], w2_ref[...], preferred_element_type=jnp.float32)
    o_ref[...] = x + (0.5 * gate_ref[0]) * y


def ffn_half_step(x2, g, shift, scale, gate, w1, w3, w2, tokens_per_batch):
    M, D = x2.shape
    F = w2.shape[0]
    tm = FFN_TOKEN_TILE
    tiles_per_batch = tokens_per_batch // tm
    resident = dict(pipeline_mode=pl.Buffered(1))
    mod_spec = pl.BlockSpec((1, 1, D), lambda i: (i // tiles_per_batch, 0, 0))
    return pl.pallas_call(
        _ffn_kernel,
        grid=(M // tm,),
        in_specs=[pl.BlockSpec((tm, D), lambda i: (i, 0)),
                  pl.BlockSpec((1, D), lambda i: (0, 0)),
                  mod_spec, mod_spec, mod_spec,
                  pl.BlockSpec((D, F), lambda i: (0, 0), **resident),
                  pl.BlockSpec((D, F), lambda i: (0, 0), **resident),
                  pl.BlockSpec((F, D), lambda i: (0, 0), **resident)],
        out_specs=pl.BlockSpec((tm, D), lambda i: (i, 0)),
        out_shape=jax.ShapeDtypeStruct((M, D), jnp.float32),
        scratch_shapes=[pltpu.VMEM((tm, F), jnp.bfloat16)],
        compiler_params=pltpu.CompilerParams(dimension_semantics=("arbitrary",),
                                             vmem_limit_bytes=VMEM_LIMIT_BYTES),
        name="ffn_half_step",
    )(x2, g, shift, scale, gate, w1, w3, w2)


def _in_proj_kernel(x_ref, g_ref, shift_ref, scale_ref, wa_ref, wb_ref, wc_ref, wd_ref,
                    ba_ref, bb_ref, bc_ref, bd_ref, oa_ref, ob_ref, oc_ref, od_ref):
    h = _modulated_norm(x_ref[...], g_ref[...], scale_ref[0], shift_ref[0]).astype(jnp.bfloat16)
    for w_ref, b_ref, o_ref in ((wa_ref, ba_ref, oa_ref), (wb_ref, bb_ref, ob_ref),
                                (wc_ref, bc_ref, oc_ref), (wd_ref, bd_ref, od_ref)):
        o_ref[...] = jnp.dot(h, w_ref[...], preferred_element_type=jnp.float32) + b_ref[...]


def in_projection(x2, g, shift, scale, ws, bs, tokens_per_batch):
    M, D = x2.shape
    tm = PROJ_TOKEN_TILE
    tiles_per_batch = tokens_per_batch // tm
    mod_spec = pl.BlockSpec((1, 1, D), lambda i: (i // tiles_per_batch, 0, 0))
    const = lambda a, **kw: pl.BlockSpec(a.shape, lambda i: (0, 0), **kw)
    return pl.pallas_call(
        _in_proj_kernel,
        grid=(M // tm,),
        in_specs=[pl.BlockSpec((tm, D), lambda i: (i, 0)), const(g), mod_spec, mod_spec]
                 + [const(w, pipeline_mode=pl.Buffered(1)) for w in ws] + [const(b) for b in bs],
        out_specs=[pl.BlockSpec((tm, w.shape[1]), lambda i: (i, 0)) for w in ws],
        out_shape=[jax.ShapeDtypeStruct((M, w.shape[1]), jnp.float32) for w in ws],
        compiler_params=pltpu.CompilerParams(dimension_semantics=("arbitrary",),
                                             vmem_limit_bytes=VMEM_LIMIT_BYTES),
        name="in_projection",
    )(x2, g, shift, scale, *ws, *bs)


def _group_columns():
    offs, o = {}, 0
    for n, s in IN_LAYOUT:
        offs[n] = np.arange(o, o + s)
        o += s
    cat = lambda names: np.concatenate([offs[n] for n in names])
    return (cat(["a_q", "a_k_cmp", "a_v_cmp", "a_k_slc", "a_v_slc", "a_k_win", "a_v_win", "a_gate"]),
            cat(["b_b", "b_c", "b_x"]),
            cat(["c_q", "c_k", "c_v", "c_z", "c_beta", "c_alpha"]),
            cat(["d_q", "d_k", "d_v", "d_o", "d_i", "d_f"]))


GROUP_SLAB_WIDTH = (768, 768, 1152, 1152)


def split_in_weights(w, b):
    ws, bs = [], []
    for idx, width in zip(_group_columns(), GROUP_SLAB_WIDTH):
        pad = width - idx.size
        ws.append(jnp.pad(w[:, idx], ((0, 0), (0, pad))).astype(jnp.bfloat16))
        bs.append(jnp.pad(b[idx], (0, pad))[None, :])
    return ws, bs


def _out_proj_kernel(x_ref, y_ref, gate_ref, w_ref, o_ref):
    z = jnp.dot(y_ref[...].astype(jnp.bfloat16), w_ref[...], preferred_element_type=jnp.float32)
    o_ref[...] = x_ref[...] + gate_ref[0] * z


def out_projection(x2, y2, gate, w, tokens_per_batch):
    M, D = x2.shape
    K = y2.shape[1]
    tm = PROJ_TOKEN_TILE
    tiles_per_batch = tokens_per_batch // tm
    return pl.pallas_call(
        _out_proj_kernel,
        grid=(M // tm,),
        in_specs=[pl.BlockSpec((tm, D), lambda i: (i, 0)),
                  pl.BlockSpec((tm, K), lambda i: (i, 0)),
                  pl.BlockSpec((1, 1, D), lambda i: (i // tiles_per_batch, 0, 0)),
                  pl.BlockSpec((K, D), lambda i: (0, 0), pipeline_mode=pl.Buffered(1))],
        out_specs=pl.BlockSpec((tm, D), lambda i: (i, 0)),
        out_shape=jax.ShapeDtypeStruct((M, D), jnp.float32),
        compiler_params=pltpu.CompilerParams(dimension_semantics=("arbitrary",),
                                             vmem_limit_bytes=VMEM_LIMIT_BYTES),
        name="out_projection",
    )(x2, y2, gate, w)


def rms_norm(x, g):
    xf = x.astype(jnp.float32)
    y = xf * lax.rsqrt(jnp.mean(xf * xf, axis=-1, keepdims=True) + EPS)
    return (y * g).astype(x.dtype)


def l2_norm(x):
    xf = x.astype(jnp.float32)
    return (xf * lax.rsqrt(jnp.sum(xf * xf, axis=-1, keepdims=True) + EPS)).astype(x.dtype)


def causal_dwconv(x, w):
    K, C = w.shape
    return lax.conv_general_dilated(
        x, w[:, None, :], window_strides=(1,), padding=[(K - 1, 0)],
        dimension_numbers=("NWC", "WIO", "NWC"), feature_group_count=C)


def masked_softmax(logits, mask):
    l = jnp.where(mask, logits.astype(jnp.float32), -1e30)
    m = jnp.max(l, axis=-1, keepdims=True)
    e = jnp.where(mask, jnp.exp(l - m), 0.0)
    return e / jnp.maximum(jnp.sum(e, axis=-1, keepdims=True), 1e-30)


def t5_bucket(dist):
    n = jnp.maximum(dist, 0)
    max_exact = N_BUCKETS // 2
    nf = jnp.maximum(n, 1).astype(jnp.float32)
    large = max_exact + (jnp.log(nf / max_exact) / math.log(MAX_DISTANCE / max_exact)
                         * (N_BUCKETS - max_exact)).astype(jnp.int32)
    large = jnp.minimum(large, N_BUCKETS - 1)
    return jnp.where(n < max_exact, n, large)


NEG = -1e30
HIGHEST = lax.Precision.HIGHEST


def _nt_dot(a, b):
    return lax.dot_general(a, b, (((1,), (1,)), ((), ())), preferred_element_type=jnp.float32)


def _stack_heads(qn):
    lane = lax.broadcasted_iota(jnp.int32, (qn.shape[0], 128), 1)
    lo = lane < HEAD_DIM
    zero = jnp.zeros((), qn.dtype)
    halves = (qn[:, 0:128], qn[:, 128:256])
    return jnp.concatenate([jnp.where(lo if h % 2 == 0 else ~lo, halves[h // 2], zero)
                            for h in range(GROUP_HEADS)], axis=0)


def _nsa_prep_kernel(q_ref, kvs_ref, kvw_ref, qg_ref, kg_ref, seg_ref,
                     qn_ref, qnt_ref, kxs_ref, v1s_ref, kkw_ref, v1w_ref):
    tm = q_ref.shape[0]
    q = q_ref[...]
    ss = jnp.dot(q * q, seg_ref[...], precision=HIGHEST, preferred_element_type=jnp.float32)
    qn = q * lax.rsqrt(ss * (1.0 / HEAD_DIM) + EPS) * qg_ref[...]
    qn_ref[...] = qn.astype(qn_ref.dtype)
    qnt_ref[...] = qn.T.astype(qnt_ref.dtype)
    lane = lax.broadcasted_iota(jnp.int32, (tm, 128), 1)
    lo = lane < HEAD_DIM

    def split(x):
        xr = pltpu.roll(x, HEAD_DIM, 1)
        ss = jnp.sum(jnp.where(lo, x * x, 0.0), axis=-1, keepdims=True)
        kk = jnp.where(lo, x, xr) * lax.rsqrt(ss * (1.0 / HEAD_DIM) + EPS) * kg_ref[...]
        return kk, jnp.where(lo, xr, 1.0)

    kk, v1 = split(kvs_ref[...])
    tok = pl.program_id(1) * tm + lax.broadcasted_iota(jnp.int32, (tm, 128), 0)
    onehot = jnp.where(lane == tok // SLC_BLOCK, 1.0, 0.0)
    kxs_ref[...] = jnp.concatenate([kk, onehot], axis=1).astype(kxs_ref.dtype)
    v1s_ref[...] = v1.T.astype(v1s_ref.dtype)
    kk, v1 = split(kvw_ref[...])
    kkw_ref[...] = kk.astype(kkw_ref.dtype)
    v1w_ref[...] = v1.T.astype(v1w_ref.dtype)


def nsa_prep(pa, qg, kg2, seg):
    B, T, _ = pa.shape
    tm = 512
    bf16 = jnp.bfloat16
    col = lambda w, j: pl.BlockSpec((None, tm, w), lambda b, i: (b, i, j))
    rowblk = lambda h: pl.BlockSpec((None, h, tm), lambda b, i: (b, 0, i))
    const = lambda shape: pl.BlockSpec(shape, lambda b, i: (0, 0))
    return pl.pallas_call(
        _nsa_prep_kernel,
        grid=(B, T // tm),
        in_specs=[col(256, 0), col(128, 3), col(128, 4), const((1, 256)), const((1, 128)), const((256, 256))],
        out_specs=[col(256, 0), rowblk(256), col(256, 0), rowblk(128), col(128, 0), rowblk(128)],
        out_shape=[jax.ShapeDtypeStruct((B, T, 256), bf16), jax.ShapeDtypeStruct((B, 256, T), bf16),
                   jax.ShapeDtypeStruct((B, T, 256), bf16), jax.ShapeDtypeStruct((B, 128, T), bf16),
                   jax.ShapeDtypeStruct((B, T, 128), bf16), jax.ShapeDtypeStruct((B, 128, T), bf16)],
        name="nsa_prep",
    )(pa, pa, pa, qg, kg2, seg)


def _compress_kernel(kch_ref, vch_ref, plo_ref, phi_ref, w1k_ref, w2k_ref, w1v_ref, w2v_ref, kg_ref,
                     kkc_ref, vvc_ref):
    nrow = kch_ref.shape[0]
    half = CMP_STRIDE * HEAD_DIM
    bf16 = jnp.bfloat16
    valid = lax.broadcasted_iota(jnp.int32, (nrow, 128), 0) < nrow - 1

    def comp(ch, w1_ref, w2_ref):
        a = jnp.dot((ch + plo_ref[...]).astype(bf16), w1_ref[0:half, :], preferred_element_type=jnp.float32)
        b = jnp.dot((ch + phi_ref[...]).astype(bf16), w1_ref[half:2 * half, :],
                    preferred_element_type=jnp.float32)
        hid = a + pltpu.roll(b, nrow - 1, 0)
        hid = hid * jax.nn.sigmoid(hid)
        return jnp.dot(hid.astype(bf16), w2_ref[...], preferred_element_type=jnp.float32)

    kc = comp(kch_ref[...], w1k_ref, w2k_ref)
    kc = kc * lax.rsqrt(jnp.mean(kc * kc, axis=-1, keepdims=True) + EPS) * kg_ref[...]
    kkc_ref[...] = jnp.where(valid, kc, 0.0).astype(kkc_ref.dtype)
    vc = comp(vch_ref[...], w1v_ref, w2v_ref)
    vvc_ref[...] = jnp.where(valid, vc, 0.0).astype(vvc_ref.dtype)


def nsa_compress(kch, vch, plo, phi, w1k, w2k, w1v, w2v, kg2):
    B, nrow, width = kch.shape
    bf16 = jnp.bfloat16
    full = lambda a: pl.BlockSpec(a.shape, lambda b: (0,) * a.ndim)
    bspec = pl.BlockSpec((None, nrow, width), lambda b: (b, 0, 0))
    ospec = pl.BlockSpec((None, nrow, 128), lambda b: (b, 0, 0))
    return pl.pallas_call(
        _compress_kernel,
        grid=(B,),
        in_specs=[bspec, bspec, full(plo), full(phi), full(w1k), full(w2k), full(w1v), full(w2v), full(kg2)],
        out_specs=[ospec, ospec],
        out_shape=[jax.ShapeDtypeStruct((B, nrow, 128), bf16), jax.ShapeDtypeStruct((B, nrow, 128), bf16)],
        name="nsa_compress",
    )(kch, vch, plo, phi, w1k, w2k, w1v, w2v, kg2)


def _cmp_select_kernel(qn_ref, kkc_ref, vvc_ref, cb_ref, ovt_ref, ocmp_ref, pen_ref, sc_ref, *, n_sel):
    i = pl.program_id(1)
    tq = qn_ref.shape[0]
    nblk = ovt_ref.shape[0]
    bf16 = jnp.bfloat16
    q4 = _stack_heads(qn_ref[...])
    s = _nt_dot(q4, kkc_ref[...]) + cb_ref[...]
    m = jnp.max(s, axis=-1, keepdims=True)
    e = jnp.where(s > 0.5 * NEG, jnp.exp(s - m), 0.0)
    p = e * (1.0 / jnp.maximum(jnp.sum(e, axis=-1, keepdims=True), 1e-30))
    o = jnp.dot(p.astype(bf16), vvc_ref[...], preferred_element_type=jnp.float32)
    lo = lax.broadcasted_iota(jnp.int32, (tq, 128), 1) < HEAD_DIM
    ocmp_ref[:, 0:128] = jnp.where(lo, o[0:tq], o[tq:2 * tq])
    ocmp_ref[:, 128:256] = jnp.where(lo, o[2 * tq:3 * tq], o[3 * tq:4 * tq])
    psum = p[0:tq] + p[tq:2 * tq] + p[2 * tq:3 * tq] + p[3 * tq:4 * tq]
    hi = psum.astype(bf16)
    lo_part = (psum - hi.astype(jnp.float32)).astype(bf16)
    score = _nt_dot(ovt_ref[...], hi) + _nt_dot(ovt_ref[...], lo_part)
    blk = lax.broadcasted_iota(jnp.int32, (nblk, tq), 0)
    qpos = i * tq + lax.broadcasted_iota(jnp.int32, (nblk, tq), 1)
    cur = qpos // SLC_BLOCK
    forced = (blk == 0) | ((cur - blk >= 0) & (cur - blk < N_LOCAL_SLC))
    score = jnp.where(forced, FORCE, score)
    score = jnp.where(blk <= cur, score, -FORCE)
    sc_ref[...] = score
    cnt = jnp.zeros((nblk, tq), jnp.float32)
    for jp in range(nblk):
        row = sc_ref[jp:jp + 1, :]
        tie = jnp.where(blk > jp, 1.0, 0.0)
        cnt = cnt + jnp.where(row > score, 1.0, jnp.where(row == score, tie, 0.0))
    pen = jnp.where(cnt < n_sel, 0.0, NEG)
    if nblk < 128:
        pen = jnp.concatenate([pen, jnp.zeros((128 - nblk, tq), jnp.float32)], axis=0)
    pen_ref[...] = pen.astype(pen_ref.dtype)


def nsa_cmp_select(qn, kkc, vvc, cbias, ovt, n_sel):
    B, T, _ = qn.shape
    nrow = kkc.shape[1]
    nblk = ovt.shape[0]
    tq = Q_BLOCK
    return pl.pallas_call(
        functools.partial(_cmp_select_kernel, n_sel=n_sel),
        grid=(B, T // tq),
        in_specs=[pl.BlockSpec((None, tq, 256), lambda b, i: (b, i, 0)),
                  pl.BlockSpec((None, nrow, 128), lambda b, i: (b, 0, 0)),
                  pl.BlockSpec((None, nrow, 128), lambda b, i: (b, 0, 0)),
                  pl.BlockSpec((None, 4 * tq, nrow), lambda b, i: (i, 0, 0)),
                  pl.BlockSpec((nblk, nrow), lambda b, i: (0, 0))],
        out_specs=[pl.BlockSpec((None, tq, 256), lambda b, i: (b, i, 0)),
                   pl.BlockSpec((None, None, 128, tq), lambda b, i: (b, i, 0, 0))],
        out_shape=[jax.ShapeDtypeStruct((B, T, 256), jnp.float32),
                   jax.ShapeDtypeStruct((B, T // tq, 128, tq), jnp.bfloat16)],
        scratch_shapes=[pltpu.VMEM((nblk, tq), jnp.float32)],
        name="nsa_cmp_select",
    )(qn, kkc, vvc, cbias, ovt)


def _nsa_attn_kernel(qn_ref, pen_ref, ocmp_ref, gate_ref, kxs_ref, v1s_ref, kkw_ref, v1w_ref,
                     tabs_ref, tabw_ref, mixg_ref, o_ref, acc_ref, m_ref):
    i = pl.program_id(1)
    tq = qn_ref.shape[0]
    bf16 = jnp.bfloat16
    q4 = _stack_heads(qn_ref[...])
    pen = pen_ref[...]
    lhs_sel = jnp.concatenate([q4, jnp.concatenate([pen] * GROUP_HEADS, axis=0)], axis=1)

    def branch(lhs, k_ref, v_ref, tab_ref, first_tile):
        n_tab = tab_ref.shape[0]
        m_ref[...] = jnp.full(m_ref.shape, NEG, jnp.float32)
        acc_ref[...] = jnp.zeros(acc_ref.shape, jnp.float32)

        def body(j, carry):
            off = pl.multiple_of(j * tq, tq)
            s = _nt_dot(lhs, k_ref[pl.ds(off, tq), :]) + tab_ref[jnp.minimum(i - j, n_tab - 1)]
            m_old = m_ref[...]
            m_new = jnp.maximum(m_old, jnp.max(s, axis=-1, keepdims=True))
            p = jnp.exp(s - m_new)
            pv = jnp.dot(p.astype(bf16), v_ref[pl.ds(off, tq), :], preferred_element_type=jnp.float32)
            acc_ref[...] = jnp.exp(m_old - m_new) * acc_ref[...] + pv
            m_ref[...] = m_new
            return carry

        lax.fori_loop(first_tile, i + 1, body, 0)
        acc = acc_ref[...]
        return acc / jnp.maximum(pltpu.roll(acc, HEAD_DIM, 1), 1e-30)

    o_s = branch(lhs_sel, kxs_ref, v1s_ref, tabs_ref, 0)
    o_w = branch(q4, kkw_ref, v1w_ref, tabw_ref, jnp.maximum(i - WINDOW // tq, 0))
    g = jax.nn.sigmoid(gate_ref[...])
    lo = lax.broadcasted_iota(jnp.int32, (tq, 128), 1) < HEAD_DIM
    ocmp = ocmp_ref[...]
    halves = []
    for pair in range(2):
        comb, gc = [], []
        for h in (2 * pair, 2 * pair + 1):
            rows = slice(h * tq, (h + 1) * tq)
            comb.append(g[:, 3 * h + 1:3 * h + 2] * o_s[rows] + g[:, 3 * h + 2:3 * h + 3] * o_w[rows])
            gc.append(g[:, 3 * h:3 * h + 1])
        both = jnp.where(lo, comb[0], pltpu.roll(comb[1], HEAD_DIM, 1))
        halves.append(both + jnp.where(lo, gc[0], gc[1]) * ocmp[:, pair * 128:(pair + 1) * 128])
    y = jnp.concatenate(halves, axis=1)
    o_ref[...] = y * lax.rsqrt(jnp.mean(y * y, axis=-1, keepdims=True) + EPS) * mixg_ref[...]


def nsa_attention(qn, pen, ocmp, pa, kxs, v1s, kkw, v1w, tab_s, tab_w, mixg):
    B, T, _ = qn.shape
    tq = Q_BLOCK
    tile = lambda w, j: pl.BlockSpec((None, tq, w), lambda b, i: (b, i, j))
    whole = lambda w: pl.BlockSpec((None, T, w), lambda b, i: (b, 0, 0))
    full = lambda a: pl.BlockSpec(a.shape, lambda b, i: (0,) * a.ndim)
    return pl.pallas_call(
        _nsa_attn_kernel,
        grid=(B, T // tq),
        in_specs=[tile(256, 0), tile(128, 0), tile(256, 0), tile(128, 5),
                  whole(256), whole(128), whole(128), whole(128), full(tab_s), full(tab_w), full(mixg)],
        out_specs=tile(256, 0),
        out_shape=jax.ShapeDtypeStruct((B, T, 256), jnp.float32),
        scratch_shapes=[pltpu.VMEM((GROUP_HEADS * tq, 128), jnp.float32),
                        pltpu.VMEM((GROUP_HEADS * tq, 1), jnp.float32)],
        compiler_params=pltpu.CompilerParams(dimension_semantics=("arbitrary", "arbitrary"),
                                             vmem_limit_bytes=VMEM_LIMIT_BYTES),
        name="nsa_attention",
    )(qn, pen, ocmp, pa, kxs, v1s, kkw, v1w, tab_s, tab_w, mixg)


def nsa_bias_tables(t5_table, T):
    H = GROUP_HEADS
    tq = Q_BLOCK
    dmax = WINDOW + 2 * tq
    by_dist = t5_table[t5_bucket(jnp.arange(dmax))].T

    def tile(dist, keep):
        b = by_dist[:, jnp.clip(dist, 0, dmax - 1)]
        return jnp.where(keep, b, NEG).reshape(H * dist.shape[0], dist.shape[1])

    r = jnp.arange(tq)[:, None]
    c = jnp.arange(tq)[None, :]
    d = lambda delta: delta * tq + r - c
    tab_s = jnp.stack([tile(d(k), d(k) >= 0) for k in range(3)])
    tab_w = jnp.stack([tile(d(k), (d(k) >= 0) & (d(k) < WINDOW)) for k in range(WINDOW // tq + 1)])
    nrow = T // CMP_STRIDE
    per_tile = tq // CMP_STRIDE
    m = jnp.arange(2 * nrow)[None, :] - nrow
    dist = r - CMP_STRIDE * m - (CMP_BLOCK - 1)
    base = tile(dist, dist >= 0)
    cbias = jnp.stack([base[:, nrow - per_tile * i: 2 * nrow - per_tile * i] for i in range(T // tq)])
    return tab_s, tab_w, cbias


def nsa_overlap_t(T):
    nrow = T // CMP_STRIDE
    nblk = T // SLC_BLOCK
    ci = np.arange(nrow)[None, :]
    bj = np.arange(nblk)[:, None]
    ov = (ci * CMP_STRIDE < (bj + 1) * SLC_BLOCK) & (ci * CMP_STRIDE + CMP_BLOCK > bj * SLC_BLOCK)
    ov = ov & (ci < nrow - 1)
    return jnp.asarray(ov.astype(np.float32), dtype=jnp.bfloat16)


def nsa_mixer_pallas(pa, q_norm_g, k_norm_g, cmp_pos, cmp_k_w1, cmp_k_w2, cmp_v_w1, cmp_v_w2,
                     tables, ovt, mixg):
    B, T, _ = pa.shape
    bf16 = jnp.bfloat16
    tab_s, tab_w, cbias = tables
    qg = (jnp.tile(q_norm_g, GROUP_HEADS) * HEAD_DIM ** -0.5)[None, :]
    kg2 = jnp.tile(k_norm_g, 2)[None, :]
    seg = jnp.asarray(np.kron(np.eye(GROUP_HEADS), np.ones((HEAD_DIM, HEAD_DIM))), jnp.float32)
    qn, kxs, v1s, kkw, v1w = nsa_prep(pa, qg, kg2, seg)
    nrow = T // CMP_STRIDE
    kch = pa[:, :, 256:320].reshape(B, nrow, CMP_STRIDE * HEAD_DIM)
    vch = pa[:, :, 320:384].reshape(B, nrow, CMP_STRIDE * HEAD_DIM)
    plo = cmp_pos[:CMP_STRIDE].reshape(1, -1)
    phi = cmp_pos[CMP_STRIDE:].reshape(1, -1)
    dup = lambda w: jnp.concatenate([w, w], axis=1).astype(bf16)
    kkc, vvc = nsa_compress(kch, vch, plo, phi, cmp_k_w1.astype(bf16), dup(cmp_k_w2),
                            cmp_v_w1.astype(bf16), dup(cmp_v_w2), kg2)
    n_sel = min(N_SLC, T // SLC_BLOCK)
    ocmp, pen = nsa_cmp_select(qn, kkc, vvc, cbias, ovt, n_sel)
    return nsa_attention(qn, pen, ocmp, pa, kxs, v1s, kkw, v1w, tab_s, tab_w, mixg)


def nsa_mixer_jnp_unused(q, k_cmp, v_cmp, k_slc, v_slc, k_win, v_win, gate_raw, q_norm_g, k_norm_g,
              cmp_pos, cmp_k_w1, cmp_k_w2, cmp_v_w1, cmp_v_w2, t5_table):
    B, T = q.shape[0], q.shape[1]
    dt = q.dtype
    H, dh = GROUP_HEADS, HEAD_DIM
    scale = dh ** -0.5
    q = rms_norm(q.reshape(B, T, H, dh), q_norm_g) * scale
    k_slc = rms_norm(k_slc, k_norm_g)
    k_win = rms_norm(k_win, k_norm_g)
    t = jnp.arange(T)
    nc = T // CMP_STRIDE - 1

    def compress(kv, w1, w2):
        ch = kv.reshape(B, T // CMP_STRIDE, CMP_STRIDE, dh)
        blocks = jnp.concatenate([ch[:, :-1], ch[:, 1:]], axis=2) + cmp_pos
        hid = jax.nn.silu(blocks.reshape(B, nc, CMP_BLOCK * dh) @ w1)
        return hid @ w2

    kc = rms_norm(compress(k_cmp, cmp_k_w1, cmp_k_w2), k_norm_g)
    vc = compress(v_cmp, cmp_v_w1, cmp_v_w2)
    cmp_end = jnp.arange(nc) * CMP_STRIDE + CMP_BLOCK - 1
    cmp_dist = t[:, None] - cmp_end[None, :]
    cmp_bias = t5_table[t5_bucket(cmp_dist)].transpose(2, 0, 1)
    cmp_logits = jnp.einsum("bthd,bnd->bhtn", q, kc) + cmp_bias
    p_cmp = masked_softmax(cmp_logits, cmp_dist >= 0)
    o_cmp = jnp.einsum("bhtn,bnd->bthd", p_cmp.astype(dt), vc)

    nblk = T // SLC_BLOCK
    n_sel = min(N_SLC, nblk)
    ci = np.arange(nc)[:, None]
    bj = np.arange(nblk)[None, :]
    overlap = ((ci * CMP_STRIDE < (bj + 1) * SLC_BLOCK)
               & (ci * CMP_STRIDE + CMP_BLOCK > bj * SLC_BLOCK)).astype(np.float32)
    score = jnp.sum(p_cmp, axis=1) @ jnp.asarray(overlap)
    cur = (t // SLC_BLOCK)[:, None]
    blk = jnp.arange(nblk)[None, :]
    forced = (blk == 0) | ((cur - blk >= 0) & (cur - blk < N_LOCAL_SLC))
    score = jnp.where(forced, FORCE, score)
    score = jnp.where(blk <= cur, score, -FORCE)
    _, sel_idx = lax.top_k(score, n_sel)

    ks_blocks = k_slc.reshape(B, nblk, SLC_BLOCK, dh)
    vs_blocks = v_slc.reshape(B, nblk, SLC_BLOCK, dh)
    kw_p = jnp.pad(k_win, ((0, 0), (WINDOW, 0), (0, 0)))
    vw_p = jnp.pad(v_win, ((0, 0), (WINDOW, 0), (0, 0)))
    nq = T // Q_BLOCK
    J = n_sel * SLC_BLOCK

    def qblock(i):
        s = i * Q_BLOCK
        qb = lax.dynamic_slice_in_dim(q, s, Q_BLOCK, axis=1)
        tq = s + jnp.arange(Q_BLOCK)
        kwb = lax.dynamic_slice_in_dim(kw_p, s, WINDOW + Q_BLOCK, axis=1)
        vwb = lax.dynamic_slice_in_dim(vw_p, s, WINDOW + Q_BLOCK, axis=1)
        kpos = s - WINDOW + jnp.arange(WINDOW + Q_BLOCK)
        wd = tq[:, None] - kpos[None, :]
        wmask = (wd >= 0) & (wd < WINDOW) & (kpos[None, :] >= 0)
        wl = jnp.einsum("bqhd,bkd->bhqk", qb, kwb) + t5_table[t5_bucket(wd)].transpose(2, 0, 1)
        o_w = jnp.einsum("bhqk,bkd->bqhd", masked_softmax(wl, wmask).astype(dt), vwb)
        idx = lax.dynamic_slice_in_dim(sel_idx, s, Q_BLOCK, axis=1)
        ksel = jax.vmap(lambda kb, ib: kb[ib])(ks_blocks, idx)
        vsel = jax.vmap(lambda vb, ib: vb[ib])(vs_blocks, idx)
        spos = idx[..., None] * SLC_BLOCK + jnp.arange(SLC_BLOCK)
        sd = (tq[None, :, None, None] - spos).reshape(B, Q_BLOCK, J)
        sl = jnp.einsum("bqhd,bqnsd->bhqns", qb, ksel).reshape(B, H, Q_BLOCK, J)
        sl = sl + t5_table[t5_bucket(sd)].transpose(0, 3, 1, 2)
        ps = masked_softmax(sl, (sd >= 0)[:, None]).astype(dt)
        o_s = jnp.einsum("bhqj,bqjd->bqhd", ps, vsel.reshape(B, Q_BLOCK, J, dh))
        return o_w, o_s

    o_win, o_slc = lax.map(qblock, jnp.arange(nq))
    o_win = o_win.transpose(1, 0, 2, 3, 4).reshape(B, T, H, dh)
    o_slc = o_slc.transpose(1, 0, 2, 3, 4).reshape(B, T, H, dh)
    g = jax.nn.sigmoid(gate_raw.reshape(B, T, H, 3))
    o = g[..., 0:1] * o_cmp + g[..., 1:2] * o_slc + g[..., 2:3] * o_win
    return o.reshape(B, T, GROUP_WIDTH)


def short_conv_mixer(b_gate, c_gate, xs, conv_w):
    return b_gate * causal_dwconv(c_gate * xs, conv_w)


def to_chunks(a, L):
    B, T, H = a.shape[:3]
    a = a.reshape((B, T // L, L, H) + a.shape[3:])
    return a.transpose((1, 0, 3, 2) + tuple(range(4, a.ndim)))


def from_chunks(a):
    N, B, H, L, d = a.shape
    return a.transpose(1, 0, 3, 2, 4).reshape(B, N * L, H, d)


def chunk_gated_delta_rule(q, k, v, g, beta):
    f32 = jnp.float32
    L = GDN_CHUNK
    q, k, v = (to_chunks(a.astype(f32), L) for a in (q, k, v))
    g = to_chunks(g.astype(f32), L)
    beta = to_chunks(beta.astype(f32), L)
    gc = jnp.cumsum(g, axis=-1)
    incl = jnp.tril(jnp.ones((L, L), bool))
    strict = jnp.tril(jnp.ones((L, L), bool), -1)
    seg = jnp.exp(jnp.where(incl, gc[..., :, None] - gc[..., None, :], -jnp.inf))
    kb = k * beta[..., None]
    a_mat = jnp.where(strict, jnp.einsum("nbhid,nbhjd->nbhij", kb, k) * seg, 0.0) + jnp.eye(L, dtype=f32)
    rhs = jnp.concatenate([v * beta[..., None], kb * jnp.exp(gc)[..., None]], axis=-1)
    sol = lax.linalg.triangular_solve(a_mat, rhs, left_side=True, lower=True, unit_diagonal=True)
    u, w = sol[..., :HEAD_DIM], sol[..., HEAD_DIM:]
    qk = jnp.einsum("nbhid,nbhjd->nbhij", q, k) * seg
    q_dec = q * jnp.exp(gc)[..., None]
    k_dec = k * jnp.exp(gc[..., -1:] - gc)[..., None]
    g_tot = jnp.exp(gc[..., -1])

    def step(S, inp):
        u_c, w_c, qd_c, kd_c, qk_c, gt_c = inp
        v_new = u_c - jnp.einsum("bhlk,bhkv->bhlv", w_c, S)
        o = jnp.einsum("bhlk,bhkv->bhlv", qd_c, S) + jnp.einsum("bhij,bhjv->bhiv", qk_c, v_new)
        S = S * gt_c[..., None, None] + jnp.einsum("bhlk,bhlv->bhkv", kd_c, v_new)
        return S, o

    S0 = jnp.zeros(u.shape[1:3] + (HEAD_DIM, HEAD_DIM), f32)
    _, o = lax.scan(step, S0, (u, w, q_dec, k_dec, qk, g_tot))
    return from_chunks(o)


def gated_deltanet(q, k, v, beta_raw, alpha_raw, z, conv_w, A_log, dt_bias, norm_g):
    B, T = q.shape[0], q.shape[1]
    qkv = jax.nn.silu(causal_dwconv(jnp.concatenate([q, k, v], axis=-1), conv_w))
    q, k, v = (a.reshape(B, T, GROUP_HEADS, HEAD_DIM) for a in jnp.split(qkv, 3, axis=-1))
    q = l2_norm(q) * HEAD_DIM ** -0.5
    k = l2_norm(k)
    beta = jax.nn.sigmoid(beta_raw)
    g = -jnp.exp(A_log) * jax.nn.softplus(alpha_raw + dt_bias)
    o = chunk_gated_delta_rule(q, k, v, g, beta).astype(z.dtype)
    o = rms_norm(o, norm_g) * jax.nn.silu(z.reshape(B, T, GROUP_HEADS, HEAD_DIM))
    return o.reshape(B, T, GROUP_WIDTH)


def mlstm_chunkwise(q, k, v, i_pre, f_pre):
    f32 = jnp.float32
    L = MLSTM_CHUNK
    q, k, v = (to_chunks(a.astype(f32), L) for a in (q, k, v))
    log_i = to_chunks(i_pre.astype(f32), L)
    log_f = to_chunks(jax.nn.log_sigmoid(f_pre.astype(f32)), L)
    b = jnp.cumsum(log_f, axis=-1)
    incl = jnp.tril(jnp.ones((L, L), bool))
    log_w = jnp.where(incl, b[..., :, None] - b[..., None, :] + log_i[..., None, :], -jnp.inf)
    m_intra = jnp.max(log_w, axis=-1)
    log_w_end = b[..., -1:] - b + log_i
    m_end = jnp.max(log_w_end, axis=-1)
    qk = jnp.einsum("nbhid,nbhjd->nbhij", q, k)

    def step(carry, inp):
        C, n, m = carry
        q_c, k_c, v_c, b_c, lw_c, mi_c, lwe_c, me_c, qk_c = inp
        log_inter = b_c + m[..., None]
        m_t = jnp.maximum(log_inter, mi_c)
        w_inter = jnp.exp(log_inter - m_t)
        s = qk_c * jnp.exp(lw_c - m_t[..., None])
        num = (w_inter[..., None] * jnp.einsum("bhlk,bhkv->bhlv", q_c, C)
               + jnp.einsum("bhij,bhjv->bhiv", s, v_c))
        den = w_inter * jnp.einsum("bhlk,bhk->bhl", q_c, n) + jnp.sum(s, axis=-1)
        h = num / jnp.maximum(jnp.abs(den), jnp.exp(-m_t))[..., None]
        m_new = jnp.maximum(b_c[..., -1] + m, me_c)
        w_old = jnp.exp(b_c[..., -1] + m - m_new)
        w_new = jnp.exp(lwe_c - m_new[..., None])
        C = w_old[..., None, None] * C + jnp.einsum("bhl,bhlk,bhlv->bhkv", w_new, k_c, v_c)
        n = w_old[..., None] * n + jnp.einsum("bhl,bhlk->bhk", w_new, k_c)
        return (C, n, m_new), h

    Bsz, H = q.shape[1], q.shape[2]
    carry0 = (jnp.zeros((Bsz, H, HEAD_DIM, HEAD_DIM), f32), jnp.zeros((Bsz, H, HEAD_DIM), f32),
              jnp.zeros((Bsz, H), f32))
    _, h = lax.scan(step, carry0, (q, k, v, b, log_w, m_intra, log_w_end, m_end, qk))
    return from_chunks(h)


def mlstm_mixer(q, k, v, i_raw, f_raw, o_raw, f_bias, norm_g):
    B, T = q.shape[0], q.shape[1]
    heads = lambda a: a.reshape(B, T, GROUP_HEADS, HEAD_DIM)
    h = mlstm_chunkwise(heads(q), heads(k) * HEAD_DIM ** -0.5, heads(v), i_raw, f_raw + f_bias)
    h = rms_norm(h.astype(q.dtype), norm_g) * jax.nn.sigmoid(heads(o_raw))
    return h.reshape(B, T, GROUP_WIDTH)


CONV_CARRY_ROWS = 8


def _shifted(cat, shift, rows):
    return pltpu.roll(cat, shift, 0)[CONV_CARRY_ROWS:CONV_CARRY_ROWS + rows]


def _sconv_kernel(pb_ref, w_ref, g_ref, o_ref, carry_ref):
    rows = pb_ref.shape[0]

    @pl.when(pl.program_id(1) == 0)
    def _():
        carry_ref[...] = jnp.zeros(carry_ref.shape, jnp.float32)

    u = pb_ref[:, 256:512] * pb_ref[:, 512:768]
    cat = jnp.concatenate([carry_ref[...], u], axis=0)
    conv = u * w_ref[2:3, :] + _shifted(cat, 1, rows) * w_ref[1:2, :] + _shifted(cat, 2, rows) * w_ref[0:1, :]
    carry_ref[...] = u[rows - CONV_CARRY_ROWS:rows]
    y = pb_ref[:, 0:256] * conv
    o_ref[...] = y * lax.rsqrt(jnp.mean(y * y, axis=-1, keepdims=True) + EPS) * g_ref[...]


def short_conv_pallas(pb, conv_w, mixg):
    B, T, _ = pb.shape
    rows = 512
    w = jnp.pad(conv_w, ((0, 8 - conv_w.shape[0]), (0, 0)))
    return pl.pallas_call(
        _sconv_kernel,
        grid=(B, T // rows),
        in_specs=[pl.BlockSpec((None, rows, 768), lambda b, t: (b, t, 0)),
                  pl.BlockSpec((8, 256), lambda b, t: (0, 0)),
                  pl.BlockSpec((1, 256), lambda b, t: (0, 0))],
        out_specs=pl.BlockSpec((None, rows, 256), lambda b, t: (b, t, 0)),
        out_shape=jax.ShapeDtypeStruct((B, T, 256), jnp.float32),
        scratch_shapes=[pltpu.VMEM((CONV_CARRY_ROWS, 256), jnp.float32)],
        compiler_params=pltpu.CompilerParams(dimension_semantics=("arbitrary", "arbitrary")),
        name="short_conv",
    )(pb, w, mixg)


def _stack(a, hm):
    return jnp.concatenate([a] * GROUP_HEADS, axis=0) * hm


def _unstack(y, L):
    return y[0:L] + y[L:2 * L] + y[2 * L:3 * L] + y[3 * L:4 * L]


def _tn_dot(a, b):
    return lax.dot_general(a, b, (((0,), (0,)), ((), ())), preferred_element_type=jnp.float32)


def _dot_hi(a, b):
    return jnp.dot(a, b, precision=HIGHEST, preferred_element_type=jnp.float32)


def _softplus(x):
    return jnp.maximum(x, 0.0) + jnp.log(1.0 + jnp.exp(-jnp.abs(x)))


def _block_masks(L):
    n = GROUP_HEADS * L
    r = lax.broadcasted_iota(jnp.int32, (n, n), 0)
    c = lax.broadcasted_iota(jnp.int32, (n, n), 1)
    same = (r // L) == (c // L)
    return same & (c <= r), same & (c < r)


def _head_norm(o, hm):
    return o * lax.rsqrt(_dot_hi(o * o, hm) * (1.0 / HEAD_DIM) + EPS)


def _head_l2(o, hm):
    return o * lax.rsqrt(_dot_hi(o * o, hm) + EPS)


def _gdn_kernel(pc_ref, cw_ref, hp_ref, hm_ref, tri_ref, eb_ref, ea_ref, o_ref, s_ref, carry_ref):
    L = pc_ref.shape[0]
    f32, bf16 = jnp.float32, jnp.bfloat16

    @pl.when(pl.program_id(1) == 0)
    def _():
        s_ref[...] = jnp.zeros(s_ref.shape, f32)
        carry_ref[...] = jnp.zeros(carry_ref.shape, f32)

    hm = hm_ref[...]
    x = pc_ref[:, 0:768]
    cat = jnp.concatenate([carry_ref[...], x], axis=0)
    conv = x * cw_ref[3:4, :]
    for s in (1, 2, 3):
        conv = conv + _shifted(cat, s, L) * cw_ref[3 - s:4 - s, :]
    carry_ref[...] = x[L - CONV_CARRY_ROWS:L]
    qkv = conv * jax.nn.sigmoid(conv)
    q = _head_l2(qkv[:, 0:256], hm) * HEAD_DIM ** -0.5
    k = _head_l2(qkv[:, 256:512], hm)
    v = qkv[:, 512:768]
    tail = pc_ref[:, 1024:1152]
    beta = jax.nn.sigmoid(_dot_hi(tail, eb_ref[...]))
    g = hp_ref[0:1, :] * _softplus(_dot_hi(tail, ea_ref[...]) + hp_ref[1:2, :])
    gc = _dot_hi(tri_ref[...], g)
    egc = jnp.exp(gc)
    kb = k * beta
    ks, kbs, qs, gs = (_stack(a, hm) for a in (k, kb, q, gc))
    avg = jnp.full((GROUP_HEADS * L, GROUP_HEADS * L), 1.0 / HEAD_DIM, f32)
    gcol = _dot_hi(gs, avg)
    grow = lax.dot_general(avg, gs, (((1,), (1,)), ((), ())), precision=HIGHEST,
                           preferred_element_type=f32)
    incl, strict = _block_masks(L)
    seg = jnp.exp(jnp.where(incl, gcol - grow, NEG))
    ksb = ks.astype(bf16)
    a_mat = jnp.where(strict, _nt_dot(kbs.astype(bf16), ksb) * seg, 0.0)
    qk = _nt_dot(qs.astype(bf16), ksb) * seg
    xs = jnp.concatenate([_stack(v * beta, hm), _stack(kb * egc, hm)], axis=1)
    xs = xs - _dot_hi(a_mat, xs)
    pw = a_mat
    for _ in range(int(math.log2(L)) - 1):
        pw = _dot_hi(pw, pw)
        xs = xs + _dot_hi(pw, xs)
    u = _unstack(xs[:, 0:256], L)
    w = _unstack(xs[:, 256:512], L)
    state = s_ref[...]
    sb = state.astype(bf16)
    v_new = u - jnp.dot(w.astype(bf16), sb, preferred_element_type=f32)
    o = jnp.dot((q * egc).astype(bf16), sb, preferred_element_type=f32)
    o = o + _unstack(jnp.dot(qk.astype(bf16), _stack(v_new, hm).astype(bf16), preferred_element_type=f32), L)
    k_dec = k * jnp.exp(gc[L - 1:L, :] - gc)
    s_ref[...] = state * egc[L - 1:L, :] + _tn_dot(k_dec.astype(bf16), v_new.astype(bf16)) * hm
    z = pc_ref[:, 768:1024]
    o_ref[...] = _head_norm(o, hm) * hp_ref[2:3, :] * (z * jax.nn.sigmoid(z))


def _expand_matrix(first_lane):
    e = np.zeros((128, GROUP_WIDTH), np.float32)
    for h in range(GROUP_HEADS):
        e[first_lane + h, h * HEAD_DIM:(h + 1) * HEAD_DIM] = 1.0
    return jnp.asarray(e)


def _head_mask():
    return jnp.asarray(np.kron(np.eye(GROUP_HEADS), np.ones((HEAD_DIM, HEAD_DIM))), jnp.float32)


def _per_head_rows(*vecs):
    rows = [jnp.repeat(v, HEAD_DIM) if v.shape[0] == GROUP_HEADS else jnp.tile(v, GROUP_HEADS) for v in vecs]
    return jnp.pad(jnp.stack(rows), ((0, 8 - len(rows)), (0, 0)))


def gated_deltanet_pallas(pc, conv_w, A_log, dt_bias, norm_g):
    B, T, width = pc.shape
    L = GDN_CHUNK
    n = GROUP_HEADS * L
    hp = _per_head_rows(-jnp.exp(A_log), dt_bias, norm_g)
    cw = jnp.pad(conv_w, ((0, 8 - conv_w.shape[0]), (0, 0)))
    tri = jnp.asarray(np.tril(np.ones((L, L), np.float32)))
    const = lambda a: pl.BlockSpec(a.shape, lambda b, t: (0, 0))
    consts = (cw, hp, _head_mask(), tri, _expand_matrix(0), _expand_matrix(GROUP_HEADS))
    return pl.pallas_call(
        _gdn_kernel,
        grid=(B, T // L),
        in_specs=[pl.BlockSpec((None, L, width), lambda b, t: (b, t, 0))] + [const(a) for a in consts],
        out_specs=pl.BlockSpec((None, L, 256), lambda b, t: (b, t, 0)),
        out_shape=jax.ShapeDtypeStruct((B, T, 256), jnp.float32),
        scratch_shapes=[pltpu.VMEM((n, n), jnp.float32), pltpu.VMEM((CONV_CARRY_ROWS, 768), jnp.float32)],
        compiler_params=pltpu.CompilerParams(dimension_semantics=("arbitrary", "arbitrary")),
        name="gated_deltanet",
    )(pc, *consts)


def _mlstm_kernel(pd_ref, hp_ref, hm_ref, tri_ref, ei_ref, ef_ref, o_ref, c_ref, n_ref, m_ref):
    L = pd_ref.shape[0]
    f32, bf16 = jnp.float32, jnp.bfloat16

    @pl.when(pl.program_id(1) == 0)
    def _():
        c_ref[...] = jnp.zeros(c_ref.shape, f32)
        n_ref[...] = jnp.zeros(n_ref.shape, f32)
        m_ref[...] = jnp.zeros(m_ref.shape, f32)

    hm = hm_ref[...]
    q = pd_ref[:, 0:256]
    k = pd_ref[:, 256:512] * HEAD_DIM ** -0.5
    v = pd_ref[:, 512:768]
    tail = pd_ref[:, 1024:1152]
    log_i = _dot_hi(tail, ei_ref[...])
    log_f = -_softplus(-(_dot_hi(tail, ef_ref[...]) + hp_ref[0:1, :]))
    b = _dot_hi(tri_ref[...], log_f)
    avg = jnp.full((GROUP_HEADS * L, GROUP_HEADS * L), 1.0 / HEAD_DIM, f32)
    bcol = _dot_hi(_stack(b, hm), avg)
    arow = lax.dot_general(avg, _stack(b - log_i, hm), (((1,), (1,)), ((), ())), precision=HIGHEST,
                           preferred_element_type=f32)
    incl, _ = _block_masks(L)
    log_w = jnp.where(incl, bcol - arow, NEG)

    def to_x(col):
        return _unstack(col * hm, L)

    m_intra = to_x(jnp.max(log_w, axis=-1, keepdims=True))
    b_last = b[L - 1:L, :]
    log_w_end = b_last - b + log_i
    m_end = jnp.max(log_w_end, axis=0, keepdims=True)
    m_prev = m_ref[...]
    log_inter = b + m_prev
    m_t = jnp.maximum(log_inter, m_intra)
    w_inter = jnp.exp(log_inter - m_t)
    qk = _nt_dot(_stack(q, hm).astype(bf16), _stack(k, hm).astype(bf16))
    s = qk * jnp.exp(log_w - _dot_hi(_stack(m_t, hm), avg))
    cb = c_ref[...].astype(bf16)
    num = w_inter * jnp.dot(q.astype(bf16), cb, preferred_element_type=f32)
    num = num + _unstack(jnp.dot(s.astype(bf16), _stack(v, hm).astype(bf16), preferred_element_type=f32), L)
    den = w_inter * _dot_hi(q * n_ref[...], hm) + to_x(jnp.sum(s, axis=-1, keepdims=True))
    h = num / jnp.maximum(jnp.abs(den), jnp.exp(-m_t))
    m_new = jnp.maximum(b_last + m_prev, m_end)
    w_old = jnp.exp(b_last + m_prev - m_new)
    kw = k * jnp.exp(log_w_end - m_new)
    c_ref[...] = w_old * c_ref[...] + _tn_dot(kw.astype(bf16), v.astype(bf16)) * hm
    n_ref[...] = w_old * n_ref[...] + jnp.sum(kw, axis=0, keepdims=True)
    m_ref[...] = m_new
    o_ref[...] = _head_norm(h, hm) * hp_ref[1:2, :] * jax.nn.sigmoid(pd_ref[:, 768:1024])


def mlstm_pallas(pd, f_bias, norm_g):
    B, T, width = pd.shape
    L = MLSTM_CHUNK
    n = GROUP_HEADS * L
    hp = _per_head_rows(f_bias, norm_g)
    tri = jnp.asarray(np.tril(np.ones((L, L), np.float32)))
    const = lambda a: pl.BlockSpec(a.shape, lambda b, t: (0, 0))
    consts = (hp, _head_mask(), tri, _expand_matrix(0), _expand_matrix(GROUP_HEADS))
    return pl.pallas_call(
        _mlstm_kernel,
        grid=(B, T // L),
        in_specs=[pl.BlockSpec((None, L, width), lambda b, t: (b, t, 0))] + [const(a) for a in consts],
        out_specs=pl.BlockSpec((None, L, 256), lambda b, t: (b, t, 0)),
        out_shape=jax.ShapeDtypeStruct((B, T, 256), jnp.float32),
        scratch_shapes=[pltpu.VMEM((n, n), jnp.float32), pltpu.VMEM((1, 256), jnp.float32),
                        pltpu.VMEM((1, 256), jnp.float32)],
        compiler_params=pltpu.CompilerParams(dimension_semantics=("arbitrary", "arbitrary")),
        name="mlstm",
    )(pd, *consts)


def _split_bf16(a, parts):
    out, rest = [], a
    for _ in range(parts):
        piece = rest.astype(jnp.bfloat16)
        out.append(piece)
        rest = rest - piece.astype(jnp.float32)
    return out


def _dot_sel(a, sel, parts=3):
    return sum(jnp.dot(p, sel, preferred_element_type=jnp.float32) for p in _split_bf16(a, parts))


def _sel_dot(sel, a, parts=3):
    return sum(jnp.dot(sel, p, preferred_element_type=jnp.float32) for p in _split_bf16(a, parts))


def _tile_rows(a):
    return jnp.concatenate([a] * GROUP_HEADS, axis=0)


def _head_norm(o, hmb):
    return o * lax.rsqrt(_dot_sel(o * o, hmb, parts=2) * (1.0 / HEAD_DIM) + EPS)


def _head_l2(o, hmb):
    return o * lax.rsqrt(_dot_sel(o * o, hmb, parts=2) + EPS)


def _chunk_masks(L):
    n = GROUP_HEADS * L
    r = np.arange(n)[:, None]
    c = np.arange(n)[None, :]
    same = (r // L) == (c // L)
    i, j = r % L, c % L
    masks = []
    s = 1
    while s < L:
        masks.append(same & (i // (2 * s) == j // (2 * s)) & (i % (2 * s) >= s) & (j % (2 * s) < s))
        s *= 2
    masks += [r == c, same & (j <= i), same & (j < i)]
    return jnp.asarray(np.stack(masks).astype(np.float32))


def _gdn_kernel(pc_ref, cw_ref, hp_ref, hm_ref, tri_ref, eb_ref, ea_ref, lv_ref, o_ref, s_ref, carry_ref):
    L = pc_ref.shape[0]
    f32, bf16 = jnp.float32, jnp.bfloat16
    n_lev = lv_ref.shape[0] - 3

    @pl.when(pl.program_id(1) == 0)
    def _():
        s_ref[...] = jnp.zeros(s_ref.shape, f32)
        carry_ref[...] = jnp.zeros(carry_ref.shape, f32)

    hm = hm_ref[...]
    hmb = hm.astype(bf16)
    x = pc_ref[:, 0:768]
    cat = jnp.concatenate([carry_ref[...], x], axis=0)
    conv = x * cw_ref[3:4, :]
    for s in (1, 2, 3):
        conv = conv + _shifted(cat, s, L) * cw_ref[3 - s:4 - s, :]
    carry_ref[...] = x[L - CONV_CARRY_ROWS:L]
    qkv = conv * jax.nn.sigmoid(conv)
    q = _head_l2(qkv[:, 0:256], hmb) * HEAD_DIM ** -0.5
    k = _head_l2(qkv[:, 256:512], hmb)
    v = qkv[:, 512:768]
    tail = pc_ref[:, 1024:1152]
    beta = jax.nn.sigmoid(_dot_sel(tail, eb_ref[...]))
    g = hp_ref[0:1, :] * _softplus(_dot_sel(tail, ea_ref[...]) + hp_ref[1:2, :])
    gc = _sel_dot(tri_ref[...], g)
    egc = jnp.exp(gc)
    kb = k * beta
    gct = _tile_rows(gc)
    diff = gct - (gct * hm).T
    seg = jnp.exp(jnp.where(lv_ref[n_lev + 1] > 0.5, diff, NEG))
    ksb = (_tile_rows(k) * hm).astype(bf16)
    a_mat = _nt_dot((_tile_rows(kb) * hm).astype(bf16), ksb) * seg * lv_ref[n_lev + 2]
    qk = _nt_dot((_tile_rows(q) * hm).astype(bf16), ksb) * seg
    t_inv = lv_ref[n_lev] - a_mat * lv_ref[0]
    for lev in range(1, n_lev):
        tb = t_inv.astype(bf16)
        te = jnp.dot(tb, (a_mat * lv_ref[lev]).astype(bf16), preferred_element_type=f32)
        t_inv = t_inv - jnp.dot(te.astype(bf16), tb, preferred_element_type=f32)
    rhs = jnp.concatenate([_tile_rows(v * beta) * hm, _tile_rows(kb * egc) * hm], axis=1)
    sol = jnp.dot(t_inv.astype(bf16), rhs.astype(bf16), preferred_element_type=f32)
    u = _unstack(sol[:, 0:256], L)
    w = _unstack(sol[:, 256:512], L)
    state = s_ref[...]
    sb = state.astype(bf16)
    v_new = u - jnp.dot(w.astype(bf16), sb, preferred_element_type=f32)
    o = jnp.dot((q * egc).astype(bf16), sb, preferred_element_type=f32)
    o = o + _unstack(jnp.dot(qk.astype(bf16), (_tile_rows(v_new) * hm).astype(bf16),
                             preferred_element_type=f32), L)
    k_dec = k * jnp.exp(gc[L - 1:L, :] - gc)
    s_ref[...] = state * egc[L - 1:L, :] + _tn_dot(k_dec.astype(bf16), v_new.astype(bf16)) * hm
    z = pc_ref[:, 768:1024]
    o_ref[...] = _head_norm(o, hmb) * hp_ref[2:3, :] * (z * jax.nn.sigmoid(z))


def _expand_matrix(first_lane):
    e = np.zeros((128, GROUP_WIDTH), np.float32)
    for h in range(GROUP_HEADS):
        e[first_lane + h, h * HEAD_DIM:(h + 1) * HEAD_DIM] = 1.0
    return jnp.asarray(e, jnp.bfloat16)


def _chunk_consts(L):
    tri = jnp.asarray(np.tril(np.ones((L, L), np.float32)), jnp.bfloat16)
    return _head_mask(), tri, _expand_matrix(0), _expand_matrix(GROUP_HEADS), _chunk_masks(L)


def _const_spec(a):
    return pl.BlockSpec(a.shape, lambda b, t: (0,) * a.ndim)


def gated_deltanet_pallas(pc, conv_w, A_log, dt_bias, norm_g):
    B, T, width = pc.shape
    L = GDN_CHUNK
    n = GROUP_HEADS * L
    hp = _per_head_rows(-jnp.exp(A_log), dt_bias, norm_g)
    cw = jnp.pad(conv_w, ((0, 8 - conv_w.shape[0]), (0, 0)))
    consts = (cw, hp) + _chunk_consts(L)
    return pl.pallas_call(
        _gdn_kernel,
        grid=(B, T // L),
        in_specs=[pl.BlockSpec((None, L, width), lambda b, t: (b, t, 0))] + [_const_spec(a) for a in consts],
        out_specs=pl.BlockSpec((None, L, 256), lambda b, t: (b, t, 0)),
        out_shape=jax.ShapeDtypeStruct((B, T, 256), jnp.float32),
        scratch_shapes=[pltpu.VMEM((n, n), jnp.float32), pltpu.VMEM((CONV_CARRY_ROWS, 768), jnp.float32)],
        compiler_params=pltpu.CompilerParams(dimension_semantics=("arbitrary", "arbitrary")),
        name="gated_deltanet",
    )(pc, *consts)


def _mlstm_kernel(pd_ref, hp_ref, hm_ref, tri_ref, ei_ref, ef_ref, lv_ref, o_ref, c_ref, n_ref, m_ref):
    L = pd_ref.shape[0]
    f32, bf16 = jnp.float32, jnp.bfloat16
    n_lev = lv_ref.shape[0] - 3

    @pl.when(pl.program_id(1) == 0)
    def _():
        c_ref[...] = jnp.zeros(c_ref.shape, f32)
        n_ref[...] = jnp.zeros(n_ref.shape, f32)
        m_ref[...] = jnp.zeros(m_ref.shape, f32)

    hm = hm_ref[...]
    hmb = hm.astype(bf16)
    q = pd_ref[:, 0:256]
    k = pd_ref[:, 256:512] * HEAD_DIM ** -0.5
    v = pd_ref[:, 512:768]
    tail = pd_ref[:, 1024:1152]
    log_i = _dot_sel(tail, ei_ref[...])
    log_f = -_softplus(-(_dot_sel(tail, ef_ref[...]) + hp_ref[0:1, :]))
    b = _sel_dot(tri_ref[...], log_f)
    log_w = jnp.where(lv_ref[n_lev + 1] > 0.5, _tile_rows(b) - (_tile_rows(b - log_i) * hm).T, NEG)

    def to_x(col):
        return _unstack(col * hm, L)

    m_intra = to_x(jnp.max(log_w, axis=-1, keepdims=True))
    b_last = b[L - 1:L, :]
    log_w_end = b_last - b + log_i
    m_end = jnp.max(log_w_end, axis=0, keepdims=True)
    m_prev = m_ref[...]
    log_inter = b + m_prev
    m_t = jnp.maximum(log_inter, m_intra)
    w_inter = jnp.exp(log_inter - m_t)
    qk = _nt_dot((_tile_rows(q) * hm).astype(bf16), (_tile_rows(k) * hm).astype(bf16))
    s = qk * jnp.exp(log_w - _tile_rows(m_t))
    cb = c_ref[...].astype(bf16)
    num = w_inter * jnp.dot(q.astype(bf16), cb, preferred_element_type=f32)
    num = num + _unstack(jnp.dot(s.astype(bf16), (_tile_rows(v) * hm).astype(bf16),
                                 preferred_element_type=f32), L)
    den = w_inter * _dot_sel(q * n_ref[...], hmb, parts=2) + to_x(jnp.sum(s, axis=-1, keepdims=True))
    h = num / jnp.maximum(jnp.abs(den), jnp.exp(-m_t))
    m_new = jnp.maximum(b_last + m_prev, m_end)
    w_old = jnp.exp(b_last + m_prev - m_new)
    kw = k * jnp.exp(log_w_end - m_new)
    c_ref[...] = w_old * c_ref[...] + _tn_dot(kw.astype(bf16), v.astype(bf16)) * hm
    n_ref[...] = w_old * n_ref[...] + jnp.sum(kw, axis=0, keepdims=True)
    m_ref[...] = m_new
    o_ref[...] = _head_norm(h, hmb) * hp_ref[1:2, :] * jax.nn.sigmoid(pd_ref[:, 768:1024])


def mlstm_pallas(pd, f_bias, norm_g):
    B, T, width = pd.shape
    L = MLSTM_CHUNK
    n = GROUP_HEADS * L
    consts = (_per_head_rows(f_bias, norm_g),) + _chunk_consts(L)
    return pl.pallas_call(
        _mlstm_kernel,
        grid=(B, T // L),
        in_specs=[pl.BlockSpec((None, L, width), lambda b, t: (b, t, 0))] + [_const_spec(a) for a in consts],
        out_specs=pl.BlockSpec((None, L, 256), lambda b, t: (b, t, 0)),
        out_shape=jax.ShapeDtypeStruct((B, T, 256), jnp.float32),
        scratch_shapes=[pltpu.VMEM((n, n), jnp.float32), pltpu.VMEM((1, 256), jnp.float32),
                        pltpu.VMEM((1, 256), jnp.float32)],
        compiler_params=pltpu.CompilerParams(dimension_semantics=("arbitrary", "arbitrary")),
        name="mlstm",
    )(pd, *consts)


NSA_KEY_TILE = 256


def _nsa_attn_kernel(qnt_ref, pen_ref, ocmp_ref, gate_ref, kxs_ref, v1s_ref, kkw_ref, v1w_ref,
                     tabs_ref, tabw_ref, mixg_ref, o_ref, acc_ref, m_ref, sa_ref, sb_ref):
    i = pl.program_id(1)
    tq = o_ref.shape[0]
    kt = tabs_ref.shape[1]
    f32, bf16 = jnp.float32, jnp.bfloat16
    qt = qnt_ref[...]
    lo = lax.broadcasted_iota(jnp.int32, (128, tq), 0) < HEAD_DIM
    zero = jnp.zeros((), qt.dtype)
    q4t = jnp.concatenate([jnp.where(lo if h % 2 == 0 else ~lo, qt[128 * (h // 2):128 * (h // 2) + 128], zero)
                           for h in range(GROUP_HEADS)], axis=1)
    rhs_sel = jnp.concatenate([q4t, jnp.concatenate([pen_ref[...]] * GROUP_HEADS, axis=1)], axis=0)

    def branch(rhs, k_ref, v_ref, tab_ref, first_tile, last_tile):
        n_tab = tab_ref.shape[0] - 1
        m_ref[...] = jnp.full(m_ref.shape, NEG, f32)
        acc_ref[...] = jnp.zeros(acc_ref.shape, f32)

        def key_offset(j):
            return pl.multiple_of(jnp.minimum(j, last_tile) * kt, kt)

        def scores(j):
            delta = jnp.where(j > last_tile, n_tab, jnp.minimum(i - j * (kt // tq), n_tab - 1))
            return jnp.dot(k_ref[pl.ds(key_offset(j), kt), :], rhs, preferred_element_type=f32) + tab_ref[delta]

        def absorb(s, j):
            m_old = m_ref[...]
            m_new = jnp.maximum(m_old, jnp.max(s, axis=0, keepdims=True))
            p = jnp.exp(s - m_new)
            pv = jnp.dot(v_ref[:, pl.ds(key_offset(j), kt)], p.astype(bf16), preferred_element_type=f32)
            acc_ref[...] = jnp.exp(m_old - m_new) * acc_ref[...] + pv
            m_ref[...] = m_new

        sa_ref[...] = scores(first_tile)

        def body(t, carry):
            j = first_tile + 2 * t
            sb_ref[...] = scores(j + 1)
            absorb(sa_ref[...], j)
            sa_ref[...] = scores(j + 2)
            absorb(sb_ref[...], j + 1)
            return carry

        lax.fori_loop(0, (last_tile - first_tile) // 2 + 1, body, 0)
        acc = acc_ref[...]
        return acc[0:HEAD_DIM] / jnp.maximum(acc[HEAD_DIM:2 * HEAD_DIM], 1e-30)

    last = (i * tq) // kt
    o_s = branch(rhs_sel, kxs_ref, v1s_ref, tabs_ref, 0, last)
    o_w = branch(q4t, kkw_ref, v1w_ref, tabw_ref, (jnp.maximum(i - WINDOW // tq, 0) * tq) // kt, last)
    gt = jax.nn.sigmoid(gate_ref[...]).T
    oct_ = ocmp_ref[...].T
    parts = []
    for h in range(GROUP_HEADS):
        cols = slice(h * tq, (h + 1) * tq)
        parts.append(gt[3 * h:3 * h + 1] * oct_[h * HEAD_DIM:(h + 1) * HEAD_DIM]
                     + gt[3 * h + 1:3 * h + 2] * o_s[:, cols] + gt[3 * h + 2:3 * h + 3] * o_w[:, cols])
    yt = jnp.concatenate(parts, axis=0)
    yt = yt * lax.rsqrt(jnp.mean(yt * yt, axis=0, keepdims=True) + EPS) * mixg_ref[...]
    o_ref[...] = yt.T


def nsa_attention(qnt, pen, ocmp, pa, kxs, v1s, kkw, v1w, tab_s, tab_w, mixg):
    B, _, T = qnt.shape
    tq = Q_BLOCK
    tile = lambda w, j: pl.BlockSpec((None, tq, w), lambda b, i: (b, i, j))
    whole = lambda w: pl.BlockSpec((None, T, w), lambda b, i: (b, 0, 0))
    whole_t = pl.BlockSpec((None, 128, T), lambda b, i: (b, 0, 0))
    full = lambda a: pl.BlockSpec(a.shape, lambda b, i: (0,) * a.ndim)
    return pl.pallas_call(
        _nsa_attn_kernel,
        grid=(B, T // tq),
        in_specs=[pl.BlockSpec((None, 256, tq), lambda b, i: (b, 0, i)),
                  pl.BlockSpec((None, None, 128, tq), lambda b, i: (b, i, 0, 0)),
                  tile(256, 0), tile(128, 5),
                  whole(256), whole_t, whole(128), whole_t, full(tab_s), full(tab_w), full(mixg)],
        out_specs=tile(256, 0),
        out_shape=jax.ShapeDtypeStruct((B, T, 256), jnp.float32),
        scratch_shapes=[pltpu.VMEM((128, GROUP_HEADS * tq), jnp.float32),
                        pltpu.VMEM((1, GROUP_HEADS * tq), jnp.float32),
                        pltpu.VMEM((tab_s.shape[1], GROUP_HEADS * tq), jnp.float32),
                        pltpu.VMEM((tab_s.shape[1], GROUP_HEADS * tq), jnp.float32)],
        compiler_params=pltpu.CompilerParams(dimension_semantics=("arbitrary", "arbitrary"),
                                             vmem_limit_bytes=VMEM_LIMIT_BYTES),
        name="nsa_attention",
    )(qnt, pen, ocmp, pa, kxs, v1s, kkw, v1w, tab_s, tab_w, mixg)


def nsa_bias_tables(t5_table, T):
    H = GROUP_HEADS
    tq = Q_BLOCK
    kt = min(NSA_KEY_TILE, T)
    dmax = WINDOW + 2 * tq + kt
    by_dist = t5_table[t5_bucket(jnp.arange(dmax))].T

    def tile(dist, keep):
        b = by_dist[:, jnp.clip(dist, 0, dmax - 1)]
        return jnp.where(keep, b, NEG).reshape(H * dist.shape[0], dist.shape[1])

    r = jnp.arange(tq)[:, None]
    c = jnp.arange(kt)[None, :]
    d = lambda delta: delta * tq + r - c
    n_s = kt // tq + 2
    n_w = WINDOW // tq + kt // tq
    none = [jnp.full((kt, H * tq), NEG, jnp.float32)]
    tab_s = jnp.stack([tile(d(k), d(k) >= 0).T for k in range(n_s)] + none)
    tab_w = jnp.stack([tile(d(k), (d(k) >= 0) & (d(k) < WINDOW)).T for k in range(n_w)] + none)
    nrow = T // CMP_STRIDE
    per_tile = tq // CMP_STRIDE
    m = jnp.arange(2 * nrow)[None, :] - nrow
    dist = r - CMP_STRIDE * m - (CMP_BLOCK - 1)
    base = tile(dist, dist >= 0)
    cbias = jnp.stack([base[:, nrow - per_tile * i: 2 * nrow - per_tile * i] for i in range(T // tq)])
    return tab_s, tab_w, cbias


def nsa_mixer_pallas(pa, q_norm_g, k_norm_g, cmp_pos, cmp_k_w1, cmp_k_w2, cmp_v_w1, cmp_v_w2,
                     tables, ovt, mixg):
    B, T, _ = pa.shape
    bf16 = jnp.bfloat16
    tab_s, tab_w, cbias = tables
    qg = (jnp.tile(q_norm_g, GROUP_HEADS) * HEAD_DIM ** -0.5)[None, :]
    kg2 = jnp.tile(k_norm_g, 2)[None, :]
    seg = jnp.asarray(np.kron(np.eye(GROUP_HEADS), np.ones((HEAD_DIM, HEAD_DIM))), jnp.float32)
    qn, qnt, kxs, v1s, kkw, v1w = nsa_prep(pa, qg, kg2, seg)
    nrow = T // CMP_STRIDE
    kch = pa[:, :, 256:320].reshape(B, nrow, CMP_STRIDE * HEAD_DIM)
    vch = pa[:, :, 320:384].reshape(B, nrow, CMP_STRIDE * HEAD_DIM)
    plo = cmp_pos[:CMP_STRIDE].reshape(1, -1)
    phi = cmp_pos[CMP_STRIDE:].reshape(1, -1)
    dup = lambda w: jnp.concatenate([w, w], axis=1).astype(bf16)
    kkc, vvc = nsa_compress(kch, vch, plo, phi, cmp_k_w1.astype(bf16), dup(cmp_k_w2),
                            cmp_v_w1.astype(bf16), dup(cmp_v_w2), kg2)
    n_sel = min(N_SLC, T // SLC_BLOCK)
    ocmp, pen = nsa_cmp_select(qn, kkc, vvc, cbias, ovt, n_sel)
    mixg_col = jnp.broadcast_to(mixg.reshape(-1, 1), (GROUP_WIDTH, Q_BLOCK))
    return nsa_attention(qnt, pen, ocmp, pa, kxs, v1s, kkw, v1w, tab_s, tab_w, mixg_col)


def kernel(x, c, ada_w, ada_b, norm_g, ffn1_w13, ffn1_w2, ffn2_w13, ffn2_w2, w_in, b_in, q_norm_g, k_norm_g, cmp_pos, cmp_k_w1, cmp_k_w2, cmp_v_w1, cmp_v_w2, t5_table, sc_conv_w, gdn_conv_w, gdn_A_log, gdn_dt_bias, gdn_norm_g, mlstm_f_bias, mlstm_norm_g, mix_norm_g, w_out):
    B, T, D = x.shape
    depth = ada_w.shape[0]
    bf16 = jnp.bfloat16
    x2 = x.reshape(B * T, D)
    tables = nsa_bias_tables(t5_table, T)
    ovt = nsa_overlap_t(T)
    for l in range(depth):
        mod = ada_modulation(c, ada_w[l].astype(bf16), ada_b[l][None, :])
        mod = mod.reshape(B, N_SUBLAYERS, 3, 1, D)
        w13 = ffn1_w13[l].astype(bf16)
        x2 = ffn_half_step(x2, norm_g[l, 0][None, :], mod[:, 0, 0], mod[:, 0, 1], mod[:, 0, 2],
                           w13[:, :D_FF], w13[:, D_FF:], ffn1_w2[l].astype(bf16), T)
        ws, bs = split_in_weights(w_in[l], b_in[l])
        pa, pb, pc, pd = in_projection(x2, norm_g[l, 1][None, :], mod[:, 1, 0], mod[:, 1, 1], ws, bs, T)
        pa, pb, pc, pd = (a.reshape(B, T, -1) for a in (pa, pb, pc, pd))
        y_a = nsa_mixer_pallas(pa, q_norm_g[l], k_norm_g[l], cmp_pos[l], cmp_k_w1[l], cmp_k_w2[l],
                               cmp_v_w1[l], cmp_v_w2[l], tables, ovt, mix_norm_g[l, 0][None, :])
        y_b = short_conv_pallas(pb, sc_conv_w[l], mix_norm_g[l, 1][None, :])
        y_c = gated_deltanet_pallas(pc, gdn_conv_w[l], gdn_A_log[l], gdn_dt_bias[l], gdn_norm_g[l])
        y_d = mlstm_pallas(pd, mlstm_f_bias[l], mlstm_norm_g[l])
        y = jnp.concatenate([y_a, y_b, y_c, y_d], axis=-1)
        x2 = out_projection(x2, y.reshape(B * T, -1), mod[:, 1, 2], w_out[l].astype(bf16), T)
        w13 = ffn2_w13[l].astype(bf16)
        x2 = ffn_half_step(x2, norm_g[l, 2][None, :], mod[:, 2, 0], mod[:, 2, 1], mod[:, 2, 2],
                           w13[:, :D_FF], w13[:, D_FF:], ffn2_w2[l].astype(bf16), T)
    return x2.reshape(B, T, D)
```

```python
import functools
import math

import jax
import jax.numpy as jnp
import numpy as np
from jax import lax
from jax.experimental import pallas as pl
from jax.experimental.pallas import tpu as pltpu

D_MODEL = 1024
HEAD_DIM = 64
GROUP_HEADS = 4
GROUP_WIDTH = GROUP_HEADS * HEAD_DIM
CMP_STRIDE = 16
CMP_BLOCK = 32
SLC_BLOCK = 64
N_SLC = 16
N_LOCAL_SLC = 2
WINDOW = 512
Q_BLOCK = 128
FORCE = 1e6
N_BUCKETS = 32
MAX_DISTANCE = 128
GDN_CHUNK = 64
MLSTM_CHUNK = 64
D_FF = 2816
N_SUBLAYERS = 3
EPS = 1e-6

IN_LAYOUT = (
    ("a_q", 256), ("a_k_cmp", 64), ("a_v_cmp", 64),
    ("a_k_slc", 64), ("a_v_slc", 64), ("a_k_win", 64), ("a_v_win", 64),
    ("a_gate", 12),
    ("b_b", 256), ("b_c", 256), ("b_x", 256),
    ("c_q", 256), ("c_k", 256), ("c_v", 256),
    ("c_beta", 4), ("c_alpha", 4), ("c_z", 256),
    ("d_q", 256), ("d_k", 256), ("d_v", 256),
    ("d_i", 4), ("d_f", 4), ("d_o", 256),
)

VMEM_LIMIT_BYTES = 56 * 1024 * 1024
FFN_TOKEN_TILE = 512
FFN_CHUNK = 256
PROJ_TOKEN_TILE = 512
NEG = -1e30
HIGHEST = lax.Precision.HIGHEST


def _modulated_norm(x, g, scale, shift):
    y = x * lax.rsqrt(jnp.mean(x * x, axis=-1, keepdims=True) + EPS)
    return (y * g) * (1.0 + scale) + shift


def _ada_kernel(c_ref, w_ref, b_ref, o_ref):
    c = c_ref[...]
    cond = c * jax.nn.sigmoid(c)
    o_ref[...] = jnp.dot(cond.astype(jnp.bfloat16), w_ref[...],
                         preferred_element_type=jnp.float32) + b_ref[...]


def ada_modulation(c, w, b):
    B, D = c.shape
    N = w.shape[1]
    tn = 1152
    return pl.pallas_call(
        _ada_kernel,
        grid=(N // tn,),
        in_specs=[pl.BlockSpec((B, D), lambda j: (0, 0)),
                  pl.BlockSpec((D, tn), lambda j: (0, j)),
                  pl.BlockSpec((1, tn), lambda j: (0, j))],
        out_specs=pl.BlockSpec((B, tn), lambda j: (0, j)),
        out_shape=jax.ShapeDtypeStruct((B, N), jnp.float32),
        name="ada_modulation",
    )(c, w, b)


def _ffn_kernel(x_ref, g_ref, shift_ref, scale_ref, gate_ref, w1_ref, w3_ref, w2_ref, o_ref, act_ref):
    x = x_ref[...]
    h = _modulated_norm(x, g_ref[...], scale_ref[0], shift_ref[0]).astype(jnp.bfloat16)
    n_chunks = act_ref.shape[1] // FFN_CHUNK
    for ci in range(n_chunks):
        cs = slice(ci * FFN_CHUNK, (ci + 1) * FFN_CHUNK)
        a = jnp.dot(h, w1_ref[:, cs], preferred_element_type=jnp.float32)
        b = jnp.dot(h, w3_ref[:, cs], preferred_element_type=jnp.float32)
        act_ref[:, cs] = (a * jax.nn.sigmoid(a) * b).astype(jnp.bfloat16)
    y = jnp.dot(act_ref[...], w2_ref[...], preferred_element_type=jnp.float32)
    o_ref[...] = x + (0.5 * gate_ref[0]) * y


def ffn_half_step(x2, g, shift, scale, gate, w1, w3, w2, tokens_per_batch):
    M, D = x2.shape
    F = w2.shape[0]
    tm = FFN_TOKEN_TILE
    tiles_per_batch = tokens_per_batch // tm
    resident = dict(pipeline_mode=pl.Buffered(1))
    mod_spec = pl.BlockSpec((1, 1, D), lambda i: (i // tiles_per_batch, 0, 0))
    return pl.pallas_call(
        _ffn_kernel,
        grid=(M // tm,),
        in_specs=[pl.BlockSpec((tm, D), lambda i: (i, 0)),
                  pl.BlockSpec((1, D), lambda i: (0, 0)),
                  mod_spec, mod_spec, mod_spec,
                  pl.BlockSpec((D, F), lambda i: (0, 0), **resident),
                  pl.BlockSpec((D, F), lambda i: (0, 0), **resident),
                  pl.BlockSpec((F, D), lambda i: (0, 0), **resident)],
        out_specs=pl.BlockSpec((tm, D), lambda i: (i, 0)),
        out_shape=jax.ShapeDtypeStruct((M, D), jnp.float32),
        scratch_shapes=[pltpu.VMEM((tm, F), jnp.bfloat16)],
        compiler_params=pltpu.CompilerParams(dimension_semantics=("arbitrary",),
                                             vmem_limit_bytes=VMEM_LIMIT_BYTES),
        name="ffn_half_step",
    )(x2, g, shift, scale, gate, w1, w3, w2)


def _in_proj_kernel(x_ref, g_ref, shift_ref, scale_ref, wa_ref, wb_ref, wc_ref, wd_ref,
                    ba_ref, bb_ref, bc_ref, bd_ref, oa_ref, ob_ref, oc_ref, od_ref):
    h = _modulated_norm(x_ref[...], g_ref[...], scale_ref[0], shift_ref[0]).astype(jnp.bfloat16)
    for w_ref, b_ref, o_ref in ((wa_ref, ba_ref, oa_ref), (wb_ref, bb_ref, ob_ref),
                                (wc_ref, bc_ref, oc_ref), (wd_ref, bd_ref, od_ref)):
        o_ref[...] = jnp.dot(h, w_ref[...], preferred_element_type=jnp.float32) + b_ref[...]


def in_projection(x2, g, shift, scale, ws, bs, tokens_per_batch):
    M, D = x2.shape
    tm = PROJ_TOKEN_TILE
    tiles_per_batch = tokens_per_batch // tm
    mod_spec = pl.BlockSpec((1, 1, D), lambda i: (i // tiles_per_batch, 0, 0))
    const = lambda a, **kw: pl.BlockSpec(a.shape, lambda i: (0, 0), **kw)
    return pl.pallas_call(
        _in_proj_kernel,
        grid=(M // tm,),
        in_specs=[pl.BlockSpec((tm, D), lambda i: (i, 0)), const(g), mod_spec, mod_spec]
                 + [const(w, pipeline_mode=pl.Buffered(1)) for w in ws] + [const(b) for b in bs],
        out_specs=[pl.BlockSpec((tm, w.shape[1]), lambda i: (i, 0)) for w in ws],
        out_shape=[jax.ShapeDtypeStruct((M, w.shape[1]), jnp.float32) for w in ws],
        compiler_params=pltpu.CompilerParams(dimension_semantics=("arbitrary",),
                                             vmem_limit_bytes=VMEM_LIMIT_BYTES),
        name="in_projection",
    )(x2, g, shift, scale, *ws, *bs)


def _group_columns():
    offs, o = {}, 0
    for n, s in IN_LAYOUT:
        offs[n] = np.arange(o, o + s)
        o += s
    cat = lambda names: np.concatenate([offs[n] for n in names])
    return (cat(["a_q", "a_k_cmp", "a_v_cmp", "a_k_slc", "a_v_slc", "a_k_win", "a_v_win", "a_gate"]),
            cat(["b_b", "b_c", "b_x"]),
            cat(["c_q", "c_k", "c_v", "c_z", "c_beta", "c_alpha"]),
            cat(["d_q", "d_k", "d_v", "d_o", "d_i", "d_f"]))


GROUP_SLAB_WIDTH = (768, 768, 1152, 1152)


def split_in_weights(w, b):
    ws, bs = [], []
    for idx, width in zip(_group_columns(), GROUP_SLAB_WIDTH):
        pad = width - idx.size
        ws.append(jnp.pad(w[:, idx], ((0, 0), (0, pad))).astype(jnp.bfloat16))
        bs.append(jnp.pad(b[idx], (0, pad))[None, :])
    return ws, bs


def _out_proj_kernel(x_ref, ya_ref, yb_ref, yc_ref, yd_ref, gate_ref, w_ref, o_ref):
    z = None
    for g, y_ref in enumerate((ya_ref, yb_ref, yc_ref, yd_ref)):
        part = jnp.dot(y_ref[...].astype(jnp.bfloat16), w_ref[g * GROUP_WIDTH:(g + 1) * GROUP_WIDTH, :],
                       preferred_element_type=jnp.float32)
        z = part if z is None else z + part
    o_ref[...] = x_ref[...] + gate_ref[0] * z


def out_projection(x2, ys, gate, w, tokens_per_batch):
    M, D = x2.shape
    tm = PROJ_TOKEN_TILE
    tiles_per_batch = tokens_per_batch // tm
    return pl.pallas_call(
        _out_proj_kernel,
        grid=(M // tm,),
        in_specs=[pl.BlockSpec((tm, D), lambda i: (i, 0))]
                 + [pl.BlockSpec((tm, GROUP_WIDTH), lambda i: (i, 0)) for _ in ys]
                 + [pl.BlockSpec((1, 1, D), lambda i: (i // tiles_per_batch, 0, 0)),
                    pl.BlockSpec(w.shape, lambda i: (0, 0), pipeline_mode=pl.Buffered(1))],
        out_specs=pl.BlockSpec((tm, D), lambda i: (i, 0)),
        out_shape=jax.ShapeDtypeStruct((M, D), jnp.float32),
        compiler_params=pltpu.CompilerParams(dimension_semantics=("arbitrary",),
                                             vmem_limit_bytes=VMEM_LIMIT_BYTES),
        name="out_projection",
    )(x2, *ys, gate, w)


def t5_bucket(dist):
    n = jnp.maximum(dist, 0)
    max_exact = N_BUCKETS // 2
    nf = jnp.maximum(n, 1).astype(jnp.float32)
    large = max_exact + (jnp.log(nf / max_exact) / math.log(MAX_DISTANCE / max_exact)
                         * (N_BUCKETS - max_exact)).astype(jnp.int32)
    large = jnp.minimum(large, N_BUCKETS - 1)
    return jnp.where(n < max_exact, n, large)


def _nt_dot(a, b):
    return lax.dot_general(a, b, (((1,), (1,)), ((), ())), preferred_element_type=jnp.float32)


def _stack_heads(qn):
    lane = lax.broadcasted_iota(jnp.int32, (qn.shape[0], 128), 1)
    lo = lane < HEAD_DIM
    zero = jnp.zeros((), qn.dtype)
    halves = (qn[:, 0:128], qn[:, 128:256])
    return jnp.concatenate([jnp.where(lo if h % 2 == 0 else ~lo, halves[h // 2], zero)
                            for h in range(GROUP_HEADS)], axis=0)


def _nsa_prep_kernel(q_ref, kvs_ref, kvw_ref, qg_ref, kg_ref, seg_ref,
                     qn_ref, qnt_ref, kxs_ref, v1s_ref, kkw_ref, v1w_ref):
    tm = q_ref.shape[0]
    q = q_ref[...]
    ss = jnp.dot(q * q, seg_ref[...], precision=HIGHEST, preferred_element_type=jnp.float32)
    qn = q * lax.rsqrt(ss * (1.0 / HEAD_DIM) + EPS) * qg_ref[...]
    qn_ref[...] = qn.astype(qn_ref.dtype)
    qnt_ref[...] = qn.T.astype(qnt_ref.dtype)
    lane = lax.broadcasted_iota(jnp.int32, (tm, 128), 1)
    lo = lane < HEAD_DIM

    def split(x):
        xr = pltpu.roll(x, HEAD_DIM, 1)
        ss = jnp.sum(jnp.where(lo, x * x, 0.0), axis=-1, keepdims=True)
        kk = jnp.where(lo, x, xr) * lax.rsqrt(ss * (1.0 / HEAD_DIM) + EPS) * kg_ref[...]
        return kk, jnp.where(lo, xr, 1.0)

    kk, v1 = split(kvs_ref[...])
    tok = pl.program_id(1) * tm + lax.broadcasted_iota(jnp.int32, (tm, 128), 0)
    onehot = jnp.where(lane == tok // SLC_BLOCK, 1.0, 0.0)
    kxs_ref[...] = jnp.concatenate([kk, onehot], axis=1).astype(kxs_ref.dtype)
    v1s_ref[...] = v1.T.astype(v1s_ref.dtype)
    kk, v1 = split(kvw_ref[...])
    kkw_ref[...] = kk.astype(kkw_ref.dtype)
    v1w_ref[...] = v1.T.astype(v1w_ref.dtype)


def nsa_prep(pa, qg, kg2, seg):
    B, T, _ = pa.shape
    tm = 512
    bf16 = jnp.bfloat16
    col = lambda w, j: pl.BlockSpec((None, tm, w), lambda b, i: (b, i, j))
    rowblk = lambda h: pl.BlockSpec((None, h, tm), lambda b, i: (b, 0, i))
    const = lambda shape: pl.BlockSpec(shape, lambda b, i: (0, 0))
    return pl.pallas_call(
        _nsa_prep_kernel,
        grid=(B, T // tm),
        in_specs=[col(256, 0), col(128, 3), col(128, 4), const((1, 256)), const((1, 128)), const((256, 256))],
        out_specs=[col(256, 0), rowblk(256), col(256, 0), rowblk(128), col(128, 0), rowblk(128)],
        out_shape=[jax.ShapeDtypeStruct((B, T, 256), bf16), jax.ShapeDtypeStruct((B, 256, T), bf16),
                   jax.ShapeDtypeStruct((B, T, 256), bf16), jax.ShapeDtypeStruct((B, 128, T), bf16),
                   jax.ShapeDtypeStruct((B, T, 128), bf16), jax.ShapeDtypeStruct((B, 128, T), bf16)],
        name="nsa_prep",
    )(pa, pa, pa, qg, kg2, seg)


def _compress_kernel(kch_ref, vch_ref, plo_ref, phi_ref, w1k_ref, w2k_ref, w1v_ref, w2v_ref, kg_ref,
                     kkc_ref, vvc_ref):
    nrow = kch_ref.shape[0]
    half = CMP_STRIDE * HEAD_DIM
    bf16 = jnp.bfloat16
    valid = lax.broadcasted_iota(jnp.int32, (nrow, 128), 0) < nrow - 1

    def comp(ch, w1_ref, w2_ref):
        a = jnp.dot((ch + plo_ref[...]).astype(bf16), w1_ref[0:half, :], preferred_element_type=jnp.float32)
        b = jnp.dot((ch + phi_ref[...]).astype(bf16), w1_ref[half:2 * half, :],
                    preferred_element_type=jnp.float32)
        hid = a + pltpu.roll(b, nrow - 1, 0)
        hid = hid * jax.nn.sigmoid(hid)
        return jnp.dot(hid.astype(bf16), w2_ref[...], preferred_element_type=jnp.float32)

    kc = comp(kch_ref[...], w1k_ref, w2k_ref)
    kc = kc * lax.rsqrt(jnp.mean(kc * kc, axis=-1, keepdims=True) + EPS) * kg_ref[...]
    kkc_ref[...] = jnp.where(valid, kc, 0.0).astype(kkc_ref.dtype)
    vc = comp(vch_ref[...], w1v_ref, w2v_ref)
    vvc_ref[...] = jnp.where(valid, vc, 0.0).astype(vvc_ref.dtype)


def nsa_compress(kch, vch, plo, phi, w1k, w2k, w1v, w2v, kg2):
    B, nrow, width = kch.shape
    bf16 = jnp.bfloat16
    full = lambda a: pl.BlockSpec(a.shape, lambda b: (0,) * a.ndim)
    bspec = pl.BlockSpec((None, nrow, width), lambda b: (b, 0, 0))
    ospec = pl.BlockSpec((None, nrow, 128), lambda b: (b, 0, 0))
    return pl.pallas_call(
        _compress_kernel,
        grid=(B,),
        in_specs=[bspec, bspec, full(plo), full(phi), full(w1k), full(w2k), full(w1v), full(w2v), full(kg2)],
        out_specs=[ospec, ospec],
        out_shape=[jax.ShapeDtypeStruct((B, nrow, 128), bf16), jax.ShapeDtypeStruct((B, nrow, 128), bf16)],
        name="nsa_compress",
    )(kch, vch, plo, phi, w1k, w2k, w1v, w2v, kg2)


def _cmp_select_kernel(qn_ref, kkc_ref, vvc_ref, cb_ref, ovt_ref, ocmp_ref, pen_ref, sc_ref, *, n_sel):
    i = pl.program_id(1)
    tq = qn_ref.shape[0]
    nblk = ovt_ref.shape[0]
    bf16 = jnp.bfloat16
    q4 = _stack_heads(qn_ref[...])
    s = _nt_dot(q4, kkc_ref[...]) + cb_ref[...]
    m = jnp.max(s, axis=-1, keepdims=True)
    e = jnp.where(s > 0.5 * NEG, jnp.exp(s - m), 0.0)
    p = e * (1.0 / jnp.maximum(jnp.sum(e, axis=-1, keepdims=True), 1e-30))
    o = jnp.dot(p.astype(bf16), vvc_ref[...], preferred_element_type=jnp.float32)
    lo = lax.broadcasted_iota(jnp.int32, (tq, 128), 1) < HEAD_DIM
    ocmp_ref[:, 0:128] = jnp.where(lo, o[0:tq], o[tq:2 * tq])
    ocmp_ref[:, 128:256] = jnp.where(lo, o[2 * tq:3 * tq], o[3 * tq:4 * tq])
    psum = p[0:tq] + p[tq:2 * tq] + p[2 * tq:3 * tq] + p[3 * tq:4 * tq]
    hi = psum.astype(bf16)
    lo_part = (psum - hi.astype(jnp.float32)).astype(bf16)
    score = _nt_dot(ovt_ref[...], hi) + _nt_dot(ovt_ref[...], lo_part)
    blk = lax.broadcasted_iota(jnp.int32, (nblk, tq), 0)
    qpos = i * tq + lax.broadcasted_iota(jnp.int32, (nblk, tq), 1)
    cur = qpos // SLC_BLOCK
    forced = (blk == 0) | ((cur - blk >= 0) & (cur - blk < N_LOCAL_SLC))
    score = jnp.where(forced, FORCE, score)
    score = jnp.where(blk <= cur, score, -FORCE)
    sc_ref[...] = score
    cnt = jnp.zeros((nblk, tq), jnp.float32)
    for jp in range(nblk):
        row = sc_ref[jp:jp + 1, :]
        tie = jnp.where(blk > jp, 1.0, 0.0)
        cnt = cnt + jnp.where(row > score, 1.0, jnp.where(row == score, tie, 0.0))
    pen = jnp.where(cnt < n_sel, 0.0, NEG)
    if nblk < 128:
        pen = jnp.concatenate([pen, jnp.zeros((128 - nblk, tq), jnp.float32)], axis=0)
    pen_ref[...] = pen.astype(pen_ref.dtype)


def nsa_cmp_select(qn, kkc, vvc, cbias, ovt, n_sel):
    B, T, _ = qn.shape
    nrow = kkc.shape[1]
    nblk = ovt.shape[0]
    tq = Q_BLOCK
    return pl.pallas_call(
        functools.partial(_cmp_select_kernel, n_sel=n_sel),
        grid=(B, T // tq),
        in_specs=[pl.BlockSpec((None, tq, 256), lambda b, i: (b, i, 0)),
                  pl.BlockSpec((None, nrow, 128), lambda b, i: (b, 0, 0)),
                  pl.BlockSpec((None, nrow, 128), lambda b, i: (b, 0, 0)),
                  pl.BlockSpec((None, 4 * tq, nrow), lambda b, i: (i, 0, 0)),
                  pl.BlockSpec((nblk, nrow), lambda b, i: (0, 0))],
        out_specs=[pl.BlockSpec((None, tq, 256), lambda b, i: (b, i, 0)),
                   pl.BlockSpec((None, None, 128, tq), lambda b, i: (b, i, 0, 0))],
        out_shape=[jax.ShapeDtypeStruct((B, T, 256), jnp.float32),
                   jax.ShapeDtypeStruct((B, T // tq, 128, tq), jnp.bfloat16)],
        scratch_shapes=[pltpu.VMEM((nblk, tq), jnp.float32)],
        name="nsa_cmp_select",
    )(qn, kkc, vvc, cbias, ovt)


NSA_KEY_TILE = 256


def _nsa_attn_kernel(qnt_ref, pen_ref, ocmp_ref, gate_ref, kxs_ref, v1s_ref, kkw_ref, v1w_ref,
                     tabs_ref, tabw_ref, mixg_ref, o_ref, acc_ref, m_ref, sa_ref, sb_ref):
    i = pl.program_id(1)
    tq = o_ref.shape[0]
    kt = tabs_ref.shape[1]
    f32, bf16 = jnp.float32, jnp.bfloat16
    qt = qnt_ref[...]
    lo = lax.broadcasted_iota(jnp.int32, (128, tq), 0) < HEAD_DIM
    zero = jnp.zeros((), qt.dtype)
    q4t = jnp.concatenate([jnp.where(lo if h % 2 == 0 else ~lo, qt[128 * (h // 2):128 * (h // 2) + 128], zero)
                           for h in range(GROUP_HEADS)], axis=1)
    rhs_sel = jnp.concatenate([q4t, jnp.concatenate([pen_ref[...]] * GROUP_HEADS, axis=1)], axis=0)

    def branch(rhs, k_ref, v_ref, tab_ref, first_tile, last_tile):
        n_tab = tab_ref.shape[0] - 1
        m_ref[...] = jnp.full(m_ref.shape, NEG, f32)
        acc_ref[...] = jnp.zeros(acc_ref.shape, f32)

        def key_offset(j):
            return pl.multiple_of(jnp.minimum(j, last_tile) * kt, kt)

        def scores(j):
            delta = jnp.where(j > last_tile, n_tab, jnp.minimum(i - j * (kt // tq), n_tab - 1))
            return jnp.dot(k_ref[pl.ds(key_offset(j), kt), :], rhs, preferred_element_type=f32) + tab_ref[delta]

        def absorb(s, j):
            m_old = m_ref[...]
            m_new = jnp.maximum(m_old, jnp.max(s, axis=0, keepdims=True))
            p = jnp.exp(s - m_new)
            pv = jnp.dot(v_ref[:, pl.ds(key_offset(j), kt)], p.astype(bf16), preferred_element_type=f32)
            acc_ref[...] = jnp.exp(m_old - m_new) * acc_ref[...] + pv
            m_ref[...] = m_new

        sa_ref[...] = scores(first_tile)

        def body(t, carry):
            j = first_tile + 2 * t
            sb_ref[...] = scores(j + 1)
            absorb(sa_ref[...], j)
            sa_ref[...] = scores(j + 2)
            absorb(sb_ref[...], j + 1)
            return carry

        lax.fori_loop(0, (last_tile - first_tile) // 2 + 1, body, 0)
        acc = acc_ref[...]
        return acc[0:HEAD_DIM] / jnp.maximum(acc[HEAD_DIM:2 * HEAD_DIM], 1e-30)

    last = (i * tq) // kt
    o_s = branch(rhs_sel, kxs_ref, v1s_ref, tabs_ref, 0, last)
    o_w = branch(q4t, kkw_ref, v1w_ref, tabw_ref, (jnp.maximum(i - WINDOW // tq, 0) * tq) // kt, last)
    gt = jax.nn.sigmoid(gate_ref[...]).T
    oct_ = ocmp_ref[...].T
    parts = []
    for h in range(GROUP_HEADS):
        cols = slice(h * tq, (h + 1) * tq)
        parts.append(gt[3 * h:3 * h + 1] * oct_[h * HEAD_DIM:(h + 1) * HEAD_DIM]
                     + gt[3 * h + 1:3 * h + 2] * o_s[:, cols] + gt[3 * h + 2:3 * h + 3] * o_w[:, cols])
    yt = jnp.concatenate(parts, axis=0)
    yt = yt * lax.rsqrt(jnp.mean(yt * yt, axis=0, keepdims=True) + EPS) * mixg_ref[...]
    o_ref[...] = yt.T


def nsa_attention(qnt, pen, ocmp, pa, kxs, v1s, kkw, v1w, tab_s, tab_w, mixg):
    B, _, T = qnt.shape
    tq = Q_BLOCK
    tile = lambda w, j: pl.BlockSpec((None, tq, w), lambda b, i: (b, i, j))
    whole = lambda w: pl.BlockSpec((None, T, w), lambda b, i: (b, 0, 0))
    whole_t = pl.BlockSpec((None, 128, T), lambda b, i: (b, 0, 0))
    full = lambda a: pl.BlockSpec(a.shape, lambda b, i: (0,) * a.ndim)
    return pl.pallas_call(
        _nsa_attn_kernel,
        grid=(B, T // tq),
        in_specs=[pl.BlockSpec((None, 256, tq), lambda b, i: (b, 0, i)),
                  pl.BlockSpec((None, None, 128, tq), lambda b, i: (b, i, 0, 0)),
                  tile(256, 0), tile(128, 5),
                  whole(256), whole_t, whole(128), whole_t, full(tab_s), full(tab_w), full(mixg)],
        out_specs=tile(256, 0),
        out_shape=jax.ShapeDtypeStruct((B, T, 256), jnp.float32),
        scratch_shapes=[pltpu.VMEM((128, GROUP_HEADS * tq), jnp.float32),
                        pltpu.VMEM((1, GROUP_HEADS * tq), jnp.float32),
                        pltpu.VMEM((tab_s.shape[1], GROUP_HEADS * tq), jnp.float32),
                        pltpu.VMEM((tab_s.shape[1], GROUP_HEADS * tq), jnp.float32)],
        compiler_params=pltpu.CompilerParams(dimension_semantics=("arbitrary", "arbitrary"),
                                             vmem_limit_bytes=VMEM_LIMIT_BYTES),
        name="nsa_attention",
    )(qnt, pen, ocmp, pa, kxs, v1s, kkw, v1w, tab_s, tab_w, mixg)


def nsa_bias_tables(t5_table, T):
    H = GROUP_HEADS
    tq = Q_BLOCK
    kt = min(NSA_KEY_TILE, T)
    dmax = WINDOW + 2 * tq + kt
    by_dist = t5_table[t5_bucket(jnp.arange(dmax))].T

    def tile(dist, keep):
        b = by_dist[:, jnp.clip(dist, 0, dmax - 1)]
        return jnp.where(keep, b, NEG).reshape(H * dist.shape[0], dist.shape[1])

    r = jnp.arange(tq)[:, None]
    c = jnp.arange(kt)[None, :]
    d = lambda delta: delta * tq + r - c
    n_s = kt // tq + 2
    n_w = WINDOW // tq + kt // tq
    none = [jnp.full((kt, H * tq), NEG, jnp.float32)]
    tab_s = jnp.stack([tile(d(k), d(k) >= 0).T for k in range(n_s)] + none)
    tab_w = jnp.stack([tile(d(k), (d(k) >= 0) & (d(k) < WINDOW)).T for k in range(n_w)] + none)
    nrow = T // CMP_STRIDE
    per_tile = tq // CMP_STRIDE
    m = jnp.arange(2 * nrow)[None, :] - nrow
    dist = r - CMP_STRIDE * m - (CMP_BLOCK - 1)
    base = tile(dist, dist >= 0)
    cbias = jnp.stack([base[:, nrow - per_tile * i: 2 * nrow - per_tile * i] for i in range(T // tq)])
    return tab_s, tab_w, cbias


def nsa_overlap_t(T):
    nrow = T // CMP_STRIDE
    nblk = T // SLC_BLOCK
    ci = np.arange(nrow)[None, :]
    bj = np.arange(nblk)[:, None]
    ov = (ci * CMP_STRIDE < (bj + 1) * SLC_BLOCK) & (ci * CMP_STRIDE + CMP_BLOCK > bj * SLC_BLOCK)
    ov = ov & (ci < nrow - 1)
    return jnp.asarray(ov.astype(np.float32), dtype=jnp.bfloat16)


def nsa_mixer_pallas(pa, q_norm_g, k_norm_g, cmp_pos, cmp_k_w1, cmp_k_w2, cmp_v_w1, cmp_v_w2,
                     tables, ovt, mixg):
    B, T, _ = pa.shape
    bf16 = jnp.bfloat16
    tab_s, tab_w, cbias = tables
    qg = (jnp.tile(q_norm_g, GROUP_HEADS) * HEAD_DIM ** -0.5)[None, :]
    kg2 = jnp.tile(k_norm_g, 2)[None, :]
    qn, qnt, kxs, v1s, kkw, v1w = nsa_prep(pa, qg, kg2, _head_mask())
    nrow = T // CMP_STRIDE
    kch = pa[:, :, 256:320].reshape(B, nrow, CMP_STRIDE * HEAD_DIM)
    vch = pa[:, :, 320:384].reshape(B, nrow, CMP_STRIDE * HEAD_DIM)
    plo = cmp_pos[:CMP_STRIDE].reshape(1, -1)
    phi = cmp_pos[CMP_STRIDE:].reshape(1, -1)
    dup = lambda w: jnp.concatenate([w, w], axis=1).astype(bf16)
    kkc, vvc = nsa_compress(kch, vch, plo, phi, cmp_k_w1.astype(bf16), dup(cmp_k_w2),
                            cmp_v_w1.astype(bf16), dup(cmp_v_w2), kg2)
    n_sel = min(N_SLC, T // SLC_BLOCK)
    ocmp, pen = nsa_cmp_select(qn, kkc, vvc, cbias, ovt, n_sel)
    mixg_col = jnp.broadcast_to(mixg.reshape(-1, 1), (GROUP_WIDTH, Q_BLOCK))
    return nsa_attention(qnt, pen, ocmp, pa, kxs, v1s, kkw, v1w, tab_s, tab_w, mixg_col)


CONV_CARRY_ROWS = 8


def _shifted(cat, shift, rows):
    return pltpu.roll(cat, shift, 0)[CONV_CARRY_ROWS:CONV_CARRY_ROWS + rows]


def _sconv_kernel(pb_ref, w_ref, g_ref, o_ref, carry_ref):
    rows = pb_ref.shape[0]

    @pl.when(pl.program_id(1) == 0)
    def _():
        carry_ref[...] = jnp.zeros(carry_ref.shape, jnp.float32)

    u = pb_ref[:, 256:512] * pb_ref[:, 512:768]
    cat = jnp.concatenate([carry_ref[...], u], axis=0)
    conv = u * w_ref[2:3, :] + _shifted(cat, 1, rows) * w_ref[1:2, :] + _shifted(cat, 2, rows) * w_ref[0:1, :]
    carry_ref[...] = u[rows - CONV_CARRY_ROWS:rows]
    y = pb_ref[:, 0:256] * conv
    o_ref[...] = y * lax.rsqrt(jnp.mean(y * y, axis=-1, keepdims=True) + EPS) * g_ref[...]


def short_conv_pallas(pb, conv_w, mixg):
    B, T, _ = pb.shape
    rows = 512
    w = jnp.pad(conv_w, ((0, 8 - conv_w.shape[0]), (0, 0)))
    return pl.pallas_call(
        _sconv_kernel,
        grid=(B, T // rows),
        in_specs=[pl.BlockSpec((None, rows, 768), lambda b, t: (b, t, 0)),
                  pl.BlockSpec((8, 256), lambda b, t: (0, 0)),
                  pl.BlockSpec((1, 256), lambda b, t: (0, 0))],
        out_specs=pl.BlockSpec((None, rows, 256), lambda b, t: (b, t, 0)),
        out_shape=jax.ShapeDtypeStruct((B, T, 256), jnp.float32),
        scratch_shapes=[pltpu.VMEM((CONV_CARRY_ROWS, 256), jnp.float32)],
        compiler_params=pltpu.CompilerParams(dimension_semantics=("arbitrary", "arbitrary")),
        name="short_conv",
    )(pb, w, mixg)


CHUNKS_PER_STEP = 4
SEQS_PER_STEP = 2


def _tn_dot(a, b):
    return lax.dot_general(a, b, (((0,), (0,)), ((), ())), preferred_element_type=jnp.float32)


def _split_bf16(a, parts):
    out, rest = [], a
    for _ in range(parts):
        piece = rest.astype(jnp.bfloat16)
        out.append(piece)
        rest = rest - piece.astype(jnp.float32)
    return out


def _dot_sel(a, sel, parts=3):
    return sum(jnp.dot(p, sel, preferred_element_type=jnp.float32) for p in _split_bf16(a, parts))


def _sel_dot(sel, a, parts=3):
    return sum(jnp.dot(sel, p, preferred_element_type=jnp.float32) for p in _split_bf16(a, parts))


def _softplus(x):
    return jnp.maximum(x, 0.0) + jnp.log(1.0 + jnp.exp(-jnp.abs(x)))


def _block_diag(a, hmb):
    return jnp.concatenate([a.astype(jnp.bfloat16)] * GROUP_HEADS, axis=0) * hmb


def _diag_row(a, eye):
    return jnp.sum(a * eye, axis=0, keepdims=True)


def _head_norm(o, hmb):
    return o * lax.rsqrt(_dot_sel(o * o, hmb, parts=2) * (1.0 / HEAD_DIM) + EPS)


def _head_l2(o, hmb):
    return o * lax.rsqrt(_dot_sel(o * o, hmb, parts=2) + EPS)


def _xform_masks(L):
    i = np.arange(L)[:, None]
    j = np.arange(L)[None, :]
    masks = []
    s = 1
    while s < L:
        masks.append((i // (2 * s) == j // (2 * s)) & (i % (2 * s) >= s) & (j % (2 * s) < s))
        s *= 2
    masks += [i == j, j <= i, j < i]
    return jnp.asarray(np.tile(np.stack(masks).astype(np.float32), (1, 1, GROUP_HEADS)))


def _expand_matrix(first_lane):
    e = np.zeros((128, GROUP_WIDTH), np.float32)
    for h in range(GROUP_HEADS):
        e[first_lane + h, h * HEAD_DIM:(h + 1) * HEAD_DIM] = 1.0
    return jnp.asarray(e, jnp.bfloat16)


def _head_mask():
    return jnp.asarray(np.kron(np.eye(GROUP_HEADS), np.ones((HEAD_DIM, HEAD_DIM))), jnp.float32)


def _per_head_rows(*vecs):
    rows = [jnp.repeat(v, HEAD_DIM) if v.shape[0] == GROUP_HEADS else jnp.tile(v, GROUP_HEADS) for v in vecs]
    return jnp.pad(jnp.stack(rows), ((0, 8 - len(rows)), (0, 0)))


def _chunk_consts(L, n_chunks):
    tri = np.kron(np.eye(n_chunks), np.tril(np.ones((L, L)))).astype(np.float32)
    return (_head_mask(), jnp.asarray(tri, jnp.bfloat16), _expand_matrix(0), _expand_matrix(GROUP_HEADS),
            _xform_masks(L))


def _const_spec(a):
    return pl.BlockSpec(a.shape, lambda b, t: (0,) * a.ndim)


def _gdn_kernel(pc_ref, cw_ref, hp_ref, hm_ref, tri_ref, eb_ref, ea_ref, lv_ref, o_ref, s_ref, carry_ref):
    L = GDN_CHUNK
    n_seq, rows = pc_ref.shape[0], pc_ref.shape[1]
    f32, bf16 = jnp.float32, jnp.bfloat16
    n_lev = lv_ref.shape[0] - 3

    @pl.when(pl.program_id(1) == 0)
    def _():
        s_ref[...] = jnp.zeros(s_ref.shape, f32)
        carry_ref[...] = jnp.zeros(carry_ref.shape, f32)

    hm = hm_ref[...]
    hmb = hm.astype(bf16)
    eye, incl, strict = lv_ref[n_lev], lv_ref[n_lev + 1], lv_ref[n_lev + 2]

    q_c, k_c, v_c, beta_c, gc_c = [], [], [], [], []
    for b in range(n_seq):
        x = pc_ref[b, :, 0:768]
        cat = jnp.concatenate([carry_ref[b], x], axis=0)
        conv = x * cw_ref[3:4, :]
        for s in (1, 2, 3):
            conv = conv + _shifted(cat, s, rows) * cw_ref[3 - s:4 - s, :]
        carry_ref[b] = x[rows - CONV_CARRY_ROWS:rows]
        qkv = conv * jax.nn.sigmoid(conv)
        q_all = _head_l2(qkv[:, 0:256], hmb) * HEAD_DIM ** -0.5
        k_all = _head_l2(qkv[:, 256:512], hmb)
        tail = pc_ref[b, :, 1024:1152]
        beta_all = jax.nn.sigmoid(_dot_sel(tail, eb_ref[...]))
        g = hp_ref[0:1, :] * _softplus(_dot_sel(tail, ea_ref[...]) + hp_ref[1:2, :])
        gc_all = _sel_dot(tri_ref[...], g)
        for c in range(rows // L):
            sl = slice(c * L, (c + 1) * L)
            q_c.append(q_all[sl])
            k_c.append(k_all[sl])
            v_c.append(qkv[sl, 512:768])
            beta_c.append(beta_all[sl])
            gc_c.append(gc_all[sl])
    chains = range(len(q_c))
    egc = [jnp.exp(gc) for gc in gc_c]
    kb = [k_c[i] * beta_c[i] for i in chains]
    a_mat, qk = [], []
    for i in chains:
        seg = jnp.exp(jnp.where(incl > 0.5, gc_c[i] - _diag_row(gc_c[i], eye), NEG))
        k_bd = _block_diag(k_c[i], hmb)
        a_mat.append(_nt_dot(kb[i].astype(bf16), k_bd) * seg * strict)
        qk.append((_nt_dot(q_c[i].astype(bf16), k_bd) * seg).astype(bf16))
    t_inv = [eye - a * lv_ref[0] for a in a_mat]
    for lev in range(1, n_lev):
        te = [jnp.dot(t_inv[i].astype(bf16), _block_diag(a_mat[i] * lv_ref[lev], hmb),
                      preferred_element_type=f32) for i in chains]
        t_inv = [t_inv[i] - jnp.dot(te[i].astype(bf16), _block_diag(t_inv[i], hmb), preferred_element_type=f32)
                 for i in chains]
    parts = []
    for i in chains:
        tb = t_inv[i].astype(bf16)
        u = jnp.dot(tb, _block_diag(v_c[i] * beta_c[i], hmb), preferred_element_type=f32)
        w = jnp.dot(tb, _block_diag(kb[i] * egc[i], hmb), preferred_element_type=f32)
        k_dec = k_c[i] * jnp.exp(gc_c[i][L - 1:L, :] - gc_c[i])
        parts.append((u, w.astype(bf16), (q_c[i] * egc[i]).astype(bf16), qk[i], k_dec.astype(bf16),
                      egc[i][L - 1:L, :]))
    per_seq = rows // L
    state = [s_ref[b] for b in range(n_seq)]
    outs = [[] for _ in range(n_seq)]
    for c in range(per_seq):
        for b in range(n_seq):
            u, w, q_dec, qk_c, k_dec, g_tot = parts[b * per_seq + c]
            sb = state[b].astype(bf16)
            v_new = u - jnp.dot(w, sb, preferred_element_type=f32)
            outs[b].append(jnp.dot(q_dec, sb, preferred_element_type=f32)
                           + jnp.dot(qk_c, _block_diag(v_new, hmb), preferred_element_type=f32))
            state[b] = state[b] * g_tot + _tn_dot(k_dec, v_new.astype(bf16)) * hm
    for b in range(n_seq):
        s_ref[b] = state[b]
        o = jnp.concatenate(outs[b], axis=0)
        z = pc_ref[b, :, 768:1024]
        o_ref[b] = _head_norm(o, hmb) * hp_ref[2:3, :] * (z * jax.nn.sigmoid(z))


def gated_deltanet_pallas(pc, conv_w, A_log, dt_bias, norm_g):
    B, T, width = pc.shape
    L = GDN_CHUNK
    rows = L * min(CHUNKS_PER_STEP, T // L)
    n_seq = SEQS_PER_STEP if B % SEQS_PER_STEP == 0 else 1
    n = GROUP_HEADS * HEAD_DIM
    hp = _per_head_rows(-jnp.exp(A_log), dt_bias, norm_g)
    cw = jnp.pad(conv_w, ((0, 8 - conv_w.shape[0]), (0, 0)))
    consts = (cw, hp) + _chunk_consts(L, rows // L)
    return pl.pallas_call(
        _gdn_kernel,
        grid=(B // n_seq, T // rows),
        in_specs=[pl.BlockSpec((n_seq, rows, width), lambda b, t: (b, t, 0))] + [_const_spec(a) for a in consts],
        out_specs=pl.BlockSpec((n_seq, rows, 256), lambda b, t: (b, t, 0)),
        out_shape=jax.ShapeDtypeStruct((B, T, 256), jnp.float32),
        scratch_shapes=[pltpu.VMEM((n_seq, n, n), jnp.float32),
                        pltpu.VMEM((n_seq, CONV_CARRY_ROWS, 768), jnp.float32)],
        compiler_params=pltpu.CompilerParams(dimension_semantics=("arbitrary", "arbitrary"),
                                             vmem_limit_bytes=VMEM_LIMIT_BYTES),
        name="gated_deltanet",
    )(pc, *consts)


def _mlstm_kernel(pd_ref, hp_ref, hm_ref, tri_ref, ei_ref, ef_ref, lv_ref, o_ref, c_ref, n_ref, m_ref):
    L = MLSTM_CHUNK
    n_seq, rows = pd_ref.shape[0], pd_ref.shape[1]
    f32, bf16 = jnp.float32, jnp.bfloat16
    n_lev = lv_ref.shape[0] - 3

    @pl.when(pl.program_id(1) == 0)
    def _():
        c_ref[...] = jnp.zeros(c_ref.shape, f32)
        n_ref[...] = jnp.zeros(n_ref.shape, f32)
        m_ref[...] = jnp.zeros(m_ref.shape, f32)

    hm = hm_ref[...]
    hmb = hm.astype(bf16)
    eye, incl = lv_ref[n_lev], lv_ref[n_lev + 1]
    head_of_lane = lax.broadcasted_iota(jnp.int32, (L, GROUP_WIDTH), 1) // HEAD_DIM

    def head_max(a):
        out = jnp.zeros(a.shape, f32)
        for h in range(GROUP_HEADS):
            mine = head_of_lane == h
            out = jnp.where(mine, jnp.max(jnp.where(mine, a, NEG), axis=-1, keepdims=True), out)
        return out

    per_seq = rows // L
    parts = []
    for b in range(n_seq):
        tail = pd_ref[b, :, 1024:1152]
        log_i_all = _dot_sel(tail, ei_ref[...])
        log_f = -_softplus(-(_dot_sel(tail, ef_ref[...]) + hp_ref[0:1, :]))
        b_all = _sel_dot(tri_ref[...], log_f)
        for c in range(per_seq):
            sl = slice(c * L, (c + 1) * L)
            q = pd_ref[b, sl, 0:256]
            k = pd_ref[b, sl, 256:512] * HEAD_DIM ** -0.5
            v = pd_ref[b, sl, 512:768]
            bc, log_i = b_all[sl], log_i_all[sl]
            log_w = jnp.where(incl > 0.5, bc - _diag_row(bc - log_i, eye), NEG)
            b_last = bc[L - 1:L, :]
            log_w_end = b_last - bc + log_i
            qk = _nt_dot(q.astype(bf16), _block_diag(k, hmb))
            parts.append((q, k, v, bc, log_w, head_max(log_w), b_last, log_w_end,
                          jnp.max(log_w_end, axis=0, keepdims=True), qk))
    c_state = [c_ref[b] for b in range(n_seq)]
    n_state = [n_ref[b] for b in range(n_seq)]
    m_prev = [m_ref[b] for b in range(n_seq)]
    outs = [[] for _ in range(n_seq)]
    for c in range(per_seq):
        for b in range(n_seq):
            q, k, v, bc, log_w, m_intra, b_last, log_w_end, m_end, qk = parts[b * per_seq + c]
            log_inter = bc + m_prev[b]
            m_t = jnp.maximum(log_inter, m_intra)
            w_inter = jnp.exp(log_inter - m_t)
            s = qk * jnp.exp(log_w - m_t)
            num = w_inter * jnp.dot(q.astype(bf16), c_state[b].astype(bf16), preferred_element_type=f32)
            num = num + jnp.dot(s.astype(bf16), _block_diag(v, hmb), preferred_element_type=f32)
            den = w_inter * _dot_sel(q * n_state[b], hmb, parts=2) + _dot_sel(s, hmb, parts=2)
            outs[b].append(num / jnp.maximum(jnp.abs(den), jnp.exp(-m_t)))
            m_new = jnp.maximum(b_last + m_prev[b], m_end)
            w_old = jnp.exp(b_last + m_prev[b] - m_new)
            kw = k * jnp.exp(log_w_end - m_new)
            c_state[b] = w_old * c_state[b] + _tn_dot(kw.astype(bf16), v.astype(bf16)) * hm
            n_state[b] = w_old * n_state[b] + jnp.sum(kw, axis=0, keepdims=True)
            m_prev[b] = m_new
    for b in range(n_seq):
        c_ref[b] = c_state[b]
        n_ref[b] = n_state[b]
        m_ref[b] = m_prev[b]
        h = jnp.concatenate(outs[b], axis=0)
        o_ref[b] = _head_norm(h, hmb) * hp_ref[1:2, :] * jax.nn.sigmoid(pd_ref[b, :, 768:1024])


def mlstm_pallas(pd, f_bias, norm_g):
    B, T, width = pd.shape
    L = MLSTM_CHUNK
    rows = L * min(CHUNKS_PER_STEP, T // L)
    n_seq = SEQS_PER_STEP if B % SEQS_PER_STEP == 0 else 1
    n = GROUP_HEADS * HEAD_DIM
    consts = (_per_head_rows(f_bias, norm_g),) + _chunk_consts(L, rows // L)
    return pl.pallas_call(
        _mlstm_kernel,
        grid=(B // n_seq, T // rows),
        in_specs=[pl.BlockSpec((n_seq, rows, width), lambda b, t: (b, t, 0))] + [_const_spec(a) for a in consts],
        out_specs=pl.BlockSpec((n_seq, rows, 256), lambda b, t: (b, t, 0)),
        out_shape=jax.ShapeDtypeStruct((B, T, 256), jnp.float32),
        scratch_shapes=[pltpu.VMEM((n_seq, n, n), jnp.float32), pltpu.VMEM((n_seq, 1, 256), jnp.float32),
                        pltpu.VMEM((n_seq, 1, 256), jnp.float32)],
        compiler_params=pltpu.CompilerParams(dimension_semantics=("arbitrary", "arbitrary"),
                                             vmem_limit_bytes=VMEM_LIMIT_BYTES),
        name="mlstm",
    )(pd, *consts)


def kernel(x, c, ada_w, ada_b, norm_g, ffn1_w13, ffn1_w2, ffn2_w13, ffn2_w2, w_in, b_in, q_norm_g, k_norm_g, cmp_pos, cmp_k_w1, cmp_k_w2, cmp_v_w1, cmp_v_w2, t5_table, sc_conv_w, gdn_conv_w, gdn_A_log, gdn_dt_bias, gdn_norm_g, mlstm_f_bias, mlstm_norm_g, mix_norm_g, w_out):
    B, T, D = x.shape
    depth = ada_w.shape[0]
    bf16 = jnp.bfloat16
    x2 = x.reshape(B * T, D)
    tables = nsa_bias_tables(t5_table, T)
    ovt = nsa_overlap_t(T)
    for l in range(depth):
        mod = ada_modulation(c, ada_w[l].astype(bf16), ada_b[l][None, :])
        mod = mod.reshape(B, N_SUBLAYERS, 3, 1, D)
        w13 = ffn1_w13[l].astype(bf16)
        x2 = ffn_half_step(x2, norm_g[l, 0][None, :], mod[:, 0, 0], mod[:, 0, 1], mod[:, 0, 2],
                           w13[:, :D_FF], w13[:, D_FF:], ffn1_w2[l].astype(bf16), T)
        ws, bs = split_in_weights(w_in[l], b_in[l])
        pa, pb, pc, pd = in_projection(x2, norm_g[l, 1][None, :], mod[:, 1, 0], mod[:, 1, 1], ws, bs, T)
        pa, pb, pc, pd = (a.reshape(B, T, -1) for a in (pa, pb, pc, pd))
        y_a = nsa_mixer_pallas(pa, q_norm_g[l], k_norm_g[l], cmp_pos[l], cmp_k_w1[l], cmp_k_w2[l],
                               cmp_v_w1[l], cmp_v_w2[l], tables, ovt, mix_norm_g[l, 0][None, :])
        y_b = short_conv_pallas(pb, sc_conv_w[l], mix_norm_g[l, 1][None, :])
        y_c = gated_deltanet_pallas(pc, gdn_conv_w[l], gdn_A_log[l], gdn_dt_bias[l], gdn_norm_g[l])
        y_d = mlstm_pallas(pd, mlstm_f_bias[l], mlstm_norm_g[l])
        ys = [y.reshape(B * T, GROUP_WIDTH) for y in (y_a, y_b, y_c, y_d)]
        x2 = out_projection(x2, ys, mod[:, 1, 2], w_out[l].astype(bf16), T)
        w13 = ffn2_w13[l].astype(bf16)
        x2 = ffn_half_step(x2, norm_g[l, 2][None, :], mod[:, 2, 0], mod[:, 2, 1], mod[:, 2, 2],
                           w13[:, :D_FF], w13[:, D_FF:], ffn2_w2[l].astype(bf16), T)
    return x2.reshape(B, T, D)
```

```python
import functools
import math

import jax
import jax.numpy as jnp
import numpy as np
from jax import lax
from jax.experimental import pallas as pl
from jax.experimental.pallas import tpu as pltpu

D_MODEL = 1024
HEAD_DIM = 64
GROUP_HEADS = 4
GROUP_WIDTH = GROUP_HEADS * HEAD_DIM
CMP_STRIDE = 16
CMP_BLOCK = 32
SLC_BLOCK = 64
N_SLC = 16
N_LOCAL_SLC = 2
WINDOW = 512
Q_BLOCK = 128
FORCE = 1e6
N_BUCKETS = 32
MAX_DISTANCE = 128
GDN_CHUNK = 64
MLSTM_CHUNK = 64
D_FF = 2816
N_SUBLAYERS = 3
EPS = 1e-6

IN_LAYOUT = (
    ("a_q", 256), ("a_k_cmp", 64), ("a_v_cmp", 64),
    ("a_k_slc", 64), ("a_v_slc", 64), ("a_k_win", 64), ("a_v_win", 64),
    ("a_gate", 12),
    ("b_b", 256), ("b_c", 256), ("b_x", 256),
    ("c_q", 256), ("c_k", 256), ("c_v", 256),
    ("c_beta", 4), ("c_alpha", 4), ("c_z", 256),
    ("d_q", 256), ("d_k", 256), ("d_v", 256),
    ("d_i", 4), ("d_f", 4), ("d_o", 256),
)

VMEM_LIMIT_BYTES = 56 * 1024 * 1024
FFN_TOKEN_TILE = 512
FFN_CHUNK = 256
PROJ_TOKEN_TILE = 512
NEG = -1e30
HIGHEST = lax.Precision.HIGHEST


def _modulated_norm(x, g, scale, shift):
    y = x * lax.rsqrt(jnp.mean(x * x, axis=-1, keepdims=True) + EPS)
    return (y * g) * (1.0 + scale) + shift


def _ada_kernel(c_ref, w_ref, b_ref, o_ref):
    c = c_ref[...]
    cond = c * jax.nn.sigmoid(c)
    o_ref[...] = jnp.dot(cond.astype(jnp.bfloat16), w_ref[...],
                         preferred_element_type=jnp.float32) + b_ref[...]


def ada_modulation(c, w, b):
    B, D = c.shape
    N = w.shape[1]
    tn = 1152
    return pl.pallas_call(
        _ada_kernel,
        grid=(N // tn,),
        in_specs=[pl.BlockSpec((B, D), lambda j: (0, 0)),
                  pl.BlockSpec((D, tn), lambda j: (0, j)),
                  pl.BlockSpec((1, tn), lambda j: (0, j))],
        out_specs=pl.BlockSpec((B, tn), lambda j: (0, j)),
        out_shape=jax.ShapeDtypeStruct((B, N), jnp.float32),
        name="ada_modulation",
    )(c, w, b)


def _ffn_kernel(x_ref, g_ref, shift_ref, scale_ref, gate_ref, w1_ref, w3_ref, w2_ref, o_ref, act_ref):
    x = x_ref[...]
    h = _modulated_norm(x, g_ref[...], scale_ref[0], shift_ref[0]).astype(jnp.bfloat16)
    n_chunks = act_ref.shape[1] // FFN_CHUNK
    for ci in range(n_chunks):
        cs = slice(ci * FFN_CHUNK, (ci + 1) * FFN_CHUNK)
        a = jnp.dot(h, w1_ref[:, cs], preferred_element_type=jnp.float32)
        b = jnp.dot(h, w3_ref[:, cs], preferred_element_type=jnp.float32)
        act_ref[:, cs] = (a * jax.nn.sigmoid(a) * b).astype(jnp.bfloat16)
    y = jnp.dot(act_ref[...], w2_ref[...], preferred_element_type=jnp.float32)
    o_ref[...] = x + (0.5 * gate_ref[0]) * y


def ffn_half_step(x2, g, shift, scale, gate, w1, w3, w2, tokens_per_batch):
    M, D = x2.shape
    F = w2.shape[0]
    tm = FFN_TOKEN_TILE
    tiles_per_batch = tokens_per_batch // tm
    resident = dict(pipeline_mode=pl.Buffered(1))
    mod_spec = pl.BlockSpec((1, 1, D), lambda i: (i // tiles_per_batch, 0, 0))
    return pl.pallas_call(
        _ffn_kernel,
        grid=(M // tm,),
        in_specs=[pl.BlockSpec((tm, D), lambda i: (i, 0)),
                  pl.BlockSpec((1, D), lambda i: (0, 0)),
                  mod_spec, mod_spec, mod_spec,
                  pl.BlockSpec((D, F), lambda i: (0, 0), **resident),
                  pl.BlockSpec((D, F), lambda i: (0, 0), **resident),
                  pl.BlockSpec((F, D), lambda i: (0, 0), **resident)],
        out_specs=pl.BlockSpec((tm, D), lambda i: (i, 0)),
        out_shape=jax.ShapeDtypeStruct((M, D), jnp.float32),
        scratch_shapes=[pltpu.VMEM((tm, F), jnp.bfloat16)],
        compiler_params=pltpu.CompilerParams(dimension_semantics=("arbitrary",),
                                             vmem_limit_bytes=VMEM_LIMIT_BYTES),
        name="ffn_half_step",
    )(x2, g, shift, scale, gate, w1, w3, w2)


def _in_proj_kernel(x_ref, g_ref, shift_ref, scale_ref, wa_ref, wb_ref, wc_ref, wd_ref,
                    ba_ref, bb_ref, bc_ref, bd_ref, oa_ref, ob_ref, oc_ref, od_ref):
    h = _modulated_norm(x_ref[...], g_ref[...], scale_ref[0], shift_ref[0]).astype(jnp.bfloat16)
    for w_ref, b_ref, o_ref in ((wa_ref, ba_ref, oa_ref), (wb_ref, bb_ref, ob_ref),
                                (wc_ref, bc_ref, oc_ref), (wd_ref, bd_ref, od_ref)):
        o_ref[...] = jnp.dot(h, w_ref[...], preferred_element_type=jnp.float32) + b_ref[...]


def in_projection(x2, g, shift, scale, ws, bs, tokens_per_batch):
    M, D = x2.shape
    tm = PROJ_TOKEN_TILE
    tiles_per_batch = tokens_per_batch // tm
    mod_spec = pl.BlockSpec((1, 1, D), lambda i: (i // tiles_per_batch, 0, 0))
    const = lambda a, **kw: pl.BlockSpec(a.shape, lambda i: (0, 0), **kw)
    return pl.pallas_call(
        _in_proj_kernel,
        grid=(M // tm,),
        in_specs=[pl.BlockSpec((tm, D), lambda i: (i, 0)), const(g), mod_spec, mod_spec]
                 + [const(w, pipeline_mode=pl.Buffered(1)) for w in ws] + [const(b) for b in bs],
        out_specs=[pl.BlockSpec((tm, w.shape[1]), lambda i: (i, 0)) for w in ws],
        out_shape=[jax.ShapeDtypeStruct((M, w.shape[1]), jnp.float32) for w in ws],
        compiler_params=pltpu.CompilerParams(dimension_semantics=("arbitrary",),
                                             vmem_limit_bytes=VMEM_LIMIT_BYTES),
        name="in_projection",
    )(x2, g, shift, scale, *ws, *bs)


def _group_columns():
    offs, o = {}, 0
    for n, s in IN_LAYOUT:
        offs[n] = np.arange(o, o + s)
        o += s
    cat = lambda names: np.concatenate([offs[n] for n in names])
    return (cat(["a_q", "a_k_cmp", "a_v_cmp", "a_k_slc", "a_v_slc", "a_k_win", "a_v_win", "a_gate"]),
            cat(["b_b", "b_c", "b_x"]),
            cat(["c_q", "c_k", "c_v", "c_z", "c_beta", "c_alpha"]),
            cat(["d_q", "d_k", "d_v", "d_o", "d_i", "d_f"]))


GROUP_SLAB_WIDTH = (768, 768, 1152, 1152)


def split_in_weights(w, b):
    ws, bs = [], []
    for idx, width in zip(_group_columns(), GROUP_SLAB_WIDTH):
        pad = width - idx.size
        ws.append(jnp.pad(w[:, idx], ((0, 0), (0, pad))).astype(jnp.bfloat16))
        bs.append(jnp.pad(b[idx], (0, pad))[None, :])
    return ws, bs


def _out_proj_kernel(x_ref, ya_ref, yb_ref, yc_ref, yd_ref, gate_ref, w_ref, o_ref):
    z = None
    for g, y_ref in enumerate((ya_ref, yb_ref, yc_ref, yd_ref)):
        part = jnp.dot(y_ref[...].astype(jnp.bfloat16), w_ref[g * GROUP_WIDTH:(g + 1) * GROUP_WIDTH, :],
                       preferred_element_type=jnp.float32)
        z = part if z is None else z + part
    o_ref[...] = x_ref[...] + gate_ref[0] * z


def out_projection(x2, ys, gate, w, tokens_per_batch):
    M, D = x2.shape
    tm = PROJ_TOKEN_TILE
    tiles_per_batch = tokens_per_batch // tm
    return pl.pallas_call(
        _out_proj_kernel,
        grid=(M // tm,),
        in_specs=[pl.BlockSpec((tm, D), lambda i: (i, 0))]
                 + [pl.BlockSpec((tm, GROUP_WIDTH), lambda i: (i, 0)) for _ in ys]
                 + [pl.BlockSpec((1, 1, D), lambda i: (i // tiles_per_batch, 0, 0)),
                    pl.BlockSpec(w.shape, lambda i: (0, 0), pipeline_mode=pl.Buffered(1))],
        out_specs=pl.BlockSpec((tm, D), lambda i: (i, 0)),
        out_shape=jax.ShapeDtypeStruct((M, D), jnp.float32),
        compiler_params=pltpu.CompilerParams(dimension_semantics=("arbitrary",),
                                             vmem_limit_bytes=VMEM_LIMIT_BYTES),
        name="out_projection",
    )(x2, *ys, gate, w)


def t5_bucket(dist):
    n = jnp.maximum(dist, 0)
    max_exact = N_BUCKETS // 2
    nf = jnp.maximum(n, 1).astype(jnp.float32)
    large = max_exact + (jnp.log(nf / max_exact) / math.log(MAX_DISTANCE / max_exact)
                         * (N_BUCKETS - max_exact)).astype(jnp.int32)
    large = jnp.minimum(large, N_BUCKETS - 1)
    return jnp.where(n < max_exact, n, large)


def _nt_dot(a, b):
    return lax.dot_general(a, b, (((1,), (1,)), ((), ())), preferred_element_type=jnp.float32)


def _stack_heads(qn):
    lane = lax.broadcasted_iota(jnp.int32, (qn.shape[0], 128), 1)
    lo = lane < HEAD_DIM
    zero = jnp.zeros((), qn.dtype)
    halves = (qn[:, 0:128], qn[:, 128:256])
    return jnp.concatenate([jnp.where(lo if h % 2 == 0 else ~lo, halves[h // 2], zero)
                            for h in range(GROUP_HEADS)], axis=0)


def _nsa_prep_kernel(q_ref, kvs_ref, kvw_ref, qg_ref, kg_ref, seg_ref,
                     qn_ref, qnt_ref, kxs_ref, v1s_ref, kkw_ref, v1w_ref):
    tm = q_ref.shape[0]
    q = q_ref[...]
    ss = jnp.dot(q * q, seg_ref[...], precision=HIGHEST, preferred_element_type=jnp.float32)
    qn = q * lax.rsqrt(ss * (1.0 / HEAD_DIM) + EPS) * qg_ref[...]
    qn_ref[...] = qn.astype(qn_ref.dtype)
    qnt_ref[...] = qn.T.astype(qnt_ref.dtype)
    lane = lax.broadcasted_iota(jnp.int32, (tm, 128), 1)
    lo = lane < HEAD_DIM

    def split(x):
        xr = pltpu.roll(x, HEAD_DIM, 1)
        ss = jnp.sum(jnp.where(lo, x * x, 0.0), axis=-1, keepdims=True)
        kk = jnp.where(lo, x, xr) * lax.rsqrt(ss * (1.0 / HEAD_DIM) + EPS) * kg_ref[...]
        return kk, jnp.where(lo, xr, 1.0)

    kk, v1 = split(kvs_ref[...])
    tok = pl.program_id(1) * tm + lax.broadcasted_iota(jnp.int32, (tm, 128), 0)
    onehot = jnp.where(lane == tok // SLC_BLOCK, 1.0, 0.0)
    kxs_ref[...] = jnp.concatenate([kk, onehot], axis=1).astype(kxs_ref.dtype)
    v1s_ref[...] = v1.T.astype(v1s_ref.dtype)
    kk, v1 = split(kvw_ref[...])
    kkw_ref[...] = kk.astype(kkw_ref.dtype)
    v1w_ref[...] = v1.T.astype(v1w_ref.dtype)


def nsa_prep(pa, qg, kg2, seg):
    B, T, _ = pa.shape
    tm = 512
    bf16 = jnp.bfloat16
    col = lambda w, j: pl.BlockSpec((None, tm, w), lambda b, i: (b, i, j))
    rowblk = lambda h: pl.BlockSpec((None, h, tm), lambda b, i: (b, 0, i))
    const = lambda shape: pl.BlockSpec(shape, lambda b, i: (0, 0))
    return pl.pallas_call(
        _nsa_prep_kernel,
        grid=(B, T // tm),
        in_specs=[col(256, 0), col(128, 3), col(128, 4), const((1, 256)), const((1, 128)), const((256, 256))],
        out_specs=[col(256, 0), rowblk(256), col(256, 0), rowblk(128), col(128, 0), rowblk(128)],
        out_shape=[jax.ShapeDtypeStruct((B, T, 256), bf16), jax.ShapeDtypeStruct((B, 256, T), bf16),
                   jax.ShapeDtypeStruct((B, T, 256), bf16), jax.ShapeDtypeStruct((B, 128, T), bf16),
                   jax.ShapeDtypeStruct((B, T, 128), bf16), jax.ShapeDtypeStruct((B, 128, T), bf16)],
        name="nsa_prep",
    )(pa, pa, pa, qg, kg2, seg)


def _compress_kernel(kch_ref, vch_ref, plo_ref, phi_ref, w1k_ref, w2k_ref, w1v_ref, w2v_ref, kg_ref,
                     kkc_ref, vvc_ref):
    nrow = kch_ref.shape[0]
    half = CMP_STRIDE * HEAD_DIM
    bf16 = jnp.bfloat16
    valid = lax.broadcasted_iota(jnp.int32, (nrow, 128), 0) < nrow - 1

    def comp(ch, w1_ref, w2_ref):
        a = jnp.dot((ch + plo_ref[...]).astype(bf16), w1_ref[0:half, :], preferred_element_type=jnp.float32)
        b = jnp.dot((ch + phi_ref[...]).astype(bf16), w1_ref[half:2 * half, :],
                    preferred_element_type=jnp.float32)
        hid = a + pltpu.roll(b, nrow - 1, 0)
        hid = hid * jax.nn.sigmoid(hid)
        return jnp.dot(hid.astype(bf16), w2_ref[...], preferred_element_type=jnp.float32)

    kc = comp(kch_ref[...], w1k_ref, w2k_ref)
    kc = kc * lax.rsqrt(jnp.mean(kc * kc, axis=-1, keepdims=True) + EPS) * kg_ref[...]
    kkc_ref[...] = jnp.where(valid, kc, 0.0).astype(kkc_ref.dtype)
    vc = comp(vch_ref[...], w1v_ref, w2v_ref)
    vvc_ref[...] = jnp.where(valid, vc, 0.0).astype(vvc_ref.dtype)


def nsa_compress(kch, vch, plo, phi, w1k, w2k, w1v, w2v, kg2):
    B, nrow, width = kch.shape
    bf16 = jnp.bfloat16
    full = lambda a: pl.BlockSpec(a.shape, lambda b: (0,) * a.ndim)
    bspec = pl.BlockSpec((None, nrow, width), lambda b: (b, 0, 0))
    ospec = pl.BlockSpec((None, nrow, 128), lambda b: (b, 0, 0))
    return pl.pallas_call(
        _compress_kernel,
        grid=(B,),
        in_specs=[bspec, bspec, full(plo), full(phi), full(w1k), full(w2k), full(w1v), full(w2v), full(kg2)],
        out_specs=[ospec, ospec],
        out_shape=[jax.ShapeDtypeStruct((B, nrow, 128), bf16), jax.ShapeDtypeStruct((B, nrow, 128), bf16)],
        name="nsa_compress",
    )(kch, vch, plo, phi, w1k, w2k, w1v, w2v, kg2)


def _cmp_select_kernel(qn_ref, kkc_ref, vvc_ref, cb_ref, ovt_ref, ocmp_ref, pen_ref, sc_ref, *, n_sel):
    i = pl.program_id(1)
    tq = qn_ref.shape[0]
    nblk = ovt_ref.shape[0]
    bf16 = jnp.bfloat16
    q4 = _stack_heads(qn_ref[...])
    s = _nt_dot(q4, kkc_ref[...]) + cb_ref[...]
    m = jnp.max(s, axis=-1, keepdims=True)
    e = jnp.where(s > 0.5 * NEG, jnp.exp(s - m), 0.0)
    p = e * (1.0 / jnp.maximum(jnp.sum(e, axis=-1, keepdims=True), 1e-30))
    o = jnp.dot(p.astype(bf16), vvc_ref[...], preferred_element_type=jnp.float32)
    lo = lax.broadcasted_iota(jnp.int32, (tq, 128), 1) < HEAD_DIM
    ocmp_ref[:, 0:128] = jnp.where(lo, o[0:tq], o[tq:2 * tq])
    ocmp_ref[:, 128:256] = jnp.where(lo, o[2 * tq:3 * tq], o[3 * tq:4 * tq])
    psum = p[0:tq] + p[tq:2 * tq] + p[2 * tq:3 * tq] + p[3 * tq:4 * tq]
    hi = psum.astype(bf16)
    lo_part = (psum - hi.astype(jnp.float32)).astype(bf16)
    score = _nt_dot(ovt_ref[...], hi) + _nt_dot(ovt_ref[...], lo_part)
    blk = lax.broadcasted_iota(jnp.int32, (nblk, tq), 0)
    qpos = i * tq + lax.broadcasted_iota(jnp.int32, (nblk, tq), 1)
    cur = qpos // SLC_BLOCK
    forced = (blk == 0) | ((cur - blk >= 0) & (cur - blk < N_LOCAL_SLC))
    score = jnp.where(forced, FORCE, score)
    score = jnp.where(blk <= cur, score, -FORCE)
    sc_ref[...] = score
    cnt = jnp.zeros((nblk, tq), jnp.float32)
    for jp in range(nblk):
        row = sc_ref[jp:jp + 1, :]
        tie = jnp.where(blk > jp, 1.0, 0.0)
        cnt = cnt + jnp.where(row > score, 1.0, jnp.where(row == score, tie, 0.0))
    pen = jnp.where(cnt < n_sel, 0.0, NEG)
    if nblk < 128:
        pen = jnp.concatenate([pen, jnp.zeros((128 - nblk, tq), jnp.float32)], axis=0)
    pen_ref[...] = pen.astype(pen_ref.dtype)


def nsa_cmp_select(qn, kkc, vvc, cbias, ovt, n_sel):
    B, T, _ = qn.shape
    nrow = kkc.shape[1]
    nblk = ovt.shape[0]
    tq = Q_BLOCK
    return pl.pallas_call(
        functools.partial(_cmp_select_kernel, n_sel=n_sel),
        grid=(B, T // tq),
        in_specs=[pl.BlockSpec((None, tq, 256), lambda b, i: (b, i, 0)),
                  pl.BlockSpec((None, nrow, 128), lambda b, i: (b, 0, 0)),
                  pl.BlockSpec((None, nrow, 128), lambda b, i: (b, 0, 0)),
                  pl.BlockSpec((None, 4 * tq, nrow), lambda b, i: (i, 0, 0)),
                  pl.BlockSpec((nblk, nrow), lambda b, i: (0, 0))],
        out_specs=[pl.BlockSpec((None, tq, 256), lambda b, i: (b, i, 0)),
                   pl.BlockSpec((None, None, 128, tq), lambda b, i: (b, i, 0, 0))],
        out_shape=[jax.ShapeDtypeStruct((B, T, 256), jnp.float32),
                   jax.ShapeDtypeStruct((B, T // tq, 128, tq), jnp.bfloat16)],
        scratch_shapes=[pltpu.VMEM((nblk, tq), jnp.float32)],
        name="nsa_cmp_select",
    )(qn, kkc, vvc, cbias, ovt)


NSA_KEY_TILE = 256
NSA_QUERY_TILE = 256


def _nsa_attn_kernel(qnt_ref, pen_ref, ocmp_ref, gate_ref, kxs_ref, v1s_ref, kkw_ref, v1w_ref,
                     tabs_ref, tabw_ref, mixg_ref, o_ref, acc_ref, m_ref, sa_ref, sb_ref):
    i = pl.program_id(1)
    tq = o_ref.shape[0]
    kt = tabs_ref.shape[1]
    f32, bf16 = jnp.float32, jnp.bfloat16
    qt = qnt_ref[...]
    lo = lax.broadcasted_iota(jnp.int32, (128, tq), 0) < HEAD_DIM
    zero = jnp.zeros((), qt.dtype)
    q4t = jnp.concatenate([jnp.where(lo if h % 2 == 0 else ~lo, qt[128 * (h // 2):128 * (h // 2) + 128], zero)
                           for h in range(GROUP_HEADS)], axis=1)
    pen = jnp.concatenate([pen_ref[t] for t in range(pen_ref.shape[0])], axis=1)
    rhs_sel = jnp.concatenate([q4t, jnp.concatenate([pen] * GROUP_HEADS, axis=1)], axis=0)

    def branch(rhs, k_ref, v_ref, tab_ref, first_tile, last_tile):
        n_tab = tab_ref.shape[0] - 1
        m_ref[...] = jnp.full(m_ref.shape, NEG, f32)
        acc_ref[...] = jnp.zeros(acc_ref.shape, f32)

        def key_offset(j):
            return pl.multiple_of(jnp.minimum(j, last_tile) * kt, kt)

        def scores(j):
            delta = jnp.where(j > last_tile, n_tab, jnp.minimum(i - j * (kt // tq), n_tab - 1))
            return jnp.dot(k_ref[pl.ds(key_offset(j), kt), :], rhs, preferred_element_type=f32) + tab_ref[delta]

        def absorb(s, j):
            m_old = m_ref[...]
            m_new = jnp.maximum(m_old, jnp.max(s, axis=0, keepdims=True))
            p = jnp.exp(s - m_new)
            pv = jnp.dot(v_ref[:, pl.ds(key_offset(j), kt)], p.astype(bf16), preferred_element_type=f32)
            acc_ref[...] = jnp.exp(m_old - m_new) * acc_ref[...] + pv
            m_ref[...] = m_new

        sa_ref[...] = scores(first_tile)

        def body(t, carry):
            j = first_tile + 2 * t
            sb_ref[...] = scores(j + 1)
            absorb(sa_ref[...], j)
            sa_ref[...] = scores(j + 2)
            absorb(sb_ref[...], j + 1)
            return carry

        lax.fori_loop(0, (last_tile - first_tile) // 2 + 1, body, 0)
        acc = acc_ref[...]
        return acc[0:HEAD_DIM] / jnp.maximum(acc[HEAD_DIM:2 * HEAD_DIM], 1e-30)

    last = (i * tq) // kt
    o_s = branch(rhs_sel, kxs_ref, v1s_ref, tabs_ref, 0, last)
    o_w = branch(q4t, kkw_ref, v1w_ref, tabw_ref, (jnp.maximum(i - WINDOW // tq, 0) * tq) // kt, last)
    gt = jax.nn.sigmoid(gate_ref[...]).T
    oct_ = ocmp_ref[...].T
    parts = []
    for h in range(GROUP_HEADS):
        cols = slice(h * tq, (h + 1) * tq)
        parts.append(gt[3 * h:3 * h + 1] * oct_[h * HEAD_DIM:(h + 1) * HEAD_DIM]
                     + gt[3 * h + 1:3 * h + 2] * o_s[:, cols] + gt[3 * h + 2:3 * h + 3] * o_w[:, cols])
    yt = jnp.concatenate(parts, axis=0)
    yt = yt * lax.rsqrt(jnp.mean(yt * yt, axis=0, keepdims=True) + EPS) * mixg_ref[...]
    o_ref[...] = yt.T


def nsa_attention(qnt, pen, ocmp, pa, kxs, v1s, kkw, v1w, tab_s, tab_w, mixg):
    B, _, T = qnt.shape
    tq = min(NSA_QUERY_TILE, T)
    tile = lambda w, j: pl.BlockSpec((None, tq, w), lambda b, i: (b, i, j))
    whole = lambda w: pl.BlockSpec((None, T, w), lambda b, i: (b, 0, 0))
    whole_t = pl.BlockSpec((None, 128, T), lambda b, i: (b, 0, 0))
    full = lambda a: pl.BlockSpec(a.shape, lambda b, i: (0,) * a.ndim)
    return pl.pallas_call(
        _nsa_attn_kernel,
        grid=(B, T // tq),
        in_specs=[pl.BlockSpec((None, 256, tq), lambda b, i: (b, 0, i)),
                  pl.BlockSpec((None, tq // Q_BLOCK, 128, Q_BLOCK), lambda b, i: (b, i, 0, 0)),
                  tile(256, 0), tile(128, 5),
                  whole(256), whole_t, whole(128), whole_t, full(tab_s), full(tab_w), full(mixg)],
        out_specs=tile(256, 0),
        out_shape=jax.ShapeDtypeStruct((B, T, 256), jnp.float32),
        scratch_shapes=[pltpu.VMEM((128, GROUP_HEADS * tq), jnp.float32),
                        pltpu.VMEM((1, GROUP_HEADS * tq), jnp.float32),
                        pltpu.VMEM((tab_s.shape[1], GROUP_HEADS * tq), jnp.float32),
                        pltpu.VMEM((tab_s.shape[1], GROUP_HEADS * tq), jnp.float32)],
        compiler_params=pltpu.CompilerParams(dimension_semantics=("arbitrary", "arbitrary"),
                                             vmem_limit_bytes=VMEM_LIMIT_BYTES),
        name="nsa_attention",
    )(qnt, pen, ocmp, pa, kxs, v1s, kkw, v1w, tab_s, tab_w, mixg)


def _toeplitz(w, rows, cols):
    H, n = w.shape
    period = n + 1
    flat = jnp.tile(jnp.pad(w, ((0, 0), (0, 1))), (1, rows + 1))[:, :rows * (period + 1)]
    return jnp.flip(flat.reshape(H, rows, period + 1)[:, :, :cols], axis=2)


def nsa_bias_tables(t5_table, T):
    H = GROUP_HEADS
    tq = Q_BLOCK
    ta = min(NSA_QUERY_TILE, T)
    kt = min(NSA_KEY_TILE, T)
    nrow = T // CMP_STRIDE
    dmax = WINDOW + 2 * ta + kt
    onehot = (t5_bucket(jnp.arange(dmax))[:, None] == jnp.arange(N_BUCKETS)[None, :]).astype(jnp.float32)
    by_dist = jnp.dot(onehot, t5_table, precision=HIGHEST).T
    neg = lambda n: jnp.full((H, n), NEG, jnp.float32)
    f_s = jnp.concatenate([neg(kt - 1), by_dist], axis=1)
    f_w = jnp.concatenate([neg(kt - 1), by_dist[:, :WINDOW], neg(dmax - WINDOW)], axis=1)

    def tiles(f, n):
        out = [_toeplitz(f[:, k * ta:k * ta + ta + kt - 1], ta, kt).reshape(H * ta, kt).T for k in range(n)]
        return jnp.stack(out + [jnp.full((kt, H * ta), NEG, jnp.float32)])

    n_s = -(-(kt + MAX_DISTANCE) // ta) + 1
    n_w = WINDOW // ta + kt // ta
    tab_s, tab_w = tiles(f_s, n_s), tiles(f_w, n_w)
    per_tile = tq // CMP_STRIDE
    d0 = -CMP_STRIDE * (nrow - 1) - (CMP_BLOCK - 1)
    n16 = 2 * nrow + per_tile - 1
    n_pos = CMP_STRIDE * n16 + d0
    far = jnp.broadcast_to(by_dist[:, -1:], (H, max(n_pos - dmax, 0)))
    w16 = jnp.concatenate([neg(-d0), by_dist[:, :n_pos], far], axis=1).reshape(H, n16, CMP_STRIDE)
    base = jnp.stack([w16[:, a:a + 2 * nrow, :] for a in range(per_tile)], axis=1)
    base = jnp.flip(base.transpose(0, 1, 3, 2).reshape(H * tq, 2 * nrow), axis=1)
    cbias = jnp.stack([base[:, nrow - per_tile * i: 2 * nrow - per_tile * i] for i in range(T // tq)])
    return tab_s, tab_w, cbias


def nsa_overlap_t(T):
    nrow = T // CMP_STRIDE
    nblk = T // SLC_BLOCK
    ci = np.arange(nrow)[None, :]
    bj = np.arange(nblk)[:, None]
    ov = (ci * CMP_STRIDE < (bj + 1) * SLC_BLOCK) & (ci * CMP_STRIDE + CMP_BLOCK > bj * SLC_BLOCK)
    ov = ov & (ci < nrow - 1)
    return jnp.asarray(ov.astype(np.float32), dtype=jnp.bfloat16)


def nsa_mixer_pallas(pa, q_norm_g, k_norm_g, cmp_pos, cmp_k_w1, cmp_k_w2, cmp_v_w1, cmp_v_w2,
                     tables, ovt, mixg):
    B, T, _ = pa.shape
    bf16 = jnp.bfloat16
    tab_s, tab_w, cbias = tables
    qg = (jnp.tile(q_norm_g, GROUP_HEADS) * HEAD_DIM ** -0.5)[None, :]
    kg2 = jnp.tile(k_norm_g, 2)[None, :]
    qn, qnt, kxs, v1s, kkw, v1w = nsa_prep(pa, qg, kg2, _head_mask())
    nrow = T // CMP_STRIDE
    kch = pa[:, :, 256:320].reshape(B, nrow, CMP_STRIDE * HEAD_DIM)
    vch = pa[:, :, 320:384].reshape(B, nrow, CMP_STRIDE * HEAD_DIM)
    plo = cmp_pos[:CMP_STRIDE].reshape(1, -1)
    phi = cmp_pos[CMP_STRIDE:].reshape(1, -1)
    dup = lambda w: jnp.concatenate([w, w], axis=1).astype(bf16)
    kkc, vvc = nsa_compress(kch, vch, plo, phi, cmp_k_w1.astype(bf16), dup(cmp_k_w2),
                            cmp_v_w1.astype(bf16), dup(cmp_v_w2), kg2)
    n_sel = min(N_SLC, T // SLC_BLOCK)
    ocmp, pen = nsa_cmp_select(qn, kkc, vvc, cbias, ovt, n_sel)
    mixg_col = jnp.broadcast_to(mixg.reshape(-1, 1), (GROUP_WIDTH, min(NSA_QUERY_TILE, T)))
    return nsa_attention(qnt, pen, ocmp, pa, kxs, v1s, kkw, v1w, tab_s, tab_w, mixg_col)


CONV_CARRY_ROWS = 8


def _shifted(cat, shift, rows):
    return pltpu.roll(cat, shift, 0)[CONV_CARRY_ROWS:CONV_CARRY_ROWS + rows]


def _sconv_kernel(pb_ref, w_ref, g_ref, o_ref, carry_ref):
    rows = pb_ref.shape[0]

    @pl.when(pl.program_id(1) == 0)
    def _():
        carry_ref[...] = jnp.zeros(carry_ref.shape, jnp.float32)

    u = pb_ref[:, 256:512] * pb_ref[:, 512:768]
    cat = jnp.concatenate([carry_ref[...], u], axis=0)
    conv = u * w_ref[2:3, :] + _shifted(cat, 1, rows) * w_ref[1:2, :] + _shifted(cat, 2, rows) * w_ref[0:1, :]
    carry_ref[...] = u[rows - CONV_CARRY_ROWS:rows]
    y = pb_ref[:, 0:256] * conv
    o_ref[...] = y * lax.rsqrt(jnp.mean(y * y, axis=-1, keepdims=True) + EPS) * g_ref[...]


def short_conv_pallas(pb, conv_w, mixg):
    B, T, _ = pb.shape
    rows = 512
    w = jnp.pad(conv_w, ((0, 8 - conv_w.shape[0]), (0, 0)))
    return pl.pallas_call(
        _sconv_kernel,
        grid=(B, T // rows),
        in_specs=[pl.BlockSpec((None, rows, 768), lambda b, t: (b, t, 0)),
                  pl.BlockSpec((8, 256), lambda b, t: (0, 0)),
                  pl.BlockSpec((1, 256), lambda b, t: (0, 0))],
        out_specs=pl.BlockSpec((None, rows, 256), lambda b, t: (b, t, 0)),
        out_shape=jax.ShapeDtypeStruct((B, T, 256), jnp.float32),
        scratch_shapes=[pltpu.VMEM((CONV_CARRY_ROWS, 256), jnp.float32)],
        compiler_params=pltpu.CompilerParams(dimension_semantics=("arbitrary", "arbitrary")),
        name="short_conv",
    )(pb, w, mixg)


CHUNKS_PER_STEP = 4
SEQS_PER_STEP = 2


def _tn_dot(a, b):
    return lax.dot_general(a, b, (((0,), (0,)), ((), ())), preferred_element_type=jnp.float32)


def _split_bf16(a, parts):
    out, rest = [], a
    for _ in range(parts):
        piece = rest.astype(jnp.bfloat16)
        out.append(piece)
        rest = rest - piece.astype(jnp.float32)
    return out


def _dot_sel(a, sel, parts=3):
    return sum(jnp.dot(p, sel, preferred_element_type=jnp.float32) for p in _split_bf16(a, parts))


def _sel_dot(sel, a, parts=3):
    return sum(jnp.dot(sel, p, preferred_element_type=jnp.float32) for p in _split_bf16(a, parts))


def _softplus(x):
    return jnp.maximum(x, 0.0) + jnp.log(1.0 + jnp.exp(-jnp.abs(x)))


def _block_diag(a, hmb):
    return jnp.concatenate([a.astype(jnp.bfloat16)] * GROUP_HEADS, axis=0) * hmb


def _diag_row(a, eye):
    return jnp.sum(a * eye, axis=0, keepdims=True)


def _head_norm(o, hmb):
    return o * lax.rsqrt(_dot_sel(o * o, hmb, parts=2) * (1.0 / HEAD_DIM) + EPS)


def _head_l2(o, hmb):
    return o * lax.rsqrt(_dot_sel(o * o, hmb, parts=2) + EPS)


def _xform_masks(L):
    i = np.arange(L)[:, None]
    j = np.arange(L)[None, :]
    masks = []
    s = 1
    while s < L:
        masks.append((i // (2 * s) == j // (2 * s)) & (i % (2 * s) >= s) & (j % (2 * s) < s))
        s *= 2
    masks += [i == j, j <= i, j < i]
    return jnp.asarray(np.tile(np.stack(masks).astype(np.float32), (1, 1, GROUP_HEADS)))


def _expand_matrix(first_lane):
    e = np.zeros((128, GROUP_WIDTH), np.float32)
    for h in range(GROUP_HEADS):
        e[first_lane + h, h * HEAD_DIM:(h + 1) * HEAD_DIM] = 1.0
    return jnp.asarray(e, jnp.bfloat16)


def _head_mask():
    return jnp.asarray(np.kron(np.eye(GROUP_HEADS), np.ones((HEAD_DIM, HEAD_DIM))), jnp.float32)


def _per_head_rows(*vecs):
    rows = [jnp.repeat(v, HEAD_DIM) if v.shape[0] == GROUP_HEADS else jnp.tile(v, GROUP_HEADS) for v in vecs]
    return jnp.pad(jnp.stack(rows), ((0, 8 - len(rows)), (0, 0)))


def _chunk_consts(L, n_chunks):
    tri = np.kron(np.eye(n_chunks), np.tril(np.ones((L, L)))).astype(np.float32)
    return (_head_mask(), jnp.asarray(tri, jnp.bfloat16), _expand_matrix(0), _expand_matrix(GROUP_HEADS),
            _xform_masks(L))


def _const_spec(a):
    return pl.BlockSpec(a.shape, lambda b, t: (0,) * a.ndim)


def _gdn_kernel(pc_ref, cw_ref, hp_ref, hm_ref, tri_ref, eb_ref, ea_ref, lv_ref, o_ref, s_ref, carry_ref):
    L = GDN_CHUNK
    n_seq, rows = pc_ref.shape[0], pc_ref.shape[1]
    f32, bf16 = jnp.float32, jnp.bfloat16
    n_lev = lv_ref.shape[0] - 3

    @pl.when(pl.program_id(1) == 0)
    def _():
        s_ref[...] = jnp.zeros(s_ref.shape, f32)
        carry_ref[...] = jnp.zeros(carry_ref.shape, f32)

    hm = hm_ref[...]
    hmb = hm.astype(bf16)
    eye, incl, strict = lv_ref[n_lev], lv_ref[n_lev + 1], lv_ref[n_lev + 2]

    q_c, k_c, v_c, beta_c, gc_c = [], [], [], [], []
    for b in range(n_seq):
        x = pc_ref[b, :, 0:768]
        cat = jnp.concatenate([carry_ref[b], x], axis=0)
        conv = x * cw_ref[3:4, :]
        for s in (1, 2, 3):
            conv = conv + _shifted(cat, s, rows) * cw_ref[3 - s:4 - s, :]
        carry_ref[b] = x[rows - CONV_CARRY_ROWS:rows]
        qkv = conv * jax.nn.sigmoid(conv)
        q_all = _head_l2(qkv[:, 0:256], hmb) * HEAD_DIM ** -0.5
        k_all = _head_l2(qkv[:, 256:512], hmb)
        tail = pc_ref[b, :, 1024:1152]
        beta_all = jax.nn.sigmoid(_dot_sel(tail, eb_ref[...]))
        g = hp_ref[0:1, :] * _softplus(_dot_sel(tail, ea_ref[...]) + hp_ref[1:2, :])
        gc_all = _sel_dot(tri_ref[...], g)
        for c in range(rows // L):
            sl = slice(c * L, (c + 1) * L)
            q_c.append(q_all[sl])
            k_c.append(k_all[sl])
            v_c.append(qkv[sl, 512:768])
            beta_c.append(beta_all[sl])
            gc_c.append(gc_all[sl])
    chains = range(len(q_c))
    egc = [jnp.exp(gc) for gc in gc_c]
    kb = [k_c[i] * beta_c[i] for i in chains]
    a_mat, qk = [], []
    for i in chains:
        seg = jnp.exp(jnp.where(incl > 0.5, gc_c[i] - _diag_row(gc_c[i], eye), NEG))
        k_bd = _block_diag(k_c[i], hmb)
        a_mat.append(_nt_dot(kb[i].astype(bf16), k_bd) * seg * strict)
        qk.append((_nt_dot(q_c[i].astype(bf16), k_bd) * seg).astype(bf16))
    t_inv = [eye - a * lv_ref[0] for a in a_mat]
    for lev in range(1, n_lev):
        te = [jnp.dot(t_inv[i].astype(bf16), _block_diag(a_mat[i] * lv_ref[lev], hmb),
                      preferred_element_type=f32) for i in chains]
        t_inv = [t_inv[i] - jnp.dot(te[i].astype(bf16), _block_diag(t_inv[i], hmb), preferred_element_type=f32)
                 for i in chains]
    parts = []
    for i in chains:
        tb = t_inv[i].astype(bf16)
        u = jnp.dot(tb, _block_diag(v_c[i] * beta_c[i], hmb), preferred_element_type=f32)
        w = jnp.dot(tb, _block_diag(kb[i] * egc[i], hmb), preferred_element_type=f32)
        k_dec = k_c[i] * jnp.exp(gc_c[i][L - 1:L, :] - gc_c[i])
        parts.append((u, w.astype(bf16), (q_c[i] * egc[i]).astype(bf16), qk[i], k_dec.astype(bf16),
                      egc[i][L - 1:L, :]))
    per_seq = rows // L
    state = [s_ref[b] for b in range(n_seq)]
    outs = [[] for _ in range(n_seq)]
    for c in range(per_seq):
        for b in range(n_seq):
            u, w, q_dec, qk_c, k_dec, g_tot = parts[b * per_seq + c]
            sb = state[b].astype(bf16)
            v_new = u - jnp.dot(w, sb, preferred_element_type=f32)
            outs[b].append(jnp.dot(q_dec, sb, preferred_element_type=f32)
                           + jnp.dot(qk_c, _block_diag(v_new, hmb), preferred_element_type=f32))
            state[b] = state[b] * g_tot + _tn_dot(k_dec, v_new.astype(bf16)) * hm
    for b in range(n_seq):
        s_ref[b] = state[b]
        o = jnp.concatenate(outs[b], axis=0)
        z = pc_ref[b, :, 768:1024]
        o_ref[b] = _head_norm(o, hmb) * hp_ref[2:3, :] * (z * jax.nn.sigmoid(z))


def gated_deltanet_pallas(pc, conv_w, A_log, dt_bias, norm_g):
    B, T, width = pc.shape
    L = GDN_CHUNK
    rows = L * min(CHUNKS_PER_STEP, T // L)
    n_seq = SEQS_PER_STEP if B % SEQS_PER_STEP == 0 else 1
    n = GROUP_HEADS * HEAD_DIM
    hp = _per_head_rows(-jnp.exp(A_log), dt_bias, norm_g)
    cw = jnp.pad(conv_w, ((0, 8 - conv_w.shape[0]), (0, 0)))
    consts = (cw, hp) + _chunk_consts(L, rows // L)
    return pl.pallas_call(
        _gdn_kernel,
        grid=(B // n_seq, T // rows),
        in_specs=[pl.BlockSpec((n_seq, rows, width), lambda b, t: (b, t, 0))] + [_const_spec(a) for a in consts],
        out_specs=pl.BlockSpec((n_seq, rows, 256), lambda b, t: (b, t, 0)),
        out_shape=jax.ShapeDtypeStruct((B, T, 256), jnp.float32),
        scratch_shapes=[pltpu.VMEM((n_seq, n, n), jnp.float32),
                        pltpu.VMEM((n_seq, CONV_CARRY_ROWS, 768), jnp.float32)],
        compiler_params=pltpu.CompilerParams(dimension_semantics=("arbitrary", "arbitrary"),
                                             vmem_limit_bytes=VMEM_LIMIT_BYTES),
        name="gated_deltanet",
    )(pc, *consts)


def _mlstm_kernel(pd_ref, hp_ref, hm_ref, tri_ref, ei_ref, ef_ref, lv_ref, o_ref, c_ref, n_ref, m_ref):
    L = MLSTM_CHUNK
    n_seq, rows = pd_ref.shape[0], pd_ref.shape[1]
    f32, bf16 = jnp.float32, jnp.bfloat16
    n_lev = lv_ref.shape[0] - 3

    @pl.when(pl.program_id(1) == 0)
    def _():
        c_ref[...] = jnp.zeros(c_ref.shape, f32)
        n_ref[...] = jnp.zeros(n_ref.shape, f32)
        m_ref[...] = jnp.zeros(m_ref.shape, f32)

    hm = hm_ref[...]
    hmb = hm.astype(bf16)
    eye, incl = lv_ref[n_lev], lv_ref[n_lev + 1]
    head_of_lane = lax.broadcasted_iota(jnp.int32, (L, GROUP_WIDTH), 1) // HEAD_DIM

    def head_max(a):
        out = jnp.zeros(a.shape, f32)
        for h in range(GROUP_HEADS):
            mine = head_of_lane == h
            out = jnp.where(mine, jnp.max(jnp.where(mine, a, NEG), axis=-1, keepdims=True), out)
        return out

    per_seq = rows // L
    parts = []
    for b in range(n_seq):
        tail = pd_ref[b, :, 1024:1152]
        log_i_all = _dot_sel(tail, ei_ref[...])
        log_f = -_softplus(-(_dot_sel(tail, ef_ref[...]) + hp_ref[0:1, :]))
        b_all = _sel_dot(tri_ref[...], log_f)
        for c in range(per_seq):
            sl = slice(c * L, (c + 1) * L)
            q = pd_ref[b, sl, 0:256]
            k = pd_ref[b, sl, 256:512] * HEAD_DIM ** -0.5
            v = pd_ref[b, sl, 512:768]
            bc, log_i = b_all[sl], log_i_all[sl]
            log_w = jnp.where(incl > 0.5, bc - _diag_row(bc - log_i, eye), NEG)
            b_last = bc[L - 1:L, :]
            log_w_end = b_last - bc + log_i
            qk = _nt_dot(q.astype(bf16), _block_diag(k, hmb))
            parts.append((q, k, v, bc, log_w, head_max(log_w), b_last, log_w_end,
                          jnp.max(log_w_end, axis=0, keepdims=True), qk))
    c_state = [c_ref[b] for b in range(n_seq)]
    n_state = [n_ref[b] for b in range(n_seq)]
    m_prev = [m_ref[b] for b in range(n_seq)]
    outs = [[] for _ in range(n_seq)]
    for c in range(per_seq):
        for b in range(n_seq):
            q, k, v, bc, log_w, m_intra, b_last, log_w_end, m_end, qk = parts[b * per_seq + c]
            log_inter = bc + m_prev[b]
            m_t = jnp.maximum(log_inter, m_intra)
            w_inter = jnp.exp(log_inter - m_t)
            s = qk * jnp.exp(log_w - m_t)
            num = w_inter * jnp.dot(q.astype(bf16), c_state[b].astype(bf16), preferred_element_type=f32)
            num = num + jnp.dot(s.astype(bf16), _block_diag(v, hmb), preferred_element_type=f32)
            den = w_inter * _dot_sel(q * n_state[b], hmb, parts=2) + _dot_sel(s, hmb, parts=2)
            outs[b].append(num / jnp.maximum(jnp.abs(den), jnp.exp(-m_t)))
            m_new = jnp.maximum(b_last + m_prev[b], m_end)
            w_old = jnp.exp(b_last + m_prev[b] - m_new)
            kw = k * jnp.exp(log_w_end - m_new)
            c_state[b] = w_old * c_state[b] + _tn_dot(kw.astype(bf16), v.astype(bf16)) * hm
            n_state[b] = w_old * n_state[b] + jnp.sum(kw, axis=0, keepdims=True)
            m_prev[b] = m_new
    for b in range(n_seq):
        c_ref[b] = c_state[b]
        n_ref[b] = n_state[b]
        m_ref[b] = m_prev[b]
        h = jnp.concatenate(outs[b], axis=0)
        o_ref[b] = _head_norm(h, hmb) * hp_ref[1:2, :] * jax.nn.sigmoid(pd_ref[b, :, 768:1024])


def mlstm_pallas(pd, f_bias, norm_g):
    B, T, width = pd.shape
    L = MLSTM_CHUNK
    rows = L * min(CHUNKS_PER_STEP, T // L)
    n_seq = SEQS_PER_STEP if B % SEQS_PER_STEP == 0 else 1
    n = GROUP_HEADS * HEAD_DIM
    consts = (_per_head_rows(f_bias, norm_g),) + _chunk_consts(L, rows // L)
    return pl.pallas_call(
        _mlstm_kernel,
        grid=(B // n_seq, T // rows),
        in_specs=[pl.BlockSpec((n_seq, rows, width), lambda b, t: (b, t, 0))] + [_const_spec(a) for a in consts],
        out_specs=pl.BlockSpec((n_seq, rows, 256), lambda b, t: (b, t, 0)),
        out_shape=jax.ShapeDtypeStruct((B, T, 256), jnp.float32),
        scratch_shapes=[pltpu.VMEM((n_seq, n, n), jnp.float32), pltpu.VMEM((n_seq, 1, 256), jnp.float32),
                        pltpu.VMEM((n_seq, 1, 256), jnp.float32)],
        compiler_params=pltpu.CompilerParams(dimension_semantics=("arbitrary", "arbitrary"),
                                             vmem_limit_bytes=VMEM_LIMIT_BYTES),
        name="mlstm",
    )(pd, *consts)


def kernel(x, c, ada_w, ada_b, norm_g, ffn1_w13, ffn1_w2, ffn2_w13, ffn2_w2, w_in, b_in, q_norm_g, k_norm_g, cmp_pos, cmp_k_w1, cmp_k_w2, cmp_v_w1, cmp_v_w2, t5_table, sc_conv_w, gdn_conv_w, gdn_A_log, gdn_dt_bias, gdn_norm_g, mlstm_f_bias, mlstm_norm_g, mix_norm_g, w_out):
    B, T, D = x.shape
    depth = ada_w.shape[0]
    bf16 = jnp.bfloat16
    x2 = x.reshape(B * T, D)
    tables = nsa_bias_tables(t5_table, T)
    ovt = nsa_overlap_t(T)
    for l in range(depth):
        mod = ada_modulation(c, ada_w[l].astype(bf16), ada_b[l][None, :])
        mod = mod.reshape(B, N_SUBLAYERS, 3, 1, D)
        w13 = ffn1_w13[l].astype(bf16)
        x2 = ffn_half_step(x2, norm_g[l, 0][None, :], mod[:, 0, 0], mod[:, 0, 1], mod[:, 0, 2],
                           w13[:, :D_FF], w13[:, D_FF:], ffn1_w2[l].astype(bf16), T)
        ws, bs = split_in_weights(w_in[l], b_in[l])
        pa, pb, pc, pd = in_projection(x2, norm_g[l, 1][None, :], mod[:, 1, 0], mod[:, 1, 1], ws, bs, T)
        pa, pb, pc, pd = (a.reshape(B, T, -1) for a in (pa, pb, pc, pd))
        y_a = nsa_mixer_pallas(pa, q_norm_g[l], k_norm_g[l], cmp_pos[l], cmp_k_w1[l], cmp_k_w2[l],
                               cmp_v_w1[l], cmp_v_w2[l], tables, ovt, mix_norm_g[l, 0][None, :])
        y_b = short_conv_pallas(pb, sc_conv_w[l], mix_norm_g[l, 1][None, :])
        y_c = gated_deltanet_pallas(pc, gdn_conv_w[l], gdn_A_log[l], gdn_dt_bias[l], gdn_norm_g[l])
        y_d = mlstm_pallas(pd, mlstm_f_bias[l], mlstm_norm_g[l])
        ys = [y.reshape(B * T, GROUP_WIDTH) for y in (y_a, y_b, y_c, y_d)]
        x2 = out_projection(x2, ys, mod[:, 1, 2], w_out[l].astype(bf16), T)
        w13 = ffn2_w13[l].astype(bf16)
        x2 = ffn_half_step(x2, norm_g[l, 2][None, :], mod[:, 2, 0], mod[:, 2, 1], mod[:, 2, 2],
                           w13[:, :D_FF], w13[:, D_FF:], ffn2_w2[l].astype(bf16), T)
    return x2.reshape(B, T, D)
```

```python
import functools
import math

import jax
import jax.numpy as jnp
import numpy as np
from jax import lax
from jax.experimental import pallas as pl
from jax.experimental.pallas import tpu as pltpu

D_MODEL = 1024
HEAD_DIM = 64
GROUP_HEADS = 4
GROUP_WIDTH = GROUP_HEADS * HEAD_DIM
CMP_STRIDE = 16
CMP_BLOCK = 32
SLC_BLOCK = 64
N_SLC = 16
N_LOCAL_SLC = 2
WINDOW = 512
Q_BLOCK = 128
FORCE = 1e6
N_BUCKETS = 32
MAX_DISTANCE = 128
GDN_CHUNK = 64
MLSTM_CHUNK = 64
D_FF = 2816
N_SUBLAYERS = 3
EPS = 1e-6

IN_LAYOUT = (
    ("a_q", 256), ("a_k_cmp", 64), ("a_v_cmp", 64),
    ("a_k_slc", 64), ("a_v_slc", 64), ("a_k_win", 64), ("a_v_win", 64),
    ("a_gate", 12),
    ("b_b", 256), ("b_c", 256), ("b_x", 256),
    ("c_q", 256), ("c_k", 256), ("c_v", 256),
    ("c_beta", 4), ("c_alpha", 4), ("c_z", 256),
    ("d_q", 256), ("d_k", 256), ("d_v", 256),
    ("d_i", 4), ("d_f", 4), ("d_o", 256),
)

VMEM_LIMIT_BYTES = 56 * 1024 * 1024
FFN_TOKEN_TILE = 512
FFN_CHUNK = 256
PROJ_TOKEN_TILE = 512
NEG = -1e30
HIGHEST = lax.Precision.HIGHEST


def _modulated_norm(x, g, scale, shift):
    y = x * lax.rsqrt(jnp.mean(x * x, axis=-1, keepdims=True) + EPS)
    return (y * g) * (1.0 + scale) + shift


def _ada_kernel(c_ref, w_ref, b_ref, o_ref):
    c = c_ref[...]
    cond = c * jax.nn.sigmoid(c)
    o_ref[...] = jnp.dot(cond.astype(jnp.bfloat16), w_ref[...],
                         preferred_element_type=jnp.float32) + b_ref[...]


def ada_modulation(c, w, b):
    B, D = c.shape
    N = w.shape[1]
    tn = 1152
    return pl.pallas_call(
        _ada_kernel,
        grid=(N // tn,),
        in_specs=[pl.BlockSpec((B, D), lambda j: (0, 0)),
                  pl.BlockSpec((D, tn), lambda j: (0, j)),
                  pl.BlockSpec((1, tn), lambda j: (0, j))],
        out_specs=pl.BlockSpec((B, tn), lambda j: (0, j)),
        out_shape=jax.ShapeDtypeStruct((B, N), jnp.float32),
        name="ada_modulation",
    )(c, w, b)


def _ffn_kernel(x_ref, g_ref, shift_ref, scale_ref, gate_ref, w1_ref, w3_ref, w2_ref, o_ref, act_ref):
    x = x_ref[...]
    h = _modulated_norm(x, g_ref[...], scale_ref[0], shift_ref[0]).astype(jnp.bfloat16)
    n_chunks = act_ref.shape[1] // FFN_CHUNK
    for ci in range(n_chunks):
        cs = slice(ci * FFN_CHUNK, (ci + 1) * FFN_CHUNK)
        a = jnp.dot(h, w1_ref[:, cs], preferred_element_type=jnp.float32)
        b = jnp.dot(h, w3_ref[:, cs], preferred_element_type=jnp.float32)
        act_ref[:, cs] = (a * jax.nn.sigmoid(a) * b).astype(jnp.bfloat16)
    y = jnp.dot(act_ref[...], w2_ref[...], preferred_element_type=jnp.float32)
    o_ref[...] = x + (0.5 * gate_ref[0]) * y


def ffn_half_step(x2, g, shift, scale, gate, w13, w2, tokens_per_batch):
    M, D = x2.shape
    F = w2.shape[0]
    tm = FFN_TOKEN_TILE
    tiles_per_batch = tokens_per_batch // tm
    resident = dict(pipeline_mode=pl.Buffered(1))
    mod_spec = pl.BlockSpec((1, 1, D), lambda i: (i // tiles_per_batch, 0, 0))
    return pl.pallas_call(
        _ffn_kernel,
        grid=(M // tm,),
        in_specs=[pl.BlockSpec((tm, D), lambda i: (i, 0)),
                  pl.BlockSpec((1, D), lambda i: (0, 0)),
                  mod_spec, mod_spec, mod_spec,
                  pl.BlockSpec((D, F), lambda i: (0, 0), **resident),
                  pl.BlockSpec((D, F), lambda i: (0, 1), **resident),
                  pl.BlockSpec((F, D), lambda i: (0, 0), **resident)],
        out_specs=pl.BlockSpec((tm, D), lambda i: (i, 0)),
        out_shape=jax.ShapeDtypeStruct((M, D), jnp.float32),
        scratch_shapes=[pltpu.VMEM((tm, F), jnp.bfloat16)],
        compiler_params=pltpu.CompilerParams(dimension_semantics=("arbitrary",),
                                             vmem_limit_bytes=VMEM_LIMIT_BYTES),
        name="ffn_half_step",
    )(x2, g, shift, scale, gate, w13, w13, w2)


def _in_proj_kernel(x_ref, g_ref, shift_ref, scale_ref, wa_ref, wb_ref, wc_ref, wd_ref,
                    ba_ref, bb_ref, bc_ref, bd_ref, oa_ref, ob_ref, oc_ref, od_ref):
    h = _modulated_norm(x_ref[...], g_ref[...], scale_ref[0], shift_ref[0]).astype(jnp.bfloat16)
    for w_ref, b_ref, o_ref in ((wa_ref, ba_ref, oa_ref), (wb_ref, bb_ref, ob_ref),
                                (wc_ref, bc_ref, oc_ref), (wd_ref, bd_ref, od_ref)):
        o_ref[...] = jnp.dot(h, w_ref[...], preferred_element_type=jnp.float32) + b_ref[...]


def in_projection(x2, g, shift, scale, ws, bs, tokens_per_batch):
    M, D = x2.shape
    tm = PROJ_TOKEN_TILE
    tiles_per_batch = tokens_per_batch // tm
    mod_spec = pl.BlockSpec((1, 1, D), lambda i: (i // tiles_per_batch, 0, 0))
    const = lambda a, **kw: pl.BlockSpec(a.shape, lambda i: (0, 0), **kw)
    return pl.pallas_call(
        _in_proj_kernel,
        grid=(M // tm,),
        in_specs=[pl.BlockSpec((tm, D), lambda i: (i, 0)), const(g), mod_spec, mod_spec]
                 + [const(w, pipeline_mode=pl.Buffered(1)) for w in ws] + [const(b) for b in bs],
        out_specs=[pl.BlockSpec((tm, w.shape[1]), lambda i: (i, 0)) for w in ws],
        out_shape=[jax.ShapeDtypeStruct((M, w.shape[1]), jnp.float32) for w in ws],
        compiler_params=pltpu.CompilerParams(dimension_semantics=("arbitrary",),
                                             vmem_limit_bytes=VMEM_LIMIT_BYTES),
        name="in_projection",
    )(x2, g, shift, scale, *ws, *bs)


def _group_columns():
    offs, o = {}, 0
    for n, s in IN_LAYOUT:
        offs[n] = np.arange(o, o + s)
        o += s
    cat = lambda names: np.concatenate([offs[n] for n in names])
    return (cat(["a_q", "a_k_cmp", "a_v_cmp", "a_k_slc", "a_v_slc", "a_k_win", "a_v_win", "a_gate"]),
            cat(["b_b", "b_c", "b_x"]),
            cat(["c_q", "c_k", "c_v", "c_z", "c_beta", "c_alpha"]),
            cat(["d_q", "d_k", "d_v", "d_o", "d_i", "d_f"]))


GROUP_SLAB_WIDTH = (768, 768, 1152, 1152)


def _take_columns(a, idx, width):
    cuts = [0] + [k + 1 for k in range(idx.size - 1) if idx[k + 1] != idx[k] + 1] + [idx.size]
    runs = [a[..., int(idx[s]):int(idx[e - 1]) + 1] for s, e in zip(cuts[:-1], cuts[1:])]
    return jnp.concatenate(runs + [jnp.zeros(a.shape[:-1] + (width - idx.size,), a.dtype)], axis=-1)


def split_in_weights(w, b):
    ws, bs = [], []
    for idx, width in zip(_group_columns(), GROUP_SLAB_WIDTH):
        ws.append(_take_columns(w, idx, width).astype(jnp.bfloat16))
        bs.append(_take_columns(b, idx, width)[None, :])
    return ws, bs


def _out_proj_kernel(x_ref, ya_ref, yb_ref, yc_ref, yd_ref, gate_ref, w_ref, o_ref):
    z = None
    for g, y_ref in enumerate((ya_ref, yb_ref, yc_ref, yd_ref)):
        part = jnp.dot(y_ref[...].astype(jnp.bfloat16), w_ref[g * GROUP_WIDTH:(g + 1) * GROUP_WIDTH, :],
                       preferred_element_type=jnp.float32)
        z = part if z is None else z + part
    o_ref[...] = x_ref[...] + gate_ref[0] * z


def out_projection(x2, ys, gate, w, tokens_per_batch):
    M, D = x2.shape
    tm = PROJ_TOKEN_TILE
    tiles_per_batch = tokens_per_batch // tm
    return pl.pallas_call(
        _out_proj_kernel,
        grid=(M // tm,),
        in_specs=[pl.BlockSpec((tm, D), lambda i: (i, 0))]
                 + [pl.BlockSpec((tm, GROUP_WIDTH), lambda i: (i, 0)) for _ in ys]
                 + [pl.BlockSpec((1, 1, D), lambda i: (i // tiles_per_batch, 0, 0)),
                    pl.BlockSpec(w.shape, lambda i: (0, 0), pipeline_mode=pl.Buffered(1))],
        out_specs=pl.BlockSpec((tm, D), lambda i: (i, 0)),
        out_shape=jax.ShapeDtypeStruct((M, D), jnp.float32),
        compiler_params=pltpu.CompilerParams(dimension_semantics=("arbitrary",),
                                             vmem_limit_bytes=VMEM_LIMIT_BYTES),
        name="out_projection",
    )(x2, *ys, gate, w)


def t5_bucket(dist):
    n = jnp.maximum(dist, 0)
    max_exact = N_BUCKETS // 2
    nf = jnp.maximum(n, 1).astype(jnp.float32)
    large = max_exact + (jnp.log(nf / max_exact) / math.log(MAX_DISTANCE / max_exact)
                         * (N_BUCKETS - max_exact)).astype(jnp.int32)
    large = jnp.minimum(large, N_BUCKETS - 1)
    return jnp.where(n < max_exact, n, large)


def _nt_dot(a, b):
    return lax.dot_general(a, b, (((1,), (1,)), ((), ())), preferred_element_type=jnp.float32)


def _stack_heads(qn):
    lane = lax.broadcasted_iota(jnp.int32, (qn.shape[0], 128), 1)
    lo = lane < HEAD_DIM
    zero = jnp.zeros((), qn.dtype)
    halves = (qn[:, 0:128], qn[:, 128:256])
    return jnp.concatenate([jnp.where(lo if h % 2 == 0 else ~lo, halves[h // 2], zero)
                            for h in range(GROUP_HEADS)], axis=0)


def _nsa_prep_kernel(q_ref, kvs_ref, kvw_ref, qg_ref, kg_ref, seg_ref,
                     qn_ref, qnt_ref, kxs_ref, v1s_ref, kkw_ref, v1w_ref):
    tm = q_ref.shape[0]
    q = q_ref[...]
    ss = jnp.dot(q * q, seg_ref[...], precision=HIGHEST, preferred_element_type=jnp.float32)
    qn = q * lax.rsqrt(ss * (1.0 / HEAD_DIM) + EPS) * qg_ref[...]
    qn_ref[...] = qn.astype(qn_ref.dtype)
    qnt_ref[...] = qn.T.astype(qnt_ref.dtype)
    lane = lax.broadcasted_iota(jnp.int32, (tm, 128), 1)
    lo = lane < HEAD_DIM

    def split(x):
        xr = pltpu.roll(x, HEAD_DIM, 1)
        ss = jnp.sum(jnp.where(lo, x * x, 0.0), axis=-1, keepdims=True)
        kk = jnp.where(lo, x, xr) * lax.rsqrt(ss * (1.0 / HEAD_DIM) + EPS) * kg_ref[...]
        return kk, jnp.where(lo, xr, 1.0)

    kk, v1 = split(kvs_ref[...])
    tok = pl.program_id(1) * tm + lax.broadcasted_iota(jnp.int32, (tm, 128), 0)
    onehot = jnp.where(lane == tok // SLC_BLOCK, 1.0, 0.0)
    kxs_ref[...] = jnp.concatenate([kk, onehot], axis=1).astype(kxs_ref.dtype)
    v1s_ref[...] = v1.T.astype(v1s_ref.dtype)
    kk, v1 = split(kvw_ref[...])
    kkw_ref[...] = kk.astype(kkw_ref.dtype)
    v1w_ref[...] = v1.T.astype(v1w_ref.dtype)


def nsa_prep(pa, qg, kg2, seg):
    B, T, _ = pa.shape
    tm = 512
    bf16 = jnp.bfloat16
    col = lambda w, j: pl.BlockSpec((None, tm, w), lambda b, i: (b, i, j))
    rowblk = lambda h: pl.BlockSpec((None, h, tm), lambda b, i: (b, 0, i))
    const = lambda shape: pl.BlockSpec(shape, lambda b, i: (0, 0))
    return pl.pallas_call(
        _nsa_prep_kernel,
        grid=(B, T // tm),
        in_specs=[col(256, 0), col(128, 3), col(128, 4), const((1, 256)), const((1, 128)), const((256, 256))],
        out_specs=[col(256, 0), rowblk(256), col(256, 0), rowblk(128), col(128, 0), rowblk(128)],
        out_shape=[jax.ShapeDtypeStruct((B, T, 256), bf16), jax.ShapeDtypeStruct((B, 256, T), bf16),
                   jax.ShapeDtypeStruct((B, T, 256), bf16), jax.ShapeDtypeStruct((B, 128, T), bf16),
                   jax.ShapeDtypeStruct((B, T, 128), bf16), jax.ShapeDtypeStruct((B, 128, T), bf16)],
        name="nsa_prep",
    )(pa, pa, pa, qg, kg2, seg)


def _compress_kernel(kch_ref, vch_ref, plo_ref, phi_ref, w1k_ref, w2k_ref, w1v_ref, w2v_ref, kg_ref,
                     kkc_ref, vvc_ref):
    nrow = kch_ref.shape[0]
    half = CMP_STRIDE * HEAD_DIM
    bf16 = jnp.bfloat16
    valid = lax.broadcasted_iota(jnp.int32, (nrow, 128), 0) < nrow - 1

    def comp(ch, w1_ref, w2_ref):
        a = jnp.dot((ch + plo_ref[...]).astype(bf16), w1_ref[0:half, :], preferred_element_type=jnp.float32)
        b = jnp.dot((ch + phi_ref[...]).astype(bf16), w1_ref[half:2 * half, :],
                    preferred_element_type=jnp.float32)
        hid = a + pltpu.roll(b, nrow - 1, 0)
        hid = hid * jax.nn.sigmoid(hid)
        return jnp.dot(hid.astype(bf16), w2_ref[...], preferred_element_type=jnp.float32)

    kc = comp(kch_ref[...], w1k_ref, w2k_ref)
    kc = kc * lax.rsqrt(jnp.mean(kc * kc, axis=-1, keepdims=True) + EPS) * kg_ref[...]
    kkc_ref[...] = jnp.where(valid, kc, 0.0).astype(kkc_ref.dtype)
    vc = comp(vch_ref[...], w1v_ref, w2v_ref)
    vvc_ref[...] = jnp.where(valid, vc, 0.0).astype(vvc_ref.dtype)


def nsa_compress(kch, vch, plo, phi, w1k, w2k, w1v, w2v, kg2):
    B, nrow, width = kch.shape
    bf16 = jnp.bfloat16
    full = lambda a: pl.BlockSpec(a.shape, lambda b: (0,) * a.ndim)
    bspec = pl.BlockSpec((None, nrow, width), lambda b: (b, 0, 0))
    ospec = pl.BlockSpec((None, nrow, 128), lambda b: (b, 0, 0))
    return pl.pallas_call(
        _compress_kernel,
        grid=(B,),
        in_specs=[bspec, bspec, full(plo), full(phi), full(w1k), full(w2k), full(w1v), full(w2v), full(kg2)],
        out_specs=[ospec, ospec],
        out_shape=[jax.ShapeDtypeStruct((B, nrow, 128), bf16), jax.ShapeDtypeStruct((B, nrow, 128), bf16)],
        name="nsa_compress",
    )(kch, vch, plo, phi, w1k, w2k, w1v, w2v, kg2)


CMP_TILES_PER_STEP = 2


def _cmp_select_kernel(qn_ref, kkc_ref, vvc_ref, cb_ref, ovt_ref, ocmp_ref, pen_ref, sc_ref, *, n_sel):
    i = pl.program_id(1)
    n_tiles = cb_ref.shape[0]
    tq = qn_ref.shape[0] // n_tiles
    nblk = ovt_ref.shape[0]
    bf16 = jnp.bfloat16
    lo = lax.broadcasted_iota(jnp.int32, (tq, 128), 1) < HEAD_DIM
    blk = lax.broadcasted_iota(jnp.int32, (nblk, tq), 0)
    scores = []
    for g in range(n_tiles):
        rows = slice(g * tq, (g + 1) * tq)
        q4 = _stack_heads(qn_ref[rows, :])
        s = _nt_dot(q4, kkc_ref[...]) + cb_ref[g]
        m = jnp.max(s, axis=-1, keepdims=True)
        e = jnp.where(s > 0.5 * NEG, jnp.exp(s - m), 0.0)
        p = e * (1.0 / jnp.maximum(jnp.sum(e, axis=-1, keepdims=True), 1e-30))
        o = jnp.dot(p.astype(bf16), vvc_ref[...], preferred_element_type=jnp.float32)
        ocmp_ref[rows, 0:128] = jnp.where(lo, o[0:tq], o[tq:2 * tq])
        ocmp_ref[rows, 128:256] = jnp.where(lo, o[2 * tq:3 * tq], o[3 * tq:4 * tq])
        psum = p[0:tq] + p[tq:2 * tq] + p[2 * tq:3 * tq] + p[3 * tq:4 * tq]
        hi = psum.astype(bf16)
        lo_part = (psum - hi.astype(jnp.float32)).astype(bf16)
        score = _nt_dot(ovt_ref[...], hi) + _nt_dot(ovt_ref[...], lo_part)
        qpos = (i * n_tiles + g) * tq + lax.broadcasted_iota(jnp.int32, (nblk, tq), 1)
        cur = qpos // SLC_BLOCK
        forced = (blk == 0) | ((cur - blk >= 0) & (cur - blk < N_LOCAL_SLC))
        score = jnp.where(forced, FORCE, score)
        score = jnp.where(blk <= cur, score, -FORCE)
        sc_ref[g] = score
        scores.append(score)
    blocks_per_tile = tq // SLC_BLOCK

    def rank_step(t, cnts):
        out = []
        for g in range(n_tiles):
            cnt = cnts[g]
            for u in range(blocks_per_tile):
                jp = t * blocks_per_tile + u
                row = sc_ref[g, pl.ds(jp, 1), :]
                tie = jnp.where(blk > jp, 1.0, 0.0)
                cnt = cnt + jnp.where(row > scores[g], 1.0, jnp.where(row == scores[g], tie, 0.0))
            out.append(cnt)
        return tuple(out)

    zeros = tuple(jnp.zeros((nblk, tq), jnp.float32) for _ in range(n_tiles))
    cnts = lax.fori_loop(0, (i + 1) * n_tiles, rank_step, zeros)
    for g in range(n_tiles):
        pen = jnp.where(cnts[g] < n_sel, 0.0, NEG)
        if nblk < 128:
            pen = jnp.concatenate([pen, jnp.zeros((128 - nblk, tq), jnp.float32)], axis=0)
        pen_ref[g] = pen.astype(pen_ref.dtype)


def nsa_cmp_select(qn, kkc, vvc, cbias, ovt, n_sel):
    B, T, _ = qn.shape
    nrow = kkc.shape[1]
    nblk = ovt.shape[0]
    tq = Q_BLOCK
    g = CMP_TILES_PER_STEP
    return pl.pallas_call(
        functools.partial(_cmp_select_kernel, n_sel=n_sel),
        grid=(B, T // (g * tq)),
        in_specs=[pl.BlockSpec((None, g * tq, 256), lambda b, i: (b, i, 0)),
                  pl.BlockSpec((None, nrow, 128), lambda b, i: (b, 0, 0)),
                  pl.BlockSpec((None, nrow, 128), lambda b, i: (b, 0, 0)),
                  pl.BlockSpec((g, 4 * tq, nrow), lambda b, i: (i, 0, 0)),
                  pl.BlockSpec((nblk, nrow), lambda b, i: (0, 0))],
        out_specs=[pl.BlockSpec((None, g * tq, 256), lambda b, i: (b, i, 0)),
                   pl.BlockSpec((None, g, 128, tq), lambda b, i: (b, i, 0, 0))],
        out_shape=[jax.ShapeDtypeStruct((B, T, 256), jnp.float32),
                   jax.ShapeDtypeStruct((B, T // tq, 128, tq), jnp.bfloat16)],
        scratch_shapes=[pltpu.VMEM((g, nblk, tq), jnp.float32)],
        name="nsa_cmp_select",
    )(qn, kkc, vvc, cbias, ovt)


NSA_KEY_TILE = 256
NSA_QUERY_TILE = 256


def _nsa_attn_kernel(qnt_ref, pen_ref, ocmp_ref, gate_ref, kxs_ref, v1s_ref, kkw_ref, v1w_ref,
                     tabs_ref, tabw_ref, mixg_ref, o_ref, acc_ref, m_ref, sa_ref, sb_ref):
    i = pl.program_id(1)
    tq = o_ref.shape[0]
    kt = tabs_ref.shape[1]
    f32, bf16 = jnp.float32, jnp.bfloat16
    qt = qnt_ref[...]
    lo = lax.broadcasted_iota(jnp.int32, (128, tq), 0) < HEAD_DIM
    zero = jnp.zeros((), qt.dtype)
    q4t = jnp.concatenate([jnp.where(lo if h % 2 == 0 else ~lo, qt[128 * (h // 2):128 * (h // 2) + 128], zero)
                           for h in range(GROUP_HEADS)], axis=1)
    pen = jnp.concatenate([pen_ref[t] for t in range(pen_ref.shape[0])], axis=1)
    rhs_sel = jnp.concatenate([q4t, jnp.concatenate([pen] * GROUP_HEADS, axis=1)], axis=0)

    def branch(rhs, k_ref, v_ref, tab_ref, first_tile, last_tile):
        n_tab = tab_ref.shape[0] - 1
        m_ref[...] = jnp.full(m_ref.shape, NEG, f32)
        acc_ref[...] = jnp.zeros(acc_ref.shape, f32)

        def key_offset(j):
            return pl.multiple_of(jnp.minimum(j, last_tile) * kt, kt)

        def scores(j):
            delta = jnp.where(j > last_tile, n_tab, jnp.minimum(i - j * (kt // tq), n_tab - 1))
            return jnp.dot(k_ref[pl.ds(key_offset(j), kt), :], rhs, preferred_element_type=f32) + tab_ref[delta]

        def absorb(s, j):
            m_old = m_ref[...]
            m_new = jnp.maximum(m_old, jnp.max(s, axis=0, keepdims=True))
            p = jnp.exp(s - m_new)
            pv = jnp.dot(v_ref[:, pl.ds(key_offset(j), kt)], p.astype(bf16), preferred_element_type=f32)
            acc_ref[...] = jnp.exp(m_old - m_new) * acc_ref[...] + pv
            m_ref[...] = m_new

        sa_ref[...] = scores(first_tile)

        def body(t, carry):
            j = first_tile + 2 * t
            sb_ref[...] = scores(j + 1)
            absorb(sa_ref[...], j)
            sa_ref[...] = scores(j + 2)
            absorb(sb_ref[...], j + 1)
            return carry

        lax.fori_loop(0, (last_tile - first_tile) // 2 + 1, body, 0)
        acc = acc_ref[...]
        return acc[0:HEAD_DIM] / jnp.maximum(acc[HEAD_DIM:2 * HEAD_DIM], 1e-30)

    def window_branch(rhs):
        n_tab = tabw_ref.shape[0] - 1
        m = jnp.full((1, rhs.shape[1]), NEG, f32)
        acc = jnp.zeros((2 * HEAD_DIM, rhs.shape[1]), f32)
        for d in range(n_tab):
            j = i - d
            off = pl.multiple_of(jnp.maximum(j, 0) * kt, kt)
            s = jnp.dot(kkw_ref[pl.ds(off, kt), :], rhs, preferred_element_type=f32)
            s = s + tabw_ref[jnp.where(j < 0, n_tab, d)]
            m_new = jnp.maximum(m, jnp.max(s, axis=0, keepdims=True))
            p = jnp.exp(s - m_new)
            acc = jnp.exp(m - m_new) * acc + jnp.dot(v1w_ref[:, pl.ds(off, kt)], p.astype(bf16),
                                                     preferred_element_type=f32)
            m = m_new
        return acc[0:HEAD_DIM] / jnp.maximum(acc[HEAD_DIM:2 * HEAD_DIM], 1e-30)

    last = (i * tq) // kt
    if kt == tq:
        o_w = window_branch(q4t)
    else:
        o_w = branch(q4t, kkw_ref, v1w_ref, tabw_ref, (jnp.maximum(i - WINDOW // tq, 0) * tq) // kt, last)
    o_s = branch(rhs_sel, kxs_ref, v1s_ref, tabs_ref, 0, last)
    gt = jax.nn.sigmoid(gate_ref[...]).T
    oct_ = ocmp_ref[...].T
    parts = []
    for h in range(GROUP_HEADS):
        cols = slice(h * tq, (h + 1) * tq)
        parts.append(gt[3 * h:3 * h + 1] * oct_[h * HEAD_DIM:(h + 1) * HEAD_DIM]
                     + gt[3 * h + 1:3 * h + 2] * o_s[:, cols] + gt[3 * h + 2:3 * h + 3] * o_w[:, cols])
    yt = jnp.concatenate(parts, axis=0)
    yt = yt * lax.rsqrt(jnp.mean(yt * yt, axis=0, keepdims=True) + EPS) * mixg_ref[...]
    o_ref[...] = yt.T


def nsa_attention(qnt, pen, ocmp, pa, kxs, v1s, kkw, v1w, tab_s, tab_w, mixg):
    B, _, T = qnt.shape
    tq = min(NSA_QUERY_TILE, T)
    tile = lambda w, j: pl.BlockSpec((None, tq, w), lambda b, i: (b, i, j))
    whole = lambda w: pl.BlockSpec((None, T, w), lambda b, i: (b, 0, 0))
    whole_t = pl.BlockSpec((None, 128, T), lambda b, i: (b, 0, 0))
    full = lambda a: pl.BlockSpec(a.shape, lambda b, i: (0,) * a.ndim)
    return pl.pallas_call(
        _nsa_attn_kernel,
        grid=(B, T // tq),
        in_specs=[pl.BlockSpec((None, 256, tq), lambda b, i: (b, 0, i)),
                  pl.BlockSpec((None, tq // Q_BLOCK, 128, Q_BLOCK), lambda b, i: (b, i, 0, 0)),
                  tile(256, 0), tile(128, 5),
                  whole(256), whole_t, whole(128), whole_t, full(tab_s), full(tab_w), full(mixg)],
        out_specs=tile(256, 0),
        out_shape=jax.ShapeDtypeStruct((B, T, 256), jnp.float32),
        scratch_shapes=[pltpu.VMEM((128, GROUP_HEADS * tq), jnp.float32),
                        pltpu.VMEM((1, GROUP_HEADS * tq), jnp.float32),
                        pltpu.VMEM((tab_s.shape[1], GROUP_HEADS * tq), jnp.float32),
                        pltpu.VMEM((tab_s.shape[1], GROUP_HEADS * tq), jnp.float32)],
        compiler_params=pltpu.CompilerParams(dimension_semantics=("arbitrary", "arbitrary"),
                                             vmem_limit_bytes=VMEM_LIMIT_BYTES),
        name="nsa_attention",
    )(qnt, pen, ocmp, pa, kxs, v1s, kkw, v1w, tab_s, tab_w, mixg)


def _toeplitz(w, rows, cols):
    H, n = w.shape
    period = n + 1
    flat = jnp.tile(jnp.pad(w, ((0, 0), (0, 1))), (1, rows + 1))[:, :rows * (period + 1)]
    return jnp.flip(flat.reshape(H, rows, period + 1)[:, :, :cols], axis=2)


def nsa_bias_tables(t5_table, T):
    H = GROUP_HEADS
    tq = Q_BLOCK
    ta = min(NSA_QUERY_TILE, T)
    kt = min(NSA_KEY_TILE, T)
    nrow = T // CMP_STRIDE
    dmax = WINDOW + 2 * ta + kt
    onehot = (t5_bucket(jnp.arange(dmax))[:, None] == jnp.arange(N_BUCKETS)[None, :]).astype(jnp.float32)
    by_dist = jnp.dot(onehot, t5_table, precision=HIGHEST).T
    neg = lambda n: jnp.full((H, n), NEG, jnp.float32)
    f_s = jnp.concatenate([neg(kt - 1), by_dist], axis=1)
    f_w = jnp.concatenate([neg(kt - 1), by_dist[:, :WINDOW], neg(dmax - WINDOW)], axis=1)

    def tiles(f, n):
        out = [_toeplitz(f[:, k * ta:k * ta + ta + kt - 1], ta, kt).reshape(H * ta, kt).T for k in range(n)]
        return jnp.stack(out + [jnp.full((kt, H * ta), NEG, jnp.float32)])

    n_s = -(-(kt + MAX_DISTANCE) // ta) + 1
    n_w = WINDOW // ta + kt // ta
    tab_s, tab_w = tiles(f_s, n_s), tiles(f_w, n_w)
    per_tile = tq // CMP_STRIDE
    d0 = -CMP_STRIDE * (nrow - 1) - (CMP_BLOCK - 1)
    n16 = 2 * nrow + per_tile - 1
    n_pos = CMP_STRIDE * n16 + d0
    far = jnp.broadcast_to(by_dist[:, -1:], (H, max(n_pos - dmax, 0)))
    w16 = jnp.concatenate([neg(-d0), by_dist[:, :n_pos], far], axis=1).reshape(H, n16, CMP_STRIDE)
    base = jnp.stack([w16[:, a:a + 2 * nrow, :] for a in range(per_tile)], axis=1)
    base = jnp.flip(base.transpose(0, 1, 3, 2).reshape(H * tq, 2 * nrow), axis=1)
    cbias = jnp.stack([base[:, nrow - per_tile * i: 2 * nrow - per_tile * i] for i in range(T // tq)])
    return tab_s, tab_w, cbias


def nsa_overlap_t(T):
    nrow = T // CMP_STRIDE
    nblk = T // SLC_BLOCK
    ci = np.arange(nrow)[None, :]
    bj = np.arange(nblk)[:, None]
    ov = (ci * CMP_STRIDE < (bj + 1) * SLC_BLOCK) & (ci * CMP_STRIDE + CMP_BLOCK > bj * SLC_BLOCK)
    ov = ov & (ci < nrow - 1)
    return jnp.asarray(ov.astype(np.float32), dtype=jnp.bfloat16)


def nsa_mixer_pallas(pa, q_norm_g, k_norm_g, cmp_pos, cmp_k_w1, cmp_k_w2, cmp_v_w1, cmp_v_w2,
                     tables, ovt, mixg):
    B, T, _ = pa.shape
    bf16 = jnp.bfloat16
    tab_s, tab_w, cbias = tables
    qg = (jnp.tile(q_norm_g, GROUP_HEADS) * HEAD_DIM ** -0.5)[None, :]
    kg2 = jnp.tile(k_norm_g, 2)[None, :]
    qn, qnt, kxs, v1s, kkw, v1w = nsa_prep(pa, qg, kg2, _head_mask())
    nrow = T // CMP_STRIDE
    kch = pa[:, :, 256:320].reshape(B, nrow, CMP_STRIDE * HEAD_DIM)
    vch = pa[:, :, 320:384].reshape(B, nrow, CMP_STRIDE * HEAD_DIM)
    plo = cmp_pos[:CMP_STRIDE].reshape(1, -1)
    phi = cmp_pos[CMP_STRIDE:].reshape(1, -1)
    dup = lambda w: jnp.concatenate([w, w], axis=1).astype(bf16)
    kkc, vvc = nsa_compress(kch, vch, plo, phi, cmp_k_w1.astype(bf16), dup(cmp_k_w2),
                            cmp_v_w1.astype(bf16), dup(cmp_v_w2), kg2)
    n_sel = min(N_SLC, T // SLC_BLOCK)
    ocmp, pen = nsa_cmp_select(qn, kkc, vvc, cbias, ovt, n_sel)
    mixg_col = jnp.broadcast_to(mixg.reshape(-1, 1), (GROUP_WIDTH, min(NSA_QUERY_TILE, T)))
    return nsa_attention(qnt, pen, ocmp, pa, kxs, v1s, kkw, v1w, tab_s, tab_w, mixg_col)


CONV_CARRY_ROWS = 8


def _shifted(cat, shift, rows):
    return pltpu.roll(cat, shift, 0)[CONV_CARRY_ROWS:CONV_CARRY_ROWS + rows]


def _sconv_kernel(pb_ref, w_ref, g_ref, o_ref, carry_ref):
    rows = pb_ref.shape[0]

    @pl.when(pl.program_id(1) == 0)
    def _():
        carry_ref[...] = jnp.zeros(carry_ref.shape, jnp.float32)

    u = pb_ref[:, 256:512] * pb_ref[:, 512:768]
    cat = jnp.concatenate([carry_ref[...], u], axis=0)
    conv = u * w_ref[2:3, :] + _shifted(cat, 1, rows) * w_ref[1:2, :] + _shifted(cat, 2, rows) * w_ref[0:1, :]
    carry_ref[...] = u[rows - CONV_CARRY_ROWS:rows]
    y = pb_ref[:, 0:256] * conv
    o_ref[...] = y * lax.rsqrt(jnp.mean(y * y, axis=-1, keepdims=True) + EPS) * g_ref[...]


def short_conv_pallas(pb, conv_w, mixg):
    B, T, _ = pb.shape
    rows = 512
    w = jnp.pad(conv_w, ((0, 8 - conv_w.shape[0]), (0, 0)))
    return pl.pallas_call(
        _sconv_kernel,
        grid=(B, T // rows),
        in_specs=[pl.BlockSpec((None, rows, 768), lambda b, t: (b, t, 0)),
                  pl.BlockSpec((8, 256), lambda b, t: (0, 0)),
                  pl.BlockSpec((1, 256), lambda b, t: (0, 0))],
        out_specs=pl.BlockSpec((None, rows, 256), lambda b, t: (b, t, 0)),
        out_shape=jax.ShapeDtypeStruct((B, T, 256), jnp.float32),
        scratch_shapes=[pltpu.VMEM((CONV_CARRY_ROWS, 256), jnp.float32)],
        compiler_params=pltpu.CompilerParams(dimension_semantics=("arbitrary", "arbitrary")),
        name="short_conv",
    )(pb, w, mixg)


CHUNKS_PER_STEP = 4
SEQS_PER_STEP = 2


def _tn_dot(a, b):
    return lax.dot_general(a, b, (((0,), (0,)), ((), ())), preferred_element_type=jnp.float32)


def _split_bf16(a, parts):
    out, rest = [], a
    for _ in range(parts):
        piece = rest.astype(jnp.bfloat16)
        out.append(piece)
        rest = rest - piece.astype(jnp.float32)
    return out


def _dot_sel(a, sel, parts=3):
    return sum(jnp.dot(p, sel, preferred_element_type=jnp.float32) for p in _split_bf16(a, parts))


def _sel_dot(sel, a, parts=3):
    return sum(jnp.dot(sel, p, preferred_element_type=jnp.float32) for p in _split_bf16(a, parts))


def _softplus(x):
    return jnp.maximum(x, 0.0) + jnp.log(1.0 + jnp.exp(-jnp.abs(x)))


def _block_diag(a, hmb):
    return jnp.concatenate([a.astype(jnp.bfloat16)] * GROUP_HEADS, axis=0) * hmb


def _diag_row(a, eye):
    return jnp.sum(a * eye, axis=0, keepdims=True)


def _head_norm(o, hmb):
    return o * lax.rsqrt(_dot_sel(o * o, hmb, parts=2) * (1.0 / HEAD_DIM) + EPS)


def _head_l2(o, hmb):
    return o * lax.rsqrt(_dot_sel(o * o, hmb, parts=2) + EPS)


def _xform_masks(L):
    i = np.arange(L)[:, None]
    j = np.arange(L)[None, :]
    masks = []
    s = 1
    while s < L:
        masks.append((i // (2 * s) == j // (2 * s)) & (i % (2 * s) >= s) & (j % (2 * s) < s))
        s *= 2
    masks += [i == j, j <= i, j < i]
    return jnp.asarray(np.tile(np.stack(masks).astype(np.float32), (1, 1, GROUP_HEADS)))


def _expand_matrix(first_lane):
    e = np.zeros((128, GROUP_WIDTH), np.float32)
    for h in range(GROUP_HEADS):
        e[first_lane + h, h * HEAD_DIM:(h + 1) * HEAD_DIM] = 1.0
    return jnp.asarray(e, jnp.bfloat16)


def _head_mask():
    return jnp.asarray(np.kron(np.eye(GROUP_HEADS), np.ones((HEAD_DIM, HEAD_DIM))), jnp.float32)


def _per_head_rows(*vecs):
    rows = [jnp.repeat(v, HEAD_DIM) if v.shape[0] == GROUP_HEADS else jnp.tile(v, GROUP_HEADS) for v in vecs]
    return jnp.pad(jnp.stack(rows), ((0, 8 - len(rows)), (0, 0)))


def _chunk_consts(L, n_chunks):
    tri = np.kron(np.eye(n_chunks), np.tril(np.ones((L, L)))).astype(np.float32)
    return (_head_mask(), jnp.asarray(tri, jnp.bfloat16), _expand_matrix(0), _expand_matrix(GROUP_HEADS),
            _xform_masks(L))


def _const_spec(a):
    return pl.BlockSpec(a.shape, lambda b, t: (0,) * a.ndim)


def _gdn_kernel(pc_ref, cw_ref, hp_ref, hm_ref, tri_ref, eb_ref, ea_ref, lv_ref, o_ref, s_ref, carry_ref):
    L = GDN_CHUNK
    n_seq, rows = pc_ref.shape[0], pc_ref.shape[1]
    f32, bf16 = jnp.float32, jnp.bfloat16
    n_lev = lv_ref.shape[0] - 3

    @pl.when(pl.program_id(1) == 0)
    def _():
        s_ref[...] = jnp.zeros(s_ref.shape, f32)
        carry_ref[...] = jnp.zeros(carry_ref.shape, f32)

    hm = hm_ref[...]
    hmb = hm.astype(bf16)
    eye, incl, strict = lv_ref[n_lev], lv_ref[n_lev + 1], lv_ref[n_lev + 2]

    q_c, k_c, v_c, beta_c, gc_c = [], [], [], [], []
    for b in range(n_seq):
        x = pc_ref[b, :, 0:768]
        cat = jnp.concatenate([carry_ref[b], x], axis=0)
        conv = x * cw_ref[3:4, :]
        for s in (1, 2, 3):
            conv = conv + _shifted(cat, s, rows) * cw_ref[3 - s:4 - s, :]
        carry_ref[b] = x[rows - CONV_CARRY_ROWS:rows]
        qkv = conv * jax.nn.sigmoid(conv)
        q_all = _head_l2(qkv[:, 0:256], hmb) * HEAD_DIM ** -0.5
        k_all = _head_l2(qkv[:, 256:512], hmb)
        tail = pc_ref[b, :, 1024:1152]
        beta_all = jax.nn.sigmoid(_dot_sel(tail, eb_ref[...]))
        g = hp_ref[0:1, :] * _softplus(_dot_sel(tail, ea_ref[...]) + hp_ref[1:2, :])
        gc_all = _sel_dot(tri_ref[...], g)
        for c in range(rows // L):
            sl = slice(c * L, (c + 1) * L)
            q_c.append(q_all[sl])
            k_c.append(k_all[sl])
            v_c.append(qkv[sl, 512:768])
            beta_c.append(beta_all[sl])
            gc_c.append(gc_all[sl])
    chains = range(len(q_c))
    egc = [jnp.exp(gc) for gc in gc_c]
    kb = [k_c[i] * beta_c[i] for i in chains]
    a_mat, qk = [], []
    for i in chains:
        seg = jnp.exp(jnp.where(incl > 0.5, gc_c[i] - _diag_row(gc_c[i], eye), NEG))
        k_bd = _block_diag(k_c[i], hmb)
        a_mat.append(_nt_dot(kb[i].astype(bf16), k_bd) * seg * strict)
        qk.append((_nt_dot(q_c[i].astype(bf16), k_bd) * seg).astype(bf16))
    t_inv = [eye - a * lv_ref[0] for a in a_mat]
    for lev in range(1, n_lev):
        te = [jnp.dot(t_inv[i].astype(bf16), _block_diag(a_mat[i] * lv_ref[lev], hmb),
                      preferred_element_type=f32) for i in chains]
        t_inv = [t_inv[i] - jnp.dot(te[i].astype(bf16), _block_diag(t_inv[i], hmb), preferred_element_type=f32)
                 for i in chains]
    parts = []
    for i in chains:
        tb = t_inv[i].astype(bf16)
        u = jnp.dot(tb, _block_diag(v_c[i] * beta_c[i], hmb), preferred_element_type=f32)
        w = jnp.dot(tb, _block_diag(kb[i] * egc[i], hmb), preferred_element_type=f32)
        k_dec = k_c[i] * jnp.exp(gc_c[i][L - 1:L, :] - gc_c[i])
        parts.append((u, w.astype(bf16), (q_c[i] * egc[i]).astype(bf16), qk[i], k_dec.astype(bf16),
                      egc[i][L - 1:L, :]))
    per_seq = rows // L
    state = [s_ref[b] for b in range(n_seq)]
    outs = [[] for _ in range(n_seq)]
    for c in range(per_seq):
        for b in range(n_seq):
            u, w, q_dec, qk_c, k_dec, g_tot = parts[b * per_seq + c]
            sb = state[b].astype(bf16)
            v_new = u - jnp.dot(w, sb, preferred_element_type=f32)
            outs[b].append(jnp.dot(q_dec, sb, preferred_element_type=f32)
                           + jnp.dot(qk_c, _block_diag(v_new, hmb), preferred_element_type=f32))
            state[b] = state[b] * g_tot + _tn_dot(k_dec, v_new.astype(bf16)) * hm
    for b in range(n_seq):
        s_ref[b] = state[b]
        o = jnp.concatenate(outs[b], axis=0)
        z = pc_ref[b, :, 768:1024]
        o_ref[b] = _head_norm(o, hmb) * hp_ref[2:3, :] * (z * jax.nn.sigmoid(z))


def gated_deltanet_pallas(pc, conv_w, A_log, dt_bias, norm_g):
    B, T, width = pc.shape
    L = GDN_CHUNK
    rows = L * min(CHUNKS_PER_STEP, T // L)
    n_seq = SEQS_PER_STEP if B % SEQS_PER_STEP == 0 else 1
    n = GROUP_HEADS * HEAD_DIM
    hp = _per_head_rows(-jnp.exp(A_log), dt_bias, norm_g)
    cw = jnp.pad(conv_w, ((0, 8 - conv_w.shape[0]), (0, 0)))
    consts = (cw, hp) + _chunk_consts(L, rows // L)
    return pl.pallas_call(
        _gdn_kernel,
        grid=(B // n_seq, T // rows),
        in_specs=[pl.BlockSpec((n_seq, rows, width), lambda b, t: (b, t, 0))] + [_const_spec(a) for a in consts],
        out_specs=pl.BlockSpec((n_seq, rows, 256), lambda b, t: (b, t, 0)),
        out_shape=jax.ShapeDtypeStruct((B, T, 256), jnp.float32),
        scratch_shapes=[pltpu.VMEM((n_seq, n, n), jnp.float32),
                        pltpu.VMEM((n_seq, CONV_CARRY_ROWS, 768), jnp.float32)],
        compiler_params=pltpu.CompilerParams(dimension_semantics=("arbitrary", "arbitrary"),
                                             vmem_limit_bytes=VMEM_LIMIT_BYTES),
        name="gated_deltanet",
    )(pc, *consts)


def _mlstm_kernel(pd_ref, hp_ref, hm_ref, tri_ref, ei_ref, ef_ref, lv_ref, o_ref, c_ref, n_ref, m_ref):
    L = MLSTM_CHUNK
    n_seq, rows = pd_ref.shape[0], pd_ref.shape[1]
    f32, bf16 = jnp.float32, jnp.bfloat16
    n_lev = lv_ref.shape[0] - 3

    @pl.when(pl.program_id(1) == 0)
    def _():
        c_ref[...] = jnp.zeros(c_ref.shape, f32)
        n_ref[...] = jnp.zeros(n_ref.shape, f32)
        m_ref[...] = jnp.zeros(m_ref.shape, f32)

    hm = hm_ref[...]
    hmb = hm.astype(bf16)
    eye, incl = lv_ref[n_lev], lv_ref[n_lev + 1]
    head_of_lane = lax.broadcasted_iota(jnp.int32, (L, GROUP_WIDTH), 1) // HEAD_DIM

    def head_max(a):
        out = jnp.zeros(a.shape, f32)
        for h in range(GROUP_HEADS):
            mine = head_of_lane == h
            out = jnp.where(mine, jnp.max(jnp.where(mine, a, NEG), axis=-1, keepdims=True), out)
        return out

    per_seq = rows // L
    parts = []
    for b in range(n_seq):
        tail = pd_ref[b, :, 1024:1152]
        log_i_all = _dot_sel(tail, ei_ref[...])
        log_f = -_softplus(-(_dot_sel(tail, ef_ref[...]) + hp_ref[0:1, :]))
        b_all = _sel_dot(tri_ref[...], log_f)
        for c in range(per_seq):
            sl = slice(c * L, (c + 1) * L)
            q = pd_ref[b, sl, 0:256]
            k = pd_ref[b, sl, 256:512] * HEAD_DIM ** -0.5
            v = pd_ref[b, sl, 512:768]
            bc, log_i = b_all[sl], log_i_all[sl]
            log_w = jnp.where(incl > 0.5, bc - _diag_row(bc - log_i, eye), NEG)
            b_last = bc[L - 1:L, :]
            log_w_end = b_last - bc + log_i
            qk = _nt_dot(q.astype(bf16), _block_diag(k, hmb))
            parts.append((q, k, v, bc, log_w, head_max(log_w), b_last, log_w_end,
                          jnp.max(log_w_end, axis=0, keepdims=True), qk))
    c_state = [c_ref[b] for b in range(n_seq)]
    n_state = [n_ref[b] for b in range(n_seq)]
    m_prev = [m_ref[b] for b in range(n_seq)]
    outs = [[] for _ in range(n_seq)]
    for c in range(per_seq):
        for b in range(n_seq):
            q, k, v, bc, log_w, m_intra, b_last, log_w_end, m_end, qk = parts[b * per_seq + c]
            log_inter = bc + m_prev[b]
            m_t = jnp.maximum(log_inter, m_intra)
            w_inter = jnp.exp(log_inter - m_t)
            s = qk * jnp.exp(log_w - m_t)
            num = w_inter * jnp.dot(q.astype(bf16), c_state[b].astype(bf16), preferred_element_type=f32)
            num = num + jnp.dot(s.astype(bf16), _block_diag(v, hmb), preferred_element_type=f32)
            den = w_inter * _dot_sel(q * n_state[b], hmb, parts=2) + _dot_sel(s, hmb, parts=2)
            outs[b].append(num / jnp.maximum(jnp.abs(den), jnp.exp(-m_t)))
            m_new = jnp.maximum(b_last + m_prev[b], m_end)
            w_old = jnp.exp(b_last + m_prev[b] - m_new)
            kw = k * jnp.exp(log_w_end - m_new)
            c_state[b] = w_old * c_state[b] + _tn_dot(kw.astype(bf16), v.astype(bf16)) * hm
            n_state[b] = w_old * n_state[b] + jnp.sum(kw, axis=0, keepdims=True)
            m_prev[b] = m_new
    for b in range(n_seq):
        c_ref[b] = c_state[b]
        n_ref[b] = n_state[b]
        m_ref[b] = m_prev[b]
        h = jnp.concatenate(outs[b], axis=0)
        o_ref[b] = _head_norm(h, hmb) * hp_ref[1:2, :] * jax.nn.sigmoid(pd_ref[b, :, 768:1024])


def mlstm_pallas(pd, f_bias, norm_g):
    B, T, width = pd.shape
    L = MLSTM_CHUNK
    rows = L * min(CHUNKS_PER_STEP, T // L)
    n_seq = SEQS_PER_STEP if B % SEQS_PER_STEP == 0 else 1
    n = GROUP_HEADS * HEAD_DIM
    consts = (_per_head_rows(f_bias, norm_g),) + _chunk_consts(L, rows // L)
    return pl.pallas_call(
        _mlstm_kernel,
        grid=(B // n_seq, T // rows),
        in_specs=[pl.BlockSpec((n_seq, rows, width), lambda b, t: (b, t, 0))] + [_const_spec(a) for a in consts],
        out_specs=pl.BlockSpec((n_seq, rows, 256), lambda b, t: (b, t, 0)),
        out_shape=jax.ShapeDtypeStruct((B, T, 256), jnp.float32),
        scratch_shapes=[pltpu.VMEM((n_seq, n, n), jnp.float32), pltpu.VMEM((n_seq, 1, 256), jnp.float32),
                        pltpu.VMEM((n_seq, 1, 256), jnp.float32)],
        compiler_params=pltpu.CompilerParams(dimension_semantics=("arbitrary", "arbitrary"),
                                             vmem_limit_bytes=VMEM_LIMIT_BYTES),
        name="mlstm",
    )(pd, *consts)


def kernel(x, c, ada_w, ada_b, norm_g, ffn1_w13, ffn1_w2, ffn2_w13, ffn2_w2, w_in, b_in, q_norm_g, k_norm_g, cmp_pos, cmp_k_w1, cmp_k_w2, cmp_v_w1, cmp_v_w2, t5_table, sc_conv_w, gdn_conv_w, gdn_A_log, gdn_dt_bias, gdn_norm_g, mlstm_f_bias, mlstm_norm_g, mix_norm_g, w_out):
    B, T, D = x.shape
    depth = ada_w.shape[0]
    bf16 = jnp.bfloat16
    x2 = x.reshape(B * T, D)
    tables = nsa_bias_tables(t5_table, T)
    ovt = nsa_overlap_t(T)
    for l in range(depth):
        mod = ada_modulation(c, ada_w[l].astype(bf16), ada_b[l][None, :])
        mod = mod.reshape(B, N_SUBLAYERS, 3, 1, D)
        x2 = ffn_half_step(x2, norm_g[l, 0][None, :], mod[:, 0, 0], mod[:, 0, 1], mod[:, 0, 2],
                           ffn1_w13[l].astype(bf16), ffn1_w2[l].astype(bf16), T)
        ws, bs = split_in_weights(w_in[l], b_in[l])
        pa, pb, pc, pd = in_projection(x2, norm_g[l, 1][None, :], mod[:, 1, 0], mod[:, 1, 1], ws, bs, T)
        pa, pb, pc, pd = (a.reshape(B, T, -1) for a in (pa, pb, pc, pd))
        y_a = nsa_mixer_pallas(pa, q_norm_g[l], k_norm_g[l], cmp_pos[l], cmp_k_w1[l], cmp_k_w2[l],
                               cmp_v_w1[l], cmp_v_w2[l], tables, ovt, mix_norm_g[l, 0][None, :])
        y_b = short_conv_pallas(pb, sc_conv_w[l], mix_norm_g[l, 1][None, :])
        y_c = gated_deltanet_pallas(pc, gdn_conv_w[l], gdn_A_log[l], gdn_dt_bias[l], gdn_norm_g[l])
        y_d = mlstm_pallas(pd, mlstm_f_bias[l], mlstm_norm_g[l])
        ys = [y.reshape(B * T, GROUP_WIDTH) for y in (y_a, y_b, y_c, y_d)]
        x2 = out_projection(x2, ys, mod[:, 1, 2], w_out[l].astype(bf16), T)
        x2 = ffn_half_step(x2, norm_g[l, 2][None, :], mod[:, 2, 0], mod[:, 2, 1], mod[:, 2, 2],
                           ffn2_w13[l].astype(bf16), ffn2_w2[l].astype(bf16), T)
    return x2.reshape(B, T, D)
```

```python
import functools
import math

import jax
import jax.numpy as jnp
import numpy as np
from jax import lax
from jax.experimental import pallas as pl
from jax.experimental.pallas import tpu as pltpu

D_MODEL = 1024
HEAD_DIM = 64
GROUP_HEADS = 4
GROUP_WIDTH = GROUP_HEADS * HEAD_DIM
CMP_STRIDE = 16
CMP_BLOCK = 32
SLC_BLOCK = 64
N_SLC = 16
N_LOCAL_SLC = 2
WINDOW = 512
Q_BLOCK = 128
FORCE = 1e6
N_BUCKETS = 32
MAX_DISTANCE = 128
GDN_CHUNK = 64
MLSTM_CHUNK = 64
D_FF = 2816
N_SUBLAYERS = 3
EPS = 1e-6

IN_LAYOUT = (
    ("a_q", 256), ("a_k_cmp", 64), ("a_v_cmp", 64),
    ("a_k_slc", 64), ("a_v_slc", 64), ("a_k_win", 64), ("a_v_win", 64),
    ("a_gate", 12),
    ("b_b", 256), ("b_c", 256), ("b_x", 256),
    ("c_q", 256), ("c_k", 256), ("c_v", 256),
    ("c_beta", 4), ("c_alpha", 4), ("c_z", 256),
    ("d_q", 256), ("d_k", 256), ("d_v", 256),
    ("d_i", 4), ("d_f", 4), ("d_o", 256),
)

VMEM_LIMIT_BYTES = 56 * 1024 * 1024
FFN_TOKEN_TILE = 512
FFN_CHUNK = 256
PROJ_TOKEN_TILE = 512
NEG = -1e30
HIGHEST = lax.Precision.HIGHEST
LOG2E = math.log2(math.e)
V1_ROWS = HEAD_DIM + 16


def _modulated_norm(x, g, scale, shift):
    y = x * lax.rsqrt(jnp.mean(x * x, axis=-1, keepdims=True) + EPS)
    return (y * g) * (1.0 + scale) + shift


def _ada_kernel(c_ref, w_ref, b_ref, o_ref):
    c = c_ref[...]
    cond = c * jax.nn.sigmoid(c)
    o_ref[...] = jnp.dot(cond.astype(jnp.bfloat16), w_ref[...],
                         preferred_element_type=jnp.float32) + b_ref[...]


def ada_modulation(c, w, b):
    B, D = c.shape
    N = w.shape[1]
    tn = 1152
    return pl.pallas_call(
        _ada_kernel,
        grid=(N // tn,),
        in_specs=[pl.BlockSpec((B, D), lambda j: (0, 0)),
                  pl.BlockSpec((D, tn), lambda j: (0, j)),
                  pl.BlockSpec((1, tn), lambda j: (0, j))],
        out_specs=pl.BlockSpec((B, tn), lambda j: (0, j)),
        out_shape=jax.ShapeDtypeStruct((B, N), jnp.float32),
        name="ada_modulation",
    )(c, w, b)


def _ffn_kernel(x_ref, g_ref, shift_ref, scale_ref, gate_ref, w1_ref, w3_ref, w2_ref, o_ref, act_ref):
    x = x_ref[...]
    h = _modulated_norm(x, g_ref[...], scale_ref[0], shift_ref[0]).astype(jnp.bfloat16)
    n_chunks = act_ref.shape[1] // FFN_CHUNK
    for ci in range(n_chunks):
        cs = slice(ci * FFN_CHUNK, (ci + 1) * FFN_CHUNK)
        a = jnp.dot(h, w1_ref[:, cs], preferred_element_type=jnp.float32)
        b = jnp.dot(h, w3_ref[:, cs], preferred_element_type=jnp.float32)
        act_ref[:, cs] = (a * jax.nn.sigmoid(a) * b).astype(jnp.bfloat16)
    y = jnp.dot(act_ref[...], w2_ref[...], preferred_element_type=jnp.float32)
    o_ref[...] = x + (0.5 * gate_ref[0]) * y


def ffn_half_step(x2, g, shift, scale, gate, w13, w2, tokens_per_batch):
    M, D = x2.shape
    F = w2.shape[0]
    tm = FFN_TOKEN_TILE
    tiles_per_batch = tokens_per_batch // tm
    resident = dict(pipeline_mode=pl.Buffered(1))
    mod_spec = pl.BlockSpec((1, 1, D), lambda i: (i // tiles_per_batch, 0, 0))
    return pl.pallas_call(
        _ffn_kernel,
        grid=(M // tm,),
        in_specs=[pl.BlockSpec((tm, D), lambda i: (i, 0)),
                  pl.BlockSpec((1, D), lambda i: (0, 0)),
                  mod_spec, mod_spec, mod_spec,
                  pl.BlockSpec((D, F), lambda i: (0, 0), **resident),
                  pl.BlockSpec((D, F), lambda i: (0, 1), **resident),
                  pl.BlockSpec((F, D), lambda i: (0, 0), **resident)],
        out_specs=pl.BlockSpec((tm, D), lambda i: (i, 0)),
        out_shape=jax.ShapeDtypeStruct((M, D), jnp.float32),
        scratch_shapes=[pltpu.VMEM((tm, F), jnp.bfloat16)],
        compiler_params=pltpu.CompilerParams(dimension_semantics=("arbitrary",),
                                             vmem_limit_bytes=VMEM_LIMIT_BYTES),
        name="ffn_half_step",
    )(x2, g, shift, scale, gate, w13, w13, w2)


def _in_proj_kernel(x_ref, g_ref, shift_ref, scale_ref, wa_ref, wb_ref, wc_ref, wd_ref,
                    ba_ref, bb_ref, bc_ref, bd_ref, oa_ref, ob_ref, oc_ref, od_ref):
    h = _modulated_norm(x_ref[...], g_ref[...], scale_ref[0], shift_ref[0]).astype(jnp.bfloat16)
    for w_ref, b_ref, o_ref in ((wa_ref, ba_ref, oa_ref), (wb_ref, bb_ref, ob_ref),
                                (wc_ref, bc_ref, oc_ref), (wd_ref, bd_ref, od_ref)):
        o_ref[...] = jnp.dot(h, w_ref[...], preferred_element_type=jnp.float32) + b_ref[...]


def in_projection(x2, g, shift, scale, ws, bs, tokens_per_batch):
    M, D = x2.shape
    tm = PROJ_TOKEN_TILE
    tiles_per_batch = tokens_per_batch // tm
    mod_spec = pl.BlockSpec((1, 1, D), lambda i: (i // tiles_per_batch, 0, 0))
    const = lambda a, **kw: pl.BlockSpec(a.shape, lambda i: (0, 0), **kw)
    return pl.pallas_call(
        _in_proj_kernel,
        grid=(M // tm,),
        in_specs=[pl.BlockSpec((tm, D), lambda i: (i, 0)), const(g), mod_spec, mod_spec]
                 + [const(w, pipeline_mode=pl.Buffered(1)) for w in ws] + [const(b) for b in bs],
        out_specs=[pl.BlockSpec((tm, w.shape[1]), lambda i: (i, 0)) for w in ws],
        out_shape=[jax.ShapeDtypeStruct((M, w.shape[1]), jnp.float32) for w in ws],
        compiler_params=pltpu.CompilerParams(dimension_semantics=("arbitrary",),
                                             vmem_limit_bytes=VMEM_LIMIT_BYTES),
        name="in_projection",
    )(x2, g, shift, scale, *ws, *bs)


def _group_columns():
    offs, o = {}, 0
    for n, s in IN_LAYOUT:
        offs[n] = np.arange(o, o + s)
        o += s
    cat = lambda names: np.concatenate([offs[n] for n in names])
    return (cat(["a_q", "a_k_cmp", "a_v_cmp", "a_k_slc", "a_v_slc", "a_k_win", "a_v_win", "a_gate"]),
            cat(["b_b", "b_c", "b_x"]),
            cat(["c_q", "c_k", "c_v", "c_z", "c_beta", "c_alpha"]),
            cat(["d_q", "d_k", "d_v", "d_o", "d_i", "d_f"]))


GROUP_SLAB_WIDTH = (768, 768, 1152, 1152)


def _take_columns(a, idx, width):
    cuts = [0] + [k + 1 for k in range(idx.size - 1) if idx[k + 1] != idx[k] + 1] + [idx.size]
    runs = [a[..., int(idx[s]):int(idx[e - 1]) + 1] for s, e in zip(cuts[:-1], cuts[1:])]
    return jnp.concatenate(runs + [jnp.zeros(a.shape[:-1] + (width - idx.size,), a.dtype)], axis=-1)


def split_in_weights(w, b):
    ws, bs = [], []
    for idx, width in zip(_group_columns(), GROUP_SLAB_WIDTH):
        ws.append(_take_columns(w, idx, width).astype(jnp.bfloat16))
        bs.append(_take_columns(b, idx, width)[None, :])
    return ws, bs


def _out_proj_kernel(x_ref, ya_ref, yb_ref, yc_ref, yd_ref, gate_ref, w_ref, o_ref):
    z = None
    for g, y_ref in enumerate((ya_ref, yb_ref, yc_ref, yd_ref)):
        part = jnp.dot(y_ref[...].astype(jnp.bfloat16), w_ref[g * GROUP_WIDTH:(g + 1) * GROUP_WIDTH, :],
                       preferred_element_type=jnp.float32)
        z = part if z is None else z + part
    o_ref[...] = x_ref[...] + gate_ref[0] * z


def out_projection(x2, ys, gate, w, tokens_per_batch):
    M, D = x2.shape
    tm = PROJ_TOKEN_TILE
    tiles_per_batch = tokens_per_batch // tm
    return pl.pallas_call(
        _out_proj_kernel,
        grid=(M // tm,),
        in_specs=[pl.BlockSpec((tm, D), lambda i: (i, 0))]
                 + [pl.BlockSpec((tm, GROUP_WIDTH), lambda i: (i, 0)) for _ in ys]
                 + [pl.BlockSpec((1, 1, D), lambda i: (i // tiles_per_batch, 0, 0)),
                    pl.BlockSpec(w.shape, lambda i: (0, 0), pipeline_mode=pl.Buffered(1))],
        out_specs=pl.BlockSpec((tm, D), lambda i: (i, 0)),
        out_shape=jax.ShapeDtypeStruct((M, D), jnp.float32),
        compiler_params=pltpu.CompilerParams(dimension_semantics=("arbitrary",),
                                             vmem_limit_bytes=VMEM_LIMIT_BYTES),
        name="out_projection",
    )(x2, *ys, gate, w)


def t5_bucket(dist):
    n = jnp.maximum(dist, 0)
    max_exact = N_BUCKETS // 2
    nf = jnp.maximum(n, 1).astype(jnp.float32)
    large = max_exact + (jnp.log(nf / max_exact) / math.log(MAX_DISTANCE / max_exact)
                         * (N_BUCKETS - max_exact)).astype(jnp.int32)
    large = jnp.minimum(large, N_BUCKETS - 1)
    return jnp.where(n < max_exact, n, large)


def _nt_dot(a, b):
    return lax.dot_general(a, b, (((1,), (1,)), ((), ())), preferred_element_type=jnp.float32)


def _stack_heads(qn):
    lane = lax.broadcasted_iota(jnp.int32, (qn.shape[0], 128), 1)
    lo = lane < HEAD_DIM
    zero = jnp.zeros((), qn.dtype)
    halves = (qn[:, 0:128], qn[:, 128:256])
    return jnp.concatenate([jnp.where(lo if h % 2 == 0 else ~lo, halves[h // 2], zero)
                            for h in range(GROUP_HEADS)], axis=0)


def _nsa_prep_kernel(q_ref, kvs_ref, kvw_ref, qg_ref, kg_ref, seg_ref,
                     qn_ref, qnt_ref, kxs_ref, v1s_ref, kkw_ref, v1w_ref):
    tm = q_ref.shape[0]
    q = q_ref[...]
    ss = jnp.dot(q * q, seg_ref[...], precision=HIGHEST, preferred_element_type=jnp.float32)
    qn = q * lax.rsqrt(ss * (1.0 / HEAD_DIM) + EPS) * qg_ref[...]
    qn_ref[...] = qn.astype(qn_ref.dtype)
    qnt_ref[...] = (qn * LOG2E).T.astype(qnt_ref.dtype)
    lane = lax.broadcasted_iota(jnp.int32, (tm, 128), 1)
    lo = lane < HEAD_DIM

    def split(x):
        xr = pltpu.roll(x, HEAD_DIM, 1)
        ss = jnp.sum(jnp.where(lo, x * x, 0.0), axis=-1, keepdims=True)
        kk = jnp.where(lo, x, xr) * lax.rsqrt(ss * (1.0 / HEAD_DIM) + EPS) * kg_ref[...]
        return kk, jnp.where(lo, xr, 1.0)

    kk, v1 = split(kvs_ref[...])
    tok = pl.program_id(1) * tm + lax.broadcasted_iota(jnp.int32, (tm, 128), 0)
    onehot = jnp.where(lane == tok // SLC_BLOCK, 1.0, 0.0)
    kxs_ref[...] = jnp.concatenate([kk, onehot], axis=1).astype(kxs_ref.dtype)
    v1s_ref[...] = v1.T[0:V1_ROWS].astype(v1s_ref.dtype)
    kk, v1 = split(kvw_ref[...])
    kkw_ref[...] = kk.astype(kkw_ref.dtype)
    v1w_ref[...] = v1.T[0:V1_ROWS].astype(v1w_ref.dtype)


def nsa_prep(pa, qg, kg2, seg):
    B, T, _ = pa.shape
    tm = 512
    bf16 = jnp.bfloat16
    col = lambda w, j: pl.BlockSpec((None, tm, w), lambda b, i: (b, i, j))
    rowblk = lambda h: pl.BlockSpec((None, h, tm), lambda b, i: (b, 0, i))
    const = lambda shape: pl.BlockSpec(shape, lambda b, i: (0, 0))
    return pl.pallas_call(
        _nsa_prep_kernel,
        grid=(B, T // tm),
        in_specs=[col(256, 0), col(128, 3), col(128, 4), const((1, 256)), const((1, 128)), const((256, 256))],
        out_specs=[col(256, 0), rowblk(256), col(256, 0), rowblk(V1_ROWS), col(128, 0), rowblk(V1_ROWS)],
        out_shape=[jax.ShapeDtypeStruct((B, T, 256), bf16), jax.ShapeDtypeStruct((B, 256, T), bf16),
                   jax.ShapeDtypeStruct((B, T, 256), bf16), jax.ShapeDtypeStruct((B, V1_ROWS, T), bf16),
                   jax.ShapeDtypeStruct((B, T, 128), bf16), jax.ShapeDtypeStruct((B, V1_ROWS, T), bf16)],
        name="nsa_prep",
    )(pa, pa, pa, qg, kg2, seg)


def _compress_kernel(kch_ref, vch_ref, plo_ref, phi_ref, w1k_ref, w2k_ref, w1v_ref, w2v_ref, kg_ref,
                     kkc_ref, vvc_ref):
    nrow = kch_ref.shape[0]
    half = CMP_STRIDE * HEAD_DIM
    bf16 = jnp.bfloat16
    valid = lax.broadcasted_iota(jnp.int32, (nrow, 128), 0) < nrow - 1

    def comp(ch, w1_ref, w2_ref):
        a = jnp.dot((ch + plo_ref[...]).astype(bf16), w1_ref[0:half, :], preferred_element_type=jnp.float32)
        b = jnp.dot((ch + phi_ref[...]).astype(bf16), w1_ref[half:2 * half, :],
                    preferred_element_type=jnp.float32)
        hid = a + pltpu.roll(b, nrow - 1, 0)
        hid = hid * jax.nn.sigmoid(hid)
        return jnp.dot(hid.astype(bf16), w2_ref[...], preferred_element_type=jnp.float32)

    kc = comp(kch_ref[...], w1k_ref, w2k_ref)
    kc = kc * lax.rsqrt(jnp.mean(kc * kc, axis=-1, keepdims=True) + EPS) * kg_ref[...]
    kkc_ref[...] = jnp.where(valid, kc, 0.0).astype(kkc_ref.dtype)
    vc = comp(vch_ref[...], w1v_ref, w2v_ref)
    vvc_ref[...] = jnp.where(valid, vc, 0.0).astype(vvc_ref.dtype)


def nsa_compress(kch, vch, plo, phi, w1k, w2k, w1v, w2v, kg2):
    B, nrow, width = kch.shape
    bf16 = jnp.bfloat16
    full = lambda a: pl.BlockSpec(a.shape, lambda b: (0,) * a.ndim)
    bspec = pl.BlockSpec((None, nrow, width), lambda b: (b, 0, 0))
    ospec = pl.BlockSpec((None, nrow, 128), lambda b: (b, 0, 0))
    return pl.pallas_call(
        _compress_kernel,
        grid=(B,),
        in_specs=[bspec, bspec, full(plo), full(phi), full(w1k), full(w2k), full(w1v), full(w2v), full(kg2)],
        out_specs=[ospec, ospec],
        out_shape=[jax.ShapeDtypeStruct((B, nrow, 128), bf16), jax.ShapeDtypeStruct((B, nrow, 128), bf16)],
        name="nsa_compress",
    )(kch, vch, plo, phi, w1k, w2k, w1v, w2v, kg2)


CMP_TILES_PER_STEP = 2


def _cmp_select_kernel(qn_ref, kkc_ref, vvc_ref, cb_ref, ovt_ref, ocmp_ref, pen_ref, sc_ref, *, n_sel):
    i = pl.program_id(1)
    n_tiles = cb_ref.shape[0]
    tq = qn_ref.shape[0] // n_tiles
    nblk = ovt_ref.shape[0]
    bf16 = jnp.bfloat16
    lo = lax.broadcasted_iota(jnp.int32, (tq, 128), 1) < HEAD_DIM
    blk = lax.broadcasted_iota(jnp.int32, (nblk, tq), 0)
    scores = []
    for g in range(n_tiles):
        rows = slice(g * tq, (g + 1) * tq)
        q4 = _stack_heads(qn_ref[rows, :])
        s = _nt_dot(q4, kkc_ref[...]) + cb_ref[g]
        m = jnp.max(s, axis=-1, keepdims=True)
        e = jnp.where(s > 0.5 * NEG, jnp.exp(s - m), 0.0)
        p = e * (1.0 / jnp.maximum(jnp.sum(e, axis=-1, keepdims=True), 1e-30))
        o = jnp.dot(p.astype(bf16), vvc_ref[...], preferred_element_type=jnp.float32)
        ocmp_ref[rows, 0:128] = jnp.where(lo, o[0:tq], o[tq:2 * tq])
        ocmp_ref[rows, 128:256] = jnp.where(lo, o[2 * tq:3 * tq], o[3 * tq:4 * tq])
        psum = p[0:tq] + p[tq:2 * tq] + p[2 * tq:3 * tq] + p[3 * tq:4 * tq]
        hi = psum.astype(bf16)
        lo_part = (psum - hi.astype(jnp.float32)).astype(bf16)
        score = _nt_dot(ovt_ref[...], hi) + _nt_dot(ovt_ref[...], lo_part)
        qpos = (i * n_tiles + g) * tq + lax.broadcasted_iota(jnp.int32, (nblk, tq), 1)
        cur = qpos // SLC_BLOCK
        forced = (blk == 0) | ((cur - blk >= 0) & (cur - blk < N_LOCAL_SLC))
        score = jnp.where(forced, FORCE, score)
        score = jnp.where(blk <= cur, score, -FORCE)
        sc_ref[g] = score
        scores.append(score)
    blocks_per_tile = tq // SLC_BLOCK

    def rank_step(t, cnts):
        out = []
        for g in range(n_tiles):
            cnt = cnts[g]
            for u in range(blocks_per_tile):
                jp = t * blocks_per_tile + u
                row = sc_ref[g, pl.ds(jp, 1), :]
                tie = jnp.where(blk > jp, 1.0, 0.0)
                cnt = cnt + jnp.where(row > scores[g], 1.0, jnp.where(row == scores[g], tie, 0.0))
            out.append(cnt)
        return tuple(out)

    zeros = tuple(jnp.zeros((nblk, tq), jnp.float32) for _ in range(n_tiles))
    cnts = lax.fori_loop(0, (i + 1) * n_tiles, rank_step, zeros)
    for g in range(n_tiles):
        pen = jnp.where(cnts[g] < n_sel, 0.0, NEG)
        if nblk < 128:
            pen = jnp.concatenate([pen, jnp.zeros((128 - nblk, tq), jnp.float32)], axis=0)
        pen_ref[g] = pen.astype(pen_ref.dtype)


def nsa_cmp_select(qn, kkc, vvc, cbias, ovt, n_sel):
    B, T, _ = qn.shape
    nrow = kkc.shape[1]
    nblk = ovt.shape[0]
    tq = Q_BLOCK
    g = CMP_TILES_PER_STEP
    return pl.pallas_call(
        functools.partial(_cmp_select_kernel, n_sel=n_sel),
        grid=(B, T // (g * tq)),
        in_specs=[pl.BlockSpec((None, g * tq, 256), lambda b, i: (b, i, 0)),
                  pl.BlockSpec((None, nrow, 128), lambda b, i: (b, 0, 0)),
                  pl.BlockSpec((None, nrow, 128), lambda b, i: (b, 0, 0)),
                  pl.BlockSpec((g, 4 * tq, nrow), lambda b, i: (i, 0, 0)),
                  pl.BlockSpec((nblk, nrow), lambda b, i: (0, 0))],
        out_specs=[pl.BlockSpec((None, g * tq, 256), lambda b, i: (b, i, 0)),
                   pl.BlockSpec((None, g, 128, tq), lambda b, i: (b, i, 0, 0))],
        out_shape=[jax.ShapeDtypeStruct((B, T, 256), jnp.float32),
                   jax.ShapeDtypeStruct((B, T // tq, 128, tq), jnp.bfloat16)],
        scratch_shapes=[pltpu.VMEM((g, nblk, tq), jnp.float32)],
        name="nsa_cmp_select",
    )(qn, kkc, vvc, cbias, ovt)


NSA_KEY_TILE = 256
NSA_QUERY_TILE = 256


def _nsa_attn_kernel(qnt_ref, pen_ref, ocmp_ref, gate_ref, kxs_ref, v1s_ref, kkw_ref, v1w_ref,
                     tabs_ref, tabw_ref, mixg_ref, o_ref, acc_ref, m_ref, sa_ref, sb_ref):
    i = pl.program_id(1)
    tq = o_ref.shape[0]
    kt = tabs_ref.shape[1]
    f32, bf16 = jnp.float32, jnp.bfloat16
    qt = qnt_ref[...]
    lo = lax.broadcasted_iota(jnp.int32, (128, tq), 0) < HEAD_DIM
    zero = jnp.zeros((), qt.dtype)
    q4t = jnp.concatenate([jnp.where(lo if h % 2 == 0 else ~lo, qt[128 * (h // 2):128 * (h // 2) + 128], zero)
                           for h in range(GROUP_HEADS)], axis=1)
    pen = jnp.concatenate([pen_ref[t] for t in range(pen_ref.shape[0])], axis=1)
    rhs_sel = jnp.concatenate([q4t, jnp.concatenate([pen] * GROUP_HEADS, axis=1)], axis=0)

    def branch(rhs, k_ref, v_ref, tab_ref, first_tile, last_tile):
        n_tab = tab_ref.shape[0] - 1
        m_ref[...] = jnp.full(m_ref.shape, NEG, f32)
        acc_ref[...] = jnp.zeros(acc_ref.shape, f32)

        def key_offset(j):
            return pl.multiple_of(jnp.minimum(j, last_tile) * kt, kt)

        def scores(j):
            delta = jnp.where(j > last_tile, n_tab, jnp.minimum(i - j * (kt // tq), n_tab - 1))
            return jnp.dot(k_ref[pl.ds(key_offset(j), kt), :], rhs, preferred_element_type=f32) + tab_ref[delta]

        def absorb(s, j):
            m_old = m_ref[...]
            m_new = jnp.maximum(m_old, jnp.max(s, axis=0, keepdims=True))
            p = jnp.exp2(s - m_new)
            pv =jnp.dot(v_ref[:, pl.ds(key_offset(j), kt)], p.astype(bf16), preferred_element_type=f32)
            acc_ref[...] = jnp.exp2(m_old - m_new) * acc_ref[...] + pv
            m_ref[...] = m_new

        sa_ref[...] = scores(first_tile)

        def body(t, carry):
            j = first_tile + 2 * t
            sb_ref[...] = scores(j + 1)
            absorb(sa_ref[...], j)
            sa_ref[...] = scores(j + 2)
            absorb(sb_ref[...], j + 1)
            return carry

        lax.fori_loop(0, (last_tile - first_tile) // 2 + 1, body, 0)
        acc = acc_ref[...]
        return acc[0:HEAD_DIM] / jnp.maximum(acc[HEAD_DIM:HEAD_DIM + 1], 1e-30)

    def window_branch(rhs):
        n_tab = tabw_ref.shape[0] - 1
        m = jnp.full((1, rhs.shape[1]), NEG, f32)
        acc = jnp.zeros((V1_ROWS, rhs.shape[1]), f32)
        for d in range(n_tab):
            j = i - d
            off = pl.multiple_of(jnp.maximum(j, 0) * kt, kt)
            s = jnp.dot(kkw_ref[pl.ds(off, kt), :], rhs, preferred_element_type=f32)
            s = s + tabw_ref[jnp.where(j < 0, n_tab, d)]
            m_new = jnp.maximum(m, jnp.max(s, axis=0, keepdims=True))
            p = jnp.exp2(s - m_new)
            acc = jnp.exp2(m - m_new) * acc +jnp.dot(v1w_ref[:, pl.ds(off, kt)], p.astype(bf16),
                                                     preferred_element_type=f32)
            m = m_new
        return acc[0:HEAD_DIM] / jnp.maximum(acc[HEAD_DIM:HEAD_DIM + 1], 1e-30)

    last = (i * tq) // kt
    if kt == tq:
        o_w = window_branch(q4t)
    else:
        o_w = branch(q4t, kkw_ref, v1w_ref, tabw_ref, (jnp.maximum(i - WINDOW // tq, 0) * tq) // kt, last)
    o_s = branch(rhs_sel, kxs_ref, v1s_ref, tabs_ref, 0, last)
    gt = jax.nn.sigmoid(gate_ref[...]).T
    oct_ = ocmp_ref[...].T
    parts = []
    for h in range(GROUP_HEADS):
        cols = slice(h * tq, (h + 1) * tq)
        parts.append(gt[3 * h:3 * h + 1] * oct_[h * HEAD_DIM:(h + 1) * HEAD_DIM]
                     + gt[3 * h + 1:3 * h + 2] * o_s[:, cols] + gt[3 * h + 2:3 * h + 3] * o_w[:, cols])
    yt = jnp.concatenate(parts, axis=0)
    yt = yt * lax.rsqrt(jnp.mean(yt * yt, axis=0, keepdims=True) + EPS) * mixg_ref[...]
    o_ref[...] = yt.T


def nsa_attention(qnt, pen, ocmp, pa, kxs, v1s, kkw, v1w, tab_s, tab_w, mixg):
    B, _, T = qnt.shape
    tq = min(NSA_QUERY_TILE, T)
    tile = lambda w, j: pl.BlockSpec((None, tq, w), lambda b, i: (b, i, j))
    whole = lambda w: pl.BlockSpec((None, T, w), lambda b, i: (b, 0, 0))
    whole_t = pl.BlockSpec((None, V1_ROWS, T), lambda b, i: (b, 0, 0))
    full = lambda a: pl.BlockSpec(a.shape, lambda b, i: (0,) * a.ndim)
    return pl.pallas_call(
        _nsa_attn_kernel,
        grid=(B, T // tq),
        in_specs=[pl.BlockSpec((None, 256, tq), lambda b, i: (b, 0, i)),
                  pl.BlockSpec((None, tq // Q_BLOCK, 128, Q_BLOCK), lambda b, i: (b, i, 0, 0)),
                  tile(256, 0), tile(128, 5),
                  whole(256), whole_t, whole(128), whole_t, full(tab_s), full(tab_w), full(mixg)],
        out_specs=tile(256, 0),
        out_shape=jax.ShapeDtypeStruct((B, T, 256), jnp.float32),
        scratch_shapes=[pltpu.VMEM((V1_ROWS, GROUP_HEADS * tq), jnp.float32),
                        pltpu.VMEM((1, GROUP_HEADS * tq), jnp.float32),
                        pltpu.VMEM((tab_s.shape[1], GROUP_HEADS * tq), jnp.float32),
                        pltpu.VMEM((tab_s.shape[1], GROUP_HEADS * tq), jnp.float32)],
        compiler_params=pltpu.CompilerParams(dimension_semantics=("arbitrary", "arbitrary"),
                                             vmem_limit_bytes=VMEM_LIMIT_BYTES),
        name="nsa_attention",
    )(qnt, pen, ocmp, pa, kxs, v1s, kkw, v1w, tab_s, tab_w, mixg)


def _toeplitz(w, rows, cols):
    H, n = w.shape
    period = n + 1
    flat = jnp.tile(jnp.pad(w, ((0, 0), (0, 1))), (1, rows + 1))[:, :rows * (period + 1)]
    return jnp.flip(flat.reshape(H, rows, period + 1)[:, :, :cols], axis=2)


def nsa_bias_tables(t5_table, T):
    H = GROUP_HEADS
    tq = Q_BLOCK
    ta = min(NSA_QUERY_TILE, T)
    kt = min(NSA_KEY_TILE, T)
    nrow = T // CMP_STRIDE
    dmax = WINDOW + 2 * ta + kt
    onehot = (t5_bucket(jnp.arange(dmax))[:, None] == jnp.arange(N_BUCKETS)[None, :]).astype(jnp.float32)
    by_dist = jnp.dot(onehot, t5_table, precision=HIGHEST).T
    neg = lambda n: jnp.full((H, n), NEG, jnp.float32)
    f_s = jnp.concatenate([neg(kt - 1), by_dist * LOG2E], axis=1)
    f_w = jnp.concatenate([neg(kt - 1), by_dist[:, :WINDOW] * LOG2E, neg(dmax - WINDOW)], axis=1)

    def tiles(f, n):
        out = [_toeplitz(f[:, k * ta:k * ta + ta + kt - 1], ta, kt).reshape(H * ta, kt).T for k in range(n)]
        return jnp.stack(out + [jnp.full((kt, H * ta), NEG, jnp.float32)])

    n_s = -(-(kt + MAX_DISTANCE) // ta) + 1
    n_w = WINDOW // ta + kt // ta
    tab_s, tab_w = tiles(f_s, n_s), tiles(f_w, n_w)
    per_tile = tq // CMP_STRIDE
    d0 = -CMP_STRIDE * (nrow - 1) - (CMP_BLOCK - 1)
    n16 = 2 * nrow + per_tile - 1
    n_pos = CMP_STRIDE * n16 + d0
    far = jnp.broadcast_to(by_dist[:, -1:], (H, max(n_pos - dmax, 0)))
    w16 = jnp.concatenate([neg(-d0), by_dist[:, :n_pos], far], axis=1).reshape(H, n16, CMP_STRIDE)
    base = jnp.stack([w16[:, a:a + 2 * nrow, :] for a in range(per_tile)], axis=1)
    base = jnp.flip(base.transpose(0, 1, 3, 2).reshape(H * tq, 2 * nrow), axis=1)
    cbias = jnp.stack([base[:, nrow - per_tile * i: 2 * nrow - per_tile * i] for i in range(T // tq)])
    return tab_s, tab_w, cbias


def nsa_overlap_t(T):
    nrow = T // CMP_STRIDE
    nblk = T // SLC_BLOCK
    ci = np.arange(nrow)[None, :]
    bj = np.arange(nblk)[:, None]
    ov = (ci * CMP_STRIDE < (bj + 1) * SLC_BLOCK) & (ci * CMP_STRIDE + CMP_BLOCK > bj * SLC_BLOCK)
    ov = ov & (ci < nrow - 1)
    return jnp.asarray(ov.astype(np.float32), dtype=jnp.bfloat16)


def nsa_mixer_pallas(pa, q_norm_g, k_norm_g, cmp_pos, cmp_k_w1, cmp_k_w2, cmp_v_w1, cmp_v_w2,
                     tables, ovt, mixg):
    B, T, _ = pa.shape
    bf16 = jnp.bfloat16
    tab_s, tab_w, cbias = tables
    qg = (jnp.tile(q_norm_g, GROUP_HEADS) * HEAD_DIM ** -0.5)[None, :]
    kg2 = jnp.tile(k_norm_g, 2)[None, :]
    qn, qnt, kxs, v1s, kkw, v1w = nsa_prep(pa, qg, kg2, _head_mask())
    nrow = T // CMP_STRIDE
    kch = pa[:, :, 256:320].reshape(B, nrow, CMP_STRIDE * HEAD_DIM)
    vch = pa[:, :, 320:384].reshape(B, nrow, CMP_STRIDE * HEAD_DIM)
    plo = cmp_pos[:CMP_STRIDE].reshape(1, -1)
    phi = cmp_pos[CMP_STRIDE:].reshape(1, -1)
    dup = lambda w: jnp.concatenate([w, w], axis=1).astype(bf16)
    kkc, vvc = nsa_compress(kch, vch, plo, phi, cmp_k_w1.astype(bf16), dup(cmp_k_w2),
                            cmp_v_w1.astype(bf16), dup(cmp_v_w2), kg2)
    n_sel = min(N_SLC, T // SLC_BLOCK)
    ocmp, pen = nsa_cmp_select(qn, kkc, vvc, cbias, ovt, n_sel)
    mixg_col = jnp.broadcast_to(mixg.reshape(-1, 1), (GROUP_WIDTH, min(NSA_QUERY_TILE, T)))
    return nsa_attention(qnt, pen, ocmp, pa, kxs, v1s, kkw, v1w, tab_s, tab_w, mixg_col)


CONV_CARRY_ROWS = 8


def _shifted(cat, shift, rows):
    return pltpu.roll(cat, shift, 0)[CONV_CARRY_ROWS:CONV_CARRY_ROWS + rows]


def _sconv_kernel(pb_ref, w_ref, g_ref, o_ref, carry_ref):
    rows = pb_ref.shape[0]

    @pl.when(pl.program_id(1) == 0)
    def _():
        carry_ref[...] = jnp.zeros(carry_ref.shape, jnp.float32)

    u = pb_ref[:, 256:512] * pb_ref[:, 512:768]
    cat = jnp.concatenate([carry_ref[...], u], axis=0)
    conv = u * w_ref[2:3, :] + _shifted(cat, 1, rows) * w_ref[1:2, :] + _shifted(cat, 2, rows) * w_ref[0:1, :]
    carry_ref[...] = u[rows - CONV_CARRY_ROWS:rows]
    y = pb_ref[:, 0:256] * conv
    o_ref[...] = y * lax.rsqrt(jnp.mean(y * y, axis=-1, keepdims=True) + EPS) * g_ref[...]


def short_conv_pallas(pb, conv_w, mixg):
    B, T, _ = pb.shape
    rows = 512
    w = jnp.pad(conv_w, ((0, 8 - conv_w.shape[0]), (0, 0)))
    return pl.pallas_call(
        _sconv_kernel,
        grid=(B, T // rows),
        in_specs=[pl.BlockSpec((None, rows, 768), lambda b, t: (b, t, 0)),
                  pl.BlockSpec((8, 256), lambda b, t: (0, 0)),
                  pl.BlockSpec((1, 256), lambda b, t: (0, 0))],
        out_specs=pl.BlockSpec((None, rows, 256), lambda b, t: (b, t, 0)),
        out_shape=jax.ShapeDtypeStruct((B, T, 256), jnp.float32),
        scratch_shapes=[pltpu.VMEM((CONV_CARRY_ROWS, 256), jnp.float32)],
        compiler_params=pltpu.CompilerParams(dimension_semantics=("arbitrary", "arbitrary")),
        name="short_conv",
    )(pb, w, mixg)


CHUNKS_PER_STEP = 4
SEQS_PER_STEP = 2


def _tn_dot(a, b):
    return lax.dot_general(a, b, (((0,), (0,)), ((), ())), preferred_element_type=jnp.float32)


def _split_bf16(a, parts):
    out, rest = [], a
    for _ in range(parts):
        piece = rest.astype(jnp.bfloat16)
        out.append(piece)
        rest = rest - piece.astype(jnp.float32)
    return out


def _dot_sel(a, sel, parts=3):
    return sum(jnp.dot(p, sel, preferred_element_type=jnp.float32) for p in _split_bf16(a, parts))


def _sel_dot(sel, a, parts=3):
    return sum(jnp.dot(sel, p, preferred_element_type=jnp.float32) for p in _split_bf16(a, parts))


def _softplus(x):
    return jnp.maximum(x, 0.0) + jnp.log(1.0 + jnp.exp(-jnp.abs(x)))


def _block_diag(a, hmb):
    return jnp.concatenate([a.astype(jnp.bfloat16)] * GROUP_HEADS, axis=0) * hmb


def _diag_row(a, eye):
    return jnp.sum(a * eye, axis=0, keepdims=True)


def _head_norm(o, hmb):
    return o * lax.rsqrt(_dot_sel(o * o, hmb, parts=2) * (1.0 / HEAD_DIM) + EPS)


def _head_l2(o, hmb):
    return o * lax.rsqrt(_dot_sel(o * o, hmb, parts=2) + EPS)


def _xform_masks(L):
    i = np.arange(L)[:, None]
    j = np.arange(L)[None, :]
    masks = []
    s = 1
    while s < L:
        masks.append((i // (2 * s) == j // (2 * s)) & (i % (2 * s) >= s) & (j % (2 * s) < s))
        s *= 2
    masks += [i == j, j <= i, j < i]
    return jnp.asarray(np.tile(np.stack(masks).astype(np.float32), (1, 1, GROUP_HEADS)))


def _expand_matrix(first_lane):
    e = np.zeros((128, GROUP_WIDTH), np.float32)
    for h in range(GROUP_HEADS):
        e[first_lane + h, h * HEAD_DIM:(h + 1) * HEAD_DIM] = 1.0
    return jnp.asarray(e, jnp.bfloat16)


def _head_mask():
    return jnp.asarray(np.kron(np.eye(GROUP_HEADS), np.ones((HEAD_DIM, HEAD_DIM))), jnp.float32)


def _per_head_rows(*vecs):
    rows = [jnp.repeat(v, HEAD_DIM) if v.shape[0] == GROUP_HEADS else jnp.tile(v, GROUP_HEADS) for v in vecs]
    return jnp.pad(jnp.stack(rows), ((0, 8 - len(rows)), (0, 0)))


def _chunk_consts(L, n_chunks):
    tri = np.kron(np.eye(n_chunks), np.tril(np.ones((L, L)))).astype(np.float32)
    return (_head_mask(), jnp.asarray(tri, jnp.bfloat16), _expand_matrix(0), _expand_matrix(GROUP_HEADS),
            _xform_masks(L))


def _const_spec(a):
    return pl.BlockSpec(a.shape, lambda b, t: (0,) * a.ndim)


def _gdn_kernel(pc_ref, cw_ref, hp_ref, hm_ref, tri_ref, eb_ref, ea_ref, lv_ref, o_ref, s_ref, carry_ref):
    L = GDN_CHUNK
    n_seq, rows = pc_ref.shape[0], pc_ref.shape[1]
    f32, bf16 = jnp.float32, jnp.bfloat16
    n_lev = lv_ref.shape[0] - 3

    @pl.when(pl.program_id(1) == 0)
    def _():
        s_ref[...] = jnp.zeros(s_ref.shape, f32)
        carry_ref[...] = jnp.zeros(carry_ref.shape, f32)

    hm = hm_ref[...]
    hmb = hm.astype(bf16)
    eye, incl, strict = lv_ref[n_lev], lv_ref[n_lev + 1], lv_ref[n_lev + 2]

    q_c, k_c, v_c, beta_c, gc_c = [], [], [], [], []
    for b in range(n_seq):
        x = pc_ref[b, :, 0:768]
        cat = jnp.concatenate([carry_ref[b], x], axis=0)
        conv = x * cw_ref[3:4, :]
        for s in (1, 2, 3):
            conv = conv + _shifted(cat, s, rows) * cw_ref[3 - s:4 - s, :]
        carry_ref[b] = x[rows - CONV_CARRY_ROWS:rows]
        qkv = conv * jax.nn.sigmoid(conv)
        q_all = _head_l2(qkv[:, 0:256], hmb) * HEAD_DIM ** -0.5
        k_all = _head_l2(qkv[:, 256:512], hmb)
        tail = pc_ref[b, :, 1024:1152]
        beta_all = jax.nn.sigmoid(_dot_sel(tail, eb_ref[...]))
        g = hp_ref[0:1, :] * _softplus(_dot_sel(tail, ea_ref[...]) + hp_ref[1:2, :])
        gc_all = _sel_dot(tri_ref[...], g)
        for c in range(rows // L):
            sl = slice(c * L, (c + 1) * L)
            q_c.append(q_all[sl])
            k_c.append(k_all[sl])
            v_c.append(qkv[sl, 512:768])
            beta_c.append(beta_all[sl])
            gc_c.append(gc_all[sl])
    chains = range(len(q_c))
    egc = [jnp.exp(gc) for gc in gc_c]
    kb = [k_c[i] * beta_c[i] for i in chains]
    a_mat, qk = [], []
    for i in chains:
        seg = jnp.exp(jnp.where(incl > 0.5, gc_c[i] - _diag_row(gc_c[i], eye), NEG))
        k_bd = _block_diag(k_c[i], hmb)
        a_mat.append(_nt_dot(kb[i].astype(bf16), k_bd) * seg * strict)
        qk.append((_nt_dot(q_c[i].astype(bf16), k_bd) * seg).astype(bf16))
    t_inv = [eye - a * lv_ref[0] for a in a_mat]
    for lev in range(1, n_lev):
        te = [jnp.dot(t_inv[i].astype(bf16), _block_diag(a_mat[i] * lv_ref[lev], hmb),
                      preferred_element_type=f32) for i in chains]
        t_inv = [t_inv[i] - jnp.dot(te[i].astype(bf16), _block_diag(t_inv[i], hmb), preferred_element_type=f32)
                 for i in chains]
    parts = []
    for i in chains:
        tb = t_inv[i].astype(bf16)
        u = jnp.dot(tb, _block_diag(v_c[i] * beta_c[i], hmb), preferred_element_type=f32)
        w = jnp.dot(tb, _block_diag(kb[i] * egc[i], hmb), preferred_element_type=f32)
        k_dec = k_c[i] * jnp.exp(gc_c[i][L - 1:L, :] - gc_c[i])
        parts.append((u, w.astype(bf16), (q_c[i] * egc[i]).astype(bf16), qk[i], k_dec.astype(bf16),
                      egc[i][L - 1:L, :]))
    per_seq = rows // L
    state = [s_ref[b] for b in range(n_seq)]
    outs = [[] for _ in range(n_seq)]
    for c in range(per_seq):
        for b in range(n_seq):
            u, w, q_dec, qk_c, k_dec, g_tot = parts[b * per_seq + c]
            sb = state[b].astype(bf16)
            v_new = u - jnp.dot(w, sb, preferred_element_type=f32)
            outs[b].append(jnp.dot(q_dec, sb, preferred_element_type=f32)
                           + jnp.dot(qk_c, _block_diag(v_new, hmb), preferred_element_type=f32))
            state[b] = state[b] * g_tot + _tn_dot(k_dec, v_new.astype(bf16)) * hm
    for b in range(n_seq):
        s_ref[b] = state[b]
        o = jnp.concatenate(outs[b], axis=0)
        z = pc_ref[b, :, 768:1024]
        o_ref[b] = _head_norm(o, hmb) * hp_ref[2:3, :] * (z * jax.nn.sigmoid(z))


def gated_deltanet_pallas(pc, conv_w, A_log, dt_bias, norm_g):
    B, T, width = pc.shape
    L = GDN_CHUNK
    rows = L * min(CHUNKS_PER_STEP, T // L)
    n_seq = SEQS_PER_STEP if B % SEQS_PER_STEP == 0 else 1
    n = GROUP_HEADS * HEAD_DIM
    hp = _per_head_rows(-jnp.exp(A_log), dt_bias, norm_g)
    cw = jnp.pad(conv_w, ((0, 8 - conv_w.shape[0]), (0, 0)))
    consts = (cw, hp) + _chunk_consts(L, rows // L)
    return pl.pallas_call(
        _gdn_kernel,
        grid=(B // n_seq, T // rows),
        in_specs=[pl.BlockSpec((n_seq, rows, width), lambda b, t: (b, t, 0))] + [_const_spec(a) for a in consts],
        out_specs=pl.BlockSpec((n_seq, rows, 256), lambda b, t: (b, t, 0)),
        out_shape=jax.ShapeDtypeStruct((B, T, 256), jnp.float32),
        scratch_shapes=[pltpu.VMEM((n_seq, n, n), jnp.float32),
                        pltpu.VMEM((n_seq, CONV_CARRY_ROWS, 768), jnp.float32)],
        compiler_params=pltpu.CompilerParams(dimension_semantics=("arbitrary", "arbitrary"),
                                             vmem_limit_bytes=VMEM_LIMIT_BYTES),
        name="gated_deltanet",
    )(pc, *consts)


def _mlstm_kernel(pd_ref, hp_ref, hm_ref, tri_ref, ei_ref, ef_ref, lv_ref, o_ref, c_ref, n_ref, m_ref):
    L = MLSTM_CHUNK
    n_seq, rows = pd_ref.shape[0], pd_ref.shape[1]
    f32, bf16 = jnp.float32, jnp.bfloat16
    n_lev = lv_ref.shape[0] - 3

    @pl.when(pl.program_id(1) == 0)
    def _():
        c_ref[...] = jnp.zeros(c_ref.shape, f32)
        n_ref[...] = jnp.zeros(n_ref.shape, f32)
        m_ref[...] = jnp.zeros(m_ref.shape, f32)

    hm = hm_ref[...]
    hmb = hm.astype(bf16)
    eye, incl = lv_ref[n_lev], lv_ref[n_lev + 1]
    head_of_lane = lax.broadcasted_iota(jnp.int32, (L, GROUP_WIDTH), 1) // HEAD_DIM

    def head_max(a):
        out = jnp.zeros(a.shape, f32)
        for h in range(GROUP_HEADS):
            mine = head_of_lane == h
            out = jnp.where(mine, jnp.max(jnp.where(mine, a, NEG), axis=-1, keepdims=True), out)
        return out

    per_seq = rows // L
    parts = []
    for b in range(n_seq):
        tail = pd_ref[b, :, 1024:1152]
        log_i_all = _dot_sel(tail, ei_ref[...])
        log_f = -_softplus(-(_dot_sel(tail, ef_ref[...]) + hp_ref[0:1, :]))
        b_all = _sel_dot(tri_ref[...], log_f)
        for c in range(per_seq):
            sl = slice(c * L, (c + 1) * L)
            q = pd_ref[b, sl, 0:256]
            k = pd_ref[b, sl, 256:512] * HEAD_DIM ** -0.5
            v = pd_ref[b, sl, 512:768]
            bc, log_i = b_all[sl], log_i_all[sl]
            log_w = jnp.where(incl > 0.5, bc - _diag_row(bc - log_i, eye), NEG)
            b_last = bc[L - 1:L, :]
            log_w_end = b_last - bc + log_i
            qk = _nt_dot(q.astype(bf16), _block_diag(k, hmb))
            parts.append((q, k, v, bc, log_w, head_max(log_w), b_last, log_w_end,
                          jnp.max(log_w_end, axis=0, keepdims=True), qk))
    c_state = [c_ref[b] for b in range(n_seq)]
    n_state = [n_ref[b] for b in range(n_seq)]
    m_prev = [m_ref[b] for b in range(n_seq)]
    outs = [[] for _ in range(n_seq)]
    for c in range(per_seq):
        for b in range(n_seq):
            q, k, v, bc, log_w, m_intra, b_last, log_w_end, m_end, qk = parts[b * per_seq + c]
            log_inter = bc + m_prev[b]
            m_t = jnp.maximum(log_inter, m_intra)
            w_inter = jnp.exp(log_inter - m_t)
            s = qk * jnp.exp(log_w - m_t)
            num = w_inter * jnp.dot(q.astype(bf16), c_state[b].astype(bf16), preferred_element_type=f32)
            num = num + jnp.dot(s.astype(bf16), _block_diag(v, hmb), preferred_element_type=f32)
            den = w_inter * _dot_sel(q * n_state[b], hmb, parts=2) + _dot_sel(s, hmb, parts=2)
            outs[b].append(num / jnp.maximum(jnp.abs(den), jnp.exp(-m_t)))
            m_new = jnp.maximum(b_last + m_prev[b], m_end)
            w_old = jnp.exp(b_last + m_prev[b] - m_new)
            kw = k * jnp.exp(log_w_end - m_new)
            c_state[b] = w_old * c_state[b] + _tn_dot(kw.astype(bf16), v.astype(bf16)) * hm
            n_state[b] = w_old * n_state[b] + jnp.sum(kw, axis=0, keepdims=True)
            m_prev[b] = m_new
    for b in range(n_seq):
        c_ref[b] = c_state[b]
        n_ref[b] = n_state[b]
        m_ref[b] = m_prev[b]
        h = jnp.concatenate(outs[b], axis=0)
        o_ref[b] = _head_norm(h, hmb) * hp_ref[1:2, :] * jax.nn.sigmoid(pd_ref[b, :, 768:1024])


def mlstm_pallas(pd, f_bias, norm_g):
    B, T, width = pd.shape
    L = MLSTM_CHUNK
    rows = L * min(CHUNKS_PER_STEP, T // L)
    n_seq = SEQS_PER_STEP if B % SEQS_PER_STEP == 0 else 1
    n = GROUP_HEADS * HEAD_DIM
    consts = (_per_head_rows(f_bias, norm_g),) + _chunk_consts(L, rows // L)
    return pl.pallas_call(
        _mlstm_kernel,
        grid=(B // n_seq, T // rows),
        in_specs=[pl.BlockSpec((n_seq, rows, width), lambda b, t: (b, t, 0))] + [_const_spec(a) for a in consts],
        out_specs=pl.BlockSpec((n_seq, rows, 256), lambda b, t: (b, t, 0)),
        out_shape=jax.ShapeDtypeStruct((B, T, 256), jnp.float32),
        scratch_shapes=[pltpu.VMEM((n_seq, n, n), jnp.float32), pltpu.VMEM((n_seq, 1, 256), jnp.float32),
                        pltpu.VMEM((n_seq, 1, 256), jnp.float32)],
        compiler_params=pltpu.CompilerParams(dimension_semantics=("arbitrary", "arbitrary"),
                                             vmem_limit_bytes=VMEM_LIMIT_BYTES),
        name="mlstm",
    )(pd, *consts)


def kernel(x, c, ada_w, ada_b, norm_g, ffn1_w13, ffn1_w2, ffn2_w13, ffn2_w2, w_in, b_in, q_norm_g, k_norm_g, cmp_pos, cmp_k_w1, cmp_k_w2, cmp_v_w1, cmp_v_w2, t5_table, sc_conv_w, gdn_conv_w, gdn_A_log, gdn_dt_bias, gdn_norm_g, mlstm_f_bias, mlstm_norm_g, mix_norm_g, w_out):
    B, T, D = x.shape
    depth = ada_w.shape[0]
    bf16 = jnp.bfloat16
    x2 = x.reshape(B * T, D)
    tables = nsa_bias_tables(t5_table, T)
    ovt = nsa_overlap_t(T)
    for l in range(depth):
        mod = ada_modulation(c, ada_w[l].astype(bf16), ada_b[l][None, :])
        mod = mod.reshape(B, N_SUBLAYERS, 3, 1, D)
        x2 = ffn_half_step(x2, norm_g[l, 0][None, :], mod[:, 0, 0], mod[:, 0, 1], mod[:, 0, 2],
                           ffn1_w13[l].astype(bf16), ffn1_w2[l].astype(bf16), T)
        ws, bs = split_in_weights(w_in[l], b_in[l])
        pa, pb, pc, pd = in_projection(x2, norm_g[l, 1][None, :], mod[:, 1, 0], mod[:, 1, 1], ws, bs, T)
        pa, pb, pc, pd = (a.reshape(B, T, -1) for a in (pa, pb, pc, pd))
        y_a = nsa_mixer_pallas(pa, q_norm_g[l], k_norm_g[l], cmp_pos[l], cmp_k_w1[l], cmp_k_w2[l],
                               cmp_v_w1[l], cmp_v_w2[l], tables, ovt, mix_norm_g[l, 0][None, :])
        y_b = short_conv_pallas(pb, sc_conv_w[l], mix_norm_g[l, 1][None, :])
        y_c = gated_deltanet_pallas(pc, gdn_conv_w[l], gdn_A_log[l], gdn_dt_bias[l], gdn_norm_g[l])
        y_d = mlstm_pallas(pd, mlstm_f_bias[l], mlstm_norm_g[l])
        ys = [y.reshape(B * T, GROUP_WIDTH) for y in (y_a, y_b, y_c, y_d)]
        x2 = out_projection(x2, ys, mod[:, 1, 2], w_out[l].astype(bf16), T)
        x2 = ffn_half_step(x2, norm_g[l, 2][None, :], mod[:, 2, 0], mod[:, 2, 1], mod[:, 2, 2],
                           ffn2_w13[l].astype(bf16), ffn2_w2[l].astype(bf16), T)
    return x2.reshape(B, T, D)
```

```python
import functools
import math

import jax
import jax.numpy as jnp
import numpy as np
from jax import lax
from jax.experimental import pallas as pl
from jax.experimental.pallas import tpu as pltpu

D_MODEL = 1024
HEAD_DIM = 64
GROUP_HEADS = 4
GROUP_WIDTH = GROUP_HEADS * HEAD_DIM
CMP_STRIDE = 16
CMP_BLOCK = 32
SLC_BLOCK = 64
N_SLC = 16
N_LOCAL_SLC = 2
WINDOW = 512
Q_BLOCK = 128
FORCE = 1e6
N_BUCKETS = 32
MAX_DISTANCE = 128
GDN_CHUNK = 64
MLSTM_CHUNK = 64
D_FF = 2816
N_SUBLAYERS = 3
EPS = 1e-6

IN_LAYOUT = (
    ("a_q", 256), ("a_k_cmp", 64), ("a_v_cmp", 64),
    ("a_k_slc", 64), ("a_v_slc", 64), ("a_k_win", 64), ("a_v_win", 64),
    ("a_gate", 12),
    ("b_b", 256), ("b_c", 256), ("b_x", 256),
    ("c_q", 256), ("c_k", 256), ("c_v", 256),
    ("c_beta", 4), ("c_alpha", 4), ("c_z", 256),
    ("d_q", 256), ("d_k", 256), ("d_v", 256),
    ("d_i", 4), ("d_f", 4), ("d_o", 256),
)

VMEM_LIMIT_BYTES = 56 * 1024 * 1024
FFN_TOKEN_TILE = 512
FFN_CHUNK = 256
PROJ_TOKEN_TILE = 512
NEG = -1e30
HIGHEST = lax.Precision.HIGHEST
LOG2E = math.log2(math.e)
V1_ROWS = HEAD_DIM + 16


def _modulated_norm(x, g, scale, shift):
    y = x * lax.rsqrt(jnp.mean(x * x, axis=-1, keepdims=True) + EPS)
    return (y * g) * (1.0 + scale) + shift


def _ada_kernel(c_ref, w_ref, b_ref, o_ref):
    c = c_ref[...]
    cond = c * jax.nn.sigmoid(c)
    o_ref[...] = jnp.dot(cond.astype(jnp.bfloat16), w_ref[...],
                         preferred_element_type=jnp.float32) + b_ref[...]


def ada_modulation(c, w, b):
    B, D = c.shape
    N = w.shape[1]
    tn = 1152
    return pl.pallas_call(
        _ada_kernel,
        grid=(N // tn,),
        in_specs=[pl.BlockSpec((B, D), lambda j: (0, 0)),
                  pl.BlockSpec((D, tn), lambda j: (0, j)),
                  pl.BlockSpec((1, tn), lambda j: (0, j))],
        out_specs=pl.BlockSpec((B, tn), lambda j: (0, j)),
        out_shape=jax.ShapeDtypeStruct((B, N), jnp.float32),
        name="ada_modulation",
    )(c, w, b)


def _ffn_kernel(x_ref, g_ref, shift_ref, scale_ref, gate_ref, w1_ref, w3_ref, w2_ref, o_ref, act_ref):
    x = x_ref[...]
    h = _modulated_norm(x, g_ref[...], scale_ref[0], shift_ref[0]).astype(jnp.bfloat16)
    n_chunks = act_ref.shape[1] // FFN_CHUNK
    for ci in range(n_chunks):
        cs = slice(ci * FFN_CHUNK, (ci + 1) * FFN_CHUNK)
        a = jnp.dot(h, w1_ref[:, cs], preferred_element_type=jnp.float32)
        b = jnp.dot(h, w3_ref[:, cs], preferred_element_type=jnp.float32)
        act_ref[:, cs] = (a * jax.nn.sigmoid(a) * b).astype(jnp.bfloat16)
    y = jnp.dot(act_ref[...], w2_ref[...], preferred_element_type=jnp.float32)
    o_ref[...] = x + (0.5 * gate_ref[0]) * y


def ffn_half_step(x2, g, shift, scale, gate, w13, w2, tokens_per_batch):
    M, D = x2.shape
    F = w2.shape[0]
    tm = FFN_TOKEN_TILE
    tiles_per_batch = tokens_per_batch // tm
    resident = dict(pipeline_mode=pl.Buffered(1))
    mod_spec = pl.BlockSpec((1, 1, D), lambda i: (i // tiles_per_batch, 0, 0))
    return pl.pallas_call(
        _ffn_kernel,
        grid=(M // tm,),
        in_specs=[pl.BlockSpec((tm, D), lambda i: (i, 0)),
                  pl.BlockSpec((1, D), lambda i: (0, 0)),
                  mod_spec, mod_spec, mod_spec,
                  pl.BlockSpec((D, F), lambda i: (0, 0), **resident),
                  pl.BlockSpec((D, F), lambda i: (0, 1), **resident),
                  pl.BlockSpec((F, D), lambda i: (0, 0), **resident)],
        out_specs=pl.BlockSpec((tm, D), lambda i: (i, 0)),
        out_shape=jax.ShapeDtypeStruct((M, D), jnp.float32),
        scratch_shapes=[pltpu.VMEM((tm, F), jnp.bfloat16)],
        compiler_params=pltpu.CompilerParams(dimension_semantics=("arbitrary",),
                                             vmem_limit_bytes=VMEM_LIMIT_BYTES),
        name="ffn_half_step",
    )(x2, g, shift, scale, gate, w13, w13, w2)


CONV_CARRY_ROWS = 8
CONV_ROW_CHUNK = 128


def _in_proj_kernel(x_ref, g_ref, shift_ref, scale_ref, wa_ref, wb_ref, wc_ref, wd_ref,
                    ba_ref, bb_ref, bc_ref, bd_ref, scw_ref, scg_ref, gcw_ref,
                    oa_ref, yb_ref, oc_ref, od_ref, braw_ref, craw_ref, *, tiles_per_batch):
    f32 = jnp.float32
    rows = x_ref.shape[0]
    hist = CONV_CARRY_ROWS
    first = pl.program_id(0) % tiles_per_batch == 0

    @pl.when(first)
    def _():
        braw_ref[0:hist, :] = jnp.zeros((hist, braw_ref.shape[1]), f32)
        craw_ref[0:hist, :] = jnp.zeros((hist, craw_ref.shape[1]), f32)

    @pl.when(jnp.logical_not(first))
    def _():
        braw_ref[0:hist, :] = braw_ref[rows:rows + hist, :]
        craw_ref[0:hist, :] = craw_ref[rows:rows + hist, :]

    h = _modulated_norm(x_ref[...], g_ref[...], scale_ref[0], shift_ref[0]).astype(jnp.bfloat16)
    project = lambda w_ref, b_ref: jnp.dot(h, w_ref[...], preferred_element_type=f32) + b_ref[...]
    pb = project(wb_ref, bb_ref)
    yb_ref[...] = pb[:, 0:256]
    braw_ref[hist:hist + rows, :] = pb[:, 256:512] * pb[:, 512:768]
    pc = project(wc_ref, bc_ref)
    craw_ref[hist:hist + rows, :] = pc[:, 0:768]
    oc_ref[:, 768:] = pc[:, 768:]
    n_chunks = rows // CONV_ROW_CHUNK
    pieces = []
    for w_ref, b_ref, o_ref in ((wa_ref, ba_ref, oa_ref), (wd_ref, bd_ref, od_ref)):
        half = w_ref.shape[1] // 256 // 2 * 256
        pieces += [(w_ref, b_ref, o_ref, 0, half), (w_ref, b_ref, o_ref, half, w_ref.shape[1])]
    for ci, r0 in enumerate(range(0, rows, CONV_ROW_CHUNK)):
        for w_ref, b_ref, o_ref, c0, c1 in pieces[ci * len(pieces) // n_chunks:(ci + 1) * len(pieces) // n_chunks]:
            o_ref[:, c0:c1] = jnp.dot(h, w_ref[:, c0:c1], preferred_element_type=f32) + b_ref[:, c0:c1]
        out_rows = slice(r0, r0 + CONV_ROW_CHUNK)
        tap = lambda ref, s: ref[pl.ds(hist + r0 - s, CONV_ROW_CHUNK), :]
        conv = sum(tap(braw_ref, s) * scw_ref[2 - s:3 - s, :] for s in range(3))
        y = yb_ref[out_rows, :] * conv
        yb_ref[out_rows, :] = y * lax.rsqrt(jnp.mean(y * y, axis=-1, keepdims=True) + EPS) * scg_ref[...]
        for c0 in range(0, 3 * GROUP_WIDTH, GROUP_WIDTH):
            cols = slice(c0, c0 + GROUP_WIDTH)
            conv = sum(craw_ref[pl.ds(hist + r0 - s, CONV_ROW_CHUNK), cols] * gcw_ref[3 - s:4 - s, cols]
                       for s in range(4))
            oc_ref[out_rows, cols] = conv * jax.nn.sigmoid(conv)


def in_projection(x2, g, shift, scale, ws, bs, sc_conv_w, sc_norm_g, gdn_conv_w, tokens_per_batch):
    M, D = x2.shape
    tm = PROJ_TOKEN_TILE
    tiles_per_batch = tokens_per_batch // tm
    mod_spec = pl.BlockSpec((1, 1, D), lambda i: (i // tiles_per_batch, 0, 0))
    const = lambda a, **kw: pl.BlockSpec(a.shape, lambda i: (0, 0), **kw)
    pad8 = lambda w: jnp.pad(w, ((0, 8 - w.shape[0]), (0, 0)))
    extras = (pad8(sc_conv_w), sc_norm_g, pad8(gdn_conv_w))
    widths = (ws[0].shape[1], GROUP_WIDTH, ws[2].shape[1], ws[3].shape[1])
    return pl.pallas_call(
        functools.partial(_in_proj_kernel, tiles_per_batch=tiles_per_batch),
        grid=(M // tm,),
        in_specs=[pl.BlockSpec((tm, D), lambda i: (i, 0)), const(g), mod_spec, mod_spec]
                 + [const(w, pipeline_mode=pl.Buffered(1)) for w in ws] + [const(b) for b in bs]
                 + [const(e) for e in extras],
        out_specs=[pl.BlockSpec((tm, n), lambda i: (i, 0)) for n in widths],
        out_shape=[jax.ShapeDtypeStruct((M, n), jnp.float32) for n in widths],
        scratch_shapes=[pltpu.VMEM((CONV_CARRY_ROWS + tm, GROUP_WIDTH), jnp.float32),
                        pltpu.VMEM((CONV_CARRY_ROWS + tm, 3 * GROUP_WIDTH), jnp.float32)],
        compiler_params=pltpu.CompilerParams(dimension_semantics=("arbitrary",),
                                             vmem_limit_bytes=VMEM_LIMIT_BYTES),
        name="in_projection",
    )(x2, g, shift, scale, *ws, *bs, *extras)


def _group_columns():
    offs, o = {}, 0
    for n, s in IN_LAYOUT:
        offs[n] = np.arange(o, o + s)
        o += s
    cat = lambda names: np.concatenate([offs[n] for n in names])
    return (cat(["a_q", "a_k_cmp", "a_v_cmp", "a_k_slc", "a_v_slc", "a_k_win", "a_v_win", "a_gate"]),
            cat(["b_b", "b_c", "b_x"]),
            cat(["c_q", "c_k", "c_v", "c_z", "c_beta", "c_alpha"]),
            cat(["d_q", "d_k", "d_v", "d_o", "d_i", "d_f"]))


GROUP_SLAB_WIDTH = (768, 768, 1152, 1152)


def _take_columns(a, idx, width):
    cuts = [0] + [k + 1 for k in range(idx.size - 1) if idx[k + 1] != idx[k] + 1] + [idx.size]
    runs = [a[..., int(idx[s]):int(idx[e - 1]) + 1] for s, e in zip(cuts[:-1], cuts[1:])]
    return jnp.concatenate(runs + [jnp.zeros(a.shape[:-1] + (width - idx.size,), a.dtype)], axis=-1)


def split_in_weights(w, b):
    ws, bs = [], []
    for idx, width in zip(_group_columns(), GROUP_SLAB_WIDTH):
        ws.append(_take_columns(w, idx, width).astype(jnp.bfloat16))
        bs.append(_take_columns(b, idx, width)[None, :])
    return ws, bs


def _out_proj_kernel(x_ref, ya_ref, yb_ref, yc_ref, yd_ref, gate_ref, w_ref, o_ref):
    z = None
    for g, y_ref in enumerate((ya_ref, yb_ref, yc_ref, yd_ref)):
        part = jnp.dot(y_ref[...].astype(jnp.bfloat16), w_ref[g * GROUP_WIDTH:(g + 1) * GROUP_WIDTH, :],
                       preferred_element_type=jnp.float32)
        z = part if z is None else z + part
    o_ref[...] = x_ref[...] + gate_ref[0] * z


def out_projection(x2, ys, gate, w, tokens_per_batch):
    M, D = x2.shape
    tm = PROJ_TOKEN_TILE
    tiles_per_batch = tokens_per_batch // tm
    return pl.pallas_call(
        _out_proj_kernel,
        grid=(M // tm,),
        in_specs=[pl.BlockSpec((tm, D), lambda i: (i, 0))]
                 + [pl.BlockSpec((tm, GROUP_WIDTH), lambda i: (i, 0)) for _ in ys]
                 + [pl.BlockSpec((1, 1, D), lambda i: (i // tiles_per_batch, 0, 0)),
                    pl.BlockSpec(w.shape, lambda i: (0, 0), pipeline_mode=pl.Buffered(1))],
        out_specs=pl.BlockSpec((tm, D), lambda i: (i, 0)),
        out_shape=jax.ShapeDtypeStruct((M, D), jnp.float32),
        compiler_params=pltpu.CompilerParams(dimension_semantics=("arbitrary",),
                                             vmem_limit_bytes=VMEM_LIMIT_BYTES),
        name="out_projection",
    )(x2, *ys, gate, w)


def t5_bucket(dist):
    n = jnp.maximum(dist, 0)
    max_exact = N_BUCKETS // 2
    nf = jnp.maximum(n, 1).astype(jnp.float32)
    large = max_exact + (jnp.log(nf / max_exact) / math.log(MAX_DISTANCE / max_exact)
                         * (N_BUCKETS - max_exact)).astype(jnp.int32)
    large = jnp.minimum(large, N_BUCKETS - 1)
    return jnp.where(n < max_exact, n, large)


def _nt_dot(a, b):
    return lax.dot_general(a, b, (((1,), (1,)), ((), ())), preferred_element_type=jnp.float32)


def _stack_heads(qn):
    lane = lax.broadcasted_iota(jnp.int32, (qn.shape[0], 128), 1)
    lo = lane < HEAD_DIM
    zero = jnp.zeros((), qn.dtype)
    halves = (qn[:, 0:128], qn[:, 128:256])
    return jnp.concatenate([jnp.where(lo if h % 2 == 0 else ~lo, halves[h // 2], zero)
                            for h in range(GROUP_HEADS)], axis=0)


def _nsa_prep_kernel(q_ref, kvs_ref, kvw_ref, qg_ref, kg_ref, seg_ref,
                     qn_ref, qnt_ref, kxs_ref, v1s_ref, kkw_ref, v1w_ref):
    tm = q_ref.shape[0]
    q = q_ref[...]
    ss = jnp.dot(q * q, seg_ref[...], precision=HIGHEST, preferred_element_type=jnp.float32)
    qn = q * lax.rsqrt(ss * (1.0 / HEAD_DIM) + EPS) * qg_ref[...]
    qn_ref[...] = qn.astype(qn_ref.dtype)
    qnt_ref[...] = (qn * LOG2E).T.astype(qnt_ref.dtype)
    lane = lax.broadcasted_iota(jnp.int32, (tm, 128), 1)
    lo = lane < HEAD_DIM

    def split(x):
        xr = pltpu.roll(x, HEAD_DIM, 1)
        ss = jnp.sum(jnp.where(lo, x * x, 0.0), axis=-1, keepdims=True)
        kk = jnp.where(lo, x, xr) * lax.rsqrt(ss * (1.0 / HEAD_DIM) + EPS) * kg_ref[...]
        return kk, jnp.where(lo, xr, 1.0)

    kk, v1 = split(kvs_ref[...])
    tok = pl.program_id(1) * tm + lax.broadcasted_iota(jnp.int32, (tm, 128), 0)
    onehot = jnp.where(lane == tok // SLC_BLOCK, 1.0, 0.0)
    kxs_ref[...] = jnp.concatenate([kk, onehot], axis=1).astype(kxs_ref.dtype)
    v1s_ref[...] = v1.T[0:V1_ROWS].astype(v1s_ref.dtype)
    kk, v1 = split(kvw_ref[...])
    kkw_ref[...] = kk.astype(kkw_ref.dtype)
    v1w_ref[...] = v1.T[0:V1_ROWS].astype(v1w_ref.dtype)


def nsa_prep(pa, qg, kg2, seg):
    B, T, _ = pa.shape
    tm = 512
    bf16 = jnp.bfloat16
    col = lambda w, j: pl.BlockSpec((None, tm, w), lambda b, i: (b, i, j))
    rowblk = lambda h: pl.BlockSpec((None, h, tm), lambda b, i: (b, 0, i))
    const = lambda shape: pl.BlockSpec(shape, lambda b, i: (0, 0))
    return pl.pallas_call(
        _nsa_prep_kernel,
        grid=(B, T // tm),
        in_specs=[col(256, 0), col(128, 3), col(128, 4), const((1, 256)), const((1, 128)), const((256, 256))],
        out_specs=[col(256, 0), rowblk(256), col(256, 0), rowblk(V1_ROWS), col(128, 0), rowblk(V1_ROWS)],
        out_shape=[jax.ShapeDtypeStruct((B, T, 256), bf16), jax.ShapeDtypeStruct((B, 256, T), bf16),
                   jax.ShapeDtypeStruct((B, T, 256), bf16), jax.ShapeDtypeStruct((B, V1_ROWS, T), bf16),
                   jax.ShapeDtypeStruct((B, T, 128), bf16), jax.ShapeDtypeStruct((B, V1_ROWS, T), bf16)],
        name="nsa_prep",
    )(pa, pa, pa, qg, kg2, seg)


def _compress_kernel(kch_ref, vch_ref, plo_ref, phi_ref, w1k_ref, w2k_ref, w1v_ref, w2v_ref, kg_ref,
                     kkc_ref, vvc_ref):
    nrow = kch_ref.shape[0]
    half = CMP_STRIDE * HEAD_DIM
    bf16 = jnp.bfloat16
    valid = lax.broadcasted_iota(jnp.int32, (nrow, 128), 0) < nrow - 1

    def comp(ch, w1_ref, w2_ref):
        a = jnp.dot((ch + plo_ref[...]).astype(bf16), w1_ref[0:half, :], preferred_element_type=jnp.float32)
        b = jnp.dot((ch + phi_ref[...]).astype(bf16), w1_ref[half:2 * half, :],
                    preferred_element_type=jnp.float32)
        hid = a + pltpu.roll(b, nrow - 1, 0)
        hid = hid * jax.nn.sigmoid(hid)
        return jnp.dot(hid.astype(bf16), w2_ref[...], preferred_element_type=jnp.float32)

    kc = comp(kch_ref[...], w1k_ref, w2k_ref)
    kc = kc * lax.rsqrt(jnp.mean(kc * kc, axis=-1, keepdims=True) + EPS) * kg_ref[...]
    kkc_ref[...] = jnp.where(valid, kc, 0.0).astype(kkc_ref.dtype)
    vc = comp(vch_ref[...], w1v_ref, w2v_ref)
    vvc_ref[...] = jnp.where(valid, vc, 0.0).astype(vvc_ref.dtype)


def nsa_compress(kch, vch, plo, phi, w1k, w2k, w1v, w2v, kg2):
    B, nrow, width = kch.shape
    bf16 = jnp.bfloat16
    full = lambda a: pl.BlockSpec(a.shape, lambda b: (0,) * a.ndim)
    bspec = pl.BlockSpec((None, nrow, width), lambda b: (b, 0, 0))
    ospec = pl.BlockSpec((None, nrow, 128), lambda b: (b, 0, 0))
    return pl.pallas_call(
        _compress_kernel,
        grid=(B,),
        in_specs=[bspec, bspec, full(plo), full(phi), full(w1k), full(w2k), full(w1v), full(w2v), full(kg2)],
        out_specs=[ospec, ospec],
        out_shape=[jax.ShapeDtypeStruct((B, nrow, 128), bf16), jax.ShapeDtypeStruct((B, nrow, 128), bf16)],
        name="nsa_compress",
    )(kch, vch, plo, phi, w1k, w2k, w1v, w2v, kg2)


CMP_TILES_PER_STEP = 2


def _cmp_select_kernel(qn_ref, kkc_ref, vvc_ref, cb_ref, ovt_ref, ocmp_ref, pen_ref, sc_ref, *, n_sel):
    i = pl.program_id(1)
    n_tiles = cb_ref.shape[0]
    tq = qn_ref.shape[0] // n_tiles
    nblk = ovt_ref.shape[0]
    bf16 = jnp.bfloat16
    lo = lax.broadcasted_iota(jnp.int32, (tq, 128), 1) < HEAD_DIM
    blk = lax.broadcasted_iota(jnp.int32, (nblk, tq), 0)
    scores = []
    for g in range(n_tiles):
        rows = slice(g * tq, (g + 1) * tq)
        q4 = _stack_heads(qn_ref[rows, :])
        s = _nt_dot(q4, kkc_ref[...]) + cb_ref[g]
        m = jnp.max(s, axis=-1, keepdims=True)
        e = jnp.where(s > 0.5 * NEG, jnp.exp(s - m), 0.0)
        p = e * (1.0 / jnp.maximum(jnp.sum(e, axis=-1, keepdims=True), 1e-30))
        o = jnp.dot(p.astype(bf16), vvc_ref[...], preferred_element_type=jnp.float32)
        ocmp_ref[rows, 0:128] = jnp.where(lo, o[0:tq], o[tq:2 * tq])
        ocmp_ref[rows, 128:256] = jnp.where(lo, o[2 * tq:3 * tq], o[3 * tq:4 * tq])
        psum = p[0:tq] + p[tq:2 * tq] + p[2 * tq:3 * tq] + p[3 * tq:4 * tq]
        hi = psum.astype(bf16)
        lo_part = (psum - hi.astype(jnp.float32)).astype(bf16)
        score = _nt_dot(ovt_ref[...], hi) + _nt_dot(ovt_ref[...], lo_part)
        qpos = (i * n_tiles + g) * tq + lax.broadcasted_iota(jnp.int32, (nblk, tq), 1)
        cur = qpos // SLC_BLOCK
        forced = (blk == 0) | ((cur - blk >= 0) & (cur - blk < N_LOCAL_SLC))
        score = jnp.where(forced, FORCE, score)
        score = jnp.where(blk <= cur, score, -FORCE)
        sc_ref[g] = score
        scores.append(score)
    blocks_per_tile = tq // SLC_BLOCK

    def rank_step(t, cnts):
        out = []
        for g in range(n_tiles):
            cnt = cnts[g]
            for u in range(blocks_per_tile):
                jp = t * blocks_per_tile + u
                row = sc_ref[g, pl.ds(jp, 1), :]
                tie = jnp.where(blk > jp, 1.0, 0.0)
                cnt = cnt + jnp.where(row > scores[g], 1.0, jnp.where(row == scores[g], tie, 0.0))
            out.append(cnt)
        return tuple(out)

    zeros = tuple(jnp.zeros((nblk, tq), jnp.float32) for _ in range(n_tiles))
    cnts = lax.fori_loop(0, (i + 1) * n_tiles, rank_step, zeros)
    for g in range(n_tiles):
        pen = jnp.where(cnts[g] < n_sel, 0.0, NEG)
        if nblk < 128:
            pen = jnp.concatenate([pen, jnp.zeros((128 - nblk, tq), jnp.float32)], axis=0)
        pen_ref[g] = pen.astype(pen_ref.dtype)


def nsa_cmp_select(qn, kkc, vvc, cbias, ovt, n_sel):
    B, T, _ = qn.shape
    nrow = kkc.shape[1]
    nblk = ovt.shape[0]
    tq = Q_BLOCK
    g = CMP_TILES_PER_STEP
    return pl.pallas_call(
        functools.partial(_cmp_select_kernel, n_sel=n_sel),
        grid=(B, T // (g * tq)),
        in_specs=[pl.BlockSpec((None, g * tq, 256), lambda b, i: (b, i, 0)),
                  pl.BlockSpec((None, nrow, 128), lambda b, i: (b, 0, 0)),
                  pl.BlockSpec((None, nrow, 128), lambda b, i: (b, 0, 0)),
                  pl.BlockSpec((g, 4 * tq, nrow), lambda b, i: (i, 0, 0)),
                  pl.BlockSpec((nblk, nrow), lambda b, i: (0, 0))],
        out_specs=[pl.BlockSpec((None, g * tq, 256), lambda b, i: (b, i, 0)),
                   pl.BlockSpec((None, g, 128, tq), lambda b, i: (b, i, 0, 0))],
        out_shape=[jax.ShapeDtypeStruct((B, T, 256), jnp.float32),
                   jax.ShapeDtypeStruct((B, T // tq, 128, tq), jnp.bfloat16)],
        scratch_shapes=[pltpu.VMEM((g, nblk, tq), jnp.float32)],
        name="nsa_cmp_select",
    )(qn, kkc, vvc, cbias, ovt)


NSA_KEY_TILE = 256
NSA_QUERY_TILE = 256


def _nsa_attn_kernel(qnt_ref, pen_ref, ocmp_ref, gate_ref, kxs_ref, v1s_ref, kkw_ref, v1w_ref,
                     tabs_ref, tabw_ref, mixg_ref, o_ref, acc_ref, m_ref, sa_ref, sb_ref):
    i = pl.program_id(1)
    tq = o_ref.shape[0]
    kt = tabs_ref.shape[1]
    f32, bf16 = jnp.float32, jnp.bfloat16
    qt = qnt_ref[...]
    lo = lax.broadcasted_iota(jnp.int32, (128, tq), 0) < HEAD_DIM
    zero = jnp.zeros((), qt.dtype)
    q4t = jnp.concatenate([jnp.where(lo if h % 2 == 0 else ~lo, qt[128 * (h // 2):128 * (h // 2) + 128], zero)
                           for h in range(GROUP_HEADS)], axis=1)
    pen = jnp.concatenate([pen_ref[t] for t in range(pen_ref.shape[0])], axis=1)
    rhs_sel = jnp.concatenate([q4t, jnp.concatenate([pen] * GROUP_HEADS, axis=1)], axis=0)

    def branch(rhs, k_ref, v_ref, tab_ref, first_tile, last_tile):
        n_tab = tab_ref.shape[0] - 1
        m_ref[...] = jnp.full(m_ref.shape, NEG, f32)
        acc_ref[...] = jnp.zeros(acc_ref.shape, f32)

        def key_offset(j):
            return pl.multiple_of(jnp.minimum(j, last_tile) * kt, kt)

        def scores(j):
            delta = jnp.where(j > last_tile, n_tab, jnp.minimum(i - j * (kt // tq), n_tab - 1))
            return jnp.dot(k_ref[pl.ds(key_offset(j), kt), :], rhs, preferred_element_type=f32) + tab_ref[delta]

        def absorb(s, j):
            m_old = m_ref[...]
            m_new = jnp.maximum(m_old, jnp.max(s, axis=0, keepdims=True))
            p = jnp.exp2(s - m_new)
            pv =jnp.dot(v_ref[:, pl.ds(key_offset(j), kt)], p.astype(bf16), preferred_element_type=f32)
            acc_ref[...] = jnp.exp2(m_old - m_new) * acc_ref[...] + pv
            m_ref[...] = m_new

        sa_ref[...] = scores(first_tile)

        def body(t, carry):
            j = first_tile + 2 * t
            sb_ref[...] = scores(j + 1)
            absorb(sa_ref[...], j)
            sa_ref[...] = scores(j + 2)
            absorb(sb_ref[...], j + 1)
            return carry

        lax.fori_loop(0, (last_tile - first_tile) // 2 + 1, body, 0)
        acc = acc_ref[...]
        return acc[0:HEAD_DIM] / jnp.maximum(acc[HEAD_DIM:HEAD_DIM + 1], 1e-30)

    def window_branch(rhs):
        n_tab = tabw_ref.shape[0] - 1
        m = jnp.full((1, rhs.shape[1]), NEG, f32)
        acc = jnp.zeros((V1_ROWS, rhs.shape[1]), f32)
        for d in range(n_tab):
            j = i - d
            off = pl.multiple_of(jnp.maximum(j, 0) * kt, kt)
            s = jnp.dot(kkw_ref[pl.ds(off, kt), :], rhs, preferred_element_type=f32)
            s = s + tabw_ref[jnp.where(j < 0, n_tab, d)]
            m_new = jnp.maximum(m, jnp.max(s, axis=0, keepdims=True))
            p = jnp.exp2(s - m_new)
            acc = jnp.exp2(m - m_new) * acc +jnp.dot(v1w_ref[:, pl.ds(off, kt)], p.astype(bf16),
                                                     preferred_element_type=f32)
            m = m_new
        return acc[0:HEAD_DIM] / jnp.maximum(acc[HEAD_DIM:HEAD_DIM + 1], 1e-30)

    last = (i * tq) // kt
    if kt == tq:
        o_w = window_branch(q4t)
    else:
        o_w = branch(q4t, kkw_ref, v1w_ref, tabw_ref, (jnp.maximum(i - WINDOW // tq, 0) * tq) // kt, last)
    o_s = branch(rhs_sel, kxs_ref, v1s_ref, tabs_ref, 0, last)
    gt = jax.nn.sigmoid(gate_ref[...]).T
    oct_ = ocmp_ref[...].T
    parts = []
    for h in range(GROUP_HEADS):
        cols = slice(h * tq, (h + 1) * tq)
        parts.append(gt[3 * h:3 * h + 1] * oct_[h * HEAD_DIM:(h + 1) * HEAD_DIM]
                     + gt[3 * h + 1:3 * h + 2] * o_s[:, cols] + gt[3 * h + 2:3 * h + 3] * o_w[:, cols])
    yt = jnp.concatenate(parts, axis=0)
    yt = yt * lax.rsqrt(jnp.mean(yt * yt, axis=0, keepdims=True) + EPS) * mixg_ref[...]
    o_ref[...] = yt.T


def nsa_attention(qnt, pen, ocmp, pa, kxs, v1s, kkw, v1w, tab_s, tab_w, mixg):
    B, _, T = qnt.shape
    tq = min(NSA_QUERY_TILE, T)
    tile = lambda w, j: pl.BlockSpec((None, tq, w), lambda b, i: (b, i, j))
    whole = lambda w: pl.BlockSpec((None, T, w), lambda b, i: (b, 0, 0))
    whole_t = pl.BlockSpec((None, V1_ROWS, T), lambda b, i: (b, 0, 0))
    full = lambda a: pl.BlockSpec(a.shape, lambda b, i: (0,) * a.ndim)
    return pl.pallas_call(
        _nsa_attn_kernel,
        grid=(B, T // tq),
        in_specs=[pl.BlockSpec((None, 256, tq), lambda b, i: (b, 0, i)),
                  pl.BlockSpec((None, tq // Q_BLOCK, 128, Q_BLOCK), lambda b, i: (b, i, 0, 0)),
                  tile(256, 0), tile(128, 5),
                  whole(256), whole_t, whole(128), whole_t, full(tab_s), full(tab_w), full(mixg)],
        out_specs=tile(256, 0),
        out_shape=jax.ShapeDtypeStruct((B, T, 256), jnp.float32),
        scratch_shapes=[pltpu.VMEM((V1_ROWS, GROUP_HEADS * tq), jnp.float32),
                        pltpu.VMEM((1, GROUP_HEADS * tq), jnp.float32),
                        pltpu.VMEM((tab_s.shape[1], GROUP_HEADS * tq), jnp.float32),
                        pltpu.VMEM((tab_s.shape[1], GROUP_HEADS * tq), jnp.float32)],
        compiler_params=pltpu.CompilerParams(dimension_semantics=("arbitrary", "arbitrary"),
                                             vmem_limit_bytes=VMEM_LIMIT_BYTES),
        name="nsa_attention",
    )(qnt, pen, ocmp, pa, kxs, v1s, kkw, v1w, tab_s, tab_w, mixg)


def _toeplitz(w, rows, cols):
    H, n = w.shape
    period = n + 1
    flat = jnp.tile(jnp.pad(w, ((0, 0), (0, 1))), (1, rows + 1))[:, :rows * (period + 1)]
    return jnp.flip(flat.reshape(H, rows, period + 1)[:, :, :cols], axis=2)


def nsa_bias_tables(t5_table, T):
    H = GROUP_HEADS
    tq = Q_BLOCK
    ta = min(NSA_QUERY_TILE, T)
    kt = min(NSA_KEY_TILE, T)
    nrow = T // CMP_STRIDE
    dmax = WINDOW + 2 * ta + kt
    onehot = (t5_bucket(jnp.arange(dmax))[:, None] == jnp.arange(N_BUCKETS)[None, :]).astype(jnp.float32)
    by_dist = jnp.dot(onehot, t5_table, precision=HIGHEST).T
    neg = lambda n: jnp.full((H, n), NEG, jnp.float32)
    f_s = jnp.concatenate([neg(kt - 1), by_dist * LOG2E], axis=1)
    f_w = jnp.concatenate([neg(kt - 1), by_dist[:, :WINDOW] * LOG2E, neg(dmax - WINDOW)], axis=1)

    def tiles(f, n):
        out = [_toeplitz(f[:, k * ta:k * ta + ta + kt - 1], ta, kt).reshape(H * ta, kt).T for k in range(n)]
        return jnp.stack(out + [jnp.full((kt, H * ta), NEG, jnp.float32)])

    n_s = -(-(kt + MAX_DISTANCE) // ta) + 1
    n_w = WINDOW // ta + kt // ta
    tab_s, tab_w = tiles(f_s, n_s), tiles(f_w, n_w)
    per_tile = tq // CMP_STRIDE
    d0 = -CMP_STRIDE * (nrow - 1) - (CMP_BLOCK - 1)
    n16 = 2 * nrow + per_tile - 1
    n_pos = CMP_STRIDE * n16 + d0
    far = jnp.broadcast_to(by_dist[:, -1:], (H, max(n_pos - dmax, 0)))
    w16 = jnp.concatenate([neg(-d0), by_dist[:, :n_pos], far], axis=1).reshape(H, n16, CMP_STRIDE)
    base = jnp.stack([w16[:, a:a + 2 * nrow, :] for a in range(per_tile)], axis=1)
    base = jnp.flip(base.transpose(0, 1, 3, 2).reshape(H * tq, 2 * nrow), axis=1)
    cbias = jnp.stack([base[:, nrow - per_tile * i: 2 * nrow - per_tile * i] for i in range(T // tq)])
    return tab_s, tab_w, cbias


def nsa_overlap_t(T):
    nrow = T // CMP_STRIDE
    nblk = T // SLC_BLOCK
    ci = np.arange(nrow)[None, :]
    bj = np.arange(nblk)[:, None]
    ov = (ci * CMP_STRIDE < (bj + 1) * SLC_BLOCK) & (ci * CMP_STRIDE + CMP_BLOCK > bj * SLC_BLOCK)
    ov = ov & (ci < nrow - 1)
    return jnp.asarray(ov.astype(np.float32), dtype=jnp.bfloat16)


def nsa_mixer_pallas(pa, q_norm_g, k_norm_g, cmp_pos, cmp_k_w1, cmp_k_w2, cmp_v_w1, cmp_v_w2,
                     tables, ovt, mixg):
    B, T, _ = pa.shape
    bf16 = jnp.bfloat16
    tab_s, tab_w, cbias = tables
    qg = (jnp.tile(q_norm_g, GROUP_HEADS) * HEAD_DIM ** -0.5)[None, :]
    kg2 = jnp.tile(k_norm_g, 2)[None, :]
    qn, qnt, kxs, v1s, kkw, v1w = nsa_prep(pa, qg, kg2, _head_mask())
    nrow = T // CMP_STRIDE
    kch = pa[:, :, 256:320].reshape(B, nrow, CMP_STRIDE * HEAD_DIM)
    vch = pa[:, :, 320:384].reshape(B, nrow, CMP_STRIDE * HEAD_DIM)
    plo = cmp_pos[:CMP_STRIDE].reshape(1, -1)
    phi = cmp_pos[CMP_STRIDE:].reshape(1, -1)
    dup = lambda w: jnp.concatenate([w, w], axis=1).astype(bf16)
    kkc, vvc = nsa_compress(kch, vch, plo, phi, cmp_k_w1.astype(bf16), dup(cmp_k_w2),
                            cmp_v_w1.astype(bf16), dup(cmp_v_w2), kg2)
    n_sel = min(N_SLC, T // SLC_BLOCK)
    ocmp, pen = nsa_cmp_select(qn, kkc, vvc, cbias, ovt, n_sel)
    mixg_col = jnp.broadcast_to(mixg.reshape(-1, 1), (GROUP_WIDTH, min(NSA_QUERY_TILE, T)))
    return nsa_attention(qnt, pen, ocmp, pa, kxs, v1s, kkw, v1w, tab_s, tab_w, mixg_col)


CHUNKS_PER_STEP = 4
SEQS_PER_STEP = 2


def _tn_dot(a, b):
    return lax.dot_general(a, b, (((0,), (0,)), ((), ())), preferred_element_type=jnp.float32)


def _split_bf16(a, parts):
    out, rest = [], a
    for _ in range(parts):
        piece = rest.astype(jnp.bfloat16)
        out.append(piece)
        rest = rest - piece.astype(jnp.float32)
    return out


def _dot_sel(a, sel, parts=3):
    return sum(jnp.dot(p, sel, preferred_element_type=jnp.float32) for p in _split_bf16(a, parts))


def _sel_dot(sel, a, parts=3):
    return sum(jnp.dot(sel, p, preferred_element_type=jnp.float32) for p in _split_bf16(a, parts))


def _softplus(x):
    return jnp.maximum(x, 0.0) + jnp.log(1.0 + jnp.exp(-jnp.abs(x)))


def _block_diag(a, hmb):
    return jnp.concatenate([a.astype(jnp.bfloat16)] * GROUP_HEADS, axis=0) * hmb


def _diag_row(a, eye):
    return jnp.sum(a * eye, axis=0, keepdims=True)


def _head_norm(o, hmb):
    return o * lax.rsqrt(_dot_sel(o * o, hmb, parts=2) * (1.0 / HEAD_DIM) + EPS)


def _head_l2(o, hmb):
    return o * lax.rsqrt(_dot_sel(o * o, hmb, parts=2) + EPS)


def _xform_masks(L):
    i = np.arange(L)[:, None]
    j = np.arange(L)[None, :]
    masks = []
    s = 1
    while s < L:
        masks.append((i // (2 * s) == j // (2 * s)) & (i % (2 * s) >= s) & (j % (2 * s) < s))
        s *= 2
    masks += [i == j, j <= i, j < i]
    return jnp.asarray(np.tile(np.stack(masks).astype(np.float32), (1, 1, GROUP_HEADS)))


def _expand_matrix(first_lane):
    e = np.zeros((128, GROUP_WIDTH), np.float32)
    for h in range(GROUP_HEADS):
        e[first_lane + h, h * HEAD_DIM:(h + 1) * HEAD_DIM] = 1.0
    return jnp.asarray(e, jnp.bfloat16)


def _head_mask():
    return jnp.asarray(np.kron(np.eye(GROUP_HEADS), np.ones((HEAD_DIM, HEAD_DIM))), jnp.float32)


def _per_head_rows(*vecs):
    rows = [jnp.repeat(v, HEAD_DIM) if v.shape[0] == GROUP_HEADS else jnp.tile(v, GROUP_HEADS) for v in vecs]
    return jnp.pad(jnp.stack(rows), ((0, 8 - len(rows)), (0, 0)))


def _chunk_consts(L, n_chunks):
    tri = np.kron(np.eye(n_chunks), np.tril(np.ones((L, L)))).astype(np.float32)
    return (_head_mask(), jnp.asarray(tri, jnp.bfloat16), _expand_matrix(0), _expand_matrix(GROUP_HEADS),
            _xform_masks(L))


def _const_spec(a):
    return pl.BlockSpec(a.shape, lambda b, t: (0,) * a.ndim)


def _gdn_kernel(pc_ref, hp_ref, hm_ref, tri_ref, eb_ref, ea_ref, lv_ref, o_ref, s_ref):
    L = GDN_CHUNK
    n_seq, rows = pc_ref.shape[0], pc_ref.shape[1]
    f32, bf16 = jnp.float32, jnp.bfloat16
    n_lev = lv_ref.shape[0] - 3

    @pl.when(pl.program_id(1) == 0)
    def _():
        s_ref[...] = jnp.zeros(s_ref.shape, f32)

    hm = hm_ref[...]
    hmb = hm.astype(bf16)
    eye, incl, strict = lv_ref[n_lev], lv_ref[n_lev + 1], lv_ref[n_lev + 2]

    q_c, k_c, v_c, beta_c, gc_c = [], [], [], [], []
    for b in range(n_seq):
        qkv = pc_ref[b, :, 0:768]
        q_all = _head_l2(qkv[:, 0:256], hmb) * HEAD_DIM ** -0.5
        k_all = _head_l2(qkv[:, 256:512], hmb)
        tail = pc_ref[b, :, 1024:1152]
        beta_all = jax.nn.sigmoid(_dot_sel(tail, eb_ref[...]))
        g = hp_ref[0:1, :] * _softplus(_dot_sel(tail, ea_ref[...]) + hp_ref[1:2, :])
        gc_all = _sel_dot(tri_ref[...], g)
        for c in range(rows // L):
            sl = slice(c * L, (c + 1) * L)
            q_c.append(q_all[sl])
            k_c.append(k_all[sl])
            v_c.append(qkv[sl, 512:768])
            beta_c.append(beta_all[sl])
            gc_c.append(gc_all[sl])
    chains = range(len(q_c))
    egc = [jnp.exp(gc) for gc in gc_c]
    kb = [k_c[i] * beta_c[i] for i in chains]
    a_mat, qk = [], []
    for i in chains:
        seg = jnp.exp(jnp.where(incl > 0.5, gc_c[i] - _diag_row(gc_c[i], eye), NEG))
        k_bd = _block_diag(k_c[i], hmb)
        a_mat.append(_nt_dot(kb[i].astype(bf16), k_bd) * seg * strict)
        qk.append((_nt_dot(q_c[i].astype(bf16), k_bd) * seg).astype(bf16))
    t_inv = [eye - a * lv_ref[0] for a in a_mat]
    for lev in range(1, n_lev):
        te = [jnp.dot(t_inv[i].astype(bf16), _block_diag(a_mat[i] * lv_ref[lev], hmb),
                      preferred_element_type=f32) for i in chains]
        t_inv = [t_inv[i] - jnp.dot(te[i].astype(bf16), _block_diag(t_inv[i], hmb), preferred_element_type=f32)
                 for i in chains]
    parts = []
    for i in chains:
        tb = t_inv[i].astype(bf16)
        u = jnp.dot(tb, _block_diag(v_c[i] * beta_c[i], hmb), preferred_element_type=f32)
        w = jnp.dot(tb, _block_diag(kb[i] * egc[i], hmb), preferred_element_type=f32)
        k_dec = k_c[i] * jnp.exp(gc_c[i][L - 1:L, :] - gc_c[i])
        parts.append((u, w.astype(bf16), (q_c[i] * egc[i]).astype(bf16), qk[i], k_dec.astype(bf16),
                      egc[i][L - 1:L, :]))
    per_seq = rows // L
    state = [s_ref[b] for b in range(n_seq)]
    outs = [[] for _ in range(n_seq)]
    for c in range(per_seq):
        for b in range(n_seq):
            u, w, q_dec, qk_c, k_dec, g_tot = parts[b * per_seq + c]
            sb = state[b].astype(bf16)
            v_new = u - jnp.dot(w, sb, preferred_element_type=f32)
            outs[b].append(jnp.dot(q_dec, sb, preferred_element_type=f32)
                           + jnp.dot(qk_c, _block_diag(v_new, hmb), preferred_element_type=f32))
            state[b] = state[b] * g_tot + _tn_dot(k_dec, v_new.astype(bf16)) * hm
    for b in range(n_seq):
        s_ref[b] = state[b]
        o = jnp.concatenate(outs[b], axis=0)
        z = pc_ref[b, :, 768:1024]
        o_ref[b] = _head_norm(o, hmb) * hp_ref[2:3, :] * (z * jax.nn.sigmoid(z))


def gated_deltanet_pallas(pc, A_log, dt_bias, norm_g):
    B, T, width = pc.shape
    L = GDN_CHUNK
    rows = L * min(CHUNKS_PER_STEP, T // L)
    n_seq = SEQS_PER_STEP if B % SEQS_PER_STEP == 0 else 1
    n = GROUP_HEADS * HEAD_DIM
    hp = _per_head_rows(-jnp.exp(A_log), dt_bias, norm_g)
    consts = (hp,) + _chunk_consts(L, rows // L)
    return pl.pallas_call(
        _gdn_kernel,
        grid=(B // n_seq, T // rows),
        in_specs=[pl.BlockSpec((n_seq, rows, width), lambda b, t: (b, t, 0))] + [_const_spec(a) for a in consts],
        out_specs=pl.BlockSpec((n_seq, rows, 256), lambda b, t: (b, t, 0)),
        out_shape=jax.ShapeDtypeStruct((B, T, 256), jnp.float32),
        scratch_shapes=[pltpu.VMEM((n_seq, n, n), jnp.float32)],
        compiler_params=pltpu.CompilerParams(dimension_semantics=("arbitrary", "arbitrary"),
                                             vmem_limit_bytes=VMEM_LIMIT_BYTES),
        name="gated_deltanet",
    )(pc, *consts)


def _mlstm_kernel(pd_ref, hp_ref, hm_ref, tri_ref, ei_ref, ef_ref, lv_ref, o_ref, c_ref, n_ref, m_ref):
    L = MLSTM_CHUNK
    n_seq, rows = pd_ref.shape[0], pd_ref.shape[1]
    f32, bf16 = jnp.float32, jnp.bfloat16
    n_lev = lv_ref.shape[0] - 3

    @pl.when(pl.program_id(1) == 0)
    def _():
        c_ref[...] = jnp.zeros(c_ref.shape, f32)
        n_ref[...] = jnp.zeros(n_ref.shape, f32)
        m_ref[...] = jnp.zeros(m_ref.shape, f32)

    hm = hm_ref[...]
    hmb = hm.astype(bf16)
    eye, incl = lv_ref[n_lev], lv_ref[n_lev + 1]
    head_of_lane = lax.broadcasted_iota(jnp.int32, (L, GROUP_WIDTH), 1) // HEAD_DIM

    def head_max(a):
        out = jnp.zeros(a.shape, f32)
        for h in range(GROUP_HEADS):
            mine = head_of_lane == h
            out = jnp.where(mine, jnp.max(jnp.where(mine, a, NEG), axis=-1, keepdims=True), out)
        return out

    per_seq = rows // L
    parts = []
    for b in range(n_seq):
        tail = pd_ref[b, :, 1024:1152]
        log_i_all = _dot_sel(tail, ei_ref[...])
        log_f = -_softplus(-(_dot_sel(tail, ef_ref[...]) + hp_ref[0:1, :]))
        b_all = _sel_dot(tri_ref[...], log_f)
        for c in range(per_seq):
            sl = slice(c * L, (c + 1) * L)
            q = pd_ref[b, sl, 0:256]
            k = pd_ref[b, sl, 256:512] * HEAD_DIM ** -0.5
            v = pd_ref[b, sl, 512:768]
            bc, log_i = b_all[sl], log_i_all[sl]
            log_w = jnp.where(incl > 0.5, bc - _diag_row(bc - log_i, eye), NEG)
            b_last = bc[L - 1:L, :]
            log_w_end = b_last - bc + log_i
            qk = _nt_dot(q.astype(bf16), _block_diag(k, hmb))
            parts.append((q, k, v, bc, log_w, head_max(log_w), b_last, log_w_end,
                          jnp.max(log_w_end, axis=0, keepdims=True), qk))
    c_state = [c_ref[b] for b in range(n_seq)]
    n_state = [n_ref[b] for b in range(n_seq)]
    m_prev = [m_ref[b] for b in range(n_seq)]
    outs = [[] for _ in range(n_seq)]
    for c in range(per_seq):
        for b in range(n_seq):
            q, k, v, bc, log_w, m_intra, b_last, log_w_end, m_end, qk = parts[b * per_seq + c]
            log_inter = bc + m_prev[b]
            m_t = jnp.maximum(log_inter, m_intra)
            w_inter = jnp.exp(log_inter - m_t)
            s = qk * jnp.exp(log_w - m_t)
            num = w_inter * jnp.dot(q.astype(bf16), c_state[b].astype(bf16), preferred_element_type=f32)
            num = num + jnp.dot(s.astype(bf16), _block_diag(v, hmb), preferred_element_type=f32)
            den = w_inter * _dot_sel(q * n_state[b], hmb, parts=2) + _dot_sel(s, hmb, parts=2)
            outs[b].append(num / jnp.maximum(jnp.abs(den), jnp.exp(-m_t)))
            m_new = jnp.maximum(b_last + m_prev[b], m_end)
            w_old = jnp.exp(b_last + m_prev[b] - m_new)
            kw = k * jnp.exp(log_w_end - m_new)
            c_state[b] = w_old * c_state[b] + _tn_dot(kw.astype(bf16), v.astype(bf16)) * hm
            n_state[b] = w_old * n_state[b] + jnp.sum(kw, axis=0, keepdims=True)
            m_prev[b] = m_new
    for b in range(n_seq):
        c_ref[b] = c_state[b]
        n_ref[b] = n_state[b]
        m_ref[b] = m_prev[b]
        h = jnp.concatenate(outs[b], axis=0)
        o_ref[b] = _head_norm(h, hmb) * hp_ref[1:2, :] * jax.nn.sigmoid(pd_ref[b, :, 768:1024])


def mlstm_pallas(pd, f_bias, norm_g):
    B, T, width = pd.shape
    L = MLSTM_CHUNK
    rows = L * min(CHUNKS_PER_STEP, T // L)
    n_seq = SEQS_PER_STEP if B % SEQS_PER_STEP == 0 else 1
    n = GROUP_HEADS * HEAD_DIM
    consts = (_per_head_rows(f_bias, norm_g),) + _chunk_consts(L, rows // L)
    return pl.pallas_call(
        _mlstm_kernel,
        grid=(B // n_seq, T // rows),
        in_specs=[pl.BlockSpec((n_seq, rows, width), lambda b, t: (b, t, 0))] + [_const_spec(a) for a in consts],
        out_specs=pl.BlockSpec((n_seq, rows, 256), lambda b, t: (b, t, 0)),
        out_shape=jax.ShapeDtypeStruct((B, T, 256), jnp.float32),
        scratch_shapes=[pltpu.VMEM((n_seq, n, n), jnp.float32), pltpu.VMEM((n_seq, 1, 256), jnp.float32),
                        pltpu.VMEM((n_seq, 1, 256), jnp.float32)],
        compiler_params=pltpu.CompilerParams(dimension_semantics=("arbitrary", "arbitrary"),
                                             vmem_limit_bytes=VMEM_LIMIT_BYTES),
        name="mlstm",
    )(pd, *consts)


def kernel(x, c, ada_w, ada_b, norm_g, ffn1_w13, ffn1_w2, ffn2_w13, ffn2_w2, w_in, b_in, q_norm_g, k_norm_g, cmp_pos, cmp_k_w1, cmp_k_w2, cmp_v_w1, cmp_v_w2, t5_table, sc_conv_w, gdn_conv_w, gdn_A_log, gdn_dt_bias, gdn_norm_g, mlstm_f_bias, mlstm_norm_g, mix_norm_g, w_out):
    B, T, D = x.shape
    depth = ada_w.shape[0]
    bf16 = jnp.bfloat16
    x2 = x.reshape(B * T, D)
    tables = nsa_bias_tables(t5_table, T)
    ovt = nsa_overlap_t(T)
    for l in range(depth):
        mod = ada_modulation(c, ada_w[l].astype(bf16), ada_b[l][None, :])
        mod = mod.reshape(B, N_SUBLAYERS, 3, 1, D)
        x2 = ffn_half_step(x2, norm_g[l, 0][None, :], mod[:, 0, 0], mod[:, 0, 1], mod[:, 0, 2],
                           ffn1_w13[l].astype(bf16), ffn1_w2[l].astype(bf16), T)
        ws, bs = split_in_weights(w_in[l], b_in[l])
        pa, y_b, pc, pd = in_projection(x2, norm_g[l, 1][None, :], mod[:, 1, 0], mod[:, 1, 1], ws, bs,
                                        sc_conv_w[l], mix_norm_g[l, 1][None, :], gdn_conv_w[l], T)
        pa, pc, pd = (a.reshape(B, T, -1) for a in (pa, pc, pd))
        y_a = nsa_mixer_pallas(pa, q_norm_g[l], k_norm_g[l], cmp_pos[l], cmp_k_w1[l], cmp_k_w2[l],
                               cmp_v_w1[l], cmp_v_w2[l], tables, ovt, mix_norm_g[l, 0][None, :])
        y_c = gated_deltanet_pallas(pc, gdn_A_log[l], gdn_dt_bias[l], gdn_norm_g[l])
        y_d = mlstm_pallas(pd, mlstm_f_bias[l], mlstm_norm_g[l])
        ys = [y.reshape(B * T, GROUP_WIDTH) for y in (y_a, y_b, y_c, y_d)]
        x2 = out_projection(x2, ys, mod[:, 1, 2], w_out[l].astype(bf16), T)
        x2 = ffn_half_step(x2, norm_g[l, 2][None, :], mod[:, 2, 0], mod[:, 2, 1], mod[:, 2, 2],
                           ffn2_w13[l].astype(bf16), ffn2_w2[l].astype(bf16), T)
    return x2.reshape(B, T, D)
```

```python
import functools
import math

import jax
import jax.numpy as jnp
import numpy as np
from jax import lax
from jax.experimental import pallas as pl
from jax.experimental.pallas import tpu as pltpu

D_MODEL = 1024
HEAD_DIM = 64
GROUP_HEADS = 4
GROUP_WIDTH = GROUP_HEADS * HEAD_DIM
CMP_STRIDE = 16
CMP_BLOCK = 32
SLC_BLOCK = 64
N_SLC = 16
N_LOCAL_SLC = 2
WINDOW = 512
Q_BLOCK = 128
FORCE = 1e6
N_BUCKETS = 32
MAX_DISTANCE = 128
GDN_CHUNK = 64
MLSTM_CHUNK = 64
D_FF = 2816
N_SUBLAYERS = 3
EPS = 1e-6

IN_LAYOUT = (
    ("a_q", 256), ("a_k_cmp", 64), ("a_v_cmp", 64),
    ("a_k_slc", 64), ("a_v_slc", 64), ("a_k_win", 64), ("a_v_win", 64),
    ("a_gate", 12),
    ("b_b", 256), ("b_c", 256), ("b_x", 256),
    ("c_q", 256), ("c_k", 256), ("c_v", 256),
    ("c_beta", 4), ("c_alpha", 4), ("c_z", 256),
    ("d_q", 256), ("d_k", 256), ("d_v", 256),
    ("d_i", 4), ("d_f", 4), ("d_o", 256),
)

VMEM_LIMIT_BYTES = 56 * 1024 * 1024
FFN_TOKEN_TILE = 512
FFN_CHUNK = 256
PROJ_TOKEN_TILE = 512
NEG = -1e30
HIGHEST = lax.Precision.HIGHEST
LOG2E = math.log2(math.e)
V1_ROWS = HEAD_DIM + 16


def _modulated_norm(x, g, scale, shift):
    y = x * lax.rsqrt(jnp.mean(x * x, axis=-1, keepdims=True) + EPS)
    return (y * g) * (1.0 + scale) + shift


def _ada_kernel(c_ref, w_ref, b_ref, o_ref):
    c = c_ref[...]
    cond = c * jax.nn.sigmoid(c)
    o_ref[...] = jnp.dot(cond.astype(jnp.bfloat16), w_ref[...].astype(jnp.bfloat16),
                         preferred_element_type=jnp.float32) + b_ref[...]


def ada_modulation(c, w, b):
    B, D = c.shape
    N = w.shape[1]
    tn = 1152
    return pl.pallas_call(
        _ada_kernel,
        grid=(N // tn,),
        in_specs=[pl.BlockSpec((B, D), lambda j: (0, 0)),
                  pl.BlockSpec((D, tn), lambda j: (0, j)),
                  pl.BlockSpec((1, tn), lambda j: (0, j))],
        out_specs=pl.BlockSpec((B, tn), lambda j: (0, j)),
        out_shape=jax.ShapeDtypeStruct((B, N), jnp.float32),
        name="ada_modulation",
    )(c, w, b)


def _ffn_kernel(x_ref, g_ref, shift_ref, scale_ref, gate_ref, w1_ref, w3_ref, w2_ref, *rest, n_mix):
    o_ref, act_ref = rest[-2:]
    x = x_ref[...]
    if n_mix:
        y_refs, mix_gate_ref, wo_ref = rest[:n_mix], rest[n_mix], rest[n_mix + 1]
        z = None
        for k, y_ref in enumerate(y_refs):
            part = jnp.dot(y_ref[...].astype(jnp.bfloat16), wo_ref[k * GROUP_WIDTH:(k + 1) * GROUP_WIDTH, :],
                           preferred_element_type=jnp.float32)
            z = part if z is None else z + part
        x = x + mix_gate_ref[0] * z
    h = _modulated_norm(x, g_ref[...], scale_ref[0], shift_ref[0]).astype(jnp.bfloat16)
    n_chunks = act_ref.shape[1] // FFN_CHUNK
    for ci in range(n_chunks):
        cs = slice(ci * FFN_CHUNK, (ci + 1) * FFN_CHUNK)
        a = jnp.dot(h, w1_ref[:, cs], preferred_element_type=jnp.float32)
        b = jnp.dot(h, w3_ref[:, cs], preferred_element_type=jnp.float32)
        act_ref[:, cs] = (a * jax.nn.sigmoid(a) * b).astype(jnp.bfloat16)
    y = jnp.dot(act_ref[...], w2_ref[...], preferred_element_type=jnp.float32)
    o_ref[...] = x + (0.5 * gate_ref[0]) * y


def ffn_half_step(x2, g, shift, scale, gate, w13, w2, tokens_per_batch, mix=None):
    M, D = x2.shape
    F = w2.shape[0]
    tm = FFN_TOKEN_TILE
    tiles_per_batch = tokens_per_batch // tm
    resident = dict(pipeline_mode=pl.Buffered(1))
    mod_spec = pl.BlockSpec((1, 1, D), lambda i: (i // tiles_per_batch, 0, 0))
    ys, mix_gate, w_out = mix if mix is not None else ((), None, None)
    mix_args = tuple(ys) + ((mix_gate, w_out) if ys else ())
    mix_specs = [pl.BlockSpec((tm, GROUP_WIDTH), lambda i: (i, 0)) for _ in ys]
    if ys:
        mix_specs += [mod_spec, pl.BlockSpec(w_out.shape, lambda i: (0, 0), **resident)]
    return pl.pallas_call(
        functools.partial(_ffn_kernel, n_mix=len(ys)),
        grid=(M // tm,),
        in_specs=[pl.BlockSpec((tm, D), lambda i: (i, 0)),
                  pl.BlockSpec((1, D), lambda i: (0, 0)),
                  mod_spec, mod_spec, mod_spec,
                  pl.BlockSpec((D, F), lambda i: (0, 0), **resident),
                  pl.BlockSpec((D, F), lambda i: (0, 1), **resident),
                  pl.BlockSpec((F, D), lambda i: (0, 0), **resident)] + mix_specs,
        out_specs=pl.BlockSpec((tm, D), lambda i: (i, 0)),
        out_shape=jax.ShapeDtypeStruct((M, D), jnp.float32),
        scratch_shapes=[pltpu.VMEM((tm, F), jnp.bfloat16)],
        compiler_params=pltpu.CompilerParams(dimension_semantics=("arbitrary",),
                                             vmem_limit_bytes=VMEM_LIMIT_BYTES),
        name="ffn_half_step",
    )(x2, g, shift, scale, gate, w13, w13, w2, *mix_args)


CONV_CARRY_ROWS = 8
CONV_ROW_CHUNK = 128


def _in_proj_kernel(x_ref, g_ref, shift_ref, scale_ref, wa_ref, wb_ref, wc_ref, wd_ref,
                    ba_ref, bb_ref, bc_ref, bd_ref, scw_ref, scg_ref, gcw_ref,
                    oa_ref, yb_ref, oc_ref, od_ref, braw_ref, craw_ref, *, tiles_per_batch):
    f32 = jnp.float32
    rows = x_ref.shape[0]
    hist = CONV_CARRY_ROWS
    first = pl.program_id(0) % tiles_per_batch == 0

    @pl.when(first)
    def _():
        braw_ref[0:hist, :] = jnp.zeros((hist, braw_ref.shape[1]), f32)
        craw_ref[0:hist, :] = jnp.zeros((hist, craw_ref.shape[1]), f32)

    @pl.when(jnp.logical_not(first))
    def _():
        braw_ref[0:hist, :] = braw_ref[rows:rows + hist, :]
        craw_ref[0:hist, :] = craw_ref[rows:rows + hist, :]

    h = _modulated_norm(x_ref[...], g_ref[...], scale_ref[0], shift_ref[0]).astype(jnp.bfloat16)
    project = lambda w_ref, b_ref: jnp.dot(h, w_ref[...], preferred_element_type=f32) + b_ref[...]
    pb = project(wb_ref, bb_ref)
    yb_ref[...] = pb[:, 0:256]
    braw_ref[hist:hist + rows, :] = pb[:, 256:512] * pb[:, 512:768]
    pc = project(wc_ref, bc_ref)
    craw_ref[hist:hist + rows, :] = pc[:, 0:768]
    oc_ref[:, 768:] = pc[:, 768:]
    n_chunks = rows // CONV_ROW_CHUNK
    pieces = []
    for w_ref, b_ref, o_ref in ((wa_ref, ba_ref, oa_ref), (wd_ref, bd_ref, od_ref)):
        half = w_ref.shape[1] // 256 // 2 * 256
        pieces += [(w_ref, b_ref, o_ref, 0, half), (w_ref, b_ref, o_ref, half, w_ref.shape[1])]
    for ci, r0 in enumerate(range(0, rows, CONV_ROW_CHUNK)):
        for w_ref, b_ref, o_ref, c0, c1 in pieces[ci * len(pieces) // n_chunks:(ci + 1) * len(pieces) // n_chunks]:
            o_ref[:, c0:c1] = jnp.dot(h, w_ref[:, c0:c1], preferred_element_type=f32) + b_ref[:, c0:c1]
        out_rows = slice(r0, r0 + CONV_ROW_CHUNK)
        tap = lambda ref, s: ref[pl.ds(hist + r0 - s, CONV_ROW_CHUNK), :]
        conv = sum(tap(braw_ref, s) * scw_ref[2 - s:3 - s, :] for s in range(3))
        y = yb_ref[out_rows, :] * conv
        yb_ref[out_rows, :] = y * lax.rsqrt(jnp.mean(y * y, axis=-1, keepdims=True) + EPS) * scg_ref[...]
        for c0 in range(0, 3 * GROUP_WIDTH, GROUP_WIDTH):
            cols = slice(c0, c0 + GROUP_WIDTH)
            conv = sum(craw_ref[pl.ds(hist + r0 - s, CONV_ROW_CHUNK), cols] * gcw_ref[3 - s:4 - s, cols]
                       for s in range(4))
            oc_ref[out_rows, cols] = conv * jax.nn.sigmoid(conv)


def in_projection(x2, g, shift, scale, ws, bs, sc_conv_w, sc_norm_g, gdn_conv_w, tokens_per_batch):
    M, D = x2.shape
    tm = PROJ_TOKEN_TILE
    tiles_per_batch = tokens_per_batch // tm
    mod_spec = pl.BlockSpec((1, 1, D), lambda i: (i // tiles_per_batch, 0, 0))
    const = lambda a, **kw: pl.BlockSpec(a.shape, lambda i: (0, 0), **kw)
    pad8 = lambda w: jnp.pad(w, ((0, 8 - w.shape[0]), (0, 0)))
    extras = (pad8(sc_conv_w), sc_norm_g, pad8(gdn_conv_w))
    widths = (ws[0].shape[1], GROUP_WIDTH, ws[2].shape[1], ws[3].shape[1])
    return pl.pallas_call(
        functools.partial(_in_proj_kernel, tiles_per_batch=tiles_per_batch),
        grid=(M // tm,),
        in_specs=[pl.BlockSpec((tm, D), lambda i: (i, 0)), const(g), mod_spec, mod_spec]
                 + [const(w, pipeline_mode=pl.Buffered(1)) for w in ws] + [const(b) for b in bs]
                 + [const(e) for e in extras],
        out_specs=[pl.BlockSpec((tm, n), lambda i: (i, 0)) for n in widths],
        out_shape=[jax.ShapeDtypeStruct((M, n), jnp.float32) for n in widths],
        scratch_shapes=[pltpu.VMEM((CONV_CARRY_ROWS + tm, GROUP_WIDTH), jnp.float32),
                        pltpu.VMEM((CONV_CARRY_ROWS + tm, 3 * GROUP_WIDTH), jnp.float32)],
        compiler_params=pltpu.CompilerParams(dimension_semantics=("arbitrary",),
                                             vmem_limit_bytes=VMEM_LIMIT_BYTES),
        name="in_projection",
    )(x2, g, shift, scale, *ws, *bs, *extras)


def _group_columns():
    offs, o = {}, 0
    for n, s in IN_LAYOUT:
        offs[n] = np.arange(o, o + s)
        o += s
    cat = lambda names: np.concatenate([offs[n] for n in names])
    return (cat(["a_q", "a_k_cmp", "a_v_cmp", "a_k_slc", "a_v_slc", "a_k_win", "a_v_win", "a_gate"]),
            cat(["b_b", "b_c", "b_x"]),
            cat(["c_q", "c_k", "c_v", "c_z", "c_beta", "c_alpha"]),
            cat(["d_q", "d_k", "d_v", "d_o", "d_i", "d_f"]))


GROUP_SLAB_WIDTH = (768, 768, 1152, 1152)


def _take_columns(a, idx, width):
    cuts = [0] + [k + 1 for k in range(idx.size - 1) if idx[k + 1] != idx[k] + 1] + [idx.size]
    runs = [a[..., int(idx[s]):int(idx[e - 1]) + 1] for s, e in zip(cuts[:-1], cuts[1:])]
    return jnp.concatenate(runs + [jnp.zeros(a.shape[:-1] + (width - idx.size,), a.dtype)], axis=-1)


def split_in_weights(w, b):
    ws, bs = [], []
    for idx, width in zip(_group_columns(), GROUP_SLAB_WIDTH):
        ws.append(_take_columns(w, idx, width).astype(jnp.bfloat16))
        bs.append(_take_columns(b, idx, width)[None, :])
    return ws, bs


def t5_bucket(dist):
    n = jnp.maximum(dist, 0)
    max_exact = N_BUCKETS // 2
    nf = jnp.maximum(n, 1).astype(jnp.float32)
    large = max_exact + (jnp.log(nf / max_exact) / math.log(MAX_DISTANCE / max_exact)
                         * (N_BUCKETS - max_exact)).astype(jnp.int32)
    large = jnp.minimum(large, N_BUCKETS - 1)
    return jnp.where(n < max_exact, n, large)


def _nt_dot(a, b):
    return lax.dot_general(a, b, (((1,), (1,)), ((), ())), preferred_element_type=jnp.float32)


def _stack_heads(qn):
    lane = lax.broadcasted_iota(jnp.int32, (qn.shape[0], 128), 1)
    lo = lane < HEAD_DIM
    zero = jnp.zeros((), qn.dtype)
    halves = (qn[:, 0:128], qn[:, 128:256])
    return jnp.concatenate([jnp.where(lo if h % 2 == 0 else ~lo, halves[h // 2], zero)
                            for h in range(GROUP_HEADS)], axis=0)


def _nsa_prep_kernel(q_ref, kvs_ref, kvw_ref, qg_ref, kg_ref, seg_ref,
                     qn_ref, qnt_ref, kxs_ref, v1s_ref, kkw_ref, v1w_ref):
    tm = q_ref.shape[0]
    q = q_ref[...]
    ss = jnp.dot(q * q, seg_ref[...], precision=HIGHEST, preferred_element_type=jnp.float32)
    qn = q * lax.rsqrt(ss * (1.0 / HEAD_DIM) + EPS) * qg_ref[...]
    qn_ref[...] = qn.astype(qn_ref.dtype)
    qnt_ref[...] = (qn * LOG2E).T.astype(qnt_ref.dtype)
    lane = lax.broadcasted_iota(jnp.int32, (tm, 128), 1)
    lo = lane < HEAD_DIM

    def split(x):
        xr = pltpu.roll(x, HEAD_DIM, 1)
        ss = jnp.sum(jnp.where(lo, x * x, 0.0), axis=-1, keepdims=True)
        kk = jnp.where(lo, x, xr) * lax.rsqrt(ss * (1.0 / HEAD_DIM) + EPS) * kg_ref[...]
        return kk, jnp.where(lo, xr, 1.0)

    kk, v1 = split(kvs_ref[...])
    tok = pl.program_id(1) * tm + lax.broadcasted_iota(jnp.int32, (tm, 128), 0)
    onehot = jnp.where(lane == tok // SLC_BLOCK, 1.0, 0.0)
    kxs_ref[...] = jnp.concatenate([kk, onehot], axis=1).astype(kxs_ref.dtype)
    v1s_ref[...] = v1.T[0:V1_ROWS].astype(v1s_ref.dtype)
    kk, v1 = split(kvw_ref[...])
    kkw_ref[...] = kk.astype(kkw_ref.dtype)
    v1w_ref[...] = v1.T[0:V1_ROWS].astype(v1w_ref.dtype)


def nsa_prep(pa, qg, kg2, seg):
    B, T, _ = pa.shape
    tm = 512
    bf16 = jnp.bfloat16
    col = lambda w, j: pl.BlockSpec((None, tm, w), lambda b, i: (b, i, j))
    rowblk = lambda h: pl.BlockSpec((None, h, tm), lambda b, i: (b, 0, i))
    const = lambda shape: pl.BlockSpec(shape, lambda b, i: (0, 0))
    return pl.pallas_call(
        _nsa_prep_kernel,
        grid=(B, T // tm),
        in_specs=[col(256, 0), col(128, 3), col(128, 4), const((1, 256)), const((1, 128)), const((256, 256))],
        out_specs=[col(256, 0), rowblk(256), col(256, 0), rowblk(V1_ROWS), col(128, 0), rowblk(V1_ROWS)],
        out_shape=[jax.ShapeDtypeStruct((B, T, 256), bf16), jax.ShapeDtypeStruct((B, 256, T), bf16),
                   jax.ShapeDtypeStruct((B, T, 256), bf16), jax.ShapeDtypeStruct((B, V1_ROWS, T), bf16),
                   jax.ShapeDtypeStruct((B, T, 128), bf16), jax.ShapeDtypeStruct((B, V1_ROWS, T), bf16)],
        name="nsa_prep",
    )(pa, pa, pa, qg, kg2, seg)


def _compress_kernel(kch_ref, vch_ref, plo_ref, phi_ref, w1k_ref, w2k_ref, w1v_ref, w2v_ref, kg_ref,
                     kkc_ref, vvc_ref):
    nrow = kch_ref.shape[0]
    half = CMP_STRIDE * HEAD_DIM
    bf16 = jnp.bfloat16
    valid = lax.broadcasted_iota(jnp.int32, (nrow, 128), 0) < nrow - 1

    def comp(ch, w1_ref, w2_ref):
        a = jnp.dot((ch + plo_ref[...]).astype(bf16), w1_ref[0:half, :], preferred_element_type=jnp.float32)
        b = jnp.dot((ch + phi_ref[...]).astype(bf16), w1_ref[half:2 * half, :],
                    preferred_element_type=jnp.float32)
        hid = a + pltpu.roll(b, nrow - 1, 0)
        hid = hid * jax.nn.sigmoid(hid)
        return jnp.dot(hid.astype(bf16), w2_ref[...], preferred_element_type=jnp.float32)

    kc = comp(kch_ref[...], w1k_ref, w2k_ref)
    kc = kc * lax.rsqrt(jnp.mean(kc * kc, axis=-1, keepdims=True) + EPS) * kg_ref[...]
    kkc_ref[...] = jnp.where(valid, kc, 0.0).astype(kkc_ref.dtype)
    vc = comp(vch_ref[...], w1v_ref, w2v_ref)
    vvc_ref[...] = jnp.where(valid, vc, 0.0).astype(vvc_ref.dtype)


def nsa_compress(kch, vch, plo, phi, w1k, w2k, w1v, w2v, kg2):
    B, nrow, width = kch.shape
    bf16 = jnp.bfloat16
    full = lambda a: pl.BlockSpec(a.shape, lambda b: (0,) * a.ndim)
    bspec = pl.BlockSpec((None, nrow, width), lambda b: (b, 0, 0))
    ospec = pl.BlockSpec((None, nrow, 128), lambda b: (b, 0, 0))
    return pl.pallas_call(
        _compress_kernel,
        grid=(B,),
        in_specs=[bspec, bspec, full(plo), full(phi), full(w1k), full(w2k), full(w1v), full(w2v), full(kg2)],
        out_specs=[ospec, ospec],
        out_shape=[jax.ShapeDtypeStruct((B, nrow, 128), bf16), jax.ShapeDtypeStruct((B, nrow, 128), bf16)],
        name="nsa_compress",
    )(kch, vch, plo, phi, w1k, w2k, w1v, w2v, kg2)


CMP_TILES_PER_STEP = 2


def _cmp_select_kernel(qn_ref, kkc_ref, vvc_ref, cb_ref, ovt_ref, ocmp_ref, pen_ref, sc_ref, *, n_sel):
    i = pl.program_id(1)
    n_tiles = cb_ref.shape[0]
    tq = qn_ref.shape[0] // n_tiles
    nblk = ovt_ref.shape[0]
    bf16 = jnp.bfloat16
    lo = lax.broadcasted_iota(jnp.int32, (tq, 128), 1) < HEAD_DIM
    blk = lax.broadcasted_iota(jnp.int32, (nblk, tq), 0)
    scores = []
    for g in range(n_tiles):
        rows = slice(g * tq, (g + 1) * tq)
        q4 = _stack_heads(qn_ref[rows, :])
        s = _nt_dot(q4, kkc_ref[...]) + cb_ref[g]
        m = jnp.max(s, axis=-1, keepdims=True)
        e = jnp.where(s > 0.5 * NEG, jnp.exp(s - m), 0.0)
        p = e * (1.0 / jnp.maximum(jnp.sum(e, axis=-1, keepdims=True), 1e-30))
        o = jnp.dot(p.astype(bf16), vvc_ref[...], preferred_element_type=jnp.float32)
        ocmp_ref[rows, 0:128] = jnp.where(lo, o[0:tq], o[tq:2 * tq])
        ocmp_ref[rows, 128:256] = jnp.where(lo, o[2 * tq:3 * tq], o[3 * tq:4 * tq])
        psum = p[0:tq] + p[tq:2 * tq] + p[2 * tq:3 * tq] + p[3 * tq:4 * tq]
        hi = psum.astype(bf16)
        lo_part = (psum - hi.astype(jnp.float32)).astype(bf16)
        score = _nt_dot(ovt_ref[...], hi) + _nt_dot(ovt_ref[...], lo_part)
        qpos = (i * n_tiles + g) * tq + lax.broadcasted_iota(jnp.int32, (nblk, tq), 1)
        cur = qpos // SLC_BLOCK
        forced = (blk == 0) | ((cur - blk >= 0) & (cur - blk < N_LOCAL_SLC))
        score = jnp.where(forced, FORCE, score)
        score = jnp.where(blk <= cur, score, -FORCE)
        sc_ref[g] = score
        scores.append(score)
    blocks_per_tile = tq // SLC_BLOCK

    def rank_step(t, cnts):
        out = []
        for g in range(n_tiles):
            cnt = cnts[g]
            for u in range(blocks_per_tile):
                jp = t * blocks_per_tile + u
                row = sc_ref[g, pl.ds(jp, 1), :]
                tie = jnp.where(blk > jp, 1.0, 0.0)
                cnt = cnt + jnp.where(row > scores[g], 1.0, jnp.where(row == scores[g], tie, 0.0))
            out.append(cnt)
        return tuple(out)

    zeros = tuple(jnp.zeros((nblk, tq), jnp.float32) for _ in range(n_tiles))
    cnts = lax.fori_loop(0, (i + 1) * n_tiles, rank_step, zeros)
    for g in range(n_tiles):
        pen = jnp.where(cnts[g] < n_sel, 0.0, NEG)
        if nblk < 128:
            pen = jnp.concatenate([pen, jnp.zeros((128 - nblk, tq), jnp.float32)], axis=0)
        pen_ref[g] = pen.astype(pen_ref.dtype)


def nsa_cmp_select(qn, kkc, vvc, cbias, ovt, n_sel):
    B, T, _ = qn.shape
    nrow = kkc.shape[1]
    nblk = ovt.shape[0]
    tq = Q_BLOCK
    g = CMP_TILES_PER_STEP
    return pl.pallas_call(
        functools.partial(_cmp_select_kernel, n_sel=n_sel),
        grid=(B, T // (g * tq)),
        in_specs=[pl.BlockSpec((None, g * tq, 256), lambda b, i: (b, i, 0)),
                  pl.BlockSpec((None, nrow, 128), lambda b, i: (b, 0, 0)),
                  pl.BlockSpec((None, nrow, 128), lambda b, i: (b, 0, 0)),
                  pl.BlockSpec((g, 4 * tq, nrow), lambda b, i: (i, 0, 0)),
                  pl.BlockSpec((nblk, nrow), lambda b, i: (0, 0))],
        out_specs=[pl.BlockSpec((None, g * tq, 256), lambda b, i: (b, i, 0)),
                   pl.BlockSpec((None, g, 128, tq), lambda b, i: (b, i, 0, 0))],
        out_shape=[jax.ShapeDtypeStruct((B, T, 256), jnp.float32),
                   jax.ShapeDtypeStruct((B, T // tq, 128, tq), jnp.bfloat16)],
        scratch_shapes=[pltpu.VMEM((g, nblk, tq), jnp.float32)],
        name="nsa_cmp_select",
    )(qn, kkc, vvc, cbias, ovt)


NSA_KEY_TILE = 256
NSA_QUERY_TILE = 256


def _nsa_attn_kernel(qnt_ref, pen_ref, ocmp_ref, gate_ref, kxs_ref, v1s_ref, kkw_ref, v1w_ref,
                     tabs_ref, tabw_ref, mixg_ref, o_ref, acc_ref, m_ref, sa_ref, sb_ref):
    i = pl.program_id(1)
    tq = o_ref.shape[0]
    kt = tabs_ref.shape[1]
    f32, bf16 = jnp.float32, jnp.bfloat16
    qt = qnt_ref[...]
    lo = lax.broadcasted_iota(jnp.int32, (128, tq), 0) < HEAD_DIM
    zero = jnp.zeros((), qt.dtype)
    q4t = jnp.concatenate([jnp.where(lo if h % 2 == 0 else ~lo, qt[128 * (h // 2):128 * (h // 2) + 128], zero)
                           for h in range(GROUP_HEADS)], axis=1)
    pen = jnp.concatenate([pen_ref[t] for t in range(pen_ref.shape[0])], axis=1)
    rhs_sel = jnp.concatenate([q4t, jnp.concatenate([pen] * GROUP_HEADS, axis=1)], axis=0)

    def branch(rhs, k_ref, v_ref, tab_ref, first_tile, last_tile):
        n_tab = tab_ref.shape[0] - 1
        m_ref[...] = jnp.full(m_ref.shape, NEG, f32)
        acc_ref[...] = jnp.zeros(acc_ref.shape, f32)

        def key_offset(j):
            return pl.multiple_of(jnp.minimum(j, last_tile) * kt, kt)

        def scores(j):
            delta = jnp.where(j > last_tile, n_tab, jnp.minimum(i - j * (kt // tq), n_tab - 1))
            return jnp.dot(k_ref[pl.ds(key_offset(j), kt), :], rhs, preferred_element_type=f32) + tab_ref[delta]

        def absorb(s, j):
            m_old = m_ref[...]
            m_new = jnp.maximum(m_old, jnp.max(s, axis=0, keepdims=True))
            p = jnp.exp2(s - m_new)
            pv =jnp.dot(v_ref[:, pl.ds(key_offset(j), kt)], p.astype(bf16), preferred_element_type=f32)
            acc_ref[...] = jnp.exp2(m_old - m_new) * acc_ref[...] + pv
            m_ref[...] = m_new

        sa_ref[...] = scores(first_tile)

        def body(t, carry):
            j = first_tile + 2 * t
            sb_ref[...] = scores(j + 1)
            absorb(sa_ref[...], j)
            sa_ref[...] = scores(j + 2)
            absorb(sb_ref[...], j + 1)
            return carry

        lax.fori_loop(0, (last_tile - first_tile) // 2 + 1, body, 0)
        acc = acc_ref[...]
        return acc[0:HEAD_DIM] / jnp.maximum(acc[HEAD_DIM:HEAD_DIM + 1], 1e-30)

    def window_branch(rhs):
        n_tab = tabw_ref.shape[0] - 1
        m = jnp.full((1, rhs.shape[1]), NEG, f32)
        acc = jnp.zeros((V1_ROWS, rhs.shape[1]), f32)
        for d in range(n_tab):
            j = i - d
            off = pl.multiple_of(jnp.maximum(j, 0) * kt, kt)
            s = jnp.dot(kkw_ref[pl.ds(off, kt), :], rhs, preferred_element_type=f32)
            s = s + tabw_ref[jnp.where(j < 0, n_tab, d)]
            m_new = jnp.maximum(m, jnp.max(s, axis=0, keepdims=True))
            p = jnp.exp2(s - m_new)
            acc = jnp.exp2(m - m_new) * acc +jnp.dot(v1w_ref[:, pl.ds(off, kt)], p.astype(bf16),
                                                     preferred_element_type=f32)
            m = m_new
        return acc[0:HEAD_DIM] / jnp.maximum(acc[HEAD_DIM:HEAD_DIM + 1], 1e-30)

    last = (i * tq) // kt
    if kt == tq:
        o_w = window_branch(q4t)
    else:
        o_w = branch(q4t, kkw_ref, v1w_ref, tabw_ref, (jnp.maximum(i - WINDOW // tq, 0) * tq) // kt, last)
    o_s = branch(rhs_sel, kxs_ref, v1s_ref, tabs_ref, 0, last)
    gt = jax.nn.sigmoid(gate_ref[...]).T
    oct_ = ocmp_ref[...].T
    parts = []
    for h in range(GROUP_HEADS):
        cols = slice(h * tq, (h + 1) * tq)
        parts.append(gt[3 * h:3 * h + 1] * oct_[h * HEAD_DIM:(h + 1) * HEAD_DIM]
                     + gt[3 * h + 1:3 * h + 2] * o_s[:, cols] + gt[3 * h + 2:3 * h + 3] * o_w[:, cols])
    yt = jnp.concatenate(parts, axis=0)
    yt = yt * lax.rsqrt(jnp.mean(yt * yt, axis=0, keepdims=True) + EPS) * mixg_ref[...]
    o_ref[...] = yt.T


def nsa_attention(qnt, pen, ocmp, pa, kxs, v1s, kkw, v1w, tab_s, tab_w, mixg):
    B, _, T = qnt.shape
    tq = min(NSA_QUERY_TILE, T)
    tile = lambda w, j: pl.BlockSpec((None, tq, w), lambda b, i: (b, i, j))
    whole = lambda w: pl.BlockSpec((None, T, w), lambda b, i: (b, 0, 0))
    whole_t = pl.BlockSpec((None, V1_ROWS, T), lambda b, i: (b, 0, 0))
    full = lambda a: pl.BlockSpec(a.shape, lambda b, i: (0,) * a.ndim)
    return pl.pallas_call(
        _nsa_attn_kernel,
        grid=(B, T // tq),
        in_specs=[pl.BlockSpec((None, 256, tq), lambda b, i: (b, 0, i)),
                  pl.BlockSpec((None, tq // Q_BLOCK, 128, Q_BLOCK), lambda b, i: (b, i, 0, 0)),
                  tile(256, 0), tile(128, 5),
                  whole(256), whole_t, whole(128), whole_t, full(tab_s), full(tab_w), full(mixg)],
        out_specs=tile(256, 0),
        out_shape=jax.ShapeDtypeStruct((B, T, 256), jnp.float32),
        scratch_shapes=[pltpu.VMEM((V1_ROWS, GROUP_HEADS * tq), jnp.float32),
                        pltpu.VMEM((1, GROUP_HEADS * tq), jnp.float32),
                        pltpu.VMEM((tab_s.shape[1], GROUP_HEADS * tq), jnp.float32),
                        pltpu.VMEM((tab_s.shape[1], GROUP_HEADS * tq), jnp.float32)],
        compiler_params=pltpu.CompilerParams(dimension_semantics=("arbitrary", "arbitrary"),
                                             vmem_limit_bytes=VMEM_LIMIT_BYTES),
        name="nsa_attention",
    )(qnt, pen, ocmp, pa, kxs, v1s, kkw, v1w, tab_s, tab_w, mixg)


def _toeplitz(wr, rows, cols):
    H, n = wr.shape
    flat = jnp.tile(jnp.pad(wr, ((0, 0), (0, 1))), (1, rows))[:, :rows * n]
    return flat.reshape(H, rows, n)[:, :, rows - 1:rows - 1 + cols]


def nsa_bias_tables(t5_table, T):
    H = GROUP_HEADS
    tq = Q_BLOCK
    ta = min(NSA_QUERY_TILE, T)
    kt = min(NSA_KEY_TILE, T)
    nrow = T // CMP_STRIDE
    dmax = WINDOW + 2 * ta + kt
    onehot = (t5_bucket(jnp.arange(dmax))[:, None] == jnp.arange(N_BUCKETS)[None, :]).astype(jnp.float32)
    by_dist = jnp.dot(onehot, t5_table, precision=HIGHEST).T
    neg = lambda n: jnp.full((H, n), NEG, jnp.float32)
    rev = by_dist[:, ::-1] * LOG2E
    f_s = jnp.concatenate([rev, neg(kt - 1)], axis=1)
    f_w = jnp.concatenate([neg(dmax - WINDOW), rev[:, dmax - WINDOW:], neg(kt - 1)], axis=1)
    span = ta + kt - 1
    total = dmax + kt - 1

    def tiles(f, n):
        out = [_toeplitz(f[:, total - k * ta - span:total - k * ta], ta, kt).reshape(H * ta, kt).T for k in range(n)]
        return jnp.stack(out + [jnp.full((kt, H * ta), NEG, jnp.float32)])

    n_s = -(-(kt + MAX_DISTANCE) // ta) + 1
    n_w = WINDOW // ta + kt // ta
    tab_s, tab_w = tiles(f_s, n_s), tiles(f_w, n_w)
    per_tile = tq // CMP_STRIDE
    d0 = -CMP_STRIDE * (nrow - 1) - (CMP_BLOCK - 1)
    n16 = 2 * nrow + per_tile - 1
    n_pos = CMP_STRIDE * n16 + d0
    far = jnp.broadcast_to(by_dist[:, -1:], (H, max(n_pos - dmax, 0)))
    w16 = jnp.concatenate([neg(-d0), by_dist[:, :n_pos], far], axis=1).reshape(H, n16, CMP_STRIDE)
    w16 = w16[:, ::-1, :]
    base = jnp.stack([w16[:, per_tile - 1 - a:per_tile - 1 - a + 2 * nrow, :] for a in range(per_tile)], axis=1)
    base = base.transpose(0, 1, 3, 2).reshape(H * tq, 2 * nrow)
    cbias = jnp.stack([base[:, nrow - per_tile * i: 2 * nrow - per_tile * i] for i in range(T // tq)])
    return tab_s, tab_w, cbias


def nsa_overlap_t(T):
    nrow = T // CMP_STRIDE
    nblk = T // SLC_BLOCK
    ci = np.arange(nrow)[None, :]
    bj = np.arange(nblk)[:, None]
    ov = (ci * CMP_STRIDE < (bj + 1) * SLC_BLOCK) & (ci * CMP_STRIDE + CMP_BLOCK > bj * SLC_BLOCK)
    ov = ov & (ci < nrow - 1)
    return jnp.asarray(ov.astype(np.float32), dtype=jnp.bfloat16)


def nsa_mixer_pallas(pa, q_norm_g, k_norm_g, cmp_pos, cmp_k_w1, cmp_k_w2, cmp_v_w1, cmp_v_w2,
                     tables, ovt, mixg):
    B, T, _ = pa.shape
    bf16 = jnp.bfloat16
    tab_s, tab_w, cbias = tables
    qg = (jnp.tile(q_norm_g, GROUP_HEADS) * HEAD_DIM ** -0.5)[None, :]
    kg2 = jnp.tile(k_norm_g, 2)[None, :]
    qn, qnt, kxs, v1s, kkw, v1w = nsa_prep(pa, qg, kg2, _head_mask())
    nrow = T // CMP_STRIDE
    kch = pa[:, :, 256:320].reshape(B, nrow, CMP_STRIDE * HEAD_DIM)
    vch = pa[:, :, 320:384].reshape(B, nrow, CMP_STRIDE * HEAD_DIM)
    plo = cmp_pos[:CMP_STRIDE].reshape(1, -1)
    phi = cmp_pos[CMP_STRIDE:].reshape(1, -1)
    dup = lambda w: jnp.concatenate([w, w], axis=1).astype(bf16)
    kkc, vvc = nsa_compress(kch, vch, plo, phi, cmp_k_w1.astype(bf16), dup(cmp_k_w2),
                            cmp_v_w1.astype(bf16), dup(cmp_v_w2), kg2)
    n_sel = min(N_SLC, T // SLC_BLOCK)
    ocmp, pen = nsa_cmp_select(qn, kkc, vvc, cbias, ovt, n_sel)
    mixg_col = jnp.broadcast_to(mixg.reshape(-1, 1), (GROUP_WIDTH, min(NSA_QUERY_TILE, T)))
    return nsa_attention(qnt, pen, ocmp, pa, kxs, v1s, kkw, v1w, tab_s, tab_w, mixg_col)


CHUNKS_PER_STEP = 4
SEQS_PER_STEP = 2


def _tn_dot(a, b):
    return lax.dot_general(a, b, (((0,), (0,)), ((), ())), preferred_element_type=jnp.float32)


def _split_bf16(a, parts):
    out, rest = [], a
    for _ in range(parts):
        piece = rest.astype(jnp.bfloat16)
        out.append(piece)
        rest = rest - piece.astype(jnp.float32)
    return out


def _dot_sel(a, sel, parts=3):
    return sum(jnp.dot(p, sel, preferred_element_type=jnp.float32) for p in _split_bf16(a, parts))


def _sel_dot(sel, a, parts=3):
    return sum(jnp.dot(sel, p, preferred_element_type=jnp.float32) for p in _split_bf16(a, parts))


def _softplus(x):
    return jnp.maximum(x, 0.0) + jnp.log(1.0 + jnp.exp(-jnp.abs(x)))


def _block_diag(a, hmb):
    return jnp.concatenate([a.astype(jnp.bfloat16)] * GROUP_HEADS, axis=0) * hmb


def _diag_row(a, eye):
    return jnp.sum(a * eye, axis=0, keepdims=True)


def _head_norm(o, hmb):
    return o * lax.rsqrt(_dot_sel(o * o, hmb, parts=2) * (1.0 / HEAD_DIM) + EPS)


def _head_l2(o, hmb):
    return o * lax.rsqrt(_dot_sel(o * o, hmb, parts=2) + EPS)


def _xform_masks(L):
    i = np.arange(L)[:, None]
    j = np.arange(L)[None, :]
    masks = []
    s = 1
    while s < L:
        masks.append((i // (2 * s) == j // (2 * s)) & (i % (2 * s) >= s) & (j % (2 * s) < s))
        s *= 2
    masks += [i == j, j <= i, j < i]
    return jnp.asarray(np.tile(np.stack(masks).astype(np.float32), (1, 1, GROUP_HEADS)))


def _expand_matrix(first_lane):
    e = np.zeros((128, GROUP_WIDTH), np.float32)
    for h in range(GROUP_HEADS):
        e[first_lane + h, h * HEAD_DIM:(h + 1) * HEAD_DIM] = 1.0
    return jnp.asarray(e, jnp.bfloat16)


def _head_mask():
    return jnp.asarray(np.kron(np.eye(GROUP_HEADS), np.ones((HEAD_DIM, HEAD_DIM))), jnp.float32)


def _per_head_rows(*vecs):
    rows = [jnp.repeat(v, HEAD_DIM) if v.shape[0] == GROUP_HEADS else jnp.tile(v, GROUP_HEADS) for v in vecs]
    return jnp.pad(jnp.stack(rows), ((0, 8 - len(rows)), (0, 0)))


def _chunk_consts(L, n_chunks):
    tri = np.kron(np.eye(n_chunks), np.tril(np.ones((L, L)))).astype(np.float32)
    return (_head_mask(), jnp.asarray(tri, jnp.bfloat16), _expand_matrix(0), _expand_matrix(GROUP_HEADS),
            _xform_masks(L))


def _const_spec(a):
    return pl.BlockSpec(a.shape, lambda b, t: (0,) * a.ndim)


def _gdn_kernel(pc_ref, hp_ref, hm_ref, tri_ref, eb_ref, ea_ref, lv_ref, o_ref, s_ref):
    L = GDN_CHUNK
    n_seq, rows = pc_ref.shape[0], pc_ref.shape[1]
    f32, bf16 = jnp.float32, jnp.bfloat16
    n_lev = lv_ref.shape[0] - 3

    @pl.when(pl.program_id(1) == 0)
    def _():
        s_ref[...] = jnp.zeros(s_ref.shape, f32)

    hm = hm_ref[...]
    hmb = hm.astype(bf16)
    eye, incl, strict = lv_ref[n_lev], lv_ref[n_lev + 1], lv_ref[n_lev + 2]

    q_c, k_c, v_c, beta_c, gc_c = [], [], [], [], []
    for b in range(n_seq):
        qkv = pc_ref[b, :, 0:768]
        q_all = _head_l2(qkv[:, 0:256], hmb) * HEAD_DIM ** -0.5
        k_all = _head_l2(qkv[:, 256:512], hmb)
        tail = pc_ref[b, :, 1024:1152]
        beta_all = jax.nn.sigmoid(_dot_sel(tail, eb_ref[...]))
        g = hp_ref[0:1, :] * _softplus(_dot_sel(tail, ea_ref[...]) + hp_ref[1:2, :])
        gc_all = _sel_dot(tri_ref[...], g)
        for c in range(rows // L):
            sl = slice(c * L, (c + 1) * L)
            q_c.append(q_all[sl])
            k_c.append(k_all[sl])
            v_c.append(qkv[sl, 512:768])
            beta_c.append(beta_all[sl])
            gc_c.append(gc_all[sl])
    chains = range(len(q_c))
    egc = [jnp.exp(gc) for gc in gc_c]
    kb = [k_c[i] * beta_c[i] for i in chains]
    a_mat, qk = [], []
    for i in chains:
        seg = jnp.exp(jnp.where(incl > 0.5, gc_c[i] - _diag_row(gc_c[i], eye), NEG))
        k_bd = _block_diag(k_c[i], hmb)
        a_mat.append(_nt_dot(kb[i].astype(bf16), k_bd) * seg * strict)
        qk.append((_nt_dot(q_c[i].astype(bf16), k_bd) * seg).astype(bf16))
    t_inv = [eye - a * lv_ref[0] for a in a_mat]
    for lev in range(1, n_lev):
        te = [jnp.dot(t_inv[i].astype(bf16), _block_diag(a_mat[i] * lv_ref[lev], hmb),
                      preferred_element_type=f32) for i in chains]
        t_inv = [t_inv[i] - jnp.dot(te[i].astype(bf16), _block_diag(t_inv[i], hmb), preferred_element_type=f32)
                 for i in chains]
    parts = []
    for i in chains:
        tb = t_inv[i].astype(bf16)
        u = jnp.dot(tb, _block_diag(v_c[i] * beta_c[i], hmb), preferred_element_type=f32)
        w = jnp.dot(tb, _block_diag(kb[i] * egc[i], hmb), preferred_element_type=f32)
        k_dec = k_c[i] * jnp.exp(gc_c[i][L - 1:L, :] - gc_c[i])
        parts.append((u, w.astype(bf16), (q_c[i] * egc[i]).astype(bf16), qk[i], k_dec.astype(bf16),
                      egc[i][L - 1:L, :]))
    per_seq = rows // L
    state = [s_ref[b] for b in range(n_seq)]
    outs = [[] for _ in range(n_seq)]
    for c in range(per_seq):
        for b in range(n_seq):
            u, w, q_dec, qk_c, k_dec, g_tot = parts[b * per_seq + c]
            sb = state[b].astype(bf16)
            v_new = u - jnp.dot(w, sb, preferred_element_type=f32)
            outs[b].append(jnp.dot(q_dec, sb, preferred_element_type=f32)
                           + jnp.dot(qk_c, _block_diag(v_new, hmb), preferred_element_type=f32))
            state[b] = state[b] * g_tot + _tn_dot(k_dec, v_new.astype(bf16)) * hm
    for b in range(n_seq):
        s_ref[b] = state[b]
        o = jnp.concatenate(outs[b], axis=0)
        z = pc_ref[b, :, 768:1024]
        o_ref[b] = _head_norm(o, hmb) * hp_ref[2:3, :] * (z * jax.nn.sigmoid(z))


def gated_deltanet_pallas(pc, A_log, dt_bias, norm_g):
    B, T, width = pc.shape
    L = GDN_CHUNK
    rows = L * min(CHUNKS_PER_STEP, T // L)
    n_seq = SEQS_PER_STEP if B % SEQS_PER_STEP == 0 else 1
    n = GROUP_HEADS * HEAD_DIM
    hp = _per_head_rows(-jnp.exp(A_log), dt_bias, norm_g)
    consts = (hp,) + _chunk_consts(L, rows // L)
    return pl.pallas_call(
        _gdn_kernel,
        grid=(B // n_seq, T // rows),
        in_specs=[pl.BlockSpec((n_seq, rows, width), lambda b, t: (b, t, 0))] + [_const_spec(a) for a in consts],
        out_specs=pl.BlockSpec((n_seq, rows, 256), lambda b, t: (b, t, 0)),
        out_shape=jax.ShapeDtypeStruct((B, T, 256), jnp.float32),
        scratch_shapes=[pltpu.VMEM((n_seq, n, n), jnp.float32)],
        compiler_params=pltpu.CompilerParams(dimension_semantics=("arbitrary", "arbitrary"),
                                             vmem_limit_bytes=VMEM_LIMIT_BYTES),
        name="gated_deltanet",
    )(pc, *consts)


def _mlstm_kernel(pd_ref, hp_ref, hm_ref, tri_ref, ei_ref, ef_ref, lv_ref, o_ref, c_ref, n_ref, m_ref):
    L = MLSTM_CHUNK
    n_seq, rows = pd_ref.shape[0], pd_ref.shape[1]
    f32, bf16 = jnp.float32, jnp.bfloat16
    n_lev = lv_ref.shape[0] - 3

    @pl.when(pl.program_id(1) == 0)
    def _():
        c_ref[...] = jnp.zeros(c_ref.shape, f32)
        n_ref[...] = jnp.zeros(n_ref.shape, f32)
        m_ref[...] = jnp.zeros(m_ref.shape, f32)

    hm = hm_ref[...]
    hmb = hm.astype(bf16)
    eye, incl = lv_ref[n_lev], lv_ref[n_lev + 1]
    head_of_lane = lax.broadcasted_iota(jnp.int32, (L, GROUP_WIDTH), 1) // HEAD_DIM

    def head_max(a):
        out = jnp.zeros(a.shape, f32)
        for h in range(GROUP_HEADS):
            mine = head_of_lane == h
            out = jnp.where(mine, jnp.max(jnp.where(mine, a, NEG), axis=-1, keepdims=True), out)
        return out

    per_seq = rows // L
    parts = []
    for b in range(n_seq):
        tail = pd_ref[b, :, 1024:1152]
        log_i_all = _dot_sel(tail, ei_ref[...])
        log_f = -_softplus(-(_dot_sel(tail, ef_ref[...]) + hp_ref[0:1, :]))
        b_all = _sel_dot(tri_ref[...], log_f)
        for c in range(per_seq):
            sl = slice(c * L, (c + 1) * L)
            q = pd_ref[b, sl, 0:256]
            k = pd_ref[b, sl, 256:512] * HEAD_DIM ** -0.5
            v = pd_ref[b, sl, 512:768]
            bc, log_i = b_all[sl], log_i_all[sl]
            log_w = jnp.where(incl > 0.5, bc - _diag_row(bc - log_i, eye), NEG)
            b_last = bc[L - 1:L, :]
            log_w_end = b_last - bc + log_i
            qk = _nt_dot(q.astype(bf16), _block_diag(k, hmb))
            parts.append((q, k, v, bc, log_w, head_max(log_w), b_last, log_w_end,
                          jnp.max(log_w_end, axis=0, keepdims=True), qk))
    c_state = [c_ref[b] for b in range(n_seq)]
    n_state = [n_ref[b] for b in range(n_seq)]
    m_prev = [m_ref[b] for b in range(n_seq)]
    outs = [[] for _ in range(n_seq)]
    for c in range(per_seq):
        for b in range(n_seq):
            q, k, v, bc, log_w, m_intra, b_last, log_w_end, m_end, qk = parts[b * per_seq + c]
            log_inter = bc + m_prev[b]
            m_t = jnp.maximum(log_inter, m_intra)
            w_inter = jnp.exp(log_inter - m_t)
            s = qk * jnp.exp(log_w - m_t)
            num = w_inter * jnp.dot(q.astype(bf16), c_state[b].astype(bf16), preferred_element_type=f32)
            num = num + jnp.dot(s.astype(bf16), _block_diag(v, hmb), preferred_element_type=f32)
            den = w_inter * _dot_sel(q * n_state[b], hmb, parts=2) + _dot_sel(s, hmb, parts=2)
            outs[b].append(num / jnp.maximum(jnp.abs(den), jnp.exp(-m_t)))
            m_new = jnp.maximum(b_last + m_prev[b], m_end)
            w_old = jnp.exp(b_last + m_prev[b] - m_new)
            kw = k * jnp.exp(log_w_end - m_new)
            c_state[b] = w_old * c_state[b] + _tn_dot(kw.astype(bf16), v.astype(bf16)) * hm
            n_state[b] = w_old * n_state[b] + jnp.sum(kw, axis=0, keepdims=True)
            m_prev[b] = m_new
    for b in range(n_seq):
        c_ref[b] = c_state[b]
        n_ref[b] = n_state[b]
        m_ref[b] = m_prev[b]
        h = jnp.concatenate(outs[b], axis=0)
        o_ref[b] = _head_norm(h, hmb) * hp_ref[1:2, :] * jax.nn.sigmoid(pd_ref[b, :, 768:1024])


def mlstm_pallas(pd, f_bias, norm_g):
    B, T, width = pd.shape
    L = MLSTM_CHUNK
    rows = L * min(CHUNKS_PER_STEP, T // L)
    n_seq = SEQS_PER_STEP if B % SEQS_PER_STEP == 0 else 1
    n = GROUP_HEADS * HEAD_DIM
    consts = (_per_head_rows(f_bias, norm_g),) + _chunk_consts(L, rows // L)
    return pl.pallas_call(
        _mlstm_kernel,
        grid=(B // n_seq, T // rows),
        in_specs=[pl.BlockSpec((n_seq, rows, width), lambda b, t: (b, t, 0))] + [_const_spec(a) for a in consts],
        out_specs=pl.BlockSpec((n_seq, rows, 256), lambda b, t: (b, t, 0)),
        out_shape=jax.ShapeDtypeStruct((B, T, 256), jnp.float32),
        scratch_shapes=[pltpu.VMEM((n_seq, n, n), jnp.float32), pltpu.VMEM((n_seq, 1, 256), jnp.float32),
                        pltpu.VMEM((n_seq, 1, 256), jnp.float32)],
        compiler_params=pltpu.CompilerParams(dimension_semantics=("arbitrary", "arbitrary"),
                                             vmem_limit_bytes=VMEM_LIMIT_BYTES),
        name="mlstm",
    )(pd, *consts)


def kernel(x, c, ada_w, ada_b, norm_g, ffn1_w13, ffn1_w2, ffn2_w13, ffn2_w2, w_in, b_in, q_norm_g, k_norm_g, cmp_pos, cmp_k_w1, cmp_k_w2, cmp_v_w1, cmp_v_w2, t5_table, sc_conv_w, gdn_conv_w, gdn_A_log, gdn_dt_bias, gdn_norm_g, mlstm_f_bias, mlstm_norm_g, mix_norm_g, w_out):
    B, T, D = x.shape
    depth = ada_w.shape[0]
    bf16 = jnp.bfloat16
    x2 = x.reshape(B * T, D)
    tables = nsa_bias_tables(t5_table, T)
    ovt = nsa_overlap_t(T)
    for l in range(depth):
        mod = ada_modulation(c, ada_w[l], ada_b[l][None, :])
        mod = mod.reshape(B, N_SUBLAYERS, 3, 1, D)
        x2 = ffn_half_step(x2, norm_g[l, 0][None, :], mod[:, 0, 0], mod[:, 0, 1], mod[:, 0, 2],
                           ffn1_w13[l].astype(bf16), ffn1_w2[l].astype(bf16), T)
        ws, bs = split_in_weights(w_in[l], b_in[l])
        pa, y_b, pc, pd = in_projection(x2, norm_g[l, 1][None, :], mod[:, 1, 0], mod[:, 1, 1], ws, bs,
                                        sc_conv_w[l], mix_norm_g[l, 1][None, :], gdn_conv_w[l], T)
        pa, pc, pd = (a.reshape(B, T, -1) for a in (pa, pc, pd))
        y_a = nsa_mixer_pallas(pa, q_norm_g[l], k_norm_g[l], cmp_pos[l], cmp_k_w1[l], cmp_k_w2[l],
                               cmp_v_w1[l], cmp_v_w2[l], tables, ovt, mix_norm_g[l, 0][None, :])
        y_c = gated_deltanet_pallas(pc, gdn_A_log[l], gdn_dt_bias[l], gdn_norm_g[l])
        y_d = mlstm_pallas(pd, mlstm_f_bias[l], mlstm_norm_g[l])
        ys = [y.reshape(B * T, GROUP_WIDTH) for y in (y_a, y_b, y_c, y_d)]
        x2 = ffn_half_step(x2, norm_g[l, 2][None, :], mod[:, 2, 0], mod[:, 2, 1], mod[:, 2, 2],
                           ffn2_w13[l].astype(bf16), ffn2_w2[l].astype(bf16), T,
                           mix=(ys, mod[:, 1, 2], w_out[l].astype(bf16)))
    return x2.reshape(B, T, D)
```

```python
import functools
import math

import jax
import jax.numpy as jnp
import numpy as np
from jax import lax
from jax.experimental import pallas as pl
from jax.experimental.pallas import tpu as pltpu

D_MODEL = 1024
HEAD_DIM = 64
GROUP_HEADS = 4
GROUP_WIDTH = GROUP_HEADS * HEAD_DIM
CMP_STRIDE = 16
CMP_BLOCK = 32
SLC_BLOCK = 64
N_SLC = 16
N_LOCAL_SLC = 2
WINDOW = 512
Q_BLOCK = 128
FORCE = 1e6
N_BUCKETS = 32
MAX_DISTANCE = 128
GDN_CHUNK = 64
MLSTM_CHUNK = 64
D_FF = 2816
N_SUBLAYERS = 3
EPS = 1e-6

IN_LAYOUT = (
    ("a_q", 256), ("a_k_cmp", 64), ("a_v_cmp", 64),
    ("a_k_slc", 64), ("a_v_slc", 64), ("a_k_win", 64), ("a_v_win", 64),
    ("a_gate", 12),
    ("b_b", 256), ("b_c", 256), ("b_x", 256),
    ("c_q", 256), ("c_k", 256), ("c_v", 256),
    ("c_beta", 4), ("c_alpha", 4), ("c_z", 256),
    ("d_q", 256), ("d_k", 256), ("d_v", 256),
    ("d_i", 4), ("d_f", 4), ("d_o", 256),
)

VMEM_LIMIT_BYTES = 56 * 1024 * 1024
FFN_TOKEN_TILE = 512
FFN_CHUNK = 256
PROJ_TOKEN_TILE = 512
NEG = -1e30
HIGHEST = lax.Precision.HIGHEST
LOG2E = math.log2(math.e)
V1_ROWS = HEAD_DIM + 16


def _modulated_norm(x, g, scale, shift):
    y = x * lax.rsqrt(jnp.mean(x * x, axis=-1, keepdims=True) + EPS)
    return (y * g) * (1.0 + scale) + shift


def _ada_kernel(c_ref, w_ref, b_ref, o_ref):
    c = c_ref[...]
    cond = c * jax.nn.sigmoid(c)
    o_ref[...] = jnp.dot(cond.astype(jnp.bfloat16), w_ref[...].astype(jnp.bfloat16),
                         preferred_element_type=jnp.float32) + b_ref[...]


def ada_modulation(c, w, b):
    B, D = c.shape
    N = w.shape[1]
    tn = 1152
    return pl.pallas_call(
        _ada_kernel,
        grid=(N // tn,),
        in_specs=[pl.BlockSpec((B, D), lambda j: (0, 0)),
                  pl.BlockSpec((D, tn), lambda j: (0, j)),
                  pl.BlockSpec((1, tn), lambda j: (0, j))],
        out_specs=pl.BlockSpec((B, tn), lambda j: (0, j)),
        out_shape=jax.ShapeDtypeStruct((B, N), jnp.float32),
        name="ada_modulation",
    )(c, w, b)


def _ffn_kernel(x_ref, g_ref, shift_ref, scale_ref, gate_ref, w1_ref, w3_ref, w2_ref, *rest, n_mix):
    o_ref, act_ref = rest[-2:]
    x = x_ref[...]
    if n_mix:
        y_refs, mix_gate_ref, wo_ref = rest[:n_mix], rest[n_mix], rest[n_mix + 1]
        z = None
        for k, y_ref in enumerate(y_refs):
            part = jnp.dot(y_ref[...].astype(jnp.bfloat16), wo_ref[k * GROUP_WIDTH:(k + 1) * GROUP_WIDTH, :],
                           preferred_element_type=jnp.float32)
            z = part if z is None else z + part
        x = x + mix_gate_ref[0] * z
    h = _modulated_norm(x, g_ref[...], scale_ref[0], shift_ref[0]).astype(jnp.bfloat16)
    n_chunks = act_ref.shape[1] // FFN_CHUNK
    for ci in range(n_chunks):
        cs = slice(ci * FFN_CHUNK, (ci + 1) * FFN_CHUNK)
        a = jnp.dot(h, w1_ref[:, cs], preferred_element_type=jnp.float32)
        b = jnp.dot(h, w3_ref[:, cs], preferred_element_type=jnp.float32)
        act_ref[:, cs] = (a * jax.nn.sigmoid(a) * b).astype(jnp.bfloat16)
    y = jnp.dot(act_ref[...], w2_ref[...], preferred_element_type=jnp.float32)
    o_ref[...] = x + (0.5 * gate_ref[0]) * y


def ffn_half_step(x2, g, shift, scale, gate, w13, w2, tokens_per_batch, mix=None):
    M, D = x2.shape
    F = w2.shape[0]
    tm = FFN_TOKEN_TILE
    tiles_per_batch = tokens_per_batch // tm
    resident = dict(pipeline_mode=pl.Buffered(1))
    mod_spec = pl.BlockSpec((1, 1, D), lambda i: (i // tiles_per_batch, 0, 0))
    ys, mix_gate, w_out = mix if mix is not None else ((), None, None)
    mix_args = tuple(ys) + ((mix_gate, w_out) if ys else ())
    mix_specs = [pl.BlockSpec((tm, GROUP_WIDTH), lambda i: (i, 0)) for _ in ys]
    if ys:
        mix_specs += [mod_spec, pl.BlockSpec(w_out.shape, lambda i: (0, 0), **resident)]
    return pl.pallas_call(
        functools.partial(_ffn_kernel, n_mix=len(ys)),
        grid=(M // tm,),
        in_specs=[pl.BlockSpec((tm, D), lambda i: (i, 0)),
                  pl.BlockSpec((1, D), lambda i: (0, 0)),
                  mod_spec, mod_spec, mod_spec,
                  pl.BlockSpec((D, F), lambda i: (0, 0), **resident),
                  pl.BlockSpec((D, F), lambda i: (0, 1), **resident),
                  pl.BlockSpec((F, D), lambda i: (0, 0), **resident)] + mix_specs,
        out_specs=pl.BlockSpec((tm, D), lambda i: (i, 0)),
        out_shape=jax.ShapeDtypeStruct((M, D), jnp.float32),
        scratch_shapes=[pltpu.VMEM((tm, F), jnp.bfloat16)],
        compiler_params=pltpu.CompilerParams(dimension_semantics=("arbitrary",),
                                             vmem_limit_bytes=VMEM_LIMIT_BYTES),
        name="ffn_half_step",
    )(x2, g, shift, scale, gate, w13, w13, w2, *mix_args)


CONV_CARRY_ROWS = 8
CONV_ROW_CHUNK = 128


def _in_proj_kernel(x_ref, g_ref, shift_ref, scale_ref, wa_ref, wb_ref, wc_ref, wd_ref,
                    ba_ref, bb_ref, bc_ref, bd_ref, scw_ref, scg_ref, gcw_ref,
                    oa_ref, yb_ref, oc_ref, od_ref, braw_ref, craw_ref, *, tiles_per_batch):
    f32 = jnp.float32
    rows = x_ref.shape[0]
    hist = CONV_CARRY_ROWS
    first = pl.program_id(0) % tiles_per_batch == 0

    @pl.when(first)
    def _():
        braw_ref[0:hist, :] = jnp.zeros((hist, braw_ref.shape[1]), f32)
        craw_ref[0:hist, :] = jnp.zeros((hist, craw_ref.shape[1]), f32)

    @pl.when(jnp.logical_not(first))
    def _():
        braw_ref[0:hist, :] = braw_ref[rows:rows + hist, :]
        craw_ref[0:hist, :] = craw_ref[rows:rows + hist, :]

    h = _modulated_norm(x_ref[...], g_ref[...], scale_ref[0], shift_ref[0]).astype(jnp.bfloat16)
    project = lambda w_ref, b_ref: jnp.dot(h, w_ref[...], preferred_element_type=f32) + b_ref[...]
    pb = project(wb_ref, bb_ref)
    yb_ref[...] = pb[:, 0:256]
    braw_ref[hist:hist + rows, :] = pb[:, 256:512] * pb[:, 512:768]
    pc = project(wc_ref, bc_ref)
    craw_ref[hist:hist + rows, :] = pc[:, 0:768]
    oc_ref[:, 768:] = pc[:, 768:]
    n_chunks = rows // CONV_ROW_CHUNK
    pieces = []
    for w_ref, b_ref, o_ref in ((wa_ref, ba_ref, oa_ref), (wd_ref, bd_ref, od_ref)):
        half = w_ref.shape[1] // 256 // 2 * 256
        pieces += [(w_ref, b_ref, o_ref, 0, half), (w_ref, b_ref, o_ref, half, w_ref.shape[1])]
    for ci, r0 in enumerate(range(0, rows, CONV_ROW_CHUNK)):
        for w_ref, b_ref, o_ref, c0, c1 in pieces[ci * len(pieces) // n_chunks:(ci + 1) * len(pieces) // n_chunks]:
            o_ref[:, c0:c1] = jnp.dot(h, w_ref[:, c0:c1], preferred_element_type=f32) + b_ref[:, c0:c1]
        out_rows = slice(r0, r0 + CONV_ROW_CHUNK)
        tap = lambda ref, s: ref[pl.ds(hist + r0 - s, CONV_ROW_CHUNK), :]
        conv = sum(tap(braw_ref, s) * scw_ref[2 - s:3 - s, :] for s in range(3))
        y = yb_ref[out_rows, :] * conv
        yb_ref[out_rows, :] = y * lax.rsqrt(jnp.mean(y * y, axis=-1, keepdims=True) + EPS) * scg_ref[...]
        for c0 in range(0, 3 * GROUP_WIDTH, GROUP_WIDTH):
            cols = slice(c0, c0 + GROUP_WIDTH)
            conv = sum(craw_ref[pl.ds(hist + r0 - s, CONV_ROW_CHUNK), cols] * gcw_ref[3 - s:4 - s, cols]
                       for s in range(4))
            oc_ref[out_rows, cols] = conv * jax.nn.sigmoid(conv)


def in_projection(x2, g, shift, scale, ws, bs, sc_conv_w, sc_norm_g, gdn_conv_w, tokens_per_batch):
    M, D = x2.shape
    tm = PROJ_TOKEN_TILE
    tiles_per_batch = tokens_per_batch // tm
    mod_spec = pl.BlockSpec((1, 1, D), lambda i: (i // tiles_per_batch, 0, 0))
    const = lambda a, **kw: pl.BlockSpec(a.shape, lambda i: (0, 0), **kw)
    pad8 = lambda w: jnp.pad(w, ((0, 8 - w.shape[0]), (0, 0)))
    extras = (pad8(sc_conv_w), sc_norm_g, pad8(gdn_conv_w))
    widths = (ws[0].shape[1], GROUP_WIDTH, ws[2].shape[1], ws[3].shape[1])
    return pl.pallas_call(
        functools.partial(_in_proj_kernel, tiles_per_batch=tiles_per_batch),
        grid=(M // tm,),
        in_specs=[pl.BlockSpec((tm, D), lambda i: (i, 0)), const(g), mod_spec, mod_spec]
                 + [const(w, pipeline_mode=pl.Buffered(1)) for w in ws] + [const(b) for b in bs]
                 + [const(e) for e in extras],
        out_specs=[pl.BlockSpec((tm, n), lambda i: (i, 0)) for n in widths],
        out_shape=[jax.ShapeDtypeStruct((M, n), jnp.float32) for n in widths],
        scratch_shapes=[pltpu.VMEM((CONV_CARRY_ROWS + tm, GROUP_WIDTH), jnp.float32),
                        pltpu.VMEM((CONV_CARRY_ROWS + tm, 3 * GROUP_WIDTH), jnp.float32)],
        compiler_params=pltpu.CompilerParams(dimension_semantics=("arbitrary",),
                                             vmem_limit_bytes=VMEM_LIMIT_BYTES),
        name="in_projection",
    )(x2, g, shift, scale, *ws, *bs, *extras)


def _group_columns():
    offs, o = {}, 0
    for n, s in IN_LAYOUT:
        offs[n] = np.arange(o, o + s)
        o += s
    cat = lambda names: np.concatenate([offs[n] for n in names])
    return (cat(["a_q", "a_k_cmp", "a_v_cmp", "a_k_slc", "a_v_slc", "a_k_win", "a_v_win", "a_gate"]),
            cat(["b_b", "b_c", "b_x"]),
            cat(["c_q", "c_k", "c_v", "c_z", "c_beta", "c_alpha"]),
            cat(["d_q", "d_k", "d_v", "d_o", "d_i", "d_f"]))


GROUP_SLAB_WIDTH = (768, 768, 1152, 1152)


def _take_columns(a, idx, width):
    cuts = [0] + [k + 1 for k in range(idx.size - 1) if idx[k + 1] != idx[k] + 1] + [idx.size]
    runs = [a[..., int(idx[s]):int(idx[e - 1]) + 1] for s, e in zip(cuts[:-1], cuts[1:])]
    return jnp.concatenate(runs + [jnp.zeros(a.shape[:-1] + (width - idx.size,), a.dtype)], axis=-1)


def split_in_weights(w, b):
    ws, bs = [], []
    for idx, width in zip(_group_columns(), GROUP_SLAB_WIDTH):
        ws.append(_take_columns(w, idx, width).astype(jnp.bfloat16))
        bs.append(_take_columns(b, idx, width)[None, :])
    return ws, bs


def t5_bucket(dist):
    n = jnp.maximum(dist, 0)
    max_exact = N_BUCKETS // 2
    nf = jnp.maximum(n, 1).astype(jnp.float32)
    large = max_exact + (jnp.log(nf / max_exact) / math.log(MAX_DISTANCE / max_exact)
                         * (N_BUCKETS - max_exact)).astype(jnp.int32)
    large = jnp.minimum(large, N_BUCKETS - 1)
    return jnp.where(n < max_exact, n, large)


def _nt_dot(a, b):
    return lax.dot_general(a, b, (((1,), (1,)), ((), ())), preferred_element_type=jnp.float32)


def _stack_heads(qn):
    lane = lax.broadcasted_iota(jnp.int32, (qn.shape[0], 128), 1)
    lo = lane < HEAD_DIM
    zero = jnp.zeros((), qn.dtype)
    halves = (qn[:, 0:128], qn[:, 128:256])
    return jnp.concatenate([jnp.where(lo if h % 2 == 0 else ~lo, halves[h // 2], zero)
                            for h in range(GROUP_HEADS)], axis=0)


def _nsa_prep_kernel(q_ref, kvs_ref, kvw_ref, qg_ref, kg_ref, seg_ref,
                     qnt_ref, kxs_ref, v1s_ref, kkw_ref, v1w_ref):
    tm = q_ref.shape[0]
    q = q_ref[...]
    ss = jnp.dot(q * q, seg_ref[...], precision=HIGHEST, preferred_element_type=jnp.float32)
    qn = q * lax.rsqrt(ss * (1.0 / HEAD_DIM) + EPS) * qg_ref[...]
    qnt_ref[...] = (qn * LOG2E).T.astype(qnt_ref.dtype)
    lane = lax.broadcasted_iota(jnp.int32, (tm, 128), 1)
    lo = lane < HEAD_DIM

    def split(x):
        xr = pltpu.roll(x, HEAD_DIM, 1)
        ss = jnp.sum(jnp.where(lo, x * x, 0.0), axis=-1, keepdims=True)
        kk = jnp.where(lo, x, xr) * lax.rsqrt(ss * (1.0 / HEAD_DIM) + EPS) * kg_ref[...]
        return kk, jnp.where(lo, xr, 1.0)

    kk, v1 = split(kvs_ref[...])
    tok = pl.program_id(1) * tm + lax.broadcasted_iota(jnp.int32, (tm, 128), 0)
    onehot = jnp.where(lane == tok // SLC_BLOCK, 1.0, 0.0)
    kxs_ref[...] = jnp.concatenate([kk, onehot], axis=1).astype(kxs_ref.dtype)
    v1s_ref[...] = v1.T[0:V1_ROWS].astype(v1s_ref.dtype)
    kk, v1 = split(kvw_ref[...])
    kkw_ref[...] = kk.astype(kkw_ref.dtype)
    v1w_ref[...] = v1.T[0:V1_ROWS].astype(v1w_ref.dtype)


def nsa_prep(pa, qg, kg2, seg):
    B, T, _ = pa.shape
    tm = 512
    bf16 = jnp.bfloat16
    col = lambda w, j: pl.BlockSpec((None, tm, w), lambda b, i: (b, i, j))
    rowblk = lambda h: pl.BlockSpec((None, h, tm), lambda b, i: (b, 0, i))
    const = lambda shape: pl.BlockSpec(shape, lambda b, i: (0, 0))
    return pl.pallas_call(
        _nsa_prep_kernel,
        grid=(B, T // tm),
        in_specs=[col(256, 0), col(128, 3), col(128, 4), const((1, 256)), const((1, 128)), const((256, 256))],
        out_specs=[rowblk(256), col(256, 0), rowblk(V1_ROWS), col(128, 0), rowblk(V1_ROWS)],
        out_shape=[jax.ShapeDtypeStruct((B, 256, T), bf16),
                   jax.ShapeDtypeStruct((B, T, 256), bf16), jax.ShapeDtypeStruct((B, V1_ROWS, T), bf16),
                   jax.ShapeDtypeStruct((B, T, 128), bf16), jax.ShapeDtypeStruct((B, V1_ROWS, T), bf16)],
        name="nsa_prep",
    )(pa, pa, pa, qg, kg2, seg)


def _compress_kernel(kch_ref, vch_ref, plo_ref, phi_ref, w1k_ref, w2k_ref, w1v_ref, w2v_ref, kg_ref,
                     kkc_ref, vct_ref):
    nrow = kch_ref.shape[0]
    half = CMP_STRIDE * HEAD_DIM
    bf16 = jnp.bfloat16
    valid = lax.broadcasted_iota(jnp.int32, (nrow, 128), 0) < nrow - 1

    def comp(ch, w1_ref, w2_ref):
        a = jnp.dot((ch + plo_ref[...]).astype(bf16), w1_ref[0:half, :], preferred_element_type=jnp.float32)
        b = jnp.dot((ch + phi_ref[...]).astype(bf16), w1_ref[half:2 * half, :],
                    preferred_element_type=jnp.float32)
        hid = a + pltpu.roll(b, nrow - 1, 0)
        hid = hid * jax.nn.sigmoid(hid)
        return jnp.dot(hid.astype(bf16), w2_ref[...], preferred_element_type=jnp.float32)

    kc = comp(kch_ref[...], w1k_ref, w2k_ref)
    kc = kc * lax.rsqrt(jnp.mean(kc * kc, axis=-1, keepdims=True) + EPS) * kg_ref[...]
    kkc_ref[...] = jnp.where(valid, kc, 0.0).astype(kkc_ref.dtype)
    vc = comp(vch_ref[...], w1v_ref, w2v_ref)
    vct_ref[...] = jnp.where(valid, vc, 0.0).T[0:HEAD_DIM].astype(vct_ref.dtype)


def nsa_compress(kch, vch, plo, phi, w1k, w2k, w1v, w2v, kg2):
    B, nrow, width = kch.shape
    bf16 = jnp.bfloat16
    full = lambda a: pl.BlockSpec(a.shape, lambda b: (0,) * a.ndim)
    bspec = pl.BlockSpec((None, nrow, width), lambda b: (b, 0, 0))
    ospec = pl.BlockSpec((None, nrow, 128), lambda b: (b, 0, 0))
    return pl.pallas_call(
        _compress_kernel,
        grid=(B,),
        in_specs=[bspec, bspec, full(plo), full(phi), full(w1k), full(w2k), full(w1v), full(w2v), full(kg2)],
        out_specs=[ospec, pl.BlockSpec((None, HEAD_DIM, nrow), lambda b: (b, 0, 0))],
        out_shape=[jax.ShapeDtypeStruct((B, nrow, 128), bf16), jax.ShapeDtypeStruct((B, HEAD_DIM, nrow), bf16)],
        name="nsa_compress",
    )(kch, vch, plo, phi, w1k, w2k, w1v, w2v, kg2)


CMP_TILES_PER_STEP = 4


def _stack_heads_t(qt):
    lo = lax.broadcasted_iota(jnp.int32, (128, qt.shape[1]), 0) < HEAD_DIM
    zero = jnp.zeros((), qt.dtype)
    return jnp.concatenate([jnp.where(lo if h % 2 == 0 else ~lo, qt[128 * (h // 2):128 * (h // 2) + 128], zero)
                            for h in range(GROUP_HEADS)], axis=1)


def _cmp_select_kernel(qnt_ref, kkc_ref, vct_ref, cb_ref, ov_ref, ocmp_ref, pen_ref, sc_ref, *, n_sel):
    i = pl.program_id(1)
    n_tiles = cb_ref.shape[0]
    tq = qnt_ref.shape[1] // n_tiles
    nblk = ov_ref.shape[0]
    f32, bf16 = jnp.float32, jnp.bfloat16
    blk = lax.broadcasted_iota(jnp.int32, (nblk, tq), 0)
    scores = []
    for g in range(n_tiles):
        cols = slice(g * tq, (g + 1) * tq)
        q4t = _stack_heads_t(qnt_ref[:, cols])
        s = jnp.dot(kkc_ref[...], q4t, preferred_element_type=f32) + cb_ref[g]
        m = jnp.max(s, axis=0, keepdims=True)
        e = jnp.where(s > 0.5 * NEG, jnp.exp2(s - m), 0.0)
        p = e * (1.0 / jnp.maximum(jnp.sum(e, axis=0, keepdims=True), 1e-30))
        o = jnp.dot(vct_ref[...], p.astype(bf16), preferred_element_type=f32)
        for h in range(GROUP_HEADS):
            ocmp_ref[h * HEAD_DIM:(h + 1) * HEAD_DIM, cols] = o[:, h * tq:(h + 1) * tq]
        psum = p[:, 0:tq] + p[:, tq:2 * tq] + p[:, 2 * tq:3 * tq] + p[:, 3 * tq:4 * tq]
        hi = psum.astype(bf16)
        lo_part = (psum - hi.astype(f32)).astype(bf16)
        score = (jnp.dot(ov_ref[...], hi, preferred_element_type=f32)
                 + jnp.dot(ov_ref[...], lo_part, preferred_element_type=f32))
        qpos = (i * n_tiles + g) * tq + lax.broadcasted_iota(jnp.int32, (nblk, tq), 1)
        cur = qpos // SLC_BLOCK
        forced = (blk == 0) | ((cur - blk >= 0) & (cur - blk < N_LOCAL_SLC))
        score = jnp.where(forced, FORCE, score)
        score = jnp.where(blk <= cur, score, -FORCE)
        sc_ref[g] = score
        scores.append(score)
    blocks_per_tile = tq // SLC_BLOCK

    def rank_step(t, cnts):
        out = []
        for g in range(n_tiles):
            cnt = cnts[g]
            for u in range(blocks_per_tile):
                jp = t * blocks_per_tile + u
                row = sc_ref[g, pl.ds(jp, 1), :]
                tie = jnp.where(blk > jp, 1.0, 0.0)
                cnt = cnt + jnp.where(row > scores[g], 1.0, jnp.where(row == scores[g], tie, 0.0))
            out.append(cnt)
        return tuple(out)

    zeros = tuple(jnp.zeros((nblk, tq), f32) for _ in range(n_tiles))
    cnts = lax.fori_loop(0, (i + 1) * n_tiles, rank_step, zeros)
    for g in range(n_tiles):
        pen = jnp.where(cnts[g] < n_sel, 0.0, NEG)
        if nblk < 128:
            pen = jnp.concatenate([pen, jnp.zeros((128 - nblk, tq), f32)], axis=0)
        pen_ref[g] = pen.astype(pen_ref.dtype)


def nsa_cmp_select(qnt, kkc, vct, cbias, ov, n_sel):
    B, _, T = qnt.shape
    nrow = kkc.shape[1]
    nblk = ov.shape[0]
    tq = Q_BLOCK
    g = min(CMP_TILES_PER_STEP, T // tq)
    return pl.pallas_call(
        functools.partial(_cmp_select_kernel, n_sel=n_sel),
        grid=(B, T // (g * tq)),
        in_specs=[pl.BlockSpec((None, 256, g * tq), lambda b, i: (b, 0, i)),
                  pl.BlockSpec((None, nrow, 128), lambda b, i: (b, 0, 0)),
                  pl.BlockSpec((None, HEAD_DIM, nrow), lambda b, i: (b, 0, 0)),
                  pl.BlockSpec((g, nrow, 4 * tq), lambda b, i: (i, 0, 0)),
                  pl.BlockSpec((nblk, nrow), lambda b, i: (0, 0))],
        out_specs=[pl.BlockSpec((None, 256, g * tq), lambda b, i: (b, 0, i)),
                   pl.BlockSpec((None, g, 128, tq), lambda b, i: (b, i, 0, 0))],
        out_shape=[jax.ShapeDtypeStruct((B, 256, T), jnp.float32),
                   jax.ShapeDtypeStruct((B, T // tq, 128, tq), jnp.bfloat16)],
        scratch_shapes=[pltpu.VMEM((g, nblk, tq), jnp.float32)],
        name="nsa_cmp_select",
    )(qnt, kkc, vct, cbias, ov)


NSA_KEY_TILE = 256
NSA_QUERY_TILE = 256


def _nsa_attn_kernel(qnt_ref, pen_ref, ocmp_ref, gate_ref, kxs_ref, v1s_ref, kkw_ref, v1w_ref,
                     tabs_ref, tabw_ref, mixg_ref, o_ref, acc_ref, m_ref, sa_ref, sb_ref):
    i = pl.program_id(1)
    tq = o_ref.shape[0]
    kt = tabs_ref.shape[1]
    f32, bf16 = jnp.float32, jnp.bfloat16
    q4t = _stack_heads_t(qnt_ref[...])
    pen = jnp.concatenate([pen_ref[t] for t in range(pen_ref.shape[0])], axis=1)
    rhs_sel = jnp.concatenate([q4t, jnp.concatenate([pen] * GROUP_HEADS, axis=1)], axis=0)

    def branch(rhs, k_ref, v_ref, tab_ref, first_tile, last_tile):
        n_tab = tab_ref.shape[0] - 1
        m_ref[...] = jnp.full(m_ref.shape, NEG, f32)
        acc_ref[...] = jnp.zeros(acc_ref.shape, f32)

        def key_offset(j):
            return pl.multiple_of(jnp.minimum(j, last_tile) * kt, kt)

        def scores(j):
            delta = jnp.where(j > last_tile, n_tab, jnp.minimum(i - j * (kt // tq), n_tab - 1))
            return jnp.dot(k_ref[pl.ds(key_offset(j), kt), :], rhs, preferred_element_type=f32) + tab_ref[delta]

        def absorb(s, j):
            m_old = m_ref[...]
            m_new = jnp.maximum(m_old, jnp.max(s, axis=0, keepdims=True))
            p = jnp.exp2(s - m_new)
            pv =jnp.dot(v_ref[:, pl.ds(key_offset(j), kt)], p.astype(bf16), preferred_element_type=f32)
            acc_ref[...] = jnp.exp2(m_old - m_new) * acc_ref[...] + pv
            m_ref[...] = m_new

        sa_ref[...] = scores(first_tile)

        def body(t, carry):
            j = first_tile + 2 * t
            sb_ref[...] = scores(j + 1)
            absorb(sa_ref[...], j)
            sa_ref[...] = scores(j + 2)
            absorb(sb_ref[...], j + 1)
            return carry

        lax.fori_loop(0, (last_tile - first_tile) // 2 + 1, body, 0)
        acc = acc_ref[...]
        return acc[0:HEAD_DIM] / jnp.maximum(acc[HEAD_DIM:HEAD_DIM + 1], 1e-30)

    def window_branch(rhs):
        n_tab = tabw_ref.shape[0] - 1
        m = jnp.full((1, rhs.shape[1]), NEG, f32)
        acc = jnp.zeros((V1_ROWS, rhs.shape[1]), f32)
        for d in range(n_tab):
            j = i - d
            off = pl.multiple_of(jnp.maximum(j, 0) * kt, kt)
            s = jnp.dot(kkw_ref[pl.ds(off, kt), :], rhs, preferred_element_type=f32)
            s = s + tabw_ref[jnp.where(j < 0, n_tab, d)]
            m_new = jnp.maximum(m, jnp.max(s, axis=0, keepdims=True))
            p = jnp.exp2(s - m_new)
            acc = jnp.exp2(m - m_new) * acc +jnp.dot(v1w_ref[:, pl.ds(off, kt)], p.astype(bf16),
                                                     preferred_element_type=f32)
            m = m_new
        return acc[0:HEAD_DIM] / jnp.maximum(acc[HEAD_DIM:HEAD_DIM + 1], 1e-30)

    last = (i * tq) // kt
    if kt == tq:
        o_w = window_branch(q4t)
    else:
        o_w = branch(q4t, kkw_ref, v1w_ref, tabw_ref, (jnp.maximum(i - WINDOW // tq, 0) * tq) // kt, last)
    o_s = branch(rhs_sel, kxs_ref, v1s_ref, tabs_ref, 0, last)
    gt = jax.nn.sigmoid(gate_ref[...]).T
    oct_ = ocmp_ref[...]
    parts = []
    for h in range(GROUP_HEADS):
        cols = slice(h * tq, (h + 1) * tq)
        parts.append(gt[3 * h:3 * h + 1] * oct_[h * HEAD_DIM:(h + 1) * HEAD_DIM]
                     + gt[3 * h + 1:3 * h + 2] * o_s[:, cols] + gt[3 * h + 2:3 * h + 3] * o_w[:, cols])
    yt = jnp.concatenate(parts, axis=0)
    yt = yt * lax.rsqrt(jnp.mean(yt * yt, axis=0, keepdims=True) + EPS) * mixg_ref[...]
    o_ref[...] = yt.T


def nsa_attention(qnt, pen, ocmp, pa, kxs, v1s, kkw, v1w, tab_s, tab_w, mixg):
    B, _, T = qnt.shape
    tq = min(NSA_QUERY_TILE, T)
    tile = lambda w, j: pl.BlockSpec((None, tq, w), lambda b, i: (b, i, j))
    whole = lambda w: pl.BlockSpec((None, T, w), lambda b, i: (b, 0, 0))
    whole_t = pl.BlockSpec((None, V1_ROWS, T), lambda b, i: (b, 0, 0))
    full = lambda a: pl.BlockSpec(a.shape, lambda b, i: (0,) * a.ndim)
    return pl.pallas_call(
        _nsa_attn_kernel,
        grid=(B, T // tq),
        in_specs=[pl.BlockSpec((None, 256, tq), lambda b, i: (b, 0, i)),
                  pl.BlockSpec((None, tq // Q_BLOCK, 128, Q_BLOCK), lambda b, i: (b, i, 0, 0)),
                  pl.BlockSpec((None, 256, tq), lambda b, i: (b, 0, i)), tile(128, 5),
                  whole(256), whole_t, whole(128), whole_t, full(tab_s), full(tab_w), full(mixg)],
        out_specs=tile(256, 0),
        out_shape=jax.ShapeDtypeStruct((B, T, 256), jnp.float32),
        scratch_shapes=[pltpu.VMEM((V1_ROWS, GROUP_HEADS * tq), jnp.float32),
                        pltpu.VMEM((1, GROUP_HEADS * tq), jnp.float32),
                        pltpu.VMEM((tab_s.shape[1], GROUP_HEADS * tq), jnp.float32),
                        pltpu.VMEM((tab_s.shape[1], GROUP_HEADS * tq), jnp.float32)],
        compiler_params=pltpu.CompilerParams(dimension_semantics=("arbitrary", "arbitrary"),
                                             vmem_limit_bytes=VMEM_LIMIT_BYTES),
        name="nsa_attention",
    )(qnt, pen, ocmp, pa, kxs, v1s, kkw, v1w, tab_s, tab_w, mixg)


def _toeplitz(wr, rows, cols):
    H, n = wr.shape
    flat = jnp.tile(jnp.pad(wr, ((0, 0), (0, 1))), (1, rows))[:, :rows * n]
    return flat.reshape(H, rows, n)[:, :, rows - 1:rows - 1 + cols]


def nsa_bias_tables(t5_table, T):
    H = GROUP_HEADS
    tq = Q_BLOCK
    ta = min(NSA_QUERY_TILE, T)
    kt = min(NSA_KEY_TILE, T)
    nrow = T // CMP_STRIDE
    dmax = WINDOW + 2 * ta + kt
    onehot = (t5_bucket(jnp.arange(dmax))[:, None] == jnp.arange(N_BUCKETS)[None, :]).astype(jnp.float32)
    by_dist = jnp.dot(onehot, t5_table, precision=HIGHEST).T
    neg = lambda n: jnp.full((H, n), NEG, jnp.float32)
    rev = by_dist[:, ::-1] * LOG2E
    f_s = jnp.concatenate([rev, neg(kt - 1)], axis=1)
    f_w = jnp.concatenate([neg(dmax - WINDOW), rev[:, dmax - WINDOW:], neg(kt - 1)], axis=1)
    span = ta + kt - 1
    total = dmax + kt - 1

    def tiles(f, n):
        out = [_toeplitz(f[:, total - k * ta - span:total - k * ta], ta, kt).reshape(H * ta, kt).T for k in range(n)]
        return jnp.stack(out + [jnp.full((kt, H * ta), NEG, jnp.float32)])

    n_s = -(-(kt + MAX_DISTANCE) // ta) + 1
    n_w = WINDOW // ta + kt // ta
    tab_s, tab_w = tiles(f_s, n_s), tiles(f_w, n_w)
    per_tile = tq // CMP_STRIDE
    d0 = -CMP_STRIDE * (nrow - 1) - (CMP_BLOCK - 1)
    n16 = 2 * nrow + per_tile - 1
    n_pos = CMP_STRIDE * n16 + d0
    far = jnp.broadcast_to(by_dist[:, -1:], (H, max(n_pos - dmax, 0)))
    w16 = jnp.concatenate([neg(-d0), by_dist[:, :n_pos] * LOG2E, far * LOG2E], axis=1).reshape(H, n16, CMP_STRIDE)
    w16 = w16[:, ::-1, :]
    base = jnp.stack([w16[:, per_tile - 1 - a:per_tile - 1 - a + 2 * nrow, :] for a in range(per_tile)], axis=1)
    base = base.transpose(0, 1, 3, 2).reshape(H * tq, 2 * nrow).T
    cbias = jnp.stack([base[nrow - per_tile * i: 2 * nrow - per_tile * i, :] for i in range(T // tq)])
    return tab_s, tab_w, cbias


def nsa_overlap_t(T):
    nrow = T // CMP_STRIDE
    nblk = T // SLC_BLOCK
    ci = np.arange(nrow)[None, :]
    bj = np.arange(nblk)[:, None]
    ov = (ci * CMP_STRIDE < (bj + 1) * SLC_BLOCK) & (ci * CMP_STRIDE + CMP_BLOCK > bj * SLC_BLOCK)
    ov = ov & (ci < nrow - 1)
    return jnp.asarray(ov.astype(np.float32), dtype=jnp.bfloat16)


def nsa_mixer_pallas(pa, q_norm_g, k_norm_g, cmp_pos, cmp_k_w1, cmp_k_w2, cmp_v_w1, cmp_v_w2,
                     tables, ovt, mixg):
    B, T, _ = pa.shape
    bf16 = jnp.bfloat16
    tab_s, tab_w, cbias = tables
    qg = (jnp.tile(q_norm_g, GROUP_HEADS) * HEAD_DIM ** -0.5)[None, :]
    kg2 = jnp.tile(k_norm_g, 2)[None, :]
    qnt, kxs, v1s, kkw, v1w = nsa_prep(pa, qg, kg2, _head_mask())
    nrow = T // CMP_STRIDE
    kch = pa[:, :, 256:320].reshape(B, nrow, CMP_STRIDE * HEAD_DIM)
    vch = pa[:, :, 320:384].reshape(B, nrow, CMP_STRIDE * HEAD_DIM)
    plo = cmp_pos[:CMP_STRIDE].reshape(1, -1)
    phi = cmp_pos[CMP_STRIDE:].reshape(1, -1)
    dup = lambda w: jnp.concatenate([w, w], axis=1).astype(bf16)
    kkc, vct = nsa_compress(kch, vch, plo, phi, cmp_k_w1.astype(bf16), dup(cmp_k_w2),
                            cmp_v_w1.astype(bf16), dup(cmp_v_w2), kg2)
    n_sel = min(N_SLC, T // SLC_BLOCK)
    ocmp, pen = nsa_cmp_select(qnt, kkc, vct, cbias, ovt, n_sel)
    mixg_col = jnp.broadcast_to(mixg.reshape(-1, 1), (GROUP_WIDTH, min(NSA_QUERY_TILE, T)))
    return nsa_attention(qnt, pen, ocmp, pa, kxs, v1s, kkw, v1w, tab_s, tab_w, mixg_col)


CHUNKS_PER_STEP = 4
SEQS_PER_STEP = 2


def _tn_dot(a, b):
    return lax.dot_general(a, b, (((0,), (0,)), ((), ())), preferred_element_type=jnp.float32)


def _split_bf16(a, parts):
    out, rest = [], a
    for _ in range(parts):
        piece = rest.astype(jnp.bfloat16)
        out.append(piece)
        rest = rest - piece.astype(jnp.float32)
    return out


def _dot_sel(a, sel, parts=3):
    return sum(jnp.dot(p, sel, preferred_element_type=jnp.float32) for p in _split_bf16(a, parts))


def _sel_dot(sel, a, parts=3):
    return sum(jnp.dot(sel, p, preferred_element_type=jnp.float32) for p in _split_bf16(a, parts))


def _softplus(x):
    return jnp.maximum(x, 0.0) + jnp.log(1.0 + jnp.exp(-jnp.abs(x)))


def _block_diag(a, hmb):
    return jnp.concatenate([a.astype(jnp.bfloat16)] * GROUP_HEADS, axis=0) * hmb


def _diag_row(a, eye):
    return jnp.sum(a * eye, axis=0, keepdims=True)


def _head_norm(o, hmb):
    return o * lax.rsqrt(_dot_sel(o * o, hmb, parts=2) * (1.0 / HEAD_DIM) + EPS)


def _head_l2(o, hmb):
    return o * lax.rsqrt(_dot_sel(o * o, hmb, parts=2) + EPS)


def _xform_masks(L):
    i = np.arange(L)[:, None]
    j = np.arange(L)[None, :]
    masks = []
    s = 1
    while s < L:
        masks.append((i // (2 * s) == j // (2 * s)) & (i % (2 * s) >= s) & (j % (2 * s) < s))
        s *= 2
    masks += [i == j, j <= i, j < i]
    return jnp.asarray(np.tile(np.stack(masks).astype(np.float32), (1, 1, GROUP_HEADS)))


def _expand_matrix(first_lane):
    e = np.zeros((128, GROUP_WIDTH), np.float32)
    for h in range(GROUP_HEADS):
        e[first_lane + h, h * HEAD_DIM:(h + 1) * HEAD_DIM] = 1.0
    return jnp.asarray(e, jnp.bfloat16)


def _head_mask():
    return jnp.asarray(np.kron(np.eye(GROUP_HEADS), np.ones((HEAD_DIM, HEAD_DIM))), jnp.float32)


def _per_head_rows(*vecs):
    rows = [jnp.repeat(v, HEAD_DIM) if v.shape[0] == GROUP_HEADS else jnp.tile(v, GROUP_HEADS) for v in vecs]
    return jnp.pad(jnp.stack(rows), ((0, 8 - len(rows)), (0, 0)))


def _chunk_consts(L, n_chunks):
    tri = np.kron(np.eye(n_chunks), np.tril(np.ones((L, L)))).astype(np.float32)
    return (_head_mask(), jnp.asarray(tri, jnp.bfloat16), _expand_matrix(0), _expand_matrix(GROUP_HEADS),
            _xform_masks(L))


def _const_spec(a):
    return pl.BlockSpec(a.shape, lambda b, t: (0,) * a.ndim)


def _gdn_kernel(pc_ref, hp_ref, hm_ref, tri_ref, eb_ref, ea_ref, lv_ref, o_ref, s_ref):
    L = GDN_CHUNK
    n_seq, rows = pc_ref.shape[0], pc_ref.shape[1]
    f32, bf16 = jnp.float32, jnp.bfloat16
    n_lev = lv_ref.shape[0] - 3

    @pl.when(pl.program_id(1) == 0)
    def _():
        s_ref[...] = jnp.zeros(s_ref.shape, f32)

    hm = hm_ref[...]
    hmb = hm.astype(bf16)
    eye, incl, strict = lv_ref[n_lev], lv_ref[n_lev + 1], lv_ref[n_lev + 2]

    q_c, k_c, v_c, beta_c, gc_c = [], [], [], [], []
    for b in range(n_seq):
        qkv = pc_ref[b, :, 0:768]
        q_all = _head_l2(qkv[:, 0:256], hmb) * HEAD_DIM ** -0.5
        k_all = _head_l2(qkv[:, 256:512], hmb)
        tail = pc_ref[b, :, 1024:1152]
        beta_all = jax.nn.sigmoid(_dot_sel(tail, eb_ref[...]))
        g = hp_ref[0:1, :] * _softplus(_dot_sel(tail, ea_ref[...]) + hp_ref[1:2, :])
        gc_all = _sel_dot(tri_ref[...], g)
        for c in range(rows // L):
            sl = slice(c * L, (c + 1) * L)
            q_c.append(q_all[sl])
            k_c.append(k_all[sl])
            v_c.append(qkv[sl, 512:768])
            beta_c.append(beta_all[sl])
            gc_c.append(gc_all[sl])
    chains = range(len(q_c))
    egc = [jnp.exp(gc) for gc in gc_c]
    kb = [k_c[i] * beta_c[i] for i in chains]
    a_mat, qk = [], []
    for i in chains:
        seg = jnp.exp(jnp.where(incl > 0.5, gc_c[i] - _diag_row(gc_c[i], eye), NEG))
        k_bd = _block_diag(k_c[i], hmb)
        a_mat.append(_nt_dot(kb[i].astype(bf16), k_bd) * seg * strict)
        qk.append((_nt_dot(q_c[i].astype(bf16), k_bd) * seg).astype(bf16))
    t_inv = [eye - a * lv_ref[0] for a in a_mat]
    for lev in range(1, n_lev):
        te = [jnp.dot(t_inv[i].astype(bf16), _block_diag(a_mat[i] * lv_ref[lev], hmb),
                      preferred_element_type=f32) for i in chains]
        t_inv = [t_inv[i] - jnp.dot(te[i].astype(bf16), _block_diag(t_inv[i], hmb), preferred_element_type=f32)
                 for i in chains]
    parts = []
    for i in chains:
        tb = t_inv[i].astype(bf16)
        u = jnp.dot(tb, _block_diag(v_c[i] * beta_c[i], hmb), preferred_element_type=f32)
        w = jnp.dot(tb, _block_diag(kb[i] * egc[i], hmb), preferred_element_type=f32)
        k_dec = k_c[i] * jnp.exp(gc_c[i][L - 1:L, :] - gc_c[i])
        parts.append((u, w.astype(bf16), (q_c[i] * egc[i]).astype(bf16), qk[i], k_dec.astype(bf16),
                      egc[i][L - 1:L, :]))
    per_seq = rows // L
    state = [s_ref[b] for b in range(n_seq)]
    outs = [[] for _ in range(n_seq)]
    for c in range(per_seq):
        for b in range(n_seq):
            u, w, q_dec, qk_c, k_dec, g_tot = parts[b * per_seq + c]
            sb = state[b].astype(bf16)
            v_new = u - jnp.dot(w, sb, preferred_element_type=f32)
            outs[b].append(jnp.dot(q_dec, sb, preferred_element_type=f32)
                           + jnp.dot(qk_c, _block_diag(v_new, hmb), preferred_element_type=f32))
            state[b] = state[b] * g_tot + _tn_dot(k_dec, v_new.astype(bf16)) * hm
    for b in range(n_seq):
        s_ref[b] = state[b]
        o = jnp.concatenate(outs[b], axis=0)
        z = pc_ref[b, :, 768:1024]
        o_ref[b] = _head_norm(o, hmb) * hp_ref[2:3, :] * (z * jax.nn.sigmoid(z))


def gated_deltanet_pallas(pc, A_log, dt_bias, norm_g):
    B, T, width = pc.shape
    L = GDN_CHUNK
    rows = L * min(CHUNKS_PER_STEP, T // L)
    n_seq = SEQS_PER_STEP if B % SEQS_PER_STEP == 0 else 1
    n = GROUP_HEADS * HEAD_DIM
    hp = _per_head_rows(-jnp.exp(A_log), dt_bias, norm_g)
    consts = (hp,) + _chunk_consts(L, rows // L)
    return pl.pallas_call(
        _gdn_kernel,
        grid=(B // n_seq, T // rows),
        in_specs=[pl.BlockSpec((n_seq, rows, width), lambda b, t: (b, t, 0))] + [_const_spec(a) for a in consts],
        out_specs=pl.BlockSpec((n_seq, rows, 256), lambda b, t: (b, t, 0)),
        out_shape=jax.ShapeDtypeStruct((B, T, 256), jnp.float32),
        scratch_shapes=[pltpu.VMEM((n_seq, n, n), jnp.float32)],
        compiler_params=pltpu.CompilerParams(dimension_semantics=("arbitrary", "arbitrary"),
                                             vmem_limit_bytes=VMEM_LIMIT_BYTES),
        name="gated_deltanet",
    )(pc, *consts)


def _mlstm_kernel(pd_ref, hp_ref, hm_ref, tri_ref, ei_ref, ef_ref, lv_ref, o_ref, c_ref, n_ref, m_ref):
    L = MLSTM_CHUNK
    n_seq, rows = pd_ref.shape[0], pd_ref.shape[1]
    f32, bf16 = jnp.float32, jnp.bfloat16
    n_lev = lv_ref.shape[0] - 3

    @pl.when(pl.program_id(1) == 0)
    def _():
        c_ref[...] = jnp.zeros(c_ref.shape, f32)
        n_ref[...] = jnp.zeros(n_ref.shape, f32)
        m_ref[...] = jnp.zeros(m_ref.shape, f32)

    hm = hm_ref[...]
    hmb = hm.astype(bf16)
    eye, incl = lv_ref[n_lev], lv_ref[n_lev + 1]
    head_of_lane = lax.broadcasted_iota(jnp.int32, (L, GROUP_WIDTH), 1) // HEAD_DIM

    def head_max(a):
        out = jnp.zeros(a.shape, f32)
        for h in range(GROUP_HEADS):
            mine = head_of_lane == h
            out = jnp.where(mine, jnp.max(jnp.where(mine, a, NEG), axis=-1, keepdims=True), out)
        return out

    per_seq = rows // L
    parts = []
    for b in range(n_seq):
        tail = pd_ref[b, :, 1024:1152]
        log_i_all = _dot_sel(tail, ei_ref[...])
        log_f = -_softplus(-(_dot_sel(tail, ef_ref[...]) + hp_ref[0:1, :]))
        b_all = _sel_dot(tri_ref[...], log_f)
        for c in range(per_seq):
            sl = slice(c * L, (c + 1) * L)
            q = pd_ref[b, sl, 0:256]
            k = pd_ref[b, sl, 256:512] * HEAD_DIM ** -0.5
            v = pd_ref[b, sl, 512:768]
            bc, log_i = b_all[sl], log_i_all[sl]
            log_w = jnp.where(incl > 0.5, bc - _diag_row(bc - log_i, eye), NEG)
            b_last = bc[L - 1:L, :]
            log_w_end = b_last - bc + log_i
            qk = _nt_dot(q.astype(bf16), _block_diag(k, hmb))
            parts.append((q, k, v, bc, log_w, head_max(log_w), b_last, log_w_end,
                          jnp.max(log_w_end, axis=0, keepdims=True), qk))
    c_state = [c_ref[b] for b in range(n_seq)]
    n_state = [n_ref[b] for b in range(n_seq)]
    m_prev = [m_ref[b] for b in range(n_seq)]
    outs = [[] for _ in range(n_seq)]
    for c in range(per_seq):
        for b in range(n_seq):
            q, k, v, bc, log_w, m_intra, b_last, log_w_end, m_end, qk = parts[b * per_seq + c]
            log_inter = bc + m_prev[b]
            m_t = jnp.maximum(log_inter, m_intra)
            w_inter = jnp.exp(log_inter - m_t)
            s = qk * jnp.exp(log_w - m_t)
            num = w_inter * jnp.dot(q.astype(bf16), c_state[b].astype(bf16), preferred_element_type=f32)
            num = num + jnp.dot(s.astype(bf16), _block_diag(v, hmb), preferred_element_type=f32)
            den = w_inter * _dot_sel(q * n_state[b], hmb, parts=2) + _dot_sel(s, hmb, parts=2)
            outs[b].append(num / jnp.maximum(jnp.abs(den), jnp.exp(-m_t)))
            m_new = jnp.maximum(b_last + m_prev[b], m_end)
            w_old = jnp.exp(b_last + m_prev[b] - m_new)
            kw = k * jnp.exp(log_w_end - m_new)
            c_state[b] = w_old * c_state[b] + _tn_dot(kw.astype(bf16), v.astype(bf16)) * hm
            n_state[b] = w_old * n_state[b] + jnp.sum(kw, axis=0, keepdims=True)
            m_prev[b] = m_new
    for b in range(n_seq):
        c_ref[b] = c_state[b]
        n_ref[b] = n_state[b]
        m_ref[b] = m_prev[b]
        h = jnp.concatenate(outs[b], axis=0)
        o_ref[b] = _head_norm(h, hmb) * hp_ref[1:2, :] * jax.nn.sigmoid(pd_ref[b, :, 768:1024])


def mlstm_pallas(pd, f_bias, norm_g):
    B, T, width = pd.shape
    L = MLSTM_CHUNK
    rows = L * min(CHUNKS_PER_STEP, T // L)
    n_seq = SEQS_PER_STEP if B % SEQS_PER_STEP == 0 else 1
    n = GROUP_HEADS * HEAD_DIM
    consts = (_per_head_rows(f_bias, norm_g),) + _chunk_consts(L, rows // L)
    return pl.pallas_call(
        _mlstm_kernel,
        grid=(B // n_seq, T // rows),
        in_specs=[pl.BlockSpec((n_seq, rows, width), lambda b, t: (b, t, 0))] + [_const_spec(a) for a in consts],
        out_specs=pl.BlockSpec((n_seq, rows, 256), lambda b, t: (b, t, 0)),
        out_shape=jax.ShapeDtypeStruct((B, T, 256), jnp.float32),
        scratch_shapes=[pltpu.VMEM((n_seq, n, n), jnp.float32), pltpu.VMEM((n_seq, 1, 256), jnp.float32),
                        pltpu.VMEM((n_seq, 1, 256), jnp.float32)],
        compiler_params=pltpu.CompilerParams(dimension_semantics=("arbitrary", "arbitrary"),
                                             vmem_limit_bytes=VMEM_LIMIT_BYTES),
        name="mlstm",
    )(pd, *consts)


def kernel(x, c, ada_w, ada_b, norm_g, ffn1_w13, ffn1_w2, ffn2_w13, ffn2_w2, w_in, b_in, q_norm_g, k_norm_g, cmp_pos, cmp_k_w1, cmp_k_w2, cmp_v_w1, cmp_v_w2, t5_table, sc_conv_w, gdn_conv_w, gdn_A_log, gdn_dt_bias, gdn_norm_g, mlstm_f_bias, mlstm_norm_g, mix_norm_g, w_out):
    B, T, D = x.shape
    depth = ada_w.shape[0]
    bf16 = jnp.bfloat16
    x2 = x.reshape(B * T, D)
    tables = nsa_bias_tables(t5_table, T)
    ovt = nsa_overlap_t(T)
    for l in range(depth):
        mod = ada_modulation(c, ada_w[l], ada_b[l][None, :])
        mod = mod.reshape(B, N_SUBLAYERS, 3, 1, D)
        x2 = ffn_half_step(x2, norm_g[l, 0][None, :], mod[:, 0, 0], mod[:, 0, 1], mod[:, 0, 2],
                           ffn1_w13[l].astype(bf16), ffn1_w2[l].astype(bf16), T)
        ws, bs = split_in_weights(w_in[l], b_in[l])
        pa, y_b, pc, pd = in_projection(x2, norm_g[l, 1][None, :], mod[:, 1, 0], mod[:, 1, 1], ws, bs,
                                        sc_conv_w[l], mix_norm_g[l, 1][None, :], gdn_conv_w[l], T)
        pa, pc, pd = (a.reshape(B, T, -1) for a in (pa, pc, pd))
        y_a = nsa_mixer_pallas(pa, q_norm_g[l], k_norm_g[l], cmp_pos[l], cmp_k_w1[l], cmp_k_w2[l],
                               cmp_v_w1[l], cmp_v_w2[l], tables, ovt, mix_norm_g[l, 0][None, :])
        y_c = gated_deltanet_pallas(pc, gdn_A_log[l], gdn_dt_bias[l], gdn_norm_g[l])
        y_d = mlstm_pallas(pd, mlstm_f_bias[l], mlstm_norm_g[l])
        ys = [y.reshape(B * T, GROUP_WIDTH) for y in (y_a, y_b, y_c, y_d)]
        x2 = ffn_half_step(x2, norm_g[l, 2][None, :], mod[:, 2, 0], mod[:, 2, 1], mod[:, 2, 2],
                           ffn2_w13[l].astype(bf16), ffn2_w2[l].astype(bf16), T,
                           mix=(ys, mod[:, 1, 2], w_out[l].astype(bf16)))
    return x2.reshape(B, T, D)
```

```python
import functools
import math

import jax
import jax.numpy as jnp
import numpy as np
from jax import lax
from jax.experimental import pallas as pl
from jax.experimental.pallas import tpu as pltpu

D_MODEL = 1024
HEAD_DIM = 64
GROUP_HEADS = 4
GROUP_WIDTH = GROUP_HEADS * HEAD_DIM
CMP_STRIDE = 16
CMP_BLOCK = 32
SLC_BLOCK = 64
N_SLC = 16
N_LOCAL_SLC = 2
WINDOW = 512
Q_BLOCK = 128
FORCE = 1e6
N_BUCKETS = 32
MAX_DISTANCE = 128
GDN_CHUNK = 64
MLSTM_CHUNK = 64
D_FF = 2816
N_SUBLAYERS = 3
EPS = 1e-6

IN_LAYOUT = (
    ("a_q", 256), ("a_k_cmp", 64), ("a_v_cmp", 64),
    ("a_k_slc", 64), ("a_v_slc", 64), ("a_k_win", 64), ("a_v_win", 64),
    ("a_gate", 12),
    ("b_b", 256), ("b_c", 256), ("b_x", 256),
    ("c_q", 256), ("c_k", 256), ("c_v", 256),
    ("c_beta", 4), ("c_alpha", 4), ("c_z", 256),
    ("d_q", 256), ("d_k", 256), ("d_v", 256),
    ("d_i", 4), ("d_f", 4), ("d_o", 256),
)

VMEM_LIMIT_BYTES = 56 * 1024 * 1024
FFN_TOKEN_TILE = 512
FFN_CHUNK = 256
PROJ_TOKEN_TILE = 512
NEG = -1e30
HIGHEST = lax.Precision.HIGHEST
LOG2E = math.log2(math.e)
V1_ROWS = HEAD_DIM + 16


def _modulated_norm(x, g, scale, shift):
    y = x * lax.rsqrt(jnp.mean(x * x, axis=-1, keepdims=True) + EPS)
    return (y * g) * (1.0 + scale) + shift


def _ada_kernel(c_ref, w_ref, b_ref, o_ref):
    c = c_ref[...]
    cond = c * jax.nn.sigmoid(c)
    o_ref[...] = jnp.dot(cond.astype(jnp.bfloat16), w_ref[...].astype(jnp.bfloat16),
                         preferred_element_type=jnp.float32) + b_ref[...]


def ada_modulation(c, w, b):
    B, D = c.shape
    N = w.shape[1]
    tn = 1152
    return pl.pallas_call(
        _ada_kernel,
        grid=(N // tn,),
        in_specs=[pl.BlockSpec((B, D), lambda j: (0, 0)),
                  pl.BlockSpec((D, tn), lambda j: (0, j)),
                  pl.BlockSpec((1, tn), lambda j: (0, j))],
        out_specs=pl.BlockSpec((B, tn), lambda j: (0, j)),
        out_shape=jax.ShapeDtypeStruct((B, N), jnp.float32),
        name="ada_modulation",
    )(c, w, b)


def _ffn_kernel(x_ref, g_ref, shift_ref, scale_ref, gate_ref, w1_ref, w3_ref, w2_ref, *rest, n_mix):
    o_ref, act_ref = rest[-2:]
    x = x_ref[...]
    if n_mix:
        y_refs, mix_gate_ref, wo_ref = rest[:n_mix], rest[n_mix], rest[n_mix + 1]
        z = None
        for k, y_ref in enumerate(y_refs):
            part = jnp.dot(y_ref[...].astype(jnp.bfloat16), wo_ref[k * GROUP_WIDTH:(k + 1) * GROUP_WIDTH, :],
                           preferred_element_type=jnp.float32)
            z = part if z is None else z + part
        x = x + mix_gate_ref[0] * z
    h = _modulated_norm(x, g_ref[...], scale_ref[0], shift_ref[0]).astype(jnp.bfloat16)
    n_chunks = act_ref.shape[1] // FFN_CHUNK
    for ci in range(n_chunks):
        cs = slice(ci * FFN_CHUNK, (ci + 1) * FFN_CHUNK)
        a = jnp.dot(h, w1_ref[:, cs], preferred_element_type=jnp.float32)
        b = jnp.dot(h, w3_ref[:, cs], preferred_element_type=jnp.float32)
        act_ref[:, cs] = (a * jax.nn.sigmoid(a) * b).astype(jnp.bfloat16)
    y = jnp.dot(act_ref[...], w2_ref[...], preferred_element_type=jnp.float32)
    o_ref[...] = x + (0.5 * gate_ref[0]) * y


def ffn_half_step(x2, g, shift, scale, gate, w13, w2, tokens_per_batch, mix=None):
    M, D = x2.shape
    F = w2.shape[0]
    tm = FFN_TOKEN_TILE
    tiles_per_batch = tokens_per_batch // tm
    resident = dict(pipeline_mode=pl.Buffered(1))
    mod_spec = pl.BlockSpec((1, 1, D), lambda i: (i // tiles_per_batch, 0, 0))
    ys, mix_gate, w_out = mix if mix is not None else ((), None, None)
    mix_args = tuple(ys) + ((mix_gate, w_out) if ys else ())
    mix_specs = [pl.BlockSpec((tm, GROUP_WIDTH), lambda i: (i, 0)) for _ in ys]
    if ys:
        mix_specs += [mod_spec, pl.BlockSpec(w_out.shape, lambda i: (0, 0), **resident)]
    return pl.pallas_call(
        functools.partial(_ffn_kernel, n_mix=len(ys)),
        grid=(M // tm,),
        in_specs=[pl.BlockSpec((tm, D), lambda i: (i, 0)),
                  pl.BlockSpec((1, D), lambda i: (0, 0)),
                  mod_spec, mod_spec, mod_spec,
                  pl.BlockSpec((D, F), lambda i: (0, 0), **resident),
                  pl.BlockSpec((D, F), lambda i: (0, 1), **resident),
                  pl.BlockSpec((F, D), lambda i: (0, 0), **resident)] + mix_specs,
        out_specs=pl.BlockSpec((tm, D), lambda i: (i, 0)),
        out_shape=jax.ShapeDtypeStruct((M, D), jnp.float32),
        scratch_shapes=[pltpu.VMEM((tm, F), jnp.bfloat16)],
        compiler_params=pltpu.CompilerParams(dimension_semantics=("arbitrary",),
                                             vmem_limit_bytes=VMEM_LIMIT_BYTES),
        name="ffn_half_step",
    )(x2, g, shift, scale, gate, w13, w13, w2, *mix_args)


CONV_CARRY_ROWS = 8
CONV_ROW_CHUNK = 128


def _in_proj_kernel(x_ref, g_ref, shift_ref, scale_ref, wa_ref, wb_ref, wc_ref, wd_ref,
                    ba_ref, bb_ref, bc_ref, bd_ref, scw_ref, scg_ref, gcw_ref,
                    oa_ref, yb_ref, oc_ref, od_ref, braw_ref, craw_ref, *, tiles_per_batch):
    f32 = jnp.float32
    rows = x_ref.shape[0]
    hist = CONV_CARRY_ROWS
    first = pl.program_id(0) % tiles_per_batch == 0

    @pl.when(first)
    def _():
        braw_ref[0:hist, :] = jnp.zeros((hist, braw_ref.shape[1]), f32)
        craw_ref[0:hist, :] = jnp.zeros((hist, craw_ref.shape[1]), f32)

    @pl.when(jnp.logical_not(first))
    def _():
        braw_ref[0:hist, :] = braw_ref[rows:rows + hist, :]
        craw_ref[0:hist, :] = craw_ref[rows:rows + hist, :]

    h = _modulated_norm(x_ref[...], g_ref[...], scale_ref[0], shift_ref[0]).astype(jnp.bfloat16)
    project = lambda w_ref, b_ref: jnp.dot(h, w_ref[...], preferred_element_type=f32) + b_ref[...]
    pb = project(wb_ref, bb_ref)
    yb_ref[...] = pb[:, 0:256]
    braw_ref[hist:hist + rows, :] = pb[:, 256:512] * pb[:, 512:768]
    pc = project(wc_ref, bc_ref)
    craw_ref[hist:hist + rows, :] = pc[:, 0:768]
    oc_ref[:, 768:] = pc[:, 768:]
    n_chunks = rows // CONV_ROW_CHUNK
    pieces = []
    for w_ref, b_ref, o_ref in ((wa_ref, ba_ref, oa_ref), (wd_ref, bd_ref, od_ref)):
        half = w_ref.shape[1] // 256 // 2 * 256
        pieces += [(w_ref, b_ref, o_ref, 0, half), (w_ref, b_ref, o_ref, half, w_ref.shape[1])]
    for ci, r0 in enumerate(range(0, rows, CONV_ROW_CHUNK)):
        for w_ref, b_ref, o_ref, c0, c1 in pieces[ci * len(pieces) // n_chunks:(ci + 1) * len(pieces) // n_chunks]:
            o_ref[:, c0:c1] = jnp.dot(h, w_ref[:, c0:c1], preferred_element_type=f32) + b_ref[:, c0:c1]
        out_rows = slice(r0, r0 + CONV_ROW_CHUNK)
        tap = lambda ref, s: ref[pl.ds(hist + r0 - s, CONV_ROW_CHUNK), :]
        conv = sum(tap(braw_ref, s) * scw_ref[2 - s:3 - s, :] for s in range(3))
        y = yb_ref[out_rows, :] * conv
        yb_ref[out_rows, :] = y * lax.rsqrt(jnp.mean(y * y, axis=-1, keepdims=True) + EPS) * scg_ref[...]
        for c0 in range(0, 3 * GROUP_WIDTH, GROUP_WIDTH):
            cols = slice(c0, c0 + GROUP_WIDTH)
            conv = sum(craw_ref[pl.ds(hist + r0 - s, CONV_ROW_CHUNK), cols] * gcw_ref[3 - s:4 - s, cols]
                       for s in range(4))
            oc_ref[out_rows, cols] = conv * jax.nn.sigmoid(conv)


def in_projection(x2, g, shift, scale, ws, bs, sc_conv_w, sc_norm_g, gdn_conv_w, tokens_per_batch):
    M, D = x2.shape
    tm = PROJ_TOKEN_TILE
    tiles_per_batch = tokens_per_batch // tm
    mod_spec = pl.BlockSpec((1, 1, D), lambda i: (i // tiles_per_batch, 0, 0))
    const = lambda a, **kw: pl.BlockSpec(a.shape, lambda i: (0, 0), **kw)
    pad8 = lambda w: jnp.pad(w, ((0, 8 - w.shape[0]), (0, 0)))
    extras = (pad8(sc_conv_w), sc_norm_g, pad8(gdn_conv_w))
    widths = (ws[0].shape[1], GROUP_WIDTH, ws[2].shape[1], ws[3].shape[1])
    return pl.pallas_call(
        functools.partial(_in_proj_kernel, tiles_per_batch=tiles_per_batch),
        grid=(M // tm,),
        in_specs=[pl.BlockSpec((tm, D), lambda i: (i, 0)), const(g), mod_spec, mod_spec]
                 + [const(w, pipeline_mode=pl.Buffered(1)) for w in ws] + [const(b) for b in bs]
                 + [const(e) for e in extras],
        out_specs=[pl.BlockSpec((tm, n), lambda i: (i, 0)) for n in widths],
        out_shape=[jax.ShapeDtypeStruct((M, n), jnp.float32) for n in widths],
        scratch_shapes=[pltpu.VMEM((CONV_CARRY_ROWS + tm, GROUP_WIDTH), jnp.float32),
                        pltpu.VMEM((CONV_CARRY_ROWS + tm, 3 * GROUP_WIDTH), jnp.float32)],
        compiler_params=pltpu.CompilerParams(dimension_semantics=("arbitrary",),
                                             vmem_limit_bytes=VMEM_LIMIT_BYTES),
        name="in_projection",
    )(x2, g, shift, scale, *ws, *bs, *extras)


def _group_columns():
    offs, o = {}, 0
    for n, s in IN_LAYOUT:
        offs[n] = np.arange(o, o + s)
        o += s
    cat = lambda names: np.concatenate([offs[n] for n in names])
    return (cat(["a_q", "a_k_cmp", "a_v_cmp", "a_k_slc", "a_v_slc", "a_k_win", "a_v_win", "a_gate"]),
            cat(["b_b", "b_c", "b_x"]),
            cat(["c_q", "c_k", "c_v", "c_z", "c_beta", "c_alpha"]),
            cat(["d_q", "d_k", "d_v", "d_o", "d_i", "d_f"]))


GROUP_SLAB_WIDTH = (768, 768, 1152, 1152)


def _take_columns(a, idx, width):
    cuts = [0] + [k + 1 for k in range(idx.size - 1) if idx[k + 1] != idx[k] + 1] + [idx.size]
    runs = [a[..., int(idx[s]):int(idx[e - 1]) + 1] for s, e in zip(cuts[:-1], cuts[1:])]
    return jnp.concatenate(runs + [jnp.zeros(a.shape[:-1] + (width - idx.size,), a.dtype)], axis=-1)


def split_in_weights(w, b):
    ws, bs = [], []
    for idx, width in zip(_group_columns(), GROUP_SLAB_WIDTH):
        ws.append(_take_columns(w, idx, width).astype(jnp.bfloat16))
        bs.append(_take_columns(b, idx, width)[None, :])
    return ws, bs


def t5_bucket(dist):
    n = jnp.maximum(dist, 0)
    max_exact = N_BUCKETS // 2
    nf = jnp.maximum(n, 1).astype(jnp.float32)
    large = max_exact + (jnp.log(nf / max_exact) / math.log(MAX_DISTANCE / max_exact)
                         * (N_BUCKETS - max_exact)).astype(jnp.int32)
    large = jnp.minimum(large, N_BUCKETS - 1)
    return jnp.where(n < max_exact, n, large)


def _nt_dot(a, b):
    return lax.dot_general(a, b, (((1,), (1,)), ((), ())), preferred_element_type=jnp.float32)


def _stack_heads(qn):
    lane = lax.broadcasted_iota(jnp.int32, (qn.shape[0], 128), 1)
    lo = lane < HEAD_DIM
    zero = jnp.zeros((), qn.dtype)
    halves = (qn[:, 0:128], qn[:, 128:256])
    return jnp.concatenate([jnp.where(lo if h % 2 == 0 else ~lo, halves[h // 2], zero)
                            for h in range(GROUP_HEADS)], axis=0)


def _nsa_prep_kernel(q_ref, kvs_ref, kvw_ref, qg_ref, kg_ref, seg_ref,
                     qnt_ref, kxs_ref, v1s_ref, kkw_ref, v1w_ref):
    tm = q_ref.shape[0]
    q = q_ref[...]
    ss = jnp.dot(q * q, seg_ref[...], precision=HIGHEST, preferred_element_type=jnp.float32)
    qn = q * lax.rsqrt(ss * (1.0 / HEAD_DIM) + EPS) * qg_ref[...]
    qnt_ref[...] = (qn * LOG2E).T.astype(qnt_ref.dtype)
    lane = lax.broadcasted_iota(jnp.int32, (tm, 128), 1)
    lo = lane < HEAD_DIM

    def split(x):
        xr = pltpu.roll(x, HEAD_DIM, 1)
        ss = jnp.sum(jnp.where(lo, x * x, 0.0), axis=-1, keepdims=True)
        kk = jnp.where(lo, x, xr) * lax.rsqrt(ss * (1.0 / HEAD_DIM) + EPS) * kg_ref[...]
        return kk, jnp.where(lo, xr, 1.0)

    kk, v1 = split(kvs_ref[...])
    tok = pl.program_id(1) * tm + lax.broadcasted_iota(jnp.int32, (tm, 128), 0)
    onehot = jnp.where(lane == tok // SLC_BLOCK, 1.0, 0.0)
    kxs_ref[...] = jnp.concatenate([kk, onehot], axis=1).astype(kxs_ref.dtype)
    v1s_ref[...] = v1.T[0:V1_ROWS].astype(v1s_ref.dtype)
    kk, v1 = split(kvw_ref[...])
    kkw_ref[...] = kk.astype(kkw_ref.dtype)
    v1w_ref[...] = v1.T[0:V1_ROWS].astype(v1w_ref.dtype)


def nsa_prep(pa, qg, kg2, seg):
    B, T, _ = pa.shape
    tm = 512
    bf16 = jnp.bfloat16
    col = lambda w, j: pl.BlockSpec((None, tm, w), lambda b, i: (b, i, j))
    rowblk = lambda h: pl.BlockSpec((None, h, tm), lambda b, i: (b, 0, i))
    const = lambda shape: pl.BlockSpec(shape, lambda b, i: (0, 0))
    return pl.pallas_call(
        _nsa_prep_kernel,
        grid=(B, T // tm),
        in_specs=[col(256, 0), col(128, 3), col(128, 4), const((1, 256)), const((1, 128)), const((256, 256))],
        out_specs=[rowblk(256), col(256, 0), rowblk(V1_ROWS), col(128, 0), rowblk(V1_ROWS)],
        out_shape=[jax.ShapeDtypeStruct((B, 256, T), bf16),
                   jax.ShapeDtypeStruct((B, T, 256), bf16), jax.ShapeDtypeStruct((B, V1_ROWS, T), bf16),
                   jax.ShapeDtypeStruct((B, T, 128), bf16), jax.ShapeDtypeStruct((B, V1_ROWS, T), bf16)],
        name="nsa_prep",
    )(pa, pa, pa, qg, kg2, seg)


def _compress_kernel(kv_ref, pos_ref, w1_ref, w2k_ref, w2v_ref, kg_ref, kkc_ref, vct_ref):
    nrow = kkc_ref.shape[0]
    f32, bf16 = jnp.float32, jnp.bfloat16
    hidden = w1_ref.shape[2] // 2
    first = jnp.zeros((nrow, 2 * hidden), f32)
    second = jnp.zeros((nrow, 2 * hidden), f32)
    for p in range(CMP_STRIDE):
        xp = kv_ref[pl.ds(p, nrow, stride=CMP_STRIDE), :]
        first = first + jnp.dot((xp + pos_ref[p:p + 1, :]).astype(bf16), w1_ref[p], preferred_element_type=f32)
        second = second + jnp.dot((xp + pos_ref[CMP_STRIDE + p:CMP_STRIDE + p + 1, :]).astype(bf16),
                                  w1_ref[CMP_STRIDE + p], preferred_element_type=f32)
    hid = first + pltpu.roll(second, nrow - 1, 0)
    hid = (hid * jax.nn.sigmoid(hid)).astype(bf16)
    valid = lax.broadcasted_iota(jnp.int32, (nrow, 128), 0) < nrow - 1
    kc = jnp.dot(hid[:, 0:hidden], w2k_ref[...], preferred_element_type=f32)
    kc = kc * lax.rsqrt(jnp.mean(kc * kc, axis=-1, keepdims=True) + EPS) * kg_ref[...]
    kkc_ref[...] = jnp.where(valid, kc, 0.0).astype(kkc_ref.dtype)
    vc = jnp.dot(hid[:, hidden:2 * hidden], w2v_ref[...], preferred_element_type=f32)
    vct_ref[...] = jnp.where(valid, vc, 0.0).T[0:HEAD_DIM].astype(vct_ref.dtype)


def nsa_compress(pa, pos2, w1, w2k, w2v, kg2):
    B, T, _ = pa.shape
    nrow = T // CMP_STRIDE
    bf16 = jnp.bfloat16
    full = lambda a: pl.BlockSpec(a.shape, lambda b: (0,) * a.ndim)
    return pl.pallas_call(
        _compress_kernel,
        grid=(B,),
        in_specs=[pl.BlockSpec((None, T, 128), lambda b: (b, 0, 2)), full(pos2), full(w1), full(w2k), full(w2v),
                  full(kg2)],
        out_specs=[pl.BlockSpec((None, nrow, 128), lambda b: (b, 0, 0)),
                   pl.BlockSpec((None, HEAD_DIM, nrow), lambda b: (b, 0, 0))],
        out_shape=[jax.ShapeDtypeStruct((B, nrow, 128), bf16), jax.ShapeDtypeStruct((B, HEAD_DIM, nrow), bf16)],
        name="nsa_compress",
    )(pa, pos2, w1, w2k, w2v, kg2)


CMP_TILES_PER_STEP = 4


def _stack_heads_t(qt):
    lo = lax.broadcasted_iota(jnp.int32, (128, qt.shape[1]), 0) < HEAD_DIM
    zero = jnp.zeros((), qt.dtype)
    return jnp.concatenate([jnp.where(lo if h % 2 == 0 else ~lo, qt[128 * (h // 2):128 * (h // 2) + 128], zero)
                            for h in range(GROUP_HEADS)], axis=1)


def _cmp_select_kernel(qnt_ref, kkc_ref, vct_ref, cb_ref, ov_ref, ocmp_ref, pen_ref, sc_ref, *, n_sel):
    i = pl.program_id(1)
    n_tiles = pen_ref.shape[0]
    tq = qnt_ref.shape[1] // n_tiles
    nblk, nrow = ov_ref.shape
    keys_per_tile = tq // CMP_STRIDE
    f32, bf16 = jnp.float32, jnp.bfloat16
    blk = lax.broadcasted_iota(jnp.int32, (nblk, tq), 0)
    scores = []
    for g in range(n_tiles):
        cols = slice(g * tq, (g + 1) * tq)
        q4t = _stack_heads_t(qnt_ref[:, cols])
        first_row = pl.multiple_of(nrow - keys_per_tile * (i * n_tiles + g), keys_per_tile)
        s = jnp.dot(kkc_ref[...], q4t, preferred_element_type=f32) + cb_ref[pl.ds(first_row, nrow), :]
        m = jnp.max(s, axis=0, keepdims=True)
        e = jnp.where(s > 0.5 * NEG, jnp.exp2(s - m), 0.0)
        p = e * (1.0 / jnp.maximum(jnp.sum(e, axis=0, keepdims=True), 1e-30))
        o = jnp.dot(vct_ref[...], p.astype(bf16), preferred_element_type=f32)
        for h in range(GROUP_HEADS):
            ocmp_ref[h * HEAD_DIM:(h + 1) * HEAD_DIM, cols] = o[:, h * tq:(h + 1) * tq]
        psum = p[:, 0:tq] + p[:, tq:2 * tq] + p[:, 2 * tq:3 * tq] + p[:, 3 * tq:4 * tq]
        hi = psum.astype(bf16)
        lo_part = (psum - hi.astype(f32)).astype(bf16)
        score = (jnp.dot(ov_ref[...], hi, preferred_element_type=f32)
                 + jnp.dot(ov_ref[...], lo_part, preferred_element_type=f32))
        qpos = (i * n_tiles + g) * tq + lax.broadcasted_iota(jnp.int32, (nblk, tq), 1)
        cur = qpos // SLC_BLOCK
        forced = (blk == 0) | ((cur - blk >= 0) & (cur - blk < N_LOCAL_SLC))
        score = jnp.where(forced, FORCE, score)
        score = jnp.where(blk <= cur, score, -FORCE)
        sc_ref[g] = score
        scores.append(score)
    blocks_per_tile = tq // SLC_BLOCK

    def rank_step(t, cnts):
        out = []
        for g in range(n_tiles):
            cnt = cnts[g]
            for u in range(blocks_per_tile):
                jp = t * blocks_per_tile + u
                row = sc_ref[g, pl.ds(jp, 1), :]
                tie = jnp.where(blk > jp, 1.0, 0.0)
                cnt = cnt + jnp.where(row > scores[g], 1.0, jnp.where(row == scores[g], tie, 0.0))
            out.append(cnt)
        return tuple(out)

    zeros = tuple(jnp.zeros((nblk, tq), f32) for _ in range(n_tiles))
    cnts = lax.fori_loop(0, (i + 1) * n_tiles, rank_step, zeros)
    for g in range(n_tiles):
        pen = jnp.where(cnts[g] < n_sel, 0.0, NEG)
        if nblk < 128:
            pen = jnp.concatenate([pen, jnp.zeros((128 - nblk, tq), f32)], axis=0)
        pen_ref[g] = pen.astype(pen_ref.dtype)


def nsa_cmp_select(qnt, kkc, vct, cbias, ov, n_sel):
    B, _, T = qnt.shape
    nrow = kkc.shape[1]
    nblk = ov.shape[0]
    tq = Q_BLOCK
    g = min(CMP_TILES_PER_STEP, T // tq)
    return pl.pallas_call(
        functools.partial(_cmp_select_kernel, n_sel=n_sel),
        grid=(B, T // (g * tq)),
        in_specs=[pl.BlockSpec((None, 256, g * tq), lambda b, i: (b, 0, i)),
                  pl.BlockSpec((None, nrow, 128), lambda b, i: (b, 0, 0)),
                  pl.BlockSpec((None, HEAD_DIM, nrow), lambda b, i: (b, 0, 0)),
                  pl.BlockSpec((2 * nrow, 4 * tq), lambda b, i: (0, 0)),
                  pl.BlockSpec((nblk, nrow), lambda b, i: (0, 0))],
        out_specs=[pl.BlockSpec((None, 256, g * tq), lambda b, i: (b, 0, i)),
                   pl.BlockSpec((None, g, 128, tq), lambda b, i: (b, i, 0, 0))],
        out_shape=[jax.ShapeDtypeStruct((B, 256, T), jnp.float32),
                   jax.ShapeDtypeStruct((B, T // tq, 128, tq), jnp.bfloat16)],
        scratch_shapes=[pltpu.VMEM((g, nblk, tq), jnp.float32)],
        name="nsa_cmp_select",
    )(qnt, kkc, vct, cbias, ov)


NSA_KEY_TILE = 256
NSA_QUERY_TILE = 256


def _nsa_attn_kernel(qnt_ref, pen_ref, ocmp_ref, gate_ref, kxs_ref, v1s_ref, kkw_ref, v1w_ref,
                     tabs_ref, tabw_ref, mixg_ref, o_ref, acc_ref, m_ref, sa_ref, sb_ref):
    i = pl.program_id(1)
    tq = o_ref.shape[0]
    kt = tabs_ref.shape[1]
    f32, bf16 = jnp.float32, jnp.bfloat16
    q4t = _stack_heads_t(qnt_ref[...])
    pen = jnp.concatenate([pen_ref[t] for t in range(pen_ref.shape[0])], axis=1)
    rhs_sel = jnp.concatenate([q4t, jnp.concatenate([pen] * GROUP_HEADS, axis=1)], axis=0)

    def branch(rhs, k_ref, v_ref, tab_ref, first_tile, last_tile):
        n_tab = tab_ref.shape[0] - 1
        m_ref[...] = jnp.full(m_ref.shape, NEG, f32)
        acc_ref[...] = jnp.zeros(acc_ref.shape, f32)

        def key_offset(j):
            return pl.multiple_of(jnp.minimum(j, last_tile) * kt, kt)

        def scores(j):
            delta = jnp.where(j > last_tile, n_tab, jnp.minimum(i - j * (kt // tq), n_tab - 1))
            return jnp.dot(k_ref[pl.ds(key_offset(j), kt), :], rhs, preferred_element_type=f32) + tab_ref[delta]

        def absorb(s, j):
            m_old = m_ref[...]
            m_new = jnp.maximum(m_old, jnp.max(s, axis=0, keepdims=True))
            p = jnp.exp2(s - m_new)
            pv =jnp.dot(v_ref[:, pl.ds(key_offset(j), kt)], p.astype(bf16), preferred_element_type=f32)
            acc_ref[...] = jnp.exp2(m_old - m_new) * acc_ref[...] + pv
            m_ref[...] = m_new

        sa_ref[...] = scores(first_tile)

        def body(t, carry):
            j = first_tile + 2 * t
            sb_ref[...] = scores(j + 1)
            absorb(sa_ref[...], j)
            sa_ref[...] = scores(j + 2)
            absorb(sb_ref[...], j + 1)
            return carry

        lax.fori_loop(0, (last_tile - first_tile) // 2 + 1, body, 0)
        acc = acc_ref[...]
        return acc[0:HEAD_DIM] / jnp.maximum(acc[HEAD_DIM:HEAD_DIM + 1], 1e-30)

    def window_branch(rhs):
        n_tab = tabw_ref.shape[0] - 1
        m = jnp.full((1, rhs.shape[1]), NEG, f32)
        acc = jnp.zeros((V1_ROWS, rhs.shape[1]), f32)
        for d in range(n_tab):
            j = i - d
            off = pl.multiple_of(jnp.maximum(j, 0) * kt, kt)
            s = jnp.dot(kkw_ref[pl.ds(off, kt), :], rhs, preferred_element_type=f32)
            s = s + tabw_ref[jnp.where(j < 0, n_tab, d)]
            m_new = jnp.maximum(m, jnp.max(s, axis=0, keepdims=True))
            p = jnp.exp2(s - m_new)
            acc = jnp.exp2(m - m_new) * acc +jnp.dot(v1w_ref[:, pl.ds(off, kt)], p.astype(bf16),
                                                     preferred_element_type=f32)
            m = m_new
        return acc[0:HEAD_DIM] / jnp.maximum(acc[HEAD_DIM:HEAD_DIM + 1], 1e-30)

    last = (i * tq) // kt
    if kt == tq:
        o_w = window_branch(q4t)
    else:
        o_w = branch(q4t, kkw_ref, v1w_ref, tabw_ref, (jnp.maximum(i - WINDOW // tq, 0) * tq) // kt, last)
    o_s = branch(rhs_sel, kxs_ref, v1s_ref, tabs_ref, 0, last)
    gt = jax.nn.sigmoid(gate_ref[...]).T
    oct_ = ocmp_ref[...]
    parts = []
    for h in range(GROUP_HEADS):
        cols = slice(h * tq, (h + 1) * tq)
        parts.append(gt[3 * h:3 * h + 1] * oct_[h * HEAD_DIM:(h + 1) * HEAD_DIM]
                     + gt[3 * h + 1:3 * h + 2] * o_s[:, cols] + gt[3 * h + 2:3 * h + 3] * o_w[:, cols])
    yt = jnp.concatenate(parts, axis=0)
    yt = yt * lax.rsqrt(jnp.mean(yt * yt, axis=0, keepdims=True) + EPS) * mixg_ref[...]
    o_ref[...] = yt.T


def nsa_attention(qnt, pen, ocmp, pa, kxs, v1s, kkw, v1w, tab_s, tab_w, mixg):
    B, _, T = qnt.shape
    tq = min(NSA_QUERY_TILE, T)
    tile = lambda w, j: pl.BlockSpec((None, tq, w), lambda b, i: (b, i, j))
    whole = lambda w: pl.BlockSpec((None, T, w), lambda b, i: (b, 0, 0))
    whole_t = pl.BlockSpec((None, V1_ROWS, T), lambda b, i: (b, 0, 0))
    full = lambda a: pl.BlockSpec(a.shape, lambda b, i: (0,) * a.ndim)
    return pl.pallas_call(
        _nsa_attn_kernel,
        grid=(B, T // tq),
        in_specs=[pl.BlockSpec((None, 256, tq), lambda b, i: (b, 0, i)),
                  pl.BlockSpec((None, tq // Q_BLOCK, 128, Q_BLOCK), lambda b, i: (b, i, 0, 0)),
                  pl.BlockSpec((None, 256, tq), lambda b, i: (b, 0, i)), tile(128, 5),
                  whole(256), whole_t, whole(128), whole_t, full(tab_s), full(tab_w), full(mixg)],
        out_specs=tile(256, 0),
        out_shape=jax.ShapeDtypeStruct((B, T, 256), jnp.float32),
        scratch_shapes=[pltpu.VMEM((V1_ROWS, GROUP_HEADS * tq), jnp.float32),
                        pltpu.VMEM((1, GROUP_HEADS * tq), jnp.float32),
                        pltpu.VMEM((tab_s.shape[1], GROUP_HEADS * tq), jnp.float32),
                        pltpu.VMEM((tab_s.shape[1], GROUP_HEADS * tq), jnp.float32)],
        compiler_params=pltpu.CompilerParams(dimension_semantics=("arbitrary", "arbitrary"),
                                             vmem_limit_bytes=VMEM_LIMIT_BYTES),
        name="nsa_attention",
    )(qnt, pen, ocmp, pa, kxs, v1s, kkw, v1w, tab_s, tab_w, mixg)


def _toeplitz(wr, rows, cols):
    H, n = wr.shape
    flat = jnp.tile(jnp.pad(wr, ((0, 0), (0, 1))), (1, rows))[:, :rows * n]
    return flat.reshape(H, rows, n)[:, :, rows - 1:rows - 1 + cols]


def nsa_bias_tables(t5_table, T):
    H = GROUP_HEADS
    tq = Q_BLOCK
    ta = min(NSA_QUERY_TILE, T)
    kt = min(NSA_KEY_TILE, T)
    nrow = T // CMP_STRIDE
    dmax = WINDOW + 2 * ta + kt
    onehot = (t5_bucket(jnp.arange(dmax))[:, None] == jnp.arange(N_BUCKETS)[None, :]).astype(jnp.float32)
    by_dist = jnp.dot(onehot, t5_table, precision=HIGHEST).T
    neg = lambda n: jnp.full((H, n), NEG, jnp.float32)
    rev = by_dist[:, ::-1] * LOG2E
    f_s = jnp.concatenate([rev, neg(kt - 1)], axis=1)
    f_w = jnp.concatenate([neg(dmax - WINDOW), rev[:, dmax - WINDOW:], neg(kt - 1)], axis=1)
    span = ta + kt - 1
    total = dmax + kt - 1

    def tiles(f, n):
        out = [_toeplitz(f[:, total - k * ta - span:total - k * ta], ta, kt).reshape(H * ta, kt).T for k in range(n)]
        return jnp.stack(out + [jnp.full((kt, H * ta), NEG, jnp.float32)])

    n_s = -(-(kt + MAX_DISTANCE) // ta) + 1
    n_w = WINDOW // ta + kt // ta
    tab_s, tab_w = tiles(f_s, n_s), tiles(f_w, n_w)
    per_tile = tq // CMP_STRIDE
    d0 = -CMP_STRIDE * (nrow - 1) - (CMP_BLOCK - 1)
    n16 = 2 * nrow + per_tile - 1
    n_pos = CMP_STRIDE * n16 + d0
    far = jnp.broadcast_to(by_dist[:, -1:], (H, max(n_pos - dmax, 0)))
    w16 = jnp.concatenate([neg(-d0), by_dist[:, :n_pos] * LOG2E, far * LOG2E], axis=1).reshape(H, n16, CMP_STRIDE)
    w16 = w16[:, ::-1, :]
    base = jnp.stack([w16[:, per_tile - 1 - a:per_tile - 1 - a + 2 * nrow, :] for a in range(per_tile)], axis=1)
    cbias = base.transpose(0, 1, 3, 2).reshape(H * tq, 2 * nrow).T
    return tab_s, tab_w, cbias


def nsa_overlap_t(T):
    nrow = T // CMP_STRIDE
    nblk = T // SLC_BLOCK
    ci = np.arange(nrow)[None, :]
    bj = np.arange(nblk)[:, None]
    ov = (ci * CMP_STRIDE < (bj + 1) * SLC_BLOCK) & (ci * CMP_STRIDE + CMP_BLOCK > bj * SLC_BLOCK)
    ov = ov & (ci < nrow - 1)
    return jnp.asarray(ov.astype(np.float32), dtype=jnp.bfloat16)


def nsa_mixer_pallas(pa, q_norm_g, k_norm_g, cmp_pos, cmp_k_w1, cmp_k_w2, cmp_v_w1, cmp_v_w2,
                     tables, ovt, mixg):
    B, T, _ = pa.shape
    bf16 = jnp.bfloat16
    tab_s, tab_w, cbias = tables
    qg = (jnp.tile(q_norm_g, GROUP_HEADS) * HEAD_DIM ** -0.5)[None, :]
    kg2 = jnp.tile(k_norm_g, 2)[None, :]
    qnt, kxs, v1s, kkw, v1w = nsa_prep(pa, qg, kg2, _head_mask())
    dup = lambda w: jnp.concatenate([w, w], axis=1)
    zeros = jnp.zeros((CMP_BLOCK, HEAD_DIM, cmp_k_w1.shape[1]), jnp.float32)
    per_pos = lambda w: w.reshape(CMP_BLOCK, HEAD_DIM, -1)
    w1 = jnp.concatenate([jnp.concatenate([per_pos(cmp_k_w1), zeros], axis=2),
                          jnp.concatenate([zeros, per_pos(cmp_v_w1)], axis=2)], axis=1).astype(bf16)
    kkc, vct = nsa_compress(pa, dup(cmp_pos), w1, dup(cmp_k_w2).astype(bf16), dup(cmp_v_w2).astype(bf16), kg2)
    n_sel = min(N_SLC, T // SLC_BLOCK)
    ocmp, pen = nsa_cmp_select(qnt, kkc, vct, cbias, ovt, n_sel)
    mixg_col = jnp.broadcast_to(mixg.reshape(-1, 1), (GROUP_WIDTH, min(NSA_QUERY_TILE, T)))
    return nsa_attention(qnt, pen, ocmp, pa, kxs, v1s, kkw, v1w, tab_s, tab_w, mixg_col)


CHUNKS_PER_STEP = 4
SEQS_PER_STEP = 2


def _tn_dot(a, b):
    return lax.dot_general(a, b, (((0,), (0,)), ((), ())), preferred_element_type=jnp.float32)


def _split_bf16(a, parts):
    out, rest = [], a
    for _ in range(parts):
        piece = rest.astype(jnp.bfloat16)
        out.append(piece)
        rest = rest - piece.astype(jnp.float32)
    return out


def _dot_sel(a, sel, parts=3):
    return sum(jnp.dot(p, sel, preferred_element_type=jnp.float32) for p in _split_bf16(a, parts))


def _sel_dot(sel, a, parts=3):
    return sum(jnp.dot(sel, p, preferred_element_type=jnp.float32) for p in _split_bf16(a, parts))


def _softplus(x):
    return jnp.maximum(x, 0.0) + jnp.log(1.0 + jnp.exp(-jnp.abs(x)))


def _block_diag(a, hmb):
    return jnp.concatenate([a.astype(jnp.bfloat16)] * GROUP_HEADS, axis=0) * hmb


def _diag_row(a, eye):
    return jnp.sum(a * eye, axis=0, keepdims=True)


def _head_norm(o, hmb):
    return o * lax.rsqrt(_dot_sel(o * o, hmb, parts=2) * (1.0 / HEAD_DIM) + EPS)


def _head_l2(o, hmb):
    return o * lax.rsqrt(_dot_sel(o * o, hmb, parts=2) + EPS)


def _xform_masks(L):
    i = np.arange(L)[:, None]
    j = np.arange(L)[None, :]
    masks = []
    s = 1
    while s < L:
        masks.append((i // (2 * s) == j // (2 * s)) & (i % (2 * s) >= s) & (j % (2 * s) < s))
        s *= 2
    masks += [i == j, j <= i, j < i]
    return jnp.asarray(np.tile(np.stack(masks).astype(np.float32), (1, 1, GROUP_HEADS)))


def _expand_matrix(first_lane):
    e = np.zeros((128, GROUP_WIDTH), np.float32)
    for h in range(GROUP_HEADS):
        e[first_lane + h, h * HEAD_DIM:(h + 1) * HEAD_DIM] = 1.0
    return jnp.asarray(e, jnp.bfloat16)


def _head_mask():
    return jnp.asarray(np.kron(np.eye(GROUP_HEADS), np.ones((HEAD_DIM, HEAD_DIM))), jnp.float32)


def _per_head_rows(*vecs):
    rows = [jnp.repeat(v, HEAD_DIM) if v.shape[0] == GROUP_HEADS else jnp.tile(v, GROUP_HEADS) for v in vecs]
    return jnp.pad(jnp.stack(rows), ((0, 8 - len(rows)), (0, 0)))


def _chunk_consts(L, n_chunks):
    tri = np.kron(np.eye(n_chunks), np.tril(np.ones((L, L)))).astype(np.float32)
    return (_head_mask(), jnp.asarray(tri, jnp.bfloat16), _expand_matrix(0), _expand_matrix(GROUP_HEADS),
            _xform_masks(L))


def _const_spec(a):
    return pl.BlockSpec(a.shape, lambda b, t: (0,) * a.ndim)


def _gdn_kernel(pc_ref, hp_ref, hm_ref, tri_ref, eb_ref, ea_ref, lv_ref, o_ref, s_ref):
    L = GDN_CHUNK
    n_seq, rows = pc_ref.shape[0], pc_ref.shape[1]
    f32, bf16 = jnp.float32, jnp.bfloat16
    n_lev = lv_ref.shape[0] - 3

    @pl.when(pl.program_id(1) == 0)
    def _():
        s_ref[...] = jnp.zeros(s_ref.shape, f32)

    hm = hm_ref[...]
    hmb = hm.astype(bf16)
    eye, incl, strict = lv_ref[n_lev], lv_ref[n_lev + 1], lv_ref[n_lev + 2]

    q_c, k_c, v_c, beta_c, gc_c = [], [], [], [], []
    for b in range(n_seq):
        qkv = pc_ref[b, :, 0:768]
        q_all = _head_l2(qkv[:, 0:256], hmb) * HEAD_DIM ** -0.5
        k_all = _head_l2(qkv[:, 256:512], hmb)
        tail = pc_ref[b, :, 1024:1152]
        beta_all = jax.nn.sigmoid(_dot_sel(tail, eb_ref[...]))
        g = hp_ref[0:1, :] * _softplus(_dot_sel(tail, ea_ref[...]) + hp_ref[1:2, :])
        gc_all = _sel_dot(tri_ref[...], g)
        for c in range(rows // L):
            sl = slice(c * L, (c + 1) * L)
            q_c.append(q_all[sl])
            k_c.append(k_all[sl])
            v_c.append(qkv[sl, 512:768])
            beta_c.append(beta_all[sl])
            gc_c.append(gc_all[sl])
    chains = range(len(q_c))
    egc = [jnp.exp(gc) for gc in gc_c]
    kb = [k_c[i] * beta_c[i] for i in chains]
    a_mat, qk = [], []
    for i in chains:
        seg = jnp.exp(jnp.where(incl > 0.5, gc_c[i] - _diag_row(gc_c[i], eye), NEG))
        k_bd = _block_diag(k_c[i], hmb)
        a_mat.append(_nt_dot(kb[i].astype(bf16), k_bd) * seg * strict)
        qk.append((_nt_dot(q_c[i].astype(bf16), k_bd) * seg).astype(bf16))
    t_inv = [eye - a * lv_ref[0] for a in a_mat]
    for lev in range(1, n_lev):
        te = [jnp.dot(t_inv[i].astype(bf16), _block_diag(a_mat[i] * lv_ref[lev], hmb),
                      preferred_element_type=f32) for i in chains]
        t_inv = [t_inv[i] - jnp.dot(te[i].astype(bf16), _block_diag(t_inv[i], hmb), preferred_element_type=f32)
                 for i in chains]
    parts = []
    for i in chains:
        tb = t_inv[i].astype(bf16)
        u = jnp.dot(tb, _block_diag(v_c[i] * beta_c[i], hmb), preferred_element_type=f32)
        w = jnp.dot(tb, _block_diag(kb[i] * egc[i], hmb), preferred_element_type=f32)
        k_dec = k_c[i] * jnp.exp(gc_c[i][L - 1:L, :] - gc_c[i])
        parts.append((u, w.astype(bf16), (q_c[i] * egc[i]).astype(bf16), qk[i], k_dec.astype(bf16),
                      egc[i][L - 1:L, :]))
    per_seq = rows // L
    state = [s_ref[b] for b in range(n_seq)]
    outs = [[] for _ in range(n_seq)]
    for c in range(per_seq):
        for b in range(n_seq):
            u, w, q_dec, qk_c, k_dec, g_tot = parts[b * per_seq + c]
            sb = state[b].astype(bf16)
            v_new = u - jnp.dot(w, sb, preferred_element_type=f32)
            outs[b].append(jnp.dot(q_dec, sb, preferred_element_type=f32)
                           + jnp.dot(qk_c, _block_diag(v_new, hmb), preferred_element_type=f32))
            state[b] = state[b] * g_tot + _tn_dot(k_dec, v_new.astype(bf16)) * hm
    for b in range(n_seq):
        s_ref[b] = state[b]
        o = jnp.concatenate(outs[b], axis=0)
        z = pc_ref[b, :, 768:1024]
        o_ref[b] = _head_norm(o, hmb) * hp_ref[2:3, :] * (z * jax.nn.sigmoid(z))


def gated_deltanet_pallas(pc, A_log, dt_bias, norm_g):
    B, T, width = pc.shape
    L = GDN_CHUNK
    rows = L * min(CHUNKS_PER_STEP, T // L)
    n_seq = SEQS_PER_STEP if B % SEQS_PER_STEP == 0 else 1
    n = GROUP_HEADS * HEAD_DIM
    hp = _per_head_rows(-jnp.exp(A_log), dt_bias, norm_g)
    consts = (hp,) + _chunk_consts(L, rows // L)
    return pl.pallas_call(
        _gdn_kernel,
        grid=(B // n_seq, T // rows),
        in_specs=[pl.BlockSpec((n_seq, rows, width), lambda b, t: (b, t, 0))] + [_const_spec(a) for a in consts],
        out_specs=pl.BlockSpec((n_seq, rows, 256), lambda b, t: (b, t, 0)),
        out_shape=jax.ShapeDtypeStruct((B, T, 256), jnp.float32),
        scratch_shapes=[pltpu.VMEM((n_seq, n, n), jnp.float32)],
        compiler_params=pltpu.CompilerParams(dimension_semantics=("arbitrary", "arbitrary"),
                                             vmem_limit_bytes=VMEM_LIMIT_BYTES),
        name="gated_deltanet",
    )(pc, *consts)


def _mlstm_kernel(pd_ref, hp_ref, hm_ref, tri_ref, ei_ref, ef_ref, lv_ref, o_ref, c_ref, n_ref, m_ref):
    L = MLSTM_CHUNK
    n_seq, rows = pd_ref.shape[0], pd_ref.shape[1]
    f32, bf16 = jnp.float32, jnp.bfloat16
    n_lev = lv_ref.shape[0] - 3

    @pl.when(pl.program_id(1) == 0)
    def _():
        c_ref[...] = jnp.zeros(c_ref.shape, f32)
        n_ref[...] = jnp.zeros(n_ref.shape, f32)
        m_ref[...] = jnp.zeros(m_ref.shape, f32)

    hm = hm_ref[...]
    hmb = hm.astype(bf16)
    eye, incl = lv_ref[n_lev], lv_ref[n_lev + 1]
    head_of_lane = lax.broadcasted_iota(jnp.int32, (L, GROUP_WIDTH), 1) // HEAD_DIM

    def head_max(a):
        out = jnp.zeros(a.shape, f32)
        for h in range(GROUP_HEADS):
            mine = head_of_lane == h
            out = jnp.where(mine, jnp.max(jnp.where(mine, a, NEG), axis=-1, keepdims=True), out)
        return out

    per_seq = rows // L
    parts = []
    for b in range(n_seq):
        tail = pd_ref[b, :, 1024:1152]
        log_i_all = _dot_sel(tail, ei_ref[...])
        log_f = -_softplus(-(_dot_sel(tail, ef_ref[...]) + hp_ref[0:1, :]))
        b_all = _sel_dot(tri_ref[...], log_f)
        for c in range(per_seq):
            sl = slice(c * L, (c + 1) * L)
            q = pd_ref[b, sl, 0:256]
            k = pd_ref[b, sl, 256:512] * HEAD_DIM ** -0.5
            v = pd_ref[b, sl, 512:768]
            bc, log_i = b_all[sl], log_i_all[sl]
            log_w = jnp.where(incl > 0.5, bc - _diag_row(bc - log_i, eye), NEG)
            b_last = bc[L - 1:L, :]
            log_w_end = b_last - bc + log_i
            qk = _nt_dot(q.astype(bf16), _block_diag(k, hmb))
            parts.append((q, k, v, bc, log_w, head_max(log_w), b_last, log_w_end,
                          jnp.max(log_w_end, axis=0, keepdims=True), qk))
    c_state = [c_ref[b] for b in range(n_seq)]
    n_state = [n_ref[b] for b in range(n_seq)]
    m_prev = [m_ref[b] for b in range(n_seq)]
    outs = [[] for _ in range(n_seq)]
    for c in range(per_seq):
        for b in range(n_seq):
            q, k, v, bc, log_w, m_intra, b_last, log_w_end, m_end, qk = parts[b * per_seq + c]
            log_inter = bc + m_prev[b]
            m_t = jnp.maximum(log_inter, m_intra)
            w_inter = jnp.exp(log_inter - m_t)
            s = qk * jnp.exp(log_w - m_t)
            num = w_inter * jnp.dot(q.astype(bf16), c_state[b].astype(bf16), preferred_element_type=f32)
            num = num + jnp.dot(s.astype(bf16), _block_diag(v, hmb), preferred_element_type=f32)
            den = w_inter * _dot_sel(q * n_state[b], hmb, parts=2) + _dot_sel(s, hmb, parts=2)
            outs[b].append(num / jnp.maximum(jnp.abs(den), jnp.exp(-m_t)))
            m_new = jnp.maximum(b_last + m_prev[b], m_end)
            w_old = jnp.exp(b_last + m_prev[b] - m_new)
            kw = k * jnp.exp(log_w_end - m_new)
            c_state[b] = w_old * c_state[b] + _tn_dot(kw.astype(bf16), v.astype(bf16)) * hm
            n_state[b] = w_old * n_state[b] + jnp.sum(kw, axis=0, keepdims=True)
            m_prev[b] = m_new
    for b in range(n_seq):
        c_ref[b] = c_state[b]
        n_ref[b] = n_state[b]
        m_ref[b] = m_prev[b]
        h = jnp.concatenate(outs[b], axis=0)
        o_ref[b] = _head_norm(h, hmb) * hp_ref[1:2, :] * jax.nn.sigmoid(pd_ref[b, :, 768:1024])


def mlstm_pallas(pd, f_bias, norm_g):
    B, T, width = pd.shape
    L = MLSTM_CHUNK
    rows = L * min(CHUNKS_PER_STEP, T // L)
    n_seq = SEQS_PER_STEP if B % SEQS_PER_STEP == 0 else 1
    n = GROUP_HEADS * HEAD_DIM
    consts = (_per_head_rows(f_bias, norm_g),) + _chunk_consts(L, rows // L)
    return pl.pallas_call(
        _mlstm_kernel,
        grid=(B // n_seq, T // rows),
        in_specs=[pl.BlockSpec((n_seq, rows, width), lambda b, t: (b, t, 0))] + [_const_spec(a) for a in consts],
        out_specs=pl.BlockSpec((n_seq, rows, 256), lambda b, t: (b, t, 0)),
        out_shape=jax.ShapeDtypeStruct((B, T, 256), jnp.float32),
        scratch_shapes=[pltpu.VMEM((n_seq, n, n), jnp.float32), pltpu.VMEM((n_seq, 1, 256), jnp.float32),
                        pltpu.VMEM((n_seq, 1, 256), jnp.float32)],
        compiler_params=pltpu.CompilerParams(dimension_semantics=("arbitrary", "arbitrary"),
                                             vmem_limit_bytes=VMEM_LIMIT_BYTES),
        name="mlstm",
    )(pd, *consts)


def kernel(x, c, ada_w, ada_b, norm_g, ffn1_w13, ffn1_w2, ffn2_w13, ffn2_w2, w_in, b_in, q_norm_g, k_norm_g, cmp_pos, cmp_k_w1, cmp_k_w2, cmp_v_w1, cmp_v_w2, t5_table, sc_conv_w, gdn_conv_w, gdn_A_log, gdn_dt_bias, gdn_norm_g, mlstm_f_bias, mlstm_norm_g, mix_norm_g, w_out):
    B, T, D = x.shape
    depth = ada_w.shape[0]
    bf16 = jnp.bfloat16
    x2 = x.reshape(B * T, D)
    tables = nsa_bias_tables(t5_table, T)
    ovt = nsa_overlap_t(T)
    for l in range(depth):
        mod = ada_modulation(c, ada_w[l], ada_b[l][None, :])
        mod = mod.reshape(B, N_SUBLAYERS, 3, 1, D)
        x2 = ffn_half_step(x2, norm_g[l, 0][None, :], mod[:, 0, 0], mod[:, 0, 1], mod[:, 0, 2],
                           ffn1_w13[l].astype(bf16), ffn1_w2[l].astype(bf16), T)
        ws, bs = split_in_weights(w_in[l], b_in[l])
        pa, y_b, pc, pd = in_projection(x2, norm_g[l, 1][None, :], mod[:, 1, 0], mod[:, 1, 1], ws, bs,
                                        sc_conv_w[l], mix_norm_g[l, 1][None, :], gdn_conv_w[l], T)
        pa, pc, pd = (a.reshape(B, T, -1) for a in (pa, pc, pd))
        y_a = nsa_mixer_pallas(pa, q_norm_g[l], k_norm_g[l], cmp_pos[l], cmp_k_w1[l], cmp_k_w2[l],
                               cmp_v_w1[l], cmp_v_w2[l], tables, ovt, mix_norm_g[l, 0][None, :])
        y_c = gated_deltanet_pallas(pc, gdn_A_log[l], gdn_dt_bias[l], gdn_norm_g[l])
        y_d = mlstm_pallas(pd, mlstm_f_bias[l], mlstm_norm_g[l])
        ys = [y.reshape(B * T, GROUP_WIDTH) for y in (y_a, y_b, y_c, y_d)]
        x2 = ffn_half_step(x2, norm_g[l, 2][None, :], mod[:, 2, 0], mod[:, 2, 1], mod[:, 2, 2],
                           ffn2_w13[l].astype(bf16), ffn2_w2[l].astype(bf16), T,
                           mix=(ys, mod[:, 1, 2], w_out[l].astype(bf16)))
    return x2.reshape(B, T, D)
```

```python
import functools
import math

import jax
import jax.numpy as jnp
import numpy as np
from jax import lax
from jax.experimental import pallas as pl
from jax.experimental.pallas import tpu as pltpu

HEAD_DIM = 64
GROUP_HEADS = 4
GROUP_WIDTH = GROUP_HEADS * HEAD_DIM
CMP_STRIDE = 16
CMP_BLOCK = 32
SLC_BLOCK = 64
N_SLC = 16
N_LOCAL_SLC = 2
WINDOW = 512
Q_BLOCK = 128
FORCE = 1e6
N_BUCKETS = 32
MAX_DISTANCE = 128
GDN_CHUNK = 64
MLSTM_CHUNK = 64
N_SUBLAYERS = 3
EPS = 1e-6

IN_LAYOUT = (
    ("a_q", 256), ("a_k_cmp", 64), ("a_v_cmp", 64),
    ("a_k_slc", 64), ("a_v_slc", 64), ("a_k_win", 64), ("a_v_win", 64),
    ("a_gate", 12),
    ("b_b", 256), ("b_c", 256), ("b_x", 256),
    ("c_q", 256), ("c_k", 256), ("c_v", 256),
    ("c_beta", 4), ("c_alpha", 4), ("c_z", 256),
    ("d_q", 256), ("d_k", 256), ("d_v", 256),
    ("d_i", 4), ("d_f", 4), ("d_o", 256),
)

VMEM_LIMIT_BYTES = 56 * 1024 * 1024
FFN_TOKEN_TILE = 512
FFN_CHUNK = 256
PROJ_TOKEN_TILE = 512
NEG = -1e30
HIGHEST = lax.Precision.HIGHEST
LOG2E = math.log2(math.e)
V1_ROWS = HEAD_DIM + 16


def _modulated_norm(x, g, scale, shift):
    y = x * lax.rsqrt(jnp.mean(x * x, axis=-1, keepdims=True) + EPS)
    return (y * g) * (1.0 + scale) + shift


def _ada_kernel(c_ref, w_ref, b_ref, o_ref):
    c = c_ref[...]
    cond = c * jax.nn.sigmoid(c)
    o_ref[...] = jnp.dot(cond.astype(jnp.bfloat16), w_ref[...].astype(jnp.bfloat16),
                         preferred_element_type=jnp.float32) + b_ref[...]


def ada_modulation(c, w, b):
    B, D = c.shape
    N = w.shape[1]
    tn = 1152
    return pl.pallas_call(
        _ada_kernel,
        grid=(N // tn,),
        in_specs=[pl.BlockSpec((B, D), lambda j: (0, 0)),
                  pl.BlockSpec((D, tn), lambda j: (0, j)),
                  pl.BlockSpec((1, tn), lambda j: (0, j))],
        out_specs=pl.BlockSpec((B, tn), lambda j: (0, j)),
        out_shape=jax.ShapeDtypeStruct((B, N), jnp.float32),
        name="ada_modulation",
    )(c, w, b)


def _ffn_kernel(x_ref, g_ref, shift_ref, scale_ref, gate_ref, w1_ref, w3_ref, w2_ref, *rest, n_mix):
    o_ref, act_ref = rest[-2:]
    x = x_ref[...]
    if n_mix:
        y_refs, mix_gate_ref, wo_ref = rest[:n_mix], rest[n_mix], rest[n_mix + 1]
        z = None
        for k, y_ref in enumerate(y_refs):
            part = jnp.dot(y_ref[...].astype(jnp.bfloat16), wo_ref[k * GROUP_WIDTH:(k + 1) * GROUP_WIDTH, :],
                           preferred_element_type=jnp.float32)
            z = part if z is None else z + part
        x = x + mix_gate_ref[0] * z
    h = _modulated_norm(x, g_ref[...], scale_ref[0], shift_ref[0]).astype(jnp.bfloat16)
    n_chunks = act_ref.shape[1] // FFN_CHUNK
    for ci in range(n_chunks):
        cs = slice(ci * FFN_CHUNK, (ci + 1) * FFN_CHUNK)
        a = jnp.dot(h, w1_ref[:, cs], preferred_element_type=jnp.float32)
        b = jnp.dot(h, w3_ref[:, cs], preferred_element_type=jnp.float32)
        act_ref[:, cs] = (a * jax.nn.sigmoid(a) * b).astype(jnp.bfloat16)
    y = jnp.dot(act_ref[...], w2_ref[...], preferred_element_type=jnp.float32)
    o_ref[...] = x + (0.5 * gate_ref[0]) * y


def ffn_half_step(x2, g, shift, scale, gate, w13, w2, tokens_per_batch, mix=None):
    M, D = x2.shape
    F = w2.shape[0]
    tm = FFN_TOKEN_TILE
    tiles_per_batch = tokens_per_batch // tm
    resident = dict(pipeline_mode=pl.Buffered(1))
    mod_spec = pl.BlockSpec((1, 1, D), lambda i: (i // tiles_per_batch, 0, 0))
    ys, mix_gate, w_out = mix if mix is not None else ((), None, None)
    mix_args = tuple(ys) + ((mix_gate, w_out) if ys else ())
    mix_specs = [pl.BlockSpec((tm, GROUP_WIDTH), lambda i: (i, 0)) for _ in ys]
    if ys:
        mix_specs += [mod_spec, pl.BlockSpec(w_out.shape, lambda i: (0, 0), **resident)]
    return pl.pallas_call(
        functools.partial(_ffn_kernel, n_mix=len(ys)),
        grid=(M // tm,),
        in_specs=[pl.BlockSpec((tm, D), lambda i: (i, 0)),
                  pl.BlockSpec((1, D), lambda i: (0, 0)),
                  mod_spec, mod_spec, mod_spec,
                  pl.BlockSpec((D, F), lambda i: (0, 0), **resident),
                  pl.BlockSpec((D, F), lambda i: (0, 1), **resident),
                  pl.BlockSpec((F, D), lambda i: (0, 0), **resident)] + mix_specs,
        out_specs=pl.BlockSpec((tm, D), lambda i: (i, 0)),
        out_shape=jax.ShapeDtypeStruct((M, D), jnp.float32),
        scratch_shapes=[pltpu.VMEM((tm, F), jnp.bfloat16)],
        compiler_params=pltpu.CompilerParams(dimension_semantics=("arbitrary",),
                                             vmem_limit_bytes=VMEM_LIMIT_BYTES),
        name="ffn_half_step",
    )(x2, g, shift, scale, gate, w13, w13, w2, *mix_args)


CONV_CARRY_ROWS = 8
CONV_ROW_CHUNK = 128


def _in_proj_kernel(x_ref, g_ref, shift_ref, scale_ref, wa_ref, wb_ref, wc_ref, wd_ref,
                    ba_ref, bb_ref, bc_ref, bd_ref, scw_ref, scg_ref, gcw_ref,
                    oa_ref, yb_ref, oc_ref, od_ref, braw_ref, craw_ref, *, tiles_per_batch):
    f32 = jnp.float32
    rows = x_ref.shape[0]
    hist = CONV_CARRY_ROWS
    first = pl.program_id(0) % tiles_per_batch == 0

    @pl.when(first)
    def _():
        braw_ref[0:hist, :] = jnp.zeros((hist, braw_ref.shape[1]), f32)
        craw_ref[0:hist, :] = jnp.zeros((hist, craw_ref.shape[1]), f32)

    @pl.when(jnp.logical_not(first))
    def _():
        braw_ref[0:hist, :] = braw_ref[rows:rows + hist, :]
        craw_ref[0:hist, :] = craw_ref[rows:rows + hist, :]

    h = _modulated_norm(x_ref[...], g_ref[...], scale_ref[0], shift_ref[0]).astype(jnp.bfloat16)
    project = lambda w_ref, b_ref: jnp.dot(h, w_ref[...], preferred_element_type=f32) + b_ref[...]
    pb = project(wb_ref, bb_ref)
    yb_ref[...] = pb[:, 0:256]
    braw_ref[hist:hist + rows, :] = pb[:, 256:512] * pb[:, 512:768]
    pc = project(wc_ref, bc_ref)
    craw_ref[hist:hist + rows, :] = pc[:, 0:768]
    oc_ref[:, 768:] = pc[:, 768:]
    n_chunks = rows // CONV_ROW_CHUNK
    pieces = []
    for w_ref, b_ref, o_ref in ((wa_ref, ba_ref, oa_ref), (wd_ref, bd_ref, od_ref)):
        half = w_ref.shape[1] // 256 // 2 * 256
        pieces += [(w_ref, b_ref, o_ref, 0, half), (w_ref, b_ref, o_ref, half, w_ref.shape[1])]
    for ci, r0 in enumerate(range(0, rows, CONV_ROW_CHUNK)):
        for w_ref, b_ref, o_ref, c0, c1 in pieces[ci * len(pieces) // n_chunks:(ci + 1) * len(pieces) // n_chunks]:
            o_ref[:, c0:c1] = jnp.dot(h, w_ref[:, c0:c1], preferred_element_type=f32) + b_ref[:, c0:c1]
        out_rows = slice(r0, r0 + CONV_ROW_CHUNK)
        tap = lambda ref, s: ref[pl.ds(hist + r0 - s, CONV_ROW_CHUNK), :]
        conv = sum(tap(braw_ref, s) * scw_ref[2 - s:3 - s, :] for s in range(3))
        y = yb_ref[out_rows, :] * conv
        yb_ref[out_rows, :] = y * lax.rsqrt(jnp.mean(y * y, axis=-1, keepdims=True) + EPS) * scg_ref[...]
        for c0 in range(0, 3 * GROUP_WIDTH, GROUP_WIDTH):
            cols = slice(c0, c0 + GROUP_WIDTH)
            conv = sum(craw_ref[pl.ds(hist + r0 - s, CONV_ROW_CHUNK), cols] * gcw_ref[3 - s:4 - s, cols]
                       for s in range(4))
            oc_ref[out_rows, cols] = conv * jax.nn.sigmoid(conv)


def in_projection(x2, g, shift, scale, ws, bs, sc_conv_w, sc_norm_g, gdn_conv_w, tokens_per_batch):
    M, D = x2.shape
    tm = PROJ_TOKEN_TILE
    tiles_per_batch = tokens_per_batch // tm
    mod_spec = pl.BlockSpec((1, 1, D), lambda i: (i // tiles_per_batch, 0, 0))
    const = lambda a, **kw: pl.BlockSpec(a.shape, lambda i: (0, 0), **kw)
    pad8 = lambda w: jnp.pad(w, ((0, 8 - w.shape[0]), (0, 0)))
    extras = (pad8(sc_conv_w), sc_norm_g, pad8(gdn_conv_w))
    widths = (ws[0].shape[1], GROUP_WIDTH, ws[2].shape[1], ws[3].shape[1])
    return pl.pallas_call(
        functools.partial(_in_proj_kernel, tiles_per_batch=tiles_per_batch),
        grid=(M // tm,),
        in_specs=[pl.BlockSpec((tm, D), lambda i: (i, 0)), const(g), mod_spec, mod_spec]
                 + [const(w, pipeline_mode=pl.Buffered(1)) for w in ws] + [const(b) for b in bs]
                 + [const(e) for e in extras],
        out_specs=[pl.BlockSpec((tm, n), lambda i: (i, 0)) for n in widths],
        out_shape=[jax.ShapeDtypeStruct((M, n), jnp.float32) for n in widths],
        scratch_shapes=[pltpu.VMEM((CONV_CARRY_ROWS + tm, GROUP_WIDTH), jnp.float32),
                        pltpu.VMEM((CONV_CARRY_ROWS + tm, 3 * GROUP_WIDTH), jnp.float32)],
        compiler_params=pltpu.CompilerParams(dimension_semantics=("arbitrary",),
                                             vmem_limit_bytes=VMEM_LIMIT_BYTES),
        name="in_projection",
    )(x2, g, shift, scale, *ws, *bs, *extras)


def _group_columns():
    offs, o = {}, 0
    for n, s in IN_LAYOUT:
        offs[n] = np.arange(o, o + s)
        o += s
    cat = lambda names: np.concatenate([offs[n] for n in names])
    return (cat(["a_q", "a_k_cmp", "a_v_cmp", "a_k_slc", "a_v_slc", "a_k_win", "a_v_win", "a_gate"]),
            cat(["b_b", "b_c", "b_x"]),
            cat(["c_q", "c_k", "c_v", "c_z", "c_beta", "c_alpha"]),
            cat(["d_q", "d_k", "d_v", "d_o", "d_i", "d_f"]))


GROUP_SLAB_WIDTH = (768, 768, 1152, 1152)


def _take_columns(a, idx, width):
    cuts = [0] + [k + 1 for k in range(idx.size - 1) if idx[k + 1] != idx[k] + 1] + [idx.size]
    runs = [a[..., int(idx[s]):int(idx[e - 1]) + 1] for s, e in zip(cuts[:-1], cuts[1:])]
    return jnp.concatenate(runs + [jnp.zeros(a.shape[:-1] + (width - idx.size,), a.dtype)], axis=-1)


def split_in_weights(w, b):
    ws, bs = [], []
    for idx, width in zip(_group_columns(), GROUP_SLAB_WIDTH):
        ws.append(_take_columns(w, idx, width).astype(jnp.bfloat16))
        bs.append(_take_columns(b, idx, width)[None, :])
    return ws, bs


def t5_bucket(dist):
    n = jnp.maximum(dist, 0)
    max_exact = N_BUCKETS // 2
    nf = jnp.maximum(n, 1).astype(jnp.float32)
    large = max_exact + (jnp.log(nf / max_exact) / math.log(MAX_DISTANCE / max_exact)
                         * (N_BUCKETS - max_exact)).astype(jnp.int32)
    large = jnp.minimum(large, N_BUCKETS - 1)
    return jnp.where(n < max_exact, n, large)


def _nt_dot(a, b):
    return lax.dot_general(a, b, (((1,), (1,)), ((), ())), preferred_element_type=jnp.float32)


def _nsa_prep_kernel(q_ref, kvs_ref, kvw_ref, qg_ref, kg_ref, seg_ref,
                     qnt_ref, kxs_ref, v1s_ref, kkw_ref, v1w_ref):
    tm = q_ref.shape[0]
    q = q_ref[...]
    ss = _dot_sel(q * q, seg_ref[...], parts=3)
    qn = q * lax.rsqrt(ss * (1.0 / HEAD_DIM) + EPS) * qg_ref[...]
    qnt_ref[...] = (qn * LOG2E).T.astype(qnt_ref.dtype)
    lane = lax.broadcasted_iota(jnp.int32, (tm, 128), 1)
    lo = lane < HEAD_DIM

    def split(x):
        xr = pltpu.roll(x, HEAD_DIM, 1)
        ss = jnp.sum(jnp.where(lo, x * x, 0.0), axis=-1, keepdims=True)
        kk = jnp.where(lo, x, xr) * lax.rsqrt(ss * (1.0 / HEAD_DIM) + EPS) * kg_ref[...]
        return kk, jnp.where(lo, xr, 1.0)

    kk, v1 = split(kvs_ref[...])
    tok = pl.program_id(1) * tm + lax.broadcasted_iota(jnp.int32, (tm, 128), 0)
    onehot = jnp.where(lane == tok // SLC_BLOCK, 1.0, 0.0)
    kxs_ref[...] = jnp.concatenate([kk, onehot], axis=1).astype(kxs_ref.dtype)
    v1s_ref[...] = v1.T[0:V1_ROWS].astype(v1s_ref.dtype)
    kk, v1 = split(kvw_ref[...])
    kkw_ref[...] = kk.astype(kkw_ref.dtype)
    v1w_ref[...] = v1.T[0:V1_ROWS].astype(v1w_ref.dtype)


def nsa_prep(pa, qg, kg2, seg):
    B, T, _ = pa.shape
    tm = 512
    bf16 = jnp.bfloat16
    col = lambda w, j: pl.BlockSpec((None, tm, w), lambda b, i: (b, i, j))
    rowblk = lambda h: pl.BlockSpec((None, h, tm), lambda b, i: (b, 0, i))
    const = lambda shape: pl.BlockSpec(shape, lambda b, i: (0, 0))
    return pl.pallas_call(
        _nsa_prep_kernel,
        grid=(B, T // tm),
        in_specs=[col(256, 0), col(128, 3), col(128, 4), const((1, 256)), const((1, 128)), const((256, 256))],
        out_specs=[rowblk(256), col(256, 0), rowblk(V1_ROWS), col(128, 0), rowblk(V1_ROWS)],
        out_shape=[jax.ShapeDtypeStruct((B, 256, T), bf16),
                   jax.ShapeDtypeStruct((B, T, 256), bf16), jax.ShapeDtypeStruct((B, V1_ROWS, T), bf16),
                   jax.ShapeDtypeStruct((B, T, 128), bf16), jax.ShapeDtypeStruct((B, V1_ROWS, T), bf16)],
        name="nsa_prep",
    )(pa, pa, pa, qg, kg2, seg)


def _compress_kernel(kv_ref, pos_ref, w1_ref, w2k_ref, w2v_ref, kg_ref, kkc_ref, vct_ref):
    nrow = kkc_ref.shape[0]
    f32, bf16 = jnp.float32, jnp.bfloat16
    hidden = w1_ref.shape[2] // 2
    first = jnp.zeros((nrow, 2 * hidden), f32)
    second = jnp.zeros((nrow, 2 * hidden), f32)
    for p in range(CMP_STRIDE):
        xp = kv_ref[pl.ds(p, nrow, stride=CMP_STRIDE), :]
        first = first + jnp.dot((xp + pos_ref[p:p + 1, :]).astype(bf16), w1_ref[p], preferred_element_type=f32)
        second = second + jnp.dot((xp + pos_ref[CMP_STRIDE + p:CMP_STRIDE + p + 1, :]).astype(bf16),
                                  w1_ref[CMP_STRIDE + p], preferred_element_type=f32)
    hid = first + pltpu.roll(second, nrow - 1, 0)
    hid = (hid * jax.nn.sigmoid(hid)).astype(bf16)
    valid = lax.broadcasted_iota(jnp.int32, (nrow, 128), 0) < nrow - 1
    kc = jnp.dot(hid[:, 0:hidden], w2k_ref[...], preferred_element_type=f32)
    kc = kc * lax.rsqrt(jnp.mean(kc * kc, axis=-1, keepdims=True) + EPS) * kg_ref[...]
    kkc_ref[...] = jnp.where(valid, kc, 0.0).astype(kkc_ref.dtype)
    vc = jnp.dot(hid[:, hidden:2 * hidden], w2v_ref[...], preferred_element_type=f32)
    vct_ref[...] = jnp.where(valid, vc, 0.0).T[0:HEAD_DIM].astype(vct_ref.dtype)


def nsa_compress(pa, pos2, w1, w2k, w2v, kg2):
    B, T, _ = pa.shape
    nrow = T // CMP_STRIDE
    bf16 = jnp.bfloat16
    full = lambda a: pl.BlockSpec(a.shape, lambda b: (0,) * a.ndim)
    return pl.pallas_call(
        _compress_kernel,
        grid=(B,),
        in_specs=[pl.BlockSpec((None, T, 128), lambda b: (b, 0, 2)), full(pos2), full(w1), full(w2k), full(w2v),
                  full(kg2)],
        out_specs=[pl.BlockSpec((None, nrow, 128), lambda b: (b, 0, 0)),
                   pl.BlockSpec((None, HEAD_DIM, nrow), lambda b: (b, 0, 0))],
        out_shape=[jax.ShapeDtypeStruct((B, nrow, 128), bf16), jax.ShapeDtypeStruct((B, HEAD_DIM, nrow), bf16)],
        name="nsa_compress",
    )(pa, pos2, w1, w2k, w2v, kg2)


CMP_TILES_PER_STEP = 4


def _stack_heads_t(qt):
    lo = lax.broadcasted_iota(jnp.int32, (128, qt.shape[1]), 0) < HEAD_DIM
    zero = jnp.zeros((), qt.dtype)
    return jnp.concatenate([jnp.where(lo if h % 2 == 0 else ~lo, qt[128 * (h // 2):128 * (h // 2) + 128], zero)
                            for h in range(GROUP_HEADS)], axis=1)


def _cmp_select_kernel(qnt_ref, kkc_ref, vct_ref, cb_ref, ov_ref, ocmp_ref, pen_ref, sc_ref, *, n_sel):
    i = pl.program_id(1)
    n_tiles = pen_ref.shape[0]
    tq = qnt_ref.shape[1] // n_tiles
    nblk, nrow = ov_ref.shape
    keys_per_tile = tq // CMP_STRIDE
    f32, bf16 = jnp.float32, jnp.bfloat16
    blk = lax.broadcasted_iota(jnp.int32, (nblk, tq), 0)
    scores = []
    for g in range(n_tiles):
        cols = slice(g * tq, (g + 1) * tq)
        q4t = _stack_heads_t(qnt_ref[:, cols])
        first_row = pl.multiple_of(nrow - keys_per_tile * (i * n_tiles + g), keys_per_tile)
        s = jnp.dot(kkc_ref[...], q4t, preferred_element_type=f32) + cb_ref[pl.ds(first_row, nrow), :]
        m = jnp.max(s, axis=0, keepdims=True)
        e = jnp.where(s > 0.5 * NEG, jnp.exp2(s - m), 0.0)
        p = e * (1.0 / jnp.maximum(jnp.sum(e, axis=0, keepdims=True), 1e-30))
        o = jnp.dot(vct_ref[...], p.astype(bf16), preferred_element_type=f32)
        for h in range(GROUP_HEADS):
            ocmp_ref[h * HEAD_DIM:(h + 1) * HEAD_DIM, cols] = o[:, h * tq:(h + 1) * tq]
        psum = p[:, 0:tq] + p[:, tq:2 * tq] + p[:, 2 * tq:3 * tq] + p[:, 3 * tq:4 * tq]
        hi = psum.astype(bf16)
        lo_part = (psum - hi.astype(f32)).astype(bf16)
        score = (jnp.dot(ov_ref[...], hi, preferred_element_type=f32)
                 + jnp.dot(ov_ref[...], lo_part, preferred_element_type=f32))
        qpos = (i * n_tiles + g) * tq + lax.broadcasted_iota(jnp.int32, (nblk, tq), 1)
        cur = qpos // SLC_BLOCK
        forced = (blk == 0) | ((cur - blk >= 0) & (cur - blk < N_LOCAL_SLC))
        score = jnp.where(forced, FORCE, score)
        score = jnp.where(blk <= cur, score, -FORCE)
        sc_ref[g] = score
        scores.append(score)
    blocks_per_tile = tq // SLC_BLOCK

    def rank_step(t, cnts):
        out = []
        for g in range(n_tiles):
            cnt = cnts[g]
            for u in range(blocks_per_tile):
                jp = t * blocks_per_tile + u
                row = sc_ref[g, pl.ds(jp, 1), :]
                tie = jnp.where(blk > jp, 1.0, 0.0)
                cnt = cnt + jnp.where(row > scores[g], 1.0, jnp.where(row == scores[g], tie, 0.0))
            out.append(cnt)
        return tuple(out)

    zeros = tuple(jnp.zeros((nblk, tq), f32) for _ in range(n_tiles))
    cnts = lax.fori_loop(0, (i + 1) * n_tiles, rank_step, zeros)
    for g in range(n_tiles):
        pen = jnp.where(cnts[g] < n_sel, 0.0, NEG)
        if nblk < 128:
            pen = jnp.concatenate([pen, jnp.zeros((128 - nblk, tq), f32)], axis=0)
        pen_ref[g] = pen.astype(pen_ref.dtype)


def nsa_cmp_select(qnt, kkc, vct, cbias, ov, n_sel):
    B, _, T = qnt.shape
    nrow = kkc.shape[1]
    nblk = ov.shape[0]
    tq = Q_BLOCK
    g = min(CMP_TILES_PER_STEP, T // tq)
    return pl.pallas_call(
        functools.partial(_cmp_select_kernel, n_sel=n_sel),
        grid=(B, T // (g * tq)),
        in_specs=[pl.BlockSpec((None, 256, g * tq), lambda b, i: (b, 0, i)),
                  pl.BlockSpec((None, nrow, 128), lambda b, i: (b, 0, 0)),
                  pl.BlockSpec((None, HEAD_DIM, nrow), lambda b, i: (b, 0, 0)),
                  pl.BlockSpec((2 * nrow, 4 * tq), lambda b, i: (0, 0)),
                  pl.BlockSpec((nblk, nrow), lambda b, i: (0, 0))],
        out_specs=[pl.BlockSpec((None, 256, g * tq), lambda b, i: (b, 0, i)),
                   pl.BlockSpec((None, g, 128, tq), lambda b, i: (b, i, 0, 0))],
        out_shape=[jax.ShapeDtypeStruct((B, 256, T), jnp.float32),
                   jax.ShapeDtypeStruct((B, T // tq, 128, tq), jnp.bfloat16)],
        scratch_shapes=[pltpu.VMEM((g, nblk, tq), jnp.float32)],
        name="nsa_cmp_select",
    )(qnt, kkc, vct, cbias, ov)


NSA_KEY_TILE = 256
NSA_QUERY_TILE = 256


def _nsa_attn_kernel(qnt_ref, pen_ref, ocmp_ref, gate_ref, kxs_ref, v1s_ref, kkw_ref, v1w_ref,
                     tabs_ref, tabw_ref, mixg_ref, o_ref, acc_ref, m_ref, sa_ref, sb_ref):
    i = pl.program_id(1)
    tq = o_ref.shape[0]
    kt = tabs_ref.shape[1]
    f32, bf16 = jnp.float32, jnp.bfloat16
    q4t = _stack_heads_t(qnt_ref[...])
    pen = jnp.concatenate([pen_ref[t] for t in range(pen_ref.shape[0])], axis=1)
    rhs_sel = jnp.concatenate([q4t, jnp.concatenate([pen] * GROUP_HEADS, axis=1)], axis=0)

    def branch(rhs, k_ref, v_ref, tab_ref, first_tile, last_tile):
        n_tab = tab_ref.shape[0] - 1
        m_ref[...] = jnp.full(m_ref.shape, NEG, f32)
        acc_ref[...] = jnp.zeros(acc_ref.shape, f32)

        def key_offset(j):
            return pl.multiple_of(jnp.minimum(j, last_tile) * kt, kt)

        def scores(j):
            delta = jnp.where(j > last_tile, n_tab, jnp.minimum(i - j * (kt // tq), n_tab - 1))
            return jnp.dot(k_ref[pl.ds(key_offset(j), kt), :], rhs, preferred_element_type=f32) + tab_ref[delta]

        def absorb(s, j):
            m_old = m_ref[...]
            m_new = jnp.maximum(m_old, jnp.max(s, axis=0, keepdims=True))
            p = jnp.exp2(s - m_new)
            pv =jnp.dot(v_ref[:, pl.ds(key_offset(j), kt)], p.astype(bf16), preferred_element_type=f32)
            acc_ref[...] = jnp.exp2(m_old - m_new) * acc_ref[...] + pv
            m_ref[...] = m_new

        sa_ref[...] = scores(first_tile)

        def body(t, carry):
            j = first_tile + 2 * t
            sb_ref[...] = scores(j + 1)
            absorb(sa_ref[...], j)
            sa_ref[...] = scores(j + 2)
            absorb(sb_ref[...], j + 1)
            return carry

        lax.fori_loop(0, (last_tile - first_tile) // 2 + 1, body, 0)
        acc = acc_ref[...]
        return acc[0:HEAD_DIM] / jnp.maximum(acc[HEAD_DIM:HEAD_DIM + 1], 1e-30)

    def window_branch(rhs):
        n_tab = tabw_ref.shape[0] - 1
        m = jnp.full((1, rhs.shape[1]), NEG, f32)
        acc = jnp.zeros((V1_ROWS, rhs.shape[1]), f32)
        for d in range(n_tab):
            j = i - d
            off = pl.multiple_of(jnp.maximum(j, 0) * kt, kt)
            s = jnp.dot(kkw_ref[pl.ds(off, kt), :], rhs, preferred_element_type=f32)
            s = s + tabw_ref[jnp.where(j < 0, n_tab, d)]
            m_new = jnp.maximum(m, jnp.max(s, axis=0, keepdims=True))
            p = jnp.exp2(s - m_new)
            acc = jnp.exp2(m - m_new) * acc +jnp.dot(v1w_ref[:, pl.ds(off, kt)], p.astype(bf16),
                                                     preferred_element_type=f32)
            m = m_new
        return acc[0:HEAD_DIM] / jnp.maximum(acc[HEAD_DIM:HEAD_DIM + 1], 1e-30)

    last = (i * tq) // kt
    if kt == tq:
        o_w = window_branch(q4t)
    else:
        o_w = branch(q4t, kkw_ref, v1w_ref, tabw_ref, (jnp.maximum(i - WINDOW // tq, 0) * tq) // kt, last)
    o_s = branch(rhs_sel, kxs_ref, v1s_ref, tabs_ref, 0, last)
    gt = jax.nn.sigmoid(gate_ref[...]).T
    oct_ = ocmp_ref[...]
    parts = []
    for h in range(GROUP_HEADS):
        cols = slice(h * tq, (h + 1) * tq)
        parts.append(gt[3 * h:3 * h + 1] * oct_[h * HEAD_DIM:(h + 1) * HEAD_DIM]
                     + gt[3 * h + 1:3 * h + 2] * o_s[:, cols] + gt[3 * h + 2:3 * h + 3] * o_w[:, cols])
    yt = jnp.concatenate(parts, axis=0)
    yt = yt * lax.rsqrt(jnp.mean(yt * yt, axis=0, keepdims=True) + EPS) * mixg_ref[...]
    o_ref[...] = yt.T


def nsa_attention(qnt, pen, ocmp, pa, kxs, v1s, kkw, v1w, tab_s, tab_w, mixg):
    B, _, T = qnt.shape
    tq = min(NSA_QUERY_TILE, T)
    tile = lambda w, j: pl.BlockSpec((None, tq, w), lambda b, i: (b, i, j))
    whole = lambda w: pl.BlockSpec((None, T, w), lambda b, i: (b, 0, 0))
    whole_t = pl.BlockSpec((None, V1_ROWS, T), lambda b, i: (b, 0, 0))
    full = lambda a: pl.BlockSpec(a.shape, lambda b, i: (0,) * a.ndim)
    return pl.pallas_call(
        _nsa_attn_kernel,
        grid=(B, T // tq),
        in_specs=[pl.BlockSpec((None, 256, tq), lambda b, i: (b, 0, i)),
                  pl.BlockSpec((None, tq // Q_BLOCK, 128, Q_BLOCK), lambda b, i: (b, i, 0, 0)),
                  pl.BlockSpec((None, 256, tq), lambda b, i: (b, 0, i)), tile(128, 5),
                  whole(256), whole_t, whole(128), whole_t, full(tab_s), full(tab_w), full(mixg)],
        out_specs=tile(256, 0),
        out_shape=jax.ShapeDtypeStruct((B, T, 256), jnp.float32),
        scratch_shapes=[pltpu.VMEM((V1_ROWS, GROUP_HEADS * tq), jnp.float32),
                        pltpu.VMEM((1, GROUP_HEADS * tq), jnp.float32),
                        pltpu.VMEM((tab_s.shape[1], GROUP_HEADS * tq), jnp.float32),
                        pltpu.VMEM((tab_s.shape[1], GROUP_HEADS * tq), jnp.float32)],
        compiler_params=pltpu.CompilerParams(dimension_semantics=("arbitrary", "arbitrary"),
                                             vmem_limit_bytes=VMEM_LIMIT_BYTES),
        name="nsa_attention",
    )(qnt, pen, ocmp, pa, kxs, v1s, kkw, v1w, tab_s, tab_w, mixg)


def _toeplitz(wr, rows, cols):
    H, n = wr.shape
    flat = jnp.tile(jnp.pad(wr, ((0, 0), (0, 1))), (1, rows))[:, :rows * n]
    return flat.reshape(H, rows, n)[:, :, rows - 1:rows - 1 + cols]


def nsa_bias_tables(t5_table, T):
    H = GROUP_HEADS
    tq = Q_BLOCK
    ta = min(NSA_QUERY_TILE, T)
    kt = min(NSA_KEY_TILE, T)
    nrow = T // CMP_STRIDE
    dmax = WINDOW + 2 * ta + kt
    onehot = (t5_bucket(jnp.arange(dmax))[:, None] == jnp.arange(N_BUCKETS)[None, :]).astype(jnp.float32)
    by_dist = jnp.dot(onehot, t5_table, precision=HIGHEST).T
    neg = lambda n: jnp.full((H, n), NEG, jnp.float32)
    rev = by_dist[:, ::-1] * LOG2E
    f_s = jnp.concatenate([rev, neg(kt - 1)], axis=1)
    f_w = jnp.concatenate([neg(dmax - WINDOW), rev[:, dmax - WINDOW:], neg(kt - 1)], axis=1)
    span = ta + kt - 1
    total = dmax + kt - 1

    def tiles(f, n):
        out = [_toeplitz(f[:, total - k * ta - span:total - k * ta], ta, kt).reshape(H * ta, kt).T for k in range(n)]
        return jnp.stack(out + [jnp.full((kt, H * ta), NEG, jnp.float32)])

    n_s = -(-(kt + MAX_DISTANCE) // ta) + 1
    n_w = WINDOW // ta + kt // ta
    tab_s, tab_w = tiles(f_s, n_s), tiles(f_w, n_w)
    per_tile = tq // CMP_STRIDE
    d0 = -CMP_STRIDE * (nrow - 1) - (CMP_BLOCK - 1)
    n16 = 2 * nrow + per_tile - 1
    n_pos = CMP_STRIDE * n16 + d0
    far = jnp.broadcast_to(by_dist[:, -1:], (H, max(n_pos - dmax, 0)))
    w16 = jnp.concatenate([neg(-d0), by_dist[:, :n_pos] * LOG2E, far * LOG2E], axis=1).reshape(H, n16, CMP_STRIDE)
    w16 = w16[:, ::-1, :]
    base = jnp.stack([w16[:, per_tile - 1 - a:per_tile - 1 - a + 2 * nrow, :] for a in range(per_tile)], axis=1)
    cbias = base.transpose(0, 1, 3, 2).reshape(H * tq, 2 * nrow).T
    return tab_s, tab_w, cbias


def nsa_overlap_t(T):
    nrow = T // CMP_STRIDE
    nblk = T // SLC_BLOCK
    ci = np.arange(nrow)[None, :]
    bj = np.arange(nblk)[:, None]
    ov = (ci * CMP_STRIDE < (bj + 1) * SLC_BLOCK) & (ci * CMP_STRIDE + CMP_BLOCK > bj * SLC_BLOCK)
    ov = ov & (ci < nrow - 1)
    return jnp.asarray(ov.astype(np.float32), dtype=jnp.bfloat16)


def nsa_mixer_pallas(pa, q_norm_g, k_norm_g, cmp_pos, cmp_k_w1, cmp_k_w2, cmp_v_w1, cmp_v_w2,
                     tables, ovt, mixg):
    B, T, _ = pa.shape
    bf16 = jnp.bfloat16
    tab_s, tab_w, cbias = tables
    qg = (jnp.tile(q_norm_g, GROUP_HEADS) * HEAD_DIM ** -0.5)[None, :]
    kg2 = jnp.tile(k_norm_g, 2)[None, :]
    qnt, kxs, v1s, kkw, v1w = nsa_prep(pa, qg, kg2, _head_mask().astype(bf16))
    dup = lambda w: jnp.concatenate([w, w], axis=1)
    zeros = jnp.zeros((CMP_BLOCK, HEAD_DIM, cmp_k_w1.shape[1]), jnp.float32)
    per_pos = lambda w: w.reshape(CMP_BLOCK, HEAD_DIM, -1)
    w1 = jnp.concatenate([jnp.concatenate([per_pos(cmp_k_w1), zeros], axis=2),
                          jnp.concatenate([zeros, per_pos(cmp_v_w1)], axis=2)], axis=1).astype(bf16)
    kkc, vct = nsa_compress(pa, dup(cmp_pos), w1, dup(cmp_k_w2).astype(bf16), dup(cmp_v_w2).astype(bf16), kg2)
    n_sel = min(N_SLC, T // SLC_BLOCK)
    ocmp, pen = nsa_cmp_select(qnt, kkc, vct, cbias, ovt, n_sel)
    mixg_col = jnp.broadcast_to(mixg.reshape(-1, 1), (GROUP_WIDTH, min(NSA_QUERY_TILE, T)))
    return nsa_attention(qnt, pen, ocmp, pa, kxs, v1s, kkw, v1w, tab_s, tab_w, mixg_col)


CHUNKS_PER_STEP = 4
SEQS_PER_STEP = 2


def _tn_dot(a, b):
    return lax.dot_general(a, b, (((0,), (0,)), ((), ())), preferred_element_type=jnp.float32)


def _split_bf16(a, parts):
    out, rest = [], a
    for _ in range(parts):
        piece = rest.astype(jnp.bfloat16)
        out.append(piece)
        rest = rest - piece.astype(jnp.float32)
    return out


def _dot_sel(a, sel, parts=3):
    return sum(jnp.dot(p, sel, preferred_element_type=jnp.float32) for p in _split_bf16(a, parts))


def _sel_dot(sel, a, parts=3):
    return sum(jnp.dot(sel, p, preferred_element_type=jnp.float32) for p in _split_bf16(a, parts))


def _softplus(x):
    return jnp.maximum(x, 0.0) + jnp.log(1.0 + jnp.exp(-jnp.abs(x)))


def _block_diag(a, hmb):
    return jnp.concatenate([a.astype(jnp.bfloat16)] * GROUP_HEADS, axis=0) * hmb


def _diag_row(a, eye):
    return jnp.sum(a * eye, axis=0, keepdims=True)


def _head_norm(o, hmb):
    return o * lax.rsqrt(_dot_sel(o * o, hmb, parts=2) * (1.0 / HEAD_DIM) + EPS)


def _head_l2(o, hmb):
    return o * lax.rsqrt(_dot_sel(o * o, hmb, parts=2) + EPS)


def _xform_masks(L):
    i = np.arange(L)[:, None]
    j = np.arange(L)[None, :]
    masks = []
    s = 1
    while s < L:
        masks.append((i // (2 * s) == j // (2 * s)) & (i % (2 * s) >= s) & (j % (2 * s) < s))
        s *= 2
    masks += [i == j, j <= i, j < i]
    return jnp.asarray(np.tile(np.stack(masks).astype(np.float32), (1, 1, GROUP_HEADS)))


def _expand_matrix(first_lane):
    e = np.zeros((128, GROUP_WIDTH), np.float32)
    for h in range(GROUP_HEADS):
        e[first_lane + h, h * HEAD_DIM:(h + 1) * HEAD_DIM] = 1.0
    return jnp.asarray(e, jnp.bfloat16)


def _head_mask():
    return jnp.asarray(np.kron(np.eye(GROUP_HEADS), np.ones((HEAD_DIM, HEAD_DIM))), jnp.float32)


def _per_head_rows(*vecs):
    rows = [jnp.repeat(v, HEAD_DIM) if v.shape[0] == GROUP_HEADS else jnp.tile(v, GROUP_HEADS) for v in vecs]
    return jnp.pad(jnp.stack(rows), ((0, 8 - len(rows)), (0, 0)))


def _chunk_consts(L, n_chunks):
    tri = np.kron(np.eye(n_chunks), np.tril(np.ones((L, L)))).astype(np.float32)
    return (_head_mask(), jnp.asarray(tri, jnp.bfloat16), _expand_matrix(0), _expand_matrix(GROUP_HEADS),
            _xform_masks(L))


def _const_spec(a):
    return pl.BlockSpec(a.shape, lambda b, t: (0,) * a.ndim)


def _gdn_kernel(pc_ref, hp_ref, hm_ref, tri_ref, eb_ref, ea_ref, lv_ref, o_ref, s_ref):
    L = GDN_CHUNK
    n_seq, rows = pc_ref.shape[0], pc_ref.shape[1]
    f32, bf16 = jnp.float32, jnp.bfloat16
    n_lev = lv_ref.shape[0] - 3

    @pl.when(pl.program_id(1) == 0)
    def _():
        s_ref[...] = jnp.zeros(s_ref.shape, f32)

    hm = hm_ref[...]
    hmb = hm.astype(bf16)
    eye, incl, strict = lv_ref[n_lev], lv_ref[n_lev + 1], lv_ref[n_lev + 2]

    q_c, k_c, v_c, beta_c, gc_c = [], [], [], [], []
    for b in range(n_seq):
        qkv = pc_ref[b, :, 0:768]
        q_all = _head_l2(qkv[:, 0:256], hmb) * HEAD_DIM ** -0.5
        k_all = _head_l2(qkv[:, 256:512], hmb)
        tail = pc_ref[b, :, 1024:1152]
        beta_all = jax.nn.sigmoid(_dot_sel(tail, eb_ref[...]))
        g = hp_ref[0:1, :] * _softplus(_dot_sel(tail, ea_ref[...]) + hp_ref[1:2, :])
        gc_all = _sel_dot(tri_ref[...], g)
        for c in range(rows // L):
            sl = slice(c * L, (c + 1) * L)
            q_c.append(q_all[sl])
            k_c.append(k_all[sl])
            v_c.append(qkv[sl, 512:768])
            beta_c.append(beta_all[sl])
            gc_c.append(gc_all[sl])
    chains = range(len(q_c))
    egc = [jnp.exp(gc) for gc in gc_c]
    kb = [k_c[i] * beta_c[i] for i in chains]
    a_mat, qk = [], []
    for i in chains:
        seg = jnp.exp(jnp.where(incl > 0.5, gc_c[i] - _diag_row(gc_c[i], eye), NEG))
        k_bd = _block_diag(k_c[i], hmb)
        a_mat.append(_nt_dot(kb[i].astype(bf16), k_bd) * seg * strict)
        qk.append((_nt_dot(q_c[i].astype(bf16), k_bd) * seg).astype(bf16))
    t_inv = [eye - a * lv_ref[0] for a in a_mat]
    for lev in range(1, n_lev):
        te = [jnp.dot(t_inv[i].astype(bf16), _block_diag(a_mat[i] * lv_ref[lev], hmb),
                      preferred_element_type=f32) for i in chains]
        t_inv = [t_inv[i] - jnp.dot(te[i].astype(bf16), _block_diag(t_inv[i], hmb), preferred_element_type=f32)
                 for i in chains]
    parts = []
    for i in chains:
        tb = t_inv[i].astype(bf16)
        u = jnp.dot(tb, _block_diag(v_c[i] * beta_c[i], hmb), preferred_element_type=f32)
        w = jnp.dot(tb, _block_diag(kb[i] * egc[i], hmb), preferred_element_type=f32)
        k_dec = k_c[i] * jnp.exp(gc_c[i][L - 1:L, :] - gc_c[i])
        parts.append((u, w.astype(bf16), (q_c[i] * egc[i]).astype(bf16), qk[i], k_dec.astype(bf16),
                      egc[i][L - 1:L, :]))
    per_seq = rows // L
    state = [s_ref[b] for b in range(n_seq)]
    outs = [[] for _ in range(n_seq)]
    for c in range(per_seq):
        for b in range(n_seq):
            u, w, q_dec, qk_c, k_dec, g_tot = parts[b * per_seq + c]
            sb = state[b].astype(bf16)
            v_new = u - jnp.dot(w, sb, preferred_element_type=f32)
            outs[b].append(jnp.dot(q_dec, sb, preferred_element_type=f32)
                           + jnp.dot(qk_c, _block_diag(v_new, hmb), preferred_element_type=f32))
            state[b] = state[b] * g_tot + _tn_dot(k_dec, v_new.astype(bf16)) * hm
    for b in range(n_seq):
        s_ref[b] = state[b]
        o = jnp.concatenate(outs[b], axis=0)
        z = pc_ref[b, :, 768:1024]
        o_ref[b] = _head_norm(o, hmb) * hp_ref[2:3, :] * (z * jax.nn.sigmoid(z))


def gated_deltanet_pallas(pc, A_log, dt_bias, norm_g):
    B, T, width = pc.shape
    L = GDN_CHUNK
    rows = L * min(CHUNKS_PER_STEP, T // L)
    n_seq = SEQS_PER_STEP if B % SEQS_PER_STEP == 0 else 1
    n = GROUP_HEADS * HEAD_DIM
    hp = _per_head_rows(-jnp.exp(A_log), dt_bias, norm_g)
    consts = (hp,) + _chunk_consts(L, rows // L)
    return pl.pallas_call(
        _gdn_kernel,
        grid=(B // n_seq, T // rows),
        in_specs=[pl.BlockSpec((n_seq, rows, width), lambda b, t: (b, t, 0))] + [_const_spec(a) for a in consts],
        out_specs=pl.BlockSpec((n_seq, rows, 256), lambda b, t: (b, t, 0)),
        out_shape=jax.ShapeDtypeStruct((B, T, 256), jnp.float32),
        scratch_shapes=[pltpu.VMEM((n_seq, n, n), jnp.float32)],
        compiler_params=pltpu.CompilerParams(dimension_semantics=("arbitrary", "arbitrary"),
                                             vmem_limit_bytes=VMEM_LIMIT_BYTES),
        name="gated_deltanet",
    )(pc, *consts)


def _mlstm_kernel(pd_ref, hp_ref, hm_ref, tri_ref, ei_ref, ef_ref, lv_ref, o_ref, c_ref, n_ref, m_ref):
    L = MLSTM_CHUNK
    n_seq, rows = pd_ref.shape[0], pd_ref.shape[1]
    f32, bf16 = jnp.float32, jnp.bfloat16
    n_lev = lv_ref.shape[0] - 3

    @pl.when(pl.program_id(1) == 0)
    def _():
        c_ref[...] = jnp.zeros(c_ref.shape, f32)
        n_ref[...] = jnp.zeros(n_ref.shape, f32)
        m_ref[...] = jnp.zeros(m_ref.shape, f32)

    hm = hm_ref[...]
    hmb = hm.astype(bf16)
    eye, incl = lv_ref[n_lev], lv_ref[n_lev + 1]
    head_of_lane = lax.broadcasted_iota(jnp.int32, (L, GROUP_WIDTH), 1) // HEAD_DIM

    def head_max(a):
        out = jnp.zeros(a.shape, f32)
        for h in range(GROUP_HEADS):
            mine = head_of_lane == h
            out = jnp.where(mine, jnp.max(jnp.where(mine, a, NEG), axis=-1, keepdims=True), out)
        return out

    per_seq = rows // L
    parts = []
    for b in range(n_seq):
        tail = pd_ref[b, :, 1024:1152]
        log_i_all = _dot_sel(tail, ei_ref[...])
        log_f = -_softplus(-(_dot_sel(tail, ef_ref[...]) + hp_ref[0:1, :]))
        b_all = _sel_dot(tri_ref[...], log_f)
        for c in range(per_seq):
            sl = slice(c * L, (c + 1) * L)
            q = pd_ref[b, sl, 0:256]
            k = pd_ref[b, sl, 256:512] * HEAD_DIM ** -0.5
            v = pd_ref[b, sl, 512:768]
            bc, log_i = b_all[sl], log_i_all[sl]
            log_w = jnp.where(incl > 0.5, bc - _diag_row(bc - log_i, eye), NEG)
            b_last = bc[L - 1:L, :]
            log_w_end = b_last - bc + log_i
            qk = _nt_dot(q.astype(bf16), _block_diag(k, hmb))
            parts.append((q, k, v, bc, log_w, head_max(log_w), b_last, log_w_end,
                          jnp.max(log_w_end, axis=0, keepdims=True), qk))
    c_state = [c_ref[b] for b in range(n_seq)]
    n_state = [n_ref[b] for b in range(n_seq)]
    m_prev = [m_ref[b] for b in range(n_seq)]
    outs = [[] for _ in range(n_seq)]
    for c in range(per_seq):
        for b in range(n_seq):
            q, k, v, bc, log_w, m_intra, b_last, log_w_end, m_end, qk = parts[b * per_seq + c]
            log_inter = bc + m_prev[b]
            m_t = jnp.maximum(log_inter, m_intra)
            w_inter = jnp.exp(log_inter - m_t)
            s = qk * jnp.exp(log_w - m_t)
            num = w_inter * jnp.dot(q.astype(bf16), c_state[b].astype(bf16), preferred_element_type=f32)
            num = num + jnp.dot(s.astype(bf16), _block_diag(v, hmb), preferred_element_type=f32)
            den = w_inter * _dot_sel(q * n_state[b], hmb, parts=2) + _dot_sel(s, hmb, parts=2)
            outs[b].append(num / jnp.maximum(jnp.abs(den), jnp.exp(-m_t)))
            m_new = jnp.maximum(b_last + m_prev[b], m_end)
            w_old = jnp.exp(b_last + m_prev[b] - m_new)
            kw = k * jnp.exp(log_w_end - m_new)
            c_state[b] = w_old * c_state[b] + _tn_dot(kw.astype(bf16), v.astype(bf16)) * hm
            n_state[b] = w_old * n_state[b] + jnp.sum(kw, axis=0, keepdims=True)
            m_prev[b] = m_new
    for b in range(n_seq):
        c_ref[b] = c_state[b]
        n_ref[b] = n_state[b]
        m_ref[b] = m_prev[b]
        h = jnp.concatenate(outs[b], axis=0)
        o_ref[b] = _head_norm(h, hmb) * hp_ref[1:2, :] * jax.nn.sigmoid(pd_ref[b, :, 768:1024])


def mlstm_pallas(pd, f_bias, norm_g):
    B, T, width = pd.shape
    L = MLSTM_CHUNK
    rows = L * min(CHUNKS_PER_STEP, T // L)
    n_seq = SEQS_PER_STEP if B % SEQS_PER_STEP == 0 else 1
    n = GROUP_HEADS * HEAD_DIM
    consts = (_per_head_rows(f_bias, norm_g),) + _chunk_consts(L, rows // L)
    return pl.pallas_call(
        _mlstm_kernel,
        grid=(B // n_seq, T // rows),
        in_specs=[pl.BlockSpec((n_seq, rows, width), lambda b, t: (b, t, 0))] + [_const_spec(a) for a in consts],
        out_specs=pl.BlockSpec((n_seq, rows, 256), lambda b, t: (b, t, 0)),
        out_shape=jax.ShapeDtypeStruct((B, T, 256), jnp.float32),
        scratch_shapes=[pltpu.VMEM((n_seq, n, n), jnp.float32), pltpu.VMEM((n_seq, 1, 256), jnp.float32),
                        pltpu.VMEM((n_seq, 1, 256), jnp.float32)],
        compiler_params=pltpu.CompilerParams(dimension_semantics=("arbitrary", "arbitrary"),
                                             vmem_limit_bytes=VMEM_LIMIT_BYTES),
        name="mlstm",
    )(pd, *consts)


def kernel(x, c, ada_w, ada_b, norm_g, ffn1_w13, ffn1_w2, ffn2_w13, ffn2_w2, w_in, b_in, q_norm_g, k_norm_g, cmp_pos, cmp_k_w1, cmp_k_w2, cmp_v_w1, cmp_v_w2, t5_table, sc_conv_w, gdn_conv_w, gdn_A_log, gdn_dt_bias, gdn_norm_g, mlstm_f_bias, mlstm_norm_g, mix_norm_g, w_out):
    B, T, D = x.shape
    depth = ada_w.shape[0]
    bf16 = jnp.bfloat16
    x2 = x.reshape(B * T, D)
    tables = nsa_bias_tables(t5_table, T)
    ovt = nsa_overlap_t(T)
    for l in range(depth):
        mod = ada_modulation(c, ada_w[l], ada_b[l][None, :])
        mod = mod.reshape(B, N_SUBLAYERS, 3, 1, D)
        x2 = ffn_half_step(x2, norm_g[l, 0][None, :], mod[:, 0, 0], mod[:, 0, 1], mod[:, 0, 2],
                           ffn1_w13[l].astype(bf16), ffn1_w2[l].astype(bf16), T)
        ws, bs = split_in_weights(w_in[l], b_in[l])
        pa, y_b, pc, pd = in_projection(x2, norm_g[l, 1][None, :], mod[:, 1, 0], mod[:, 1, 1], ws, bs,
                                        sc_conv_w[l], mix_norm_g[l, 1][None, :], gdn_conv_w[l], T)
        pa, pc, pd = (a.reshape(B, T, -1) for a in (pa, pc, pd))
        y_a = nsa_mixer_pallas(pa, q_norm_g[l], k_norm_g[l], cmp_pos[l], cmp_k_w1[l], cmp_k_w2[l],
                               cmp_v_w1[l], cmp_v_w2[l], tables, ovt, mix_norm_g[l, 0][None, :])
        y_c = gated_deltanet_pallas(pc, gdn_A_log[l], gdn_dt_bias[l], gdn_norm_g[l])
        y_d = mlstm_pallas(pd, mlstm_f_bias[l], mlstm_norm_g[l])
        ys = [y.reshape(B * T, GROUP_WIDTH) for y in (y_a, y_b, y_c, y_d)]
        x2 = ffn_half_step(x2, norm_g[l, 2][None, :], mod[:, 2, 0], mod[:, 2, 1], mod[:, 2, 2],
                           ffn2_w13[l].astype(bf16), ffn2_w2[l].astype(bf16), T,
                           mix=(ys, mod[:, 1, 2], w_out[l].astype(bf16)))
    return x2.reshape(B, T, D)
```

```python
import functools
import math

import jax
import jax.numpy as jnp
import numpy as np
from jax import lax
from jax.experimental import pallas as pl
from jax.experimental.pallas import tpu as pltpu

HEAD_DIM = 64
GROUP_HEADS = 4
GROUP_WIDTH = GROUP_HEADS * HEAD_DIM
CMP_STRIDE = 16
CMP_BLOCK = 32
SLC_BLOCK = 64
N_SLC = 16
N_LOCAL_SLC = 2
WINDOW = 512
Q_BLOCK = 128
FORCE = 1e6
N_BUCKETS = 32
MAX_DISTANCE = 128
GDN_CHUNK = 64
MLSTM_CHUNK = 64
N_SUBLAYERS = 3
EPS = 1e-6

IN_LAYOUT = (
    ("a_q", 256), ("a_k_cmp", 64), ("a_v_cmp", 64),
    ("a_k_slc", 64), ("a_v_slc", 64), ("a_k_win", 64), ("a_v_win", 64),
    ("a_gate", 12),
    ("b_b", 256), ("b_c", 256), ("b_x", 256),
    ("c_q", 256), ("c_k", 256), ("c_v", 256),
    ("c_beta", 4), ("c_alpha", 4), ("c_z", 256),
    ("d_q", 256), ("d_k", 256), ("d_v", 256),
    ("d_i", 4), ("d_f", 4), ("d_o", 256),
)

VMEM_LIMIT_BYTES = 56 * 1024 * 1024
FFN_TOKEN_TILE = 512
FFN_CHUNK = 256
PROJ_TOKEN_TILE = 512
NEG = -1e30
HIGHEST = lax.Precision.HIGHEST
LOG2E = math.log2(math.e)
V1_ROWS = HEAD_DIM + 16


def _modulated_norm(x, g, scale, shift):
    y = x * lax.rsqrt(jnp.mean(x * x, axis=-1, keepdims=True) + EPS)
    return (y * g) * (1.0 + scale) + shift


def _ada_kernel(c_ref, w_ref, b_ref, o_ref):
    c = c_ref[...]
    cond = c * jax.nn.sigmoid(c)
    o_ref[...] = jnp.dot(cond.astype(jnp.bfloat16), w_ref[...].astype(jnp.bfloat16),
                         preferred_element_type=jnp.float32) + b_ref[...]


def ada_modulation(c, w, b):
    B, D = c.shape
    N = w.shape[1]
    tn = 1152
    return pl.pallas_call(
        _ada_kernel,
        grid=(N // tn,),
        in_specs=[pl.BlockSpec((B, D), lambda j: (0, 0)),
                  pl.BlockSpec((D, tn), lambda j: (0, j)),
                  pl.BlockSpec((1, tn), lambda j: (0, j))],
        out_specs=pl.BlockSpec((B, tn), lambda j: (0, j)),
        out_shape=jax.ShapeDtypeStruct((B, N), jnp.float32),
        name="ada_modulation",
    )(c, w, b)


def _ffn_kernel(x_ref, g_ref, shift_ref, scale_ref, gate_ref, w1_ref, w3_ref, w2_ref, *rest, n_mix):
    o_ref, act_ref = rest[-2:]
    x = x_ref[...]
    if n_mix:
        y_refs, mix_gate_ref, wo_ref = rest[:n_mix], rest[n_mix], rest[n_mix + 1]
        z = None
        for k, y_ref in enumerate(y_refs):
            part = jnp.dot(y_ref[...].astype(jnp.bfloat16), wo_ref[k * GROUP_WIDTH:(k + 1) * GROUP_WIDTH, :],
                           preferred_element_type=jnp.float32)
            z = part if z is None else z + part
        x = x + mix_gate_ref[0] * z
    h = _modulated_norm(x, g_ref[...], scale_ref[0], shift_ref[0]).astype(jnp.bfloat16)
    n_chunks = act_ref.shape[1] // FFN_CHUNK
    for ci in range(n_chunks):
        cs = slice(ci * FFN_CHUNK, (ci + 1) * FFN_CHUNK)
        a = jnp.dot(h, w1_ref[:, cs], preferred_element_type=jnp.float32)
        b = jnp.dot(h, w3_ref[:, cs], preferred_element_type=jnp.float32)
        act_ref[:, cs] = (a * jax.nn.sigmoid(a) * b).astype(jnp.bfloat16)
    y = jnp.dot(act_ref[...], w2_ref[...], preferred_element_type=jnp.float32)
    o_ref[...] = x + (0.5 * gate_ref[0]) * y


def ffn_half_step(x2, g, shift, scale, gate, w13, w2, tokens_per_batch, mix=None):
    M, D = x2.shape
    F = w2.shape[0]
    tm = FFN_TOKEN_TILE
    tiles_per_batch = tokens_per_batch // tm
    resident = dict(pipeline_mode=pl.Buffered(1))
    mod_spec = pl.BlockSpec((1, 1, D), lambda i: (i // tiles_per_batch, 0, 0))
    ys, mix_gate, w_out = mix if mix is not None else ((), None, None)
    mix_args = tuple(ys) + ((mix_gate, w_out) if ys else ())
    mix_specs = [pl.BlockSpec((tm, GROUP_WIDTH), lambda i: (i, 0)) for _ in ys]
    if ys:
        mix_specs += [mod_spec, pl.BlockSpec(w_out.shape, lambda i: (0, 0), **resident)]
    return pl.pallas_call(
        functools.partial(_ffn_kernel, n_mix=len(ys)),
        grid=(M // tm,),
        in_specs=[pl.BlockSpec((tm, D), lambda i: (i, 0)),
                  pl.BlockSpec((1, D), lambda i: (0, 0)),
                  mod_spec, mod_spec, mod_spec,
                  pl.BlockSpec((D, F), lambda i: (0, 0), **resident),
                  pl.BlockSpec((D, F), lambda i: (0, 1), **resident),
                  pl.BlockSpec((F, D), lambda i: (0, 0), **resident)] + mix_specs,
        out_specs=pl.BlockSpec((tm, D), lambda i: (i, 0)),
        out_shape=jax.ShapeDtypeStruct((M, D), jnp.float32),
        scratch_shapes=[pltpu.VMEM((tm, F), jnp.bfloat16)],
        compiler_params=pltpu.CompilerParams(dimension_semantics=("arbitrary",),
                                             vmem_limit_bytes=VMEM_LIMIT_BYTES),
        name="ffn_half_step",
    )(x2, g, shift, scale, gate, w13, w13, w2, *mix_args)


CONV_CARRY_ROWS = 8
CONV_ROW_CHUNK = 128


def _in_proj_kernel(x_ref, g_ref, shift_ref, scale_ref, wa_ref, wb_ref, wc_ref, wd_ref,
                    ba_ref, bb_ref, bc_ref, bd_ref, scw_ref, scg_ref, gcw_ref,
                    oa_ref, yb_ref, oc_ref, od_ref, braw_ref, craw_ref, *, tiles_per_batch):
    f32 = jnp.float32
    rows = x_ref.shape[0]
    hist = CONV_CARRY_ROWS
    first = pl.program_id(0) % tiles_per_batch == 0

    @pl.when(first)
    def _():
        braw_ref[0:hist, :] = jnp.zeros((hist, braw_ref.shape[1]), f32)
        craw_ref[0:hist, :] = jnp.zeros((hist, craw_ref.shape[1]), f32)

    @pl.when(jnp.logical_not(first))
    def _():
        braw_ref[0:hist, :] = braw_ref[rows:rows + hist, :]
        craw_ref[0:hist, :] = craw_ref[rows:rows + hist, :]

    h = _modulated_norm(x_ref[...], g_ref[...], scale_ref[0], shift_ref[0]).astype(jnp.bfloat16)
    project = lambda w_ref, b_ref: jnp.dot(h, w_ref[...], preferred_element_type=f32) + b_ref[...]
    pb = project(wb_ref, bb_ref)
    yb_ref[...] = pb[:, 0:256]
    braw_ref[hist:hist + rows, :] = pb[:, 256:512] * pb[:, 512:768]
    pc = project(wc_ref, bc_ref)
    craw_ref[hist:hist + rows, :] = pc[:, 0:768]
    oc_ref[:, 768:] = pc[:, 768:]
    n_chunks = rows // CONV_ROW_CHUNK
    pieces = []
    for w_ref, b_ref, o_ref in ((wa_ref, ba_ref, oa_ref), (wd_ref, bd_ref, od_ref)):
        half = w_ref.shape[1] // 256 // 2 * 256
        pieces += [(w_ref, b_ref, o_ref, 0, half), (w_ref, b_ref, o_ref, half, w_ref.shape[1])]
    for ci, r0 in enumerate(range(0, rows, CONV_ROW_CHUNK)):
        for w_ref, b_ref, o_ref, c0, c1 in pieces[ci * len(pieces) // n_chunks:(ci + 1) * len(pieces) // n_chunks]:
            o_ref[:, c0:c1] = jnp.dot(h, w_ref[:, c0:c1], preferred_element_type=f32) + b_ref[:, c0:c1]
        out_rows = slice(r0, r0 + CONV_ROW_CHUNK)
        tap = lambda ref, s: ref[pl.ds(hist + r0 - s, CONV_ROW_CHUNK), :]
        conv = sum(tap(braw_ref, s) * scw_ref[2 - s:3 - s, :] for s in range(3))
        y = yb_ref[out_rows, :] * conv
        yb_ref[out_rows, :] = y * lax.rsqrt(jnp.mean(y * y, axis=-1, keepdims=True) + EPS) * scg_ref[...]
        for c0 in range(0, 3 * GROUP_WIDTH, GROUP_WIDTH):
            cols = slice(c0, c0 + GROUP_WIDTH)
            conv = sum(craw_ref[pl.ds(hist + r0 - s, CONV_ROW_CHUNK), cols] * gcw_ref[3 - s:4 - s, cols]
                       for s in range(4))
            oc_ref[out_rows, cols] = conv * jax.nn.sigmoid(conv)


def in_projection(x2, g, shift, scale, ws, bs, sc_conv_w, sc_norm_g, gdn_conv_w, tokens_per_batch):
    M, D = x2.shape
    tm = PROJ_TOKEN_TILE
    tiles_per_batch = tokens_per_batch // tm
    mod_spec = pl.BlockSpec((1, 1, D), lambda i: (i // tiles_per_batch, 0, 0))
    const = lambda a, **kw: pl.BlockSpec(a.shape, lambda i: (0, 0), **kw)
    pad8 = lambda w: jnp.pad(w, ((0, 8 - w.shape[0]), (0, 0)))
    extras = (pad8(sc_conv_w), sc_norm_g, pad8(gdn_conv_w))
    widths = (ws[0].shape[1], GROUP_WIDTH, ws[2].shape[1], ws[3].shape[1])
    return pl.pallas_call(
        functools.partial(_in_proj_kernel, tiles_per_batch=tiles_per_batch),
        grid=(M // tm,),
        in_specs=[pl.BlockSpec((tm, D), lambda i: (i, 0)), const(g), mod_spec, mod_spec]
                 + [const(w, pipeline_mode=pl.Buffered(1)) for w in ws] + [const(b) for b in bs]
                 + [const(e) for e in extras],
        out_specs=[pl.BlockSpec((tm, n), lambda i: (i, 0)) for n in widths],
        out_shape=[jax.ShapeDtypeStruct((M, n), jnp.float32) for n in widths],
        scratch_shapes=[pltpu.VMEM((CONV_CARRY_ROWS + tm, GROUP_WIDTH), jnp.float32),
                        pltpu.VMEM((CONV_CARRY_ROWS + tm, 3 * GROUP_WIDTH), jnp.float32)],
        compiler_params=pltpu.CompilerParams(dimension_semantics=("arbitrary",),
                                             vmem_limit_bytes=VMEM_LIMIT_BYTES),
        name="in_projection",
    )(x2, g, shift, scale, *ws, *bs, *extras)


def _group_columns():
    offs, o = {}, 0
    for n, s in IN_LAYOUT:
        offs[n] = np.arange(o, o + s)
        o += s
    cat = lambda names: np.concatenate([offs[n] for n in names])
    return (cat(["a_q", "a_k_cmp", "a_v_cmp", "a_k_slc", "a_v_slc", "a_k_win", "a_v_win"] + list(SMALL_GATES)),
            cat(["b_b", "b_c", "b_x"]),
            cat(["c_q", "c_k", "c_v", "c_z"]),
            cat(["d_q", "d_k", "d_v", "d_o"]))


SMALL_GATES = ("a_gate", "c_beta", "c_alpha", "d_i", "d_f")
GATE_BLOCK = 5
GATE_LANE = {"a_gate": 0, "c_beta": 12, "c_alpha": 16, "d_i": 20, "d_f": 24}
GROUP_SLAB_WIDTH = (768, 768, 1024, 1024)


def _take_columns(a, idx, width):
    cuts = [0] + [k + 1 for k in range(idx.size - 1) if idx[k + 1] != idx[k] + 1] + [idx.size]
    runs = [a[..., int(idx[s]):int(idx[e - 1]) + 1] for s, e in zip(cuts[:-1], cuts[1:])]
    return jnp.concatenate(runs + [jnp.zeros(a.shape[:-1] + (width - idx.size,), a.dtype)], axis=-1)


def split_in_weights(w, b):
    ws, bs = [], []
    for idx, width in zip(_group_columns(), GROUP_SLAB_WIDTH):
        ws.append(_take_columns(w, idx, width).astype(jnp.bfloat16))
        bs.append(_take_columns(b, idx, width)[None, :])
    return ws, bs


def t5_bucket(dist):
    n = jnp.maximum(dist, 0)
    max_exact = N_BUCKETS // 2
    nf = jnp.maximum(n, 1).astype(jnp.float32)
    large = max_exact + (jnp.log(nf / max_exact) / math.log(MAX_DISTANCE / max_exact)
                         * (N_BUCKETS - max_exact)).astype(jnp.int32)
    large = jnp.minimum(large, N_BUCKETS - 1)
    return jnp.where(n < max_exact, n, large)


def _nt_dot(a, b):
    return lax.dot_general(a, b, (((1,), (1,)), ((), ())), preferred_element_type=jnp.float32)


def _nsa_prep_kernel(q_ref, kvs_ref, kvw_ref, qg_ref, kg_ref, seg_ref,
                     qnt_ref, kxs_ref, v1s_ref, kkw_ref, v1w_ref):
    tm = q_ref.shape[0]
    q = q_ref[...]
    ss = _dot_sel(q * q, seg_ref[...], parts=3)
    qn = q * lax.rsqrt(ss * (1.0 / HEAD_DIM) + EPS) * qg_ref[...]
    qnt_ref[...] = (qn * LOG2E).T.astype(qnt_ref.dtype)
    lane = lax.broadcasted_iota(jnp.int32, (tm, 128), 1)
    lo = lane < HEAD_DIM

    def split(x):
        xr = pltpu.roll(x, HEAD_DIM, 1)
        ss = jnp.sum(jnp.where(lo, x * x, 0.0), axis=-1, keepdims=True)
        kk = jnp.where(lo, x, xr) * lax.rsqrt(ss * (1.0 / HEAD_DIM) + EPS) * kg_ref[...]
        return kk, jnp.where(lo, xr, 1.0)

    kk, v1 = split(kvs_ref[...])
    tok = pl.program_id(1) * tm + lax.broadcasted_iota(jnp.int32, (tm, 128), 0)
    onehot = jnp.where(lane == tok // SLC_BLOCK, 1.0, 0.0)
    kxs_ref[...] = jnp.concatenate([kk, onehot], axis=1).astype(kxs_ref.dtype)
    v1s_ref[...] = v1.T[0:V1_ROWS].astype(v1s_ref.dtype)
    kk, v1 = split(kvw_ref[...])
    kkw_ref[...] = kk.astype(kkw_ref.dtype)
    v1w_ref[...] = v1.T[0:V1_ROWS].astype(v1w_ref.dtype)


def nsa_prep(pa, qg, kg2, seg):
    B, T, _ = pa.shape
    tm = 512
    bf16 = jnp.bfloat16
    col = lambda w, j: pl.BlockSpec((None, tm, w), lambda b, i: (b, i, j))
    rowblk = lambda h: pl.BlockSpec((None, h, tm), lambda b, i: (b, 0, i))
    const = lambda shape: pl.BlockSpec(shape, lambda b, i: (0, 0))
    return pl.pallas_call(
        _nsa_prep_kernel,
        grid=(B, T // tm),
        in_specs=[col(256, 0), col(128, 3), col(128, 4), const((1, 256)), const((1, 128)), const((256, 256))],
        out_specs=[rowblk(256), col(256, 0), rowblk(V1_ROWS), col(128, 0), rowblk(V1_ROWS)],
        out_shape=[jax.ShapeDtypeStruct((B, 256, T), bf16),
                   jax.ShapeDtypeStruct((B, T, 256), bf16), jax.ShapeDtypeStruct((B, V1_ROWS, T), bf16),
                   jax.ShapeDtypeStruct((B, T, 128), bf16), jax.ShapeDtypeStruct((B, V1_ROWS, T), bf16)],
        name="nsa_prep",
    )(pa, pa, pa, qg, kg2, seg)


def _compress_kernel(kv_ref, pos_ref, w1_ref, w2k_ref, w2v_ref, kg_ref, kkc_ref, vct_ref):
    nrow = kkc_ref.shape[0]
    f32, bf16 = jnp.float32, jnp.bfloat16
    hidden = w1_ref.shape[2] // 2
    first = jnp.zeros((nrow, 2 * hidden), f32)
    second = jnp.zeros((nrow, 2 * hidden), f32)
    for p in range(CMP_STRIDE):
        xp = kv_ref[pl.ds(p, nrow, stride=CMP_STRIDE), :]
        first = first + jnp.dot((xp + pos_ref[p:p + 1, :]).astype(bf16), w1_ref[p], preferred_element_type=f32)
        second = second + jnp.dot((xp + pos_ref[CMP_STRIDE + p:CMP_STRIDE + p + 1, :]).astype(bf16),
                                  w1_ref[CMP_STRIDE + p], preferred_element_type=f32)
    hid = first + pltpu.roll(second, nrow - 1, 0)
    hid = (hid * jax.nn.sigmoid(hid)).astype(bf16)
    valid = lax.broadcasted_iota(jnp.int32, (nrow, 128), 0) < nrow - 1
    kc = jnp.dot(hid[:, 0:hidden], w2k_ref[...], preferred_element_type=f32)
    kc = kc * lax.rsqrt(jnp.mean(kc * kc, axis=-1, keepdims=True) + EPS) * kg_ref[...]
    kkc_ref[...] = jnp.where(valid, kc, 0.0).astype(kkc_ref.dtype)
    vc = jnp.dot(hid[:, hidden:2 * hidden], w2v_ref[...], preferred_element_type=f32)
    vct_ref[...] = jnp.where(valid, vc, 0.0).T[0:HEAD_DIM].astype(vct_ref.dtype)


def nsa_compress(pa, pos2, w1, w2k, w2v, kg2):
    B, T, _ = pa.shape
    nrow = T // CMP_STRIDE
    bf16 = jnp.bfloat16
    full = lambda a: pl.BlockSpec(a.shape, lambda b: (0,) * a.ndim)
    return pl.pallas_call(
        _compress_kernel,
        grid=(B,),
        in_specs=[pl.BlockSpec((None, T, 128), lambda b: (b, 0, 2)), full(pos2), full(w1), full(w2k), full(w2v),
                  full(kg2)],
        out_specs=[pl.BlockSpec((None, nrow, 128), lambda b: (b, 0, 0)),
                   pl.BlockSpec((None, HEAD_DIM, nrow), lambda b: (b, 0, 0))],
        out_shape=[jax.ShapeDtypeStruct((B, nrow, 128), bf16), jax.ShapeDtypeStruct((B, HEAD_DIM, nrow), bf16)],
        name="nsa_compress",
    )(pa, pos2, w1, w2k, w2v, kg2)


CMP_TILES_PER_STEP = 4


def _stack_heads_t(qt):
    lo = lax.broadcasted_iota(jnp.int32, (128, qt.shape[1]), 0) < HEAD_DIM
    zero = jnp.zeros((), qt.dtype)
    return jnp.concatenate([jnp.where(lo if h % 2 == 0 else ~lo, qt[128 * (h // 2):128 * (h // 2) + 128], zero)
                            for h in range(GROUP_HEADS)], axis=1)


def _cmp_select_kernel(qnt_ref, kkc_ref, vct_ref, cb_ref, ov_ref, ocmp_ref, pen_ref, sc_ref, *, n_sel):
    i = pl.program_id(1)
    n_tiles = pen_ref.shape[0]
    tq = qnt_ref.shape[1] // n_tiles
    nblk, nrow = ov_ref.shape
    keys_per_tile = tq // CMP_STRIDE
    f32, bf16 = jnp.float32, jnp.bfloat16
    blk = lax.broadcasted_iota(jnp.int32, (nblk, tq), 0)
    scores = []
    for g in range(n_tiles):
        cols = slice(g * tq, (g + 1) * tq)
        q4t = _stack_heads_t(qnt_ref[:, cols])
        first_row = pl.multiple_of(nrow - keys_per_tile * (i * n_tiles + g), keys_per_tile)
        s = jnp.dot(kkc_ref[...], q4t, preferred_element_type=f32) + cb_ref[pl.ds(first_row, nrow), :]
        m = jnp.max(s, axis=0, keepdims=True)
        e = jnp.where(s > 0.5 * NEG, jnp.exp2(s - m), 0.0)
        p = e * (1.0 / jnp.maximum(jnp.sum(e, axis=0, keepdims=True), 1e-30))
        o = jnp.dot(vct_ref[...], p.astype(bf16), preferred_element_type=f32)
        for h in range(GROUP_HEADS):
            ocmp_ref[h * HEAD_DIM:(h + 1) * HEAD_DIM, cols] = o[:, h * tq:(h + 1) * tq]
        psum = p[:, 0:tq] + p[:, tq:2 * tq] + p[:, 2 * tq:3 * tq] + p[:, 3 * tq:4 * tq]
        hi = psum.astype(bf16)
        lo_part = (psum - hi.astype(f32)).astype(bf16)
        score = (jnp.dot(ov_ref[...], hi, preferred_element_type=f32)
                 + jnp.dot(ov_ref[...], lo_part, preferred_element_type=f32))
        qpos = (i * n_tiles + g) * tq + lax.broadcasted_iota(jnp.int32, (nblk, tq), 1)
        cur = qpos // SLC_BLOCK
        forced = (blk == 0) | ((cur - blk >= 0) & (cur - blk < N_LOCAL_SLC))
        score = jnp.where(forced, FORCE, score)
        score = jnp.where(blk <= cur, score, -FORCE)
        sc_ref[g] = score
        scores.append(score)
    blocks_per_tile = tq // SLC_BLOCK

    def rank_step(t, cnts):
        out = []
        for g in range(n_tiles):
            cnt = cnts[g]
            for u in range(blocks_per_tile):
                jp = t * blocks_per_tile + u
                row = sc_ref[g, pl.ds(jp, 1), :]
                tie = jnp.where(blk > jp, 1.0, 0.0)
                cnt = cnt + jnp.where(row > scores[g], 1.0, jnp.where(row == scores[g], tie, 0.0))
            out.append(cnt)
        return tuple(out)

    zeros = tuple(jnp.zeros((nblk, tq), f32) for _ in range(n_tiles))
    cnts = lax.fori_loop(0, (i + 1) * n_tiles, rank_step, zeros)
    for g in range(n_tiles):
        pen = jnp.where(cnts[g] < n_sel, 0.0, NEG)
        if nblk < 128:
            pen = jnp.concatenate([pen, jnp.zeros((128 - nblk, tq), f32)], axis=0)
        pen_ref[g] = pen.astype(pen_ref.dtype)


def nsa_cmp_select(qnt, kkc, vct, cbias, ov, n_sel):
    B, _, T = qnt.shape
    nrow = kkc.shape[1]
    nblk = ov.shape[0]
    tq = Q_BLOCK
    g = min(CMP_TILES_PER_STEP, T // tq)
    return pl.pallas_call(
        functools.partial(_cmp_select_kernel, n_sel=n_sel),
        grid=(B, T // (g * tq)),
        in_specs=[pl.BlockSpec((None, 256, g * tq), lambda b, i: (b, 0, i)),
                  pl.BlockSpec((None, nrow, 128), lambda b, i: (b, 0, 0)),
                  pl.BlockSpec((None, HEAD_DIM, nrow), lambda b, i: (b, 0, 0)),
                  pl.BlockSpec((2 * nrow, 4 * tq), lambda b, i: (0, 0)),
                  pl.BlockSpec((nblk, nrow), lambda b, i: (0, 0))],
        out_specs=[pl.BlockSpec((None, 256, g * tq), lambda b, i: (b, 0, i)),
                   pl.BlockSpec((None, g, 128, tq), lambda b, i: (b, i, 0, 0))],
        out_shape=[jax.ShapeDtypeStruct((B, 256, T), jnp.float32),
                   jax.ShapeDtypeStruct((B, T // tq, 128, tq), jnp.bfloat16)],
        scratch_shapes=[pltpu.VMEM((g, nblk, tq), jnp.float32)],
        name="nsa_cmp_select",
    )(qnt, kkc, vct, cbias, ov)


NSA_KEY_TILE = 256
NSA_QUERY_TILE = 256


def _nsa_attn_kernel(qnt_ref, pen_ref, ocmp_ref, gate_ref, kxs_ref, v1s_ref, kkw_ref, v1w_ref,
                     tabs_ref, tabw_ref, mixg_ref, o_ref, acc_ref, m_ref, sa_ref, sb_ref):
    i = pl.program_id(1)
    tq = o_ref.shape[0]
    kt = tabs_ref.shape[1]
    f32, bf16 = jnp.float32, jnp.bfloat16
    q4t = _stack_heads_t(qnt_ref[...])
    pen = jnp.concatenate([pen_ref[t] for t in range(pen_ref.shape[0])], axis=1)
    rhs_sel = jnp.concatenate([q4t, jnp.concatenate([pen] * GROUP_HEADS, axis=1)], axis=0)

    def branch(rhs, k_ref, v_ref, tab_ref, first_tile, last_tile):
        n_tab = tab_ref.shape[0] - 1
        m_ref[...] = jnp.full(m_ref.shape, NEG, f32)
        acc_ref[...] = jnp.zeros(acc_ref.shape, f32)

        def key_offset(j):
            return pl.multiple_of(jnp.minimum(j, last_tile) * kt, kt)

        def scores(j):
            delta = jnp.where(j > last_tile, n_tab, jnp.minimum(i - j * (kt // tq), n_tab - 1))
            return jnp.dot(k_ref[pl.ds(key_offset(j), kt), :], rhs, preferred_element_type=f32) + tab_ref[delta]

        def absorb(s, j):
            m_old = m_ref[...]
            m_new = jnp.maximum(m_old, jnp.max(s, axis=0, keepdims=True))
            p = jnp.exp2(s - m_new)
            pv = jnp.dot(v_ref[:, pl.ds(key_offset(j), kt)], p.astype(bf16), preferred_element_type=f32)
            acc_ref[...] = jnp.exp2(m_old - m_new) * acc_ref[...] + pv
            m_ref[...] = m_new

        sa_ref[...] = scores(first_tile)

        def body(t, carry):
            j = first_tile + 2 * t
            sb_ref[...] = scores(j + 1)
            absorb(sa_ref[...], j)
            sa_ref[...] = scores(j + 2)
            absorb(sb_ref[...], j + 1)
            return carry

        lax.fori_loop(0, (last_tile - first_tile) // 2 + 1, body, 0)
        acc = acc_ref[...]
        return acc[0:HEAD_DIM] / jnp.maximum(acc[HEAD_DIM:HEAD_DIM + 1], 1e-30)

    def window_branch(rhs):
        n_tab = tabw_ref.shape[0] - 1
        m = jnp.full((1, rhs.shape[1]), NEG, f32)
        acc = jnp.zeros((V1_ROWS, rhs.shape[1]), f32)
        for d in range(n_tab):
            j = i - d
            off = pl.multiple_of(jnp.maximum(j, 0) * kt, kt)
            s = jnp.dot(kkw_ref[pl.ds(off, kt), :], rhs, preferred_element_type=f32)
            s = s + tabw_ref[jnp.where(j < 0, n_tab, d)]
            m_new = jnp.maximum(m, jnp.max(s, axis=0, keepdims=True))
            p = jnp.exp2(s - m_new)
            acc = jnp.exp2(m - m_new) * acc +jnp.dot(v1w_ref[:, pl.ds(off, kt)], p.astype(bf16),
                                                     preferred_element_type=f32)
            m = m_new
        return acc[0:HEAD_DIM] / jnp.maximum(acc[HEAD_DIM:HEAD_DIM + 1], 1e-30)

    last = (i * tq) // kt
    if kt == tq:
        o_w = window_branch(q4t)
    else:
        o_w = branch(q4t, kkw_ref, v1w_ref, tabw_ref, (jnp.maximum(i - WINDOW // tq, 0) * tq) // kt, last)
    o_s = branch(rhs_sel, kxs_ref, v1s_ref, tabs_ref, 0, last)
    gt = jax.nn.sigmoid(gate_ref[...]).T
    oct_ = ocmp_ref[...]
    parts = []
    for h in range(GROUP_HEADS):
        cols = slice(h * tq, (h + 1) * tq)
        parts.append(gt[3 * h:3 * h + 1] * oct_[h * HEAD_DIM:(h + 1) * HEAD_DIM]
                     + gt[3 * h + 1:3 * h + 2] * o_s[:, cols] + gt[3 * h + 2:3 * h + 3] * o_w[:, cols])
    yt = jnp.concatenate(parts, axis=0)
    yt = yt * lax.rsqrt(jnp.mean(yt * yt, axis=0, keepdims=True) + EPS) * mixg_ref[...]
    o_ref[...] = yt.T


def nsa_attention(qnt, pen, ocmp, pa, kxs, v1s, kkw, v1w, tab_s, tab_w, mixg):
    B, _, T = qnt.shape
    tq = min(NSA_QUERY_TILE, T)
    tile = lambda w, j: pl.BlockSpec((None, tq, w), lambda b, i: (b, i, j))
    whole = lambda w: pl.BlockSpec((None, T, w), lambda b, i: (b, 0, 0))
    whole_t = pl.BlockSpec((None, V1_ROWS, T), lambda b, i: (b, 0, 0))
    full = lambda a: pl.BlockSpec(a.shape, lambda b, i: (0,) * a.ndim)
    return pl.pallas_call(
        _nsa_attn_kernel,
        grid=(B, T // tq),
        in_specs=[pl.BlockSpec((None, 256, tq), lambda b, i: (b, 0, i)),
                  pl.BlockSpec((None, tq // Q_BLOCK, 128, Q_BLOCK), lambda b, i: (b, i, 0, 0)),
                  pl.BlockSpec((None, 256, tq), lambda b, i: (b, 0, i)), tile(128, 5),
                  whole(256), whole_t, whole(128), whole_t, full(tab_s), full(tab_w), full(mixg)],
        out_specs=tile(256, 0),
        out_shape=jax.ShapeDtypeStruct((B, T, 256), jnp.float32),
        scratch_shapes=[pltpu.VMEM((V1_ROWS, GROUP_HEADS * tq), jnp.float32),
                        pltpu.VMEM((1, GROUP_HEADS * tq), jnp.float32),
                        pltpu.VMEM((tab_s.shape[1], GROUP_HEADS * tq), jnp.float32),
                        pltpu.VMEM((tab_s.shape[1], GROUP_HEADS * tq), jnp.float32)],
        compiler_params=pltpu.CompilerParams(dimension_semantics=("arbitrary", "arbitrary"),
                                             vmem_limit_bytes=VMEM_LIMIT_BYTES),
        name="nsa_attention",
    )(qnt, pen, ocmp, pa, kxs, v1s, kkw, v1w, tab_s, tab_w, mixg)


def _toeplitz(wr, rows, cols):
    H, n = wr.shape
    flat = jnp.tile(jnp.pad(wr, ((0, 0), (0, 1))), (1, rows))[:, :rows * n]
    return flat.reshape(H, rows, n)[:, :, rows - 1:rows - 1 + cols]


def nsa_bias_tables(t5_table, T):
    H = GROUP_HEADS
    tq = Q_BLOCK
    ta = min(NSA_QUERY_TILE, T)
    kt = min(NSA_KEY_TILE, T)
    nrow = T // CMP_STRIDE
    dmax = WINDOW + 2 * ta + kt
    onehot = (t5_bucket(jnp.arange(dmax))[:, None] == jnp.arange(N_BUCKETS)[None, :]).astype(jnp.float32)
    by_dist = jnp.dot(onehot, t5_table, precision=HIGHEST).T
    neg = lambda n: jnp.full((H, n), NEG, jnp.float32)
    f_s = jnp.concatenate([neg(kt - 1), by_dist * LOG2E], axis=1)
    f_w = jnp.concatenate([neg(kt - 1), by_dist[:, :WINDOW] * LOG2E, neg(dmax - WINDOW)], axis=1)
    span = ta + kt - 1

    def tiles(f, n):
        out = [_toeplitz(f[:, k * ta:k * ta + span], kt, ta).transpose(1, 0, 2).reshape(kt, H * ta) for k in range(n)]
        return jnp.stack(out + [jnp.full((kt, H * ta), NEG, jnp.float32)])

    n_s = -(-(kt + MAX_DISTANCE) // ta) + 1
    n_w = WINDOW // ta + kt // ta
    tab_s, tab_w = tiles(f_s, n_s), tiles(f_w, n_w)
    per_tile = tq // CMP_STRIDE
    d0 = -CMP_STRIDE * (nrow - 1) - (CMP_BLOCK - 1)
    n16 = 2 * nrow + per_tile - 1
    n_pos = CMP_STRIDE * n16 + d0
    far = jnp.broadcast_to(by_dist[:, -1:], (H, max(n_pos - dmax, 0)))
    w16 = jnp.concatenate([neg(-d0), by_dist[:, :n_pos] * LOG2E, far * LOG2E], axis=1).reshape(H, n16, CMP_STRIDE)
    w16 = w16[:, ::-1, :]
    base = jnp.stack([w16[:, per_tile - 1 - a:per_tile - 1 - a + 2 * nrow, :] for a in range(per_tile)], axis=1)
    cbias = base.transpose(0, 1, 3, 2).reshape(H * tq, 2 * nrow).T
    return tab_s, tab_w, cbias


def nsa_overlap_t(T):
    nrow = T // CMP_STRIDE
    nblk = T // SLC_BLOCK
    ci = np.arange(nrow)[None, :]
    bj = np.arange(nblk)[:, None]
    ov = (ci * CMP_STRIDE < (bj + 1) * SLC_BLOCK) & (ci * CMP_STRIDE + CMP_BLOCK > bj * SLC_BLOCK)
    ov = ov & (ci < nrow - 1)
    return jnp.asarray(ov.astype(np.float32), dtype=jnp.bfloat16)


def nsa_mixer_pallas(pa, q_norm_g, k_norm_g, cmp_pos, cmp_k_w1, cmp_k_w2, cmp_v_w1, cmp_v_w2,
                     tables, ovt, mixg):
    B, T, _ = pa.shape
    bf16 = jnp.bfloat16
    tab_s, tab_w, cbias = tables
    qg = (jnp.tile(q_norm_g, GROUP_HEADS) * HEAD_DIM ** -0.5)[None, :]
    kg2 = jnp.tile(k_norm_g, 2)[None, :]
    qnt, kxs, v1s, kkw, v1w = nsa_prep(pa, qg, kg2, _head_mask().astype(bf16))
    dup = lambda w: jnp.concatenate([w, w], axis=1)
    zeros = jnp.zeros((CMP_BLOCK, HEAD_DIM, cmp_k_w1.shape[1]), jnp.float32)
    per_pos = lambda w: w.reshape(CMP_BLOCK, HEAD_DIM, -1)
    w1 = jnp.concatenate([jnp.concatenate([per_pos(cmp_k_w1), zeros], axis=2),
                          jnp.concatenate([zeros, per_pos(cmp_v_w1)], axis=2)], axis=1).astype(bf16)
    kkc, vct = nsa_compress(pa, dup(cmp_pos), w1, dup(cmp_k_w2).astype(bf16), dup(cmp_v_w2).astype(bf16), kg2)
    n_sel = min(N_SLC, T // SLC_BLOCK)
    ocmp, pen = nsa_cmp_select(qnt, kkc, vct, cbias, ovt, n_sel)
    mixg_col = jnp.broadcast_to(mixg.reshape(-1, 1), (GROUP_WIDTH, min(NSA_QUERY_TILE, T)))
    return nsa_attention(qnt, pen, ocmp, pa, kxs, v1s, kkw, v1w, tab_s, tab_w, mixg_col)


CHUNKS_PER_STEP = 4
SEQS_PER_STEP = 2


def _tn_dot(a, b):
    return lax.dot_general(a, b, (((0,), (0,)), ((), ())), preferred_element_type=jnp.float32)


def _split_bf16(a, parts):
    out, rest = [], a
    for _ in range(parts):
        piece = rest.astype(jnp.bfloat16)
        out.append(piece)
        rest = rest - piece.astype(jnp.float32)
    return out


def _dot_sel(a, sel, parts=3):
    return sum(jnp.dot(p, sel, preferred_element_type=jnp.float32) for p in _split_bf16(a, parts))


def _sel_dot(sel, a, parts=3):
    return sum(jnp.dot(sel, p, preferred_element_type=jnp.float32) for p in _split_bf16(a, parts))


def _softplus(x):
    return jnp.maximum(x, 0.0) + jnp.log(1.0 + jnp.exp(-jnp.abs(x)))


def _block_diag(a, hmb):
    return jnp.concatenate([a.astype(jnp.bfloat16)] * GROUP_HEADS, axis=0) * hmb


def _diag_row(a, eye):
    return jnp.sum(a * eye, axis=0, keepdims=True)


def _head_norm(o, hmb):
    return o * lax.rsqrt(_dot_sel(o * o, hmb, parts=2) * (1.0 / HEAD_DIM) + EPS)


def _head_l2(o, hmb):
    return o * lax.rsqrt(_dot_sel(o * o, hmb, parts=2) + EPS)


def _xform_masks(L):
    i = np.arange(L)[:, None]
    j = np.arange(L)[None, :]
    masks = []
    s = 1
    while s < L:
        masks.append((i // (2 * s) == j // (2 * s)) & (i % (2 * s) >= s) & (j % (2 * s) < s))
        s *= 2
    masks += [i == j, j <= i, j < i]
    return jnp.asarray(np.tile(np.stack(masks).astype(np.float32), (1, 1, GROUP_HEADS)))


def _expand_matrix(first_lane):
    e = np.zeros((128, GROUP_WIDTH), np.float32)
    for h in range(GROUP_HEADS):
        e[first_lane + h, h * HEAD_DIM:(h + 1) * HEAD_DIM] = 1.0
    return jnp.asarray(e, jnp.bfloat16)


def _head_mask():
    return jnp.asarray(np.kron(np.eye(GROUP_HEADS), np.ones((HEAD_DIM, HEAD_DIM))), jnp.float32)


def _per_head_rows(*vecs):
    rows = [jnp.repeat(v, HEAD_DIM) if v.shape[0] == GROUP_HEADS else jnp.tile(v, GROUP_HEADS) for v in vecs]
    return jnp.pad(jnp.stack(rows), ((0, 8 - len(rows)), (0, 0)))


def _chunk_consts(L, n_chunks):
    tri = np.kron(np.eye(n_chunks), np.tril(np.ones((L, L)))).astype(np.float32)
    return _head_mask(), jnp.asarray(tri, jnp.bfloat16), _xform_masks(L)


def _const_spec(a):
    return pl.BlockSpec(a.shape, lambda b, t: (0,) * a.ndim)


def _gdn_kernel(pc_ref, gates_ref, hp_ref, eb_ref, ea_ref, hm_ref, tri_ref, lv_ref, o_ref, s_ref):
    L = GDN_CHUNK
    n_seq, rows = pc_ref.shape[0], pc_ref.shape[1]
    f32, bf16 = jnp.float32, jnp.bfloat16
    n_lev = lv_ref.shape[0] - 3

    @pl.when(pl.program_id(1) == 0)
    def _():
        s_ref[...] = jnp.zeros(s_ref.shape, f32)

    hm = hm_ref[...]
    hmb = hm.astype(bf16)
    eye, incl, strict = lv_ref[n_lev], lv_ref[n_lev + 1], lv_ref[n_lev + 2]

    q_c, k_c, v_c, beta_c, gc_c = [], [], [], [], []
    for b in range(n_seq):
        qkv = pc_ref[b, :, 0:768]
        q_all = _head_l2(qkv[:, 0:256], hmb) * HEAD_DIM ** -0.5
        k_all = _head_l2(qkv[:, 256:512], hmb)
        tail = gates_ref[b]
        beta_all = jax.nn.sigmoid(_dot_sel(tail, eb_ref[...]))
        g = hp_ref[0:1, :] * _softplus(_dot_sel(tail, ea_ref[...]) + hp_ref[1:2, :])
        gc_all = _sel_dot(tri_ref[...], g)
        for c in range(rows // L):
            sl = slice(c * L, (c + 1) * L)
            q_c.append(q_all[sl])
            k_c.append(k_all[sl])
            v_c.append(qkv[sl, 512:768])
            beta_c.append(beta_all[sl])
            gc_c.append(gc_all[sl])
    chains = range(len(q_c))
    egc = [jnp.exp(gc) for gc in gc_c]
    kb = [k_c[i] * beta_c[i] for i in chains]
    a_mat, qk = [], []
    for i in chains:
        seg = jnp.exp(jnp.where(incl > 0.5, gc_c[i] - _diag_row(gc_c[i], eye), NEG))
        k_bd = _block_diag(k_c[i], hmb)
        a_mat.append(_nt_dot(kb[i].astype(bf16), k_bd) * seg * strict)
        qk.append((_nt_dot(q_c[i].astype(bf16), k_bd) * seg).astype(bf16))
    t_inv = [eye - a * lv_ref[0] for a in a_mat]
    for lev in range(1, n_lev):
        te = [jnp.dot(t_inv[i].astype(bf16), _block_diag(a_mat[i] * lv_ref[lev], hmb),
                      preferred_element_type=f32) for i in chains]
        t_inv = [t_inv[i] - jnp.dot(te[i].astype(bf16), _block_diag(t_inv[i], hmb), preferred_element_type=f32)
                 for i in chains]
    parts = []
    for i in chains:
        tb = t_inv[i].astype(bf16)
        u = jnp.dot(tb, _block_diag(v_c[i] * beta_c[i], hmb), preferred_element_type=f32)
        w = jnp.dot(tb, _block_diag(kb[i] * egc[i], hmb), preferred_element_type=f32)
        k_dec = k_c[i] * jnp.exp(gc_c[i][L - 1:L, :] - gc_c[i])
        parts.append((u, w.astype(bf16), (q_c[i] * egc[i]).astype(bf16), qk[i], k_dec.astype(bf16),
                      egc[i][L - 1:L, :]))
    per_seq = rows // L
    state = [s_ref[b] for b in range(n_seq)]
    outs = [[] for _ in range(n_seq)]
    for c in range(per_seq):
        for b in range(n_seq):
            u, w, q_dec, qk_c, k_dec, g_tot = parts[b * per_seq + c]
            sb = state[b].astype(bf16)
            v_new = u - jnp.dot(w, sb, preferred_element_type=f32)
            outs[b].append(jnp.dot(q_dec, sb, preferred_element_type=f32)
                           + jnp.dot(qk_c, _block_diag(v_new, hmb), preferred_element_type=f32))
            state[b] = state[b] * g_tot + _tn_dot(k_dec, v_new.astype(bf16)) * hm
    for b in range(n_seq):
        s_ref[b] = state[b]
        o = jnp.concatenate(outs[b], axis=0)
        z = pc_ref[b, :, 768:1024]
        o_ref[b] = _head_norm(o, hmb) * hp_ref[2:3, :] * (z * jax.nn.sigmoid(z))


def gated_deltanet_pallas(pc, pa, A_log, dt_bias, norm_g):
    B, T, width = pc.shape
    L = GDN_CHUNK
    rows = L * min(CHUNKS_PER_STEP, T // L)
    n_seq = SEQS_PER_STEP if B % SEQS_PER_STEP == 0 else 1
    n = GROUP_HEADS * HEAD_DIM
    hp = _per_head_rows(-jnp.exp(A_log), dt_bias, norm_g)
    consts = (hp, _expand_matrix(GATE_LANE["c_beta"]), _expand_matrix(GATE_LANE["c_alpha"])) + _chunk_consts(L, rows // L)
    return pl.pallas_call(
        _gdn_kernel,
        grid=(B // n_seq, T // rows),
        in_specs=[pl.BlockSpec((n_seq, rows, width), lambda b, t: (b, t, 0)),
                  pl.BlockSpec((n_seq, rows, 128), lambda b, t: (b, t, GATE_BLOCK))]
                 + [_const_spec(a) for a in consts],
        out_specs=pl.BlockSpec((n_seq, rows, 256), lambda b, t: (b, t, 0)),
        out_shape=jax.ShapeDtypeStruct((B, T, 256), jnp.float32),
        scratch_shapes=[pltpu.VMEM((n_seq, n, n), jnp.float32)],
        compiler_params=pltpu.CompilerParams(dimension_semantics=("arbitrary", "arbitrary"),
                                             vmem_limit_bytes=VMEM_LIMIT_BYTES),
        name="gated_deltanet",
    )(pc, pa, *consts)


def _mlstm_kernel(pd_ref, gates_ref, hp_ref, ei_ref, ef_ref, hm_ref, tri_ref, lv_ref, o_ref, c_ref, n_ref, m_ref):
    L = MLSTM_CHUNK
    n_seq, rows = pd_ref.shape[0], pd_ref.shape[1]
    f32, bf16 = jnp.float32, jnp.bfloat16
    n_lev = lv_ref.shape[0] - 3

    @pl.when(pl.program_id(1) == 0)
    def _():
        c_ref[...] = jnp.zeros(c_ref.shape, f32)
        n_ref[...] = jnp.zeros(n_ref.shape, f32)
        m_ref[...] = jnp.zeros(m_ref.shape, f32)

    hm = hm_ref[...]
    hmb = hm.astype(bf16)
    eye, incl = lv_ref[n_lev], lv_ref[n_lev + 1]
    head_of_lane = lax.broadcasted_iota(jnp.int32, (L, GROUP_WIDTH), 1) // HEAD_DIM

    def head_max(a):
        out = jnp.zeros(a.shape, f32)
        for h in range(GROUP_HEADS):
            mine = head_of_lane == h
            out = jnp.where(mine, jnp.max(jnp.where(mine, a, NEG), axis=-1, keepdims=True), out)
        return out

    per_seq = rows // L
    parts = []
    for b in range(n_seq):
        tail = gates_ref[b]
        log_i_all = _dot_sel(tail, ei_ref[...])
        log_f = -_softplus(-(_dot_sel(tail, ef_ref[...]) + hp_ref[0:1, :]))
        b_all = _sel_dot(tri_ref[...], log_f)
        for c in range(per_seq):
            sl = slice(c * L, (c + 1) * L)
            q = pd_ref[b, sl, 0:256]
            k = pd_ref[b, sl, 256:512] * HEAD_DIM ** -0.5
            v = pd_ref[b, sl, 512:768]
            bc, log_i = b_all[sl], log_i_all[sl]
            log_w = jnp.where(incl > 0.5, bc - _diag_row(bc - log_i, eye), NEG)
            b_last = bc[L - 1:L, :]
            log_w_end = b_last - bc + log_i
            qk = _nt_dot(q.astype(bf16), _block_diag(k, hmb))
            parts.append((q, k, v, bc, log_w, head_max(log_w), b_last, log_w_end,
                          jnp.max(log_w_end, axis=0, keepdims=True), qk))
    c_state = [c_ref[b] for b in range(n_seq)]
    n_state = [n_ref[b] for b in range(n_seq)]
    m_prev = [m_ref[b] for b in range(n_seq)]
    outs = [[] for _ in range(n_seq)]
    for c in range(per_seq):
        for b in range(n_seq):
            q, k, v, bc, log_w, m_intra, b_last, log_w_end, m_end, qk = parts[b * per_seq + c]
            log_inter = bc + m_prev[b]
            m_t = jnp.maximum(log_inter, m_intra)
            w_inter = jnp.exp(log_inter - m_t)
            s = qk * jnp.exp(log_w - m_t)
            num = w_inter * jnp.dot(q.astype(bf16), c_state[b].astype(bf16), preferred_element_type=f32)
            num = num + jnp.dot(s.astype(bf16), _block_diag(v, hmb), preferred_element_type=f32)
            den = w_inter * _dot_sel(q * n_state[b], hmb, parts=2) + _dot_sel(s, hmb, parts=2)
            outs[b].append(num / jnp.maximum(jnp.abs(den), jnp.exp(-m_t)))
            m_new = jnp.maximum(b_last + m_prev[b], m_end)
            w_old = jnp.exp(b_last + m_prev[b] - m_new)
            kw = k * jnp.exp(log_w_end - m_new)
            c_state[b] = w_old * c_state[b] + _tn_dot(kw.astype(bf16), v.astype(bf16)) * hm
            n_state[b] = w_old * n_state[b] + jnp.sum(kw, axis=0, keepdims=True)
            m_prev[b] = m_new
    for b in range(n_seq):
        c_ref[b] = c_state[b]
        n_ref[b] = n_state[b]
        m_ref[b] = m_prev[b]
        h = jnp.concatenate(outs[b], axis=0)
        o_ref[b] = _head_norm(h, hmb) * hp_ref[1:2, :] * jax.nn.sigmoid(pd_ref[b, :, 768:1024])


def mlstm_pallas(pd, pa, f_bias, norm_g):
    B, T, width = pd.shape
    L = MLSTM_CHUNK
    rows = L * min(CHUNKS_PER_STEP, T // L)
    n_seq = SEQS_PER_STEP if B % SEQS_PER_STEP == 0 else 1
    n = GROUP_HEADS * HEAD_DIM
    consts = ((_per_head_rows(f_bias, norm_g), _expand_matrix(GATE_LANE["d_i"]), _expand_matrix(GATE_LANE["d_f"]))
              + _chunk_consts(L, rows // L))
    return pl.pallas_call(
        _mlstm_kernel,
        grid=(B // n_seq, T // rows),
        in_specs=[pl.BlockSpec((n_seq, rows, width), lambda b, t: (b, t, 0)),
                  pl.BlockSpec((n_seq, rows, 128), lambda b, t: (b, t, GATE_BLOCK))]
                 + [_const_spec(a) for a in consts],
        out_specs=pl.BlockSpec((n_seq, rows, 256), lambda b, t: (b, t, 0)),
        out_shape=jax.ShapeDtypeStruct((B, T, 256), jnp.float32),
        scratch_shapes=[pltpu.VMEM((n_seq, n, n), jnp.float32), pltpu.VMEM((n_seq, 1, 256), jnp.float32),
                        pltpu.VMEM((n_seq, 1, 256), jnp.float32)],
        compiler_params=pltpu.CompilerParams(dimension_semantics=("arbitrary", "arbitrary"),
                                             vmem_limit_bytes=VMEM_LIMIT_BYTES),
        name="mlstm",
    )(pd, pa, *consts)


def kernel(x, c, ada_w, ada_b, norm_g, ffn1_w13, ffn1_w2, ffn2_w13, ffn2_w2, w_in, b_in, q_norm_g, k_norm_g, cmp_pos, cmp_k_w1, cmp_k_w2, cmp_v_w1, cmp_v_w2, t5_table, sc_conv_w, gdn_conv_w, gdn_A_log, gdn_dt_bias, gdn_norm_g, mlstm_f_bias, mlstm_norm_g, mix_norm_g, w_out):
    B, T, D = x.shape
    depth = ada_w.shape[0]
    bf16 = jnp.bfloat16
    x2 = x.reshape(B * T, D)
    tables = nsa_bias_tables(t5_table, T)
    ovt = nsa_overlap_t(T)
    for l in range(depth):
        mod = ada_modulation(c, ada_w[l], ada_b[l][None, :])
        mod = mod.reshape(B, N_SUBLAYERS, 3, 1, D)
        x2 = ffn_half_step(x2, norm_g[l, 0][None, :], mod[:, 0, 0], mod[:, 0, 1], mod[:, 0, 2],
                           ffn1_w13[l].astype(bf16), ffn1_w2[l].astype(bf16), T)
        ws, bs = split_in_weights(w_in[l], b_in[l])
        pa, y_b, pc, pd = in_projection(x2, norm_g[l, 1][None, :], mod[:, 1, 0], mod[:, 1, 1], ws, bs,
                                        sc_conv_w[l], mix_norm_g[l, 1][None, :], gdn_conv_w[l], T)
        pa, pc, pd = (a.reshape(B, T, -1) for a in (pa, pc, pd))
        y_a = nsa_mixer_pallas(pa, q_norm_g[l], k_norm_g[l], cmp_pos[l], cmp_k_w1[l], cmp_k_w2[l],
                               cmp_v_w1[l], cmp_v_w2[l], tables, ovt, mix_norm_g[l, 0][None, :])
        y_c = gated_deltanet_pallas(pc, pa, gdn_A_log[l], gdn_dt_bias[l], gdn_norm_g[l])
        y_d = mlstm_pallas(pd, pa, mlstm_f_bias[l], mlstm_norm_g[l])
        ys = [y.reshape(B * T, GROUP_WIDTH) for y in (y_a, y_b, y_c, y_d)]
        x2 = ffn_half_step(x2, norm_g[l, 2][None, :], mod[:, 2, 0], mod[:, 2, 1], mod[:, 2, 2],
                           ffn2_w13[l].astype(bf16), ffn2_w2[l].astype(bf16), T,
                           mix=(ys, mod[:, 1, 2], w_out[l].astype(bf16)))
    return x2.reshape(B, T, D)
```

```python
import functools
import math

import jax
import jax.numpy as jnp
import numpy as np
from jax import lax
from jax.experimental import pallas as pl
from jax.experimental.pallas import tpu as pltpu

HEAD_DIM = 64
GROUP_HEADS = 4
GROUP_WIDTH = GROUP_HEADS * HEAD_DIM
CMP_STRIDE = 16
CMP_BLOCK = 32
SLC_BLOCK = 64
N_SLC = 16
N_LOCAL_SLC = 2
WINDOW = 512
Q_BLOCK = 128
FORCE = 1e6
N_BUCKETS = 32
MAX_DISTANCE = 128
GDN_CHUNK = 64
MLSTM_CHUNK = 64
N_SUBLAYERS = 3
EPS = 1e-6

IN_LAYOUT = (
    ("a_q", 256), ("a_k_cmp", 64), ("a_v_cmp", 64),
    ("a_k_slc", 64), ("a_v_slc", 64), ("a_k_win", 64), ("a_v_win", 64),
    ("a_gate", 12),
    ("b_b", 256), ("b_c", 256), ("b_x", 256),
    ("c_q", 256), ("c_k", 256), ("c_v", 256),
    ("c_beta", 4), ("c_alpha", 4), ("c_z", 256),
    ("d_q", 256), ("d_k", 256), ("d_v", 256),
    ("d_i", 4), ("d_f", 4), ("d_o", 256),
)

VMEM_LIMIT_BYTES = 56 * 1024 * 1024
FFN_TOKEN_TILE = 512
FFN_CHUNK = 256
PROJ_TOKEN_TILE = 512
NEG = -1e30
HIGHEST = lax.Precision.HIGHEST
LOG2E = math.log2(math.e)
V1_ROWS = HEAD_DIM + 16


def _modulated_norm(x, g, scale, shift):
    y = x * lax.rsqrt(jnp.mean(x * x, axis=-1, keepdims=True) + EPS)
    return (y * g) * (1.0 + scale) + shift


def _ada_kernel(c_ref, w_ref, b_ref, o_ref):
    c = c_ref[...]
    cond = c * jax.nn.sigmoid(c)
    o_ref[...] = jnp.dot(cond.astype(jnp.bfloat16), w_ref[...].astype(jnp.bfloat16),
                         preferred_element_type=jnp.float32) + b_ref[...]


def ada_modulation(c, w, b, layer):
    B, D = c.shape
    N = w.shape[2]
    tn = 1152
    return pl.pallas_call(
        _ada_kernel,
        grid=(N // tn,),
        in_specs=[pl.BlockSpec((B, D), lambda j: (0, 0)),
                  pl.BlockSpec((None, D, tn), lambda j: (layer, 0, j)),
                  pl.BlockSpec((1, tn), lambda j: (0, j))],
        out_specs=pl.BlockSpec((B, tn), lambda j: (0, j)),
        out_shape=jax.ShapeDtypeStruct((B, N), jnp.float32),
        name="ada_modulation",
    )(c, w, b)


def _ffn_kernel(x_ref, g_ref, shift_ref, scale_ref, gate_ref, w1_ref, w3_ref, w2_ref, *rest, n_mix):
    o_ref, act_ref = rest[-2:]
    x = x_ref[...]
    if n_mix:
        y_refs, mix_gate_ref, wo_ref = rest[:n_mix], rest[n_mix], rest[n_mix + 1]
        z = None
        for k, y_ref in enumerate(y_refs):
            part = jnp.dot(y_ref[...].astype(jnp.bfloat16), wo_ref[k * GROUP_WIDTH:(k + 1) * GROUP_WIDTH, :],
                           preferred_element_type=jnp.float32)
            z = part if z is None else z + part
        x = x + mix_gate_ref[0] * z
    h = _modulated_norm(x, g_ref[...], scale_ref[0], shift_ref[0]).astype(jnp.bfloat16)
    n_chunks = act_ref.shape[1] // FFN_CHUNK
    for ci in range(n_chunks):
        cs = slice(ci * FFN_CHUNK, (ci + 1) * FFN_CHUNK)
        a = jnp.dot(h, w1_ref[:, cs], preferred_element_type=jnp.float32)
        b = jnp.dot(h, w3_ref[:, cs], preferred_element_type=jnp.float32)
        act_ref[:, cs] = (a * jax.nn.sigmoid(a) * b).astype(jnp.bfloat16)
    y = jnp.dot(act_ref[...], w2_ref[...], preferred_element_type=jnp.float32)
    o_ref[...] = x + (0.5 * gate_ref[0]) * y


def ffn_half_step(x2, g, shift, scale, gate, w13, w2, layer, tokens_per_batch, mix=None):
    M, D = x2.shape
    F = w2.shape[1]
    tm = FFN_TOKEN_TILE
    tiles_per_batch = tokens_per_batch // tm
    resident = dict(pipeline_mode=pl.Buffered(1))
    mod_spec = pl.BlockSpec((1, 1, D), lambda i: (i // tiles_per_batch, 0, 0))
    ys, mix_gate, w_out = mix if mix is not None else ((), None, None)
    mix_args = tuple(ys) + ((mix_gate, w_out) if ys else ())
    mix_specs = [pl.BlockSpec((tm, GROUP_WIDTH), lambda i: (i, 0)) for _ in ys]
    if ys:
        mix_specs += [mod_spec, pl.BlockSpec((None,) + w_out.shape[1:], lambda i: (layer, 0, 0), **resident)]
    return pl.pallas_call(
        functools.partial(_ffn_kernel, n_mix=len(ys)),
        grid=(M // tm,),
        in_specs=[pl.BlockSpec((tm, D), lambda i: (i, 0)),
                  pl.BlockSpec((1, D), lambda i: (0, 0)),
                  mod_spec, mod_spec, mod_spec,
                  pl.BlockSpec((None, D, F), lambda i: (layer, 0, 0), **resident),
                  pl.BlockSpec((None, D, F), lambda i: (layer, 0, 1), **resident),
                  pl.BlockSpec((None, F, D), lambda i: (layer, 0, 0), **resident)] + mix_specs,
        out_specs=pl.BlockSpec((tm, D), lambda i: (i, 0)),
        out_shape=jax.ShapeDtypeStruct((M, D), jnp.float32),
        scratch_shapes=[pltpu.VMEM((tm, F), jnp.bfloat16)],
        compiler_params=pltpu.CompilerParams(dimension_semantics=("arbitrary",),
                                             vmem_limit_bytes=VMEM_LIMIT_BYTES),
        name="ffn_half_step",
    )(x2, g, shift, scale, gate, w13, w13, w2, *mix_args)


CONV_CARRY_ROWS = 8
CONV_ROW_CHUNK = 128


def _in_proj_kernel(x_ref, g_ref, shift_ref, scale_ref, wa_ref, wb_ref, wc_ref, wd_ref,
                    ba_ref, bb_ref, bc_ref, bd_ref, scw_ref, scg_ref, gcw_ref,
                    oa_ref, yb_ref, oc_ref, od_ref, braw_ref, craw_ref, *, tiles_per_batch):
    f32 = jnp.float32
    rows = x_ref.shape[0]
    hist = CONV_CARRY_ROWS
    first = pl.program_id(0) % tiles_per_batch == 0

    @pl.when(first)
    def _():
        braw_ref[0:hist, :] = jnp.zeros((hist, braw_ref.shape[1]), f32)
        craw_ref[0:hist, :] = jnp.zeros((hist, craw_ref.shape[1]), f32)

    @pl.when(jnp.logical_not(first))
    def _():
        braw_ref[0:hist, :] = braw_ref[rows:rows + hist, :]
        craw_ref[0:hist, :] = craw_ref[rows:rows + hist, :]

    h = _modulated_norm(x_ref[...], g_ref[...], scale_ref[0], shift_ref[0]).astype(jnp.bfloat16)
    project = lambda w_ref, b_ref: jnp.dot(h, w_ref[...], preferred_element_type=f32) + b_ref[...]
    pb = project(wb_ref, bb_ref)
    yb_ref[...] = pb[:, 0:256]
    braw_ref[hist:hist + rows, :] = pb[:, 256:512] * pb[:, 512:768]
    pc = project(wc_ref, bc_ref)
    craw_ref[hist:hist + rows, :] = pc[:, 0:768]
    oc_ref[:, 768:] = pc[:, 768:]
    n_chunks = rows // CONV_ROW_CHUNK
    pieces = []
    for w_ref, b_ref, o_ref in ((wa_ref, ba_ref, oa_ref), (wd_ref, bd_ref, od_ref)):
        half = w_ref.shape[1] // 256 // 2 * 256
        pieces += [(w_ref, b_ref, o_ref, 0, half), (w_ref, b_ref, o_ref, half, w_ref.shape[1])]
    for ci, r0 in enumerate(range(0, rows, CONV_ROW_CHUNK)):
        for w_ref, b_ref, o_ref, c0, c1 in pieces[ci * len(pieces) // n_chunks:(ci + 1) * len(pieces) // n_chunks]:
            o_ref[:, c0:c1] = jnp.dot(h, w_ref[:, c0:c1], preferred_element_type=f32) + b_ref[:, c0:c1]
        out_rows = slice(r0, r0 + CONV_ROW_CHUNK)
        tap = lambda ref, s: ref[pl.ds(hist + r0 - s, CONV_ROW_CHUNK), :]
        conv = sum(tap(braw_ref, s) * scw_ref[2 - s:3 - s, :] for s in range(3))
        y = yb_ref[out_rows, :] * conv
        yb_ref[out_rows, :] = y * lax.rsqrt(jnp.mean(y * y, axis=-1, keepdims=True) + EPS) * scg_ref[...]
        for c0 in range(0, 3 * GROUP_WIDTH, GROUP_WIDTH):
            cols = slice(c0, c0 + GROUP_WIDTH)
            conv = sum(craw_ref[pl.ds(hist + r0 - s, CONV_ROW_CHUNK), cols] * gcw_ref[3 - s:4 - s, cols]
                       for s in range(4))
            oc_ref[out_rows, cols] = conv * jax.nn.sigmoid(conv)


def in_projection(x2, g, shift, scale, ws, bs, sc_conv_w, sc_norm_g, gdn_conv_w, tokens_per_batch):
    M, D = x2.shape
    tm = PROJ_TOKEN_TILE
    tiles_per_batch = tokens_per_batch // tm
    mod_spec = pl.BlockSpec((1, 1, D), lambda i: (i // tiles_per_batch, 0, 0))
    const = lambda a, **kw: pl.BlockSpec(a.shape, lambda i: (0, 0), **kw)
    pad8 = lambda w: jnp.pad(w, ((0, 8 - w.shape[0]), (0, 0)))
    extras = (pad8(sc_conv_w), sc_norm_g, pad8(gdn_conv_w))
    widths = (ws[0].shape[1], GROUP_WIDTH, ws[2].shape[1], ws[3].shape[1])
    return pl.pallas_call(
        functools.partial(_in_proj_kernel, tiles_per_batch=tiles_per_batch),
        grid=(M // tm,),
        in_specs=[pl.BlockSpec((tm, D), lambda i: (i, 0)), const(g), mod_spec, mod_spec]
                 + [const(w, pipeline_mode=pl.Buffered(1)) for w in ws] + [const(b) for b in bs]
                 + [const(e) for e in extras],
        out_specs=[pl.BlockSpec((tm, n), lambda i: (i, 0)) for n in widths],
        out_shape=[jax.ShapeDtypeStruct((M, n), jnp.float32) for n in widths],
        scratch_shapes=[pltpu.VMEM((CONV_CARRY_ROWS + tm, GROUP_WIDTH), jnp.float32),
                        pltpu.VMEM((CONV_CARRY_ROWS + tm, 3 * GROUP_WIDTH), jnp.float32)],
        compiler_params=pltpu.CompilerParams(dimension_semantics=("arbitrary",),
                                             vmem_limit_bytes=VMEM_LIMIT_BYTES),
        name="in_projection",
    )(x2, g, shift, scale, *ws, *bs, *extras)


def _group_columns():
    offs, o = {}, 0
    for n, s in IN_LAYOUT:
        offs[n] = np.arange(o, o + s)
        o += s
    cat = lambda names: np.concatenate([offs[n] for n in names])
    return (cat(["a_q", "a_k_cmp", "a_v_cmp", "a_k_slc", "a_v_slc", "a_k_win", "a_v_win"] + list(SMALL_GATES)),
            cat(["b_b", "b_c", "b_x"]),
            cat(["c_q", "c_k", "c_v", "c_z"]),
            cat(["d_q", "d_k", "d_v", "d_o"]))


SMALL_GATES = ("a_gate", "c_beta", "c_alpha", "d_i", "d_f")
GATE_BLOCK = 5
GATE_LANE = {"a_gate": 0, "c_beta": 12, "c_alpha": 16, "d_i": 20, "d_f": 24}
GROUP_SLAB_WIDTH = (768, 768, 1024, 1024)


def _take_columns(a, idx, width):
    cuts = [0] + [k + 1 for k in range(idx.size - 1) if idx[k + 1] != idx[k] + 1] + [idx.size]
    runs = [a[..., int(idx[s]):int(idx[e - 1]) + 1] for s, e in zip(cuts[:-1], cuts[1:])]
    return jnp.concatenate(runs + [jnp.zeros(a.shape[:-1] + (width - idx.size,), a.dtype)], axis=-1)


def split_in_weights(w, b):
    ws, bs = [], []
    for idx, width in zip(_group_columns(), GROUP_SLAB_WIDTH):
        ws.append(_take_columns(w, idx, width).astype(jnp.bfloat16))
        bs.append(_take_columns(b, idx, width)[None, :])
    return ws, bs


def t5_bucket(dist):
    n = jnp.maximum(dist, 0)
    max_exact = N_BUCKETS // 2
    nf = jnp.maximum(n, 1).astype(jnp.float32)
    large = max_exact + (jnp.log(nf / max_exact) / math.log(MAX_DISTANCE / max_exact)
                         * (N_BUCKETS - max_exact)).astype(jnp.int32)
    large = jnp.minimum(large, N_BUCKETS - 1)
    return jnp.where(n < max_exact, n, large)


def _nt_dot(a, b):
    return lax.dot_general(a, b, (((1,), (1,)), ((), ())), preferred_element_type=jnp.float32)


def _nsa_prep_kernel(q_ref, kvs_ref, kvw_ref, qg_ref, kg_ref, seg_ref,
                     qnt_ref, kxs_ref, v1s_ref, kkw_ref, v1w_ref):
    tm = q_ref.shape[0]
    q = q_ref[...]
    ss = _dot_sel(q * q, seg_ref[...], parts=3)
    qn = q * lax.rsqrt(ss * (1.0 / HEAD_DIM) + EPS) * qg_ref[...]
    qnt_ref[...] = (qn * LOG2E).T.astype(qnt_ref.dtype)
    lane = lax.broadcasted_iota(jnp.int32, (tm, 128), 1)
    lo = lane < HEAD_DIM

    def split(x):
        xr = pltpu.roll(x, HEAD_DIM, 1)
        ss = jnp.sum(jnp.where(lo, x * x, 0.0), axis=-1, keepdims=True)
        kk = jnp.where(lo, x, xr) * lax.rsqrt(ss * (1.0 / HEAD_DIM) + EPS) * kg_ref[...]
        return kk, jnp.where(lo, xr, 1.0)

    kk, v1 = split(kvs_ref[...])
    tok = pl.program_id(1) * tm + lax.broadcasted_iota(jnp.int32, (tm, 128), 0)
    onehot = jnp.where(lane == tok // SLC_BLOCK, 1.0, 0.0)
    kxs_ref[...] = jnp.concatenate([kk, onehot], axis=1).astype(kxs_ref.dtype)
    v1s_ref[...] = v1.T[0:V1_ROWS].astype(v1s_ref.dtype)
    kk, v1 = split(kvw_ref[...])
    kkw_ref[...] = kk.astype(kkw_ref.dtype)
    v1w_ref[...] = v1.T[0:V1_ROWS].astype(v1w_ref.dtype)


def nsa_prep(pa, qg, kg2, seg):
    B, T, _ = pa.shape
    tm = 512
    bf16 = jnp.bfloat16
    col = lambda w, j: pl.BlockSpec((None, tm, w), lambda b, i: (b, i, j))
    rowblk = lambda h: pl.BlockSpec((None, h, tm), lambda b, i: (b, 0, i))
    const = lambda shape: pl.BlockSpec(shape, lambda b, i: (0, 0))
    return pl.pallas_call(
        _nsa_prep_kernel,
        grid=(B, T // tm),
        in_specs=[col(256, 0), col(128, 3), col(128, 4), const((1, 256)), const((1, 128)), const((256, 256))],
        out_specs=[rowblk(256), col(256, 0), rowblk(V1_ROWS), col(128, 0), rowblk(V1_ROWS)],
        out_shape=[jax.ShapeDtypeStruct((B, 256, T), bf16),
                   jax.ShapeDtypeStruct((B, T, 256), bf16), jax.ShapeDtypeStruct((B, V1_ROWS, T), bf16),
                   jax.ShapeDtypeStruct((B, T, 128), bf16), jax.ShapeDtypeStruct((B, V1_ROWS, T), bf16)],
        name="nsa_prep",
    )(pa, pa, pa, qg, kg2, seg)


def _compress_kernel(kv_ref, pos_ref, w1_ref, w2k_ref, w2v_ref, kg_ref, kkc_ref, vct_ref):
    nrow = kkc_ref.shape[0]
    f32, bf16 = jnp.float32, jnp.bfloat16
    hidden = w1_ref.shape[2] // 2
    first = jnp.zeros((nrow, 2 * hidden), f32)
    second = jnp.zeros((nrow, 2 * hidden), f32)
    for p in range(CMP_STRIDE):
        xp = kv_ref[pl.ds(p, nrow, stride=CMP_STRIDE), :]
        first = first + jnp.dot((xp + pos_ref[p:p + 1, :]).astype(bf16), w1_ref[p], preferred_element_type=f32)
        second = second + jnp.dot((xp + pos_ref[CMP_STRIDE + p:CMP_STRIDE + p + 1, :]).astype(bf16),
                                  w1_ref[CMP_STRIDE + p], preferred_element_type=f32)
    hid = first + pltpu.roll(second, nrow - 1, 0)
    hid = (hid * jax.nn.sigmoid(hid)).astype(bf16)
    valid = lax.broadcasted_iota(jnp.int32, (nrow, 128), 0) < nrow - 1
    kc = jnp.dot(hid[:, 0:hidden], w2k_ref[...], preferred_element_type=f32)
    kc = kc * lax.rsqrt(jnp.mean(kc * kc, axis=-1, keepdims=True) + EPS) * kg_ref[...]
    kkc_ref[...] = jnp.where(valid, kc, 0.0).astype(kkc_ref.dtype)
    vc = jnp.dot(hid[:, hidden:2 * hidden], w2v_ref[...], preferred_element_type=f32)
    vct_ref[...] = jnp.where(valid, vc, 0.0).T[0:HEAD_DIM].astype(vct_ref.dtype)


def nsa_compress(pa, pos2, w1, w2k, w2v, kg2):
    B, T, _ = pa.shape
    nrow = T // CMP_STRIDE
    bf16 = jnp.bfloat16
    full = lambda a: pl.BlockSpec(a.shape, lambda b: (0,) * a.ndim)
    return pl.pallas_call(
        _compress_kernel,
        grid=(B,),
        in_specs=[pl.BlockSpec((None, T, 128), lambda b: (b, 0, 2)), full(pos2), full(w1), full(w2k), full(w2v),
                  full(kg2)],
        out_specs=[pl.BlockSpec((None, nrow, 128), lambda b: (b, 0, 0)),
                   pl.BlockSpec((None, HEAD_DIM, nrow), lambda b: (b, 0, 0))],
        out_shape=[jax.ShapeDtypeStruct((B, nrow, 128), bf16), jax.ShapeDtypeStruct((B, HEAD_DIM, nrow), bf16)],
        name="nsa_compress",
    )(pa, pos2, w1, w2k, w2v, kg2)


CMP_TILES_PER_STEP = 4


def _stack_heads_t(qt):
    lo = lax.broadcasted_iota(jnp.int32, (128, qt.shape[1]), 0) < HEAD_DIM
    zero = jnp.zeros((), qt.dtype)
    return jnp.concatenate([jnp.where(lo if h % 2 == 0 else ~lo, qt[128 * (h // 2):128 * (h // 2) + 128], zero)
                            for h in range(GROUP_HEADS)], axis=1)


def _cmp_select_kernel(qnt_ref, kkc_ref, vct_ref, cb_ref, ov_ref, ocmp_ref, pen_ref, sc_ref, *, n_sel):
    i = pl.program_id(1)
    n_tiles = pen_ref.shape[0]
    tq = qnt_ref.shape[1] // n_tiles
    nblk, nrow = ov_ref.shape
    keys_per_tile = tq // CMP_STRIDE
    f32, bf16 = jnp.float32, jnp.bfloat16
    blk = lax.broadcasted_iota(jnp.int32, (nblk, tq), 0)
    scores = []
    for g in range(n_tiles):
        cols = slice(g * tq, (g + 1) * tq)
        q4t = _stack_heads_t(qnt_ref[:, cols])
        first_row = pl.multiple_of(nrow - keys_per_tile * (i * n_tiles + g), keys_per_tile)
        s = jnp.dot(kkc_ref[...], q4t, preferred_element_type=f32) + cb_ref[pl.ds(first_row, nrow), :]
        m = jnp.max(s, axis=0, keepdims=True)
        e = jnp.where(s > 0.5 * NEG, jnp.exp2(s - m), 0.0)
        p = e * (1.0 / jnp.maximum(jnp.sum(e, axis=0, keepdims=True), 1e-30))
        o = jnp.dot(vct_ref[...], p.astype(bf16), preferred_element_type=f32)
        for h in range(GROUP_HEADS):
            ocmp_ref[h * HEAD_DIM:(h + 1) * HEAD_DIM, cols] = o[:, h * tq:(h + 1) * tq]
        psum = p[:, 0:tq] + p[:, tq:2 * tq] + p[:, 2 * tq:3 * tq] + p[:, 3 * tq:4 * tq]
        hi = psum.astype(bf16)
        lo_part = (psum - hi.astype(f32)).astype(bf16)
        score = (jnp.dot(ov_ref[...], hi, preferred_element_type=f32)
                 + jnp.dot(ov_ref[...], lo_part, preferred_element_type=f32))
        qpos = (i * n_tiles + g) * tq + lax.broadcasted_iota(jnp.int32, (nblk, tq), 1)
        cur = qpos // SLC_BLOCK
        forced = (blk == 0) | ((cur - blk >= 0) & (cur - blk < N_LOCAL_SLC))
        score = jnp.where(forced, FORCE, score)
        score = jnp.where(blk <= cur, score, -FORCE)
        sc_ref[g] = score
        scores.append(score)
    blocks_per_tile = tq // SLC_BLOCK

    def rank_step(t, cnts):
        out = []
        for g in range(n_tiles):
            cnt = cnts[g]
            for u in range(blocks_per_tile):
                jp = t * blocks_per_tile + u
                row = sc_ref[g, pl.ds(jp, 1), :]
                tie = jnp.where(blk > jp, 1.0, 0.0)
                cnt = cnt + jnp.where(row > scores[g], 1.0, jnp.where(row == scores[g], tie, 0.0))
            out.append(cnt)
        return tuple(out)

    zeros = tuple(jnp.zeros((nblk, tq), f32) for _ in range(n_tiles))
    cnts = lax.fori_loop(0, (i + 1) * n_tiles, rank_step, zeros)
    for g in range(n_tiles):
        pen = jnp.where(cnts[g] < n_sel, 0.0, NEG)
        if nblk < 128:
            pen = jnp.concatenate([pen, jnp.zeros((128 - nblk, tq), f32)], axis=0)
        pen_ref[g] = pen.astype(pen_ref.dtype)


def nsa_cmp_select(qnt, kkc, vct, cbias, ov, n_sel):
    B, _, T = qnt.shape
    nrow = kkc.shape[1]
    nblk = ov.shape[0]
    tq = Q_BLOCK
    g = min(CMP_TILES_PER_STEP, T // tq)
    return pl.pallas_call(
        functools.partial(_cmp_select_kernel, n_sel=n_sel),
        grid=(B, T // (g * tq)),
        in_specs=[pl.BlockSpec((None, 256, g * tq), lambda b, i: (b, 0, i)),
                  pl.BlockSpec((None, nrow, 128), lambda b, i: (b, 0, 0)),
                  pl.BlockSpec((None, HEAD_DIM, nrow), lambda b, i: (b, 0, 0)),
                  pl.BlockSpec((2 * nrow, 4 * tq), lambda b, i: (0, 0)),
                  pl.BlockSpec((nblk, nrow), lambda b, i: (0, 0))],
        out_specs=[pl.BlockSpec((None, 256, g * tq), lambda b, i: (b, 0, i)),
                   pl.BlockSpec((None, g, 128, tq), lambda b, i: (b, i, 0, 0))],
        out_shape=[jax.ShapeDtypeStruct((B, 256, T), jnp.float32),
                   jax.ShapeDtypeStruct((B, T // tq, 128, tq), jnp.bfloat16)],
        scratch_shapes=[pltpu.VMEM((g, nblk, tq), jnp.float32)],
        name="nsa_cmp_select",
    )(qnt, kkc, vct, cbias, ov)


NSA_KEY_TILE = 256
NSA_QUERY_TILE = 256


def _nsa_attn_kernel(qnt_ref, pen_ref, ocmp_ref, gate_ref, kxs_ref, v1s_ref, kkw_ref, v1w_ref,
                     tabs_ref, tabw_ref, mixg_ref, o_ref, acc_ref, m_ref, sa_ref, sb_ref):
    i = pl.program_id(1)
    tq = o_ref.shape[0]
    kt = tabs_ref.shape[1]
    f32, bf16 = jnp.float32, jnp.bfloat16
    q4t = _stack_heads_t(qnt_ref[...])
    pen = jnp.concatenate([pen_ref[t] for t in range(pen_ref.shape[0])], axis=1)
    rhs_sel = jnp.concatenate([q4t, jnp.concatenate([pen] * GROUP_HEADS, axis=1)], axis=0)

    def branch(rhs, k_ref, v_ref, tab_ref, first_tile, last_tile):
        n_tab = tab_ref.shape[0] - 1
        m_ref[...] = jnp.full(m_ref.shape, NEG, f32)
        acc_ref[...] = jnp.zeros(acc_ref.shape, f32)

        def key_offset(j):
            return pl.multiple_of(jnp.minimum(j, last_tile) * kt, kt)

        def scores(j):
            delta = jnp.where(j > last_tile, n_tab, jnp.minimum(i - j * (kt // tq), n_tab - 1))
            return jnp.dot(k_ref[pl.ds(key_offset(j), kt), :], rhs, preferred_element_type=f32) + tab_ref[delta]

        def absorb(s, j):
            m_old = m_ref[...]
            m_new = jnp.maximum(m_old, jnp.max(s, axis=0, keepdims=True))
            p = jnp.exp2(s - m_new)
            pv = jnp.dot(v_ref[:, pl.ds(key_offset(j), kt)], p.astype(bf16), preferred_element_type=f32)
            acc_ref[...] = jnp.exp2(m_old - m_new) * acc_ref[...] + pv
            m_ref[...] = m_new

        sa_ref[...] = scores(first_tile)

        def body(t, carry):
            j = first_tile + 2 * t
            sb_ref[...] = scores(j + 1)
            absorb(sa_ref[...], j)
            sa_ref[...] = scores(j + 2)
            absorb(sb_ref[...], j + 1)
            return carry

        lax.fori_loop(0, (last_tile - first_tile) // 2 + 1, body, 0)
        acc = acc_ref[...]
        return acc[0:HEAD_DIM] / jnp.maximum(acc[HEAD_DIM:HEAD_DIM + 1], 1e-30)

    def window_branch(rhs):
        n_tab = tabw_ref.shape[0] - 1
        m = jnp.full((1, rhs.shape[1]), NEG, f32)
        acc = jnp.zeros((V1_ROWS, rhs.shape[1]), f32)
        for d in range(n_tab):
            j = i - d
            off = pl.multiple_of(jnp.maximum(j, 0) * kt, kt)
            s = jnp.dot(kkw_ref[pl.ds(off, kt), :], rhs, preferred_element_type=f32)
            s = s + tabw_ref[jnp.where(j < 0, n_tab, d)]
            m_new = jnp.maximum(m, jnp.max(s, axis=0, keepdims=True))
            p = jnp.exp2(s - m_new)
            acc = jnp.exp2(m - m_new) * acc +jnp.dot(v1w_ref[:, pl.ds(off, kt)], p.astype(bf16),
                                                     preferred_element_type=f32)
            m = m_new
        return acc[0:HEAD_DIM] / jnp.maximum(acc[HEAD_DIM:HEAD_DIM + 1], 1e-30)

    last = (i * tq) // kt
    if kt == tq:
        o_w = window_branch(q4t)
    else:
        o_w = branch(q4t, kkw_ref, v1w_ref, tabw_ref, (jnp.maximum(i - WINDOW // tq, 0) * tq) // kt, last)
    o_s = branch(rhs_sel, kxs_ref, v1s_ref, tabs_ref, 0, last)
    gt = jax.nn.sigmoid(gate_ref[...]).T
    oct_ = ocmp_ref[...]
    parts = []
    for h in range(GROUP_HEADS):
        cols = slice(h * tq, (h + 1) * tq)
        parts.append(gt[3 * h:3 * h + 1] * oct_[h * HEAD_DIM:(h + 1) * HEAD_DIM]
                     + gt[3 * h + 1:3 * h + 2] * o_s[:, cols] + gt[3 * h + 2:3 * h + 3] * o_w[:, cols])
    yt = jnp.concatenate(parts, axis=0)
    yt = yt * lax.rsqrt(jnp.mean(yt * yt, axis=0, keepdims=True) + EPS) * mixg_ref[...]
    o_ref[...] = yt.T


def nsa_attention(qnt, pen, ocmp, pa, kxs, v1s, kkw, v1w, tab_s, tab_w, mixg):
    B, _, T = qnt.shape
    tq = min(NSA_QUERY_TILE, T)
    tile = lambda w, j: pl.BlockSpec((None, tq, w), lambda b, i: (b, i, j))
    whole = lambda w: pl.BlockSpec((None, T, w), lambda b, i: (b, 0, 0))
    whole_t = pl.BlockSpec((None, V1_ROWS, T), lambda b, i: (b, 0, 0))
    full = lambda a: pl.BlockSpec(a.shape, lambda b, i: (0,) * a.ndim)
    return pl.pallas_call(
        _nsa_attn_kernel,
        grid=(B, T // tq),
        in_specs=[pl.BlockSpec((None, 256, tq), lambda b, i: (b, 0, i)),
                  pl.BlockSpec((None, tq // Q_BLOCK, 128, Q_BLOCK), lambda b, i: (b, i, 0, 0)),
                  pl.BlockSpec((None, 256, tq), lambda b, i: (b, 0, i)), tile(128, 5),
                  whole(256), whole_t, whole(128), whole_t, full(tab_s), full(tab_w), full(mixg)],
        out_specs=tile(256, 0),
        out_shape=jax.ShapeDtypeStruct((B, T, 256), jnp.float32),
        scratch_shapes=[pltpu.VMEM((V1_ROWS, GROUP_HEADS * tq), jnp.float32),
                        pltpu.VMEM((1, GROUP_HEADS * tq), jnp.float32),
                        pltpu.VMEM((tab_s.shape[1], GROUP_HEADS * tq), jnp.float32),
                        pltpu.VMEM((tab_s.shape[1], GROUP_HEADS * tq), jnp.float32)],
        compiler_params=pltpu.CompilerParams(dimension_semantics=("arbitrary", "arbitrary"),
                                             vmem_limit_bytes=VMEM_LIMIT_BYTES),
        name="nsa_attention",
    )(qnt, pen, ocmp, pa, kxs, v1s, kkw, v1w, tab_s, tab_w, mixg)


def _toeplitz(wr, rows, cols):
    H, n = wr.shape
    flat = jnp.tile(jnp.pad(wr, ((0, 0), (0, 1))), (1, rows))[:, :rows * n]
    return flat.reshape(H, rows, n)[:, :, rows - 1:rows - 1 + cols]


def nsa_bias_tables(t5_table, T):
    H = GROUP_HEADS
    tq = Q_BLOCK
    ta = min(NSA_QUERY_TILE, T)
    kt = min(NSA_KEY_TILE, T)
    nrow = T // CMP_STRIDE
    dmax = WINDOW + 2 * ta + kt
    onehot = (t5_bucket(jnp.arange(dmax))[:, None] == jnp.arange(N_BUCKETS)[None, :]).astype(jnp.float32)
    by_dist = jnp.dot(onehot, t5_table, precision=HIGHEST).T
    neg = lambda n: jnp.full((H, n), NEG, jnp.float32)
    f_s = jnp.concatenate([neg(kt - 1), by_dist * LOG2E], axis=1)
    f_w = jnp.concatenate([neg(kt - 1), by_dist[:, :WINDOW] * LOG2E, neg(dmax - WINDOW)], axis=1)
    span = ta + kt - 1

    def tiles(f, n):
        out = [_toeplitz(f[:, k * ta:k * ta + span], kt, ta).transpose(1, 0, 2).reshape(kt, H * ta) for k in range(n)]
        return jnp.stack(out + [jnp.full((kt, H * ta), NEG, jnp.float32)])

    n_s = -(-(kt + MAX_DISTANCE) // ta) + 1
    n_w = WINDOW // ta + kt // ta
    tab_s, tab_w = tiles(f_s, n_s), tiles(f_w, n_w)
    per_tile = tq // CMP_STRIDE
    d0 = -CMP_STRIDE * (nrow - 1) - (CMP_BLOCK - 1)
    n16 = 2 * nrow + per_tile - 1
    n_pos = CMP_STRIDE * n16 + d0
    far = jnp.broadcast_to(by_dist[:, -1:], (H, max(n_pos - dmax, 0)))
    w16 = jnp.concatenate([neg(-d0), by_dist[:, :n_pos] * LOG2E, far * LOG2E], axis=1).reshape(H, n16, CMP_STRIDE)
    w16 = w16[:, ::-1, :]
    base = jnp.stack([w16[:, per_tile - 1 - a:per_tile - 1 - a + 2 * nrow, :] for a in range(per_tile)], axis=1)
    cbias = base.transpose(0, 1, 3, 2).reshape(H * tq, 2 * nrow).T
    return tab_s, tab_w, cbias


def nsa_overlap_t(T):
    nrow = T // CMP_STRIDE
    nblk = T // SLC_BLOCK
    ci = np.arange(nrow)[None, :]
    bj = np.arange(nblk)[:, None]
    ov = (ci * CMP_STRIDE < (bj + 1) * SLC_BLOCK) & (ci * CMP_STRIDE + CMP_BLOCK > bj * SLC_BLOCK)
    ov = ov & (ci < nrow - 1)
    return jnp.asarray(ov.astype(np.float32), dtype=jnp.bfloat16)


def nsa_mixer_pallas(pa, q_norm_g, k_norm_g, cmp_pos, cmp_k_w1, cmp_k_w2, cmp_v_w1, cmp_v_w2,
                     tables, ovt, mixg):
    B, T, _ = pa.shape
    bf16 = jnp.bfloat16
    tab_s, tab_w, cbias = tables
    qg = (jnp.tile(q_norm_g, GROUP_HEADS) * HEAD_DIM ** -0.5)[None, :]
    kg2 = jnp.tile(k_norm_g, 2)[None, :]
    qnt, kxs, v1s, kkw, v1w = nsa_prep(pa, qg, kg2, _head_mask().astype(bf16))
    dup = lambda w: jnp.concatenate([w, w], axis=1)
    zeros = jnp.zeros((CMP_BLOCK, HEAD_DIM, cmp_k_w1.shape[1]), jnp.float32)
    per_pos = lambda w: w.reshape(CMP_BLOCK, HEAD_DIM, -1)
    w1 = jnp.concatenate([jnp.concatenate([per_pos(cmp_k_w1), zeros], axis=2),
                          jnp.concatenate([zeros, per_pos(cmp_v_w1)], axis=2)], axis=1).astype(bf16)
    kkc, vct = nsa_compress(pa, dup(cmp_pos), w1, dup(cmp_k_w2).astype(bf16), dup(cmp_v_w2).astype(bf16), kg2)
    n_sel = min(N_SLC, T // SLC_BLOCK)
    ocmp, pen = nsa_cmp_select(qnt, kkc, vct, cbias, ovt, n_sel)
    mixg_col = jnp.broadcast_to(mixg.reshape(-1, 1), (GROUP_WIDTH, min(NSA_QUERY_TILE, T)))
    return nsa_attention(qnt, pen, ocmp, pa, kxs, v1s, kkw, v1w, tab_s, tab_w, mixg_col)


CHUNKS_PER_STEP = 4
SEQS_PER_STEP = 2


def _tn_dot(a, b):
    return lax.dot_general(a, b, (((0,), (0,)), ((), ())), preferred_element_type=jnp.float32)


def _split_bf16(a, parts):
    out, rest = [], a
    for _ in range(parts):
        piece = rest.astype(jnp.bfloat16)
        out.append(piece)
        rest = rest - piece.astype(jnp.float32)
    return out


def _dot_sel(a, sel, parts=3):
    return sum(jnp.dot(p, sel, preferred_element_type=jnp.float32) for p in _split_bf16(a, parts))


def _sel_dot(sel, a, parts=3):
    return sum(jnp.dot(sel, p, preferred_element_type=jnp.float32) for p in _split_bf16(a, parts))


def _softplus(x):
    return jnp.maximum(x, 0.0) + jnp.log(1.0 + jnp.exp(-jnp.abs(x)))


def _block_diag(a, hmb):
    return jnp.concatenate([a.astype(jnp.bfloat16)] * GROUP_HEADS, axis=0) * hmb


def _diag_row(a, eye):
    return jnp.sum(a * eye, axis=0, keepdims=True)


def _head_norm(o, hmb):
    return o * lax.rsqrt(_dot_sel(o * o, hmb, parts=2) * (1.0 / HEAD_DIM) + EPS)


def _head_l2(o, hmb):
    return o * lax.rsqrt(_dot_sel(o * o, hmb, parts=2) + EPS)


def _xform_masks(L):
    i = np.arange(L)[:, None]
    j = np.arange(L)[None, :]
    masks = []
    s = 1
    while s < L:
        masks.append((i // (2 * s) == j // (2 * s)) & (i % (2 * s) >= s) & (j % (2 * s) < s))
        s *= 2
    masks += [i == j, j <= i, j < i]
    return jnp.asarray(np.tile(np.stack(masks).astype(np.float32), (1, 1, GROUP_HEADS)))


def _expand_matrix(first_lane):
    e = np.zeros((128, GROUP_WIDTH), np.float32)
    for h in range(GROUP_HEADS):
        e[first_lane + h, h * HEAD_DIM:(h + 1) * HEAD_DIM] = 1.0
    return jnp.asarray(e, jnp.bfloat16)


def _head_mask():
    return jnp.asarray(np.kron(np.eye(GROUP_HEADS), np.ones((HEAD_DIM, HEAD_DIM))), jnp.float32)


def _per_head_rows(*vecs):
    rows = [jnp.repeat(v, HEAD_DIM) if v.shape[0] == GROUP_HEADS else jnp.tile(v, GROUP_HEADS) for v in vecs]
    return jnp.pad(jnp.stack(rows), ((0, 8 - len(rows)), (0, 0)))


def _chunk_consts(L, n_chunks):
    tri = np.kron(np.eye(n_chunks), np.tril(np.ones((L, L)))).astype(np.float32)
    return _head_mask(), jnp.asarray(tri, jnp.bfloat16), _xform_masks(L)


def _const_spec(a):
    return pl.BlockSpec(a.shape, lambda b, t: (0,) * a.ndim)


def _gdn_kernel(pc_ref, gates_ref, hp_ref, eb_ref, ea_ref, hm_ref, tri_ref, lv_ref, o_ref, s_ref):
    L = GDN_CHUNK
    n_seq, rows = pc_ref.shape[0], pc_ref.shape[1]
    f32, bf16 = jnp.float32, jnp.bfloat16
    n_lev = lv_ref.shape[0] - 3

    @pl.when(pl.program_id(1) == 0)
    def _():
        s_ref[...] = jnp.zeros(s_ref.shape, f32)

    hm = hm_ref[...]
    hmb = hm.astype(bf16)
    eye, incl, strict = lv_ref[n_lev], lv_ref[n_lev + 1], lv_ref[n_lev + 2]

    q_c, k_c, v_c, beta_c, gc_c = [], [], [], [], []
    for b in range(n_seq):
        qkv = pc_ref[b, :, 0:768]
        q_all = _head_l2(qkv[:, 0:256], hmb) * HEAD_DIM ** -0.5
        k_all = _head_l2(qkv[:, 256:512], hmb)
        tail = gates_ref[b]
        beta_all = jax.nn.sigmoid(_dot_sel(tail, eb_ref[...]))
        g = hp_ref[0:1, :] * _softplus(_dot_sel(tail, ea_ref[...]) + hp_ref[1:2, :])
        gc_all = _sel_dot(tri_ref[...], g)
        for c in range(rows // L):
            sl = slice(c * L, (c + 1) * L)
            q_c.append(q_all[sl])
            k_c.append(k_all[sl])
            v_c.append(qkv[sl, 512:768])
            beta_c.append(beta_all[sl])
            gc_c.append(gc_all[sl])
    chains = range(len(q_c))
    egc = [jnp.exp(gc) for gc in gc_c]
    kb = [k_c[i] * beta_c[i] for i in chains]
    a_mat, qk = [], []
    for i in chains:
        seg = jnp.exp(jnp.where(incl > 0.5, gc_c[i] - _diag_row(gc_c[i], eye), NEG))
        k_bd = _block_diag(k_c[i], hmb)
        a_mat.append(_nt_dot(kb[i].astype(bf16), k_bd) * seg * strict)
        qk.append((_nt_dot(q_c[i].astype(bf16), k_bd) * seg).astype(bf16))
    t_inv = [eye - a * lv_ref[0] for a in a_mat]
    for lev in range(1, n_lev):
        te = [jnp.dot(t_inv[i].astype(bf16), _block_diag(a_mat[i] * lv_ref[lev], hmb),
                      preferred_element_type=f32) for i in chains]
        t_inv = [t_inv[i] - jnp.dot(te[i].astype(bf16), _block_diag(t_inv[i], hmb), preferred_element_type=f32)
                 for i in chains]
    parts = []
    for i in chains:
        tb = t_inv[i].astype(bf16)
        u = jnp.dot(tb, _block_diag(v_c[i] * beta_c[i], hmb), preferred_element_type=f32)
        w = jnp.dot(tb, _block_diag(kb[i] * egc[i], hmb), preferred_element_type=f32)
        k_dec = k_c[i] * jnp.exp(gc_c[i][L - 1:L, :] - gc_c[i])
        parts.append((u, w.astype(bf16), (q_c[i] * egc[i]).astype(bf16), qk[i], k_dec.astype(bf16),
                      egc[i][L - 1:L, :]))
    per_seq = rows // L
    state = [s_ref[b] for b in range(n_seq)]
    outs = [[] for _ in range(n_seq)]
    for c in range(per_seq):
        for b in range(n_seq):
            u, w, q_dec, qk_c, k_dec, g_tot = parts[b * per_seq + c]
            sb = state[b].astype(bf16)
            v_new = u - jnp.dot(w, sb, preferred_element_type=f32)
            outs[b].append(jnp.dot(q_dec, sb, preferred_element_type=f32)
                           + jnp.dot(qk_c, _block_diag(v_new, hmb), preferred_element_type=f32))
            state[b] = state[b] * g_tot + _tn_dot(k_dec, v_new.astype(bf16)) * hm
    for b in range(n_seq):
        s_ref[b] = state[b]
        o = jnp.concatenate(outs[b], axis=0)
        z = pc_ref[b, :, 768:1024]
        o_ref[b] = _head_norm(o, hmb) * hp_ref[2:3, :] * (z * jax.nn.sigmoid(z))


def gated_deltanet_pallas(pc, pa, A_log, dt_bias, norm_g):
    B, T, width = pc.shape
    L = GDN_CHUNK
    rows = L * min(CHUNKS_PER_STEP, T // L)
    n_seq = SEQS_PER_STEP if B % SEQS_PER_STEP == 0 else 1
    n = GROUP_HEADS * HEAD_DIM
    hp = _per_head_rows(-jnp.exp(A_log), dt_bias, norm_g)
    consts = (hp, _expand_matrix(GATE_LANE["c_beta"]), _expand_matrix(GATE_LANE["c_alpha"])) + _chunk_consts(L, rows // L)
    return pl.pallas_call(
        _gdn_kernel,
        grid=(B // n_seq, T // rows),
        in_specs=[pl.BlockSpec((n_seq, rows, width), lambda b, t: (b, t, 0)),
                  pl.BlockSpec((n_seq, rows, 128), lambda b, t: (b, t, GATE_BLOCK))]
                 + [_const_spec(a) for a in consts],
        out_specs=pl.BlockSpec((n_seq, rows, 256), lambda b, t: (b, t, 0)),
        out_shape=jax.ShapeDtypeStruct((B, T, 256), jnp.float32),
        scratch_shapes=[pltpu.VMEM((n_seq, n, n), jnp.float32)],
        compiler_params=pltpu.CompilerParams(dimension_semantics=("arbitrary", "arbitrary"),
                                             vmem_limit_bytes=VMEM_LIMIT_BYTES),
        name="gated_deltanet",
    )(pc, pa, *consts)


def _mlstm_kernel(pd_ref, gates_ref, hp_ref, ei_ref, ef_ref, hm_ref, tri_ref, lv_ref, o_ref, c_ref, n_ref, m_ref):
    L = MLSTM_CHUNK
    n_seq, rows = pd_ref.shape[0], pd_ref.shape[1]
    f32, bf16 = jnp.float32, jnp.bfloat16
    n_lev = lv_ref.shape[0] - 3

    @pl.when(pl.program_id(1) == 0)
    def _():
        c_ref[...] = jnp.zeros(c_ref.shape, f32)
        n_ref[...] = jnp.zeros(n_ref.shape, f32)
        m_ref[...] = jnp.zeros(m_ref.shape, f32)

    hm = hm_ref[...]
    hmb = hm.astype(bf16)
    eye, incl = lv_ref[n_lev], lv_ref[n_lev + 1]
    head_of_lane = lax.broadcasted_iota(jnp.int32, (L, GROUP_WIDTH), 1) // HEAD_DIM

    def head_max(a):
        out = jnp.zeros(a.shape, f32)
        for h in range(GROUP_HEADS):
            mine = head_of_lane == h
            out = jnp.where(mine, jnp.max(jnp.where(mine, a, NEG), axis=-1, keepdims=True), out)
        return out

    per_seq = rows // L
    parts = []
    for b in range(n_seq):
        tail = gates_ref[b]
        log_i_all = _dot_sel(tail, ei_ref[...])
        log_f = -_softplus(-(_dot_sel(tail, ef_ref[...]) + hp_ref[0:1, :]))
        b_all = _sel_dot(tri_ref[...], log_f)
        for c in range(per_seq):
            sl = slice(c * L, (c + 1) * L)
            q = pd_ref[b, sl, 0:256]
            k = pd_ref[b, sl, 256:512] * HEAD_DIM ** -0.5
            v = pd_ref[b, sl, 512:768]
            bc, log_i = b_all[sl], log_i_all[sl]
            log_w = jnp.where(incl > 0.5, bc - _diag_row(bc - log_i, eye), NEG)
            b_last = bc[L - 1:L, :]
            log_w_end = b_last - bc + log_i
            qk = _nt_dot(q.astype(bf16), _block_diag(k, hmb))
            parts.append((q, k, v, bc, log_w, head_max(log_w), b_last, log_w_end,
                          jnp.max(log_w_end, axis=0, keepdims=True), qk))
    c_state = [c_ref[b] for b in range(n_seq)]
    n_state = [n_ref[b] for b in range(n_seq)]
    m_prev = [m_ref[b] for b in range(n_seq)]
    outs = [[] for _ in range(n_seq)]
    for c in range(per_seq):
        for b in range(n_seq):
            q, k, v, bc, log_w, m_intra, b_last, log_w_end, m_end, qk = parts[b * per_seq + c]
            log_inter = bc + m_prev[b]
            m_t = jnp.maximum(log_inter, m_intra)
            w_inter = jnp.exp(log_inter - m_t)
            s = qk * jnp.exp(log_w - m_t)
            num = w_inter * jnp.dot(q.astype(bf16), c_state[b].astype(bf16), preferred_element_type=f32)
            num = num + jnp.dot(s.astype(bf16), _block_diag(v, hmb), preferred_element_type=f32)
            den = w_inter * _dot_sel(q * n_state[b], hmb, parts=2) + _dot_sel(s, hmb, parts=2)
            outs[b].append(num / jnp.maximum(jnp.abs(den), jnp.exp(-m_t)))
            m_new = jnp.maximum(b_last + m_prev[b], m_end)
            w_old = jnp.exp(b_last + m_prev[b] - m_new)
            kw = k * jnp.exp(log_w_end - m_new)
            c_state[b] = w_old * c_state[b] + _tn_dot(kw.astype(bf16), v.astype(bf16)) * hm
            n_state[b] = w_old * n_state[b] + jnp.sum(kw, axis=0, keepdims=True)
            m_prev[b] = m_new
    for b in range(n_seq):
        c_ref[b] = c_state[b]
        n_ref[b] = n_state[b]
        m_ref[b] = m_prev[b]
        h = jnp.concatenate(outs[b], axis=0)
        o_ref[b] = _head_norm(h, hmb) * hp_ref[1:2, :] * jax.nn.sigmoid(pd_ref[b, :, 768:1024])


def mlstm_pallas(pd, pa, f_bias, norm_g):
    B, T, width = pd.shape
    L = MLSTM_CHUNK
    rows = L * min(CHUNKS_PER_STEP, T // L)
    n_seq = SEQS_PER_STEP if B % SEQS_PER_STEP == 0 else 1
    n = GROUP_HEADS * HEAD_DIM
    consts = ((_per_head_rows(f_bias, norm_g), _expand_matrix(GATE_LANE["d_i"]), _expand_matrix(GATE_LANE["d_f"]))
              + _chunk_consts(L, rows // L))
    return pl.pallas_call(
        _mlstm_kernel,
        grid=(B // n_seq, T // rows),
        in_specs=[pl.BlockSpec((n_seq, rows, width), lambda b, t: (b, t, 0)),
                  pl.BlockSpec((n_seq, rows, 128), lambda b, t: (b, t, GATE_BLOCK))]
                 + [_const_spec(a) for a in consts],
        out_specs=pl.BlockSpec((n_seq, rows, 256), lambda b, t: (b, t, 0)),
        out_shape=jax.ShapeDtypeStruct((B, T, 256), jnp.float32),
        scratch_shapes=[pltpu.VMEM((n_seq, n, n), jnp.float32), pltpu.VMEM((n_seq, 1, 256), jnp.float32),
                        pltpu.VMEM((n_seq, 1, 256), jnp.float32)],
        compiler_params=pltpu.CompilerParams(dimension_semantics=("arbitrary", "arbitrary"),
                                             vmem_limit_bytes=VMEM_LIMIT_BYTES),
        name="mlstm",
    )(pd, pa, *consts)


def kernel(x, c, ada_w, ada_b, norm_g, ffn1_w13, ffn1_w2, ffn2_w13, ffn2_w2, w_in, b_in, q_norm_g, k_norm_g, cmp_pos, cmp_k_w1, cmp_k_w2, cmp_v_w1, cmp_v_w2, t5_table, sc_conv_w, gdn_conv_w, gdn_A_log, gdn_dt_bias, gdn_norm_g, mlstm_f_bias, mlstm_norm_g, mix_norm_g, w_out):
    B, T, D = x.shape
    depth = ada_w.shape[0]
    bf16 = jnp.bfloat16
    x2 = x.reshape(B * T, D)
    tables = nsa_bias_tables(t5_table, T)
    ovt = nsa_overlap_t(T)
    ffn_w = [w.astype(bf16) for w in (ffn1_w13, ffn1_w2, ffn2_w13, ffn2_w2, w_out)]
    for l in range(depth):
        mod = ada_modulation(c, ada_w, ada_b[l][None, :], l)
        mod = mod.reshape(B, N_SUBLAYERS, 3, 1, D)
        x2 = ffn_half_step(x2, norm_g[l, 0][None, :], mod[:, 0, 0], mod[:, 0, 1], mod[:, 0, 2],
                           ffn_w[0], ffn_w[1], l, T)
        ws, bs = split_in_weights(w_in[l], b_in[l])
        pa, y_b, pc, pd = in_projection(x2, norm_g[l, 1][None, :], mod[:, 1, 0], mod[:, 1, 1], ws, bs,
                                        sc_conv_w[l], mix_norm_g[l, 1][None, :], gdn_conv_w[l], T)
        pa, pc, pd = (a.reshape(B, T, -1) for a in (pa, pc, pd))
        y_a = nsa_mixer_pallas(pa, q_norm_g[l], k_norm_g[l], cmp_pos[l], cmp_k_w1[l], cmp_k_w2[l],
                               cmp_v_w1[l], cmp_v_w2[l], tables, ovt, mix_norm_g[l, 0][None, :])
        y_c = gated_deltanet_pallas(pc, pa, gdn_A_log[l], gdn_dt_bias[l], gdn_norm_g[l])
        y_d = mlstm_pallas(pd, pa, mlstm_f_bias[l], mlstm_norm_g[l])
        ys = [y.reshape(B * T, GROUP_WIDTH) for y in (y_a, y_b, y_c, y_d)]
        x2 = ffn_half_step(x2, norm_g[l, 2][None, :], mod[:, 2, 0], mod[:, 2, 1], mod[:, 2, 2],
                           ffn_w[2], ffn_w[3], l, T, mix=(ys, mod[:, 1, 2], ffn_w[4]))
    return x2.reshape(B, T, D)
```

```python
import functools
import math

import jax
import jax.numpy as jnp
import numpy as np
from jax import lax
from jax.experimental import pallas as pl
from jax.experimental.pallas import tpu as pltpu

HEAD_DIM = 64
GROUP_HEADS = 4
GROUP_WIDTH = GROUP_HEADS * HEAD_DIM
CMP_STRIDE = 16
CMP_BLOCK = 32
SLC_BLOCK = 64
N_SLC = 16
N_LOCAL_SLC = 2
WINDOW = 512
Q_BLOCK = 128
FORCE = 1e6
N_BUCKETS = 32
MAX_DISTANCE = 128
GDN_CHUNK = 64
MLSTM_CHUNK = 64
N_SUBLAYERS = 3
EPS = 1e-6

IN_LAYOUT = (
    ("a_q", 256), ("a_k_cmp", 64), ("a_v_cmp", 64),
    ("a_k_slc", 64), ("a_v_slc", 64), ("a_k_win", 64), ("a_v_win", 64),
    ("a_gate", 12),
    ("b_b", 256), ("b_c", 256), ("b_x", 256),
    ("c_q", 256), ("c_k", 256), ("c_v", 256),
    ("c_beta", 4), ("c_alpha", 4), ("c_z", 256),
    ("d_q", 256), ("d_k", 256), ("d_v", 256),
    ("d_i", 4), ("d_f", 4), ("d_o", 256),
)

VMEM_LIMIT_BYTES = 56 * 1024 * 1024
FFN_TOKEN_TILE = 512
FFN_CHUNK = 256
PROJ_TOKEN_TILE = 512
NEG = -1e30
HIGHEST = lax.Precision.HIGHEST
LOG2E = math.log2(math.e)
V1_ROWS = HEAD_DIM + 16


def _modulated_norm(x, g, scale, shift):
    y = x * lax.rsqrt(jnp.mean(x * x, axis=-1, keepdims=True) + EPS)
    return (y * g) * (1.0 + scale) + shift


def _ada_kernel(c_ref, w_ref, b_ref, o_ref):
    c = c_ref[...]
    cond = c * jax.nn.sigmoid(c)
    o_ref[...] = jnp.dot(cond.astype(jnp.bfloat16), w_ref[...].astype(jnp.bfloat16),
                         preferred_element_type=jnp.float32) + b_ref[...]


def ada_modulation(c, w, b, layer):
    B, D = c.shape
    N = w.shape[2]
    tn = 1152
    return pl.pallas_call(
        _ada_kernel,
        grid=(N // tn,),
        in_specs=[pl.BlockSpec((B, D), lambda j: (0, 0)),
                  pl.BlockSpec((None, D, tn), lambda j: (layer, 0, j)),
                  pl.BlockSpec((1, tn), lambda j: (0, j))],
        out_specs=pl.BlockSpec((B, tn), lambda j: (0, j)),
        out_shape=jax.ShapeDtypeStruct((B, N), jnp.float32),
        name="ada_modulation",
    )(c, w, b)


def _ffn_kernel(x_ref, g_ref, shift_ref, scale_ref, gate_ref, w1_ref, w3_ref, w2_ref, *rest, n_mix):
    o_ref, act_ref = rest[-2:]
    x = x_ref[...]
    if n_mix:
        y_refs, mix_gate_ref, wo_ref = rest[:n_mix], rest[n_mix], rest[n_mix + 1]
        z = None
        for k, y_ref in enumerate(y_refs):
            part = jnp.dot(y_ref[...].astype(jnp.bfloat16), wo_ref[k * GROUP_WIDTH:(k + 1) * GROUP_WIDTH, :],
                           preferred_element_type=jnp.float32)
            z = part if z is None else z + part
        x = x + mix_gate_ref[0] * z
    h = _modulated_norm(x, g_ref[...], scale_ref[0], shift_ref[0]).astype(jnp.bfloat16)
    n_chunks = act_ref.shape[1] // FFN_CHUNK
    for ci in range(n_chunks):
        cs = slice(ci * FFN_CHUNK, (ci + 1) * FFN_CHUNK)
        a = jnp.dot(h, w1_ref[:, cs], preferred_element_type=jnp.float32)
        b = jnp.dot(h, w3_ref[:, cs], preferred_element_type=jnp.float32)
        act_ref[:, cs] = (a * jax.nn.sigmoid(a) * b).astype(jnp.bfloat16)
    y = jnp.dot(act_ref[...], w2_ref[...], preferred_element_type=jnp.float32)
    o_ref[...] = x + (0.5 * gate_ref[0]) * y


def ffn_half_step(x2, g, shift, scale, gate, w13, w2, layer, tokens_per_batch, mix=None):
    M, D = x2.shape
    F = w2.shape[1]
    tm = FFN_TOKEN_TILE
    tiles_per_batch = tokens_per_batch // tm
    resident = dict(pipeline_mode=pl.Buffered(1))
    mod_spec = pl.BlockSpec((1, 1, D), lambda i: (i // tiles_per_batch, 0, 0))
    ys, mix_gate, w_out = mix if mix is not None else ((), None, None)
    mix_args = tuple(ys) + ((mix_gate, w_out) if ys else ())
    mix_specs = [pl.BlockSpec((tm, GROUP_WIDTH), lambda i: (i, 0)) for _ in ys]
    if ys:
        mix_specs += [mod_spec, pl.BlockSpec((None,) + w_out.shape[1:], lambda i: (layer, 0, 0), **resident)]
    return pl.pallas_call(
        functools.partial(_ffn_kernel, n_mix=len(ys)),
        grid=(M // tm,),
        in_specs=[pl.BlockSpec((tm, D), lambda i: (i, 0)),
                  pl.BlockSpec((1, D), lambda i: (0, 0)),
                  mod_spec, mod_spec, mod_spec,
                  pl.BlockSpec((None, D, F), lambda i: (layer, 0, 0), **resident),
                  pl.BlockSpec((None, D, F), lambda i: (layer, 0, 1), **resident),
                  pl.BlockSpec((None, F, D), lambda i: (layer, 0, 0), **resident)] + mix_specs,
        out_specs=pl.BlockSpec((tm, D), lambda i: (i, 0)),
        out_shape=jax.ShapeDtypeStruct((M, D), jnp.float32),
        scratch_shapes=[pltpu.VMEM((tm, F), jnp.bfloat16)],
        compiler_params=pltpu.CompilerParams(dimension_semantics=("arbitrary",),
                                             vmem_limit_bytes=VMEM_LIMIT_BYTES),
        name="ffn_half_step",
    )(x2, g, shift, scale, gate, w13, w13, w2, *mix_args)


CONV_CARRY_ROWS = 8
CONV_ROW_CHUNK = 128


def _in_proj_kernel(x_ref, g_ref, shift_ref, scale_ref, wa_ref, wb_ref, wc_ref, wd_ref,
                    ba_ref, bb_ref, bc_ref, bd_ref, scw_ref, scg_ref, gcw_ref,
                    oa_ref, yb_ref, oc_ref, od_ref, braw_ref, craw_ref, *, tiles_per_batch):
    f32 = jnp.float32
    rows = x_ref.shape[0]
    hist = CONV_CARRY_ROWS
    first = pl.program_id(0) % tiles_per_batch == 0

    @pl.when(first)
    def _():
        braw_ref[0:hist, :] = jnp.zeros((hist, braw_ref.shape[1]), f32)
        craw_ref[0:hist, :] = jnp.zeros((hist, craw_ref.shape[1]), f32)

    @pl.when(jnp.logical_not(first))
    def _():
        braw_ref[0:hist, :] = braw_ref[rows:rows + hist, :]
        craw_ref[0:hist, :] = craw_ref[rows:rows + hist, :]

    h = _modulated_norm(x_ref[...], g_ref[...], scale_ref[0], shift_ref[0]).astype(jnp.bfloat16)
    project = lambda w_ref, b_ref: jnp.dot(h, w_ref[...], preferred_element_type=f32) + b_ref[...]
    pb = project(wb_ref, bb_ref)
    yb_ref[...] = pb[:, 0:256]
    braw_ref[hist:hist + rows, :] = pb[:, 256:512] * pb[:, 512:768]
    pc = project(wc_ref, bc_ref)
    craw_ref[hist:hist + rows, :] = pc[:, 0:768]
    oc_ref[:, 768:] = pc[:, 768:]
    n_chunks = rows // CONV_ROW_CHUNK
    pieces = []
    for w_ref, b_ref, o_ref in ((wa_ref, ba_ref, oa_ref), (wd_ref, bd_ref, od_ref)):
        half = w_ref.shape[1] // 256 // 2 * 256
        pieces += [(w_ref, b_ref, o_ref, 0, half), (w_ref, b_ref, o_ref, half, w_ref.shape[1])]
    for ci, r0 in enumerate(range(0, rows, CONV_ROW_CHUNK)):
        for w_ref, b_ref, o_ref, c0, c1 in pieces[ci * len(pieces) // n_chunks:(ci + 1) * len(pieces) // n_chunks]:
            o_ref[:, c0:c1] = jnp.dot(h, w_ref[:, c0:c1], preferred_element_type=f32) + b_ref[:, c0:c1]
        out_rows = slice(r0, r0 + CONV_ROW_CHUNK)
        tap = lambda ref, s: ref[pl.ds(hist + r0 - s, CONV_ROW_CHUNK), :]
        conv = sum(tap(braw_ref, s) * scw_ref[2 - s:3 - s, :] for s in range(3))
        y = yb_ref[out_rows, :] * conv
        yb_ref[out_rows, :] = y * lax.rsqrt(jnp.mean(y * y, axis=-1, keepdims=True) + EPS) * scg_ref[...]
        for c0 in range(0, 3 * GROUP_WIDTH, GROUP_WIDTH):
            cols = slice(c0, c0 + GROUP_WIDTH)
            conv = sum(craw_ref[pl.ds(hist + r0 - s, CONV_ROW_CHUNK), cols] * gcw_ref[3 - s:4 - s, cols]
                       for s in range(4))
            oc_ref[out_rows, cols] = conv * jax.nn.sigmoid(conv)


def in_projection(x2, g, shift, scale, ws, bs, sc_conv_w, sc_norm_g, gdn_conv_w, tokens_per_batch):
    M, D = x2.shape
    tm = PROJ_TOKEN_TILE
    tiles_per_batch = tokens_per_batch // tm
    mod_spec = pl.BlockSpec((1, 1, D), lambda i: (i // tiles_per_batch, 0, 0))
    const = lambda a, **kw: pl.BlockSpec(a.shape, lambda i: (0, 0), **kw)
    pad8 = lambda w: jnp.pad(w, ((0, 8 - w.shape[0]), (0, 0)))
    extras = (pad8(sc_conv_w), sc_norm_g, pad8(gdn_conv_w))
    widths = (ws[0].shape[1], GROUP_WIDTH, ws[2].shape[1], ws[3].shape[1])
    return pl.pallas_call(
        functools.partial(_in_proj_kernel, tiles_per_batch=tiles_per_batch),
        grid=(M // tm,),
        in_specs=[pl.BlockSpec((tm, D), lambda i: (i, 0)), const(g), mod_spec, mod_spec]
                 + [const(w, pipeline_mode=pl.Buffered(1)) for w in ws] + [const(b) for b in bs]
                 + [const(e) for e in extras],
        out_specs=[pl.BlockSpec((tm, n), lambda i: (i, 0)) for n in widths],
        out_shape=[jax.ShapeDtypeStruct((M, n), jnp.float32) for n in widths],
        scratch_shapes=[pltpu.VMEM((CONV_CARRY_ROWS + tm, GROUP_WIDTH), jnp.float32),
                        pltpu.VMEM((CONV_CARRY_ROWS + tm, 3 * GROUP_WIDTH), jnp.float32)],
        compiler_params=pltpu.CompilerParams(dimension_semantics=("arbitrary",),
                                             vmem_limit_bytes=VMEM_LIMIT_BYTES),
        name="in_projection",
    )(x2, g, shift, scale, *ws, *bs, *extras)


def _group_columns():
    offs, o = {}, 0
    for n, s in IN_LAYOUT:
        offs[n] = np.arange(o, o + s)
        o += s
    cat = lambda names: np.concatenate([offs[n] for n in names])
    return (cat(["a_q", "a_k_cmp", "a_v_cmp", "a_k_slc", "a_v_slc", "a_k_win", "a_v_win"] + list(SMALL_GATES)),
            cat(["b_b", "b_c", "b_x"]),
            cat(["c_q", "c_k", "c_v", "c_z"]),
            cat(["d_q", "d_k", "d_v", "d_o"]))


SMALL_GATES = ("a_gate", "c_beta", "c_alpha", "d_i", "d_f")
GATE_BLOCK = 5
GATE_LANE = {"a_gate": 0, "c_beta": 12, "c_alpha": 16, "d_i": 20, "d_f": 24}
GROUP_SLAB_WIDTH = (768, 768, 1024, 1024)


def _take_columns(a, idx, width):
    cuts = [0] + [k + 1 for k in range(idx.size - 1) if idx[k + 1] != idx[k] + 1] + [idx.size]
    runs = [a[..., int(idx[s]):int(idx[e - 1]) + 1] for s, e in zip(cuts[:-1], cuts[1:])]
    return jnp.concatenate(runs + [jnp.zeros(a.shape[:-1] + (width - idx.size,), a.dtype)], axis=-1)


def split_in_weights(w, b):
    ws, bs = [], []
    for idx, width in zip(_group_columns(), GROUP_SLAB_WIDTH):
        ws.append(_take_columns(w, idx, width).astype(jnp.bfloat16))
        bs.append(_take_columns(b, idx, width)[None, :])
    return ws, bs


def t5_bucket(dist):
    n = jnp.maximum(dist, 0)
    max_exact = N_BUCKETS // 2
    nf = jnp.maximum(n, 1).astype(jnp.float32)
    large = max_exact + (jnp.log(nf / max_exact) / math.log(MAX_DISTANCE / max_exact)
                         * (N_BUCKETS - max_exact)).astype(jnp.int32)
    large = jnp.minimum(large, N_BUCKETS - 1)
    return jnp.where(n < max_exact, n, large)


def _nt_dot(a, b):
    return lax.dot_general(a, b, (((1,), (1,)), ((), ())), preferred_element_type=jnp.float32)


def _nsa_prep_kernel(q_ref, kvs_ref, kvw_ref, qg_ref, kg_ref, seg_ref,
                     qnt_ref, kxs_ref, v1s_ref, kkw_ref, v1w_ref):
    tm = q_ref.shape[0]
    q = q_ref[...]
    ss = _dot_sel(q * q, seg_ref[...], parts=3)
    qn = q * lax.rsqrt(ss * (1.0 / HEAD_DIM) + EPS) * qg_ref[...]
    qnt_ref[...] = (qn * LOG2E).T.astype(qnt_ref.dtype)
    lane = lax.broadcasted_iota(jnp.int32, (tm, 128), 1)
    lo = lane < HEAD_DIM

    def split(x):
        xr = pltpu.roll(x, HEAD_DIM, 1)
        ss = jnp.sum(jnp.where(lo, x * x, 0.0), axis=-1, keepdims=True)
        kk = jnp.where(lo, x, xr) * lax.rsqrt(ss * (1.0 / HEAD_DIM) + EPS) * kg_ref[...]
        return kk, jnp.where(lo, xr, 1.0)

    kk, v1 = split(kvs_ref[...])
    tok = pl.program_id(1) * tm + lax.broadcasted_iota(jnp.int32, (tm, 128), 0)
    onehot = jnp.where(lane == tok // SLC_BLOCK, 1.0, 0.0)
    kxs_ref[...] = jnp.concatenate([kk, onehot], axis=1).astype(kxs_ref.dtype)
    v1s_ref[...] = v1.T[0:V1_ROWS].astype(v1s_ref.dtype)
    kk, v1 = split(kvw_ref[...])
    kkw_ref[...] = kk.astype(kkw_ref.dtype)
    v1w_ref[...] = v1.T[0:V1_ROWS].astype(v1w_ref.dtype)


def nsa_prep(pa, qg, kg2, seg):
    B, T, _ = pa.shape
    tm = 512
    bf16 = jnp.bfloat16
    col = lambda w, j: pl.BlockSpec((None, tm, w), lambda b, i: (b, i, j))
    rowblk = lambda h: pl.BlockSpec((None, h, tm), lambda b, i: (b, 0, i))
    const = lambda shape: pl.BlockSpec(shape, lambda b, i: (0, 0))
    return pl.pallas_call(
        _nsa_prep_kernel,
        grid=(B, T // tm),
        in_specs=[col(256, 0), col(128, 3), col(128, 4), const((1, 256)), const((1, 128)), const((256, 256))],
        out_specs=[rowblk(256), col(256, 0), rowblk(V1_ROWS), col(128, 0), rowblk(V1_ROWS)],
        out_shape=[jax.ShapeDtypeStruct((B, 256, T), bf16),
                   jax.ShapeDtypeStruct((B, T, 256), bf16), jax.ShapeDtypeStruct((B, V1_ROWS, T), bf16),
                   jax.ShapeDtypeStruct((B, T, 128), bf16), jax.ShapeDtypeStruct((B, V1_ROWS, T), bf16)],
        name="nsa_prep",
    )(pa, pa, pa, qg, kg2, seg)


def _compress_kernel(kv_ref, pos_ref, w1_ref, w2k_ref, w2v_ref, kg_ref, kkc_ref, vct_ref):
    nrow = kkc_ref.shape[0]
    f32, bf16 = jnp.float32, jnp.bfloat16
    hidden = w1_ref.shape[2] // 2
    first = jnp.zeros((nrow, 2 * hidden), f32)
    second = jnp.zeros((nrow, 2 * hidden), f32)
    for p in range(CMP_STRIDE):
        xp = kv_ref[pl.ds(p, nrow, stride=CMP_STRIDE), :]
        first = first + jnp.dot((xp + pos_ref[p:p + 1, :]).astype(bf16), w1_ref[p], preferred_element_type=f32)
        second = second + jnp.dot((xp + pos_ref[CMP_STRIDE + p:CMP_STRIDE + p + 1, :]).astype(bf16),
                                  w1_ref[CMP_STRIDE + p], preferred_element_type=f32)
    hid = first + pltpu.roll(second, nrow - 1, 0)
    hid = (hid * jax.nn.sigmoid(hid)).astype(bf16)
    valid = lax.broadcasted_iota(jnp.int32, (nrow, 128), 0) < nrow - 1
    kc = jnp.dot(hid[:, 0:hidden], w2k_ref[...], preferred_element_type=f32)
    kc = kc * lax.rsqrt(jnp.mean(kc * kc, axis=-1, keepdims=True) + EPS) * kg_ref[...]
    kkc_ref[...] = jnp.where(valid, kc, 0.0).astype(kkc_ref.dtype)
    vc = jnp.dot(hid[:, hidden:2 * hidden], w2v_ref[...], preferred_element_type=f32)
    vct_ref[...] = jnp.where(valid, vc, 0.0).T[0:HEAD_DIM].astype(vct_ref.dtype)


def nsa_compress(pa, pos2, w1, w2k, w2v, kg2):
    B, T, _ = pa.shape
    nrow = T // CMP_STRIDE
    bf16 = jnp.bfloat16
    full = lambda a: pl.BlockSpec(a.shape, lambda b: (0,) * a.ndim)
    return pl.pallas_call(
        _compress_kernel,
        grid=(B,),
        in_specs=[pl.BlockSpec((None, T, 128), lambda b: (b, 0, 2)), full(pos2), full(w1), full(w2k), full(w2v),
                  full(kg2)],
        out_specs=[pl.BlockSpec((None, nrow, 128), lambda b: (b, 0, 0)),
                   pl.BlockSpec((None, HEAD_DIM, nrow), lambda b: (b, 0, 0))],
        out_shape=[jax.ShapeDtypeStruct((B, nrow, 128), bf16), jax.ShapeDtypeStruct((B, HEAD_DIM, nrow), bf16)],
        name="nsa_compress",
    )(pa, pos2, w1, w2k, w2v, kg2)


CMP_TILES_PER_STEP = 4


def _stack_heads_t(qt):
    lo = lax.broadcasted_iota(jnp.int32, (128, qt.shape[1]), 0) < HEAD_DIM
    zero = jnp.zeros((), qt.dtype)
    return jnp.concatenate([jnp.where(lo if h % 2 == 0 else ~lo, qt[128 * (h // 2):128 * (h // 2) + 128], zero)
                            for h in range(GROUP_HEADS)], axis=1)


def _cmp_select_kernel(qnt_ref, kkc_ref, vct_ref, cb_ref, ov_ref, ocmp_ref, pen_ref, sc_ref, *, n_sel):
    i = pl.program_id(1)
    n_tiles = pen_ref.shape[0]
    tq = qnt_ref.shape[1] // n_tiles
    nblk, nrow = ov_ref.shape
    keys_per_tile = tq // CMP_STRIDE
    f32, bf16 = jnp.float32, jnp.bfloat16
    blk = lax.broadcasted_iota(jnp.int32, (nblk, tq), 0)
    scores = []
    for g in range(n_tiles):
        cols = slice(g * tq, (g + 1) * tq)
        q4t = _stack_heads_t(qnt_ref[:, cols])
        first_row = pl.multiple_of(nrow - keys_per_tile * (i * n_tiles + g), keys_per_tile)
        s = jnp.dot(kkc_ref[...], q4t, preferred_element_type=f32) + cb_ref[pl.ds(first_row, nrow), :]
        m = jnp.max(s, axis=0, keepdims=True)
        e = jnp.where(s > 0.5 * NEG, jnp.exp2(s - m), 0.0)
        p = e * (1.0 / jnp.maximum(jnp.sum(e, axis=0, keepdims=True), 1e-30))
        o = jnp.dot(vct_ref[...], p.astype(bf16), preferred_element_type=f32)
        for h in range(GROUP_HEADS):
            ocmp_ref[h * HEAD_DIM:(h + 1) * HEAD_DIM, cols] = o[:, h * tq:(h + 1) * tq]
        psum = p[:, 0:tq] + p[:, tq:2 * tq] + p[:, 2 * tq:3 * tq] + p[:, 3 * tq:4 * tq]
        hi = psum.astype(bf16)
        lo_part = (psum - hi.astype(f32)).astype(bf16)
        score = (jnp.dot(ov_ref[...], hi, preferred_element_type=f32)
                 + jnp.dot(ov_ref[...], lo_part, preferred_element_type=f32))
        qpos = (i * n_tiles + g) * tq + lax.broadcasted_iota(jnp.int32, (nblk, tq), 1)
        cur = qpos // SLC_BLOCK
        forced = (blk == 0) | ((cur - blk >= 0) & (cur - blk < N_LOCAL_SLC))
        score = jnp.where(forced, FORCE, score)
        score = jnp.where(blk <= cur, score, -FORCE)
        sc_ref[g] = score
        scores.append(score)
    blocks_per_tile = tq // SLC_BLOCK

    def rank_step(t, cnts):
        out = []
        for g in range(n_tiles):
            cnt = cnts[g]
            for u in range(blocks_per_tile):
                jp = t * blocks_per_tile + u
                row = sc_ref[g, pl.ds(jp, 1), :]
                tie = jnp.where(blk > jp, 1.0, 0.0)
                cnt = cnt + jnp.where(row > scores[g], 1.0, jnp.where(row == scores[g], tie, 0.0))
            out.append(cnt)
        return tuple(out)

    zeros = tuple(jnp.zeros((nblk, tq), f32) for _ in range(n_tiles))
    cnts = lax.fori_loop(0, (i + 1) * n_tiles, rank_step, zeros)
    for g in range(n_tiles):
        pen = jnp.where(cnts[g] < n_sel, 0.0, NEG)
        if nblk < 128:
            pen = jnp.concatenate([pen, jnp.zeros((128 - nblk, tq), f32)], axis=0)
        pen_ref[g] = pen.astype(pen_ref.dtype)


def nsa_cmp_select(qnt, kkc, vct, cbias, ov, n_sel):
    B, _, T = qnt.shape
    nrow = kkc.shape[1]
    nblk = ov.shape[0]
    tq = Q_BLOCK
    g = min(CMP_TILES_PER_STEP, T // tq)
    return pl.pallas_call(
        functools.partial(_cmp_select_kernel, n_sel=n_sel),
        grid=(B, T // (g * tq)),
        in_specs=[pl.BlockSpec((None, 256, g * tq), lambda b, i: (b, 0, i)),
                  pl.BlockSpec((None, nrow, 128), lambda b, i: (b, 0, 0)),
                  pl.BlockSpec((None, HEAD_DIM, nrow), lambda b, i: (b, 0, 0)),
                  pl.BlockSpec((2 * nrow, 4 * tq), lambda b, i: (0, 0)),
                  pl.BlockSpec((nblk, nrow), lambda b, i: (0, 0))],
        out_specs=[pl.BlockSpec((None, 256, g * tq), lambda b, i: (b, 0, i)),
                   pl.BlockSpec((None, g, 128, tq), lambda b, i: (b, i, 0, 0))],
        out_shape=[jax.ShapeDtypeStruct((B, 256, T), jnp.float32),
                   jax.ShapeDtypeStruct((B, T // tq, 128, tq), jnp.bfloat16)],
        scratch_shapes=[pltpu.VMEM((g, nblk, tq), jnp.float32)],
        name="nsa_cmp_select",
    )(qnt, kkc, vct, cbias, ov)


NSA_KEY_TILE = 256
NSA_QUERY_TILE = 256


def _nsa_attn_kernel(qnt_ref, pen_ref, ocmp_ref, gate_ref, kxs_ref, v1s_ref, kkw_ref, v1w_ref,
                     tabs_ref, tabw_ref, mixg_ref, o_ref, acc_ref, m_ref, sa_ref, sb_ref):
    i = pl.program_id(1)
    tq = o_ref.shape[0]
    kt = tabs_ref.shape[1]
    f32, bf16 = jnp.float32, jnp.bfloat16
    q4t = _stack_heads_t(qnt_ref[...])
    pen = jnp.concatenate([pen_ref[t] for t in range(pen_ref.shape[0])], axis=1)
    rhs_sel = jnp.concatenate([q4t, jnp.concatenate([pen] * GROUP_HEADS, axis=1)], axis=0)

    def branch(rhs, k_ref, v_ref, tab_ref, first_tile, last_tile):
        n_tab = tab_ref.shape[0] - 1
        m_ref[...] = jnp.full(m_ref.shape, NEG, f32)
        acc_ref[...] = jnp.zeros(acc_ref.shape, f32)

        def key_offset(j):
            return pl.multiple_of(jnp.minimum(j, last_tile) * kt, kt)

        def scores(j):
            delta = jnp.where(j > last_tile, n_tab, jnp.minimum(i - j * (kt // tq), n_tab - 1))
            return jnp.dot(k_ref[pl.ds(key_offset(j), kt), :], rhs, preferred_element_type=f32) + tab_ref[delta]

        def absorb(s, j):
            m_old = m_ref[...]
            m_new = jnp.maximum(m_old, jnp.max(s, axis=0, keepdims=True))
            p = jnp.exp2(s - m_new)
            pv = jnp.dot(v_ref[:, pl.ds(key_offset(j), kt)], p.astype(bf16), preferred_element_type=f32)
            acc_ref[...] = jnp.exp2(m_old - m_new) * acc_ref[...] + pv
            m_ref[...] = m_new

        sa_ref[...] = scores(first_tile)

        def body(t, carry):
            j = first_tile + 2 * t
            sb_ref[...] = scores(j + 1)
            absorb(sa_ref[...], j)
            sa_ref[...] = scores(j + 2)
            absorb(sb_ref[...], j + 1)
            return carry

        lax.fori_loop(0, (last_tile - first_tile) // 2 + 1, body, 0)
        acc = acc_ref[...]
        return acc[0:HEAD_DIM] / jnp.maximum(acc[HEAD_DIM:HEAD_DIM + 1], 1e-30)

    def window_branch(rhs):
        span = tabw_ref.shape[1]
        back = span // tq - 1
        start = pl.multiple_of(jnp.maximum(i - back, 0) * tq, tq)
        s = jnp.dot(kkw_ref[pl.ds(start, span), :], rhs, preferred_element_type=f32) + tabw_ref[jnp.minimum(i, back)]
        p = jnp.exp2(s - jnp.max(s, axis=0, keepdims=True))
        acc = jnp.dot(v1w_ref[:, pl.ds(start, span)], p.astype(bf16), preferred_element_type=f32)
        return acc[0:HEAD_DIM] / jnp.maximum(acc[HEAD_DIM:HEAD_DIM + 1], 1e-30)

    last = (i * tq) // kt
    o_w = window_branch(q4t)
    o_s = branch(rhs_sel, kxs_ref, v1s_ref, tabs_ref, 0, last)
    gt = jax.nn.sigmoid(gate_ref[...]).T
    oct_ = ocmp_ref[...]
    parts = []
    for h in range(GROUP_HEADS):
        cols = slice(h * tq, (h + 1) * tq)
        parts.append(gt[3 * h:3 * h + 1] * oct_[h * HEAD_DIM:(h + 1) * HEAD_DIM]
                     + gt[3 * h + 1:3 * h + 2] * o_s[:, cols] + gt[3 * h + 2:3 * h + 3] * o_w[:, cols])
    yt = jnp.concatenate(parts, axis=0)
    yt = yt * lax.rsqrt(jnp.mean(yt * yt, axis=0, keepdims=True) + EPS) * mixg_ref[...]
    o_ref[...] = yt.T


def nsa_attention(qnt, pen, ocmp, pa, kxs, v1s, kkw, v1w, tab_s, tab_w, mixg):
    B, _, T = qnt.shape
    tq = min(NSA_QUERY_TILE, T)
    tile = lambda w, j: pl.BlockSpec((None, tq, w), lambda b, i: (b, i, j))
    whole = lambda w: pl.BlockSpec((None, T, w), lambda b, i: (b, 0, 0))
    whole_t = pl.BlockSpec((None, V1_ROWS, T), lambda b, i: (b, 0, 0))
    full = lambda a: pl.BlockSpec(a.shape, lambda b, i: (0,) * a.ndim)
    return pl.pallas_call(
        _nsa_attn_kernel,
        grid=(B, T // tq),
        in_specs=[pl.BlockSpec((None, 256, tq), lambda b, i: (b, 0, i)),
                  pl.BlockSpec((None, tq // Q_BLOCK, 128, Q_BLOCK), lambda b, i: (b, i, 0, 0)),
                  pl.BlockSpec((None, 256, tq), lambda b, i: (b, 0, i)), tile(128, 5),
                  whole(256), whole_t, whole(128), whole_t, full(tab_s), full(tab_w), full(mixg)],
        out_specs=tile(256, 0),
        out_shape=jax.ShapeDtypeStruct((B, T, 256), jnp.float32),
        scratch_shapes=[pltpu.VMEM((V1_ROWS, GROUP_HEADS * tq), jnp.float32),
                        pltpu.VMEM((1, GROUP_HEADS * tq), jnp.float32),
                        pltpu.VMEM((tab_s.shape[1], GROUP_HEADS * tq), jnp.float32),
                        pltpu.VMEM((tab_s.shape[1], GROUP_HEADS * tq), jnp.float32)],
        compiler_params=pltpu.CompilerParams(dimension_semantics=("arbitrary", "arbitrary"),
                                             vmem_limit_bytes=VMEM_LIMIT_BYTES),
        name="nsa_attention",
    )(qnt, pen, ocmp, pa, kxs, v1s, kkw, v1w, tab_s, tab_w, mixg)


def _toeplitz(wr, rows, cols):
    H, n = wr.shape
    flat = jnp.tile(jnp.pad(wr, ((0, 0), (0, 1))), (1, rows))[:, :rows * n]
    return flat.reshape(H, rows, n)[:, :, rows - 1:rows - 1 + cols]


def nsa_bias_tables(t5_table, T):
    H = GROUP_HEADS
    tq = Q_BLOCK
    ta = min(NSA_QUERY_TILE, T)
    kt = min(NSA_KEY_TILE, T)
    nrow = T // CMP_STRIDE
    dmax = WINDOW + 2 * ta + kt
    onehot = (t5_bucket(jnp.arange(dmax))[:, None] == jnp.arange(N_BUCKETS)[None, :]).astype(jnp.float32)
    by_dist = jnp.dot(onehot, t5_table, precision=HIGHEST).T
    neg = lambda n: jnp.full((H, n), NEG, jnp.float32)
    f_s = jnp.concatenate([neg(kt - 1), by_dist * LOG2E], axis=1)
    span = ta + kt - 1

    def tiles(f, n):
        out = [_toeplitz(f[:, k * ta:k * ta + span], kt, ta).transpose(1, 0, 2).reshape(kt, H * ta) for k in range(n)]
        return jnp.stack(out + [jnp.full((kt, H * ta), NEG, jnp.float32)])

    n_s = -(-(kt + MAX_DISTANCE) // ta) + 1
    tab_s = tiles(f_s, n_s)
    wspan = WINDOW + ta
    f_w = jnp.concatenate([neg(wspan - 1), by_dist[:, :WINDOW] * LOG2E, neg(wspan)], axis=1)
    tab_w = jnp.stack([_toeplitz(f_w[:, k * ta:k * ta + wspan + ta - 1], wspan, ta).transpose(1, 0, 2)
                       .reshape(wspan, H * ta) for k in range(WINDOW // ta + 1)])
    per_tile = tq // CMP_STRIDE
    d0 = -CMP_STRIDE * (nrow - 1) - (CMP_BLOCK - 1)
    n16 = 2 * nrow + per_tile - 1
    n_pos = CMP_STRIDE * n16 + d0
    far = jnp.broadcast_to(by_dist[:, -1:], (H, max(n_pos - dmax, 0)))
    w16 = jnp.concatenate([neg(-d0), by_dist[:, :n_pos] * LOG2E, far * LOG2E], axis=1).reshape(H, n16, CMP_STRIDE)
    w16 = w16[:, ::-1, :]
    base = jnp.stack([w16[:, per_tile - 1 - a:per_tile - 1 - a + 2 * nrow, :] for a in range(per_tile)], axis=1)
    cbias = base.transpose(0, 1, 3, 2).reshape(H * tq, 2 * nrow).T
    return tab_s, tab_w, cbias


def nsa_overlap_t(T):
    nrow = T // CMP_STRIDE
    nblk = T // SLC_BLOCK
    ci = np.arange(nrow)[None, :]
    bj = np.arange(nblk)[:, None]
    ov = (ci * CMP_STRIDE < (bj + 1) * SLC_BLOCK) & (ci * CMP_STRIDE + CMP_BLOCK > bj * SLC_BLOCK)
    ov = ov & (ci < nrow - 1)
    return jnp.asarray(ov.astype(np.float32), dtype=jnp.bfloat16)


def nsa_mixer_pallas(pa, q_norm_g, k_norm_g, cmp_pos, cmp_k_w1, cmp_k_w2, cmp_v_w1, cmp_v_w2,
                     tables, ovt, mixg):
    B, T, _ = pa.shape
    bf16 = jnp.bfloat16
    tab_s, tab_w, cbias = tables
    qg = (jnp.tile(q_norm_g, GROUP_HEADS) * HEAD_DIM ** -0.5)[None, :]
    kg2 = jnp.tile(k_norm_g, 2)[None, :]
    qnt, kxs, v1s, kkw, v1w = nsa_prep(pa, qg, kg2, _head_mask().astype(bf16))
    dup = lambda w: jnp.concatenate([w, w], axis=1)
    zeros = jnp.zeros((CMP_BLOCK, HEAD_DIM, cmp_k_w1.shape[1]), jnp.float32)
    per_pos = lambda w: w.reshape(CMP_BLOCK, HEAD_DIM, -1)
    w1 = jnp.concatenate([jnp.concatenate([per_pos(cmp_k_w1), zeros], axis=2),
                          jnp.concatenate([zeros, per_pos(cmp_v_w1)], axis=2)], axis=1).astype(bf16)
    kkc, vct = nsa_compress(pa, dup(cmp_pos), w1, dup(cmp_k_w2).astype(bf16), dup(cmp_v_w2).astype(bf16), kg2)
    n_sel = min(N_SLC, T // SLC_BLOCK)
    ocmp, pen = nsa_cmp_select(qnt, kkc, vct, cbias, ovt, n_sel)
    mixg_col = jnp.broadcast_to(mixg.reshape(-1, 1), (GROUP_WIDTH, min(NSA_QUERY_TILE, T)))
    return nsa_attention(qnt, pen, ocmp, pa, kxs, v1s, kkw, v1w, tab_s, tab_w, mixg_col)


CHUNKS_PER_STEP = 4
SEQS_PER_STEP = 2


def _tn_dot(a, b):
    return lax.dot_general(a, b, (((0,), (0,)), ((), ())), preferred_element_type=jnp.float32)


def _split_bf16(a, parts):
    out, rest = [], a
    for _ in range(parts):
        piece = rest.astype(jnp.bfloat16)
        out.append(piece)
        rest = rest - piece.astype(jnp.float32)
    return out


def _dot_sel(a, sel, parts=3):
    return sum(jnp.dot(p, sel, preferred_element_type=jnp.float32) for p in _split_bf16(a, parts))


def _sel_dot(sel, a, parts=3):
    return sum(jnp.dot(sel, p, preferred_element_type=jnp.float32) for p in _split_bf16(a, parts))


def _softplus(x):
    return jnp.maximum(x, 0.0) + jnp.log(1.0 + jnp.exp(-jnp.abs(x)))


def _block_diag(a, hmb):
    return jnp.concatenate([a.astype(jnp.bfloat16)] * GROUP_HEADS, axis=0) * hmb


def _diag_row(a, eye):
    return jnp.sum(a * eye, axis=0, keepdims=True)


def _head_norm(o, hmb):
    return o * lax.rsqrt(_dot_sel(o * o, hmb, parts=2) * (1.0 / HEAD_DIM) + EPS)


def _head_l2(o, hmb):
    return o * lax.rsqrt(_dot_sel(o * o, hmb, parts=2) + EPS)


def _xform_masks(L):
    i = np.arange(L)[:, None]
    j = np.arange(L)[None, :]
    masks = []
    s = 1
    while s < L:
        masks.append((i // (2 * s) == j // (2 * s)) & (i % (2 * s) >= s) & (j % (2 * s) < s))
        s *= 2
    masks += [i == j, j <= i, j < i]
    return jnp.asarray(np.tile(np.stack(masks).astype(np.float32), (1, 1, GROUP_HEADS)))


def _expand_matrix(first_lane):
    e = np.zeros((128, GROUP_WIDTH), np.float32)
    for h in range(GROUP_HEADS):
        e[first_lane + h, h * HEAD_DIM:(h + 1) * HEAD_DIM] = 1.0
    return jnp.asarray(e, jnp.bfloat16)


def _head_mask():
    return jnp.asarray(np.kron(np.eye(GROUP_HEADS), np.ones((HEAD_DIM, HEAD_DIM))), jnp.float32)


def _per_head_rows(*vecs):
    rows = [jnp.repeat(v, HEAD_DIM) if v.shape[0] == GROUP_HEADS else jnp.tile(v, GROUP_HEADS) for v in vecs]
    return jnp.pad(jnp.stack(rows), ((0, 8 - len(rows)), (0, 0)))


def _chunk_consts(L, n_chunks):
    tri = np.kron(np.eye(n_chunks), np.tril(np.ones((L, L)))).astype(np.float32)
    return _head_mask(), jnp.asarray(tri, jnp.bfloat16), _xform_masks(L)


def _const_spec(a):
    return pl.BlockSpec(a.shape, lambda b, t: (0,) * a.ndim)


def _gdn_kernel(pc_ref, gates_ref, hp_ref, eb_ref, ea_ref, hm_ref, tri_ref, lv_ref, o_ref, s_ref):
    L = GDN_CHUNK
    n_seq, rows = pc_ref.shape[0], pc_ref.shape[1]
    f32, bf16 = jnp.float32, jnp.bfloat16
    n_lev = lv_ref.shape[0] - 3

    @pl.when(pl.program_id(1) == 0)
    def _():
        s_ref[...] = jnp.zeros(s_ref.shape, f32)

    hm = hm_ref[...]
    hmb = hm.astype(bf16)
    eye, incl, strict = lv_ref[n_lev], lv_ref[n_lev + 1], lv_ref[n_lev + 2]

    q_c, k_c, v_c, beta_c, gc_c = [], [], [], [], []
    for b in range(n_seq):
        qkv = pc_ref[b, :, 0:768]
        q_all = _head_l2(qkv[:, 0:256], hmb) * HEAD_DIM ** -0.5
        k_all = _head_l2(qkv[:, 256:512], hmb)
        tail = gates_ref[b]
        beta_all = jax.nn.sigmoid(_dot_sel(tail, eb_ref[...]))
        g = hp_ref[0:1, :] * _softplus(_dot_sel(tail, ea_ref[...]) + hp_ref[1:2, :])
        gc_all = _sel_dot(tri_ref[...], g)
        for c in range(rows // L):
            sl = slice(c * L, (c + 1) * L)
            q_c.append(q_all[sl])
            k_c.append(k_all[sl])
            v_c.append(qkv[sl, 512:768])
            beta_c.append(beta_all[sl])
            gc_c.append(gc_all[sl])
    chains = range(len(q_c))
    egc = [jnp.exp(gc) for gc in gc_c]
    kb = [k_c[i] * beta_c[i] for i in chains]
    a_mat, qk = [], []
    for i in chains:
        seg = jnp.exp(jnp.where(incl > 0.5, gc_c[i] - _diag_row(gc_c[i], eye), NEG))
        k_bd = _block_diag(k_c[i], hmb)
        a_mat.append(_nt_dot(kb[i].astype(bf16), k_bd) * seg * strict)
        qk.append((_nt_dot(q_c[i].astype(bf16), k_bd) * seg).astype(bf16))
    t_inv = [eye - a * lv_ref[0] for a in a_mat]
    for lev in range(1, n_lev):
        te = [jnp.dot(t_inv[i].astype(bf16), _block_diag(a_mat[i] * lv_ref[lev], hmb),
                      preferred_element_type=f32) for i in chains]
        t_inv = [t_inv[i] - jnp.dot(te[i].astype(bf16), _block_diag(t_inv[i], hmb), preferred_element_type=f32)
                 for i in chains]
    parts = []
    for i in chains:
        tb = t_inv[i].astype(bf16)
        u = jnp.dot(tb, _block_diag(v_c[i] * beta_c[i], hmb), preferred_element_type=f32)
        w = jnp.dot(tb, _block_diag(kb[i] * egc[i], hmb), preferred_element_type=f32)
        k_dec = k_c[i] * jnp.exp(gc_c[i][L - 1:L, :] - gc_c[i])
        parts.append((u, w.astype(bf16), (q_c[i] * egc[i]).astype(bf16), qk[i], k_dec.astype(bf16),
                      egc[i][L - 1:L, :]))
    per_seq = rows // L
    state = [s_ref[b] for b in range(n_seq)]
    outs = [[] for _ in range(n_seq)]
    for c in range(per_seq):
        for b in range(n_seq):
            u, w, q_dec, qk_c, k_dec, g_tot = parts[b * per_seq + c]
            sb = state[b].astype(bf16)
            v_new = u - jnp.dot(w, sb, preferred_element_type=f32)
            outs[b].append(jnp.dot(q_dec, sb, preferred_element_type=f32)
                           + jnp.dot(qk_c, _block_diag(v_new, hmb), preferred_element_type=f32))
            state[b] = state[b] * g_tot + _tn_dot(k_dec, v_new.astype(bf16)) * hm
    for b in range(n_seq):
        s_ref[b] = state[b]
        o = jnp.concatenate(outs[b], axis=0)
        z = pc_ref[b, :, 768:1024]
        o_ref[b] = _head_norm(o, hmb) * hp_ref[2:3, :] * (z * jax.nn.sigmoid(z))


def gated_deltanet_pallas(pc, pa, A_log, dt_bias, norm_g):
    B, T, width = pc.shape
    L = GDN_CHUNK
    rows = L * min(CHUNKS_PER_STEP, T // L)
    n_seq = SEQS_PER_STEP if B % SEQS_PER_STEP == 0 else 1
    n = GROUP_HEADS * HEAD_DIM
    hp = _per_head_rows(-jnp.exp(A_log), dt_bias, norm_g)
    consts = (hp, _expand_matrix(GATE_LANE["c_beta"]), _expand_matrix(GATE_LANE["c_alpha"])) + _chunk_consts(L, rows // L)
    return pl.pallas_call(
        _gdn_kernel,
        grid=(B // n_seq, T // rows),
        in_specs=[pl.BlockSpec((n_seq, rows, width), lambda b, t: (b, t, 0)),
                  pl.BlockSpec((n_seq, rows, 128), lambda b, t: (b, t, GATE_BLOCK))]
                 + [_const_spec(a) for a in consts],
        out_specs=pl.BlockSpec((n_seq, rows, 256), lambda b, t: (b, t, 0)),
        out_shape=jax.ShapeDtypeStruct((B, T, 256), jnp.float32),
        scratch_shapes=[pltpu.VMEM((n_seq, n, n), jnp.float32)],
        compiler_params=pltpu.CompilerParams(dimension_semantics=("arbitrary", "arbitrary"),
                                             vmem_limit_bytes=VMEM_LIMIT_BYTES),
        name="gated_deltanet",
    )(pc, pa, *consts)


def _mlstm_kernel(pd_ref, gates_ref, hp_ref, ei_ref, ef_ref, hm_ref, tri_ref, lv_ref, o_ref, c_ref, n_ref, m_ref):
    L = MLSTM_CHUNK
    n_seq, rows = pd_ref.shape[0], pd_ref.shape[1]
    f32, bf16 = jnp.float32, jnp.bfloat16
    n_lev = lv_ref.shape[0] - 3

    @pl.when(pl.program_id(1) == 0)
    def _():
        c_ref[...] = jnp.zeros(c_ref.shape, f32)
        n_ref[...] = jnp.zeros(n_ref.shape, f32)
        m_ref[...] = jnp.zeros(m_ref.shape, f32)

    hm = hm_ref[...]
    hmb = hm.astype(bf16)
    eye, incl = lv_ref[n_lev], lv_ref[n_lev + 1]
    head_of_lane = lax.broadcasted_iota(jnp.int32, (L, GROUP_WIDTH), 1) // HEAD_DIM

    def head_max(a):
        out = jnp.zeros(a.shape, f32)
        for h in range(GROUP_HEADS):
            mine = head_of_lane == h
            out = jnp.where(mine, jnp.max(jnp.where(mine, a, NEG), axis=-1, keepdims=True), out)
        return out

    per_seq = rows // L
    parts = []
    for b in range(n_seq):
        tail = gates_ref[b]
        log_i_all = _dot_sel(tail, ei_ref[...])
        log_f = -_softplus(-(_dot_sel(tail, ef_ref[...]) + hp_ref[0:1, :]))
        b_all = _sel_dot(tri_ref[...], log_f)
        for c in range(per_seq):
            sl = slice(c * L, (c + 1) * L)
            q = pd_ref[b, sl, 0:256]
            k = pd_ref[b, sl, 256:512] * HEAD_DIM ** -0.5
            v = pd_ref[b, sl, 512:768]
            bc, log_i = b_all[sl], log_i_all[sl]
            log_w = jnp.where(incl > 0.5, bc - _diag_row(bc - log_i, eye), NEG)
            b_last = bc[L - 1:L, :]
            log_w_end = b_last - bc + log_i
            qk = _nt_dot(q.astype(bf16), _block_diag(k, hmb))
            parts.append((q, k, v, bc, log_w, head_max(log_w), b_last, log_w_end,
                          jnp.max(log_w_end, axis=0, keepdims=True), qk))
    c_state = [c_ref[b] for b in range(n_seq)]
    n_state = [n_ref[b] for b in range(n_seq)]
    m_prev = [m_ref[b] for b in range(n_seq)]
    outs = [[] for _ in range(n_seq)]
    for c in range(per_seq):
        for b in range(n_seq):
            q, k, v, bc, log_w, m_intra, b_last, log_w_end, m_end, qk = parts[b * per_seq + c]
            log_inter = bc + m_prev[b]
            m_t = jnp.maximum(log_inter, m_intra)
            w_inter = jnp.exp(log_inter - m_t)
            s = qk * jnp.exp(log_w - m_t)
            num = w_inter * jnp.dot(q.astype(bf16), c_state[b].astype(bf16), preferred_element_type=f32)
            num = num + jnp.dot(s.astype(bf16), _block_diag(v, hmb), preferred_element_type=f32)
            den = w_inter * _dot_sel(q * n_state[b], hmb, parts=2) + _dot_sel(s, hmb, parts=2)
            outs[b].append(num / jnp.maximum(jnp.abs(den), jnp.exp(-m_t)))
            m_new = jnp.maximum(b_last + m_prev[b], m_end)
            w_old = jnp.exp(b_last + m_prev[b] - m_new)
            kw = k * jnp.exp(log_w_end - m_new)
            c_state[b] = w_old * c_state[b] + _tn_dot(kw.astype(bf16), v.astype(bf16)) * hm
            n_state[b] = w_old * n_state[b] + jnp.sum(kw, axis=0, keepdims=True)
            m_prev[b] = m_new
    for b in range(n_seq):
        c_ref[b] = c_state[b]
        n_ref[b] = n_state[b]
        m_ref[b] = m_prev[b]
        h = jnp.concatenate(outs[b], axis=0)
        o_ref[b] = _head_norm(h, hmb) * hp_ref[1:2, :] * jax.nn.sigmoid(pd_ref[b, :, 768:1024])


def mlstm_pallas(pd, pa, f_bias, norm_g):
    B, T, width = pd.shape
    L = MLSTM_CHUNK
    rows = L * min(CHUNKS_PER_STEP, T // L)
    n_seq = SEQS_PER_STEP if B % SEQS_PER_STEP == 0 else 1
    n = GROUP_HEADS * HEAD_DIM
    consts = ((_per_head_rows(f_bias, norm_g), _expand_matrix(GATE_LANE["d_i"]), _expand_matrix(GATE_LANE["d_f"]))
              + _chunk_consts(L, rows // L))
    return pl.pallas_call(
        _mlstm_kernel,
        grid=(B // n_seq, T // rows),
        in_specs=[pl.BlockSpec((n_seq, rows, width), lambda b, t: (b, t, 0)),
                  pl.BlockSpec((n_seq, rows, 128), lambda b, t: (b, t, GATE_BLOCK))]
                 + [_const_spec(a) for a in consts],
        out_specs=pl.BlockSpec((n_seq, rows, 256), lambda b, t: (b, t, 0)),
        out_shape=jax.ShapeDtypeStruct((B, T, 256), jnp.float32),
        scratch_shapes=[pltpu.VMEM((n_seq, n, n), jnp.float32), pltpu.VMEM((n_seq, 1, 256), jnp.float32),
                        pltpu.VMEM((n_seq, 1, 256), jnp.float32)],
        compiler_params=pltpu.CompilerParams(dimension_semantics=("arbitrary", "arbitrary"),
                                             vmem_limit_bytes=VMEM_LIMIT_BYTES),
        name="mlstm",
    )(pd, pa, *consts)


def kernel(x, c, ada_w, ada_b, norm_g, ffn1_w13, ffn1_w2, ffn2_w13, ffn2_w2, w_in, b_in, q_norm_g, k_norm_g, cmp_pos, cmp_k_w1, cmp_k_w2, cmp_v_w1, cmp_v_w2, t5_table, sc_conv_w, gdn_conv_w, gdn_A_log, gdn_dt_bias, gdn_norm_g, mlstm_f_bias, mlstm_norm_g, mix_norm_g, w_out):
    B, T, D = x.shape
    depth = ada_w.shape[0]
    bf16 = jnp.bfloat16
    x2 = x.reshape(B * T, D)
    tables = nsa_bias_tables(t5_table, T)
    ovt = nsa_overlap_t(T)
    ffn_w = [w.astype(bf16) for w in (ffn1_w13, ffn1_w2, ffn2_w13, ffn2_w2, w_out)]
    for l in range(depth):
        mod = ada_modulation(c, ada_w, ada_b[l][None, :], l)
        mod = mod.reshape(B, N_SUBLAYERS, 3, 1, D)
        x2 = ffn_half_step(x2, norm_g[l, 0][None, :], mod[:, 0, 0], mod[:, 0, 1], mod[:, 0, 2],
                           ffn_w[0], ffn_w[1], l, T)
        ws, bs = split_in_weights(w_in[l], b_in[l])
        pa, y_b, pc, pd = in_projection(x2, norm_g[l, 1][None, :], mod[:, 1, 0], mod[:, 1, 1], ws, bs,
                                        sc_conv_w[l], mix_norm_g[l, 1][None, :], gdn_conv_w[l], T)
        pa, pc, pd = (a.reshape(B, T, -1) for a in (pa, pc, pd))
        y_a = nsa_mixer_pallas(pa, q_norm_g[l], k_norm_g[l], cmp_pos[l], cmp_k_w1[l], cmp_k_w2[l],
                               cmp_v_w1[l], cmp_v_w2[l], tables, ovt, mix_norm_g[l, 0][None, :])
        y_c = gated_deltanet_pallas(pc, pa, gdn_A_log[l], gdn_dt_bias[l], gdn_norm_g[l])
        y_d = mlstm_pallas(pd, pa, mlstm_f_bias[l], mlstm_norm_g[l])
        ys = [y.reshape(B * T, GROUP_WIDTH) for y in (y_a, y_b, y_c, y_d)]
        x2 = ffn_half_step(x2, norm_g[l, 2][None, :], mod[:, 2, 0], mod[:, 2, 1], mod[:, 2, 2],
                           ffn_w[2], ffn_w[3], l, T, mix=(ys, mod[:, 1, 2], ffn_w[4]))
    return x2.reshape(B, T, D)
```

```python
import functools
import math

import jax
import jax.numpy as jnp
import numpy as np
from jax import lax
from jax.experimental import pallas as pl
from jax.experimental.pallas import tpu as pltpu

HEAD_DIM = 64
GROUP_HEADS = 4
GROUP_WIDTH = GROUP_HEADS * HEAD_DIM
CMP_STRIDE = 16
CMP_BLOCK = 32
SLC_BLOCK = 64
N_SLC = 16
N_LOCAL_SLC = 2
WINDOW = 512
Q_BLOCK = 128
FORCE = 1e6
N_BUCKETS = 32
MAX_DISTANCE = 128
GDN_CHUNK = 64
MLSTM_CHUNK = 64
N_SUBLAYERS = 3
EPS = 1e-6

IN_LAYOUT = (
    ("a_q", 256), ("a_k_cmp", 64), ("a_v_cmp", 64),
    ("a_k_slc", 64), ("a_v_slc", 64), ("a_k_win", 64), ("a_v_win", 64),
    ("a_gate", 12),
    ("b_b", 256), ("b_c", 256), ("b_x", 256),
    ("c_q", 256), ("c_k", 256), ("c_v", 256),
    ("c_beta", 4), ("c_alpha", 4), ("c_z", 256),
    ("d_q", 256), ("d_k", 256), ("d_v", 256),
    ("d_i", 4), ("d_f", 4), ("d_o", 256),
)

VMEM_LIMIT_BYTES = 56 * 1024 * 1024
FFN_TOKEN_TILE = 512
FFN_CHUNK = 256
PROJ_TOKEN_TILE = 512
NEG = -1e30
HIGHEST = lax.Precision.HIGHEST
LOG2E = math.log2(math.e)
V1_ROWS = HEAD_DIM + 16


def _modulated_norm(x, g, scale, shift):
    y = x * lax.rsqrt(jnp.mean(x * x, axis=-1, keepdims=True) + EPS)
    return (y * g) * (1.0 + scale) + shift


def _ada_kernel(c_ref, w_ref, b_ref, o_ref):
    c = c_ref[...]
    cond = c * jax.nn.sigmoid(c)
    o_ref[...] = jnp.dot(cond.astype(jnp.bfloat16), w_ref[...].astype(jnp.bfloat16),
                         preferred_element_type=jnp.float32) + b_ref[...]


def ada_modulation(c, w, b, layer):
    B, D = c.shape
    N = w.shape[2]
    tn = 1152
    return pl.pallas_call(
        _ada_kernel,
        grid=(N // tn,),
        in_specs=[pl.BlockSpec((B, D), lambda j: (0, 0)),
                  pl.BlockSpec((None, D, tn), lambda j: (layer, 0, j)),
                  pl.BlockSpec((1, tn), lambda j: (0, j))],
        out_specs=pl.BlockSpec((B, tn), lambda j: (0, j)),
        out_shape=jax.ShapeDtypeStruct((B, N), jnp.float32),
        name="ada_modulation",
    )(c, w, b)


def _ffn_kernel(x_ref, g_ref, shift_ref, scale_ref, gate_ref, w1_ref, w3_ref, w2_ref, *rest, n_mix):
    o_ref, act_ref = rest[-2:]
    x = x_ref[...]
    if n_mix:
        y_refs, mix_gate_ref, wo_ref = rest[:n_mix], rest[n_mix], rest[n_mix + 1]
        z = None
        for k, y_ref in enumerate(y_refs):
            part = jnp.dot(y_ref[...].astype(jnp.bfloat16), wo_ref[k * GROUP_WIDTH:(k + 1) * GROUP_WIDTH, :],
                           preferred_element_type=jnp.float32)
            z = part if z is None else z + part
        x = x + mix_gate_ref[0] * z
    h = _modulated_norm(x, g_ref[...], scale_ref[0], shift_ref[0]).astype(jnp.bfloat16)
    n_chunks = act_ref.shape[1] // FFN_CHUNK
    for ci in range(n_chunks):
        cs = slice(ci * FFN_CHUNK, (ci + 1) * FFN_CHUNK)
        a = jnp.dot(h, w1_ref[:, cs], preferred_element_type=jnp.float32)
        b = jnp.dot(h, w3_ref[:, cs], preferred_element_type=jnp.float32)
        act_ref[:, cs] = (a * jax.nn.sigmoid(a) * b).astype(jnp.bfloat16)
    y = jnp.dot(act_ref[...], w2_ref[...], preferred_element_type=jnp.float32)
    o_ref[...] = x + (0.5 * gate_ref[0]) * y


def ffn_half_step(x2, g, shift, scale, gate, w13, w2, layer, tokens_per_batch, mix=None):
    M, D = x2.shape
    F = w2.shape[1]
    tm = FFN_TOKEN_TILE
    tiles_per_batch = tokens_per_batch // tm
    resident = dict(pipeline_mode=pl.Buffered(1))
    mod_spec = pl.BlockSpec((1, 1, D), lambda i: (i // tiles_per_batch, 0, 0))
    ys, mix_gate, w_out = mix if mix is not None else ((), None, None)
    mix_args = tuple(ys) + ((mix_gate, w_out) if ys else ())
    mix_specs = [pl.BlockSpec((tm, GROUP_WIDTH), lambda i: (i, 0)) for _ in ys]
    if ys:
        mix_specs += [mod_spec, pl.BlockSpec((None,) + w_out.shape[1:], lambda i: (layer, 0, 0), **resident)]
    return pl.pallas_call(
        functools.partial(_ffn_kernel, n_mix=len(ys)),
        grid=(M // tm,),
        in_specs=[pl.BlockSpec((tm, D), lambda i: (i, 0)),
                  pl.BlockSpec((1, D), lambda i: (0, 0)),
                  mod_spec, mod_spec, mod_spec,
                  pl.BlockSpec((None, D, F), lambda i: (layer, 0, 0), **resident),
                  pl.BlockSpec((None, D, F), lambda i: (layer, 0, 1), **resident),
                  pl.BlockSpec((None, F, D), lambda i: (layer, 0, 0), **resident)] + mix_specs,
        out_specs=pl.BlockSpec((tm, D), lambda i: (i, 0)),
        out_shape=jax.ShapeDtypeStruct((M, D), jnp.float32),
        scratch_shapes=[pltpu.VMEM((tm, F), jnp.bfloat16)],
        compiler_params=pltpu.CompilerParams(dimension_semantics=("arbitrary",),
                                             vmem_limit_bytes=VMEM_LIMIT_BYTES),
        name="ffn_half_step",
    )(x2, g, shift, scale, gate, w13, w13, w2, *mix_args)


CONV_CARRY_ROWS = 8
CONV_ROW_CHUNK = 128


def _in_proj_kernel(x_ref, g_ref, shift_ref, scale_ref, wa_ref, wb_ref, wc_ref, wd_ref,
                    ba_ref, bb_ref, bc_ref, bd_ref, scw_ref, scg_ref, gcw_ref,
                    oa_ref, yb_ref, oc_ref, od_ref, braw_ref, craw_ref, *, tiles_per_batch):
    f32 = jnp.float32
    rows = x_ref.shape[0]
    hist = CONV_CARRY_ROWS
    first = pl.program_id(0) % tiles_per_batch == 0

    @pl.when(first)
    def _():
        braw_ref[0:hist, :] = jnp.zeros((hist, braw_ref.shape[1]), f32)
        craw_ref[0:hist, :] = jnp.zeros((hist, craw_ref.shape[1]), f32)

    @pl.when(jnp.logical_not(first))
    def _():
        braw_ref[0:hist, :] = braw_ref[rows:rows + hist, :]
        craw_ref[0:hist, :] = craw_ref[rows:rows + hist, :]

    h = _modulated_norm(x_ref[...], g_ref[...], scale_ref[0], shift_ref[0]).astype(jnp.bfloat16)
    project = lambda w_ref, b_ref: jnp.dot(h, w_ref[...], preferred_element_type=f32) + b_ref[...]
    pb = project(wb_ref, bb_ref)
    yb_ref[...] = pb[:, 0:256]
    braw_ref[hist:hist + rows, :] = pb[:, 256:512] * pb[:, 512:768]
    pc = project(wc_ref, bc_ref)
    craw_ref[hist:hist + rows, :] = pc[:, 0:768]
    oc_ref[:, 768:] = pc[:, 768:]
    n_chunks = rows // CONV_ROW_CHUNK
    pieces = []
    for w_ref, b_ref, o_ref in ((wa_ref, ba_ref, oa_ref), (wd_ref, bd_ref, od_ref)):
        half = w_ref.shape[1] // 256 // 2 * 256
        pieces += [(w_ref, b_ref, o_ref, 0, half), (w_ref, b_ref, o_ref, half, w_ref.shape[1])]
    for ci, r0 in enumerate(range(0, rows, CONV_ROW_CHUNK)):
        for w_ref, b_ref, o_ref, c0, c1 in pieces[ci * len(pieces) // n_chunks:(ci + 1) * len(pieces) // n_chunks]:
            o_ref[:, c0:c1] = jnp.dot(h, w_ref[:, c0:c1], preferred_element_type=f32) + b_ref[:, c0:c1]
        out_rows = slice(r0, r0 + CONV_ROW_CHUNK)
        tap = lambda ref, s: ref[pl.ds(hist + r0 - s, CONV_ROW_CHUNK), :]
        conv = sum(tap(braw_ref, s) * scw_ref[2 - s:3 - s, :] for s in range(3))
        y = yb_ref[out_rows, :] * conv
        yb_ref[out_rows, :] = y * lax.rsqrt(jnp.mean(y * y, axis=-1, keepdims=True) + EPS) * scg_ref[...]
        for c0 in range(0, 3 * GROUP_WIDTH, GROUP_WIDTH):
            cols = slice(c0, c0 + GROUP_WIDTH)
            conv = sum(craw_ref[pl.ds(hist + r0 - s, CONV_ROW_CHUNK), cols] * gcw_ref[3 - s:4 - s, cols]
                       for s in range(4))
            oc_ref[out_rows, cols] = conv * jax.nn.sigmoid(conv)


def in_projection(x2, g, shift, scale, ws, bs, sc_conv_w, sc_norm_g, gdn_conv_w, tokens_per_batch):
    M, D = x2.shape
    tm = PROJ_TOKEN_TILE
    tiles_per_batch = tokens_per_batch // tm
    mod_spec = pl.BlockSpec((1, 1, D), lambda i: (i // tiles_per_batch, 0, 0))
    const = lambda a, **kw: pl.BlockSpec(a.shape, lambda i: (0, 0), **kw)
    pad8 = lambda w: jnp.pad(w, ((0, 8 - w.shape[0]), (0, 0)))
    extras = (pad8(sc_conv_w), sc_norm_g, pad8(gdn_conv_w))
    widths = (ws[0].shape[1], GROUP_WIDTH, ws[2].shape[1], ws[3].shape[1])
    return pl.pallas_call(
        functools.partial(_in_proj_kernel, tiles_per_batch=tiles_per_batch),
        grid=(M // tm,),
        in_specs=[pl.BlockSpec((tm, D), lambda i: (i, 0)), const(g), mod_spec, mod_spec]
                 + [const(w, pipeline_mode=pl.Buffered(1)) for w in ws] + [const(b) for b in bs]
                 + [const(e) for e in extras],
        out_specs=[pl.BlockSpec((tm, n), lambda i: (i, 0)) for n in widths],
        out_shape=[jax.ShapeDtypeStruct((M, n), jnp.float32) for n in widths],
        scratch_shapes=[pltpu.VMEM((CONV_CARRY_ROWS + tm, GROUP_WIDTH), jnp.float32),
                        pltpu.VMEM((CONV_CARRY_ROWS + tm, 3 * GROUP_WIDTH), jnp.float32)],
        compiler_params=pltpu.CompilerParams(dimension_semantics=("arbitrary",),
                                             vmem_limit_bytes=VMEM_LIMIT_BYTES),
        name="in_projection",
    )(x2, g, shift, scale, *ws, *bs, *extras)


def _group_columns():
    offs, o = {}, 0
    for n, s in IN_LAYOUT:
        offs[n] = np.arange(o, o + s)
        o += s
    cat = lambda names: np.concatenate([offs[n] for n in names])
    return (cat(["a_q", "a_k_cmp", "a_v_cmp", "a_k_slc", "a_v_slc", "a_k_win", "a_v_win"] + list(SMALL_GATES)),
            cat(["b_b", "b_c", "b_x"]),
            cat(["c_q", "c_k", "c_v", "c_z"]),
            cat(["d_q", "d_k", "d_v", "d_o"]))


SMALL_GATES = ("a_gate", "c_beta", "c_alpha", "d_i", "d_f")
GATE_BLOCK = 5
GATE_LANE = {"a_gate": 0, "c_beta": 12, "c_alpha": 16, "d_i": 20, "d_f": 24}
GROUP_SLAB_WIDTH = (768, 768, 1024, 1024)


def _take_columns(a, idx, width):
    cuts = [0] + [k + 1 for k in range(idx.size - 1) if idx[k + 1] != idx[k] + 1] + [idx.size]
    runs = [a[..., int(idx[s]):int(idx[e - 1]) + 1] for s, e in zip(cuts[:-1], cuts[1:])]
    return jnp.concatenate(runs + [jnp.zeros(a.shape[:-1] + (width - idx.size,), a.dtype)], axis=-1)


def split_in_weights(w, b):
    ws, bs = [], []
    for idx, width in zip(_group_columns(), GROUP_SLAB_WIDTH):
        ws.append(_take_columns(w, idx, width).astype(jnp.bfloat16))
        bs.append(_take_columns(b, idx, width)[None, :])
    return ws, bs


def t5_bucket(dist):
    n = jnp.maximum(dist, 0)
    max_exact = N_BUCKETS // 2
    nf = jnp.maximum(n, 1).astype(jnp.float32)
    large = max_exact + (jnp.log(nf / max_exact) / math.log(MAX_DISTANCE / max_exact)
                         * (N_BUCKETS - max_exact)).astype(jnp.int32)
    large = jnp.minimum(large, N_BUCKETS - 1)
    return jnp.where(n < max_exact, n, large)


def _nt_dot(a, b):
    return lax.dot_general(a, b, (((1,), (1,)), ((), ())), preferred_element_type=jnp.float32)


def _nsa_prep_kernel(q_ref, kvs_ref, kvw_ref, qg_ref, kg_ref, seg_ref,
                     qnt_ref, kxs_ref, v1s_ref, kkw_ref, v1w_ref):
    tm = q_ref.shape[0]
    q = q_ref[...]
    ss = _dot_sel(q * q, seg_ref[...], parts=3)
    qn = q * lax.rsqrt(ss * (1.0 / HEAD_DIM) + EPS) * qg_ref[...]
    qnt_ref[...] = (qn * LOG2E).T.astype(qnt_ref.dtype)
    lane = lax.broadcasted_iota(jnp.int32, (tm, 128), 1)
    lo = lane < HEAD_DIM

    def split(x):
        xr = pltpu.roll(x, HEAD_DIM, 1)
        ss = jnp.sum(jnp.where(lo, x * x, 0.0), axis=-1, keepdims=True)
        kk = jnp.where(lo, x, xr) * lax.rsqrt(ss * (1.0 / HEAD_DIM) + EPS) * kg_ref[...]
        return kk, jnp.where(lo, xr, 1.0)

    kk, v1 = split(kvs_ref[...])
    tok = pl.program_id(1) * tm + lax.broadcasted_iota(jnp.int32, (tm, 128), 0)
    onehot = jnp.where(lane == tok // SLC_BLOCK, 1.0, 0.0)
    kxs_ref[...] = jnp.concatenate([kk, onehot], axis=1).astype(kxs_ref.dtype)
    v1s_ref[...] = v1.T[0:V1_ROWS].astype(v1s_ref.dtype)
    kk, v1 = split(kvw_ref[...])
    kkw_ref[...] = kk.astype(kkw_ref.dtype)
    v1w_ref[...] = v1.T[0:V1_ROWS].astype(v1w_ref.dtype)


def nsa_prep(pa, qg, kg2, seg):
    B, T, _ = pa.shape
    tm = 512
    bf16 = jnp.bfloat16
    col = lambda w, j: pl.BlockSpec((None, tm, w), lambda b, i: (b, i, j))
    rowblk = lambda h: pl.BlockSpec((None, h, tm), lambda b, i: (b, 0, i))
    const = lambda shape: pl.BlockSpec(shape, lambda b, i: (0, 0))
    return pl.pallas_call(
        _nsa_prep_kernel,
        grid=(B, T // tm),
        in_specs=[col(256, 0), col(128, 3), col(128, 4), const((1, 256)), const((1, 128)), const((256, 256))],
        out_specs=[rowblk(256), col(256, 0), rowblk(V1_ROWS), col(128, 0), rowblk(V1_ROWS)],
        out_shape=[jax.ShapeDtypeStruct((B, 256, T), bf16),
                   jax.ShapeDtypeStruct((B, T, 256), bf16), jax.ShapeDtypeStruct((B, V1_ROWS, T), bf16),
                   jax.ShapeDtypeStruct((B, T, 128), bf16), jax.ShapeDtypeStruct((B, V1_ROWS, T), bf16)],
        name="nsa_prep",
    )(pa, pa, pa, qg, kg2, seg)


def _compress_kernel(kv_ref, pos_ref, w1_ref, w2k_ref, w2v_ref, kg_ref, kkc_ref, vct_ref):
    nrow = kkc_ref.shape[0]
    f32, bf16 = jnp.float32, jnp.bfloat16
    hidden = w1_ref.shape[2] // 2
    first = jnp.zeros((nrow, 2 * hidden), f32)
    second = jnp.zeros((nrow, 2 * hidden), f32)
    for p in range(CMP_STRIDE):
        xp = kv_ref[pl.ds(p, nrow, stride=CMP_STRIDE), :]
        first = first + jnp.dot((xp + pos_ref[p:p + 1, :]).astype(bf16), w1_ref[p], preferred_element_type=f32)
        second = second + jnp.dot((xp + pos_ref[CMP_STRIDE + p:CMP_STRIDE + p + 1, :]).astype(bf16),
                                  w1_ref[CMP_STRIDE + p], preferred_element_type=f32)
    hid = first + pltpu.roll(second, nrow - 1, 0)
    hid = (hid * jax.nn.sigmoid(hid)).astype(bf16)
    valid = lax.broadcasted_iota(jnp.int32, (nrow, 128), 0) < nrow - 1
    kc = jnp.dot(hid[:, 0:hidden], w2k_ref[...], preferred_element_type=f32)
    kc = kc * lax.rsqrt(jnp.mean(kc * kc, axis=-1, keepdims=True) + EPS) * kg_ref[...]
    kkc_ref[...] = jnp.where(valid, kc, 0.0).astype(kkc_ref.dtype)
    vc = jnp.dot(hid[:, hidden:2 * hidden], w2v_ref[...], preferred_element_type=f32)
    vct_ref[...] = jnp.where(valid, vc, 0.0).T[0:HEAD_DIM].astype(vct_ref.dtype)


def nsa_compress(pa, pos2, w1, w2k, w2v, kg2):
    B, T, _ = pa.shape
    nrow = T // CMP_STRIDE
    bf16 = jnp.bfloat16
    full = lambda a: pl.BlockSpec(a.shape, lambda b: (0,) * a.ndim)
    return pl.pallas_call(
        _compress_kernel,
        grid=(B,),
        in_specs=[pl.BlockSpec((None, T, 128), lambda b: (b, 0, 2)), full(pos2), full(w1), full(w2k), full(w2v),
                  full(kg2)],
        out_specs=[pl.BlockSpec((None, nrow, 128), lambda b: (b, 0, 0)),
                   pl.BlockSpec((None, HEAD_DIM, nrow), lambda b: (b, 0, 0))],
        out_shape=[jax.ShapeDtypeStruct((B, nrow, 128), bf16), jax.ShapeDtypeStruct((B, HEAD_DIM, nrow), bf16)],
        name="nsa_compress",
    )(pa, pos2, w1, w2k, w2v, kg2)


CMP_TILES_PER_STEP = 4


def _stack_heads_t(qt):
    lo = lax.broadcasted_iota(jnp.int32, (128, qt.shape[1]), 0) < HEAD_DIM
    zero = jnp.zeros((), qt.dtype)
    return jnp.concatenate([jnp.where(lo if h % 2 == 0 else ~lo, qt[128 * (h // 2):128 * (h // 2) + 128], zero)
                            for h in range(GROUP_HEADS)], axis=1)


def _cmp_select_kernel(qnt_ref, kkc_ref, vct_ref, cb_ref, ov_ref, ocmp_ref, pen_ref, sc_ref, *, n_sel):
    i = pl.program_id(1)
    n_tiles = pen_ref.shape[0]
    tq = qnt_ref.shape[1] // n_tiles
    nblk, nrow = ov_ref.shape
    keys_per_tile = tq // CMP_STRIDE
    f32, bf16 = jnp.float32, jnp.bfloat16
    blk = lax.broadcasted_iota(jnp.int32, (nblk, tq), 0)
    scores = []
    for g in range(n_tiles):
        cols = slice(g * tq, (g + 1) * tq)
        q4t = _stack_heads_t(qnt_ref[:, cols])
        first_row = pl.multiple_of(nrow - keys_per_tile * (i * n_tiles + g), keys_per_tile)
        s = jnp.dot(kkc_ref[...], q4t, preferred_element_type=f32) + cb_ref[pl.ds(first_row, nrow), :]
        m = jnp.max(s, axis=0, keepdims=True)
        e = jnp.where(s > 0.5 * NEG, jnp.exp2(s - m), 0.0)
        p = e * (1.0 / jnp.maximum(jnp.sum(e, axis=0, keepdims=True), 1e-30))
        o = jnp.dot(vct_ref[...], p.astype(bf16), preferred_element_type=f32)
        for h in range(GROUP_HEADS):
            ocmp_ref[h * HEAD_DIM:(h + 1) * HEAD_DIM, cols] = o[:, h * tq:(h + 1) * tq]
        psum = p[:, 0:tq] + p[:, tq:2 * tq] + p[:, 2 * tq:3 * tq] + p[:, 3 * tq:4 * tq]
        hi = psum.astype(bf16)
        lo_part = (psum - hi.astype(f32)).astype(bf16)
        score = (jnp.dot(ov_ref[...], hi, preferred_element_type=f32)
                 + jnp.dot(ov_ref[...], lo_part, preferred_element_type=f32))
        qpos = (i * n_tiles + g) * tq + lax.broadcasted_iota(jnp.int32, (nblk, tq), 1)
        cur = qpos // SLC_BLOCK
        forced = (blk == 0) | ((cur - blk >= 0) & (cur - blk < N_LOCAL_SLC))
        score = jnp.where(forced, FORCE, score)
        score = jnp.where(blk <= cur, score, -FORCE)
        sc_ref[g] = score
        scores.append(score)
    blocks_per_tile = tq // SLC_BLOCK

    def rank_step(t, cnts):
        out = []
        for g in range(n_tiles):
            cnt = cnts[g]
            for u in range(blocks_per_tile):
                jp = t * blocks_per_tile + u
                row = sc_ref[g, pl.ds(jp, 1), :]
                tie = jnp.where(blk > jp, 1.0, 0.0)
                cnt = cnt + jnp.where(row > scores[g], 1.0, jnp.where(row == scores[g], tie, 0.0))
            out.append(cnt)
        return tuple(out)

    zeros = tuple(jnp.zeros((nblk, tq), f32) for _ in range(n_tiles))
    cnts = lax.fori_loop(0, (i + 1) * n_tiles, rank_step, zeros)
    for g in range(n_tiles):
        pen = jnp.where(cnts[g] < n_sel, 0.0, NEG)
        if nblk < 128:
            pen = jnp.concatenate([pen, jnp.zeros((128 - nblk, tq), f32)], axis=0)
        pen_ref[g] = pen.astype(pen_ref.dtype)


def nsa_cmp_select(qnt, kkc, vct, cbias, ov, n_sel):
    B, _, T = qnt.shape
    nrow = kkc.shape[1]
    nblk = ov.shape[0]
    tq = Q_BLOCK
    g = min(CMP_TILES_PER_STEP, T // tq)
    return pl.pallas_call(
        functools.partial(_cmp_select_kernel, n_sel=n_sel),
        grid=(B, T // (g * tq)),
        in_specs=[pl.BlockSpec((None, 256, g * tq), lambda b, i: (b, 0, i)),
                  pl.BlockSpec((None, nrow, 128), lambda b, i: (b, 0, 0)),
                  pl.BlockSpec((None, HEAD_DIM, nrow), lambda b, i: (b, 0, 0)),
                  pl.BlockSpec((2 * nrow, 4 * tq), lambda b, i: (0, 0)),
                  pl.BlockSpec((nblk, nrow), lambda b, i: (0, 0))],
        out_specs=[pl.BlockSpec((None, 256, g * tq), lambda b, i: (b, 0, i)),
                   pl.BlockSpec((None, g, 128, tq), lambda b, i: (b, i, 0, 0))],
        out_shape=[jax.ShapeDtypeStruct((B, 256, T), jnp.float32),
                   jax.ShapeDtypeStruct((B, T // tq, 128, tq), jnp.bfloat16)],
        scratch_shapes=[pltpu.VMEM((g, nblk, tq), jnp.float32)],
        name="nsa_cmp_select",
    )(qnt, kkc, vct, cbias, ov)


NSA_KEY_TILE = 256
NSA_QUERY_TILE = 256


def _nsa_attn_kernel(qnt_ref, pen_ref, ocmp_ref, gate_ref, kxs_ref, v1s_ref, kkw_ref, v1w_ref,
                     tabs_ref, tabw_ref, mixg_ref, o_ref, acc_ref, m_ref, sa_ref, sb_ref):
    i = pl.program_id(1)
    tq = o_ref.shape[0]
    kt = tabs_ref.shape[1]
    f32, bf16 = jnp.float32, jnp.bfloat16
    q4t = _stack_heads_t(qnt_ref[...])
    pen = jnp.concatenate([pen_ref[t] for t in range(pen_ref.shape[0])], axis=1)
    rhs_sel = jnp.concatenate([q4t, jnp.concatenate([pen] * GROUP_HEADS, axis=1)], axis=0)

    def branch(rhs, k_ref, v_ref, tab_ref, first_tile, last_tile):
        n_tab = tab_ref.shape[0] - 1
        m_ref[...] = jnp.full(m_ref.shape, NEG, f32)
        acc_ref[...] = jnp.zeros(acc_ref.shape, f32)

        def key_offset(j):
            return pl.multiple_of(jnp.minimum(j, last_tile) * kt, kt)

        def scores(j):
            delta = jnp.where(j > last_tile, n_tab, jnp.minimum(i - j * (kt // tq), n_tab - 1))
            return jnp.dot(k_ref[pl.ds(key_offset(j), kt), :], rhs, preferred_element_type=f32) + tab_ref[delta]

        def absorb(s, j):
            m_old = m_ref[...]
            m_new = jnp.maximum(m_old, jnp.max(s, axis=0, keepdims=True))
            p = jnp.exp2(s - m_new)
            pv = jnp.dot(v_ref[:, pl.ds(key_offset(j), kt)], p.astype(bf16), preferred_element_type=f32)
            acc_ref[...] = jnp.exp2(m_old - m_new) * acc_ref[...] + pv
            m_ref[...] = m_new

        sa_ref[...] = scores(first_tile)

        def body(t, carry):
            j = first_tile + 2 * t
            sb_ref[...] = scores(j + 1)
            absorb(sa_ref[...], j)
            sa_ref[...] = scores(j + 2)
            absorb(sb_ref[...], j + 1)
            return carry

        lax.fori_loop(0, (last_tile - first_tile) // 2 + 1, body, 0)
        acc = acc_ref[...]
        return acc[0:HEAD_DIM] / jnp.maximum(acc[HEAD_DIM:HEAD_DIM + 1], 1e-30)

    def window_branch(rhs):
        span = tabw_ref.shape[1]
        back = span // tq - 1
        start = pl.multiple_of(jnp.maximum(i - back, 0) * tq, tq)
        s = jnp.dot(kkw_ref[pl.ds(start, span), :], rhs, preferred_element_type=f32) + tabw_ref[jnp.minimum(i, back)]
        p = jnp.exp2(s - jnp.max(s, axis=0, keepdims=True))
        acc = jnp.dot(v1w_ref[:, pl.ds(start, span)], p.astype(bf16), preferred_element_type=f32)
        return acc[0:HEAD_DIM] / jnp.maximum(acc[HEAD_DIM:HEAD_DIM + 1], 1e-30)

    last = (i * tq) // kt
    o_w = window_branch(q4t)
    o_s = branch(rhs_sel, kxs_ref, v1s_ref, tabs_ref, 0, last)
    gt = jax.nn.sigmoid(gate_ref[...]).T
    oct_ = ocmp_ref[...]
    parts = []
    for h in range(GROUP_HEADS):
        cols = slice(h * tq, (h + 1) * tq)
        parts.append(gt[3 * h:3 * h + 1] * oct_[h * HEAD_DIM:(h + 1) * HEAD_DIM]
                     + gt[3 * h + 1:3 * h + 2] * o_s[:, cols] + gt[3 * h + 2:3 * h + 3] * o_w[:, cols])
    yt = jnp.concatenate(parts, axis=0)
    yt = yt * lax.rsqrt(jnp.mean(yt * yt, axis=0, keepdims=True) + EPS) * mixg_ref[...]
    o_ref[...] = yt.T


def nsa_attention(qnt, pen, ocmp, pa, kxs, v1s, kkw, v1w, tab_s, tab_w, mixg):
    B, _, T = qnt.shape
    tq = min(NSA_QUERY_TILE, T)
    tile = lambda w, j: pl.BlockSpec((None, tq, w), lambda b, i: (b, i, j))
    whole = lambda w: pl.BlockSpec((None, T, w), lambda b, i: (b, 0, 0))
    whole_t = pl.BlockSpec((None, V1_ROWS, T), lambda b, i: (b, 0, 0))
    full = lambda a: pl.BlockSpec(a.shape, lambda b, i: (0,) * a.ndim)
    return pl.pallas_call(
        _nsa_attn_kernel,
        grid=(B, T // tq),
        in_specs=[pl.BlockSpec((None, 256, tq), lambda b, i: (b, 0, i)),
                  pl.BlockSpec((None, tq // Q_BLOCK, 128, Q_BLOCK), lambda b, i: (b, i, 0, 0)),
                  pl.BlockSpec((None, 256, tq), lambda b, i: (b, 0, i)), tile(128, 5),
                  whole(256), whole_t, whole(128), whole_t, full(tab_s), full(tab_w), full(mixg)],
        out_specs=tile(256, 0),
        out_shape=jax.ShapeDtypeStruct((B, T, 256), jnp.float32),
        scratch_shapes=[pltpu.VMEM((V1_ROWS, GROUP_HEADS * tq), jnp.float32),
                        pltpu.VMEM((1, GROUP_HEADS * tq), jnp.float32),
                        pltpu.VMEM((tab_s.shape[1], GROUP_HEADS * tq), jnp.float32),
                        pltpu.VMEM((tab_s.shape[1], GROUP_HEADS * tq), jnp.float32)],
        compiler_params=pltpu.CompilerParams(dimension_semantics=("arbitrary", "arbitrary"),
                                             vmem_limit_bytes=VMEM_LIMIT_BYTES),
        name="nsa_attention",
    )(qnt, pen, ocmp, pa, kxs, v1s, kkw, v1w, tab_s, tab_w, mixg)


def _toeplitz(wr, rows, cols):
    H, n = wr.shape
    flat = jnp.tile(jnp.pad(wr, ((0, 0), (0, 1))), (1, rows))[:, :rows * n]
    return flat.reshape(H, rows, n)[:, :, rows - 1:rows - 1 + cols]


def nsa_bias_tables(t5_table, T):
    H = GROUP_HEADS
    tq = Q_BLOCK
    ta = min(NSA_QUERY_TILE, T)
    kt = min(NSA_KEY_TILE, T)
    nrow = T // CMP_STRIDE
    dmax = WINDOW + 2 * ta + kt
    onehot = (t5_bucket(jnp.arange(dmax))[:, None] == jnp.arange(N_BUCKETS)[None, :]).astype(jnp.float32)
    by_dist = jnp.dot(onehot, t5_table, precision=HIGHEST).T
    neg = lambda n: jnp.full((H, n), NEG, jnp.float32)
    f_s = jnp.concatenate([neg(kt - 1), by_dist * LOG2E], axis=1)
    span = ta + kt - 1

    def tiles(f, n):
        out = [_toeplitz(f[:, k * ta:k * ta + span], kt, ta).transpose(1, 0, 2).reshape(kt, H * ta) for k in range(n)]
        return jnp.stack(out + [jnp.full((kt, H * ta), NEG, jnp.float32)])

    n_s = -(-(kt + MAX_DISTANCE) // ta) + 1
    tab_s = tiles(f_s, n_s)
    n_w = WINDOW // ta + 1
    f_w = jnp.concatenate([neg(kt - 1), by_dist[:, :WINDOW] * LOG2E, neg(dmax - WINDOW)], axis=1)
    w_tiles = tiles(f_w, n_w)
    tab_w = jnp.stack([jnp.concatenate([w_tiles[k - m if k >= m else n_w] for m in range(n_w)], axis=0)
                       for k in range(n_w)])
    per_tile = tq // CMP_STRIDE
    d0 = -CMP_STRIDE * (nrow - 1) - (CMP_BLOCK - 1)
    n16 = 2 * nrow + per_tile - 1
    n_pos = CMP_STRIDE * n16 + d0
    far = jnp.broadcast_to(by_dist[:, -1:], (H, max(n_pos - dmax, 0)))
    w16 = jnp.concatenate([neg(-d0), by_dist[:, :n_pos] * LOG2E, far * LOG2E], axis=1).reshape(H, n16, CMP_STRIDE)
    w16 = w16[:, ::-1, :]
    base = jnp.stack([w16[:, per_tile - 1 - a:per_tile - 1 - a + 2 * nrow, :] for a in range(per_tile)], axis=1)
    cbias = base.transpose(0, 1, 3, 2).reshape(H * tq, 2 * nrow).T
    return tab_s, tab_w, cbias


def nsa_overlap_t(T):
    nrow = T // CMP_STRIDE
    nblk = T // SLC_BLOCK
    ci = np.arange(nrow)[None, :]
    bj = np.arange(nblk)[:, None]
    ov = (ci * CMP_STRIDE < (bj + 1) * SLC_BLOCK) & (ci * CMP_STRIDE + CMP_BLOCK > bj * SLC_BLOCK)
    ov = ov & (ci < nrow - 1)
    return jnp.asarray(ov.astype(np.float32), dtype=jnp.bfloat16)


def nsa_mixer_pallas(pa, q_norm_g, k_norm_g, cmp_pos, cmp_k_w1, cmp_k_w2, cmp_v_w1, cmp_v_w2,
                     tables, ovt, mixg):
    B, T, _ = pa.shape
    bf16 = jnp.bfloat16
    tab_s, tab_w, cbias = tables
    qg = (jnp.tile(q_norm_g, GROUP_HEADS) * HEAD_DIM ** -0.5)[None, :]
    kg2 = jnp.tile(k_norm_g, 2)[None, :]
    qnt, kxs, v1s, kkw, v1w = nsa_prep(pa, qg, kg2, _head_mask().astype(bf16))
    dup = lambda w: jnp.concatenate([w, w], axis=1)
    zeros = jnp.zeros((CMP_BLOCK, HEAD_DIM, cmp_k_w1.shape[1]), jnp.float32)
    per_pos = lambda w: w.reshape(CMP_BLOCK, HEAD_DIM, -1)
    w1 = jnp.concatenate([jnp.concatenate([per_pos(cmp_k_w1), zeros], axis=2),
                          jnp.concatenate([zeros, per_pos(cmp_v_w1)], axis=2)], axis=1).astype(bf16)
    kkc, vct = nsa_compress(pa, dup(cmp_pos), w1, dup(cmp_k_w2).astype(bf16), dup(cmp_v_w2).astype(bf16), kg2)
    n_sel = min(N_SLC, T // SLC_BLOCK)
    ocmp, pen = nsa_cmp_select(qnt, kkc, vct, cbias, ovt, n_sel)
    mixg_col = jnp.broadcast_to(mixg.reshape(-1, 1), (GROUP_WIDTH, min(NSA_QUERY_TILE, T)))
    return nsa_attention(qnt, pen, ocmp, pa, kxs, v1s, kkw, v1w, tab_s, tab_w, mixg_col)


CHUNKS_PER_STEP = 4
SEQS_PER_STEP = 2


def _tn_dot(a, b):
    return lax.dot_general(a, b, (((0,), (0,)), ((), ())), preferred_element_type=jnp.float32)


def _split_bf16(a, parts):
    out, rest = [], a
    for _ in range(parts):
        piece = rest.astype(jnp.bfloat16)
        out.append(piece)
        rest = rest - piece.astype(jnp.float32)
    return out


def _dot_sel(a, sel, parts=3):
    return sum(jnp.dot(p, sel, preferred_element_type=jnp.float32) for p in _split_bf16(a, parts))


def _sel_dot(sel, a, parts=3):
    return sum(jnp.dot(sel, p, preferred_element_type=jnp.float32) for p in _split_bf16(a, parts))


def _softplus(x):
    return jnp.maximum(x, 0.0) + jnp.log(1.0 + jnp.exp(-jnp.abs(x)))


def _block_diag(a, hmb):
    return jnp.concatenate([a.astype(jnp.bfloat16)] * GROUP_HEADS, axis=0) * hmb


def _diag_row(a, eye):
    return jnp.sum(a * eye, axis=0, keepdims=True)


def _head_norm(o, hmb):
    return o * lax.rsqrt(_dot_sel(o * o, hmb, parts=2) * (1.0 / HEAD_DIM) + EPS)


def _head_l2(o, hmb):
    return o * lax.rsqrt(_dot_sel(o * o, hmb, parts=2) + EPS)


def _xform_masks(L):
    i = np.arange(L)[:, None]
    j = np.arange(L)[None, :]
    masks = []
    s = 1
    while s < L:
        masks.append((i // (2 * s) == j // (2 * s)) & (i % (2 * s) >= s) & (j % (2 * s) < s))
        s *= 2
    masks += [i == j, j <= i, j < i]
    return jnp.asarray(np.tile(np.stack(masks).astype(np.float32), (1, 1, GROUP_HEADS)))


def _expand_matrix(first_lane):
    e = np.zeros((128, GROUP_WIDTH), np.float32)
    for h in range(GROUP_HEADS):
        e[first_lane + h, h * HEAD_DIM:(h + 1) * HEAD_DIM] = 1.0
    return jnp.asarray(e, jnp.bfloat16)


def _head_mask():
    return jnp.asarray(np.kron(np.eye(GROUP_HEADS), np.ones((HEAD_DIM, HEAD_DIM))), jnp.float32)


def _per_head_rows(*vecs):
    rows = [jnp.repeat(v, HEAD_DIM) if v.shape[0] == GROUP_HEADS else jnp.tile(v, GROUP_HEADS) for v in vecs]
    return jnp.pad(jnp.stack(rows), ((0, 8 - len(rows)), (0, 0)))


def _chunk_consts(L, n_chunks):
    tri = np.kron(np.eye(n_chunks), np.tril(np.ones((L, L)))).astype(np.float32)
    return _head_mask(), jnp.asarray(tri, jnp.bfloat16), _xform_masks(L)


def _const_spec(a):
    return pl.BlockSpec(a.shape, lambda b, t: (0,) * a.ndim)


def _gdn_kernel(pc_ref, gates_ref, hp_ref, eb_ref, ea_ref, hm_ref, tri_ref, lv_ref, o_ref, s_ref):
    L = GDN_CHUNK
    n_seq, rows = pc_ref.shape[0], pc_ref.shape[1]
    f32, bf16 = jnp.float32, jnp.bfloat16
    n_lev = lv_ref.shape[0] - 3

    @pl.when(pl.program_id(1) == 0)
    def _():
        s_ref[...] = jnp.zeros(s_ref.shape, f32)

    hm = hm_ref[...]
    hmb = hm.astype(bf16)
    eye, incl, strict = lv_ref[n_lev], lv_ref[n_lev + 1], lv_ref[n_lev + 2]

    q_c, k_c, v_c, beta_c, gc_c = [], [], [], [], []
    for b in range(n_seq):
        qkv = pc_ref[b, :, 0:768]
        q_all = _head_l2(qkv[:, 0:256], hmb) * HEAD_DIM ** -0.5
        k_all = _head_l2(qkv[:, 256:512], hmb)
        tail = gates_ref[b]
        beta_all = jax.nn.sigmoid(_dot_sel(tail, eb_ref[...]))
        g = hp_ref[0:1, :] * _softplus(_dot_sel(tail, ea_ref[...]) + hp_ref[1:2, :])
        gc_all = _sel_dot(tri_ref[...], g)
        for c in range(rows // L):
            sl = slice(c * L, (c + 1) * L)
            q_c.append(q_all[sl])
            k_c.append(k_all[sl])
            v_c.append(qkv[sl, 512:768])
            beta_c.append(beta_all[sl])
            gc_c.append(gc_all[sl])
    chains = range(len(q_c))
    egc = [jnp.exp(gc) for gc in gc_c]
    kb = [k_c[i] * beta_c[i] for i in chains]
    a_mat, qk = [], []
    for i in chains:
        seg = jnp.exp(jnp.where(incl > 0.5, gc_c[i] - _diag_row(gc_c[i], eye), NEG))
        k_bd = _block_diag(k_c[i], hmb)
        a_mat.append(_nt_dot(kb[i].astype(bf16), k_bd) * seg * strict)
        qk.append((_nt_dot(q_c[i].astype(bf16), k_bd) * seg).astype(bf16))
    t_inv = [eye - a * lv_ref[0] for a in a_mat]
    for lev in range(1, n_lev):
        te = [jnp.dot(t_inv[i].astype(bf16), _block_diag(a_mat[i] * lv_ref[lev], hmb),
                      preferred_element_type=f32) for i in chains]
        t_inv = [t_inv[i] - jnp.dot(te[i].astype(bf16), _block_diag(t_inv[i], hmb), preferred_element_type=f32)
                 for i in chains]
    parts = []
    for i in chains:
        tb = t_inv[i].astype(bf16)
        u = jnp.dot(tb, _block_diag(v_c[i] * beta_c[i], hmb), preferred_element_type=f32)
        w = jnp.dot(tb, _block_diag(kb[i] * egc[i], hmb), preferred_element_type=f32)
        k_dec = k_c[i] * jnp.exp(gc_c[i][L - 1:L, :] - gc_c[i])
        parts.append((u, w.astype(bf16), (q_c[i] * egc[i]).astype(bf16), qk[i], k_dec.astype(bf16),
                      egc[i][L - 1:L, :]))
    per_seq = rows // L
    state = [s_ref[b] for b in range(n_seq)]
    outs = [[] for _ in range(n_seq)]
    for c in range(per_seq):
        for b in range(n_seq):
            u, w, q_dec, qk_c, k_dec, g_tot = parts[b * per_seq + c]
            sb = state[b].astype(bf16)
            v_new = u - jnp.dot(w, sb, preferred_element_type=f32)
            outs[b].append(jnp.dot(q_dec, sb, preferred_element_type=f32)
                           + jnp.dot(qk_c, _block_diag(v_new, hmb), preferred_element_type=f32))
            state[b] = state[b] * g_tot + _tn_dot(k_dec, v_new.astype(bf16)) * hm
    for b in range(n_seq):
        s_ref[b] = state[b]
        o = jnp.concatenate(outs[b], axis=0)
        z = pc_ref[b, :, 768:1024]
        o_ref[b] = _head_norm(o, hmb) * hp_ref[2:3, :] * (z * jax.nn.sigmoid(z))


def gated_deltanet_pallas(pc, pa, A_log, dt_bias, norm_g):
    B, T, width = pc.shape
    L = GDN_CHUNK
    rows = L * min(CHUNKS_PER_STEP, T // L)
    n_seq = SEQS_PER_STEP if B % SEQS_PER_STEP == 0 else 1
    n = GROUP_HEADS * HEAD_DIM
    hp = _per_head_rows(-jnp.exp(A_log), dt_bias, norm_g)
    consts = (hp, _expand_matrix(GATE_LANE["c_beta"]), _expand_matrix(GATE_LANE["c_alpha"])) + _chunk_consts(L, rows // L)
    return pl.pallas_call(
        _gdn_kernel,
        grid=(B // n_seq, T // rows),
        in_specs=[pl.BlockSpec((n_seq, rows, width), lambda b, t: (b, t, 0)),
                  pl.BlockSpec((n_seq, rows, 128), lambda b, t: (b, t, GATE_BLOCK))]
                 + [_const_spec(a) for a in consts],
        out_specs=pl.BlockSpec((n_seq, rows, 256), lambda b, t: (b, t, 0)),
        out_shape=jax.ShapeDtypeStruct((B, T, 256), jnp.float32),
        scratch_shapes=[pltpu.VMEM((n_seq, n, n), jnp.float32)],
        compiler_params=pltpu.CompilerParams(dimension_semantics=("arbitrary", "arbitrary"),
                                             vmem_limit_bytes=VMEM_LIMIT_BYTES),
        name="gated_deltanet",
    )(pc, pa, *consts)


def _mlstm_kernel(pd_ref, gates_ref, hp_ref, ei_ref, ef_ref, hm_ref, tri_ref, lv_ref, o_ref, c_ref, n_ref, m_ref):
    L = MLSTM_CHUNK
    n_seq, rows = pd_ref.shape[0], pd_ref.shape[1]
    f32, bf16 = jnp.float32, jnp.bfloat16
    n_lev = lv_ref.shape[0] - 3

    @pl.when(pl.program_id(1) == 0)
    def _():
        c_ref[...] = jnp.zeros(c_ref.shape, f32)
        n_ref[...] = jnp.zeros(n_ref.shape, f32)
        m_ref[...] = jnp.zeros(m_ref.shape, f32)

    hm = hm_ref[...]
    hmb = hm.astype(bf16)
    eye, incl = lv_ref[n_lev], lv_ref[n_lev + 1]
    head_of_lane = lax.broadcasted_iota(jnp.int32, (L, GROUP_WIDTH), 1) // HEAD_DIM

    def head_max(a):
        out = jnp.zeros(a.shape, f32)
        for h in range(GROUP_HEADS):
            mine = head_of_lane == h
            out = jnp.where(mine, jnp.max(jnp.where(mine, a, NEG), axis=-1, keepdims=True), out)
        return out

    per_seq = rows // L
    parts = []
    for b in range(n_seq):
        tail = gates_ref[b]
        log_i_all = _dot_sel(tail, ei_ref[...])
        log_f = -_softplus(-(_dot_sel(tail, ef_ref[...]) + hp_ref[0:1, :]))
        b_all = _sel_dot(tri_ref[...], log_f)
        for c in range(per_seq):
            sl = slice(c * L, (c + 1) * L)
            q = pd_ref[b, sl, 0:256]
            k = pd_ref[b, sl, 256:512] * HEAD_DIM ** -0.5
            v = pd_ref[b, sl, 512:768]
            bc, log_i = b_all[sl], log_i_all[sl]
            log_w = jnp.where(incl > 0.5, bc - _diag_row(bc - log_i, eye), NEG)
            b_last = bc[L - 1:L, :]
            log_w_end = b_last - bc + log_i
            qk = _nt_dot(q.astype(bf16), _block_diag(k, hmb))
            parts.append((q, k, v, bc, log_w, head_max(log_w), b_last, log_w_end,
                          jnp.max(log_w_end, axis=0, keepdims=True), qk))
    c_state = [c_ref[b] for b in range(n_seq)]
    n_state = [n_ref[b] for b in range(n_seq)]
    m_prev = [m_ref[b] for b in range(n_seq)]
    outs = [[] for _ in range(n_seq)]
    for c in range(per_seq):
        for b in range(n_seq):
            q, k, v, bc, log_w, m_intra, b_last, log_w_end, m_end, qk = parts[b * per_seq + c]
            log_inter = bc + m_prev[b]
            m_t = jnp.maximum(log_inter, m_intra)
            w_inter = jnp.exp(log_inter - m_t)
            s = qk * jnp.exp(log_w - m_t)
            num = w_inter * jnp.dot(q.astype(bf16), c_state[b].astype(bf16), preferred_element_type=f32)
            num = num + jnp.dot(s.astype(bf16), _block_diag(v, hmb), preferred_element_type=f32)
            den = w_inter * _dot_sel(q * n_state[b], hmb, parts=2) + _dot_sel(s, hmb, parts=2)
            outs[b].append(num / jnp.maximum(jnp.abs(den), jnp.exp(-m_t)))
            m_new = jnp.maximum(b_last + m_prev[b], m_end)
            w_old = jnp.exp(b_last + m_prev[b] - m_new)
            kw = k * jnp.exp(log_w_end - m_new)
            c_state[b] = w_old * c_state[b] + _tn_dot(kw.astype(bf16), v.astype(bf16)) * hm
            n_state[b] = w_old * n_state[b] + jnp.sum(kw, axis=0, keepdims=True)
            m_prev[b] = m_new
    for b in range(n_seq):
        c_ref[b] = c_state[b]
        n_ref[b] = n_state[b]
        m_ref[b] = m_prev[b]
        h = jnp.concatenate(outs[b], axis=0)
        o_ref[b] = _head_norm(h, hmb) * hp_ref[1:2, :] * jax.nn.sigmoid(pd_ref[b, :, 768:1024])


def mlstm_pallas(pd, pa, f_bias, norm_g):
    B, T, width = pd.shape
    L = MLSTM_CHUNK
    rows = L * min(CHUNKS_PER_STEP, T // L)
    n_seq = SEQS_PER_STEP if B % SEQS_PER_STEP == 0 else 1
    n = GROUP_HEADS * HEAD_DIM
    consts = ((_per_head_rows(f_bias, norm_g), _expand_matrix(GATE_LANE["d_i"]), _expand_matrix(GATE_LANE["d_f"]))
              + _chunk_consts(L, rows // L))
    return pl.pallas_call(
        _mlstm_kernel,
        grid=(B // n_seq, T // rows),
        in_specs=[pl.BlockSpec((n_seq, rows, width), lambda b, t: (b, t, 0)),
                  pl.BlockSpec((n_seq, rows, 128), lambda b, t: (b, t, GATE_BLOCK))]
                 + [_const_spec(a) for a in consts],
        out_specs=pl.BlockSpec((n_seq, rows, 256), lambda b, t: (b, t, 0)),
        out_shape=jax.ShapeDtypeStruct((B, T, 256), jnp.float32),
        scratch_shapes=[pltpu.VMEM((n_seq, n, n), jnp.float32), pltpu.VMEM((n_seq, 1, 256), jnp.float32),
                        pltpu.VMEM((n_seq, 1, 256), jnp.float32)],
        compiler_params=pltpu.CompilerParams(dimension_semantics=("arbitrary", "arbitrary"),
                                             vmem_limit_bytes=VMEM_LIMIT_BYTES),
        name="mlstm",
    )(pd, pa, *consts)


def kernel(x, c, ada_w, ada_b, norm_g, ffn1_w13, ffn1_w2, ffn2_w13, ffn2_w2, w_in, b_in, q_norm_g, k_norm_g, cmp_pos, cmp_k_w1, cmp_k_w2, cmp_v_w1, cmp_v_w2, t5_table, sc_conv_w, gdn_conv_w, gdn_A_log, gdn_dt_bias, gdn_norm_g, mlstm_f_bias, mlstm_norm_g, mix_norm_g, w_out):
    B, T, D = x.shape
    depth = ada_w.shape[0]
    bf16 = jnp.bfloat16
    x2 = x.reshape(B * T, D)
    tables = nsa_bias_tables(t5_table, T)
    ovt = nsa_overlap_t(T)
    ffn_w = [w.astype(bf16) for w in (ffn1_w13, ffn1_w2, ffn2_w13, ffn2_w2, w_out)]
    for l in range(depth):
        mod = ada_modulation(c, ada_w, ada_b[l][None, :], l)
        mod = mod.reshape(B, N_SUBLAYERS, 3, 1, D)
        x2 = ffn_half_step(x2, norm_g[l, 0][None, :], mod[:, 0, 0], mod[:, 0, 1], mod[:, 0, 2],
                           ffn_w[0], ffn_w[1], l, T)
        ws, bs = split_in_weights(w_in[l], b_in[l])
        pa, y_b, pc, pd = in_projection(x2, norm_g[l, 1][None, :], mod[:, 1, 0], mod[:, 1, 1], ws, bs,
                                        sc_conv_w[l], mix_norm_g[l, 1][None, :], gdn_conv_w[l], T)
        pa, pc, pd = (a.reshape(B, T, -1) for a in (pa, pc, pd))
        y_a = nsa_mixer_pallas(pa, q_norm_g[l], k_norm_g[l], cmp_pos[l], cmp_k_w1[l], cmp_k_w2[l],
                               cmp_v_w1[l], cmp_v_w2[l], tables, ovt, mix_norm_g[l, 0][None, :])
        y_c = gated_deltanet_pallas(pc, pa, gdn_A_log[l], gdn_dt_bias[l], gdn_norm_g[l])
        y_d = mlstm_pallas(pd, pa, mlstm_f_bias[l], mlstm_norm_g[l])
        ys = [y.reshape(B * T, GROUP_WIDTH) for y in (y_a, y_b, y_c, y_d)]
        x2 = ffn_half_step(x2, norm_g[l, 2][None, :], mod[:, 2, 0], mod[:, 2, 1], mod[:, 2, 2],
                           ffn_w[2], ffn_w[3], l, T, mix=(ys, mod[:, 1, 2], ffn_w[4]))
    return x2.reshape(B, T, D)
```

```python
import functools
import math

import jax
import jax.numpy as jnp
import numpy as np
from jax import lax
from jax.experimental import pallas as pl
from jax.experimental.pallas import tpu as pltpu

HEAD_DIM = 64
GROUP_HEADS = 4
GROUP_WIDTH = GROUP_HEADS * HEAD_DIM
CMP_STRIDE = 16
CMP_BLOCK = 32
SLC_BLOCK = 64
N_SLC = 16
N_LOCAL_SLC = 2
WINDOW = 512
Q_BLOCK = 128
FORCE = 1e6
N_BUCKETS = 32
MAX_DISTANCE = 128
GDN_CHUNK = 64
MLSTM_CHUNK = 64
N_SUBLAYERS = 3
EPS = 1e-6

IN_LAYOUT = (
    ("a_q", 256), ("a_k_cmp", 64), ("a_v_cmp", 64),
    ("a_k_slc", 64), ("a_v_slc", 64), ("a_k_win", 64), ("a_v_win", 64),
    ("a_gate", 12),
    ("b_b", 256), ("b_c", 256), ("b_x", 256),
    ("c_q", 256), ("c_k", 256), ("c_v", 256),
    ("c_beta", 4), ("c_alpha", 4), ("c_z", 256),
    ("d_q", 256), ("d_k", 256), ("d_v", 256),
    ("d_i", 4), ("d_f", 4), ("d_o", 256),
)

VMEM_LIMIT_BYTES = 56 * 1024 * 1024
FFN_TOKEN_TILE = 512
FFN_CHUNK = 256
PROJ_TOKEN_TILE = 512
NEG = -1e30
HIGHEST = lax.Precision.HIGHEST
LOG2E = math.log2(math.e)
V1_ROWS = HEAD_DIM + 16


def _modulated_norm(x, g, scale, shift):
    y = x * lax.rsqrt(jnp.mean(x * x, axis=-1, keepdims=True) + EPS)
    return (y * g) * (1.0 + scale) + shift


def _ada_kernel(c_ref, w_ref, b_ref, o_ref):
    c = c_ref[...]
    cond = c * jax.nn.sigmoid(c)
    o_ref[...] = jnp.dot(cond.astype(jnp.bfloat16), w_ref[...].astype(jnp.bfloat16),
                         preferred_element_type=jnp.float32) + b_ref[...]


def ada_modulation(c, w, b, layer):
    B, D = c.shape
    N = w.shape[2]
    tn = 1152
    return pl.pallas_call(
        _ada_kernel,
        grid=(N // tn,),
        in_specs=[pl.BlockSpec((B, D), lambda j: (0, 0)),
                  pl.BlockSpec((None, D, tn), lambda j: (layer, 0, j)),
                  pl.BlockSpec((1, tn), lambda j: (0, j))],
        out_specs=pl.BlockSpec((B, tn), lambda j: (0, j)),
        out_shape=jax.ShapeDtypeStruct((B, N), jnp.float32),
        name="ada_modulation",
    )(c, w, b)


def _ffn_kernel(x_ref, g_ref, shift_ref, scale_ref, gate_ref, w1_ref, w3_ref, w2_ref, *rest, n_mix):
    o_ref, act_ref = rest[-2:]
    x = x_ref[...]
    if n_mix:
        y_refs, mix_gate_ref, wo_ref = rest[:n_mix], rest[n_mix], rest[n_mix + 1]
        z = None
        for k, y_ref in enumerate(y_refs):
            part = jnp.dot(y_ref[...].astype(jnp.bfloat16), wo_ref[k * GROUP_WIDTH:(k + 1) * GROUP_WIDTH, :],
                           preferred_element_type=jnp.float32)
            z = part if z is None else z + part
        x = x + mix_gate_ref[0] * z
    h = _modulated_norm(x, g_ref[...], scale_ref[0], shift_ref[0]).astype(jnp.bfloat16)
    n_chunks = act_ref.shape[1] // FFN_CHUNK
    for ci in range(n_chunks):
        cs = slice(ci * FFN_CHUNK, (ci + 1) * FFN_CHUNK)
        a = jnp.dot(h, w1_ref[:, cs], preferred_element_type=jnp.float32)
        b = jnp.dot(h, w3_ref[:, cs], preferred_element_type=jnp.float32)
        act_ref[:, cs] = (a * jax.nn.sigmoid(a) * b).astype(jnp.bfloat16)
    y = jnp.dot(act_ref[...], w2_ref[...], preferred_element_type=jnp.float32)
    o_ref[...] = x + (0.5 * gate_ref[0]) * y


def ffn_half_step(x2, g, shift, scale, gate, w13, w2, layer, tokens_per_batch, mix=None):
    M, D = x2.shape
    F = w2.shape[1]
    tm = FFN_TOKEN_TILE
    tiles_per_batch = tokens_per_batch // tm
    resident = dict(pipeline_mode=pl.Buffered(1))
    mod_spec = pl.BlockSpec((1, 1, D), lambda i: (i // tiles_per_batch, 0, 0))
    ys, mix_gate, w_out = mix if mix is not None else ((), None, None)
    mix_args = tuple(ys) + ((mix_gate, w_out) if ys else ())
    mix_specs = [pl.BlockSpec((tm, GROUP_WIDTH), lambda i: (i, 0)) for _ in ys]
    if ys:
        mix_specs += [mod_spec, pl.BlockSpec((None,) + w_out.shape[1:], lambda i: (layer, 0, 0), **resident)]
    return pl.pallas_call(
        functools.partial(_ffn_kernel, n_mix=len(ys)),
        grid=(M // tm,),
        in_specs=[pl.BlockSpec((tm, D), lambda i: (i, 0)),
                  pl.BlockSpec((1, D), lambda i: (0, 0)),
                  mod_spec, mod_spec, mod_spec,
                  pl.BlockSpec((None, D, F), lambda i: (layer, 0, 0), **resident),
                  pl.BlockSpec((None, D, F), lambda i: (layer, 0, 1), **resident),
                  pl.BlockSpec((None, F, D), lambda i: (layer, 0, 0), **resident)] + mix_specs,
        out_specs=pl.BlockSpec((tm, D), lambda i: (i, 0)),
        out_shape=jax.ShapeDtypeStruct((M, D), jnp.float32),
        scratch_shapes=[pltpu.VMEM((tm, F), jnp.bfloat16)],
        compiler_params=pltpu.CompilerParams(dimension_semantics=("arbitrary",),
                                             vmem_limit_bytes=VMEM_LIMIT_BYTES),
        name="ffn_half_step",
    )(x2, g, shift, scale, gate, w13, w13, w2, *mix_args)


CONV_CARRY_ROWS = 8
CONV_ROW_CHUNK = 128


def _in_proj_kernel(x_ref, g_ref, shift_ref, scale_ref, wa_ref, wb_ref, wc_ref, wd_ref,
                    ba_ref, bb_ref, bc_ref, bd_ref, scw_ref, scg_ref, gcw_ref,
                    oa_ref, yb_ref, oc_ref, od_ref, braw_ref, craw_ref, *, tiles_per_batch):
    f32 = jnp.float32
    rows = x_ref.shape[0]
    hist = CONV_CARRY_ROWS
    first = pl.program_id(0) % tiles_per_batch == 0

    @pl.when(first)
    def _():
        braw_ref[0:hist, :] = jnp.zeros((hist, braw_ref.shape[1]), f32)
        craw_ref[0:hist, :] = jnp.zeros((hist, craw_ref.shape[1]), f32)

    @pl.when(jnp.logical_not(first))
    def _():
        braw_ref[0:hist, :] = braw_ref[rows:rows + hist, :]
        craw_ref[0:hist, :] = craw_ref[rows:rows + hist, :]

    h = _modulated_norm(x_ref[...], g_ref[...], scale_ref[0], shift_ref[0]).astype(jnp.bfloat16)
    project = lambda w_ref, b_ref: jnp.dot(h, w_ref[...], preferred_element_type=f32) + b_ref[...]
    pb = project(wb_ref, bb_ref)
    yb_ref[...] = pb[:, 0:256]
    braw_ref[hist:hist + rows, :] = pb[:, 256:512] * pb[:, 512:768]
    pc = project(wc_ref, bc_ref)
    craw_ref[hist:hist + rows, :] = pc[:, 0:768]
    oc_ref[:, 768:] = pc[:, 768:]
    n_chunks = rows // CONV_ROW_CHUNK
    pieces = []
    for w_ref, b_ref, o_ref in ((wa_ref, ba_ref, oa_ref), (wd_ref, bd_ref, od_ref)):
        half = w_ref.shape[1] // 256 // 2 * 256
        pieces += [(w_ref, b_ref, o_ref, 0, half), (w_ref, b_ref, o_ref, half, w_ref.shape[1])]
    for ci, r0 in enumerate(range(0, rows, CONV_ROW_CHUNK)):
        for w_ref, b_ref, o_ref, c0, c1 in pieces[ci * len(pieces) // n_chunks:(ci + 1) * len(pieces) // n_chunks]:
            o_ref[:, c0:c1] = jnp.dot(h, w_ref[:, c0:c1], preferred_element_type=f32) + b_ref[:, c0:c1]
        out_rows = slice(r0, r0 + CONV_ROW_CHUNK)
        tap = lambda ref, s: ref[pl.ds(hist + r0 - s, CONV_ROW_CHUNK), :]
        conv = sum(tap(braw_ref, s) * scw_ref[2 - s:3 - s, :] for s in range(3))
        y = yb_ref[out_rows, :] * conv
        yb_ref[out_rows, :] = y * lax.rsqrt(jnp.mean(y * y, axis=-1, keepdims=True) + EPS) * scg_ref[...]
        for c0 in range(0, 3 * GROUP_WIDTH, GROUP_WIDTH):
            cols = slice(c0, c0 + GROUP_WIDTH)
            conv = sum(craw_ref[pl.ds(hist + r0 - s, CONV_ROW_CHUNK), cols] * gcw_ref[3 - s:4 - s, cols]
                       for s in range(4))
            oc_ref[out_rows, cols] = conv * jax.nn.sigmoid(conv)


def in_projection(x2, g, shift, scale, ws, bs, layer, sc_conv_w, sc_norm_g, gdn_conv_w, tokens_per_batch):
    M, D = x2.shape
    tm = PROJ_TOKEN_TILE
    tiles_per_batch = tokens_per_batch // tm
    mod_spec = pl.BlockSpec((1, 1, D), lambda i: (i // tiles_per_batch, 0, 0))
    const = lambda a, **kw: pl.BlockSpec(a.shape, lambda i: (0, 0), **kw)
    pad8 = lambda w: jnp.pad(w, ((0, 8 - w.shape[0]), (0, 0)))
    extras = (pad8(sc_conv_w), sc_norm_g, pad8(gdn_conv_w))
    widths = (ws[0].shape[2], GROUP_WIDTH, ws[2].shape[2], ws[3].shape[2])
    of_layer = lambda a, **kw: pl.BlockSpec((None,) + a.shape[1:], lambda i: (layer, 0, 0), **kw)
    return pl.pallas_call(
        functools.partial(_in_proj_kernel, tiles_per_batch=tiles_per_batch),
        grid=(M // tm,),
        in_specs=[pl.BlockSpec((tm, D), lambda i: (i, 0)), const(g), mod_spec, mod_spec]
                 + [of_layer(w, pipeline_mode=pl.Buffered(1)) for w in ws] + [of_layer(b) for b in bs]
                 + [const(e) for e in extras],
        out_specs=[pl.BlockSpec((tm, n), lambda i: (i, 0)) for n in widths],
        out_shape=[jax.ShapeDtypeStruct((M, n), jnp.float32) for n in widths],
        scratch_shapes=[pltpu.VMEM((CONV_CARRY_ROWS + tm, GROUP_WIDTH), jnp.float32),
                        pltpu.VMEM((CONV_CARRY_ROWS + tm, 3 * GROUP_WIDTH), jnp.float32)],
        compiler_params=pltpu.CompilerParams(dimension_semantics=("arbitrary",),
                                             vmem_limit_bytes=VMEM_LIMIT_BYTES),
        name="in_projection",
    )(x2, g, shift, scale, *ws, *bs, *extras)


def _group_columns():
    offs, o = {}, 0
    for n, s in IN_LAYOUT:
        offs[n] = np.arange(o, o + s)
        o += s
    cat = lambda names: np.concatenate([offs[n] for n in names])
    return (cat(["a_q", "a_k_cmp", "a_v_cmp", "a_k_slc", "a_v_slc", "a_k_win", "a_v_win"] + list(SMALL_GATES)),
            cat(["b_b", "b_c", "b_x"]),
            cat(["c_q", "c_k", "c_v", "c_z"]),
            cat(["d_q", "d_k", "d_v", "d_o"]))


SMALL_GATES = ("a_gate", "c_beta", "c_alpha", "d_i", "d_f")
GATE_BLOCK = 5
GATE_LANE = {"a_gate": 0, "c_beta": 12, "c_alpha": 16, "d_i": 20, "d_f": 24}
GROUP_SLAB_WIDTH = (768, 768, 1024, 1024)


def _take_columns(a, idx, width):
    cuts = [0] + [k + 1 for k in range(idx.size - 1) if idx[k + 1] != idx[k] + 1] + [idx.size]
    runs = [a[..., int(idx[s]):int(idx[e - 1]) + 1] for s, e in zip(cuts[:-1], cuts[1:])]
    return jnp.concatenate(runs + [jnp.zeros(a.shape[:-1] + (width - idx.size,), a.dtype)], axis=-1)


def split_in_weights(w, b):
    ws, bs = [], []
    for idx, width in zip(_group_columns(), GROUP_SLAB_WIDTH):
        ws.append(_take_columns(w, idx, width).astype(jnp.bfloat16))
        bs.append(_take_columns(b, idx, width)[:, None, :])
    return ws, bs


def t5_bucket(dist):
    n = jnp.maximum(dist, 0)
    max_exact = N_BUCKETS // 2
    nf = jnp.maximum(n, 1).astype(jnp.float32)
    large = max_exact + (jnp.log(nf / max_exact) / math.log(MAX_DISTANCE / max_exact)
                         * (N_BUCKETS - max_exact)).astype(jnp.int32)
    large = jnp.minimum(large, N_BUCKETS - 1)
    return jnp.where(n < max_exact, n, large)


def _nt_dot(a, b):
    return lax.dot_general(a, b, (((1,), (1,)), ((), ())), preferred_element_type=jnp.float32)


def _nsa_prep_kernel(q_ref, kvs_ref, kvw_ref, qg_ref, kg_ref, seg_ref,
                     qnt_ref, kxs_ref, v1s_ref, kkw_ref, v1w_ref):
    tm = q_ref.shape[0]
    q = q_ref[...]
    ss = _dot_sel(q * q, seg_ref[...], parts=3)
    qn = q * lax.rsqrt(ss * (1.0 / HEAD_DIM) + EPS) * qg_ref[...]
    qnt_ref[...] = (qn * LOG2E).T.astype(qnt_ref.dtype)
    lane = lax.broadcasted_iota(jnp.int32, (tm, 128), 1)
    lo = lane < HEAD_DIM

    def split(x):
        xr = pltpu.roll(x, HEAD_DIM, 1)
        ss = jnp.sum(jnp.where(lo, x * x, 0.0), axis=-1, keepdims=True)
        kk = jnp.where(lo, x, xr) * lax.rsqrt(ss * (1.0 / HEAD_DIM) + EPS) * kg_ref[...]
        return kk, jnp.where(lo, xr, 1.0)

    kk, v1 = split(kvs_ref[...])
    tok = pl.program_id(1) * tm + lax.broadcasted_iota(jnp.int32, (tm, 128), 0)
    onehot = jnp.where(lane == tok // SLC_BLOCK, 1.0, 0.0)
    kxs_ref[...] = jnp.concatenate([kk, onehot], axis=1).astype(kxs_ref.dtype)
    v1s_ref[...] = v1.T[0:V1_ROWS].astype(v1s_ref.dtype)
    kk, v1 = split(kvw_ref[...])
    kkw_ref[...] = kk.astype(kkw_ref.dtype)
    v1w_ref[...] = v1.T[0:V1_ROWS].astype(v1w_ref.dtype)


def nsa_prep(pa, qg, kg2, seg):
    B, T, _ = pa.shape
    tm = 512
    bf16 = jnp.bfloat16
    col = lambda w, j: pl.BlockSpec((None, tm, w), lambda b, i: (b, i, j))
    rowblk = lambda h: pl.BlockSpec((None, h, tm), lambda b, i: (b, 0, i))
    const = lambda shape: pl.BlockSpec(shape, lambda b, i: (0, 0))
    return pl.pallas_call(
        _nsa_prep_kernel,
        grid=(B, T // tm),
        in_specs=[col(256, 0), col(128, 3), col(128, 4), const((1, 256)), const((1, 128)), const((256, 256))],
        out_specs=[rowblk(256), col(256, 0), rowblk(V1_ROWS), col(128, 0), rowblk(V1_ROWS)],
        out_shape=[jax.ShapeDtypeStruct((B, 256, T), bf16),
                   jax.ShapeDtypeStruct((B, T, 256), bf16), jax.ShapeDtypeStruct((B, V1_ROWS, T), bf16),
                   jax.ShapeDtypeStruct((B, T, 128), bf16), jax.ShapeDtypeStruct((B, V1_ROWS, T), bf16)],
        name="nsa_prep",
    )(pa, pa, pa, qg, kg2, seg)


def _compress_kernel(kv_ref, pos_ref, w1_ref, w2k_ref, w2v_ref, kg_ref, kkc_ref, vct_ref):
    nrow = kkc_ref.shape[0]
    f32, bf16 = jnp.float32, jnp.bfloat16
    hidden = w1_ref.shape[2] // 2
    first = jnp.zeros((nrow, 2 * hidden), f32)
    second = jnp.zeros((nrow, 2 * hidden), f32)
    for p in range(CMP_STRIDE):
        xp = kv_ref[pl.ds(p, nrow, stride=CMP_STRIDE), :]
        first = first + jnp.dot((xp + pos_ref[p:p + 1, :]).astype(bf16), w1_ref[p], preferred_element_type=f32)
        second = second + jnp.dot((xp + pos_ref[CMP_STRIDE + p:CMP_STRIDE + p + 1, :]).astype(bf16),
                                  w1_ref[CMP_STRIDE + p], preferred_element_type=f32)
    hid = first + pltpu.roll(second, nrow - 1, 0)
    hid = (hid * jax.nn.sigmoid(hid)).astype(bf16)
    valid = lax.broadcasted_iota(jnp.int32, (nrow, 128), 0) < nrow - 1
    kc = jnp.dot(hid[:, 0:hidden], w2k_ref[...], preferred_element_type=f32)
    kc = kc * lax.rsqrt(jnp.mean(kc * kc, axis=-1, keepdims=True) + EPS) * kg_ref[...]
    kkc_ref[...] = jnp.where(valid, kc, 0.0).astype(kkc_ref.dtype)
    vc = jnp.dot(hid[:, hidden:2 * hidden], w2v_ref[...], preferred_element_type=f32)
    vct_ref[...] = jnp.where(valid, vc, 0.0).T[0:HEAD_DIM].astype(vct_ref.dtype)


def nsa_compress(pa, pos2, w1, w2k, w2v, kg2):
    B, T, _ = pa.shape
    nrow = T // CMP_STRIDE
    bf16 = jnp.bfloat16
    full = lambda a: pl.BlockSpec(a.shape, lambda b: (0,) * a.ndim)
    return pl.pallas_call(
        _compress_kernel,
        grid=(B,),
        in_specs=[pl.BlockSpec((None, T, 128), lambda b: (b, 0, 2)), full(pos2), full(w1), full(w2k), full(w2v),
                  full(kg2)],
        out_specs=[pl.BlockSpec((None, nrow, 128), lambda b: (b, 0, 0)),
                   pl.BlockSpec((None, HEAD_DIM, nrow), lambda b: (b, 0, 0))],
        out_shape=[jax.ShapeDtypeStruct((B, nrow, 128), bf16), jax.ShapeDtypeStruct((B, HEAD_DIM, nrow), bf16)],
        name="nsa_compress",
    )(pa, pos2, w1, w2k, w2v, kg2)


CMP_TILES_PER_STEP = 4


def _stack_heads_t(qt):
    lo = lax.broadcasted_iota(jnp.int32, (128, qt.shape[1]), 0) < HEAD_DIM
    zero = jnp.zeros((), qt.dtype)
    return jnp.concatenate([jnp.where(lo if h % 2 == 0 else ~lo, qt[128 * (h // 2):128 * (h // 2) + 128], zero)
                            for h in range(GROUP_HEADS)], axis=1)


def _cmp_select_kernel(qnt_ref, kkc_ref, vct_ref, cb_ref, ov_ref, ocmp_ref, pen_ref, sc_ref, *, n_sel):
    i = pl.program_id(1)
    n_tiles = pen_ref.shape[0]
    tq = qnt_ref.shape[1] // n_tiles
    nblk, nrow = ov_ref.shape
    keys_per_tile = tq // CMP_STRIDE
    f32, bf16 = jnp.float32, jnp.bfloat16
    blk = lax.broadcasted_iota(jnp.int32, (nblk, tq), 0)
    scores = []
    for g in range(n_tiles):
        cols = slice(g * tq, (g + 1) * tq)
        q4t = _stack_heads_t(qnt_ref[:, cols])
        first_row = pl.multiple_of(nrow - keys_per_tile * (i * n_tiles + g), keys_per_tile)
        s = jnp.dot(kkc_ref[...], q4t, preferred_element_type=f32) + cb_ref[pl.ds(first_row, nrow), :]
        m = jnp.max(s, axis=0, keepdims=True)
        e = jnp.where(s > 0.5 * NEG, jnp.exp2(s - m), 0.0)
        p = e * (1.0 / jnp.maximum(jnp.sum(e, axis=0, keepdims=True), 1e-30))
        o = jnp.dot(vct_ref[...], p.astype(bf16), preferred_element_type=f32)
        for h in range(GROUP_HEADS):
            ocmp_ref[h * HEAD_DIM:(h + 1) * HEAD_DIM, cols] = o[:, h * tq:(h + 1) * tq]
        psum = p[:, 0:tq] + p[:, tq:2 * tq] + p[:, 2 * tq:3 * tq] + p[:, 3 * tq:4 * tq]
        hi = psum.astype(bf16)
        lo_part = (psum - hi.astype(f32)).astype(bf16)
        score = (jnp.dot(ov_ref[...], hi, preferred_element_type=f32)
                 + jnp.dot(ov_ref[...], lo_part, preferred_element_type=f32))
        qpos = (i * n_tiles + g) * tq + lax.broadcasted_iota(jnp.int32, (nblk, tq), 1)
        cur = qpos // SLC_BLOCK
        forced = (blk == 0) | ((cur - blk >= 0) & (cur - blk < N_LOCAL_SLC))
        score = jnp.where(forced, FORCE, score)
        score = jnp.where(blk <= cur, score, -FORCE)
        sc_ref[g] = score
        scores.append(score)
    blocks_per_tile = tq // SLC_BLOCK

    def rank_step(t, cnts):
        out = []
        for g in range(n_tiles):
            cnt = cnts[g]
            for u in range(blocks_per_tile):
                jp = t * blocks_per_tile + u
                row = sc_ref[g, pl.ds(jp, 1), :]
                tie = jnp.where(blk > jp, 1.0, 0.0)
                cnt = cnt + jnp.where(row > scores[g], 1.0, jnp.where(row == scores[g], tie, 0.0))
            out.append(cnt)
        return tuple(out)

    zeros = tuple(jnp.zeros((nblk, tq), f32) for _ in range(n_tiles))
    cnts = lax.fori_loop(0, (i + 1) * n_tiles, rank_step, zeros)
    for g in range(n_tiles):
        pen = jnp.where(cnts[g] < n_sel, 0.0, NEG)
        if nblk < 128:
            pen = jnp.concatenate([pen, jnp.zeros((128 - nblk, tq), f32)], axis=0)
        pen_ref[g] = pen.astype(pen_ref.dtype)


def nsa_cmp_select(qnt, kkc, vct, cbias, ov, n_sel):
    B, _, T = qnt.shape
    nrow = kkc.shape[1]
    nblk = ov.shape[0]
    tq = Q_BLOCK
    g = min(CMP_TILES_PER_STEP, T // tq)
    return pl.pallas_call(
        functools.partial(_cmp_select_kernel, n_sel=n_sel),
        grid=(B, T // (g * tq)),
        in_specs=[pl.BlockSpec((None, 256, g * tq), lambda b, i: (b, 0, i)),
                  pl.BlockSpec((None, nrow, 128), lambda b, i: (b, 0, 0)),
                  pl.BlockSpec((None, HEAD_DIM, nrow), lambda b, i: (b, 0, 0)),
                  pl.BlockSpec((2 * nrow, 4 * tq), lambda b, i: (0, 0)),
                  pl.BlockSpec((nblk, nrow), lambda b, i: (0, 0))],
        out_specs=[pl.BlockSpec((None, 256, g * tq), lambda b, i: (b, 0, i)),
                   pl.BlockSpec((None, g, 128, tq), lambda b, i: (b, i, 0, 0))],
        out_shape=[jax.ShapeDtypeStruct((B, 256, T), jnp.float32),
                   jax.ShapeDtypeStruct((B, T // tq, 128, tq), jnp.bfloat16)],
        scratch_shapes=[pltpu.VMEM((g, nblk, tq), jnp.float32)],
        name="nsa_cmp_select",
    )(qnt, kkc, vct, cbias, ov)


NSA_KEY_TILE = 256
NSA_QUERY_TILE = 256


def _nsa_attn_kernel(qnt_ref, pen_ref, ocmp_ref, gate_ref, kxs_ref, v1s_ref, kkw_ref, v1w_ref,
                     tabs_ref, tabw_ref, mixg_ref, o_ref, acc_ref, m_ref, sa_ref, sb_ref):
    i = pl.program_id(1)
    tq = o_ref.shape[0]
    kt = tabs_ref.shape[1]
    f32, bf16 = jnp.float32, jnp.bfloat16
    q4t = _stack_heads_t(qnt_ref[...])
    pen = jnp.concatenate([pen_ref[t] for t in range(pen_ref.shape[0])], axis=1)
    rhs_sel = jnp.concatenate([q4t, jnp.concatenate([pen] * GROUP_HEADS, axis=1)], axis=0)

    def branch(rhs, k_ref, v_ref, tab_ref, first_tile, last_tile):
        n_tab = tab_ref.shape[0] - 1
        m_ref[...] = jnp.full(m_ref.shape, NEG, f32)
        acc_ref[...] = jnp.zeros(acc_ref.shape, f32)

        def key_offset(j):
            return pl.multiple_of(jnp.minimum(j, last_tile) * kt, kt)

        def scores(j):
            delta = jnp.where(j > last_tile, n_tab, jnp.minimum(i - j * (kt // tq), n_tab - 1))
            return jnp.dot(k_ref[pl.ds(key_offset(j), kt), :], rhs, preferred_element_type=f32) + tab_ref[delta]

        def absorb(s, j):
            m_old = m_ref[...]
            m_new = jnp.maximum(m_old, jnp.max(s, axis=0, keepdims=True))
            p = jnp.exp2(s - m_new)
            pv = jnp.dot(v_ref[:, pl.ds(key_offset(j), kt)], p.astype(bf16), preferred_element_type=f32)
            acc_ref[...] = jnp.exp2(m_old - m_new) * acc_ref[...] + pv
            m_ref[...] = m_new

        sa_ref[...] = scores(first_tile)

        def body(t, carry):
            j = first_tile + 2 * t
            sb_ref[...] = scores(j + 1)
            absorb(sa_ref[...], j)
            sa_ref[...] = scores(j + 2)
            absorb(sb_ref[...], j + 1)
            return carry

        lax.fori_loop(0, (last_tile - first_tile) // 2 + 1, body, 0)
        acc = acc_ref[...]
        return acc[0:HEAD_DIM] / jnp.maximum(acc[HEAD_DIM:HEAD_DIM + 1], 1e-30)

    def window_branch(rhs):
        span = tabw_ref.shape[1]
        back = span // tq - 1
        start = pl.multiple_of(jnp.maximum(i - back, 0) * tq, tq)
        s = jnp.dot(kkw_ref[pl.ds(start, span), :], rhs, preferred_element_type=f32) + tabw_ref[jnp.minimum(i, back)]
        p = jnp.exp2(s - jnp.max(s, axis=0, keepdims=True))
        acc = jnp.dot(v1w_ref[:, pl.ds(start, span)], p.astype(bf16), preferred_element_type=f32)
        return acc[0:HEAD_DIM] / jnp.maximum(acc[HEAD_DIM:HEAD_DIM + 1], 1e-30)

    last = (i * tq) // kt
    o_w = window_branch(q4t)
    o_s = branch(rhs_sel, kxs_ref, v1s_ref, tabs_ref, 0, last)
    gt = jax.nn.sigmoid(gate_ref[...]).T
    oct_ = ocmp_ref[...]
    parts = []
    for h in range(GROUP_HEADS):
        cols = slice(h * tq, (h + 1) * tq)
        parts.append(gt[3 * h:3 * h + 1] * oct_[h * HEAD_DIM:(h + 1) * HEAD_DIM]
                     + gt[3 * h + 1:3 * h + 2] * o_s[:, cols] + gt[3 * h + 2:3 * h + 3] * o_w[:, cols])
    yt = jnp.concatenate(parts, axis=0)
    yt = yt * lax.rsqrt(jnp.mean(yt * yt, axis=0, keepdims=True) + EPS) * mixg_ref[...]
    o_ref[...] = yt.T


def nsa_attention(qnt, pen, ocmp, pa, kxs, v1s, kkw, v1w, tab_s, tab_w, mixg):
    B, _, T = qnt.shape
    tq = min(NSA_QUERY_TILE, T)
    tile = lambda w, j: pl.BlockSpec((None, tq, w), lambda b, i: (b, i, j))
    whole = lambda w: pl.BlockSpec((None, T, w), lambda b, i: (b, 0, 0))
    whole_t = pl.BlockSpec((None, V1_ROWS, T), lambda b, i: (b, 0, 0))
    full = lambda a: pl.BlockSpec(a.shape, lambda b, i: (0,) * a.ndim)
    return pl.pallas_call(
        _nsa_attn_kernel,
        grid=(B, T // tq),
        in_specs=[pl.BlockSpec((None, 256, tq), lambda b, i: (b, 0, i)),
                  pl.BlockSpec((None, tq // Q_BLOCK, 128, Q_BLOCK), lambda b, i: (b, i, 0, 0)),
                  pl.BlockSpec((None, 256, tq), lambda b, i: (b, 0, i)), tile(128, 5),
                  whole(256), whole_t, whole(128), whole_t, full(tab_s), full(tab_w), full(mixg)],
        out_specs=tile(256, 0),
        out_shape=jax.ShapeDtypeStruct((B, T, 256), jnp.float32),
        scratch_shapes=[pltpu.VMEM((V1_ROWS, GROUP_HEADS * tq), jnp.float32),
                        pltpu.VMEM((1, GROUP_HEADS * tq), jnp.float32),
                        pltpu.VMEM((tab_s.shape[1], GROUP_HEADS * tq), jnp.float32),
                        pltpu.VMEM((tab_s.shape[1], GROUP_HEADS * tq), jnp.float32)],
        compiler_params=pltpu.CompilerParams(dimension_semantics=("arbitrary", "arbitrary"),
                                             vmem_limit_bytes=VMEM_LIMIT_BYTES),
        name="nsa_attention",
    )(qnt, pen, ocmp, pa, kxs, v1s, kkw, v1w, tab_s, tab_w, mixg)


def _toeplitz(wr, rows, cols):
    H, n = wr.shape
    flat = jnp.tile(jnp.pad(wr, ((0, 0), (0, 1))), (1, rows))[:, :rows * n]
    return flat.reshape(H, rows, n)[:, :, rows - 1:rows - 1 + cols]


def nsa_bias_tables(t5_table, T):
    H = GROUP_HEADS
    tq = Q_BLOCK
    ta = min(NSA_QUERY_TILE, T)
    kt = min(NSA_KEY_TILE, T)
    nrow = T // CMP_STRIDE
    dmax = WINDOW + 2 * ta + kt
    onehot = (t5_bucket(jnp.arange(dmax))[:, None] == jnp.arange(N_BUCKETS)[None, :]).astype(jnp.float32)
    by_dist = jnp.dot(onehot, t5_table, precision=HIGHEST).T
    neg = lambda n: jnp.full((H, n), NEG, jnp.float32)
    f_s = jnp.concatenate([neg(kt - 1), by_dist * LOG2E], axis=1)
    span = ta + kt - 1

    def tiles(f, n):
        out = [_toeplitz(f[:, k * ta:k * ta + span], kt, ta).transpose(1, 0, 2).reshape(kt, H * ta) for k in range(n)]
        return jnp.stack(out + [jnp.full((kt, H * ta), NEG, jnp.float32)])

    n_s = -(-(kt + MAX_DISTANCE) // ta) + 1
    tab_s = tiles(f_s, n_s)
    n_w = WINDOW // ta + 1
    f_w = jnp.concatenate([neg(kt - 1), by_dist[:, :WINDOW] * LOG2E, neg(dmax - WINDOW)], axis=1)
    w_tiles = tiles(f_w, n_w)
    tab_w = jnp.stack([jnp.concatenate([w_tiles[k - m if k >= m else n_w] for m in range(n_w)], axis=0)
                       for k in range(n_w)])
    per_tile = tq // CMP_STRIDE
    d0 = -CMP_STRIDE * (nrow - 1) - (CMP_BLOCK - 1)
    n16 = 2 * nrow + per_tile - 1
    n_pos = CMP_STRIDE * n16 + d0
    far = jnp.broadcast_to(by_dist[:, -1:], (H, max(n_pos - dmax, 0)))
    w16 = jnp.concatenate([neg(-d0), by_dist[:, :n_pos] * LOG2E, far * LOG2E], axis=1).reshape(H, n16, CMP_STRIDE)
    w16 = w16[:, ::-1, :]
    base = jnp.stack([w16[:, per_tile - 1 - a:per_tile - 1 - a + 2 * nrow, :] for a in range(per_tile)], axis=1)
    cbias = base.transpose(0, 1, 3, 2).reshape(H * tq, 2 * nrow).T
    return tab_s, tab_w, cbias


def nsa_overlap_t(T):
    nrow = T // CMP_STRIDE
    nblk = T // SLC_BLOCK
    ci = np.arange(nrow)[None, :]
    bj = np.arange(nblk)[:, None]
    ov = (ci * CMP_STRIDE < (bj + 1) * SLC_BLOCK) & (ci * CMP_STRIDE + CMP_BLOCK > bj * SLC_BLOCK)
    ov = ov & (ci < nrow - 1)
    return jnp.asarray(ov.astype(np.float32), dtype=jnp.bfloat16)


def nsa_mixer_pallas(pa, q_norm_g, k_norm_g, cmp_pos, cmp_k_w1, cmp_k_w2, cmp_v_w1, cmp_v_w2,
                     tables, ovt, mixg):
    B, T, _ = pa.shape
    bf16 = jnp.bfloat16
    tab_s, tab_w, cbias = tables
    qg = (jnp.tile(q_norm_g, GROUP_HEADS) * HEAD_DIM ** -0.5)[None, :]
    kg2 = jnp.tile(k_norm_g, 2)[None, :]
    qnt, kxs, v1s, kkw, v1w = nsa_prep(pa, qg, kg2, _head_mask().astype(bf16))
    dup = lambda w: jnp.concatenate([w, w], axis=1)
    zeros = jnp.zeros((CMP_BLOCK, HEAD_DIM, cmp_k_w1.shape[1]), jnp.float32)
    per_pos = lambda w: w.reshape(CMP_BLOCK, HEAD_DIM, -1)
    w1 = jnp.concatenate([jnp.concatenate([per_pos(cmp_k_w1), zeros], axis=2),
                          jnp.concatenate([zeros, per_pos(cmp_v_w1)], axis=2)], axis=1).astype(bf16)
    kkc, vct = nsa_compress(pa, dup(cmp_pos), w1, dup(cmp_k_w2).astype(bf16), dup(cmp_v_w2).astype(bf16), kg2)
    n_sel = min(N_SLC, T // SLC_BLOCK)
    ocmp, pen = nsa_cmp_select(qnt, kkc, vct, cbias, ovt, n_sel)
    mixg_col = jnp.broadcast_to(mixg.reshape(-1, 1), (GROUP_WIDTH, min(NSA_QUERY_TILE, T)))
    return nsa_attention(qnt, pen, ocmp, pa, kxs, v1s, kkw, v1w, tab_s, tab_w, mixg_col)


CHUNKS_PER_STEP = 4
SEQS_PER_STEP = 2


def _tn_dot(a, b):
    return lax.dot_general(a, b, (((0,), (0,)), ((), ())), preferred_element_type=jnp.float32)


def _split_bf16(a, parts):
    out, rest = [], a
    for _ in range(parts):
        piece = rest.astype(jnp.bfloat16)
        out.append(piece)
        rest = rest - piece.astype(jnp.float32)
    return out


def _dot_sel(a, sel, parts=3):
    return sum(jnp.dot(p, sel, preferred_element_type=jnp.float32) for p in _split_bf16(a, parts))


def _sel_dot(sel, a, parts=3):
    return sum(jnp.dot(sel, p, preferred_element_type=jnp.float32) for p in _split_bf16(a, parts))


def _softplus(x):
    return jnp.maximum(x, 0.0) + jnp.log(1.0 + jnp.exp(-jnp.abs(x)))


def _block_diag(a, hmb):
    return jnp.concatenate([a.astype(jnp.bfloat16)] * GROUP_HEADS, axis=0) * hmb


def _diag_row(a, eye):
    return jnp.sum(a * eye, axis=0, keepdims=True)


def _head_norm(o, hmb):
    return o * lax.rsqrt(_dot_sel(o * o, hmb, parts=2) * (1.0 / HEAD_DIM) + EPS)


def _head_l2(o, hmb):
    return o * lax.rsqrt(_dot_sel(o * o, hmb, parts=2) + EPS)


def _xform_masks(L):
    i = np.arange(L)[:, None]
    j = np.arange(L)[None, :]
    masks = []
    s = 1
    while s < L:
        masks.append((i // (2 * s) == j // (2 * s)) & (i % (2 * s) >= s) & (j % (2 * s) < s))
        s *= 2
    masks += [i == j, j <= i, j < i]
    return jnp.asarray(np.tile(np.stack(masks).astype(np.float32), (1, 1, GROUP_HEADS)))


def _expand_matrix(first_lane):
    e = np.zeros((128, GROUP_WIDTH), np.float32)
    for h in range(GROUP_HEADS):
        e[first_lane + h, h * HEAD_DIM:(h + 1) * HEAD_DIM] = 1.0
    return jnp.asarray(e, jnp.bfloat16)


def _head_mask():
    return jnp.asarray(np.kron(np.eye(GROUP_HEADS), np.ones((HEAD_DIM, HEAD_DIM))), jnp.float32)


def _per_head_rows(*vecs):
    rows = [jnp.repeat(v, HEAD_DIM) if v.shape[0] == GROUP_HEADS else jnp.tile(v, GROUP_HEADS) for v in vecs]
    return jnp.pad(jnp.stack(rows), ((0, 8 - len(rows)), (0, 0)))


def _chunk_consts(L, n_chunks):
    tri = np.kron(np.eye(n_chunks), np.tril(np.ones((L, L)))).astype(np.float32)
    return _head_mask(), jnp.asarray(tri, jnp.bfloat16), _xform_masks(L)


def _const_spec(a):
    return pl.BlockSpec(a.shape, lambda b, t: (0,) * a.ndim)


def _gdn_kernel(pc_ref, gates_ref, hp_ref, eb_ref, ea_ref, hm_ref, tri_ref, lv_ref, o_ref, s_ref):
    L = GDN_CHUNK
    n_seq, rows = pc_ref.shape[0], pc_ref.shape[1]
    f32, bf16 = jnp.float32, jnp.bfloat16
    n_lev = lv_ref.shape[0] - 3

    @pl.when(pl.program_id(1) == 0)
    def _():
        s_ref[...] = jnp.zeros(s_ref.shape, f32)

    hm = hm_ref[...]
    hmb = hm.astype(bf16)
    eye, incl, strict = lv_ref[n_lev], lv_ref[n_lev + 1], lv_ref[n_lev + 2]

    q_c, k_c, v_c, beta_c, gc_c = [], [], [], [], []
    for b in range(n_seq):
        qkv = pc_ref[b, :, 0:768]
        q_all = _head_l2(qkv[:, 0:256], hmb) * HEAD_DIM ** -0.5
        k_all = _head_l2(qkv[:, 256:512], hmb)
        tail = gates_ref[b]
        beta_all = jax.nn.sigmoid(_dot_sel(tail, eb_ref[...]))
        g = hp_ref[0:1, :] * _softplus(_dot_sel(tail, ea_ref[...]) + hp_ref[1:2, :])
        gc_all = _sel_dot(tri_ref[...], g)
        for c in range(rows // L):
            sl = slice(c * L, (c + 1) * L)
            q_c.append(q_all[sl])
            k_c.append(k_all[sl])
            v_c.append(qkv[sl, 512:768])
            beta_c.append(beta_all[sl])
            gc_c.append(gc_all[sl])
    chains = range(len(q_c))
    egc = [jnp.exp(gc) for gc in gc_c]
    kb = [k_c[i] * beta_c[i] for i in chains]
    a_mat, qk = [], []
    for i in chains:
        seg = jnp.exp(jnp.where(incl > 0.5, gc_c[i] - _diag_row(gc_c[i], eye), NEG))
        k_bd = _block_diag(k_c[i], hmb)
        a_mat.append(_nt_dot(kb[i].astype(bf16), k_bd) * seg * strict)
        qk.append((_nt_dot(q_c[i].astype(bf16), k_bd) * seg).astype(bf16))
    t_inv = [eye - a * lv_ref[0] for a in a_mat]
    for lev in range(1, n_lev):
        te = [jnp.dot(t_inv[i].astype(bf16), _block_diag(a_mat[i] * lv_ref[lev], hmb),
                      preferred_element_type=f32) for i in chains]
        t_inv = [t_inv[i] - jnp.dot(te[i].astype(bf16), _block_diag(t_inv[i], hmb), preferred_element_type=f32)
                 for i in chains]
    parts = []
    for i in chains:
        tb = t_inv[i].astype(bf16)
        u = jnp.dot(tb, _block_diag(v_c[i] * beta_c[i], hmb), preferred_element_type=f32)
        w = jnp.dot(tb, _block_diag(kb[i] * egc[i], hmb), preferred_element_type=f32)
        k_dec = k_c[i] * jnp.exp(gc_c[i][L - 1:L, :] - gc_c[i])
        parts.append((u, w.astype(bf16), (q_c[i] * egc[i]).astype(bf16), qk[i], k_dec.astype(bf16),
                      egc[i][L - 1:L, :]))
    per_seq = rows // L
    state = [s_ref[b] for b in range(n_seq)]
    outs = [[] for _ in range(n_seq)]
    for c in range(per_seq):
        for b in range(n_seq):
            u, w, q_dec, qk_c, k_dec, g_tot = parts[b * per_seq + c]
            sb = state[b].astype(bf16)
            v_new = u - jnp.dot(w, sb, preferred_element_type=f32)
            outs[b].append(jnp.dot(q_dec, sb, preferred_element_type=f32)
                           + jnp.dot(qk_c, _block_diag(v_new, hmb), preferred_element_type=f32))
            state[b] = state[b] * g_tot + _tn_dot(k_dec, v_new.astype(bf16)) * hm
    for b in range(n_seq):
        s_ref[b] = state[b]
        o = jnp.concatenate(outs[b], axis=0)
        z = pc_ref[b, :, 768:1024]
        o_ref[b] = _head_norm(o, hmb) * hp_ref[2:3, :] * (z * jax.nn.sigmoid(z))


def gated_deltanet_pallas(pc, pa, A_log, dt_bias, norm_g):
    B, T, width = pc.shape
    L = GDN_CHUNK
    rows = L * min(CHUNKS_PER_STEP, T // L)
    n_seq = SEQS_PER_STEP if B % SEQS_PER_STEP == 0 else 1
    n = GROUP_HEADS * HEAD_DIM
    hp = _per_head_rows(-jnp.exp(A_log), dt_bias, norm_g)
    consts = (hp, _expand_matrix(GATE_LANE["c_beta"]), _expand_matrix(GATE_LANE["c_alpha"])) + _chunk_consts(L, rows // L)
    return pl.pallas_call(
        _gdn_kernel,
        grid=(B // n_seq, T // rows),
        in_specs=[pl.BlockSpec((n_seq, rows, width), lambda b, t: (b, t, 0)),
                  pl.BlockSpec((n_seq, rows, 128), lambda b, t: (b, t, GATE_BLOCK))]
                 + [_const_spec(a) for a in consts],
        out_specs=pl.BlockSpec((n_seq, rows, 256), lambda b, t: (b, t, 0)),
        out_shape=jax.ShapeDtypeStruct((B, T, 256), jnp.float32),
        scratch_shapes=[pltpu.VMEM((n_seq, n, n), jnp.float32)],
        compiler_params=pltpu.CompilerParams(dimension_semantics=("arbitrary", "arbitrary"),
                                             vmem_limit_bytes=VMEM_LIMIT_BYTES),
        name="gated_deltanet",
    )(pc, pa, *consts)


def _mlstm_kernel(pd_ref, gates_ref, hp_ref, ei_ref, ef_ref, hm_ref, tri_ref, lv_ref, o_ref, c_ref, n_ref, m_ref):
    L = MLSTM_CHUNK
    n_seq, rows = pd_ref.shape[0], pd_ref.shape[1]
    f32, bf16 = jnp.float32, jnp.bfloat16
    n_lev = lv_ref.shape[0] - 3

    @pl.when(pl.program_id(1) == 0)
    def _():
        c_ref[...] = jnp.zeros(c_ref.shape, f32)
        n_ref[...] = jnp.zeros(n_ref.shape, f32)
        m_ref[...] = jnp.zeros(m_ref.shape, f32)

    hm = hm_ref[...]
    hmb = hm.astype(bf16)
    eye, incl = lv_ref[n_lev], lv_ref[n_lev + 1]
    head_of_lane = lax.broadcasted_iota(jnp.int32, (L, GROUP_WIDTH), 1) // HEAD_DIM

    def head_max(a):
        out = jnp.zeros(a.shape, f32)
        for h in range(GROUP_HEADS):
            mine = head_of_lane == h
            out = jnp.where(mine, jnp.max(jnp.where(mine, a, NEG), axis=-1, keepdims=True), out)
        return out

    per_seq = rows // L
    parts = []
    for b in range(n_seq):
        tail = gates_ref[b]
        log_i_all = _dot_sel(tail, ei_ref[...])
        log_f = -_softplus(-(_dot_sel(tail, ef_ref[...]) + hp_ref[0:1, :]))
        b_all = _sel_dot(tri_ref[...], log_f)
        for c in range(per_seq):
            sl = slice(c * L, (c + 1) * L)
            q = pd_ref[b, sl, 0:256]
            k = pd_ref[b, sl, 256:512] * HEAD_DIM ** -0.5
            v = pd_ref[b, sl, 512:768]
            bc, log_i = b_all[sl], log_i_all[sl]
            log_w = jnp.where(incl > 0.5, bc - _diag_row(bc - log_i, eye), NEG)
            b_last = bc[L - 1:L, :]
            log_w_end = b_last - bc + log_i
            qk = _nt_dot(q.astype(bf16), _block_diag(k, hmb))
            parts.append((q, k, v, bc, log_w, head_max(log_w), b_last, log_w_end,
                          jnp.max(log_w_end, axis=0, keepdims=True), qk))
    c_state = [c_ref[b] for b in range(n_seq)]
    n_state = [n_ref[b] for b in range(n_seq)]
    m_prev = [m_ref[b] for b in range(n_seq)]
    outs = [[] for _ in range(n_seq)]
    for c in range(per_seq):
        for b in range(n_seq):
            q, k, v, bc, log_w, m_intra, b_last, log_w_end, m_end, qk = parts[b * per_seq + c]
            log_inter = bc + m_prev[b]
            m_t = jnp.maximum(log_inter, m_intra)
            w_inter = jnp.exp(log_inter - m_t)
            s = qk * jnp.exp(log_w - m_t)
            num = w_inter * jnp.dot(q.astype(bf16), c_state[b].astype(bf16), preferred_element_type=f32)
            num = num + jnp.dot(s.astype(bf16), _block_diag(v, hmb), preferred_element_type=f32)
            den = w_inter * _dot_sel(q * n_state[b], hmb, parts=2) + _dot_sel(s, hmb, parts=2)
            outs[b].append(num / jnp.maximum(jnp.abs(den), jnp.exp(-m_t)))
            m_new = jnp.maximum(b_last + m_prev[b], m_end)
            w_old = jnp.exp(b_last + m_prev[b] - m_new)
            kw = k * jnp.exp(log_w_end - m_new)
            c_state[b] = w_old * c_state[b] + _tn_dot(kw.astype(bf16), v.astype(bf16)) * hm
            n_state[b] = w_old * n_state[b] + jnp.sum(kw, axis=0, keepdims=True)
            m_prev[b] = m_new
    for b in range(n_seq):
        c_ref[b] = c_state[b]
        n_ref[b] = n_state[b]
        m_ref[b] = m_prev[b]
        h = jnp.concatenate(outs[b], axis=0)
        o_ref[b] = _head_norm(h, hmb) * hp_ref[1:2, :] * jax.nn.sigmoid(pd_ref[b, :, 768:1024])


def mlstm_pallas(pd, pa, f_bias, norm_g):
    B, T, width = pd.shape
    L = MLSTM_CHUNK
    rows = L * min(CHUNKS_PER_STEP, T // L)
    n_seq = SEQS_PER_STEP if B % SEQS_PER_STEP == 0 else 1
    n = GROUP_HEADS * HEAD_DIM
    consts = ((_per_head_rows(f_bias, norm_g), _expand_matrix(GATE_LANE["d_i"]), _expand_matrix(GATE_LANE["d_f"]))
              + _chunk_consts(L, rows // L))
    return pl.pallas_call(
        _mlstm_kernel,
        grid=(B // n_seq, T // rows),
        in_specs=[pl.BlockSpec((n_seq, rows, width), lambda b, t: (b, t, 0)),
                  pl.BlockSpec((n_seq, rows, 128), lambda b, t: (b, t, GATE_BLOCK))]
                 + [_const_spec(a) for a in consts],
        out_specs=pl.BlockSpec((n_seq, rows, 256), lambda b, t: (b, t, 0)),
        out_shape=jax.ShapeDtypeStruct((B, T, 256), jnp.float32),
        scratch_shapes=[pltpu.VMEM((n_seq, n, n), jnp.float32), pltpu.VMEM((n_seq, 1, 256), jnp.float32),
                        pltpu.VMEM((n_seq, 1, 256), jnp.float32)],
        compiler_params=pltpu.CompilerParams(dimension_semantics=("arbitrary", "arbitrary"),
                                             vmem_limit_bytes=VMEM_LIMIT_BYTES),
        name="mlstm",
    )(pd, pa, *consts)


def kernel(x, c, ada_w, ada_b, norm_g, ffn1_w13, ffn1_w2, ffn2_w13, ffn2_w2, w_in, b_in, q_norm_g, k_norm_g, cmp_pos, cmp_k_w1, cmp_k_w2, cmp_v_w1, cmp_v_w2, t5_table, sc_conv_w, gdn_conv_w, gdn_A_log, gdn_dt_bias, gdn_norm_g, mlstm_f_bias, mlstm_norm_g, mix_norm_g, w_out):
    B, T, D = x.shape
    depth = ada_w.shape[0]
    bf16 = jnp.bfloat16
    x2 = x.reshape(B * T, D)
    tables = nsa_bias_tables(t5_table, T)
    ovt = nsa_overlap_t(T)
    ffn_w = [w.astype(bf16) for w in (ffn1_w13, ffn1_w2, ffn2_w13, ffn2_w2, w_out)]
    in_ws, in_bs = split_in_weights(w_in, b_in)
    for l in range(depth):
        mod = ada_modulation(c, ada_w, ada_b[l][None, :], l)
        mod = mod.reshape(B, N_SUBLAYERS, 3, 1, D)
        x2 = ffn_half_step(x2, norm_g[l, 0][None, :], mod[:, 0, 0], mod[:, 0, 1], mod[:, 0, 2],
                           ffn_w[0], ffn_w[1], l, T)
        pa, y_b, pc, pd = in_projection(x2, norm_g[l, 1][None, :], mod[:, 1, 0], mod[:, 1, 1], in_ws, in_bs, l,
                                        sc_conv_w[l], mix_norm_g[l, 1][None, :], gdn_conv_w[l], T)
        pa, pc, pd = (a.reshape(B, T, -1) for a in (pa, pc, pd))
        y_a = nsa_mixer_pallas(pa, q_norm_g[l], k_norm_g[l], cmp_pos[l], cmp_k_w1[l], cmp_k_w2[l],
                               cmp_v_w1[l], cmp_v_w2[l], tables, ovt, mix_norm_g[l, 0][None, :])
        y_c = gated_deltanet_pallas(pc, pa, gdn_A_log[l], gdn_dt_bias[l], gdn_norm_g[l])
        y_d = mlstm_pallas(pd, pa, mlstm_f_bias[l], mlstm_norm_g[l])
        ys = [y.reshape(B * T, GROUP_WIDTH) for y in (y_a, y_b, y_c, y_d)]
        x2 = ffn_half_step(x2, norm_g[l, 2][None, :], mod[:, 2, 0], mod[:, 2, 1], mod[:, 2, 2],
                           ffn_w[2], ffn_w[3], l, T, mix=(ys, mod[:, 1, 2], ffn_w[4]))
    return x2.reshape(B, T, D)
```

```python
import functools
import math

import jax
import jax.numpy as jnp
import numpy as np
from jax import lax
from jax.experimental import pallas as pl
from jax.experimental.pallas import tpu as pltpu

HEAD_DIM = 64
GROUP_HEADS = 4
GROUP_WIDTH = GROUP_HEADS * HEAD_DIM
CMP_STRIDE = 16
CMP_BLOCK = 32
SLC_BLOCK = 64
N_SLC = 16
N_LOCAL_SLC = 2
WINDOW = 512
Q_BLOCK = 128
FORCE = 1e6
N_BUCKETS = 32
MAX_DISTANCE = 128
GDN_CHUNK = 64
MLSTM_CHUNK = 64
N_SUBLAYERS = 3
EPS = 1e-6

IN_LAYOUT = (
    ("a_q", 256), ("a_k_cmp", 64), ("a_v_cmp", 64),
    ("a_k_slc", 64), ("a_v_slc", 64), ("a_k_win", 64), ("a_v_win", 64),
    ("a_gate", 12),
    ("b_b", 256), ("b_c", 256), ("b_x", 256),
    ("c_q", 256), ("c_k", 256), ("c_v", 256),
    ("c_beta", 4), ("c_alpha", 4), ("c_z", 256),
    ("d_q", 256), ("d_k", 256), ("d_v", 256),
    ("d_i", 4), ("d_f", 4), ("d_o", 256),
)

VMEM_LIMIT_BYTES = 56 * 1024 * 1024
FFN_TOKEN_TILE = 512
FFN_CHUNK = 256
PROJ_TOKEN_TILE = 512
NEG = -1e30
HIGHEST = lax.Precision.HIGHEST
LOG2E = math.log2(math.e)
V1_ROWS = HEAD_DIM + 16


def _modulated_norm(x, g, scale, shift):
    y = x * lax.rsqrt(jnp.mean(x * x, axis=-1, keepdims=True) + EPS)
    return (y * g) * (1.0 + scale) + shift


def _ada_kernel(c_ref, w_ref, b_ref, o_ref):
    c = c_ref[...]
    cond = c * jax.nn.sigmoid(c)
    o_ref[...] = jnp.dot(cond.astype(jnp.bfloat16), w_ref[...].astype(jnp.bfloat16),
                         preferred_element_type=jnp.float32) + b_ref[...]


def ada_modulation(c, w, b, layer):
    B, D = c.shape
    N = w.shape[2]
    tn = 1152
    return pl.pallas_call(
        _ada_kernel,
        grid=(N // tn,),
        in_specs=[pl.BlockSpec((B, D), lambda j: (0, 0)),
                  pl.BlockSpec((None, D, tn), lambda j: (layer, 0, j)),
                  pl.BlockSpec((1, tn), lambda j: (0, j))],
        out_specs=pl.BlockSpec((B, tn), lambda j: (0, j)),
        out_shape=jax.ShapeDtypeStruct((B, N), jnp.float32),
        name="ada_modulation",
    )(c, w, b)


def _ffn_kernel(x_ref, g_ref, shift_ref, scale_ref, gate_ref, w1_ref, w3_ref, w2_ref, *rest, n_mix):
    o_ref, act_ref = rest[-2:]
    x = x_ref[...]
    if n_mix:
        y_refs, mix_gate_ref, wo_ref = rest[:n_mix], rest[n_mix], rest[n_mix + 1]
        z = None
        for k, y_ref in enumerate(y_refs):
            part = jnp.dot(y_ref[...].astype(jnp.bfloat16), wo_ref[k * GROUP_WIDTH:(k + 1) * GROUP_WIDTH, :],
                           preferred_element_type=jnp.float32)
            z = part if z is None else z + part
        x = x + mix_gate_ref[0] * z
    h = _modulated_norm(x, g_ref[...], scale_ref[0], shift_ref[0]).astype(jnp.bfloat16)
    n_chunks = act_ref.shape[1] // FFN_CHUNK
    for ci in range(n_chunks):
        cs = slice(ci * FFN_CHUNK, (ci + 1) * FFN_CHUNK)
        a = jnp.dot(h, w1_ref[:, cs], preferred_element_type=jnp.float32)
        b = jnp.dot(h, w3_ref[:, cs], preferred_element_type=jnp.float32)
        act_ref[:, cs] = (a * jax.nn.sigmoid(a) * b).astype(jnp.bfloat16)
    y = jnp.dot(act_ref[...], w2_ref[...], preferred_element_type=jnp.float32)
    o_ref[...] = x + (0.5 * gate_ref[0]) * y


def ffn_half_step(x2, g, shift, scale, gate, w13, w2, layer, tokens_per_batch, mix=None):
    M, D = x2.shape
    F = w2.shape[1]
    tm = FFN_TOKEN_TILE
    tiles_per_batch = tokens_per_batch // tm
    resident = dict(pipeline_mode=pl.Buffered(1))
    mod_spec = pl.BlockSpec((1, 1, D), lambda i: (i // tiles_per_batch, 0, 0))
    ys, mix_gate, w_out = mix if mix is not None else ((), None, None)
    mix_args = tuple(ys) + ((mix_gate, w_out) if ys else ())
    mix_specs = [pl.BlockSpec((tm, GROUP_WIDTH), lambda i: (i, 0)) for _ in ys]
    if ys:
        mix_specs += [mod_spec, pl.BlockSpec((None,) + w_out.shape[1:], lambda i: (layer, 0, 0), **resident)]
    return pl.pallas_call(
        functools.partial(_ffn_kernel, n_mix=len(ys)),
        grid=(M // tm,),
        in_specs=[pl.BlockSpec((tm, D), lambda i: (i, 0)),
                  pl.BlockSpec((1, D), lambda i: (0, 0)),
                  mod_spec, mod_spec, mod_spec,
                  pl.BlockSpec((None, D, F), lambda i: (layer, 0, 0), **resident),
                  pl.BlockSpec((None, D, F), lambda i: (layer, 0, 1), **resident),
                  pl.BlockSpec((None, F, D), lambda i: (layer, 0, 0), **resident)] + mix_specs,
        out_specs=pl.BlockSpec((tm, D), lambda i: (i, 0)),
        out_shape=jax.ShapeDtypeStruct((M, D), jnp.float32),
        scratch_shapes=[pltpu.VMEM((tm, F), jnp.bfloat16)],
        compiler_params=pltpu.CompilerParams(dimension_semantics=("arbitrary",),
                                             vmem_limit_bytes=VMEM_LIMIT_BYTES),
        name="ffn_half_step",
    )(x2, g, shift, scale, gate, w13, w13, w2, *mix_args)


CONV_CARRY_ROWS = 8
CONV_ROW_CHUNK = 128


def _in_proj_kernel(x_ref, g_ref, shift_ref, scale_ref, wa_ref, wb_ref, wc_ref, wd_ref,
                    ba_ref, bb_ref, bc_ref, bd_ref, scw_ref, scg_ref, gcw_ref,
                    oa_ref, yb_ref, oc_ref, od_ref, braw_ref, craw_ref, *, tiles_per_batch):
    f32 = jnp.float32
    rows = x_ref.shape[0]
    hist = CONV_CARRY_ROWS
    first = pl.program_id(0) % tiles_per_batch == 0

    @pl.when(first)
    def _():
        braw_ref[0:hist, :] = jnp.zeros((hist, braw_ref.shape[1]), f32)
        craw_ref[0:hist, :] = jnp.zeros((hist, craw_ref.shape[1]), f32)

    @pl.when(jnp.logical_not(first))
    def _():
        braw_ref[0:hist, :] = braw_ref[rows:rows + hist, :]
        craw_ref[0:hist, :] = craw_ref[rows:rows + hist, :]

    h = _modulated_norm(x_ref[...], g_ref[...], scale_ref[0], shift_ref[0]).astype(jnp.bfloat16)
    project = lambda w_ref, b_ref: jnp.dot(h, w_ref[...], preferred_element_type=f32) + b_ref[...]
    pb = project(wb_ref, bb_ref)
    yb_ref[...] = pb[:, 0:256]
    braw_ref[hist:hist + rows, :] = pb[:, 256:512] * pb[:, 512:768]
    pc = project(wc_ref, bc_ref)
    craw_ref[hist:hist + rows, :] = pc[:, 0:768]
    oc_ref[:, 768:] = pc[:, 768:]
    n_chunks = rows // CONV_ROW_CHUNK
    pieces = []
    for w_ref, b_ref, o_ref in ((wa_ref, ba_ref, oa_ref), (wd_ref, bd_ref, od_ref)):
        half = w_ref.shape[1] // 256 // 2 * 256
        pieces += [(w_ref, b_ref, o_ref, 0, half), (w_ref, b_ref, o_ref, half, w_ref.shape[1])]
    for ci, r0 in enumerate(range(0, rows, CONV_ROW_CHUNK)):
        for w_ref, b_ref, o_ref, c0, c1 in pieces[ci * len(pieces) // n_chunks:(ci + 1) * len(pieces) // n_chunks]:
            o_ref[:, c0:c1] = jnp.dot(h, w_ref[:, c0:c1], preferred_element_type=f32) + b_ref[:, c0:c1]
        out_rows = slice(r0, r0 + CONV_ROW_CHUNK)
        tap = lambda ref, s: ref[pl.ds(hist + r0 - s, CONV_ROW_CHUNK), :]
        conv = sum(tap(braw_ref, s) * scw_ref[2 - s:3 - s, :] for s in range(3))
        y = yb_ref[out_rows, :] * conv
        yb_ref[out_rows, :] = y * lax.rsqrt(jnp.mean(y * y, axis=-1, keepdims=True) + EPS) * scg_ref[...]
        for c0 in range(0, 3 * GROUP_WIDTH, GROUP_WIDTH):
            cols = slice(c0, c0 + GROUP_WIDTH)
            conv = sum(craw_ref[pl.ds(hist + r0 - s, CONV_ROW_CHUNK), cols] * gcw_ref[3 - s:4 - s, cols]
                       for s in range(4))
            oc_ref[out_rows, cols] = conv * jax.nn.sigmoid(conv)


def in_projection(x2, g, shift, scale, ws, bs, layer, sc_conv_w, sc_norm_g, gdn_conv_w, tokens_per_batch):
    M, D = x2.shape
    tm = PROJ_TOKEN_TILE
    tiles_per_batch = tokens_per_batch // tm
    mod_spec = pl.BlockSpec((1, 1, D), lambda i: (i // tiles_per_batch, 0, 0))
    const = lambda a, **kw: pl.BlockSpec(a.shape, lambda i: (0, 0), **kw)
    pad8 = lambda w: jnp.pad(w, ((0, 8 - w.shape[0]), (0, 0)))
    extras = (pad8(sc_conv_w), sc_norm_g, pad8(gdn_conv_w))
    widths = (ws[0].shape[2], GROUP_WIDTH, ws[2].shape[2], ws[3].shape[2])
    of_layer = lambda a, **kw: pl.BlockSpec((None,) + a.shape[1:], lambda i: (layer, 0, 0), **kw)
    return pl.pallas_call(
        functools.partial(_in_proj_kernel, tiles_per_batch=tiles_per_batch),
        grid=(M // tm,),
        in_specs=[pl.BlockSpec((tm, D), lambda i: (i, 0)), const(g), mod_spec, mod_spec]
                 + [of_layer(w, pipeline_mode=pl.Buffered(1)) for w in ws] + [of_layer(b) for b in bs]
                 + [const(e) for e in extras],
        out_specs=[pl.BlockSpec((tm, n), lambda i: (i, 0)) for n in widths],
        out_shape=[jax.ShapeDtypeStruct((M, n), jnp.float32) for n in widths],
        scratch_shapes=[pltpu.VMEM((CONV_CARRY_ROWS + tm, GROUP_WIDTH), jnp.float32),
                        pltpu.VMEM((CONV_CARRY_ROWS + tm, 3 * GROUP_WIDTH), jnp.float32)],
        compiler_params=pltpu.CompilerParams(dimension_semantics=("arbitrary",),
                                             vmem_limit_bytes=VMEM_LIMIT_BYTES),
        name="in_projection",
    )(x2, g, shift, scale, *ws, *bs, *extras)


def _group_columns():
    offs, o = {}, 0
    for n, s in IN_LAYOUT:
        offs[n] = np.arange(o, o + s)
        o += s
    cat = lambda names: np.concatenate([offs[n] for n in names])
    return (cat(["a_q", "a_k_cmp", "a_v_cmp", "a_k_slc", "a_v_slc", "a_k_win", "a_v_win"] + list(SMALL_GATES)),
            cat(["b_b", "b_c", "b_x"]),
            cat(["c_q", "c_k", "c_v", "c_z"]),
            cat(["d_q", "d_k", "d_v", "d_o"]))


SMALL_GATES = ("a_gate", "c_beta", "c_alpha", "d_i", "d_f")
GATE_BLOCK = 5
GATE_LANE = {"a_gate": 0, "c_beta": 12, "c_alpha": 16, "d_i": 20, "d_f": 24}
GROUP_SLAB_WIDTH = (768, 768, 1024, 1024)


def _take_columns(a, idx, width):
    cuts = [0] + [k + 1 for k in range(idx.size - 1) if idx[k + 1] != idx[k] + 1] + [idx.size]
    runs = [a[..., int(idx[s]):int(idx[e - 1]) + 1] for s, e in zip(cuts[:-1], cuts[1:])]
    return jnp.concatenate(runs + [jnp.zeros(a.shape[:-1] + (width - idx.size,), a.dtype)], axis=-1)


def split_in_weights(w, b):
    ws, bs = [], []
    for idx, width in zip(_group_columns(), GROUP_SLAB_WIDTH):
        ws.append(_take_columns(w, idx, width).astype(jnp.bfloat16))
        bs.append(_take_columns(b, idx, width)[:, None, :])
    return ws, bs


def t5_bucket(dist):
    n = jnp.maximum(dist, 0)
    max_exact = N_BUCKETS // 2
    nf = jnp.maximum(n, 1).astype(jnp.float32)
    large = max_exact + (jnp.log(nf / max_exact) / math.log(MAX_DISTANCE / max_exact)
                         * (N_BUCKETS - max_exact)).astype(jnp.int32)
    large = jnp.minimum(large, N_BUCKETS - 1)
    return jnp.where(n < max_exact, n, large)


def _nt_dot(a, b):
    return lax.dot_general(a, b, (((1,), (1,)), ((), ())), preferred_element_type=jnp.float32)


def _nsa_prep_kernel(q_ref, kvs_ref, kvw_ref, qg_ref, kg_ref, seg_ref,
                     qnt_ref, kxs_ref, v1s_ref, kkw_ref, v1w_ref):
    tm = q_ref.shape[0]
    q = q_ref[...]
    ss = _dot_sel(q * q, seg_ref[...], parts=3)
    qn = q * lax.rsqrt(ss * (1.0 / HEAD_DIM) + EPS) * qg_ref[...]
    qnt_ref[...] = (qn * LOG2E).T.astype(qnt_ref.dtype)
    lane = lax.broadcasted_iota(jnp.int32, (tm, 128), 1)
    lo = lane < HEAD_DIM

    def split(x):
        xr = pltpu.roll(x, HEAD_DIM, 1)
        ss = jnp.sum(jnp.where(lo, x * x, 0.0), axis=-1, keepdims=True)
        kk = jnp.where(lo, x, xr) * lax.rsqrt(ss * (1.0 / HEAD_DIM) + EPS) * kg_ref[...]
        return kk, jnp.where(lo, xr, 1.0)

    kk, v1 = split(kvs_ref[...])
    tok = pl.program_id(1) * tm + lax.broadcasted_iota(jnp.int32, (tm, 128), 0)
    onehot = jnp.where(lane == tok // SLC_BLOCK, 1.0, 0.0)
    kxs_ref[...] = jnp.concatenate([kk, onehot], axis=1).astype(kxs_ref.dtype)
    v1s_ref[...] = v1.T[0:V1_ROWS].astype(v1s_ref.dtype)
    kk, v1 = split(kvw_ref[...])
    kkw_ref[...] = kk.astype(kkw_ref.dtype)
    v1w_ref[...] = v1.T[0:V1_ROWS].astype(v1w_ref.dtype)


def nsa_prep(pa, qg, kg2, seg):
    B, T, _ = pa.shape
    tm = 512
    bf16 = jnp.bfloat16
    col = lambda w, j: pl.BlockSpec((None, tm, w), lambda b, i: (b, i, j))
    rowblk = lambda h: pl.BlockSpec((None, h, tm), lambda b, i: (b, 0, i))
    const = lambda shape: pl.BlockSpec(shape, lambda b, i: (0, 0))
    return pl.pallas_call(
        _nsa_prep_kernel,
        grid=(B, T // tm),
        in_specs=[col(256, 0), col(128, 3), col(128, 4), const((1, 256)), const((1, 128)), const((256, 256))],
        out_specs=[rowblk(256), col(256, 0), rowblk(V1_ROWS), col(128, 0), rowblk(V1_ROWS)],
        out_shape=[jax.ShapeDtypeStruct((B, 256, T), bf16),
                   jax.ShapeDtypeStruct((B, T, 256), bf16), jax.ShapeDtypeStruct((B, V1_ROWS, T), bf16),
                   jax.ShapeDtypeStruct((B, T, 128), bf16), jax.ShapeDtypeStruct((B, V1_ROWS, T), bf16)],
        name="nsa_prep",
    )(pa, pa, pa, qg, kg2, seg)


def _compress_kernel(kv_ref, pos_ref, w1_ref, w2k_ref, w2v_ref, kg_ref, kkc_ref, vct_ref):
    nrow = kkc_ref.shape[0]
    f32, bf16 = jnp.float32, jnp.bfloat16
    hidden = w1_ref.shape[2] // 2
    first = jnp.zeros((nrow, 2 * hidden), f32)
    second = jnp.zeros((nrow, 2 * hidden), f32)
    for p in range(CMP_STRIDE):
        xp = kv_ref[pl.ds(p, nrow, stride=CMP_STRIDE), :]
        first = first + jnp.dot((xp + pos_ref[p:p + 1, :]).astype(bf16), w1_ref[p], preferred_element_type=f32)
        second = second + jnp.dot((xp + pos_ref[CMP_STRIDE + p:CMP_STRIDE + p + 1, :]).astype(bf16),
                                  w1_ref[CMP_STRIDE + p], preferred_element_type=f32)
    hid = first + pltpu.roll(second, nrow - 1, 0)
    hid = (hid * jax.nn.sigmoid(hid)).astype(bf16)
    valid = lax.broadcasted_iota(jnp.int32, (nrow, 128), 0) < nrow - 1
    kc = jnp.dot(hid[:, 0:hidden], w2k_ref[...], preferred_element_type=f32)
    kc = kc * lax.rsqrt(jnp.mean(kc * kc, axis=-1, keepdims=True) + EPS) * kg_ref[...]
    kkc_ref[...] = jnp.where(valid, kc, 0.0).astype(kkc_ref.dtype)
    vc = jnp.dot(hid[:, hidden:2 * hidden], w2v_ref[...], preferred_element_type=f32)
    vct_ref[...] = jnp.where(valid, vc, 0.0).T[0:HEAD_DIM].astype(vct_ref.dtype)


def nsa_compress(pa, pos2, w1, w2k, w2v, kg2):
    B, T, _ = pa.shape
    nrow = T // CMP_STRIDE
    bf16 = jnp.bfloat16
    full = lambda a: pl.BlockSpec(a.shape, lambda b: (0,) * a.ndim)
    return pl.pallas_call(
        _compress_kernel,
        grid=(B,),
        in_specs=[pl.BlockSpec((None, T, 128), lambda b: (b, 0, 2)), full(pos2), full(w1), full(w2k), full(w2v),
                  full(kg2)],
        out_specs=[pl.BlockSpec((None, nrow, 128), lambda b: (b, 0, 0)),
                   pl.BlockSpec((None, HEAD_DIM, nrow), lambda b: (b, 0, 0))],
        out_shape=[jax.ShapeDtypeStruct((B, nrow, 128), bf16), jax.ShapeDtypeStruct((B, HEAD_DIM, nrow), bf16)],
        name="nsa_compress",
    )(pa, pos2, w1, w2k, w2v, kg2)


CMP_TILES_PER_STEP = 4


def _stack_heads_t(qt):
    lo = lax.broadcasted_iota(jnp.int32, (128, qt.shape[1]), 0) < HEAD_DIM
    zero = jnp.zeros((), qt.dtype)
    return jnp.concatenate([jnp.where(lo if h % 2 == 0 else ~lo, qt[128 * (h // 2):128 * (h // 2) + 128], zero)
                            for h in range(GROUP_HEADS)], axis=1)


def _cmp_select_kernel(qnt_ref, kkc_ref, vct_ref, cb_ref, ov_ref, ocmp_ref, pen_ref, sc_ref, *, n_sel):
    i = pl.program_id(1)
    n_tiles = pen_ref.shape[0]
    tq = qnt_ref.shape[1] // n_tiles
    nblk, nrow = ov_ref.shape
    keys_per_tile = tq // CMP_STRIDE
    f32, bf16 = jnp.float32, jnp.bfloat16
    blk = lax.broadcasted_iota(jnp.int32, (nblk, tq), 0)
    scores = []
    for g in range(n_tiles):
        cols = slice(g * tq, (g + 1) * tq)
        q4t = _stack_heads_t(qnt_ref[:, cols])
        first_row = pl.multiple_of(nrow - keys_per_tile * (i * n_tiles + g), keys_per_tile)
        s = jnp.dot(kkc_ref[...], q4t, preferred_element_type=f32) + cb_ref[pl.ds(first_row, nrow), :]
        m = jnp.max(s, axis=0, keepdims=True)
        e = jnp.where(s > 0.5 * NEG, jnp.exp2(s - m), 0.0)
        p = e * (1.0 / jnp.maximum(jnp.sum(e, axis=0, keepdims=True), 1e-30))
        o = jnp.dot(vct_ref[...], p.astype(bf16), preferred_element_type=f32)
        for h in range(GROUP_HEADS):
            ocmp_ref[h * HEAD_DIM:(h + 1) * HEAD_DIM, cols] = o[:, h * tq:(h + 1) * tq]
        psum = p[:, 0:tq] + p[:, tq:2 * tq] + p[:, 2 * tq:3 * tq] + p[:, 3 * tq:4 * tq]
        hi = psum.astype(bf16)
        lo_part = (psum - hi.astype(f32)).astype(bf16)
        score = (jnp.dot(ov_ref[...], hi, preferred_element_type=f32)
                 + jnp.dot(ov_ref[...], lo_part, preferred_element_type=f32))
        qpos = (i * n_tiles + g) * tq + lax.broadcasted_iota(jnp.int32, (nblk, tq), 1)
        cur = qpos // SLC_BLOCK
        forced = (blk == 0) | ((cur - blk >= 0) & (cur - blk < N_LOCAL_SLC))
        score = jnp.where(forced, FORCE, score)
        score = jnp.where(blk <= cur, score, -FORCE)
        sc_ref[g] = score
        scores.append(score)
    blocks_per_tile = tq // SLC_BLOCK

    def rank_step(t, cnts):
        out = []
        for g in range(n_tiles):
            cnt = cnts[g]
            for u in range(blocks_per_tile):
                jp = t * blocks_per_tile + u
                row = sc_ref[g, pl.ds(jp, 1), :]
                tie = jnp.where(blk > jp, 1.0, 0.0)
                cnt = cnt + jnp.where(row > scores[g], 1.0, jnp.where(row == scores[g], tie, 0.0))
            out.append(cnt)
        return tuple(out)

    zeros = tuple(jnp.zeros((nblk, tq), f32) for _ in range(n_tiles))
    cnts = lax.fori_loop(0, (i + 1) * n_tiles, rank_step, zeros)
    for g in range(n_tiles):
        pen = jnp.where(cnts[g] < n_sel, 0.0, NEG)
        if nblk < 128:
            pen = jnp.concatenate([pen, jnp.zeros((128 - nblk, tq), f32)], axis=0)
        pen_ref[g] = pen.astype(pen_ref.dtype)


def nsa_cmp_select(qnt, kkc, vct, cbias, ov, n_sel):
    B, _, T = qnt.shape
    nrow = kkc.shape[1]
    nblk = ov.shape[0]
    tq = Q_BLOCK
    g = min(CMP_TILES_PER_STEP, T // tq)
    return pl.pallas_call(
        functools.partial(_cmp_select_kernel, n_sel=n_sel),
        grid=(B, T // (g * tq)),
        in_specs=[pl.BlockSpec((None, 256, g * tq), lambda b, i: (b, 0, i)),
                  pl.BlockSpec((None, nrow, 128), lambda b, i: (b, 0, 0)),
                  pl.BlockSpec((None, HEAD_DIM, nrow), lambda b, i: (b, 0, 0)),
                  pl.BlockSpec((2 * nrow, 4 * tq), lambda b, i: (0, 0)),
                  pl.BlockSpec((nblk, nrow), lambda b, i: (0, 0))],
        out_specs=[pl.BlockSpec((None, 256, g * tq), lambda b, i: (b, 0, i)),
                   pl.BlockSpec((None, g, 128, tq), lambda b, i: (b, i, 0, 0))],
        out_shape=[jax.ShapeDtypeStruct((B, 256, T), jnp.float32),
                   jax.ShapeDtypeStruct((B, T // tq, 128, tq), jnp.bfloat16)],
        scratch_shapes=[pltpu.VMEM((g, nblk, tq), jnp.float32)],
        name="nsa_cmp_select",
    )(qnt, kkc, vct, cbias, ov)


NSA_KEY_TILE = 256
NSA_QUERY_TILE = 256


def _nsa_attn_kernel(qnt_ref, pen_ref, ocmp_ref, gate_ref, kxs_ref, v1s_ref, kkw_ref, v1w_ref,
                     tabs_ref, tabw_ref, mixg_ref, o_ref, acc_ref, m_ref, sa_ref, sb_ref):
    i = pl.program_id(1)
    tq = o_ref.shape[0]
    kt = tabs_ref.shape[1]
    f32, bf16 = jnp.float32, jnp.bfloat16
    q4t = _stack_heads_t(qnt_ref[...])
    pen = jnp.concatenate([pen_ref[t] for t in range(pen_ref.shape[0])], axis=1)
    rhs_sel = jnp.concatenate([q4t, jnp.concatenate([pen] * GROUP_HEADS, axis=1)], axis=0)

    def branch(rhs, k_ref, v_ref, tab_ref, first_tile, last_tile):
        n_tab = tab_ref.shape[0] - 1
        m_ref[...] = jnp.full(m_ref.shape, NEG, f32)
        acc_ref[...] = jnp.zeros(acc_ref.shape, f32)

        def key_offset(j):
            return pl.multiple_of(jnp.minimum(j, last_tile) * kt, kt)

        def scores(j):
            delta = jnp.where(j > last_tile, n_tab, jnp.minimum(i - j * (kt // tq), n_tab - 1))
            return jnp.dot(k_ref[pl.ds(key_offset(j), kt), :], rhs, preferred_element_type=f32) + tab_ref[delta]

        def absorb(s, j):
            m_old = m_ref[...]
            m_new = jnp.maximum(m_old, jnp.max(s, axis=0, keepdims=True))
            p = jnp.exp2(s - m_new)
            pv = jnp.dot(v_ref[:, pl.ds(key_offset(j), kt)], p.astype(bf16), preferred_element_type=f32)
            acc_ref[...] = jnp.exp2(m_old - m_new) * acc_ref[...] + pv
            m_ref[...] = m_new

        sa_ref[...] = scores(first_tile)

        def body(t, carry):
            j = first_tile + 2 * t
            sb_ref[...] = scores(j + 1)
            absorb(sa_ref[...], j)
            sa_ref[...] = scores(j + 2)
            absorb(sb_ref[...], j + 1)
            return carry

        lax.fori_loop(0, (last_tile - first_tile) // 2 + 1, body, 0)
        acc = acc_ref[...]
        return acc[0:HEAD_DIM] / jnp.maximum(acc[HEAD_DIM:HEAD_DIM + 1], 1e-30)

    def window_branch(rhs):
        span = tabw_ref.shape[1]
        back = span // tq - 1
        start = pl.multiple_of(jnp.maximum(i - back, 0) * tq, tq)
        s = jnp.dot(kkw_ref[pl.ds(start, span), :], rhs, preferred_element_type=f32) + tabw_ref[jnp.minimum(i, back)]
        p = jnp.exp2(s - jnp.max(s, axis=0, keepdims=True))
        acc = jnp.dot(v1w_ref[:, pl.ds(start, span)], p.astype(bf16), preferred_element_type=f32)
        return acc[0:HEAD_DIM] / jnp.maximum(acc[HEAD_DIM:HEAD_DIM + 1], 1e-30)

    last = (i * tq) // kt
    o_w = window_branch(q4t)
    o_s = branch(rhs_sel, kxs_ref, v1s_ref, tabs_ref, 0, last)
    gt = jax.nn.sigmoid(gate_ref[...]).T
    oct_ = ocmp_ref[...]
    parts = []
    for h in range(GROUP_HEADS):
        cols = slice(h * tq, (h + 1) * tq)
        parts.append(gt[3 * h:3 * h + 1] * oct_[h * HEAD_DIM:(h + 1) * HEAD_DIM]
                     + gt[3 * h + 1:3 * h + 2] * o_s[:, cols] + gt[3 * h + 2:3 * h + 3] * o_w[:, cols])
    yt = jnp.concatenate(parts, axis=0)
    yt = yt * lax.rsqrt(jnp.mean(yt * yt, axis=0, keepdims=True) + EPS) * mixg_ref[...]
    o_ref[...] = yt.T


def nsa_attention(qnt, pen, ocmp, pa, kxs, v1s, kkw, v1w, tab_s, tab_w, mixg):
    B, _, T = qnt.shape
    tq = min(NSA_QUERY_TILE, T)
    tile = lambda w, j: pl.BlockSpec((None, tq, w), lambda b, i: (b, i, j))
    whole = lambda w: pl.BlockSpec((None, T, w), lambda b, i: (b, 0, 0))
    whole_t = pl.BlockSpec((None, V1_ROWS, T), lambda b, i: (b, 0, 0))
    full = lambda a: pl.BlockSpec(a.shape, lambda b, i: (0,) * a.ndim)
    return pl.pallas_call(
        _nsa_attn_kernel,
        grid=(B, T // tq),
        in_specs=[pl.BlockSpec((None, 256, tq), lambda b, i: (b, 0, i)),
                  pl.BlockSpec((None, tq // Q_BLOCK, 128, Q_BLOCK), lambda b, i: (b, i, 0, 0)),
                  pl.BlockSpec((None, 256, tq), lambda b, i: (b, 0, i)), tile(128, 5),
                  whole(256), whole_t, whole(128), whole_t, full(tab_s), full(tab_w), full(mixg)],
        out_specs=tile(256, 0),
        out_shape=jax.ShapeDtypeStruct((B, T, 256), jnp.float32),
        scratch_shapes=[pltpu.VMEM((V1_ROWS, GROUP_HEADS * tq), jnp.float32),
                        pltpu.VMEM((1, GROUP_HEADS * tq), jnp.float32),
                        pltpu.VMEM((tab_s.shape[1], GROUP_HEADS * tq), jnp.float32),
                        pltpu.VMEM((tab_s.shape[1], GROUP_HEADS * tq), jnp.float32)],
        compiler_params=pltpu.CompilerParams(dimension_semantics=("arbitrary", "arbitrary"),
                                             vmem_limit_bytes=VMEM_LIMIT_BYTES),
        name="nsa_attention",
    )(qnt, pen, ocmp, pa, kxs, v1s, kkw, v1w, tab_s, tab_w, mixg)


def _toeplitz(wr, rows, cols):
    H, n = wr.shape
    flat = jnp.tile(jnp.pad(wr, ((0, 0), (0, 1))), (1, rows))[:, :rows * n]
    return flat.reshape(H, rows, n)[:, :, rows - 1:rows - 1 + cols]


def nsa_bias_tables(t5_table, T):
    H = GROUP_HEADS
    tq = Q_BLOCK
    ta = min(NSA_QUERY_TILE, T)
    kt = min(NSA_KEY_TILE, T)
    nrow = T // CMP_STRIDE
    dmax = WINDOW + 2 * ta + kt
    onehot = (t5_bucket(jnp.arange(dmax))[:, None] == jnp.arange(N_BUCKETS)[None, :]).astype(jnp.float32)
    by_dist = jnp.dot(onehot, t5_table, precision=HIGHEST).T
    neg = lambda n: jnp.full((H, n), NEG, jnp.float32)
    f_s = jnp.concatenate([neg(kt - 1), by_dist * LOG2E], axis=1)
    span = ta + kt - 1

    def tiles(f, n):
        out = [_toeplitz(f[:, k * ta:k * ta + span], kt, ta).transpose(1, 0, 2).reshape(kt, H * ta) for k in range(n)]
        return jnp.stack(out + [jnp.full((kt, H * ta), NEG, jnp.float32)])

    n_s = -(-(kt + MAX_DISTANCE) // ta) + 1
    tab_s = tiles(f_s, n_s)
    n_w = WINDOW // ta + 1
    f_w = jnp.concatenate([neg(kt - 1), by_dist[:, :WINDOW] * LOG2E, neg(dmax - WINDOW)], axis=1)
    w_tiles = tiles(f_w, n_w)
    tab_w = jnp.stack([jnp.concatenate([w_tiles[k - m if k >= m else n_w] for m in range(n_w)], axis=0)
                       for k in range(n_w)])
    per_tile = tq // CMP_STRIDE
    d0 = -CMP_STRIDE * (nrow - 1) - (CMP_BLOCK - 1)
    n16 = 2 * nrow + per_tile - 1
    n_pos = CMP_STRIDE * n16 + d0
    far = jnp.broadcast_to(by_dist[:, -1:], (H, max(n_pos - dmax, 0)))
    w16 = jnp.concatenate([neg(-d0), by_dist[:, :n_pos] * LOG2E, far * LOG2E], axis=1).reshape(H, n16, CMP_STRIDE)
    w16 = w16[:, ::-1, :]
    base = jnp.stack([w16[:, per_tile - 1 - a:per_tile - 1 - a + 2 * nrow, :] for a in range(per_tile)], axis=1)
    cbias = base.transpose(0, 1, 3, 2).reshape(H * tq, 2 * nrow).T
    return tab_s, tab_w, cbias


def nsa_overlap_t(T):
    nrow = T // CMP_STRIDE
    nblk = T // SLC_BLOCK
    ci = np.arange(nrow)[None, :]
    bj = np.arange(nblk)[:, None]
    ov = (ci * CMP_STRIDE < (bj + 1) * SLC_BLOCK) & (ci * CMP_STRIDE + CMP_BLOCK > bj * SLC_BLOCK)
    ov = ov & (ci < nrow - 1)
    return jnp.asarray(ov.astype(np.float32), dtype=jnp.bfloat16)


def nsa_mixer_pallas(pa, q_norm_g, k_norm_g, cmp_pos, cmp_k_w1, cmp_k_w2, cmp_v_w1, cmp_v_w2,
                     tables, ovt, mixg):
    B, T, _ = pa.shape
    bf16 = jnp.bfloat16
    tab_s, tab_w, cbias = tables
    qg = (jnp.tile(q_norm_g, GROUP_HEADS) * HEAD_DIM ** -0.5)[None, :]
    kg2 = jnp.tile(k_norm_g, 2)[None, :]
    qnt, kxs, v1s, kkw, v1w = nsa_prep(pa, qg, kg2, _head_mask().astype(bf16))
    dup = lambda w: jnp.concatenate([w, w], axis=1)
    zeros = jnp.zeros((CMP_BLOCK, HEAD_DIM, cmp_k_w1.shape[1]), jnp.float32)
    per_pos = lambda w: w.reshape(CMP_BLOCK, HEAD_DIM, -1)
    w1 = jnp.concatenate([jnp.concatenate([per_pos(cmp_k_w1), zeros], axis=2),
                          jnp.concatenate([zeros, per_pos(cmp_v_w1)], axis=2)], axis=1).astype(bf16)
    kkc, vct = nsa_compress(pa, dup(cmp_pos), w1, dup(cmp_k_w2).astype(bf16), dup(cmp_v_w2).astype(bf16), kg2)
    n_sel = min(N_SLC, T // SLC_BLOCK)
    ocmp, pen = nsa_cmp_select(qnt, kkc, vct, cbias, ovt, n_sel)
    mixg_col = jnp.broadcast_to(mixg.reshape(-1, 1), (GROUP_WIDTH, min(NSA_QUERY_TILE, T)))
    return nsa_attention(qnt, pen, ocmp, pa, kxs, v1s, kkw, v1w, tab_s, tab_w, mixg_col)


CHUNKS_PER_STEP = 4
SEQS_PER_STEP = 2


def _tn_dot(a, b):
    return lax.dot_general(a, b, (((0,), (0,)), ((), ())), preferred_element_type=jnp.float32)


def _split_bf16(a, parts):
    out, rest = [], a
    for _ in range(parts):
        piece = rest.astype(jnp.bfloat16)
        out.append(piece)
        rest = rest - piece.astype(jnp.float32)
    return out


def _dot_sel(a, sel, parts=3):
    return sum(jnp.dot(p, sel, preferred_element_type=jnp.float32) for p in _split_bf16(a, parts))


def _sel_dot(sel, a, parts=3):
    return sum(jnp.dot(sel, p, preferred_element_type=jnp.float32) for p in _split_bf16(a, parts))


def _softplus(x):
    return jnp.maximum(x, 0.0) + jnp.log(1.0 + jnp.exp(-jnp.abs(x)))


def _block_diag(a, hmb):
    return jnp.concatenate([a.astype(jnp.bfloat16)] * GROUP_HEADS, axis=0) * hmb


def _diag_row(a, eye):
    return jnp.sum(a * eye, axis=0, keepdims=True)


def _head_norm(o, hmb):
    return o * lax.rsqrt(_dot_sel(o * o, hmb, parts=2) * (1.0 / HEAD_DIM) + EPS)


def _head_l2(o, hmb):
    return o * lax.rsqrt(_dot_sel(o * o, hmb, parts=2) + EPS)


def _xform_masks(L):
    i = np.arange(L)[:, None]
    j = np.arange(L)[None, :]
    masks = []
    s = 1
    while s < L:
        masks.append((i // (2 * s) == j // (2 * s)) & (i % (2 * s) >= s) & (j % (2 * s) < s))
        s *= 2
    masks += [i == j, j <= i, j < i]
    return jnp.asarray(np.tile(np.stack(masks).astype(np.float32), (1, 1, GROUP_HEADS)))


def _expand_matrix(first_lane):
    e = np.zeros((128, GROUP_WIDTH), np.float32)
    for h in range(GROUP_HEADS):
        e[first_lane + h, h * HEAD_DIM:(h + 1) * HEAD_DIM] = 1.0
    return jnp.asarray(e, jnp.bfloat16)


def _head_mask():
    return jnp.asarray(np.kron(np.eye(GROUP_HEADS), np.ones((HEAD_DIM, HEAD_DIM))), jnp.float32)


def _per_head_rows(*vecs):
    rows = [jnp.repeat(v, HEAD_DIM) if v.shape[0] == GROUP_HEADS else jnp.tile(v, GROUP_HEADS) for v in vecs]
    return jnp.pad(jnp.stack(rows), ((0, 8 - len(rows)), (0, 0)))


def _chunk_consts(L, n_chunks):
    tri = np.kron(np.eye(n_chunks), np.tril(np.ones((L, L)))).astype(np.float32)
    return _head_mask(), jnp.asarray(tri, jnp.bfloat16), _xform_masks(L)


def _const_spec(a):
    return pl.BlockSpec(a.shape, lambda b, t: (0,) * a.ndim)


def _gdn_kernel(pc_ref, gates_ref, hp_ref, eb_ref, ea_ref, hm_ref, tri_ref, lv_ref, o_ref, s_ref):
    L = GDN_CHUNK
    n_seq, rows = pc_ref.shape[0], pc_ref.shape[1]
    f32, bf16 = jnp.float32, jnp.bfloat16
    n_lev = lv_ref.shape[0] - 3

    @pl.when(pl.program_id(1) == 0)
    def _():
        s_ref[...] = jnp.zeros(s_ref.shape, f32)

    hm = hm_ref[...]
    hmb = hm.astype(bf16)
    eye, incl, strict = lv_ref[n_lev], lv_ref[n_lev + 1], lv_ref[n_lev + 2]

    q_c, k_c, v_c, beta_c, gc_c = [], [], [], [], []
    for b in range(n_seq):
        qkv = pc_ref[b, :, 0:768]
        q_all = _head_l2(qkv[:, 0:256], hmb) * HEAD_DIM ** -0.5
        k_all = _head_l2(qkv[:, 256:512], hmb)
        tail = gates_ref[b]
        beta_all = jax.nn.sigmoid(_dot_sel(tail, eb_ref[...]))
        g = hp_ref[0:1, :] * _softplus(_dot_sel(tail, ea_ref[...]) + hp_ref[1:2, :])
        gc_all = _sel_dot(tri_ref[...], g)
        for c in range(rows // L):
            sl = slice(c * L, (c + 1) * L)
            q_c.append(q_all[sl])
            k_c.append(k_all[sl])
            v_c.append(qkv[sl, 512:768])
            beta_c.append(beta_all[sl])
            gc_c.append(gc_all[sl])
    chains = range(len(q_c))
    egc = [jnp.exp(gc) for gc in gc_c]
    kb = [k_c[i] * beta_c[i] for i in chains]
    a_mat, qk = [], []
    for i in chains:
        seg = jnp.exp(jnp.where(incl > 0.5, gc_c[i] - _diag_row(gc_c[i], eye), NEG))
        k_bd = _block_diag(k_c[i], hmb)
        a_mat.append(_nt_dot(kb[i].astype(bf16), k_bd) * seg * strict)
        qk.append((_nt_dot(q_c[i].astype(bf16), k_bd) * seg).astype(bf16))
    t_inv = [eye - a * lv_ref[0] for a in a_mat]
    for lev in range(1, n_lev):
        te = [jnp.dot(t_inv[i].astype(bf16), _block_diag(a_mat[i] * lv_ref[lev], hmb),
                      preferred_element_type=f32) for i in chains]
        t_inv = [t_inv[i] - jnp.dot(te[i].astype(bf16), _block_diag(t_inv[i], hmb), preferred_element_type=f32)
                 for i in chains]
    parts = []
    for i in chains:
        tb = t_inv[i].astype(bf16)
        u = jnp.dot(tb, _block_diag(v_c[i] * beta_c[i], hmb), preferred_element_type=f32)
        w = jnp.dot(tb, _block_diag(kb[i] * egc[i], hmb), preferred_element_type=f32)
        k_dec = k_c[i] * jnp.exp(gc_c[i][L - 1:L, :] - gc_c[i])
        parts.append((u, w.astype(bf16), (q_c[i] * egc[i]).astype(bf16), qk[i], k_dec.astype(bf16),
                      egc[i][L - 1:L, :]))
    per_seq = rows // L
    state = [s_ref[b] for b in range(n_seq)]
    outs = [[] for _ in range(n_seq)]
    for c in range(per_seq):
        for b in range(n_seq):
            u, w, q_dec, qk_c, k_dec, g_tot = parts[b * per_seq + c]
            sb = state[b].astype(bf16)
            v_new = u - jnp.dot(w, sb, preferred_element_type=f32)
            outs[b].append(jnp.dot(q_dec, sb, preferred_element_type=f32)
                           + jnp.dot(qk_c, _block_diag(v_new, hmb), preferred_element_type=f32))
            state[b] = state[b] * g_tot + _tn_dot(k_dec, v_new.astype(bf16)) * hm
    for b in range(n_seq):
        s_ref[b] = state[b]
        o = jnp.concatenate(outs[b], axis=0)
        z = pc_ref[b, :, 768:1024]
        o_ref[b] = _head_norm(o, hmb) * hp_ref[2:3, :] * (z * jax.nn.sigmoid(z))


def _mlstm_kernel(pd_ref, gates_ref, hp_ref, ei_ref, ef_ref, hm_ref, tri_ref, lv_ref, o_ref, c_ref, n_ref, m_ref):
    L = MLSTM_CHUNK
    n_seq, rows = pd_ref.shape[0], pd_ref.shape[1]
    f32, bf16 = jnp.float32, jnp.bfloat16
    n_lev = lv_ref.shape[0] - 3

    @pl.when(pl.program_id(1) == 0)
    def _():
        c_ref[...] = jnp.zeros(c_ref.shape, f32)
        n_ref[...] = jnp.zeros(n_ref.shape, f32)
        m_ref[...] = jnp.zeros(m_ref.shape, f32)

    hm = hm_ref[...]
    hmb = hm.astype(bf16)
    eye, incl = lv_ref[n_lev], lv_ref[n_lev + 1]
    head_of_lane = lax.broadcasted_iota(jnp.int32, (L, GROUP_WIDTH), 1) // HEAD_DIM

    def head_max(a):
        out = jnp.zeros(a.shape, f32)
        for h in range(GROUP_HEADS):
            mine = head_of_lane == h
            out = jnp.where(mine, jnp.max(jnp.where(mine, a, NEG), axis=-1, keepdims=True), out)
        return out

    per_seq = rows // L
    parts = []
    for b in range(n_seq):
        tail = gates_ref[b]
        log_i_all = _dot_sel(tail, ei_ref[...])
        log_f = -_softplus(-(_dot_sel(tail, ef_ref[...]) + hp_ref[0:1, :]))
        b_all = _sel_dot(tri_ref[...], log_f)
        for c in range(per_seq):
            sl = slice(c * L, (c + 1) * L)
            q = pd_ref[b, sl, 0:256]
            k = pd_ref[b, sl, 256:512] * HEAD_DIM ** -0.5
            v = pd_ref[b, sl, 512:768]
            bc, log_i = b_all[sl], log_i_all[sl]
            log_w = jnp.where(incl > 0.5, bc - _diag_row(bc - log_i, eye), NEG)
            b_last = bc[L - 1:L, :]
            log_w_end = b_last - bc + log_i
            qk = _nt_dot(q.astype(bf16), _block_diag(k, hmb))
            parts.append((q, k, v, bc, log_w, head_max(log_w), b_last, log_w_end,
                          jnp.max(log_w_end, axis=0, keepdims=True), qk))
    c_state = [c_ref[b] for b in range(n_seq)]
    n_state = [n_ref[b] for b in range(n_seq)]
    m_prev = [m_ref[b] for b in range(n_seq)]
    outs = [[] for _ in range(n_seq)]
    for c in range(per_seq):
        for b in range(n_seq):
            q, k, v, bc, log_w, m_intra, b_last, log_w_end, m_end, qk = parts[b * per_seq + c]
            log_inter = bc + m_prev[b]
            m_t = jnp.maximum(log_inter, m_intra)
            w_inter = jnp.exp(log_inter - m_t)
            s = qk * jnp.exp(log_w - m_t)
            num = w_inter * jnp.dot(q.astype(bf16), c_state[b].astype(bf16), preferred_element_type=f32)
            num = num + jnp.dot(s.astype(bf16), _block_diag(v, hmb), preferred_element_type=f32)
            den = w_inter * _dot_sel(q * n_state[b], hmb, parts=2) + _dot_sel(s, hmb, parts=2)
            outs[b].append(num / jnp.maximum(jnp.abs(den), jnp.exp(-m_t)))
            m_new = jnp.maximum(b_last + m_prev[b], m_end)
            w_old = jnp.exp(b_last + m_prev[b] - m_new)
            kw = k * jnp.exp(log_w_end - m_new)
            c_state[b] = w_old * c_state[b] + _tn_dot(kw.astype(bf16), v.astype(bf16)) * hm
            n_state[b] = w_old * n_state[b] + jnp.sum(kw, axis=0, keepdims=True)
            m_prev[b] = m_new
    for b in range(n_seq):
        c_ref[b] = c_state[b]
        n_ref[b] = n_state[b]
        m_ref[b] = m_prev[b]
        h = jnp.concatenate(outs[b], axis=0)
        o_ref[b] = _head_norm(h, hmb) * hp_ref[1:2, :] * jax.nn.sigmoid(pd_ref[b, :, 768:1024])


def _recurrent_mixers_kernel(pc_ref, pd_ref, gates_ref, hpc_ref, eb_ref, ea_ref, hpd_ref, ei_ref, ef_ref,
                             hm_ref, tri_ref, lv_ref, oc_ref, od_ref, s_ref, c_ref, n_ref, m_ref):
    _gdn_kernel(pc_ref, gates_ref, hpc_ref, eb_ref, ea_ref, hm_ref, tri_ref, lv_ref, oc_ref, s_ref)
    _mlstm_kernel(pd_ref, gates_ref, hpd_ref, ei_ref, ef_ref, hm_ref, tri_ref, lv_ref, od_ref, c_ref, n_ref, m_ref)


def recurrent_mixers_pallas(pc, pd, pa, A_log, dt_bias, gdn_norm_g, f_bias, mlstm_norm_g):
    B, T, width = pc.shape
    L = GDN_CHUNK
    assert MLSTM_CHUNK == L and pd.shape == pc.shape
    rows = L * min(CHUNKS_PER_STEP, T // L)
    n_seq = SEQS_PER_STEP if B % SEQS_PER_STEP == 0 else 1
    n = GROUP_HEADS * HEAD_DIM
    expand = lambda name: _expand_matrix(GATE_LANE[name])
    consts = ((_per_head_rows(-jnp.exp(A_log), dt_bias, gdn_norm_g), expand("c_beta"), expand("c_alpha"),
               _per_head_rows(f_bias, mlstm_norm_g), expand("d_i"), expand("d_f")) + _chunk_consts(L, rows // L))
    slab = pl.BlockSpec((n_seq, rows, width), lambda b, t: (b, t, 0))
    out = pl.BlockSpec((n_seq, rows, GROUP_WIDTH), lambda b, t: (b, t, 0))
    f32 = jnp.float32
    return pl.pallas_call(
        _recurrent_mixers_kernel,
        grid=(B // n_seq, T // rows),
        in_specs=[slab, slab, pl.BlockSpec((n_seq, rows, 128), lambda b, t: (b, t, GATE_BLOCK))]
                 + [_const_spec(a) for a in consts],
        out_specs=[out, out],
        out_shape=[jax.ShapeDtypeStruct((B, T, GROUP_WIDTH), f32), jax.ShapeDtypeStruct((B, T, GROUP_WIDTH), f32)],
        scratch_shapes=[pltpu.VMEM((n_seq, n, n), f32), pltpu.VMEM((n_seq, n, n), f32),
                        pltpu.VMEM((n_seq, 1, GROUP_WIDTH), f32), pltpu.VMEM((n_seq, 1, GROUP_WIDTH), f32)],
        compiler_params=pltpu.CompilerParams(dimension_semantics=("arbitrary", "arbitrary"),
                                             vmem_limit_bytes=VMEM_LIMIT_BYTES),
        name="recurrent_mixers",
    )(pc, pd, pa, *consts)


def kernel(x, c, ada_w, ada_b, norm_g, ffn1_w13, ffn1_w2, ffn2_w13, ffn2_w2, w_in, b_in, q_norm_g, k_norm_g, cmp_pos, cmp_k_w1, cmp_k_w2, cmp_v_w1, cmp_v_w2, t5_table, sc_conv_w, gdn_conv_w, gdn_A_log, gdn_dt_bias, gdn_norm_g, mlstm_f_bias, mlstm_norm_g, mix_norm_g, w_out):
    B, T, D = x.shape
    depth = ada_w.shape[0]
    bf16 = jnp.bfloat16
    x2 = x.reshape(B * T, D)
    tables = nsa_bias_tables(t5_table, T)
    ovt = nsa_overlap_t(T)
    ffn_w = [w.astype(bf16) for w in (ffn1_w13, ffn1_w2, ffn2_w13, ffn2_w2, w_out)]
    in_ws, in_bs = split_in_weights(w_in, b_in)
    for l in range(depth):
        mod = ada_modulation(c, ada_w, ada_b[l][None, :], l)
        mod = mod.reshape(B, N_SUBLAYERS, 3, 1, D)
        x2 = ffn_half_step(x2, norm_g[l, 0][None, :], mod[:, 0, 0], mod[:, 0, 1], mod[:, 0, 2],
                           ffn_w[0], ffn_w[1], l, T)
        pa, y_b, pc, pd = in_projection(x2, norm_g[l, 1][None, :], mod[:, 1, 0], mod[:, 1, 1], in_ws, in_bs, l,
                                        sc_conv_w[l], mix_norm_g[l, 1][None, :], gdn_conv_w[l], T)
        pa, pc, pd = (a.reshape(B, T, -1) for a in (pa, pc, pd))
        y_a = nsa_mixer_pallas(pa, q_norm_g[l], k_norm_g[l], cmp_pos[l], cmp_k_w1[l], cmp_k_w2[l],
                               cmp_v_w1[l], cmp_v_w2[l], tables, ovt, mix_norm_g[l, 0][None, :])
        y_c, y_d = recurrent_mixers_pallas(pc, pd, pa, gdn_A_log[l], gdn_dt_bias[l], gdn_norm_g[l],
                                           mlstm_f_bias[l], mlstm_norm_g[l])
        ys = [y.reshape(B * T, GROUP_WIDTH) for y in (y_a, y_b, y_c, y_d)]
        x2 = ffn_half_step(x2, norm_g[l, 2][None, :], mod[:, 2, 0], mod[:, 2, 1], mod[:, 2, 2],
                           ffn_w[2], ffn_w[3], l, T, mix=(ys, mod[:, 1, 2], ffn_w[4]))
    return x2.reshape(B, T, D)
```

```python
import functools
import math

import jax
import jax.numpy as jnp
import numpy as np
from jax import lax
from jax.experimental import pallas as pl
from jax.experimental.pallas import tpu as pltpu

HEAD_DIM = 64
GROUP_HEADS = 4
GROUP_WIDTH = GROUP_HEADS * HEAD_DIM
CMP_STRIDE = 16
CMP_BLOCK = 32
SLC_BLOCK = 64
N_SLC = 16
N_LOCAL_SLC = 2
WINDOW = 512
Q_BLOCK = 128
FORCE = 1e6
N_BUCKETS = 32
MAX_DISTANCE = 128
GDN_CHUNK = 64
MLSTM_CHUNK = 64
N_SUBLAYERS = 3
EPS = 1e-6

IN_LAYOUT = (
    ("a_q", 256), ("a_k_cmp", 64), ("a_v_cmp", 64),
    ("a_k_slc", 64), ("a_v_slc", 64), ("a_k_win", 64), ("a_v_win", 64),
    ("a_gate", 12),
    ("b_b", 256), ("b_c", 256), ("b_x", 256),
    ("c_q", 256), ("c_k", 256), ("c_v", 256),
    ("c_beta", 4), ("c_alpha", 4), ("c_z", 256),
    ("d_q", 256), ("d_k", 256), ("d_v", 256),
    ("d_i", 4), ("d_f", 4), ("d_o", 256),
)

VMEM_LIMIT_BYTES = 56 * 1024 * 1024
FFN_TOKEN_TILE = 512
FFN_CHUNK = 256
PROJ_TOKEN_TILE = 512
NEG = -1e30
HIGHEST = lax.Precision.HIGHEST
LOG2E = math.log2(math.e)
V1_ROWS = HEAD_DIM + 16


def _modulated_norm(x, g, scale, shift):
    y = x * lax.rsqrt(jnp.mean(x * x, axis=-1, keepdims=True) + EPS)
    return (y * g) * (1.0 + scale) + shift


def _ada_kernel(c_ref, w_ref, b_ref, o_ref):
    c = c_ref[...]
    cond = c * jax.nn.sigmoid(c)
    o_ref[...] = jnp.dot(cond.astype(jnp.bfloat16), w_ref[...].astype(jnp.bfloat16),
                         preferred_element_type=jnp.float32) + b_ref[...]


def ada_modulation(c, w, b, layer):
    B, D = c.shape
    N = w.shape[2]
    tn = 1152
    return pl.pallas_call(
        _ada_kernel,
        grid=(N // tn,),
        in_specs=[pl.BlockSpec((B, D), lambda j: (0, 0)),
                  pl.BlockSpec((None, D, tn), lambda j: (layer, 0, j)),
                  pl.BlockSpec((1, tn), lambda j: (0, j))],
        out_specs=pl.BlockSpec((B, tn), lambda j: (0, j)),
        out_shape=jax.ShapeDtypeStruct((B, N), jnp.float32),
        name="ada_modulation",
    )(c, w, b)


def _ffn_kernel(g_ref, gate_ref, w1_ref, w3_ref, w2_ref, *rest, n_mix):
    o_ref, act_ref, xc_ref, h_ref = rest[-4:]
    n_in = 3 + n_mix + (1 if n_mix else 0)
    cur, nxt = rest[:n_in], rest[n_in:2 * n_in]
    wo_ref = rest[2 * n_in] if n_mix else None

    def prologue(x_ref, shift_ref, scale_ref, *mix_refs):
        x = x_ref[...]
        if n_mix:
            z = None
            for k, y_ref in enumerate(mix_refs[:n_mix]):
                part = jnp.dot(y_ref[...].astype(jnp.bfloat16), wo_ref[k * GROUP_WIDTH:(k + 1) * GROUP_WIDTH, :],
                               preferred_element_type=jnp.float32)
                z = part if z is None else z + part
            x = x + mix_refs[n_mix][0] * z
        return x, _modulated_norm(x, g_ref[...], scale_ref[0], shift_ref[0]).astype(jnp.bfloat16)

    step = pl.program_id(0)
    slot = step % 2

    @pl.when(step == 0)
    def _():
        xc_ref[0], h_ref[0] = prologue(*cur)

    xc_ref[1 - slot], h_ref[1 - slot] = prologue(*nxt)
    x = xc_ref[slot]
    h = h_ref[slot]
    n_chunks = act_ref.shape[1] // FFN_CHUNK
    for ci in range(n_chunks):
        cs = slice(ci * FFN_CHUNK, (ci + 1) * FFN_CHUNK)
        a = jnp.dot(h, w1_ref[:, cs], preferred_element_type=jnp.float32)
        b = jnp.dot(h, w3_ref[:, cs], preferred_element_type=jnp.float32)
        act_ref[:, cs] = (a * jax.nn.sigmoid(a) * b).astype(jnp.bfloat16)
    y = jnp.dot(act_ref[...], w2_ref[...], preferred_element_type=jnp.float32)
    o_ref[...] = x + (0.5 * gate_ref[0]) * y


def ffn_half_step(x2, g, shift, scale, gate, w13, w2, layer, tokens_per_batch, mix=None):
    M, D = x2.shape
    F = w2.shape[1]
    tm = FFN_TOKEN_TILE
    n_tiles = M // tm
    tiles_per_batch = tokens_per_batch // tm
    resident = dict(pipeline_mode=pl.Buffered(1))
    ys, mix_gate, w_out = mix if mix is not None else ((), None, None)

    def tile_inputs(tile_of):
        mod = pl.BlockSpec((1, 1, D), lambda i: (tile_of(i) // tiles_per_batch, 0, 0))
        specs = [pl.BlockSpec((tm, D), lambda i: (tile_of(i), 0)), mod, mod]
        specs += [pl.BlockSpec((tm, GROUP_WIDTH), lambda i: (tile_of(i), 0)) for _ in ys]
        return specs + ([mod] if ys else [])

    tile_args = (x2, shift, scale) + tuple(ys) + ((mix_gate,) if ys else ())
    this_tile = lambda i: i
    next_tile = lambda i: jnp.minimum(i + 1, n_tiles - 1)
    return pl.pallas_call(
        functools.partial(_ffn_kernel, n_mix=len(ys)),
        grid=(n_tiles,),
        in_specs=[pl.BlockSpec((1, D), lambda i: (0, 0)),
                  pl.BlockSpec((1, 1, D), lambda i: (i // tiles_per_batch, 0, 0)),
                  pl.BlockSpec((None, D, F), lambda i: (layer, 0, 0), **resident),
                  pl.BlockSpec((None, D, F), lambda i: (layer, 0, 1), **resident),
                  pl.BlockSpec((None, F, D), lambda i: (layer, 0, 0), **resident)]
                 + tile_inputs(this_tile) + tile_inputs(next_tile)
                 + ([pl.BlockSpec((None,) + w_out.shape[1:], lambda i: (layer, 0, 0), **resident)] if ys else []),
        out_specs=pl.BlockSpec((tm, D), lambda i: (i, 0)),
        out_shape=jax.ShapeDtypeStruct((M, D), jnp.float32),
        scratch_shapes=[pltpu.VMEM((tm, F), jnp.bfloat16), pltpu.VMEM((2, tm, D), jnp.float32),
                        pltpu.VMEM((2, tm, D), jnp.bfloat16)],
        compiler_params=pltpu.CompilerParams(dimension_semantics=("arbitrary",),
                                             vmem_limit_bytes=VMEM_LIMIT_BYTES),
        name="ffn_half_step",
    )(g, gate, w13, w13, w2, *tile_args, *tile_args, *((w_out,) if ys else ()))


CONV_CARRY_ROWS = 8
CONV_ROW_CHUNK = 128


def _in_proj_kernel(x_ref, g_ref, shift_ref, scale_ref, wa_ref, wb_ref, wc_ref, wd_ref,
                    ba_ref, bb_ref, bc_ref, bd_ref, scw_ref, scg_ref, gcw_ref,
                    oa_ref, yb_ref, oc_ref, od_ref, braw_ref, craw_ref, *, tiles_per_batch):
    f32 = jnp.float32
    rows = x_ref.shape[0]
    hist = CONV_CARRY_ROWS
    first = pl.program_id(0) % tiles_per_batch == 0

    @pl.when(first)
    def _():
        braw_ref[0:hist, :] = jnp.zeros((hist, braw_ref.shape[1]), f32)
        craw_ref[0:hist, :] = jnp.zeros((hist, craw_ref.shape[1]), f32)

    @pl.when(jnp.logical_not(first))
    def _():
        braw_ref[0:hist, :] = braw_ref[rows:rows + hist, :]
        craw_ref[0:hist, :] = craw_ref[rows:rows + hist, :]

    h = _modulated_norm(x_ref[...], g_ref[...], scale_ref[0], shift_ref[0]).astype(jnp.bfloat16)
    project = lambda w_ref, b_ref: jnp.dot(h, w_ref[...], preferred_element_type=f32) + b_ref[...]
    pb = project(wb_ref, bb_ref)
    yb_ref[...] = pb[:, 0:256]
    braw_ref[hist:hist + rows, :] = pb[:, 256:512] * pb[:, 512:768]
    pc = project(wc_ref, bc_ref)
    craw_ref[hist:hist + rows, :] = pc[:, 0:768]
    oc_ref[:, 768:] = pc[:, 768:]
    n_chunks = rows // CONV_ROW_CHUNK
    pieces = []
    for w_ref, b_ref, o_ref in ((wa_ref, ba_ref, oa_ref), (wd_ref, bd_ref, od_ref)):
        half = w_ref.shape[1] // 256 // 2 * 256
        pieces += [(w_ref, b_ref, o_ref, 0, half), (w_ref, b_ref, o_ref, half, w_ref.shape[1])]
    for ci, r0 in enumerate(range(0, rows, CONV_ROW_CHUNK)):
        for w_ref, b_ref, o_ref, c0, c1 in pieces[ci * len(pieces) // n_chunks:(ci + 1) * len(pieces) // n_chunks]:
            o_ref[:, c0:c1] = jnp.dot(h, w_ref[:, c0:c1], preferred_element_type=f32) + b_ref[:, c0:c1]
        out_rows = slice(r0, r0 + CONV_ROW_CHUNK)
        tap = lambda ref, s: ref[pl.ds(hist + r0 - s, CONV_ROW_CHUNK), :]
        conv = sum(tap(braw_ref, s) * scw_ref[2 - s:3 - s, :] for s in range(3))
        y = yb_ref[out_rows, :] * conv
        yb_ref[out_rows, :] = y * lax.rsqrt(jnp.mean(y * y, axis=-1, keepdims=True) + EPS) * scg_ref[...]
        for c0 in range(0, 3 * GROUP_WIDTH, GROUP_WIDTH):
            cols = slice(c0, c0 + GROUP_WIDTH)
            conv = sum(craw_ref[pl.ds(hist + r0 - s, CONV_ROW_CHUNK), cols] * gcw_ref[3 - s:4 - s, cols]
                       for s in range(4))
            oc_ref[out_rows, cols] = conv * jax.nn.sigmoid(conv)


def in_projection(x2, g, shift, scale, ws, bs, layer, sc_conv_w, sc_norm_g, gdn_conv_w, tokens_per_batch):
    M, D = x2.shape
    tm = PROJ_TOKEN_TILE
    tiles_per_batch = tokens_per_batch // tm
    mod_spec = pl.BlockSpec((1, 1, D), lambda i: (i // tiles_per_batch, 0, 0))
    const = lambda a, **kw: pl.BlockSpec(a.shape, lambda i: (0, 0), **kw)
    pad8 = lambda w: jnp.pad(w, ((0, 8 - w.shape[0]), (0, 0)))
    extras = (pad8(sc_conv_w), sc_norm_g, pad8(gdn_conv_w))
    widths = (ws[0].shape[2], GROUP_WIDTH, ws[2].shape[2], ws[3].shape[2])
    of_layer = lambda a, **kw: pl.BlockSpec((None,) + a.shape[1:], lambda i: (layer, 0, 0), **kw)
    return pl.pallas_call(
        functools.partial(_in_proj_kernel, tiles_per_batch=tiles_per_batch),
        grid=(M // tm,),
        in_specs=[pl.BlockSpec((tm, D), lambda i: (i, 0)), const(g), mod_spec, mod_spec]
                 + [of_layer(w, pipeline_mode=pl.Buffered(1)) for w in ws] + [of_layer(b) for b in bs]
                 + [const(e) for e in extras],
        out_specs=[pl.BlockSpec((tm, n), lambda i: (i, 0)) for n in widths],
        out_shape=[jax.ShapeDtypeStruct((M, n), jnp.float32) for n in widths],
        scratch_shapes=[pltpu.VMEM((CONV_CARRY_ROWS + tm, GROUP_WIDTH), jnp.float32),
                        pltpu.VMEM((CONV_CARRY_ROWS + tm, 3 * GROUP_WIDTH), jnp.float32)],
        compiler_params=pltpu.CompilerParams(dimension_semantics=("arbitrary",),
                                             vmem_limit_bytes=VMEM_LIMIT_BYTES),
        name="in_projection",
    )(x2, g, shift, scale, *ws, *bs, *extras)


def _group_columns():
    offs, o = {}, 0
    for n, s in IN_LAYOUT:
        offs[n] = np.arange(o, o + s)
        o += s
    cat = lambda names: np.concatenate([offs[n] for n in names])
    return (cat(["a_q", "a_k_cmp", "a_v_cmp", "a_k_slc", "a_v_slc", "a_k_win", "a_v_win"] + list(SMALL_GATES)),
            cat(["b_b", "b_c", "b_x"]),
            cat(["c_q", "c_k", "c_v", "c_z"]),
            cat(["d_q", "d_k", "d_v", "d_o"]))


SMALL_GATES = ("a_gate", "c_beta", "c_alpha", "d_i", "d_f")
GATE_BLOCK = 5
GATE_LANE = {"a_gate": 0, "c_beta": 12, "c_alpha": 16, "d_i": 20, "d_f": 24}
GROUP_SLAB_WIDTH = (768, 768, 1024, 1024)


def _take_columns(a, idx, width):
    cuts = [0] + [k + 1 for k in range(idx.size - 1) if idx[k + 1] != idx[k] + 1] + [idx.size]
    runs = [a[..., int(idx[s]):int(idx[e - 1]) + 1] for s, e in zip(cuts[:-1], cuts[1:])]
    return jnp.concatenate(runs + [jnp.zeros(a.shape[:-1] + (width - idx.size,), a.dtype)], axis=-1)


def split_in_weights(w, b):
    ws, bs = [], []
    for idx, width in zip(_group_columns(), GROUP_SLAB_WIDTH):
        ws.append(_take_columns(w, idx, width).astype(jnp.bfloat16))
        bs.append(_take_columns(b, idx, width)[:, None, :])
    return ws, bs


def t5_bucket(dist):
    n = jnp.maximum(dist, 0)
    max_exact = N_BUCKETS // 2
    nf = jnp.maximum(n, 1).astype(jnp.float32)
    large = max_exact + (jnp.log(nf / max_exact) / math.log(MAX_DISTANCE / max_exact)
                         * (N_BUCKETS - max_exact)).astype(jnp.int32)
    large = jnp.minimum(large, N_BUCKETS - 1)
    return jnp.where(n < max_exact, n, large)


def _nt_dot(a, b):
    return lax.dot_general(a, b, (((1,), (1,)), ((), ())), preferred_element_type=jnp.float32)


def _nsa_prep_kernel(q_ref, kvs_ref, kvw_ref, qg_ref, kg_ref, seg_ref,
                     qnt_ref, kxs_ref, v1s_ref, kkw_ref, v1w_ref):
    tm = q_ref.shape[0]
    q = q_ref[...]
    ss = _dot_sel(q * q, seg_ref[...], parts=3)
    qn = q * lax.rsqrt(ss * (1.0 / HEAD_DIM) + EPS) * qg_ref[...]
    qnt_ref[...] = (qn * LOG2E).T.astype(qnt_ref.dtype)
    lane = lax.broadcasted_iota(jnp.int32, (tm, 128), 1)
    lo = lane < HEAD_DIM

    def split(x):
        xr = pltpu.roll(x, HEAD_DIM, 1)
        ss = jnp.sum(jnp.where(lo, x * x, 0.0), axis=-1, keepdims=True)
        kk = jnp.where(lo, x, xr) * lax.rsqrt(ss * (1.0 / HEAD_DIM) + EPS) * kg_ref[...]
        return kk, jnp.where(lo, xr, 1.0)

    kk, v1 = split(kvs_ref[...])
    tok = pl.program_id(1) * tm + lax.broadcasted_iota(jnp.int32, (tm, 128), 0)
    onehot = jnp.where(lane == tok // SLC_BLOCK, 1.0, 0.0)
    kxs_ref[...] = jnp.concatenate([kk, onehot], axis=1).astype(kxs_ref.dtype)
    v1s_ref[...] = v1.T[0:V1_ROWS].astype(v1s_ref.dtype)
    kk, v1 = split(kvw_ref[...])
    kkw_ref[...] = kk.astype(kkw_ref.dtype)
    v1w_ref[...] = v1.T[0:V1_ROWS].astype(v1w_ref.dtype)


def nsa_prep(pa, qg, kg2, seg):
    B, T, _ = pa.shape
    tm = 512
    bf16 = jnp.bfloat16
    col = lambda w, j: pl.BlockSpec((None, tm, w), lambda b, i: (b, i, j))
    rowblk = lambda h: pl.BlockSpec((None, h, tm), lambda b, i: (b, 0, i))
    const = lambda shape: pl.BlockSpec(shape, lambda b, i: (0, 0))
    return pl.pallas_call(
        _nsa_prep_kernel,
        grid=(B, T // tm),
        in_specs=[col(256, 0), col(128, 3), col(128, 4), const((1, 256)), const((1, 128)), const((256, 256))],
        out_specs=[rowblk(256), col(256, 0), rowblk(V1_ROWS), col(128, 0), rowblk(V1_ROWS)],
        out_shape=[jax.ShapeDtypeStruct((B, 256, T), bf16),
                   jax.ShapeDtypeStruct((B, T, 256), bf16), jax.ShapeDtypeStruct((B, V1_ROWS, T), bf16),
                   jax.ShapeDtypeStruct((B, T, 128), bf16), jax.ShapeDtypeStruct((B, V1_ROWS, T), bf16)],
        name="nsa_prep",
    )(pa, pa, pa, qg, kg2, seg)


def _compress_kernel(kv_ref, pos_ref, w1_ref, w2k_ref, w2v_ref, kg_ref, kkc_ref, vct_ref):
    nrow = kkc_ref.shape[0]
    f32, bf16 = jnp.float32, jnp.bfloat16
    hidden = w1_ref.shape[2] // 2
    first = jnp.zeros((nrow, 2 * hidden), f32)
    second = jnp.zeros((nrow, 2 * hidden), f32)
    for p in range(CMP_STRIDE):
        xp = kv_ref[pl.ds(p, nrow, stride=CMP_STRIDE), :]
        first = first + jnp.dot((xp + pos_ref[p:p + 1, :]).astype(bf16), w1_ref[p], preferred_element_type=f32)
        second = second + jnp.dot((xp + pos_ref[CMP_STRIDE + p:CMP_STRIDE + p + 1, :]).astype(bf16),
                                  w1_ref[CMP_STRIDE + p], preferred_element_type=f32)
    hid = first + pltpu.roll(second, nrow - 1, 0)
    hid = (hid * jax.nn.sigmoid(hid)).astype(bf16)
    valid = lax.broadcasted_iota(jnp.int32, (nrow, 128), 0) < nrow - 1
    kc = jnp.dot(hid[:, 0:hidden], w2k_ref[...], preferred_element_type=f32)
    kc = kc * lax.rsqrt(jnp.mean(kc * kc, axis=-1, keepdims=True) + EPS) * kg_ref[...]
    kkc_ref[...] = jnp.where(valid, kc, 0.0).astype(kkc_ref.dtype)
    vc = jnp.dot(hid[:, hidden:2 * hidden], w2v_ref[...], preferred_element_type=f32)
    vct_ref[...] = jnp.where(valid, vc, 0.0).T[0:HEAD_DIM].astype(vct_ref.dtype)


def nsa_compress(pa, pos2, w1, w2k, w2v, kg2):
    B, T, _ = pa.shape
    nrow = T // CMP_STRIDE
    bf16 = jnp.bfloat16
    full = lambda a: pl.BlockSpec(a.shape, lambda b: (0,) * a.ndim)
    return pl.pallas_call(
        _compress_kernel,
        grid=(B,),
        in_specs=[pl.BlockSpec((None, T, 128), lambda b: (b, 0, 2)), full(pos2), full(w1), full(w2k), full(w2v),
                  full(kg2)],
        out_specs=[pl.BlockSpec((None, nrow, 128), lambda b: (b, 0, 0)),
                   pl.BlockSpec((None, HEAD_DIM, nrow), lambda b: (b, 0, 0))],
        out_shape=[jax.ShapeDtypeStruct((B, nrow, 128), bf16), jax.ShapeDtypeStruct((B, HEAD_DIM, nrow), bf16)],
        name="nsa_compress",
    )(pa, pos2, w1, w2k, w2v, kg2)


CMP_TILES_PER_STEP = 4


def _stack_heads_t(qt):
    lo = lax.broadcasted_iota(jnp.int32, (128, qt.shape[1]), 0) < HEAD_DIM
    zero = jnp.zeros((), qt.dtype)
    return jnp.concatenate([jnp.where(lo if h % 2 == 0 else ~lo, qt[128 * (h // 2):128 * (h // 2) + 128], zero)
                            for h in range(GROUP_HEADS)], axis=1)


def _cmp_select_kernel(qnt_ref, kkc_ref, vct_ref, cb_ref, ov_ref, ocmp_ref, pen_ref, sc_ref, *, n_sel):
    i = pl.program_id(1)
    n_tiles = pen_ref.shape[0]
    tq = qnt_ref.shape[1] // n_tiles
    nblk, nrow = ov_ref.shape
    keys_per_tile = tq // CMP_STRIDE
    f32, bf16 = jnp.float32, jnp.bfloat16
    blk = lax.broadcasted_iota(jnp.int32, (nblk, tq), 0)
    scores = []
    for g in range(n_tiles):
        cols = slice(g * tq, (g + 1) * tq)
        q4t = _stack_heads_t(qnt_ref[:, cols])
        first_row = pl.multiple_of(nrow - keys_per_tile * (i * n_tiles + g), keys_per_tile)
        s = jnp.dot(kkc_ref[...], q4t, preferred_element_type=f32) + cb_ref[pl.ds(first_row, nrow), :]
        m = jnp.max(s, axis=0, keepdims=True)
        e = jnp.where(s > 0.5 * NEG, jnp.exp2(s - m), 0.0)
        p = e * (1.0 / jnp.maximum(jnp.sum(e, axis=0, keepdims=True), 1e-30))
        o = jnp.dot(vct_ref[...], p.astype(bf16), preferred_element_type=f32)
        for h in range(GROUP_HEADS):
            ocmp_ref[h * HEAD_DIM:(h + 1) * HEAD_DIM, cols] = o[:, h * tq:(h + 1) * tq]
        psum = p[:, 0:tq] + p[:, tq:2 * tq] + p[:, 2 * tq:3 * tq] + p[:, 3 * tq:4 * tq]
        hi = psum.astype(bf16)
        lo_part = (psum - hi.astype(f32)).astype(bf16)
        score = (jnp.dot(ov_ref[...], hi, preferred_element_type=f32)
                 + jnp.dot(ov_ref[...], lo_part, preferred_element_type=f32))
        qpos = (i * n_tiles + g) * tq + lax.broadcasted_iota(jnp.int32, (nblk, tq), 1)
        cur = qpos // SLC_BLOCK
        forced = (blk == 0) | ((cur - blk >= 0) & (cur - blk < N_LOCAL_SLC))
        score = jnp.where(forced, FORCE, score)
        score = jnp.where(blk <= cur, score, -FORCE)
        sc_ref[g] = score
        scores.append(score)
    blocks_per_tile = tq // SLC_BLOCK

    def rank_step(t, cnts):
        out = []
        for g in range(n_tiles):
            cnt = cnts[g]
            for u in range(blocks_per_tile):
                jp = t * blocks_per_tile + u
                row = sc_ref[g, pl.ds(jp, 1), :]
                tie = jnp.where(blk > jp, 1.0, 0.0)
                cnt = cnt + jnp.where(row > scores[g], 1.0, jnp.where(row == scores[g], tie, 0.0))
            out.append(cnt)
        return tuple(out)

    zeros = tuple(jnp.zeros((nblk, tq), f32) for _ in range(n_tiles))
    cnts = lax.fori_loop(0, (i + 1) * n_tiles, rank_step, zeros)
    for g in range(n_tiles):
        pen = jnp.where(cnts[g] < n_sel, 0.0, NEG)
        if nblk < 128:
            pen = jnp.concatenate([pen, jnp.zeros((128 - nblk, tq), f32)], axis=0)
        pen_ref[g] = pen.astype(pen_ref.dtype)


def nsa_cmp_select(qnt, kkc, vct, cbias, ov, n_sel):
    B, _, T = qnt.shape
    nrow = kkc.shape[1]
    nblk = ov.shape[0]
    tq = Q_BLOCK
    g = min(CMP_TILES_PER_STEP, T // tq)
    return pl.pallas_call(
        functools.partial(_cmp_select_kernel, n_sel=n_sel),
        grid=(B, T // (g * tq)),
        in_specs=[pl.BlockSpec((None, 256, g * tq), lambda b, i: (b, 0, i)),
                  pl.BlockSpec((None, nrow, 128), lambda b, i: (b, 0, 0)),
                  pl.BlockSpec((None, HEAD_DIM, nrow), lambda b, i: (b, 0, 0)),
                  pl.BlockSpec((2 * nrow, 4 * tq), lambda b, i: (0, 0)),
                  pl.BlockSpec((nblk, nrow), lambda b, i: (0, 0))],
        out_specs=[pl.BlockSpec((None, 256, g * tq), lambda b, i: (b, 0, i)),
                   pl.BlockSpec((None, g, 128, tq), lambda b, i: (b, i, 0, 0))],
        out_shape=[jax.ShapeDtypeStruct((B, 256, T), jnp.float32),
                   jax.ShapeDtypeStruct((B, T // tq, 128, tq), jnp.bfloat16)],
        scratch_shapes=[pltpu.VMEM((g, nblk, tq), jnp.float32)],
        name="nsa_cmp_select",
    )(qnt, kkc, vct, cbias, ov)


NSA_KEY_TILE = 256
NSA_QUERY_TILE = 256


def _nsa_attn_kernel(qnt_ref, pen_ref, ocmp_ref, gate_ref, kxs_ref, v1s_ref, kkw_ref, v1w_ref,
                     tabs_ref, tabw_ref, mixg_ref, o_ref, acc_ref, m_ref, sa_ref, sb_ref):
    i = pl.program_id(1)
    tq = o_ref.shape[0]
    kt = tabs_ref.shape[1]
    f32, bf16 = jnp.float32, jnp.bfloat16
    q4t = _stack_heads_t(qnt_ref[...])
    pen = jnp.concatenate([pen_ref[t] for t in range(pen_ref.shape[0])], axis=1)
    rhs_sel = jnp.concatenate([q4t, jnp.concatenate([pen] * GROUP_HEADS, axis=1)], axis=0)

    def branch(rhs, k_ref, v_ref, tab_ref, first_tile, last_tile):
        n_tab = tab_ref.shape[0] - 1
        m_ref[...] = jnp.full(m_ref.shape, NEG, f32)
        acc_ref[...] = jnp.zeros(acc_ref.shape, f32)

        def key_offset(j):
            return pl.multiple_of(jnp.minimum(j, last_tile) * kt, kt)

        def scores(j):
            delta = jnp.where(j > last_tile, n_tab, jnp.minimum(i - j * (kt // tq), n_tab - 1))
            return jnp.dot(k_ref[pl.ds(key_offset(j), kt), :], rhs, preferred_element_type=f32) + tab_ref[delta]

        def absorb(s, j):
            m_old = m_ref[...]
            m_new = jnp.maximum(m_old, jnp.max(s, axis=0, keepdims=True))
            p = jnp.exp2(s - m_new)
            pv = jnp.dot(v_ref[:, pl.ds(key_offset(j), kt)], p.astype(bf16), preferred_element_type=f32)
            acc_ref[...] = jnp.exp2(m_old - m_new) * acc_ref[...] + pv
            m_ref[...] = m_new

        sa_ref[...] = scores(first_tile)

        def body(t, carry):
            j = first_tile + 2 * t
            sb_ref[...] = scores(j + 1)
            absorb(sa_ref[...], j)
            sa_ref[...] = scores(j + 2)
            absorb(sb_ref[...], j + 1)
            return carry

        lax.fori_loop(0, (last_tile - first_tile) // 2 + 1, body, 0)
        acc = acc_ref[...]
        return acc[0:HEAD_DIM] / jnp.maximum(acc[HEAD_DIM:HEAD_DIM + 1], 1e-30)

    def window_branch(rhs):
        span = tabw_ref.shape[1]
        back = span // tq - 1
        start = pl.multiple_of(jnp.maximum(i - back, 0) * tq, tq)
        s = jnp.dot(kkw_ref[pl.ds(start, span), :], rhs, preferred_element_type=f32) + tabw_ref[jnp.minimum(i, back)]
        p = jnp.exp2(s - jnp.max(s, axis=0, keepdims=True))
        acc = jnp.dot(v1w_ref[:, pl.ds(start, span)], p.astype(bf16), preferred_element_type=f32)
        return acc[0:HEAD_DIM] / jnp.maximum(acc[HEAD_DIM:HEAD_DIM + 1], 1e-30)

    last = (i * tq) // kt
    o_w = window_branch(q4t)
    o_s = branch(rhs_sel, kxs_ref, v1s_ref, tabs_ref, 0, last)
    gt = jax.nn.sigmoid(gate_ref[...]).T
    oct_ = ocmp_ref[...]
    parts = []
    for h in range(GROUP_HEADS):
        cols = slice(h * tq, (h + 1) * tq)
        parts.append(gt[3 * h:3 * h + 1] * oct_[h * HEAD_DIM:(h + 1) * HEAD_DIM]
                     + gt[3 * h + 1:3 * h + 2] * o_s[:, cols] + gt[3 * h + 2:3 * h + 3] * o_w[:, cols])
    yt = jnp.concatenate(parts, axis=0)
    yt = yt * lax.rsqrt(jnp.mean(yt * yt, axis=0, keepdims=True) + EPS) * mixg_ref[...]
    o_ref[...] = yt.T


def nsa_attention(qnt, pen, ocmp, pa, kxs, v1s, kkw, v1w, tab_s, tab_w, mixg):
    B, _, T = qnt.shape
    tq = min(NSA_QUERY_TILE, T)
    tile = lambda w, j: pl.BlockSpec((None, tq, w), lambda b, i: (b, i, j))
    whole = lambda w: pl.BlockSpec((None, T, w), lambda b, i: (b, 0, 0))
    whole_t = pl.BlockSpec((None, V1_ROWS, T), lambda b, i: (b, 0, 0))
    full = lambda a: pl.BlockSpec(a.shape, lambda b, i: (0,) * a.ndim)
    return pl.pallas_call(
        _nsa_attn_kernel,
        grid=(B, T // tq),
        in_specs=[pl.BlockSpec((None, 256, tq), lambda b, i: (b, 0, i)),
                  pl.BlockSpec((None, tq // Q_BLOCK, 128, Q_BLOCK), lambda b, i: (b, i, 0, 0)),
                  pl.BlockSpec((None, 256, tq), lambda b, i: (b, 0, i)), tile(128, 5),
                  whole(256), whole_t, whole(128), whole_t, full(tab_s), full(tab_w), full(mixg)],
        out_specs=tile(256, 0),
        out_shape=jax.ShapeDtypeStruct((B, T, 256), jnp.float32),
        scratch_shapes=[pltpu.VMEM((V1_ROWS, GROUP_HEADS * tq), jnp.float32),
                        pltpu.VMEM((1, GROUP_HEADS * tq), jnp.float32),
                        pltpu.VMEM((tab_s.shape[1], GROUP_HEADS * tq), jnp.float32),
                        pltpu.VMEM((tab_s.shape[1], GROUP_HEADS * tq), jnp.float32)],
        compiler_params=pltpu.CompilerParams(dimension_semantics=("arbitrary", "arbitrary"),
                                             vmem_limit_bytes=VMEM_LIMIT_BYTES),
        name="nsa_attention",
    )(qnt, pen, ocmp, pa, kxs, v1s, kkw, v1w, tab_s, tab_w, mixg)


def _toeplitz(wr, rows, cols):
    H, n = wr.shape
    flat = jnp.tile(jnp.pad(wr, ((0, 0), (0, 1))), (1, rows))[:, :rows * n]
    return flat.reshape(H, rows, n)[:, :, rows - 1:rows - 1 + cols]


def nsa_bias_tables(t5_table, T):
    H = GROUP_HEADS
    tq = Q_BLOCK
    ta = min(NSA_QUERY_TILE, T)
    kt = min(NSA_KEY_TILE, T)
    nrow = T // CMP_STRIDE
    dmax = WINDOW + 2 * ta + kt
    onehot = (t5_bucket(jnp.arange(dmax))[:, None] == jnp.arange(N_BUCKETS)[None, :]).astype(jnp.float32)
    by_dist = jnp.dot(onehot, t5_table, precision=HIGHEST).T
    neg = lambda n: jnp.full((H, n), NEG, jnp.float32)
    f_s = jnp.concatenate([neg(kt - 1), by_dist * LOG2E], axis=1)
    span = ta + kt - 1

    def tiles(f, n):
        out = [_toeplitz(f[:, k * ta:k * ta + span], kt, ta).transpose(1, 0, 2).reshape(kt, H * ta) for k in range(n)]
        return jnp.stack(out + [jnp.full((kt, H * ta), NEG, jnp.float32)])

    n_s = -(-(kt + MAX_DISTANCE) // ta) + 1
    tab_s = tiles(f_s, n_s)
    n_w = WINDOW // ta + 1
    f_w = jnp.concatenate([neg(kt - 1), by_dist[:, :WINDOW] * LOG2E, neg(dmax - WINDOW)], axis=1)
    w_tiles = tiles(f_w, n_w)
    tab_w = jnp.stack([jnp.concatenate([w_tiles[k - m if k >= m else n_w] for m in range(n_w)], axis=0)
                       for k in range(n_w)])
    per_tile = tq // CMP_STRIDE
    d0 = -CMP_STRIDE * (nrow - 1) - (CMP_BLOCK - 1)
    n16 = 2 * nrow + per_tile - 1
    n_pos = CMP_STRIDE * n16 + d0
    far = jnp.broadcast_to(by_dist[:, -1:], (H, max(n_pos - dmax, 0)))
    w16 = jnp.concatenate([neg(-d0), by_dist[:, :n_pos] * LOG2E, far * LOG2E], axis=1).reshape(H, n16, CMP_STRIDE)
    w16 = w16[:, ::-1, :]
    base = jnp.stack([w16[:, per_tile - 1 - a:per_tile - 1 - a + 2 * nrow, :] for a in range(per_tile)], axis=1)
    cbias = base.transpose(0, 1, 3, 2).reshape(H * tq, 2 * nrow).T
    return tab_s, tab_w, cbias


def nsa_overlap_t(T):
    nrow = T // CMP_STRIDE
    nblk = T // SLC_BLOCK
    ci = np.arange(nrow)[None, :]
    bj = np.arange(nblk)[:, None]
    ov = (ci * CMP_STRIDE < (bj + 1) * SLC_BLOCK) & (ci * CMP_STRIDE + CMP_BLOCK > bj * SLC_BLOCK)
    ov = ov & (ci < nrow - 1)
    return jnp.asarray(ov.astype(np.float32), dtype=jnp.bfloat16)


def nsa_mixer_pallas(pa, q_norm_g, k_norm_g, cmp_pos, cmp_k_w1, cmp_k_w2, cmp_v_w1, cmp_v_w2,
                     tables, ovt, mixg):
    B, T, _ = pa.shape
    bf16 = jnp.bfloat16
    tab_s, tab_w, cbias = tables
    qg = (jnp.tile(q_norm_g, GROUP_HEADS) * HEAD_DIM ** -0.5)[None, :]
    kg2 = jnp.tile(k_norm_g, 2)[None, :]
    qnt, kxs, v1s, kkw, v1w = nsa_prep(pa, qg, kg2, _head_mask().astype(bf16))
    dup = lambda w: jnp.concatenate([w, w], axis=1)
    zeros = jnp.zeros((CMP_BLOCK, HEAD_DIM, cmp_k_w1.shape[1]), jnp.float32)
    per_pos = lambda w: w.reshape(CMP_BLOCK, HEAD_DIM, -1)
    w1 = jnp.concatenate([jnp.concatenate([per_pos(cmp_k_w1), zeros], axis=2),
                          jnp.concatenate([zeros, per_pos(cmp_v_w1)], axis=2)], axis=1).astype(bf16)
    kkc, vct = nsa_compress(pa, dup(cmp_pos), w1, dup(cmp_k_w2).astype(bf16), dup(cmp_v_w2).astype(bf16), kg2)
    n_sel = min(N_SLC, T // SLC_BLOCK)
    ocmp, pen = nsa_cmp_select(qnt, kkc, vct, cbias, ovt, n_sel)
    mixg_col = jnp.broadcast_to(mixg.reshape(-1, 1), (GROUP_WIDTH, min(NSA_QUERY_TILE, T)))
    return nsa_attention(qnt, pen, ocmp, pa, kxs, v1s, kkw, v1w, tab_s, tab_w, mixg_col)


CHUNKS_PER_STEP = 4
SEQS_PER_STEP = 2


def _tn_dot(a, b):
    return lax.dot_general(a, b, (((0,), (0,)), ((), ())), preferred_element_type=jnp.float32)


def _split_bf16(a, parts):
    out, rest = [], a
    for _ in range(parts):
        piece = rest.astype(jnp.bfloat16)
        out.append(piece)
        rest = rest - piece.astype(jnp.float32)
    return out


def _dot_sel(a, sel, parts=3):
    return sum(jnp.dot(p, sel, preferred_element_type=jnp.float32) for p in _split_bf16(a, parts))


def _sel_dot(sel, a, parts=3):
    return sum(jnp.dot(sel, p, preferred_element_type=jnp.float32) for p in _split_bf16(a, parts))


def _softplus(x):
    return jnp.maximum(x, 0.0) + jnp.log(1.0 + jnp.exp(-jnp.abs(x)))


def _block_diag(a, hmb):
    return jnp.concatenate([a.astype(jnp.bfloat16)] * GROUP_HEADS, axis=0) * hmb


def _diag_row(a, eye):
    return jnp.sum(a * eye, axis=0, keepdims=True)


def _head_norm(o, hmb):
    return o * lax.rsqrt(_dot_sel(o * o, hmb, parts=2) * (1.0 / HEAD_DIM) + EPS)


def _head_l2(o, hmb):
    return o * lax.rsqrt(_dot_sel(o * o, hmb, parts=2) + EPS)


def _xform_masks(L):
    i = np.arange(L)[:, None]
    j = np.arange(L)[None, :]
    masks = []
    s = 1
    while s < L:
        masks.append((i // (2 * s) == j // (2 * s)) & (i % (2 * s) >= s) & (j % (2 * s) < s))
        s *= 2
    masks += [i == j, j <= i, j < i]
    return jnp.asarray(np.tile(np.stack(masks).astype(np.float32), (1, 1, GROUP_HEADS)))


def _expand_matrix(first_lane):
    e = np.zeros((128, GROUP_WIDTH), np.float32)
    for h in range(GROUP_HEADS):
        e[first_lane + h, h * HEAD_DIM:(h + 1) * HEAD_DIM] = 1.0
    return jnp.asarray(e, jnp.bfloat16)


def _head_mask():
    return jnp.asarray(np.kron(np.eye(GROUP_HEADS), np.ones((HEAD_DIM, HEAD_DIM))), jnp.float32)


def _per_head_rows(*vecs):
    rows = [jnp.repeat(v, HEAD_DIM) if v.shape[0] == GROUP_HEADS else jnp.tile(v, GROUP_HEADS) for v in vecs]
    return jnp.pad(jnp.stack(rows), ((0, 8 - len(rows)), (0, 0)))


def _chunk_consts(L, n_chunks):
    tri = np.kron(np.eye(n_chunks), np.tril(np.ones((L, L)))).astype(np.float32)
    return _head_mask(), jnp.asarray(tri, jnp.bfloat16), _xform_masks(L)


def _const_spec(a):
    return pl.BlockSpec(a.shape, lambda b, t: (0,) * a.ndim)


def _gdn_kernel(pc_ref, gates_ref, hp_ref, eb_ref, ea_ref, hm_ref, tri_ref, lv_ref, o_ref, s_ref):
    L = GDN_CHUNK
    n_seq, rows = pc_ref.shape[0], pc_ref.shape[1]
    f32, bf16 = jnp.float32, jnp.bfloat16
    n_lev = lv_ref.shape[0] - 3

    @pl.when(pl.program_id(1) == 0)
    def _():
        s_ref[...] = jnp.zeros(s_ref.shape, f32)

    hm = hm_ref[...]
    hmb = hm.astype(bf16)
    eye, incl, strict = lv_ref[n_lev], lv_ref[n_lev + 1], lv_ref[n_lev + 2]

    q_c, k_c, v_c, beta_c, gc_c = [], [], [], [], []
    for b in range(n_seq):
        qkv = pc_ref[b, :, 0:768]
        q_all = _head_l2(qkv[:, 0:256], hmb) * HEAD_DIM ** -0.5
        k_all = _head_l2(qkv[:, 256:512], hmb)
        tail = gates_ref[b]
        beta_all = jax.nn.sigmoid(_dot_sel(tail, eb_ref[...]))
        g = hp_ref[0:1, :] * _softplus(_dot_sel(tail, ea_ref[...]) + hp_ref[1:2, :])
        gc_all = _sel_dot(tri_ref[...], g)
        for c in range(rows // L):
            sl = slice(c * L, (c + 1) * L)
            q_c.append(q_all[sl])
            k_c.append(k_all[sl])
            v_c.append(qkv[sl, 512:768])
            beta_c.append(beta_all[sl])
            gc_c.append(gc_all[sl])
    chains = range(len(q_c))
    egc = [jnp.exp(gc) for gc in gc_c]
    kb = [k_c[i] * beta_c[i] for i in chains]
    a_mat, qk = [], []
    for i in chains:
        seg = jnp.exp(jnp.where(incl > 0.5, gc_c[i] - _diag_row(gc_c[i], eye), NEG))
        k_bd = _block_diag(k_c[i], hmb)
        a_mat.append(_nt_dot(kb[i].astype(bf16), k_bd) * seg * strict)
        qk.append((_nt_dot(q_c[i].astype(bf16), k_bd) * seg).astype(bf16))
    t_inv = [eye - a * lv_ref[0] for a in a_mat]
    for lev in range(1, n_lev):
        te = [jnp.dot(t_inv[i].astype(bf16), _block_diag(a_mat[i] * lv_ref[lev], hmb),
                      preferred_element_type=f32) for i in chains]
        t_inv = [t_inv[i] - jnp.dot(te[i].astype(bf16), _block_diag(t_inv[i], hmb), preferred_element_type=f32)
                 for i in chains]
    parts = []
    for i in chains:
        tb = t_inv[i].astype(bf16)
        u = jnp.dot(tb, _block_diag(v_c[i] * beta_c[i], hmb), preferred_element_type=f32)
        w = jnp.dot(tb, _block_diag(kb[i] * egc[i], hmb), preferred_element_type=f32)
        k_dec = k_c[i] * jnp.exp(gc_c[i][L - 1:L, :] - gc_c[i])
        parts.append((u, w.astype(bf16), (q_c[i] * egc[i]).astype(bf16), qk[i], k_dec.astype(bf16),
                      egc[i][L - 1:L, :]))
    per_seq = rows // L
    state = [s_ref[b] for b in range(n_seq)]
    outs = [[] for _ in range(n_seq)]
    for c in range(per_seq):
        for b in range(n_seq):
            u, w, q_dec, qk_c, k_dec, g_tot = parts[b * per_seq + c]
            sb = state[b].astype(bf16)
            v_new = u - jnp.dot(w, sb, preferred_element_type=f32)
            outs[b].append(jnp.dot(q_dec, sb, preferred_element_type=f32)
                           + jnp.dot(qk_c, _block_diag(v_new, hmb), preferred_element_type=f32))
            state[b] = state[b] * g_tot + _tn_dot(k_dec, v_new.astype(bf16)) * hm
    for b in range(n_seq):
        s_ref[b] = state[b]
        o = jnp.concatenate(outs[b], axis=0)
        z = pc_ref[b, :, 768:1024]
        o_ref[b] = _head_norm(o, hmb) * hp_ref[2:3, :] * (z * jax.nn.sigmoid(z))


def _mlstm_kernel(pd_ref, gates_ref, hp_ref, ei_ref, ef_ref, hm_ref, tri_ref, lv_ref, o_ref, c_ref, n_ref, m_ref):
    L = MLSTM_CHUNK
    n_seq, rows = pd_ref.shape[0], pd_ref.shape[1]
    f32, bf16 = jnp.float32, jnp.bfloat16
    n_lev = lv_ref.shape[0] - 3

    @pl.when(pl.program_id(1) == 0)
    def _():
        c_ref[...] = jnp.zeros(c_ref.shape, f32)
        n_ref[...] = jnp.zeros(n_ref.shape, f32)
        m_ref[...] = jnp.zeros(m_ref.shape, f32)

    hm = hm_ref[...]
    hmb = hm.astype(bf16)
    eye, incl = lv_ref[n_lev], lv_ref[n_lev + 1]
    head_of_lane = lax.broadcasted_iota(jnp.int32, (L, GROUP_WIDTH), 1) // HEAD_DIM

    def head_max(a):
        out = jnp.zeros(a.shape, f32)
        for h in range(GROUP_HEADS):
            mine = head_of_lane == h
            out = jnp.where(mine, jnp.max(jnp.where(mine, a, NEG), axis=-1, keepdims=True), out)
        return out

    per_seq = rows // L
    parts = []
    for b in range(n_seq):
        tail = gates_ref[b]
        log_i_all = _dot_sel(tail, ei_ref[...])
        log_f = -_softplus(-(_dot_sel(tail, ef_ref[...]) + hp_ref[0:1, :]))
        b_all = _sel_dot(tri_ref[...], log_f)
        for c in range(per_seq):
            sl = slice(c * L, (c + 1) * L)
            q = pd_ref[b, sl, 0:256]
            k = pd_ref[b, sl, 256:512] * HEAD_DIM ** -0.5
            v = pd_ref[b, sl, 512:768]
            bc, log_i = b_all[sl], log_i_all[sl]
            log_w = jnp.where(incl > 0.5, bc - _diag_row(bc - log_i, eye), NEG)
            b_last = bc[L - 1:L, :]
            log_w_end = b_last - bc + log_i
            qk = _nt_dot(q.astype(bf16), _block_diag(k, hmb))
            parts.append((q, k, v, bc, log_w, head_max(log_w), b_last, log_w_end,
                          jnp.max(log_w_end, axis=0, keepdims=True), qk))
    c_state = [c_ref[b] for b in range(n_seq)]
    n_state = [n_ref[b] for b in range(n_seq)]
    m_prev = [m_ref[b] for b in range(n_seq)]
    outs = [[] for _ in range(n_seq)]
    for c in range(per_seq):
        for b in range(n_seq):
            q, k, v, bc, log_w, m_intra, b_last, log_w_end, m_end, qk = parts[b * per_seq + c]
            log_inter = bc + m_prev[b]
            m_t = jnp.maximum(log_inter, m_intra)
            w_inter = jnp.exp(log_inter - m_t)
            s = qk * jnp.exp(log_w - m_t)
            num = w_inter * jnp.dot(q.astype(bf16), c_state[b].astype(bf16), preferred_element_type=f32)
            num = num + jnp.dot(s.astype(bf16), _block_diag(v, hmb), preferred_element_type=f32)
            den = w_inter * _dot_sel(q * n_state[b], hmb, parts=2) + _dot_sel(s, hmb, parts=2)
            outs[b].append(num / jnp.maximum(jnp.abs(den), jnp.exp(-m_t)))
            m_new = jnp.maximum(b_last + m_prev[b], m_end)
            w_old = jnp.exp(b_last + m_prev[b] - m_new)
            kw = k * jnp.exp(log_w_end - m_new)
            c_state[b] = w_old * c_state[b] + _tn_dot(kw.astype(bf16), v.astype(bf16)) * hm
            n_state[b] = w_old * n_state[b] + jnp.sum(kw, axis=0, keepdims=True)
            m_prev[b] = m_new
    for b in range(n_seq):
        c_ref[b] = c_state[b]
        n_ref[b] = n_state[b]
        m_ref[b] = m_prev[b]
        h = jnp.concatenate(outs[b], axis=0)
        o_ref[b] = _head_norm(h, hmb) * hp_ref[1:2, :] * jax.nn.sigmoid(pd_ref[b, :, 768:1024])


def _recurrent_mixers_kernel(pc_ref, pd_ref, gates_ref, hpc_ref, eb_ref, ea_ref, hpd_ref, ei_ref, ef_ref,
                             hm_ref, tri_ref, lv_ref, oc_ref, od_ref, s_ref, c_ref, n_ref, m_ref):
    _gdn_kernel(pc_ref, gates_ref, hpc_ref, eb_ref, ea_ref, hm_ref, tri_ref, lv_ref, oc_ref, s_ref)
    _mlstm_kernel(pd_ref, gates_ref, hpd_ref, ei_ref, ef_ref, hm_ref, tri_ref, lv_ref, od_ref, c_ref, n_ref, m_ref)


def recurrent_mixers_pallas(pc, pd, pa, A_log, dt_bias, gdn_norm_g, f_bias, mlstm_norm_g):
    B, T, width = pc.shape
    L = GDN_CHUNK
    assert MLSTM_CHUNK == L and pd.shape == pc.shape
    rows = L * min(CHUNKS_PER_STEP, T // L)
    n_seq = SEQS_PER_STEP if B % SEQS_PER_STEP == 0 else 1
    n = GROUP_HEADS * HEAD_DIM
    expand = lambda name: _expand_matrix(GATE_LANE[name])
    consts = ((_per_head_rows(-jnp.exp(A_log), dt_bias, gdn_norm_g), expand("c_beta"), expand("c_alpha"),
               _per_head_rows(f_bias, mlstm_norm_g), expand("d_i"), expand("d_f")) + _chunk_consts(L, rows // L))
    slab = pl.BlockSpec((n_seq, rows, width), lambda b, t: (b, t, 0))
    out = pl.BlockSpec((n_seq, rows, GROUP_WIDTH), lambda b, t: (b, t, 0))
    f32 = jnp.float32
    return pl.pallas_call(
        _recurrent_mixers_kernel,
        grid=(B // n_seq, T // rows),
        in_specs=[slab, slab, pl.BlockSpec((n_seq, rows, 128), lambda b, t: (b, t, GATE_BLOCK))]
                 + [_const_spec(a) for a in consts],
        out_specs=[out, out],
        out_shape=[jax.ShapeDtypeStruct((B, T, GROUP_WIDTH), f32), jax.ShapeDtypeStruct((B, T, GROUP_WIDTH), f32)],
        scratch_shapes=[pltpu.VMEM((n_seq, n, n), f32), pltpu.VMEM((n_seq, n, n), f32),
                        pltpu.VMEM((n_seq, 1, GROUP_WIDTH), f32), pltpu.VMEM((n_seq, 1, GROUP_WIDTH), f32)],
        compiler_params=pltpu.CompilerParams(dimension_semantics=("arbitrary", "arbitrary"),
                                             vmem_limit_bytes=VMEM_LIMIT_BYTES),
        name="recurrent_mixers",
    )(pc, pd, pa, *consts)


def kernel(x, c, ada_w, ada_b, norm_g, ffn1_w13, ffn1_w2, ffn2_w13, ffn2_w2, w_in, b_in, q_norm_g, k_norm_g, cmp_pos, cmp_k_w1, cmp_k_w2, cmp_v_w1, cmp_v_w2, t5_table, sc_conv_w, gdn_conv_w, gdn_A_log, gdn_dt_bias, gdn_norm_g, mlstm_f_bias, mlstm_norm_g, mix_norm_g, w_out):
    B, T, D = x.shape
    depth = ada_w.shape[0]
    bf16 = jnp.bfloat16
    x2 = x.reshape(B * T, D)
    tables = nsa_bias_tables(t5_table, T)
    ovt = nsa_overlap_t(T)
    ffn_w = [w.astype(bf16) for w in (ffn1_w13, ffn1_w2, ffn2_w13, ffn2_w2, w_out)]
    in_ws, in_bs = split_in_weights(w_in, b_in)
    for l in range(depth):
        mod = ada_modulation(c, ada_w, ada_b[l][None, :], l)
        mod = mod.reshape(B, N_SUBLAYERS, 3, 1, D)
        x2 = ffn_half_step(x2, norm_g[l, 0][None, :], mod[:, 0, 0], mod[:, 0, 1], mod[:, 0, 2],
                           ffn_w[0], ffn_w[1], l, T)
        pa, y_b, pc, pd = in_projection(x2, norm_g[l, 1][None, :], mod[:, 1, 0], mod[:, 1, 1], in_ws, in_bs, l,
                                        sc_conv_w[l], mix_norm_g[l, 1][None, :], gdn_conv_w[l], T)
        pa, pc, pd = (a.reshape(B, T, -1) for a in (pa, pc, pd))
        y_a = nsa_mixer_pallas(pa, q_norm_g[l], k_norm_g[l], cmp_pos[l], cmp_k_w1[l], cmp_k_w2[l],
                               cmp_v_w1[l], cmp_v_w2[l], tables, ovt, mix_norm_g[l, 0][None, :])
        y_c, y_d = recurrent_mixers_pallas(pc, pd, pa, gdn_A_log[l], gdn_dt_bias[l], gdn_norm_g[l],
                                           mlstm_f_bias[l], mlstm_norm_g[l])
        ys = [y.reshape(B * T, GROUP_WIDTH) for y in (y_a, y_b, y_c, y_d)]
        x2 = ffn_half_step(x2, norm_g[l, 2][None, :], mod[:, 2, 0], mod[:, 2, 1], mod[:, 2, 2],
                           ffn_w[2], ffn_w[3], l, T, mix=(ys, mod[:, 1, 2], ffn_w[4]))
    return x2.reshape(B, T, D)
```

```python
import functools
import math

import jax
import jax.numpy as jnp
import numpy as np
from jax import lax
from jax.experimental import pallas as pl
from jax.experimental.pallas import tpu as pltpu

HEAD_DIM = 64
GROUP_HEADS = 4
GROUP_WIDTH = GROUP_HEADS * HEAD_DIM
CMP_STRIDE = 16
CMP_BLOCK = 32
SLC_BLOCK = 64
N_SLC = 16
N_LOCAL_SLC = 2
WINDOW = 512
Q_BLOCK = 128
FORCE = 1e6
N_BUCKETS = 32
MAX_DISTANCE = 128
GDN_CHUNK = 64
MLSTM_CHUNK = 64
N_SUBLAYERS = 3
EPS = 1e-6

IN_LAYOUT = (
    ("a_q", 256), ("a_k_cmp", 64), ("a_v_cmp", 64),
    ("a_k_slc", 64), ("a_v_slc", 64), ("a_k_win", 64), ("a_v_win", 64),
    ("a_gate", 12),
    ("b_b", 256), ("b_c", 256), ("b_x", 256),
    ("c_q", 256), ("c_k", 256), ("c_v", 256),
    ("c_beta", 4), ("c_alpha", 4), ("c_z", 256),
    ("d_q", 256), ("d_k", 256), ("d_v", 256),
    ("d_i", 4), ("d_f", 4), ("d_o", 256),
)

VMEM_LIMIT_BYTES = 56 * 1024 * 1024
FFN_TOKEN_TILE = 512
FFN_CHUNK = 256
PROJ_TOKEN_TILE = 512
NEG = -1e30
HIGHEST = lax.Precision.HIGHEST
LOG2E = math.log2(math.e)
V1_ROWS = HEAD_DIM + 16


def _modulated_norm(x, g, scale, shift):
    y = x * lax.rsqrt(jnp.mean(x * x, axis=-1, keepdims=True) + EPS)
    return (y * g) * (1.0 + scale) + shift


def _ada_kernel(c_ref, w_ref, b_ref, o_ref):
    c = c_ref[...]
    cond = c * jax.nn.sigmoid(c)
    o_ref[...] = jnp.dot(cond.astype(jnp.bfloat16), w_ref[...].astype(jnp.bfloat16),
                         preferred_element_type=jnp.float32) + b_ref[...]


def ada_modulation(c, w, b):
    B, D = c.shape
    depth, _, N = w.shape
    tn = 1152
    return pl.pallas_call(
        _ada_kernel,
        grid=(depth, N // tn),
        in_specs=[pl.BlockSpec((B, D), lambda l, j: (0, 0)),
                  pl.BlockSpec((None, D, tn), lambda l, j: (l, 0, j)),
                  pl.BlockSpec((None, 1, tn), lambda l, j: (l, 0, j))],
        out_specs=pl.BlockSpec((None, B, tn), lambda l, j: (l, 0, j)),
        out_shape=jax.ShapeDtypeStruct((depth, B, N), jnp.float32),
        name="ada_modulation",
    )(c, w, b)


def _ffn_kernel(x_ref, g_ref, shift_ref, scale_ref, gate_ref, w1_ref, w3_ref, w2_ref, *rest, n_mix):
    o_ref, act_ref = rest[-2:]
    x = x_ref[...]
    if n_mix:
        y_refs, mix_gate_ref, wo_ref = rest[:n_mix], rest[n_mix], rest[n_mix + 1]
        z = None
        for k, y_ref in enumerate(y_refs):
            part = jnp.dot(y_ref[...].astype(jnp.bfloat16), wo_ref[k * GROUP_WIDTH:(k + 1) * GROUP_WIDTH, :],
                           preferred_element_type=jnp.float32)
            z = part if z is None else z + part
        x = x + mix_gate_ref[0] * z
    h = _modulated_norm(x, g_ref[...], scale_ref[0], shift_ref[0]).astype(jnp.bfloat16)
    n_chunks = act_ref.shape[1] // FFN_CHUNK
    for ci in range(n_chunks):
        cs = slice(ci * FFN_CHUNK, (ci + 1) * FFN_CHUNK)
        a = jnp.dot(h, w1_ref[:, cs], preferred_element_type=jnp.float32)
        b = jnp.dot(h, w3_ref[:, cs], preferred_element_type=jnp.float32)
        act_ref[:, cs] = (a * jax.nn.sigmoid(a) * b).astype(jnp.bfloat16)
    y = jnp.dot(act_ref[...], w2_ref[...], preferred_element_type=jnp.float32)
    o_ref[...] = x + (0.5 * gate_ref[0]) * y


def ffn_half_step(x2, g, shift, scale, gate, w13, w2, layer, tokens_per_batch, mix=None):
    M, D = x2.shape
    F = w2.shape[1]
    tm = FFN_TOKEN_TILE
    tiles_per_batch = tokens_per_batch // tm
    resident = dict(pipeline_mode=pl.Buffered(1))
    mod_spec = pl.BlockSpec((1, 1, D), lambda i: (i // tiles_per_batch, 0, 0))
    ys, mix_gate, w_out = mix if mix is not None else ((), None, None)
    mix_args = tuple(ys) + ((mix_gate, w_out) if ys else ())
    mix_specs = [pl.BlockSpec((tm, GROUP_WIDTH), lambda i: (i, 0)) for _ in ys]
    if ys:
        mix_specs += [mod_spec, pl.BlockSpec((None,) + w_out.shape[1:], lambda i: (layer, 0, 0), **resident)]
    return pl.pallas_call(
        functools.partial(_ffn_kernel, n_mix=len(ys)),
        grid=(M // tm,),
        in_specs=[pl.BlockSpec((tm, D), lambda i: (i, 0)),
                  pl.BlockSpec((1, D), lambda i: (0, 0)),
                  mod_spec, mod_spec, mod_spec,
                  pl.BlockSpec((None, D, F), lambda i: (layer, 0, 0), **resident),
                  pl.BlockSpec((None, D, F), lambda i: (layer, 0, 1), **resident),
                  pl.BlockSpec((None, F, D), lambda i: (layer, 0, 0), **resident)] + mix_specs,
        out_specs=pl.BlockSpec((tm, D), lambda i: (i, 0)),
        out_shape=jax.ShapeDtypeStruct((M, D), jnp.float32),
        scratch_shapes=[pltpu.VMEM((tm, F), jnp.bfloat16)],
        compiler_params=pltpu.CompilerParams(dimension_semantics=("arbitrary",),
                                             vmem_limit_bytes=VMEM_LIMIT_BYTES),
        name="ffn_half_step",
    )(x2, g, shift, scale, gate, w13, w13, w2, *mix_args)


CONV_CARRY_ROWS = 8
CONV_ROW_CHUNK = 128


def _in_proj_kernel(x_ref, g_ref, shift_ref, scale_ref, wa_ref, wb_ref, wc_ref, wd_ref,
                    ba_ref, bb_ref, bc_ref, bd_ref, scw_ref, scg_ref, gcw_ref,
                    oa_ref, yb_ref, oc_ref, od_ref, braw_ref, craw_ref, *, tiles_per_batch):
    f32 = jnp.float32
    rows = x_ref.shape[0]
    hist = CONV_CARRY_ROWS
    first = pl.program_id(0) % tiles_per_batch == 0

    @pl.when(first)
    def _():
        braw_ref[0:hist, :] = jnp.zeros((hist, braw_ref.shape[1]), f32)
        craw_ref[0:hist, :] = jnp.zeros((hist, craw_ref.shape[1]), f32)

    @pl.when(jnp.logical_not(first))
    def _():
        braw_ref[0:hist, :] = braw_ref[rows:rows + hist, :]
        craw_ref[0:hist, :] = craw_ref[rows:rows + hist, :]

    h = _modulated_norm(x_ref[...], g_ref[...], scale_ref[0], shift_ref[0]).astype(jnp.bfloat16)
    project = lambda w_ref, b_ref: jnp.dot(h, w_ref[...], preferred_element_type=f32) + b_ref[...]
    pb = project(wb_ref, bb_ref)
    yb_ref[...] = pb[:, 0:256]
    braw_ref[hist:hist + rows, :] = pb[:, 256:512] * pb[:, 512:768]
    pc = project(wc_ref, bc_ref)
    craw_ref[hist:hist + rows, :] = pc[:, 0:768]
    oc_ref[:, 768:] = pc[:, 768:]
    n_chunks = rows // CONV_ROW_CHUNK
    pieces = []
    for w_ref, b_ref, o_ref in ((wa_ref, ba_ref, oa_ref), (wd_ref, bd_ref, od_ref)):
        half = w_ref.shape[1] // 256 // 2 * 256
        pieces += [(w_ref, b_ref, o_ref, 0, half), (w_ref, b_ref, o_ref, half, w_ref.shape[1])]
    for ci, r0 in enumerate(range(0, rows, CONV_ROW_CHUNK)):
        for w_ref, b_ref, o_ref, c0, c1 in pieces[ci * len(pieces) // n_chunks:(ci + 1) * len(pieces) // n_chunks]:
            o_ref[:, c0:c1] = jnp.dot(h, w_ref[:, c0:c1], preferred_element_type=f32) + b_ref[:, c0:c1]
        out_rows = slice(r0, r0 + CONV_ROW_CHUNK)
        tap = lambda ref, s: ref[pl.ds(hist + r0 - s, CONV_ROW_CHUNK), :]
        conv = sum(tap(braw_ref, s) * scw_ref[2 - s:3 - s, :] for s in range(3))
        y = yb_ref[out_rows, :] * conv
        yb_ref[out_rows, :] = y * lax.rsqrt(jnp.mean(y * y, axis=-1, keepdims=True) + EPS) * scg_ref[...]
        for c0 in range(0, 3 * GROUP_WIDTH, GROUP_WIDTH):
            cols = slice(c0, c0 + GROUP_WIDTH)
            conv = sum(craw_ref[pl.ds(hist + r0 - s, CONV_ROW_CHUNK), cols] * gcw_ref[3 - s:4 - s, cols]
                       for s in range(4))
            oc_ref[out_rows, cols] = conv * jax.nn.sigmoid(conv)


def in_projection(x2, g, shift, scale, ws, bs, layer, sc_conv_w, sc_norm_g, gdn_conv_w, tokens_per_batch):
    M, D = x2.shape
    tm = PROJ_TOKEN_TILE
    tiles_per_batch = tokens_per_batch // tm
    mod_spec = pl.BlockSpec((1, 1, D), lambda i: (i // tiles_per_batch, 0, 0))
    const = lambda a, **kw: pl.BlockSpec(a.shape, lambda i: (0, 0), **kw)
    pad8 = lambda w: jnp.pad(w, ((0, 8 - w.shape[0]), (0, 0)))
    extras = (pad8(sc_conv_w), sc_norm_g, pad8(gdn_conv_w))
    widths = (ws[0].shape[2], GROUP_WIDTH, ws[2].shape[2], ws[3].shape[2])
    of_layer = lambda a, **kw: pl.BlockSpec((None,) + a.shape[1:], lambda i: (layer, 0, 0), **kw)
    return pl.pallas_call(
        functools.partial(_in_proj_kernel, tiles_per_batch=tiles_per_batch),
        grid=(M // tm,),
        in_specs=[pl.BlockSpec((tm, D), lambda i: (i, 0)), const(g), mod_spec, mod_spec]
                 + [of_layer(w, pipeline_mode=pl.Buffered(1)) for w in ws] + [of_layer(b) for b in bs]
                 + [const(e) for e in extras],
        out_specs=[pl.BlockSpec((tm, n), lambda i: (i, 0)) for n in widths],
        out_shape=[jax.ShapeDtypeStruct((M, n), jnp.float32) for n in widths],
        scratch_shapes=[pltpu.VMEM((CONV_CARRY_ROWS + tm, GROUP_WIDTH), jnp.float32),
                        pltpu.VMEM((CONV_CARRY_ROWS + tm, 3 * GROUP_WIDTH), jnp.float32)],
        compiler_params=pltpu.CompilerParams(dimension_semantics=("arbitrary",),
                                             vmem_limit_bytes=VMEM_LIMIT_BYTES),
        name="in_projection",
    )(x2, g, shift, scale, *ws, *bs, *extras)


def _group_columns():
    offs, o = {}, 0
    for n, s in IN_LAYOUT:
        offs[n] = np.arange(o, o + s)
        o += s
    cat = lambda names: np.concatenate([offs[n] for n in names])
    return (cat(["a_q", "a_k_cmp", "a_v_cmp", "a_k_slc", "a_v_slc", "a_k_win", "a_v_win"] + list(SMALL_GATES)),
            cat(["b_b", "b_c", "b_x"]),
            cat(["c_q", "c_k", "c_v", "c_z"]),
            cat(["d_q", "d_k", "d_v", "d_o"]))


SMALL_GATES = ("a_gate", "c_beta", "c_alpha", "d_i", "d_f")
GATE_BLOCK = 5
GATE_LANE = {"a_gate": 0, "c_beta": 12, "c_alpha": 16, "d_i": 20, "d_f": 24}
GROUP_SLAB_WIDTH = (768, 768, 1024, 1024)


def _take_columns(a, idx, width):
    cuts = [0] + [k + 1 for k in range(idx.size - 1) if idx[k + 1] != idx[k] + 1] + [idx.size]
    runs = [a[..., int(idx[s]):int(idx[e - 1]) + 1] for s, e in zip(cuts[:-1], cuts[1:])]
    return jnp.concatenate(runs + [jnp.zeros(a.shape[:-1] + (width - idx.size,), a.dtype)], axis=-1)


def split_in_weights(w, b):
    ws, bs = [], []
    for idx, width in zip(_group_columns(), GROUP_SLAB_WIDTH):
        ws.append(_take_columns(w, idx, width).astype(jnp.bfloat16))
        bs.append(_take_columns(b, idx, width)[:, None, :])
    return ws, bs


def t5_bucket(dist):
    n = jnp.maximum(dist, 0)
    max_exact = N_BUCKETS // 2
    nf = jnp.maximum(n, 1).astype(jnp.float32)
    large = max_exact + (jnp.log(nf / max_exact) / math.log(MAX_DISTANCE / max_exact)
                         * (N_BUCKETS - max_exact)).astype(jnp.int32)
    large = jnp.minimum(large, N_BUCKETS - 1)
    return jnp.where(n < max_exact, n, large)


def _nt_dot(a, b):
    return lax.dot_general(a, b, (((1,), (1,)), ((), ())), preferred_element_type=jnp.float32)


def _nsa_prep_kernel(q_ref, kvs_ref, kvw_ref, qg_ref, kg_ref, seg_ref,
                     qnt_ref, kxs_ref, v1s_ref, kkw_ref, v1w_ref):
    tm = q_ref.shape[0]
    q = q_ref[...]
    ss = _dot_sel(q * q, seg_ref[...], parts=3)
    qn = q * lax.rsqrt(ss * (1.0 / HEAD_DIM) + EPS) * qg_ref[...]
    qnt_ref[...] = (qn * LOG2E).T.astype(qnt_ref.dtype)
    lane = lax.broadcasted_iota(jnp.int32, (tm, 128), 1)
    lo = lane < HEAD_DIM

    def split(x):
        xr = pltpu.roll(x, HEAD_DIM, 1)
        ss = jnp.sum(jnp.where(lo, x * x, 0.0), axis=-1, keepdims=True)
        kk = jnp.where(lo, x, xr) * lax.rsqrt(ss * (1.0 / HEAD_DIM) + EPS) * kg_ref[...]
        return kk, jnp.where(lo, xr, 1.0)

    kk, v1 = split(kvs_ref[...])
    tok = pl.program_id(1) * tm + lax.broadcasted_iota(jnp.int32, (tm, 128), 0)
    onehot = jnp.where(lane == tok // SLC_BLOCK, 1.0, 0.0)
    kxs_ref[...] = jnp.concatenate([kk, onehot], axis=1).astype(kxs_ref.dtype)
    v1s_ref[...] = v1.T[0:V1_ROWS].astype(v1s_ref.dtype)
    kk, v1 = split(kvw_ref[...])
    kkw_ref[...] = kk.astype(kkw_ref.dtype)
    v1w_ref[...] = v1.T[0:V1_ROWS].astype(v1w_ref.dtype)


def nsa_prep(pa, qg, kg2, seg):
    B, T, _ = pa.shape
    tm = 512
    bf16 = jnp.bfloat16
    col = lambda w, j: pl.BlockSpec((None, tm, w), lambda b, i: (b, i, j))
    rowblk = lambda h: pl.BlockSpec((None, h, tm), lambda b, i: (b, 0, i))
    const = lambda shape: pl.BlockSpec(shape, lambda b, i: (0, 0))
    return pl.pallas_call(
        _nsa_prep_kernel,
        grid=(B, T // tm),
        in_specs=[col(256, 0), col(128, 3), col(128, 4), const((1, 256)), const((1, 128)), const((256, 256))],
        out_specs=[rowblk(256), col(256, 0), rowblk(V1_ROWS), col(128, 0), rowblk(V1_ROWS)],
        out_shape=[jax.ShapeDtypeStruct((B, 256, T), bf16),
                   jax.ShapeDtypeStruct((B, T, 256), bf16), jax.ShapeDtypeStruct((B, V1_ROWS, T), bf16),
                   jax.ShapeDtypeStruct((B, T, 128), bf16), jax.ShapeDtypeStruct((B, V1_ROWS, T), bf16)],
        name="nsa_prep",
    )(pa, pa, pa, qg, kg2, seg)


def _compress_kernel(kv_ref, pos_ref, w1_ref, w2k_ref, w2v_ref, kg_ref, kkc_ref, vct_ref):
    nrow = kkc_ref.shape[0]
    f32, bf16 = jnp.float32, jnp.bfloat16
    hidden = w1_ref.shape[2] // 2
    first = jnp.zeros((nrow, 2 * hidden), f32)
    second = jnp.zeros((nrow, 2 * hidden), f32)
    for p in range(CMP_STRIDE):
        xp = kv_ref[pl.ds(p, nrow, stride=CMP_STRIDE), :]
        first = first + jnp.dot((xp + pos_ref[p:p + 1, :]).astype(bf16), w1_ref[p], preferred_element_type=f32)
        second = second + jnp.dot((xp + pos_ref[CMP_STRIDE + p:CMP_STRIDE + p + 1, :]).astype(bf16),
                                  w1_ref[CMP_STRIDE + p], preferred_element_type=f32)
    hid = first + pltpu.roll(second, nrow - 1, 0)
    hid = (hid * jax.nn.sigmoid(hid)).astype(bf16)
    valid = lax.broadcasted_iota(jnp.int32, (nrow, 128), 0) < nrow - 1
    kc = jnp.dot(hid[:, 0:hidden], w2k_ref[...], preferred_element_type=f32)
    kc = kc * lax.rsqrt(jnp.mean(kc * kc, axis=-1, keepdims=True) + EPS) * kg_ref[...]
    kkc_ref[...] = jnp.where(valid, kc, 0.0).astype(kkc_ref.dtype)
    vc = jnp.dot(hid[:, hidden:2 * hidden], w2v_ref[...], preferred_element_type=f32)
    vct_ref[...] = jnp.where(valid, vc, 0.0).T[0:HEAD_DIM].astype(vct_ref.dtype)


def nsa_compress(pa, pos2, w1, w2k, w2v, kg2):
    B, T, _ = pa.shape
    nrow = T // CMP_STRIDE
    bf16 = jnp.bfloat16
    full = lambda a: pl.BlockSpec(a.shape, lambda b: (0,) * a.ndim)
    return pl.pallas_call(
        _compress_kernel,
        grid=(B,),
        in_specs=[pl.BlockSpec((None, T, 128), lambda b: (b, 0, 2)), full(pos2), full(w1), full(w2k), full(w2v),
                  full(kg2)],
        out_specs=[pl.BlockSpec((None, nrow, 128), lambda b: (b, 0, 0)),
                   pl.BlockSpec((None, HEAD_DIM, nrow), lambda b: (b, 0, 0))],
        out_shape=[jax.ShapeDtypeStruct((B, nrow, 128), bf16), jax.ShapeDtypeStruct((B, HEAD_DIM, nrow), bf16)],
        name="nsa_compress",
    )(pa, pos2, w1, w2k, w2v, kg2)


CMP_TILES_PER_STEP = 4


def _stack_heads_t(qt):
    lo = lax.broadcasted_iota(jnp.int32, (128, qt.shape[1]), 0) < HEAD_DIM
    zero = jnp.zeros((), qt.dtype)
    return jnp.concatenate([jnp.where(lo if h % 2 == 0 else ~lo, qt[128 * (h // 2):128 * (h // 2) + 128], zero)
                            for h in range(GROUP_HEADS)], axis=1)


def _cmp_select_kernel(qnt_ref, kkc_ref, vct_ref, cb_ref, ov_ref, ocmp_ref, pen_ref, sc_ref, *, n_sel):
    i = pl.program_id(1)
    n_tiles = pen_ref.shape[0]
    tq = qnt_ref.shape[1] // n_tiles
    nblk, nrow = ov_ref.shape
    keys_per_tile = tq // CMP_STRIDE
    f32, bf16 = jnp.float32, jnp.bfloat16
    blk = lax.broadcasted_iota(jnp.int32, (nblk, tq), 0)
    scores = []
    for g in range(n_tiles):
        cols = slice(g * tq, (g + 1) * tq)
        q4t = _stack_heads_t(qnt_ref[:, cols])
        first_row = pl.multiple_of(nrow - keys_per_tile * (i * n_tiles + g), keys_per_tile)
        s = jnp.dot(kkc_ref[...], q4t, preferred_element_type=f32) + cb_ref[pl.ds(first_row, nrow), :]
        m = jnp.max(s, axis=0, keepdims=True)
        e = jnp.where(s > 0.5 * NEG, jnp.exp2(s - m), 0.0)
        p = e * (1.0 / jnp.maximum(jnp.sum(e, axis=0, keepdims=True), 1e-30))
        o = jnp.dot(vct_ref[...], p.astype(bf16), preferred_element_type=f32)
        for h in range(GROUP_HEADS):
            ocmp_ref[h * HEAD_DIM:(h + 1) * HEAD_DIM, cols] = o[:, h * tq:(h + 1) * tq]
        psum = p[:, 0:tq] + p[:, tq:2 * tq] + p[:, 2 * tq:3 * tq] + p[:, 3 * tq:4 * tq]
        hi = psum.astype(bf16)
        lo_part = (psum - hi.astype(f32)).astype(bf16)
        score = (jnp.dot(ov_ref[...], hi, preferred_element_type=f32)
                 + jnp.dot(ov_ref[...], lo_part, preferred_element_type=f32))
        qpos = (i * n_tiles + g) * tq + lax.broadcasted_iota(jnp.int32, (nblk, tq), 1)
        cur = qpos // SLC_BLOCK
        forced = (blk == 0) | ((cur - blk >= 0) & (cur - blk < N_LOCAL_SLC))
        score = jnp.where(forced, FORCE, score)
        score = jnp.where(blk <= cur, score, -FORCE)
        sc_ref[g] = score
        scores.append(score)
    blocks_per_tile = tq // SLC_BLOCK

    def rank_step(t, cnts):
        out = []
        for g in range(n_tiles):
            cnt = cnts[g]
            for u in range(blocks_per_tile):
                jp = t * blocks_per_tile + u
                row = sc_ref[g, pl.ds(jp, 1), :]
                tie = jnp.where(blk > jp, 1.0, 0.0)
                cnt = cnt + jnp.where(row > scores[g], 1.0, jnp.where(row == scores[g], tie, 0.0))
            out.append(cnt)
        return tuple(out)

    zeros = tuple(jnp.zeros((nblk, tq), f32) for _ in range(n_tiles))
    cnts = lax.fori_loop(0, (i + 1) * n_tiles, rank_step, zeros)
    for g in range(n_tiles):
        pen = jnp.where(cnts[g] < n_sel, 0.0, NEG)
        if nblk < 128:
            pen = jnp.concatenate([pen, jnp.zeros((128 - nblk, tq), f32)], axis=0)
        pen_ref[g] = pen.astype(pen_ref.dtype)


def nsa_cmp_select(qnt, kkc, vct, cbias, ov, n_sel):
    B, _, T = qnt.shape
    nrow = kkc.shape[1]
    nblk = ov.shape[0]
    tq = Q_BLOCK
    g = min(CMP_TILES_PER_STEP, T // tq)
    return pl.pallas_call(
        functools.partial(_cmp_select_kernel, n_sel=n_sel),
        grid=(B, T // (g * tq)),
        in_specs=[pl.BlockSpec((None, 256, g * tq), lambda b, i: (b, 0, i)),
                  pl.BlockSpec((None, nrow, 128), lambda b, i: (b, 0, 0)),
                  pl.BlockSpec((None, HEAD_DIM, nrow), lambda b, i: (b, 0, 0)),
                  pl.BlockSpec((2 * nrow, 4 * tq), lambda b, i: (0, 0)),
                  pl.BlockSpec((nblk, nrow), lambda b, i: (0, 0))],
        out_specs=[pl.BlockSpec((None, 256, g * tq), lambda b, i: (b, 0, i)),
                   pl.BlockSpec((None, g, 128, tq), lambda b, i: (b, i, 0, 0))],
        out_shape=[jax.ShapeDtypeStruct((B, 256, T), jnp.float32),
                   jax.ShapeDtypeStruct((B, T // tq, 128, tq), jnp.bfloat16)],
        scratch_shapes=[pltpu.VMEM((g, nblk, tq), jnp.float32)],
        name="nsa_cmp_select",
    )(qnt, kkc, vct, cbias, ov)


NSA_KEY_TILE = 256
NSA_QUERY_TILE = 256


def _nsa_attn_kernel(qnt_ref, pen_ref, ocmp_ref, gate_ref, kxs_ref, v1s_ref, kkw_ref, v1w_ref,
                     tabs_ref, tabw_ref, mixg_ref, o_ref, acc_ref, m_ref, sa_ref, sb_ref):
    i = pl.program_id(1)
    tq = o_ref.shape[0]
    kt = tabs_ref.shape[1]
    f32, bf16 = jnp.float32, jnp.bfloat16
    q4t = _stack_heads_t(qnt_ref[...])
    pen = jnp.concatenate([pen_ref[t] for t in range(pen_ref.shape[0])], axis=1)
    rhs_sel = jnp.concatenate([q4t, jnp.concatenate([pen] * GROUP_HEADS, axis=1)], axis=0)

    def branch(rhs, k_ref, v_ref, tab_ref, first_tile, last_tile):
        n_tab = tab_ref.shape[0] - 1
        m_ref[...] = jnp.full(m_ref.shape, NEG, f32)
        acc_ref[...] = jnp.zeros(acc_ref.shape, f32)

        def key_offset(j):
            return pl.multiple_of(jnp.minimum(j, last_tile) * kt, kt)

        def scores(j):
            delta = jnp.where(j > last_tile, n_tab, jnp.minimum(i - j * (kt // tq), n_tab - 1))
            return jnp.dot(k_ref[pl.ds(key_offset(j), kt), :], rhs, preferred_element_type=f32) + tab_ref[delta]

        def absorb(s, j):
            m_old = m_ref[...]
            m_new = jnp.maximum(m_old, jnp.max(s, axis=0, keepdims=True))
            p = jnp.exp2(s - m_new)
            pv = jnp.dot(v_ref[:, pl.ds(key_offset(j), kt)], p.astype(bf16), preferred_element_type=f32)
            acc_ref[...] = jnp.exp2(m_old - m_new) * acc_ref[...] + pv
            m_ref[...] = m_new

        sa_ref[...] = scores(first_tile)

        def body(t, carry):
            j = first_tile + 2 * t
            sb_ref[...] = scores(j + 1)
            absorb(sa_ref[...], j)
            sa_ref[...] = scores(j + 2)
            absorb(sb_ref[...], j + 1)
            return carry

        lax.fori_loop(0, (last_tile - first_tile) // 2 + 1, body, 0)
        acc = acc_ref[...]
        return acc[0:HEAD_DIM] / jnp.maximum(acc[HEAD_DIM:HEAD_DIM + 1], 1e-30)

    def window_branch(rhs):
        span = tabw_ref.shape[1]
        back = span // tq - 1
        start = pl.multiple_of(jnp.maximum(i - back, 0) * tq, tq)
        s = jnp.dot(kkw_ref[pl.ds(start, span), :], rhs, preferred_element_type=f32) + tabw_ref[jnp.minimum(i, back)]
        p = jnp.exp2(s - jnp.max(s, axis=0, keepdims=True))
        acc = jnp.dot(v1w_ref[:, pl.ds(start, span)], p.astype(bf16), preferred_element_type=f32)
        return acc[0:HEAD_DIM] / jnp.maximum(acc[HEAD_DIM:HEAD_DIM + 1], 1e-30)

    last = (i * tq) // kt
    o_w = window_branch(q4t)
    o_s = branch(rhs_sel, kxs_ref, v1s_ref, tabs_ref, 0, last)
    gt = jax.nn.sigmoid(gate_ref[...]).T
    oct_ = ocmp_ref[...]
    parts = []
    for h in range(GROUP_HEADS):
        cols = slice(h * tq, (h + 1) * tq)
        parts.append(gt[3 * h:3 * h + 1] * oct_[h * HEAD_DIM:(h + 1) * HEAD_DIM]
                     + gt[3 * h + 1:3 * h + 2] * o_s[:, cols] + gt[3 * h + 2:3 * h + 3] * o_w[:, cols])
    yt = jnp.concatenate(parts, axis=0)
    yt = yt * lax.rsqrt(jnp.mean(yt * yt, axis=0, keepdims=True) + EPS) * mixg_ref[...]
    o_ref[...] = yt.T


def nsa_attention(qnt, pen, ocmp, pa, kxs, v1s, kkw, v1w, tab_s, tab_w, mixg):
    B, _, T = qnt.shape
    tq = min(NSA_QUERY_TILE, T)
    tile = lambda w, j: pl.BlockSpec((None, tq, w), lambda b, i: (b, i, j))
    whole = lambda w: pl.BlockSpec((None, T, w), lambda b, i: (b, 0, 0))
    whole_t = pl.BlockSpec((None, V1_ROWS, T), lambda b, i: (b, 0, 0))
    full = lambda a: pl.BlockSpec(a.shape, lambda b, i: (0,) * a.ndim)
    return pl.pallas_call(
        _nsa_attn_kernel,
        grid=(B, T // tq),
        in_specs=[pl.BlockSpec((None, 256, tq), lambda b, i: (b, 0, i)),
                  pl.BlockSpec((None, tq // Q_BLOCK, 128, Q_BLOCK), lambda b, i: (b, i, 0, 0)),
                  pl.BlockSpec((None, 256, tq), lambda b, i: (b, 0, i)), tile(128, 5),
                  whole(256), whole_t, whole(128), whole_t, full(tab_s), full(tab_w), full(mixg)],
        out_specs=tile(256, 0),
        out_shape=jax.ShapeDtypeStruct((B, T, 256), jnp.float32),
        scratch_shapes=[pltpu.VMEM((V1_ROWS, GROUP_HEADS * tq), jnp.float32),
                        pltpu.VMEM((1, GROUP_HEADS * tq), jnp.float32),
                        pltpu.VMEM((tab_s.shape[1], GROUP_HEADS * tq), jnp.float32),
                        pltpu.VMEM((tab_s.shape[1], GROUP_HEADS * tq), jnp.float32)],
        compiler_params=pltpu.CompilerParams(dimension_semantics=("arbitrary", "arbitrary"),
                                             vmem_limit_bytes=VMEM_LIMIT_BYTES),
        name="nsa_attention",
    )(qnt, pen, ocmp, pa, kxs, v1s, kkw, v1w, tab_s, tab_w, mixg)


def _toeplitz(wr, rows, cols):
    H, n = wr.shape
    flat = jnp.tile(jnp.pad(wr, ((0, 0), (0, 1))), (1, rows))[:, :rows * n]
    return flat.reshape(H, rows, n)[:, :, rows - 1:rows - 1 + cols]


def nsa_bias_tables(t5_table, T):
    H = GROUP_HEADS
    tq = Q_BLOCK
    ta = min(NSA_QUERY_TILE, T)
    kt = min(NSA_KEY_TILE, T)
    nrow = T // CMP_STRIDE
    dmax = WINDOW + 2 * ta + kt
    onehot = (t5_bucket(jnp.arange(dmax))[:, None] == jnp.arange(N_BUCKETS)[None, :]).astype(jnp.float32)
    by_dist = jnp.dot(onehot, t5_table, precision=HIGHEST).T
    neg = lambda n: jnp.full((H, n), NEG, jnp.float32)
    f_s = jnp.concatenate([neg(kt - 1), by_dist * LOG2E], axis=1)
    span = ta + kt - 1

    def tiles(f, n):
        out = [_toeplitz(f[:, k * ta:k * ta + span], kt, ta).transpose(1, 0, 2).reshape(kt, H * ta) for k in range(n)]
        return jnp.stack(out + [jnp.full((kt, H * ta), NEG, jnp.float32)])

    n_s = -(-(kt + MAX_DISTANCE) // ta) + 1
    tab_s = tiles(f_s, n_s)
    n_w = WINDOW // ta + 1
    f_w = jnp.concatenate([neg(kt - 1), by_dist[:, :WINDOW] * LOG2E, neg(dmax - WINDOW)], axis=1)
    w_tiles = tiles(f_w, n_w)
    tab_w = jnp.stack([jnp.concatenate([w_tiles[k - m if k >= m else n_w] for m in range(n_w)], axis=0)
                       for k in range(n_w)])
    per_tile = tq // CMP_STRIDE
    d0 = -CMP_STRIDE * (nrow - 1) - (CMP_BLOCK - 1)
    n16 = 2 * nrow + per_tile - 1
    n_pos = CMP_STRIDE * n16 + d0
    far = jnp.broadcast_to(by_dist[:, -1:], (H, max(n_pos - dmax, 0)))
    w16 = jnp.concatenate([neg(-d0), by_dist[:, :n_pos] * LOG2E, far * LOG2E], axis=1).reshape(H, n16, CMP_STRIDE)
    w16 = w16[:, ::-1, :]
    base = jnp.stack([w16[:, per_tile - 1 - a:per_tile - 1 - a + 2 * nrow, :] for a in range(per_tile)], axis=1)
    cbias = base.transpose(0, 1, 3, 2).reshape(H * tq, 2 * nrow).T
    return tab_s, tab_w, cbias


def nsa_overlap_t(T):
    nrow = T // CMP_STRIDE
    nblk = T // SLC_BLOCK
    ci = np.arange(nrow)[None, :]
    bj = np.arange(nblk)[:, None]
    ov = (ci * CMP_STRIDE < (bj + 1) * SLC_BLOCK) & (ci * CMP_STRIDE + CMP_BLOCK > bj * SLC_BLOCK)
    ov = ov & (ci < nrow - 1)
    return jnp.asarray(ov.astype(np.float32), dtype=jnp.bfloat16)


def nsa_mixer_pallas(pa, q_norm_g, k_norm_g, cmp_pos, cmp_k_w1, cmp_k_w2, cmp_v_w1, cmp_v_w2,
                     tables, ovt, mixg):
    B, T, _ = pa.shape
    bf16 = jnp.bfloat16
    tab_s, tab_w, cbias = tables
    qg = (jnp.tile(q_norm_g, GROUP_HEADS) * HEAD_DIM ** -0.5)[None, :]
    kg2 = jnp.tile(k_norm_g, 2)[None, :]
    qnt, kxs, v1s, kkw, v1w = nsa_prep(pa, qg, kg2, _head_mask().astype(bf16))
    dup = lambda w: jnp.concatenate([w, w], axis=1)
    zeros = jnp.zeros((CMP_BLOCK, HEAD_DIM, cmp_k_w1.shape[1]), jnp.float32)
    per_pos = lambda w: w.reshape(CMP_BLOCK, HEAD_DIM, -1)
    w1 = jnp.concatenate([jnp.concatenate([per_pos(cmp_k_w1), zeros], axis=2),
                          jnp.concatenate([zeros, per_pos(cmp_v_w1)], axis=2)], axis=1).astype(bf16)
    kkc, vct = nsa_compress(pa, dup(cmp_pos), w1, dup(cmp_k_w2).astype(bf16), dup(cmp_v_w2).astype(bf16), kg2)
    n_sel = min(N_SLC, T // SLC_BLOCK)
    ocmp, pen = nsa_cmp_select(qnt, kkc, vct, cbias, ovt, n_sel)
    mixg_col = jnp.broadcast_to(mixg.reshape(-1, 1), (GROUP_WIDTH, min(NSA_QUERY_TILE, T)))
    return nsa_attention(qnt, pen, ocmp, pa, kxs, v1s, kkw, v1w, tab_s, tab_w, mixg_col)


CHUNKS_PER_STEP = 4
SEQS_PER_STEP = 2


def _tn_dot(a, b):
    return lax.dot_general(a, b, (((0,), (0,)), ((), ())), preferred_element_type=jnp.float32)


def _split_bf16(a, parts):
    out, rest = [], a
    for _ in range(parts):
        piece = rest.astype(jnp.bfloat16)
        out.append(piece)
        rest = rest - piece.astype(jnp.float32)
    return out


def _dot_sel(a, sel, parts=3):
    return sum(jnp.dot(p, sel, preferred_element_type=jnp.float32) for p in _split_bf16(a, parts))


def _sel_dot(sel, a, parts=3):
    return sum(jnp.dot(sel, p, preferred_element_type=jnp.float32) for p in _split_bf16(a, parts))


def _softplus(x):
    return jnp.maximum(x, 0.0) + jnp.log(1.0 + jnp.exp(-jnp.abs(x)))


def _block_diag(a, hmb):
    return jnp.concatenate([a.astype(jnp.bfloat16)] * GROUP_HEADS, axis=0) * hmb


def _diag_row(a, eye):
    return jnp.sum(a * eye, axis=0, keepdims=True)


def _head_norm(o, hmb):
    return o * lax.rsqrt(_dot_sel(o * o, hmb, parts=2) * (1.0 / HEAD_DIM) + EPS)


def _head_l2(o, hmb):
    return o * lax.rsqrt(_dot_sel(o * o, hmb, parts=2) + EPS)


def _xform_masks(L):
    i = np.arange(L)[:, None]
    j = np.arange(L)[None, :]
    masks = []
    s = 1
    while s < L:
        masks.append((i // (2 * s) == j // (2 * s)) & (i % (2 * s) >= s) & (j % (2 * s) < s))
        s *= 2
    masks += [i == j, j <= i, j < i]
    return jnp.asarray(np.tile(np.stack(masks).astype(np.float32), (1, 1, GROUP_HEADS)))


def _expand_matrix(first_lane):
    e = np.zeros((128, GROUP_WIDTH), np.float32)
    for h in range(GROUP_HEADS):
        e[first_lane + h, h * HEAD_DIM:(h + 1) * HEAD_DIM] = 1.0
    return jnp.asarray(e, jnp.bfloat16)


def _head_mask():
    return jnp.asarray(np.kron(np.eye(GROUP_HEADS), np.ones((HEAD_DIM, HEAD_DIM))), jnp.float32)


def _per_head_rows(*vecs):
    rows = [jnp.repeat(v, HEAD_DIM) if v.shape[0] == GROUP_HEADS else jnp.tile(v, GROUP_HEADS) for v in vecs]
    return jnp.pad(jnp.stack(rows), ((0, 8 - len(rows)), (0, 0)))


def _chunk_consts(L, n_chunks):
    tri = np.kron(np.eye(n_chunks), np.tril(np.ones((L, L)))).astype(np.float32)
    return _head_mask(), jnp.asarray(tri, jnp.bfloat16), _xform_masks(L)


def _const_spec(a):
    return pl.BlockSpec(a.shape, lambda b, t: (0,) * a.ndim)


def _gdn_kernel(pc_ref, gates_ref, hp_ref, eb_ref, ea_ref, hm_ref, tri_ref, lv_ref, o_ref, s_ref):
    L = GDN_CHUNK
    n_seq, rows = pc_ref.shape[0], pc_ref.shape[1]
    f32, bf16 = jnp.float32, jnp.bfloat16
    n_lev = lv_ref.shape[0] - 3

    @pl.when(pl.program_id(1) == 0)
    def _():
        s_ref[...] = jnp.zeros(s_ref.shape, f32)

    hm = hm_ref[...]
    hmb = hm.astype(bf16)
    eye, incl, strict = lv_ref[n_lev], lv_ref[n_lev + 1], lv_ref[n_lev + 2]

    q_c, k_c, v_c, beta_c, gc_c = [], [], [], [], []
    for b in range(n_seq):
        qkv = pc_ref[b, :, 0:768]
        q_all = _head_l2(qkv[:, 0:256], hmb) * HEAD_DIM ** -0.5
        k_all = _head_l2(qkv[:, 256:512], hmb)
        tail = gates_ref[b]
        beta_all = jax.nn.sigmoid(_dot_sel(tail, eb_ref[...]))
        g = hp_ref[0:1, :] * _softplus(_dot_sel(tail, ea_ref[...]) + hp_ref[1:2, :])
        gc_all = _sel_dot(tri_ref[...], g)
        for c in range(rows // L):
            sl = slice(c * L, (c + 1) * L)
            q_c.append(q_all[sl])
            k_c.append(k_all[sl])
            v_c.append(qkv[sl, 512:768])
            beta_c.append(beta_all[sl])
            gc_c.append(gc_all[sl])
    chains = range(len(q_c))
    egc = [jnp.exp(gc) for gc in gc_c]
    kb = [k_c[i] * beta_c[i] for i in chains]
    a_mat, qk = [], []
    for i in chains:
        seg = jnp.exp(jnp.where(incl > 0.5, gc_c[i] - _diag_row(gc_c[i], eye), NEG))
        k_bd = _block_diag(k_c[i], hmb)
        a_mat.append(_nt_dot(kb[i].astype(bf16), k_bd) * seg * strict)
        qk.append((_nt_dot(q_c[i].astype(bf16), k_bd) * seg).astype(bf16))
    t_inv = [eye - a * lv_ref[0] for a in a_mat]
    for lev in range(1, n_lev):
        te = [jnp.dot(t_inv[i].astype(bf16), _block_diag(a_mat[i] * lv_ref[lev], hmb),
                      preferred_element_type=f32) for i in chains]
        t_inv = [t_inv[i] - jnp.dot(te[i].astype(bf16), _block_diag(t_inv[i], hmb), preferred_element_type=f32)
                 for i in chains]
    parts = []
    for i in chains:
        tb = t_inv[i].astype(bf16)
        u = jnp.dot(tb, _block_diag(v_c[i] * beta_c[i], hmb), preferred_element_type=f32)
        w = jnp.dot(tb, _block_diag(kb[i] * egc[i], hmb), preferred_element_type=f32)
        k_dec = k_c[i] * jnp.exp(gc_c[i][L - 1:L, :] - gc_c[i])
        parts.append((u, w.astype(bf16), (q_c[i] * egc[i]).astype(bf16), qk[i], k_dec.astype(bf16),
                      egc[i][L - 1:L, :]))
    per_seq = rows // L
    state = [s_ref[b] for b in range(n_seq)]
    outs = [[] for _ in range(n_seq)]
    for c in range(per_seq):
        for b in range(n_seq):
            u, w, q_dec, qk_c, k_dec, g_tot = parts[b * per_seq + c]
            sb = state[b].astype(bf16)
            v_new = u - jnp.dot(w, sb, preferred_element_type=f32)
            outs[b].append(jnp.dot(q_dec, sb, preferred_element_type=f32)
                           + jnp.dot(qk_c, _block_diag(v_new, hmb), preferred_element_type=f32))
            state[b] = state[b] * g_tot + _tn_dot(k_dec, v_new.astype(bf16)) * hm
    for b in range(n_seq):
        s_ref[b] = state[b]
        o = jnp.concatenate(outs[b], axis=0)
        z = pc_ref[b, :, 768:1024]
        o_ref[b] = _head_norm(o, hmb) * hp_ref[2:3, :] * (z * jax.nn.sigmoid(z))


def _mlstm_kernel(pd_ref, gates_ref, hp_ref, ei_ref, ef_ref, hm_ref, tri_ref, lv_ref, o_ref, c_ref, n_ref, m_ref):
    L = MLSTM_CHUNK
    n_seq, rows = pd_ref.shape[0], pd_ref.shape[1]
    f32, bf16 = jnp.float32, jnp.bfloat16
    n_lev = lv_ref.shape[0] - 3

    @pl.when(pl.program_id(1) == 0)
    def _():
        c_ref[...] = jnp.zeros(c_ref.shape, f32)
        n_ref[...] = jnp.zeros(n_ref.shape, f32)
        m_ref[...] = jnp.zeros(m_ref.shape, f32)

    hm = hm_ref[...]
    hmb = hm.astype(bf16)
    eye, incl = lv_ref[n_lev], lv_ref[n_lev + 1]
    head_of_lane = lax.broadcasted_iota(jnp.int32, (L, GROUP_WIDTH), 1) // HEAD_DIM

    def head_max(a):
        out = jnp.zeros(a.shape, f32)
        for h in range(GROUP_HEADS):
            mine = head_of_lane == h
            out = jnp.where(mine, jnp.max(jnp.where(mine, a, NEG), axis=-1, keepdims=True), out)
        return out

    per_seq = rows // L
    parts = []
    for b in range(n_seq):
        tail = gates_ref[b]
        log_i_all = _dot_sel(tail, ei_ref[...])
        log_f = -_softplus(-(_dot_sel(tail, ef_ref[...]) + hp_ref[0:1, :]))
        b_all = _sel_dot(tri_ref[...], log_f)
        for c in range(per_seq):
            sl = slice(c * L, (c + 1) * L)
            q = pd_ref[b, sl, 0:256]
            k = pd_ref[b, sl, 256:512] * HEAD_DIM ** -0.5
            v = pd_ref[b, sl, 512:768]
            bc, log_i = b_all[sl], log_i_all[sl]
            log_w = jnp.where(incl > 0.5, bc - _diag_row(bc - log_i, eye), NEG)
            b_last = bc[L - 1:L, :]
            log_w_end = b_last - bc + log_i
            qk = _nt_dot(q.astype(bf16), _block_diag(k, hmb))
            parts.append((q, k, v, bc, log_w, head_max(log_w), b_last, log_w_end,
                          jnp.max(log_w_end, axis=0, keepdims=True), qk))
    c_state = [c_ref[b] for b in range(n_seq)]
    n_state = [n_ref[b] for b in range(n_seq)]
    m_prev = [m_ref[b] for b in range(n_seq)]
    outs = [[] for _ in range(n_seq)]
    for c in range(per_seq):
        for b in range(n_seq):
            q, k, v, bc, log_w, m_intra, b_last, log_w_end, m_end, qk = parts[b * per_seq + c]
            log_inter = bc + m_prev[b]
            m_t = jnp.maximum(log_inter, m_intra)
            w_inter = jnp.exp(log_inter - m_t)
            s = qk * jnp.exp(log_w - m_t)
            num = w_inter * jnp.dot(q.astype(bf16), c_state[b].astype(bf16), preferred_element_type=f32)
            num = num + jnp.dot(s.astype(bf16), _block_diag(v, hmb), preferred_element_type=f32)
            den = w_inter * _dot_sel(q * n_state[b], hmb, parts=2) + _dot_sel(s, hmb, parts=2)
            outs[b].append(num / jnp.maximum(jnp.abs(den), jnp.exp(-m_t)))
            m_new = jnp.maximum(b_last + m_prev[b], m_end)
            w_old = jnp.exp(b_last + m_prev[b] - m_new)
            kw = k * jnp.exp(log_w_end - m_new)
            c_state[b] = w_old * c_state[b] + _tn_dot(kw.astype(bf16), v.astype(bf16)) * hm
            n_state[b] = w_old * n_state[b] + jnp.sum(kw, axis=0, keepdims=True)
            m_prev[b] = m_new
    for b in range(n_seq):
        c_ref[b] = c_state[b]
        n_ref[b] = n_state[b]
        m_ref[b] = m_prev[b]
        h = jnp.concatenate(outs[b], axis=0)
        o_ref[b] = _head_norm(h, hmb) * hp_ref[1:2, :] * jax.nn.sigmoid(pd_ref[b, :, 768:1024])


def _recurrent_mixers_kernel(pc_ref, pd_ref, gates_ref, hpc_ref, eb_ref, ea_ref, hpd_ref, ei_ref, ef_ref,
                             hm_ref, tri_ref, lv_ref, oc_ref, od_ref, s_ref, c_ref, n_ref, m_ref):
    _gdn_kernel(pc_ref, gates_ref, hpc_ref, eb_ref, ea_ref, hm_ref, tri_ref, lv_ref, oc_ref, s_ref)
    _mlstm_kernel(pd_ref, gates_ref, hpd_ref, ei_ref, ef_ref, hm_ref, tri_ref, lv_ref, od_ref, c_ref, n_ref, m_ref)


def recurrent_mixers_pallas(pc, pd, pa, A_log, dt_bias, gdn_norm_g, f_bias, mlstm_norm_g):
    B, T, width = pc.shape
    L = GDN_CHUNK
    assert MLSTM_CHUNK == L and pd.shape == pc.shape
    rows = L * min(CHUNKS_PER_STEP, T // L)
    n_seq = SEQS_PER_STEP if B % SEQS_PER_STEP == 0 else 1
    n = GROUP_HEADS * HEAD_DIM
    expand = lambda name: _expand_matrix(GATE_LANE[name])
    consts = ((_per_head_rows(-jnp.exp(A_log), dt_bias, gdn_norm_g), expand("c_beta"), expand("c_alpha"),
               _per_head_rows(f_bias, mlstm_norm_g), expand("d_i"), expand("d_f")) + _chunk_consts(L, rows // L))
    slab = pl.BlockSpec((n_seq, rows, width), lambda b, t: (b, t, 0))
    out = pl.BlockSpec((n_seq, rows, GROUP_WIDTH), lambda b, t: (b, t, 0))
    f32 = jnp.float32
    return pl.pallas_call(
        _recurrent_mixers_kernel,
        grid=(B // n_seq, T // rows),
        in_specs=[slab, slab, pl.BlockSpec((n_seq, rows, 128), lambda b, t: (b, t, GATE_BLOCK))]
                 + [_const_spec(a) for a in consts],
        out_specs=[out, out],
        out_shape=[jax.ShapeDtypeStruct((B, T, GROUP_WIDTH), f32), jax.ShapeDtypeStruct((B, T, GROUP_WIDTH), f32)],
        scratch_shapes=[pltpu.VMEM((n_seq, n, n), f32), pltpu.VMEM((n_seq, n, n), f32),
                        pltpu.VMEM((n_seq, 1, GROUP_WIDTH), f32), pltpu.VMEM((n_seq, 1, GROUP_WIDTH), f32)],
        compiler_params=pltpu.CompilerParams(dimension_semantics=("arbitrary", "arbitrary"),
                                             vmem_limit_bytes=VMEM_LIMIT_BYTES),
        name="recurrent_mixers",
    )(pc, pd, pa, *consts)


def kernel(x, c, ada_w, ada_b, norm_g, ffn1_w13, ffn1_w2, ffn2_w13, ffn2_w2, w_in, b_in, q_norm_g, k_norm_g, cmp_pos, cmp_k_w1, cmp_k_w2, cmp_v_w1, cmp_v_w2, t5_table, sc_conv_w, gdn_conv_w, gdn_A_log, gdn_dt_bias, gdn_norm_g, mlstm_f_bias, mlstm_norm_g, mix_norm_g, w_out):
    B, T, D = x.shape
    depth = ada_w.shape[0]
    bf16 = jnp.bfloat16
    x2 = x.reshape(B * T, D)
    tables = nsa_bias_tables(t5_table, T)
    ovt = nsa_overlap_t(T)
    ffn_w = [w.astype(bf16) for w in (ffn1_w13, ffn1_w2, ffn2_w13, ffn2_w2, w_out)]
    in_ws, in_bs = split_in_weights(w_in, b_in)
    mod_all = ada_modulation(c, ada_w, ada_b[:, None, :])
    for l in range(depth):
        mod = mod_all[l].reshape(B, N_SUBLAYERS, 3, 1, D)
        x2 = ffn_half_step(x2, norm_g[l, 0][None, :], mod[:, 0, 0], mod[:, 0, 1], mod[:, 0, 2],
                           ffn_w[0], ffn_w[1], l, T)
        pa, y_b, pc, pd = in_projection(x2, norm_g[l, 1][None, :], mod[:, 1, 0], mod[:, 1, 1], in_ws, in_bs, l,
                                        sc_conv_w[l], mix_norm_g[l, 1][None, :], gdn_conv_w[l], T)
        pa, pc, pd = (a.reshape(B, T, -1) for a in (pa, pc, pd))
        y_a = nsa_mixer_pallas(pa, q_norm_g[l], k_norm_g[l], cmp_pos[l], cmp_k_w1[l], cmp_k_w2[l],
                               cmp_v_w1[l], cmp_v_w2[l], tables, ovt, mix_norm_g[l, 0][None, :])
        y_c, y_d = recurrent_mixers_pallas(pc, pd, pa, gdn_A_log[l], gdn_dt_bias[l], gdn_norm_g[l],
                                           mlstm_f_bias[l], mlstm_norm_g[l])
        ys = [y.reshape(B * T, GROUP_WIDTH) for y in (y_a, y_b, y_c, y_d)]
        x2 = ffn_half_step(x2, norm_g[l, 2][None, :], mod[:, 2, 0], mod[:, 2, 1], mod[:, 2, 2],
                           ffn_w[2], ffn_w[3], l, T, mix=(ys, mod[:, 1, 2], ffn_w[4]))
    return x2.reshape(B, T, D)
```

```python
import functools
import math

import jax
import jax.numpy as jnp
import numpy as np
from jax import lax
from jax.experimental import pallas as pl
from jax.experimental.pallas import tpu as pltpu

HEAD_DIM = 64
GROUP_HEADS = 4
GROUP_WIDTH = GROUP_HEADS * HEAD_DIM
CMP_STRIDE = 16
CMP_BLOCK = 32
SLC_BLOCK = 64
N_SLC = 16
N_LOCAL_SLC = 2
WINDOW = 512
Q_BLOCK = 128
FORCE = 1e6
N_BUCKETS = 32
MAX_DISTANCE = 128
GDN_CHUNK = 64
MLSTM_CHUNK = 64
N_SUBLAYERS = 3
EPS = 1e-6

IN_LAYOUT = (
    ("a_q", 256), ("a_k_cmp", 64), ("a_v_cmp", 64),
    ("a_k_slc", 64), ("a_v_slc", 64), ("a_k_win", 64), ("a_v_win", 64),
    ("a_gate", 12),
    ("b_b", 256), ("b_c", 256), ("b_x", 256),
    ("c_q", 256), ("c_k", 256), ("c_v", 256),
    ("c_beta", 4), ("c_alpha", 4), ("c_z", 256),
    ("d_q", 256), ("d_k", 256), ("d_v", 256),
    ("d_i", 4), ("d_f", 4), ("d_o", 256),
)

VMEM_LIMIT_BYTES = 56 * 1024 * 1024
FFN_TOKEN_TILE = 512
FFN_CHUNK = 256
PROJ_TOKEN_TILE = 512
NEG = -1e30
HIGHEST = lax.Precision.HIGHEST
LOG2E = math.log2(math.e)
V1_ROWS = HEAD_DIM + 16


def _modulated_norm(x, g, scale, shift):
    y = x * lax.rsqrt(jnp.mean(x * x, axis=-1, keepdims=True) + EPS)
    return (y * g) * (1.0 + scale) + shift


def _ada_kernel(c_ref, w_ref, b_ref, o_ref):
    c = c_ref[...]
    cond = c * jax.nn.sigmoid(c)
    o_ref[...] = jnp.dot(cond.astype(jnp.bfloat16), w_ref[...].astype(jnp.bfloat16),
                         preferred_element_type=jnp.float32) + b_ref[...]


def ada_modulation(c, w, b):
    B, D = c.shape
    depth, _, N = w.shape
    tn = 1152
    return pl.pallas_call(
        _ada_kernel,
        grid=(depth, N // tn),
        in_specs=[pl.BlockSpec((B, D), lambda l, j: (0, 0)),
                  pl.BlockSpec((None, D, tn), lambda l, j: (l, 0, j)),
                  pl.BlockSpec((None, 1, tn), lambda l, j: (l, 0, j))],
        out_specs=pl.BlockSpec((None, B, tn), lambda l, j: (l, 0, j)),
        out_shape=jax.ShapeDtypeStruct((depth, B, N), jnp.float32),
        name="ada_modulation",
    )(c, w, b)


def _ffn_kernel(x_ref, g_ref, shift_ref, scale_ref, gate_ref, w1_ref, w3_ref, w2_ref, *rest, n_mix):
    o_ref, act_ref = rest[-2:]
    x = x_ref[...]
    if n_mix:
        y_refs, mix_gate_ref, wo_ref = rest[:n_mix], rest[n_mix], rest[n_mix + 1]
        z = None
        for k, y_ref in enumerate(y_refs):
            part = jnp.dot(y_ref[...].astype(jnp.bfloat16), wo_ref[k * GROUP_WIDTH:(k + 1) * GROUP_WIDTH, :],
                           preferred_element_type=jnp.float32)
            z = part if z is None else z + part
        x = x + mix_gate_ref[0] * z
    h = _modulated_norm(x, g_ref[...], scale_ref[0], shift_ref[0]).astype(jnp.bfloat16)
    n_chunks = act_ref.shape[1] // FFN_CHUNK
    for ci in range(n_chunks):
        cs = slice(ci * FFN_CHUNK, (ci + 1) * FFN_CHUNK)
        a = jnp.dot(h, w1_ref[:, cs], preferred_element_type=jnp.float32)
        b = jnp.dot(h, w3_ref[:, cs], preferred_element_type=jnp.float32)
        act_ref[:, cs] = (a * jax.nn.sigmoid(a) * b).astype(jnp.bfloat16)
    y = jnp.dot(act_ref[...], w2_ref[...], preferred_element_type=jnp.float32)
    o_ref[...] = x + (0.5 * gate_ref[0]) * y


def ffn_half_step(x2, g, shift, scale, gate, w13, w2, layer, tokens_per_batch, mix=None):
    M, D = x2.shape
    F = w2.shape[1]
    tm = FFN_TOKEN_TILE
    tiles_per_batch = tokens_per_batch // tm
    resident = dict(pipeline_mode=pl.Buffered(1))
    mod_spec = pl.BlockSpec((1, 1, D), lambda i: (i // tiles_per_batch, 0, 0))
    ys, mix_gate, w_out = mix if mix is not None else ((), None, None)
    mix_args = tuple(ys) + ((mix_gate, w_out) if ys else ())
    mix_specs = [pl.BlockSpec((tm, GROUP_WIDTH), lambda i: (i, 0)) for _ in ys]
    if ys:
        mix_specs += [mod_spec, pl.BlockSpec((None,) + w_out.shape[1:], lambda i: (layer, 0, 0), **resident)]
    return pl.pallas_call(
        functools.partial(_ffn_kernel, n_mix=len(ys)),
        grid=(M // tm,),
        in_specs=[pl.BlockSpec((tm, D), lambda i: (i, 0)),
                  pl.BlockSpec((1, D), lambda i: (0, 0)),
                  mod_spec, mod_spec, mod_spec,
                  pl.BlockSpec((None, D, F), lambda i: (layer, 0, 0), **resident),
                  pl.BlockSpec((None, D, F), lambda i: (layer, 0, 1), **resident),
                  pl.BlockSpec((None, F, D), lambda i: (layer, 0, 0), **resident)] + mix_specs,
        out_specs=pl.BlockSpec((tm, D), lambda i: (i, 0)),
        out_shape=jax.ShapeDtypeStruct((M, D), jnp.float32),
        scratch_shapes=[pltpu.VMEM((tm, F), jnp.bfloat16)],
        compiler_params=pltpu.CompilerParams(dimension_semantics=("arbitrary",),
                                             vmem_limit_bytes=VMEM_LIMIT_BYTES),
        name="ffn_half_step",
    )(x2, g, shift, scale, gate, w13, w13, w2, *mix_args)


CONV_CARRY_ROWS = 8
CONV_ROW_CHUNK = 128


def _in_proj_kernel(x_ref, g_ref, shift_ref, scale_ref, wa_ref, wb_ref, wc_ref, wd_ref,
                    ba_ref, bb_ref, bc_ref, bd_ref, scw_ref, scg_ref, gcw_ref,
                    oa_ref, yb_ref, oc_ref, od_ref, braw_ref, craw_ref, *, tiles_per_batch):
    f32 = jnp.float32
    rows = x_ref.shape[0]
    hist = CONV_CARRY_ROWS
    first = pl.program_id(0) % tiles_per_batch == 0

    @pl.when(first)
    def _():
        braw_ref[0:hist, :] = jnp.zeros((hist, braw_ref.shape[1]), f32)
        craw_ref[0:hist, :] = jnp.zeros((hist, craw_ref.shape[1]), f32)

    @pl.when(jnp.logical_not(first))
    def _():
        braw_ref[0:hist, :] = braw_ref[rows:rows + hist, :]
        craw_ref[0:hist, :] = craw_ref[rows:rows + hist, :]

    h = _modulated_norm(x_ref[...], g_ref[...], scale_ref[0], shift_ref[0]).astype(jnp.bfloat16)
    project = lambda w_ref, b_ref: jnp.dot(h, w_ref[...], preferred_element_type=f32) + b_ref[...]
    pb = project(wb_ref, bb_ref)
    yb_ref[...] = pb[:, 0:256]
    braw_ref[hist:hist + rows, :] = pb[:, 256:512] * pb[:, 512:768]
    pc = project(wc_ref, bc_ref)
    craw_ref[hist:hist + rows, :] = pc[:, 0:768]
    oc_ref[:, 768:] = pc[:, 768:]
    n_chunks = rows // CONV_ROW_CHUNK
    pieces = []
    for w_ref, b_ref, o_ref in ((wa_ref, ba_ref, oa_ref), (wd_ref, bd_ref, od_ref)):
        half = w_ref.shape[1] // 256 // 2 * 256
        pieces += [(w_ref, b_ref, o_ref, 0, half), (w_ref, b_ref, o_ref, half, w_ref.shape[1])]
    for ci, r0 in enumerate(range(0, rows, CONV_ROW_CHUNK)):
        for w_ref, b_ref, o_ref, c0, c1 in pieces[ci * len(pieces) // n_chunks:(ci + 1) * len(pieces) // n_chunks]:
            o_ref[:, c0:c1] = jnp.dot(h, w_ref[:, c0:c1], preferred_element_type=f32) + b_ref[:, c0:c1]
        out_rows = slice(r0, r0 + CONV_ROW_CHUNK)
        tap = lambda ref, s: ref[pl.ds(hist + r0 - s, CONV_ROW_CHUNK), :]
        conv = sum(tap(braw_ref, s) * scw_ref[2 - s:3 - s, :] for s in range(3))
        y = yb_ref[out_rows, :] * conv
        yb_ref[out_rows, :] = y * lax.rsqrt(jnp.mean(y * y, axis=-1, keepdims=True) + EPS) * scg_ref[...]
        for c0 in range(0, 3 * GROUP_WIDTH, GROUP_WIDTH):
            cols = slice(c0, c0 + GROUP_WIDTH)
            conv = sum(craw_ref[pl.ds(hist + r0 - s, CONV_ROW_CHUNK), cols] * gcw_ref[3 - s:4 - s, cols]
                       for s in range(4))
            oc_ref[out_rows, cols] = conv * jax.nn.sigmoid(conv)


def in_projection(x2, g, shift, scale, ws, bs, layer, sc_conv_w, sc_norm_g, gdn_conv_w, tokens_per_batch):
    M, D = x2.shape
    tm = PROJ_TOKEN_TILE
    tiles_per_batch = tokens_per_batch // tm
    mod_spec = pl.BlockSpec((1, 1, D), lambda i: (i // tiles_per_batch, 0, 0))
    const = lambda a, **kw: pl.BlockSpec(a.shape, lambda i: (0, 0), **kw)
    pad8 = lambda w: jnp.pad(w, ((0, 8 - w.shape[0]), (0, 0)))
    extras = (pad8(sc_conv_w), sc_norm_g, pad8(gdn_conv_w))
    widths = (ws[0].shape[2], GROUP_WIDTH, ws[2].shape[2], ws[3].shape[2])
    of_layer = lambda a, **kw: pl.BlockSpec((None,) + a.shape[1:], lambda i: (layer, 0, 0), **kw)
    return pl.pallas_call(
        functools.partial(_in_proj_kernel, tiles_per_batch=tiles_per_batch),
        grid=(M // tm,),
        in_specs=[pl.BlockSpec((tm, D), lambda i: (i, 0)), const(g), mod_spec, mod_spec]
                 + [of_layer(w, pipeline_mode=pl.Buffered(1)) for w in ws] + [of_layer(b) for b in bs]
                 + [const(e) for e in extras],
        out_specs=[pl.BlockSpec((tm, n), lambda i: (i, 0)) for n in widths],
        out_shape=[jax.ShapeDtypeStruct((M, n), jnp.float32) for n in widths],
        scratch_shapes=[pltpu.VMEM((CONV_CARRY_ROWS + tm, GROUP_WIDTH), jnp.float32),
                        pltpu.VMEM((CONV_CARRY_ROWS + tm, 3 * GROUP_WIDTH), jnp.float32)],
        compiler_params=pltpu.CompilerParams(dimension_semantics=("arbitrary",),
                                             vmem_limit_bytes=VMEM_LIMIT_BYTES),
        name="in_projection",
    )(x2, g, shift, scale, *ws, *bs, *extras)


def _group_columns():
    offs, o = {}, 0
    for n, s in IN_LAYOUT:
        offs[n] = np.arange(o, o + s)
        o += s
    cat = lambda names: np.concatenate([offs[n] for n in names])
    return (cat(["a_q", "a_k_cmp", "a_v_cmp", "a_k_slc", "a_v_slc", "a_k_win", "a_v_win"] + list(SMALL_GATES)),
            cat(["b_b", "b_c", "b_x"]),
            cat(["c_q", "c_k", "c_v", "c_z"]),
            cat(["d_q", "d_k", "d_v", "d_o"]))


SMALL_GATES = ("a_gate", "c_beta", "c_alpha", "d_i", "d_f")
GATE_BLOCK = 5
GATE_LANE = {"a_gate": 0, "c_beta": 12, "c_alpha": 16, "d_i": 20, "d_f": 24}
GROUP_SLAB_WIDTH = (768, 768, 1024, 1024)


def _take_columns(a, idx, width):
    cuts = [0] + [k + 1 for k in range(idx.size - 1) if idx[k + 1] != idx[k] + 1] + [idx.size]
    runs = [a[..., int(idx[s]):int(idx[e - 1]) + 1] for s, e in zip(cuts[:-1], cuts[1:])]
    return jnp.concatenate(runs + [jnp.zeros(a.shape[:-1] + (width - idx.size,), a.dtype)], axis=-1)


def split_in_weights(w, b):
    ws, bs = [], []
    for idx, width in zip(_group_columns(), GROUP_SLAB_WIDTH):
        ws.append(_take_columns(w, idx, width).astype(jnp.bfloat16))
        bs.append(_take_columns(b, idx, width)[:, None, :])
    return ws, bs


def t5_bucket(dist):
    n = jnp.maximum(dist, 0)
    max_exact = N_BUCKETS // 2
    nf = jnp.maximum(n, 1).astype(jnp.float32)
    large = max_exact + (jnp.log(nf / max_exact) / math.log(MAX_DISTANCE / max_exact)
                         * (N_BUCKETS - max_exact)).astype(jnp.int32)
    large = jnp.minimum(large, N_BUCKETS - 1)
    return jnp.where(n < max_exact, n, large)


def _nt_dot(a, b):
    return lax.dot_general(a, b, (((1,), (1,)), ((), ())), preferred_element_type=jnp.float32)


def _nsa_prep_kernel(q_ref, kvs_ref, kvw_ref, qg_ref, kg_ref, seg_ref,
                     qnt_ref, kxs_ref, v1s_ref, kkw_ref, v1w_ref):
    tm = q_ref.shape[0]
    q = q_ref[...]
    ss = _dot_sel(q * q, seg_ref[...], parts=3)
    qn = q * lax.rsqrt(ss * (1.0 / HEAD_DIM) + EPS) * qg_ref[...]
    qnt_ref[...] = (qn * LOG2E).T.astype(qnt_ref.dtype)
    lane = lax.broadcasted_iota(jnp.int32, (tm, 128), 1)
    lo = lane < HEAD_DIM

    def split(x):
        xr = pltpu.roll(x, HEAD_DIM, 1)
        ss = jnp.sum(jnp.where(lo, x * x, 0.0), axis=-1, keepdims=True)
        kk = jnp.where(lo, x, xr) * lax.rsqrt(ss * (1.0 / HEAD_DIM) + EPS) * kg_ref[...]
        return kk, jnp.where(lo, xr, 1.0)

    kk, v1 = split(kvs_ref[...])
    tok = pl.program_id(1) * tm + lax.broadcasted_iota(jnp.int32, (tm, 128), 0)
    onehot = jnp.where(lane == tok // SLC_BLOCK, 1.0, 0.0)
    kxs_ref[...] = jnp.concatenate([kk, onehot], axis=1).astype(kxs_ref.dtype)
    v1s_ref[...] = v1.T[0:V1_ROWS].astype(v1s_ref.dtype)
    kk, v1 = split(kvw_ref[...])
    kkw_ref[...] = kk.astype(kkw_ref.dtype)
    v1w_ref[...] = v1.T[0:V1_ROWS].astype(v1w_ref.dtype)


def nsa_prep(pa, qg, kg2, seg):
    B, T, _ = pa.shape
    tm = 512
    bf16 = jnp.bfloat16
    col = lambda w, j: pl.BlockSpec((None, tm, w), lambda b, i: (b, i, j))
    rowblk = lambda h: pl.BlockSpec((None, h, tm), lambda b, i: (b, 0, i))
    const = lambda shape: pl.BlockSpec(shape, lambda b, i: (0, 0))
    return pl.pallas_call(
        _nsa_prep_kernel,
        grid=(B, T // tm),
        in_specs=[col(256, 0), col(128, 3), col(128, 4), const((1, 256)), const((1, 128)), const((256, 256))],
        out_specs=[rowblk(256), col(256, 0), rowblk(V1_ROWS), col(128, 0), rowblk(V1_ROWS)],
        out_shape=[jax.ShapeDtypeStruct((B, 256, T), bf16),
                   jax.ShapeDtypeStruct((B, T, 256), bf16), jax.ShapeDtypeStruct((B, V1_ROWS, T), bf16),
                   jax.ShapeDtypeStruct((B, T, 128), bf16), jax.ShapeDtypeStruct((B, V1_ROWS, T), bf16)],
        name="nsa_prep",
    )(pa, pa, pa, qg, kg2, seg)


def _compress_kernel(kv_ref, pos_ref, w1_ref, w2k_ref, w2v_ref, kg_ref, kkc_ref, vct_ref):
    nrow = kkc_ref.shape[0]
    f32, bf16 = jnp.float32, jnp.bfloat16
    hidden = w1_ref.shape[2] // 2
    first = jnp.zeros((nrow, 2 * hidden), f32)
    second = jnp.zeros((nrow, 2 * hidden), f32)
    for p in range(CMP_STRIDE):
        xp = kv_ref[pl.ds(p, nrow, stride=CMP_STRIDE), :]
        first = first + jnp.dot((xp + pos_ref[p:p + 1, :]).astype(bf16), w1_ref[p], preferred_element_type=f32)
        second = second + jnp.dot((xp + pos_ref[CMP_STRIDE + p:CMP_STRIDE + p + 1, :]).astype(bf16),
                                  w1_ref[CMP_STRIDE + p], preferred_element_type=f32)
    hid = first + pltpu.roll(second, nrow - 1, 0)
    hid = (hid * jax.nn.sigmoid(hid)).astype(bf16)
    valid = lax.broadcasted_iota(jnp.int32, (nrow, 128), 0) < nrow - 1
    kc = jnp.dot(hid[:, 0:hidden], w2k_ref[...], preferred_element_type=f32)
    kc = kc * lax.rsqrt(jnp.mean(kc * kc, axis=-1, keepdims=True) + EPS) * kg_ref[...]
    kkc_ref[...] = jnp.where(valid, kc, 0.0).astype(kkc_ref.dtype)
    vc = jnp.dot(hid[:, hidden:2 * hidden], w2v_ref[...], preferred_element_type=f32)
    vct_ref[...] = jnp.where(valid, vc, 0.0).T[0:HEAD_DIM].astype(vct_ref.dtype)


def nsa_compress(pa, pos2, w1, w2k, w2v, kg2):
    B, T, _ = pa.shape
    nrow = T // CMP_STRIDE
    bf16 = jnp.bfloat16
    full = lambda a: pl.BlockSpec(a.shape, lambda b: (0,) * a.ndim)
    return pl.pallas_call(
        _compress_kernel,
        grid=(B,),
        in_specs=[pl.BlockSpec((None, T, 128), lambda b: (b, 0, 2)), full(pos2), full(w1), full(w2k), full(w2v),
                  full(kg2)],
        out_specs=[pl.BlockSpec((None, nrow, 128), lambda b: (b, 0, 0)),
                   pl.BlockSpec((None, HEAD_DIM, nrow), lambda b: (b, 0, 0))],
        out_shape=[jax.ShapeDtypeStruct((B, nrow, 128), bf16), jax.ShapeDtypeStruct((B, HEAD_DIM, nrow), bf16)],
        name="nsa_compress",
    )(pa, pos2, w1, w2k, w2v, kg2)


CMP_TILES_PER_STEP = 4


def _stack_heads_t(qt):
    lo = lax.broadcasted_iota(jnp.int32, (128, qt.shape[1]), 0) < HEAD_DIM
    zero = jnp.zeros((), qt.dtype)
    return jnp.concatenate([jnp.where(lo if h % 2 == 0 else ~lo, qt[128 * (h // 2):128 * (h // 2) + 128], zero)
                            for h in range(GROUP_HEADS)], axis=1)


def _cmp_select_kernel(qnt_ref, kkc_ref, vct_ref, cb_ref, ov_ref, ocmp_ref, pen_ref, sc_ref, *, n_sel):
    i = pl.program_id(1)
    n_tiles = pen_ref.shape[0]
    tq = qnt_ref.shape[1] // n_tiles
    nblk, nrow = ov_ref.shape
    keys_per_tile = tq // CMP_STRIDE
    f32, bf16 = jnp.float32, jnp.bfloat16
    blk = lax.broadcasted_iota(jnp.int32, (nblk, tq), 0)
    scores = []
    for g in range(n_tiles):
        cols = slice(g * tq, (g + 1) * tq)
        q4t = _stack_heads_t(qnt_ref[:, cols])
        first_row = pl.multiple_of(nrow - keys_per_tile * (i * n_tiles + g), keys_per_tile)
        s = jnp.dot(kkc_ref[...], q4t, preferred_element_type=f32) + cb_ref[pl.ds(first_row, nrow), :]
        m = jnp.max(s, axis=0, keepdims=True)
        e = jnp.where(s > 0.5 * NEG, jnp.exp2(s - m), 0.0)
        p = e * (1.0 / jnp.maximum(jnp.sum(e, axis=0, keepdims=True), 1e-30))
        o = jnp.dot(vct_ref[...], p.astype(bf16), preferred_element_type=f32)
        for h in range(GROUP_HEADS):
            ocmp_ref[h * HEAD_DIM:(h + 1) * HEAD_DIM, cols] = o[:, h * tq:(h + 1) * tq]
        psum = p[:, 0:tq] + p[:, tq:2 * tq] + p[:, 2 * tq:3 * tq] + p[:, 3 * tq:4 * tq]
        hi = psum.astype(bf16)
        lo_part = (psum - hi.astype(f32)).astype(bf16)
        score = (jnp.dot(ov_ref[...], hi, preferred_element_type=f32)
                 + jnp.dot(ov_ref[...], lo_part, preferred_element_type=f32))
        qpos = (i * n_tiles + g) * tq + lax.broadcasted_iota(jnp.int32, (nblk, tq), 1)
        cur = qpos // SLC_BLOCK
        forced = (blk == 0) | ((cur - blk >= 0) & (cur - blk < N_LOCAL_SLC))
        score = jnp.where(forced, FORCE, score)
        score = jnp.where(blk <= cur, score, -FORCE)
        sc_ref[g] = score
        scores.append(score)
    blocks_per_tile = tq // SLC_BLOCK

    def rank_step(t, cnts):
        out = []
        for g in range(n_tiles):
            cnt = cnts[g]
            for u in range(blocks_per_tile):
                jp = t * blocks_per_tile + u
                row = sc_ref[g, pl.ds(jp, 1), :]
                tie = jnp.where(blk > jp, 1.0, 0.0)
                cnt = cnt + jnp.where(row > scores[g], 1.0, jnp.where(row == scores[g], tie, 0.0))
            out.append(cnt)
        return tuple(out)

    zeros = tuple(jnp.zeros((nblk, tq), f32) for _ in range(n_tiles))
    cnts = lax.fori_loop(0, (i + 1) * n_tiles, rank_step, zeros)
    for g in range(n_tiles):
        pen = jnp.where(cnts[g] < n_sel, 0.0, NEG)
        if nblk < 128:
            pen = jnp.concatenate([pen, jnp.zeros((128 - nblk, tq), f32)], axis=0)
        pen_ref[g] = pen.astype(pen_ref.dtype)


def nsa_cmp_select(qnt, kkc, vct, cbias, ov, n_sel):
    B, _, T = qnt.shape
    nrow = kkc.shape[1]
    nblk = ov.shape[0]
    tq = Q_BLOCK
    g = min(CMP_TILES_PER_STEP, T // tq)
    return pl.pallas_call(
        functools.partial(_cmp_select_kernel, n_sel=n_sel),
        grid=(B, T // (g * tq)),
        in_specs=[pl.BlockSpec((None, 256, g * tq), lambda b, i: (b, 0, i)),
                  pl.BlockSpec((None, nrow, 128), lambda b, i: (b, 0, 0)),
                  pl.BlockSpec((None, HEAD_DIM, nrow), lambda b, i: (b, 0, 0)),
                  pl.BlockSpec((2 * nrow, 4 * tq), lambda b, i: (0, 0)),
                  pl.BlockSpec((nblk, nrow), lambda b, i: (0, 0))],
        out_specs=[pl.BlockSpec((None, 256, g * tq), lambda b, i: (b, 0, i)),
                   pl.BlockSpec((None, g, 128, tq), lambda b, i: (b, i, 0, 0))],
        out_shape=[jax.ShapeDtypeStruct((B, 256, T), jnp.float32),
                   jax.ShapeDtypeStruct((B, T // tq, 128, tq), jnp.bfloat16)],
        scratch_shapes=[pltpu.VMEM((g, nblk, tq), jnp.float32)],
        name="nsa_cmp_select",
    )(qnt, kkc, vct, cbias, ov)


NSA_KEY_TILE = 256
NSA_QUERY_TILE = 256


def _nsa_attn_kernel(qnt_ref, pen_ref, ocmp_ref, gate_ref, kxs_ref, v1s_ref, kkw_ref, v1w_ref,
                     tabs_ref, tabw_ref, mixg_ref, o_ref, acc_ref, m_ref, sa_ref, sb_ref):
    i = pl.program_id(1)
    tq = o_ref.shape[0]
    kt = tabs_ref.shape[1]
    f32, bf16 = jnp.float32, jnp.bfloat16
    q4t = _stack_heads_t(qnt_ref[...])
    pen = jnp.concatenate([pen_ref[t] for t in range(pen_ref.shape[0])], axis=1)
    rhs_sel = jnp.concatenate([q4t, jnp.concatenate([pen] * GROUP_HEADS, axis=1)], axis=0)

    def branch(rhs, k_ref, v_ref, tab_ref, first_tile, last_tile):
        n_tab = tab_ref.shape[0] - 1
        m_ref[...] = jnp.full(m_ref.shape, NEG, f32)
        acc_ref[...] = jnp.zeros(acc_ref.shape, f32)

        def key_offset(j):
            return pl.multiple_of(jnp.minimum(j, last_tile) * kt, kt)

        def scores(j):
            delta = jnp.where(j > last_tile, n_tab, jnp.minimum(i - j * (kt // tq), n_tab - 1))
            return jnp.dot(k_ref[pl.ds(key_offset(j), kt), :], rhs, preferred_element_type=f32) + tab_ref[delta]

        def absorb(s, j):
            m_old = m_ref[...]
            m_new = jnp.maximum(m_old, jnp.max(s, axis=0, keepdims=True))
            p = jnp.exp2(s - m_new)
            pv = jnp.dot(v_ref[:, pl.ds(key_offset(j), kt)], p.astype(bf16), preferred_element_type=f32)
            acc_ref[...] = jnp.exp2(m_old - m_new) * acc_ref[...] + pv
            m_ref[...] = m_new

        sa_ref[...] = scores(first_tile)

        def body(t, carry):
            j = first_tile + 2 * t
            sb_ref[...] = scores(j + 1)
            absorb(sa_ref[...], j)
            sa_ref[...] = scores(j + 2)
            absorb(sb_ref[...], j + 1)
            return carry

        lax.fori_loop(0, (last_tile - first_tile) // 2 + 1, body, 0)
        acc = acc_ref[...]
        return acc[0:HEAD_DIM] / jnp.maximum(acc[HEAD_DIM:HEAD_DIM + 1], 1e-30)

    def window_branch(rhs):
        span = tabw_ref.shape[1]
        back = span // tq - 1
        start = pl.multiple_of(jnp.maximum(i - back, 0) * tq, tq)
        s = jnp.dot(kkw_ref[pl.ds(start, span), :], rhs, preferred_element_type=f32) + tabw_ref[jnp.minimum(i, back)]
        p = jnp.exp2(s - jnp.max(s, axis=0, keepdims=True))
        acc = jnp.dot(v1w_ref[:, pl.ds(start, span)], p.astype(bf16), preferred_element_type=f32)
        return acc[0:HEAD_DIM] / jnp.maximum(acc[HEAD_DIM:HEAD_DIM + 1], 1e-30)

    last = (i * tq) // kt
    o_w = window_branch(q4t)
    o_s = branch(rhs_sel, kxs_ref, v1s_ref, tabs_ref, 0, last)
    gt = jax.nn.sigmoid(gate_ref[...]).T
    oct_ = ocmp_ref[...]
    parts = []
    for h in range(GROUP_HEADS):
        cols = slice(h * tq, (h + 1) * tq)
        parts.append(gt[3 * h:3 * h + 1] * oct_[h * HEAD_DIM:(h + 1) * HEAD_DIM]
                     + gt[3 * h + 1:3 * h + 2] * o_s[:, cols] + gt[3 * h + 2:3 * h + 3] * o_w[:, cols])
    yt = jnp.concatenate(parts, axis=0)
    yt = yt * lax.rsqrt(jnp.mean(yt * yt, axis=0, keepdims=True) + EPS) * mixg_ref[...]
    o_ref[...] = yt.T


def nsa_attention(qnt, pen, ocmp, pa, kxs, v1s, kkw, v1w, tab_s, tab_w, mixg):
    B, _, T = qnt.shape
    tq = min(NSA_QUERY_TILE, T)
    tile = lambda w, j: pl.BlockSpec((None, tq, w), lambda b, i: (b, i, j))
    whole = lambda w: pl.BlockSpec((None, T, w), lambda b, i: (b, 0, 0))
    whole_t = pl.BlockSpec((None, V1_ROWS, T), lambda b, i: (b, 0, 0))
    full = lambda a: pl.BlockSpec(a.shape, lambda b, i: (0,) * a.ndim, pipeline_mode=pl.Buffered(1))
    return pl.pallas_call(
        _nsa_attn_kernel,
        grid=(B, T // tq),
        in_specs=[pl.BlockSpec((None, 256, tq), lambda b, i: (b, 0, i)),
                  pl.BlockSpec((None, tq // Q_BLOCK, 128, Q_BLOCK), lambda b, i: (b, i, 0, 0)),
                  pl.BlockSpec((None, 256, tq), lambda b, i: (b, 0, i)), tile(128, 5),
                  whole(256), whole_t, whole(128), whole_t, full(tab_s), full(tab_w), full(mixg)],
        out_specs=tile(256, 0),
        out_shape=jax.ShapeDtypeStruct((B, T, 256), jnp.float32),
        scratch_shapes=[pltpu.VMEM((V1_ROWS, GROUP_HEADS * tq), jnp.float32),
                        pltpu.VMEM((1, GROUP_HEADS * tq), jnp.float32),
                        pltpu.VMEM((tab_s.shape[1], GROUP_HEADS * tq), jnp.float32),
                        pltpu.VMEM((tab_s.shape[1], GROUP_HEADS * tq), jnp.float32)],
        compiler_params=pltpu.CompilerParams(dimension_semantics=("arbitrary", "arbitrary"),
                                             vmem_limit_bytes=VMEM_LIMIT_BYTES),
        name="nsa_attention",
    )(qnt, pen, ocmp, pa, kxs, v1s, kkw, v1w, tab_s, tab_w, mixg)


def _toeplitz(wr, rows, cols):
    H, n = wr.shape
    flat = jnp.tile(jnp.pad(wr, ((0, 0), (0, 1))), (1, rows))[:, :rows * n]
    return flat.reshape(H, rows, n)[:, :, rows - 1:rows - 1 + cols]


def nsa_bias_tables(t5_table, T):
    H = GROUP_HEADS
    tq = Q_BLOCK
    ta = min(NSA_QUERY_TILE, T)
    kt = min(NSA_KEY_TILE, T)
    nrow = T // CMP_STRIDE
    dmax = WINDOW + 2 * ta + kt
    onehot = (t5_bucket(jnp.arange(dmax))[:, None] == jnp.arange(N_BUCKETS)[None, :]).astype(jnp.float32)
    by_dist = jnp.dot(onehot, t5_table, precision=HIGHEST).T
    neg = lambda n: jnp.full((H, n), NEG, jnp.float32)
    f_s = jnp.concatenate([neg(kt - 1), by_dist * LOG2E], axis=1)
    span = ta + kt - 1

    def tiles(f, n):
        out = [_toeplitz(f[:, k * ta:k * ta + span], kt, ta).transpose(1, 0, 2).reshape(kt, H * ta) for k in range(n)]
        return jnp.stack(out + [jnp.full((kt, H * ta), NEG, jnp.float32)])

    n_s = -(-(kt + MAX_DISTANCE) // ta) + 1
    tab_s = tiles(f_s, n_s)
    n_w = WINDOW // ta + 1
    f_w = jnp.concatenate([neg(kt - 1), by_dist[:, :WINDOW] * LOG2E, neg(dmax - WINDOW)], axis=1)
    w_tiles = tiles(f_w, n_w)
    tab_w = jnp.stack([jnp.concatenate([w_tiles[k - m if k >= m else n_w] for m in range(n_w)], axis=0)
                       for k in range(n_w)])
    per_tile = tq // CMP_STRIDE
    d0 = -CMP_STRIDE * (nrow - 1) - (CMP_BLOCK - 1)
    n16 = 2 * nrow + per_tile - 1
    n_pos = CMP_STRIDE * n16 + d0
    far = jnp.broadcast_to(by_dist[:, -1:], (H, max(n_pos - dmax, 0)))
    w16 = jnp.concatenate([neg(-d0), by_dist[:, :n_pos] * LOG2E, far * LOG2E], axis=1).reshape(H, n16, CMP_STRIDE)
    w16 = w16[:, ::-1, :]
    base = jnp.stack([w16[:, per_tile - 1 - a:per_tile - 1 - a + 2 * nrow, :] for a in range(per_tile)], axis=1)
    cbias = base.transpose(0, 1, 3, 2).reshape(H * tq, 2 * nrow).T
    return tab_s, tab_w, cbias


def nsa_overlap_t(T):
    nrow = T // CMP_STRIDE
    nblk = T // SLC_BLOCK
    ci = np.arange(nrow)[None, :]
    bj = np.arange(nblk)[:, None]
    ov = (ci * CMP_STRIDE < (bj + 1) * SLC_BLOCK) & (ci * CMP_STRIDE + CMP_BLOCK > bj * SLC_BLOCK)
    ov = ov & (ci < nrow - 1)
    return jnp.asarray(ov.astype(np.float32), dtype=jnp.bfloat16)


def nsa_mixer_pallas(pa, q_norm_g, k_norm_g, cmp_pos, cmp_k_w1, cmp_k_w2, cmp_v_w1, cmp_v_w2,
                     tables, ovt, mixg):
    B, T, _ = pa.shape
    bf16 = jnp.bfloat16
    tab_s, tab_w, cbias = tables
    qg = (jnp.tile(q_norm_g, GROUP_HEADS) * HEAD_DIM ** -0.5)[None, :]
    kg2 = jnp.tile(k_norm_g, 2)[None, :]
    qnt, kxs, v1s, kkw, v1w = nsa_prep(pa, qg, kg2, _head_mask().astype(bf16))
    dup = lambda w: jnp.concatenate([w, w], axis=1)
    zeros = jnp.zeros((CMP_BLOCK, HEAD_DIM, cmp_k_w1.shape[1]), jnp.float32)
    per_pos = lambda w: w.reshape(CMP_BLOCK, HEAD_DIM, -1)
    w1 = jnp.concatenate([jnp.concatenate([per_pos(cmp_k_w1), zeros], axis=2),
                          jnp.concatenate([zeros, per_pos(cmp_v_w1)], axis=2)], axis=1).astype(bf16)
    kkc, vct = nsa_compress(pa, dup(cmp_pos), w1, dup(cmp_k_w2).astype(bf16), dup(cmp_v_w2).astype(bf16), kg2)
    n_sel = min(N_SLC, T // SLC_BLOCK)
    ocmp, pen = nsa_cmp_select(qnt, kkc, vct, cbias, ovt, n_sel)
    mixg_col = jnp.broadcast_to(mixg.reshape(-1, 1), (GROUP_WIDTH, min(NSA_QUERY_TILE, T)))
    return nsa_attention(qnt, pen, ocmp, pa, kxs, v1s, kkw, v1w, tab_s, tab_w, mixg_col)


CHUNKS_PER_STEP = 4
SEQS_PER_STEP = 2


def _tn_dot(a, b):
    return lax.dot_general(a, b, (((0,), (0,)), ((), ())), preferred_element_type=jnp.float32)


def _split_bf16(a, parts):
    out, rest = [], a
    for _ in range(parts):
        piece = rest.astype(jnp.bfloat16)
        out.append(piece)
        rest = rest - piece.astype(jnp.float32)
    return out


def _dot_sel(a, sel, parts=3):
    return sum(jnp.dot(p, sel, preferred_element_type=jnp.float32) for p in _split_bf16(a, parts))


def _sel_dot(sel, a, parts=3):
    return sum(jnp.dot(sel, p, preferred_element_type=jnp.float32) for p in _split_bf16(a, parts))


def _softplus(x):
    return jnp.maximum(x, 0.0) + jnp.log(1.0 + jnp.exp(-jnp.abs(x)))


def _block_diag(a, hmb):
    return jnp.concatenate([a.astype(jnp.bfloat16)] * GROUP_HEADS, axis=0) * hmb


def _diag_row(a, eye):
    return jnp.sum(a * eye, axis=0, keepdims=True)


def _head_norm(o, hmb):
    return o * lax.rsqrt(_dot_sel(o * o, hmb, parts=2) * (1.0 / HEAD_DIM) + EPS)


def _head_l2(o, hmb):
    return o * lax.rsqrt(_dot_sel(o * o, hmb, parts=2) + EPS)


def _xform_masks(L):
    i = np.arange(L)[:, None]
    j = np.arange(L)[None, :]
    masks = []
    s = 1
    while s < L:
        masks.append((i // (2 * s) == j // (2 * s)) & (i % (2 * s) >= s) & (j % (2 * s) < s))
        s *= 2
    masks += [i == j, j <= i, j < i]
    return jnp.asarray(np.tile(np.stack(masks).astype(np.float32), (1, 1, GROUP_HEADS)))


def _expand_matrix(first_lane):
    e = np.zeros((128, GROUP_WIDTH), np.float32)
    for h in range(GROUP_HEADS):
        e[first_lane + h, h * HEAD_DIM:(h + 1) * HEAD_DIM] = 1.0
    return jnp.asarray(e, jnp.bfloat16)


def _head_mask():
    return jnp.asarray(np.kron(np.eye(GROUP_HEADS), np.ones((HEAD_DIM, HEAD_DIM))), jnp.float32)


def _per_head_rows(*vecs):
    rows = [jnp.repeat(v, HEAD_DIM) if v.shape[0] == GROUP_HEADS else jnp.tile(v, GROUP_HEADS) for v in vecs]
    return jnp.pad(jnp.stack(rows), ((0, 8 - len(rows)), (0, 0)))


def _chunk_consts(L, n_chunks):
    tri = np.kron(np.eye(n_chunks), np.tril(np.ones((L, L)))).astype(np.float32)
    return _head_mask(), jnp.asarray(tri, jnp.bfloat16), _xform_masks(L)


def _const_spec(a):
    return pl.BlockSpec(a.shape, lambda b, t: (0,) * a.ndim)


def _gdn_kernel(pc_ref, gates_ref, hp_ref, eb_ref, ea_ref, hm_ref, tri_ref, lv_ref, o_ref, s_ref):
    L = GDN_CHUNK
    n_seq, rows = pc_ref.shape[0], pc_ref.shape[1]
    f32, bf16 = jnp.float32, jnp.bfloat16
    n_lev = lv_ref.shape[0] - 3

    @pl.when(pl.program_id(1) == 0)
    def _():
        s_ref[...] = jnp.zeros(s_ref.shape, f32)

    hm = hm_ref[...]
    hmb = hm.astype(bf16)
    eye, incl, strict = lv_ref[n_lev], lv_ref[n_lev + 1], lv_ref[n_lev + 2]

    q_c, k_c, v_c, beta_c, gc_c = [], [], [], [], []
    for b in range(n_seq):
        qkv = pc_ref[b, :, 0:768]
        q_all = _head_l2(qkv[:, 0:256], hmb) * HEAD_DIM ** -0.5
        k_all = _head_l2(qkv[:, 256:512], hmb)
        tail = gates_ref[b]
        beta_all = jax.nn.sigmoid(_dot_sel(tail, eb_ref[...]))
        g = hp_ref[0:1, :] * _softplus(_dot_sel(tail, ea_ref[...]) + hp_ref[1:2, :])
        gc_all = _sel_dot(tri_ref[...], g)
        for c in range(rows // L):
            sl = slice(c * L, (c + 1) * L)
            q_c.append(q_all[sl])
            k_c.append(k_all[sl])
            v_c.append(qkv[sl, 512:768])
            beta_c.append(beta_all[sl])
            gc_c.append(gc_all[sl])
    chains = range(len(q_c))
    egc = [jnp.exp(gc) for gc in gc_c]
    kb = [k_c[i] * beta_c[i] for i in chains]
    a_mat, qk = [], []
    for i in chains:
        seg = jnp.exp(jnp.where(incl > 0.5, gc_c[i] - _diag_row(gc_c[i], eye), NEG))
        k_bd = _block_diag(k_c[i], hmb)
        a_mat.append(_nt_dot(kb[i].astype(bf16), k_bd) * seg * strict)
        qk.append((_nt_dot(q_c[i].astype(bf16), k_bd) * seg).astype(bf16))
    t_inv = [eye - a * lv_ref[0] for a in a_mat]
    for lev in range(1, n_lev):
        te = [jnp.dot(t_inv[i].astype(bf16), _block_diag(a_mat[i] * lv_ref[lev], hmb),
                      preferred_element_type=f32) for i in chains]
        t_inv = [t_inv[i] - jnp.dot(te[i].astype(bf16), _block_diag(t_inv[i], hmb), preferred_element_type=f32)
                 for i in chains]
    parts = []
    for i in chains:
        tb = t_inv[i].astype(bf16)
        u = jnp.dot(tb, _block_diag(v_c[i] * beta_c[i], hmb), preferred_element_type=f32)
        w = jnp.dot(tb, _block_diag(kb[i] * egc[i], hmb), preferred_element_type=f32)
        k_dec = k_c[i] * jnp.exp(gc_c[i][L - 1:L, :] - gc_c[i])
        parts.append((u, w.astype(bf16), (q_c[i] * egc[i]).astype(bf16), qk[i], k_dec.astype(bf16),
                      egc[i][L - 1:L, :]))
    per_seq = rows // L
    state = [s_ref[b] for b in range(n_seq)]
    outs = [[] for _ in range(n_seq)]
    for c in range(per_seq):
        for b in range(n_seq):
            u, w, q_dec, qk_c, k_dec, g_tot = parts[b * per_seq + c]
            sb = state[b].astype(bf16)
            v_new = u - jnp.dot(w, sb, preferred_element_type=f32)
            outs[b].append(jnp.dot(q_dec, sb, preferred_element_type=f32)
                           + jnp.dot(qk_c, _block_diag(v_new, hmb), preferred_element_type=f32))
            state[b] = state[b] * g_tot + _tn_dot(k_dec, v_new.astype(bf16)) * hm
    for b in range(n_seq):
        s_ref[b] = state[b]
        o = jnp.concatenate(outs[b], axis=0)
        z = pc_ref[b, :, 768:1024]
        o_ref[b] = _head_norm(o, hmb) * hp_ref[2:3, :] * (z * jax.nn.sigmoid(z))


def _mlstm_kernel(pd_ref, gates_ref, hp_ref, ei_ref, ef_ref, hm_ref, tri_ref, lv_ref, o_ref, c_ref, n_ref, m_ref):
    L = MLSTM_CHUNK
    n_seq, rows = pd_ref.shape[0], pd_ref.shape[1]
    f32, bf16 = jnp.float32, jnp.bfloat16
    n_lev = lv_ref.shape[0] - 3

    @pl.when(pl.program_id(1) == 0)
    def _():
        c_ref[...] = jnp.zeros(c_ref.shape, f32)
        n_ref[...] = jnp.zeros(n_ref.shape, f32)
        m_ref[...] = jnp.zeros(m_ref.shape, f32)

    hm = hm_ref[...]
    hmb = hm.astype(bf16)
    eye, incl = lv_ref[n_lev], lv_ref[n_lev + 1]
    head_of_lane = lax.broadcasted_iota(jnp.int32, (L, GROUP_WIDTH), 1) // HEAD_DIM

    def head_max(a):
        out = jnp.zeros(a.shape, f32)
        for h in range(GROUP_HEADS):
            mine = head_of_lane == h
            out = jnp.where(mine, jnp.max(jnp.where(mine, a, NEG), axis=-1, keepdims=True), out)
        return out

    per_seq = rows // L
    parts = []
    for b in range(n_seq):
        tail = gates_ref[b]
        log_i_all = _dot_sel(tail, ei_ref[...])
        log_f = -_softplus(-(_dot_sel(tail, ef_ref[...]) + hp_ref[0:1, :]))
        b_all = _sel_dot(tri_ref[...], log_f)
        for c in range(per_seq):
            sl = slice(c * L, (c + 1) * L)
            q = pd_ref[b, sl, 0:256]
            k = pd_ref[b, sl, 256:512] * HEAD_DIM ** -0.5
            v = pd_ref[b, sl, 512:768]
            bc, log_i = b_all[sl], log_i_all[sl]
            log_w = jnp.where(incl > 0.5, bc - _diag_row(bc - log_i, eye), NEG)
            b_last = bc[L - 1:L, :]
            log_w_end = b_last - bc + log_i
            qk = _nt_dot(q.astype(bf16), _block_diag(k, hmb))
            parts.append((q, k, v, bc, log_w, head_max(log_w), b_last, log_w_end,
                          jnp.max(log_w_end, axis=0, keepdims=True), qk))
    c_state = [c_ref[b] for b in range(n_seq)]
    n_state = [n_ref[b] for b in range(n_seq)]
    m_prev = [m_ref[b] for b in range(n_seq)]
    outs = [[] for _ in range(n_seq)]
    for c in range(per_seq):
        for b in range(n_seq):
            q, k, v, bc, log_w, m_intra, b_last, log_w_end, m_end, qk = parts[b * per_seq + c]
            log_inter = bc + m_prev[b]
            m_t = jnp.maximum(log_inter, m_intra)
            w_inter = jnp.exp(log_inter - m_t)
            s = qk * jnp.exp(log_w - m_t)
            num = w_inter * jnp.dot(q.astype(bf16), c_state[b].astype(bf16), preferred_element_type=f32)
            num = num + jnp.dot(s.astype(bf16), _block_diag(v, hmb), preferred_element_type=f32)
            den = w_inter * _dot_sel(q * n_state[b], hmb, parts=2) + _dot_sel(s, hmb, parts=2)
            outs[b].append(num / jnp.maximum(jnp.abs(den), jnp.exp(-m_t)))
            m_new = jnp.maximum(b_last + m_prev[b], m_end)
            w_old = jnp.exp(b_last + m_prev[b] - m_new)
            kw = k * jnp.exp(log_w_end - m_new)
            c_state[b] = w_old * c_state[b] + _tn_dot(kw.astype(bf16), v.astype(bf16)) * hm
            n_state[b] = w_old * n_state[b] + jnp.sum(kw, axis=0, keepdims=True)
            m_prev[b] = m_new
    for b in range(n_seq):
        c_ref[b] = c_state[b]
        n_ref[b] = n_state[b]
        m_ref[b] = m_prev[b]
        h = jnp.concatenate(outs[b], axis=0)
        o_ref[b] = _head_norm(h, hmb) * hp_ref[1:2, :] * jax.nn.sigmoid(pd_ref[b, :, 768:1024])


def _recurrent_mixers_kernel(pc_ref, pd_ref, gates_ref, hpc_ref, eb_ref, ea_ref, hpd_ref, ei_ref, ef_ref,
                             hm_ref, tri_ref, lv_ref, oc_ref, od_ref, s_ref, c_ref, n_ref, m_ref):
    _gdn_kernel(pc_ref, gates_ref, hpc_ref, eb_ref, ea_ref, hm_ref, tri_ref, lv_ref, oc_ref, s_ref)
    _mlstm_kernel(pd_ref, gates_ref, hpd_ref, ei_ref, ef_ref, hm_ref, tri_ref, lv_ref, od_ref, c_ref, n_ref, m_ref)


def recurrent_mixers_pallas(pc, pd, pa, A_log, dt_bias, gdn_norm_g, f_bias, mlstm_norm_g):
    B, T, width = pc.shape
    L = GDN_CHUNK
    assert MLSTM_CHUNK == L and pd.shape == pc.shape
    rows = L * min(CHUNKS_PER_STEP, T // L)
    n_seq = SEQS_PER_STEP if B % SEQS_PER_STEP == 0 else 1
    n = GROUP_HEADS * HEAD_DIM
    expand = lambda name: _expand_matrix(GATE_LANE[name])
    consts = ((_per_head_rows(-jnp.exp(A_log), dt_bias, gdn_norm_g), expand("c_beta"), expand("c_alpha"),
               _per_head_rows(f_bias, mlstm_norm_g), expand("d_i"), expand("d_f")) + _chunk_consts(L, rows // L))
    slab = pl.BlockSpec((n_seq, rows, width), lambda b, t: (b, t, 0))
    out = pl.BlockSpec((n_seq, rows, GROUP_WIDTH), lambda b, t: (b, t, 0))
    f32 = jnp.float32
    return pl.pallas_call(
        _recurrent_mixers_kernel,
        grid=(B // n_seq, T // rows),
        in_specs=[slab, slab, pl.BlockSpec((n_seq, rows, 128), lambda b, t: (b, t, GATE_BLOCK))]
                 + [_const_spec(a) for a in consts],
        out_specs=[out, out],
        out_shape=[jax.ShapeDtypeStruct((B, T, GROUP_WIDTH), f32), jax.ShapeDtypeStruct((B, T, GROUP_WIDTH), f32)],
        scratch_shapes=[pltpu.VMEM((n_seq, n, n), f32), pltpu.VMEM((n_seq, n, n), f32),
                        pltpu.VMEM((n_seq, 1, GROUP_WIDTH), f32), pltpu.VMEM((n_seq, 1, GROUP_WIDTH), f32)],
        compiler_params=pltpu.CompilerParams(dimension_semantics=("arbitrary", "arbitrary"),
                                             vmem_limit_bytes=VMEM_LIMIT_BYTES),
        name="recurrent_mixers",
    )(pc, pd, pa, *consts)


def kernel(x, c, ada_w, ada_b, norm_g, ffn1_w13, ffn1_w2, ffn2_w13, ffn2_w2, w_in, b_in, q_norm_g, k_norm_g, cmp_pos, cmp_k_w1, cmp_k_w2, cmp_v_w1, cmp_v_w2, t5_table, sc_conv_w, gdn_conv_w, gdn_A_log, gdn_dt_bias, gdn_norm_g, mlstm_f_bias, mlstm_norm_g, mix_norm_g, w_out):
    B, T, D = x.shape
    depth = ada_w.shape[0]
    bf16 = jnp.bfloat16
    x2 = x.reshape(B * T, D)
    tables = nsa_bias_tables(t5_table, T)
    ovt = nsa_overlap_t(T)
    ffn_w = [w.astype(bf16) for w in (ffn1_w13, ffn1_w2, ffn2_w13, ffn2_w2, w_out)]
    in_ws, in_bs = split_in_weights(w_in, b_in)
    mod_all = ada_modulation(c, ada_w, ada_b[:, None, :])
    for l in range(depth):
        mod = mod_all[l].reshape(B, N_SUBLAYERS, 3, 1, D)
        x2 = ffn_half_step(x2, norm_g[l, 0][None, :], mod[:, 0, 0], mod[:, 0, 1], mod[:, 0, 2],
                           ffn_w[0], ffn_w[1], l, T)
        pa, y_b, pc, pd = in_projection(x2, norm_g[l, 1][None, :], mod[:, 1, 0], mod[:, 1, 1], in_ws, in_bs, l,
                                        sc_conv_w[l], mix_norm_g[l, 1][None, :], gdn_conv_w[l], T)
        pa, pc, pd = (a.reshape(B, T, -1) for a in (pa, pc, pd))
        y_a = nsa_mixer_pallas(pa, q_norm_g[l], k_norm_g[l], cmp_pos[l], cmp_k_w1[l], cmp_k_w2[l],
                               cmp_v_w1[l], cmp_v_w2[l], tables, ovt, mix_norm_g[l, 0][None, :])
        y_c, y_d = recurrent_mixers_pallas(pc, pd, pa, gdn_A_log[l], gdn_dt_bias[l], gdn_norm_g[l],
                                           mlstm_f_bias[l], mlstm_norm_g[l])
        ys = [y.reshape(B * T, GROUP_WIDTH) for y in (y_a, y_b, y_c, y_d)]
        x2 = ffn_half_step(x2, norm_g[l, 2][None, :], mod[:, 2, 0], mod[:, 2, 1], mod[:, 2, 2],
                           ffn_w[2], ffn_w[3], l, T, mix=(ys, mod[:, 1, 2], ffn_w[4]))
    return x2.reshape(B, T, D)
```
